```python
import math
import jax, jax.numpy as jnp
from jax import lax
import numpy as np

D_MODEL = 1024
BATCH = 32
SEQ = 2048
DEPTH = 1

GDN_HEADS = 8
GDN_DK = 128
GDN_DV = 128
SSD_HEADS = 16
SSD_HEADDIM = 64
SSD_GROUPS = 2
SSD_STATE = 128
CONV_K = 4
CHUNK = 64
D_FF = 2816
FFN_CONV_K = 3
EPS = 1e-6

GDN_QK = GDN_HEADS * GDN_DK
GDN_V = GDN_HEADS * GDN_DV
SSD_D = SSD_HEADS * SSD_HEADDIM
SSD_BC = SSD_GROUPS * SSD_STATE
SSD_HPG = SSD_HEADS // SSD_GROUPS
MIX_WIDTH = GDN_V + SSD_D
GDN_CONV_CH = 2 * GDN_QK + GDN_V
SSD_CONV_CH = SSD_D + 2 * SSD_BC
IN_SPLITS = (GDN_QK, GDN_QK, GDN_V, GDN_V, GDN_HEADS, GDN_HEADS,
             SSD_D, SSD_D, SSD_BC, SSD_BC, SSD_HEADS)
D_IN_PROJ = sum(IN_SPLITS)

kernel_name = "hybrid_gdn_ssd_parallel_heads_convffn"


def rms_norm(x, w):
    xf = x.astype(jnp.float32)
    y = xf * lax.rsqrt(jnp.mean(xf * xf, axis=-1, keepdims=True) + EPS)
    return (y * w.astype(jnp.float32)).astype(x.dtype)


def l2_normalize(x):
    xf = x.astype(jnp.float32)
    return xf * lax.rsqrt(jnp.sum(xf * xf, axis=-1, keepdims=True) + EPS)


def causal_dwconv(x, w, b=None):
    k_w, ch = w.shape
    y = lax.conv_general_dilated(
        x, w[:, None, :].astype(x.dtype), window_strides=(1,), padding=[(k_w - 1, 0)],
        dimension_numbers=("NWC", "WIO", "NWC"), feature_group_count=ch)
    if b is not None:
        y = y + b.astype(x.dtype)
    return y


def gdn_chunked(q, k, v, g, beta):
    bsz, s, h, dk = q.shape
    dv = v.shape[-1]
    n = s // CHUNK

    def chunk(t):
        return jnp.moveaxis(t.reshape(bsz, n, CHUNK, h, *t.shape[3:]), 3, 1)

    q, k, v, g, beta = (chunk(t) for t in (q, k, v, g, beta))
    gc = jnp.cumsum(g, axis=-1)
    causal = jnp.tril(jnp.ones((CHUNK, CHUNK), dtype=bool))
    strict = jnp.tril(jnp.ones((CHUNK, CHUNK), dtype=bool), -1)
    decay = jnp.exp(jnp.where(causal, gc[..., :, None] - gc[..., None, :], -jnp.inf))
    kb = k * beta[..., None]
    a_in = jnp.einsum("bhnid,bhnjd->bhnij", kb, k) * decay
    lmat = jnp.where(strict, a_in, 0.0) + jnp.eye(CHUNK, dtype=a_in.dtype)
    rhs = jnp.concatenate([v * beta[..., None], kb * jnp.exp(gc)[..., None]], axis=-1)
    sol = lax.linalg.triangular_solve(lmat, rhs, left_side=True, lower=True, unit_diagonal=True)
    u, w = sol[..., :dv], sol[..., dv:]
    qk = jnp.einsum("bhnid,bhnjd->bhnij", q, k) * decay
    q_dec = q * jnp.exp(gc)[..., None]
    k_dec = k * jnp.exp(gc[..., -1:] - gc)[..., None]
    g_last = jnp.exp(gc[..., -1])

    def step(state, xs):
        u_i, w_i, qk_i, q_i, k_i, gl = xs
        v_new = u_i - jnp.einsum("bhcd,bhde->bhce", w_i, state)
        o = jnp.einsum("bhcd,bhde->bhce", q_i, state) + jnp.einsum("bhij,bhje->bhie", qk_i, v_new)
        state = state * gl[..., None, None] + jnp.einsum("bhcd,bhce->bhde", k_i, v_new)
        return state, o

    xs = tuple(jnp.moveaxis(t, 2, 0) for t in (u, w, qk, q_dec, k_dec, g_last))
    s0 = jnp.zeros((bsz, h, dk, dv), dtype=q.dtype)
    _, o = lax.scan(step, s0, xs)
    return jnp.transpose(o, (1, 0, 3, 2, 4)).reshape(bsz, s, h, dv)


def ssd_chunked(x, dt, a_neg, bmat, cmat):
    bsz, s, grp, hg, p = x.shape
    nst = bmat.shape[-1]
    n = s // CHUNK
    xdt = x * dt[..., None]
    adt = dt * a_neg

    def chunk(t):
        return jnp.moveaxis(t.reshape(bsz, n, CHUNK, *t.shape[2:]), 1, 0)

    xs = tuple(chunk(t) for t in (xdt, adt, bmat, cmat))
    causal = jnp.tril(jnp.ones((CHUNK, CHUNK), dtype=bool))

    def step(state, xs):
        xdt_i, adt_i, b_i, c_i = xs
        acs = jnp.cumsum(jnp.moveaxis(adt_i, 1, -1), axis=-1)
        lmat = jnp.exp(jnp.where(causal, acs[..., :, None] - acs[..., None, :], -jnp.inf))
        cb = jnp.einsum("blgn,bsgn->bgls", c_i, b_i)
        y_diag = jnp.einsum("bgls,bghls,bsghp->blghp", cb, lmat, xdt_i)
        y_off = jnp.einsum("blgn,bghpn,bghl->blghp", c_i, state, jnp.exp(acs))
        decay_s = jnp.exp(acs[..., -1:] - acs)
        state = state * jnp.exp(acs[..., -1])[..., None, None] + jnp.einsum(
            "bsgn,bghs,bsghp->bghpn", b_i, decay_s, xdt_i)
        return state, y_diag + y_off

    s0 = jnp.zeros((bsz, grp, hg, p, nst), dtype=x.dtype)
    _, y = lax.scan(step, s0, xs)
    return jnp.moveaxis(y, 0, 1).reshape(bsz, s, grp, hg, p)


def hybrid_layer(x, pre_mix_norm, w_in, gdn_conv_w, gdn_a_log, gdn_dt_bias, gdn_norm_w,
                 ssd_conv_w, ssd_conv_b, ssd_a_log, ssd_dt_bias, ssd_d, ssd_norm_w,
                 w_out, post_mix_norm, pre_ffn_norm, w_up, ffn_conv_w, ffn_conv_b,
                 w_down, post_ffn_norm):
    f32 = jnp.float32
    bsz, s, _ = x.shape
    h = rms_norm(x, pre_mix_norm)
    proj = h @ w_in
    offsets = np.cumsum(IN_SPLITS)[:-1].tolist()
    q, k, v, z_a, b_a, a_a, z_s, x_s, b_s, c_s, dt_s = jnp.split(proj, offsets, axis=-1)

    qkv = jax.nn.silu(causal_dwconv(jnp.concatenate([q, k, v], axis=-1), gdn_conv_w))
    q, k, v = jnp.split(qkv, [GDN_QK, 2 * GDN_QK], axis=-1)
    q = l2_normalize(q.reshape(bsz, s, GDN_HEADS, GDN_DK)) * (GDN_DK ** -0.5)
    k = l2_normalize(k.reshape(bsz, s, GDN_HEADS, GDN_DK))
    v = v.reshape(bsz, s, GDN_HEADS, GDN_DV).astype(f32)
    beta = jax.nn.sigmoid(b_a.astype(f32))
    g = -jnp.exp(gdn_a_log.astype(f32)) * jax.nn.softplus(a_a.astype(f32) + gdn_dt_bias.astype(f32))
    o_a = gdn_chunked(q, k, v, g, beta)
    o_a = rms_norm(o_a, gdn_norm_w) * jax.nn.silu(z_a.reshape(bsz, s, GDN_HEADS, GDN_DV).astype(f32))
    o_a = o_a.reshape(bsz, s, GDN_V).astype(x.dtype)

    xbc = jax.nn.silu(causal_dwconv(jnp.concatenate([x_s, b_s, c_s], axis=-1), ssd_conv_w, ssd_conv_b))
    xs_, bm, cm = jnp.split(xbc, [SSD_D, SSD_D + SSD_BC], axis=-1)
    xs_ = xs_.reshape(bsz, s, SSD_GROUPS, SSD_HPG, SSD_HEADDIM).astype(f32)
    bm = bm.reshape(bsz, s, SSD_GROUPS, SSD_STATE).astype(f32)
    cm = cm.reshape(bsz, s, SSD_GROUPS, SSD_STATE).astype(f32)
    dt = jax.nn.softplus(dt_s.astype(f32) + ssd_dt_bias.astype(f32)).reshape(bsz, s, SSD_GROUPS, SSD_HPG)
    a_neg = -jnp.exp(ssd_a_log.astype(f32)).reshape(SSD_GROUPS, SSD_HPG)
    y = ssd_chunked(xs_, dt, a_neg, bm, cm)
    y = y + ssd_d.astype(f32).reshape(SSD_GROUPS, SSD_HPG)[..., None] * xs_
    y = y.reshape(bsz, s, SSD_D) * jax.nn.silu(z_s.astype(f32))
    y = rms_norm(y.reshape(bsz, s, SSD_GROUPS, SSD_D // SSD_GROUPS),
                 ssd_norm_w.reshape(SSD_GROUPS, SSD_D // SSD_GROUPS))
    o_s = y.reshape(bsz, s, SSD_D).astype(x.dtype)

    mix = jnp.concatenate([o_a, o_s], axis=-1) @ w_out
    x = x + rms_norm(mix, post_mix_norm)

    h = rms_norm(x, pre_ffn_norm)
    u = causal_dwconv(h @ w_up, ffn_conv_w, ffn_conv_b)
    gate, up = jnp.split(u, 2, axis=-1)
    f = (jax.nn.silu(gate) * up) @ w_down
    return x + rms_norm(f, post_ffn_norm)


def _fwd_setup_inputs(seed: int = 0) -> dict:
    key = jax.random.key(seed)
    ks = jax.random.split(key, 24)
    f32 = jnp.float32
    L = DEPTH

    def nrm(k, shape, scale):
        return jax.random.normal(k, shape, f32) * scale

    def gain(k, shape):
        return 1.0 + 0.05 * jax.random.normal(k, shape, f32)

    def dt_bias(k, shape):
        dt = jnp.exp(jax.random.uniform(k, shape, f32, math.log(1e-3), math.log(1e-1)))
        return dt + jnp.log(-jnp.expm1(-dt))

    def a_log(k, shape):
        return jnp.log(jax.random.uniform(k, shape, f32, 1.0, 16.0))

    return {
        "x": nrm(ks[0], (BATCH, SEQ, D_MODEL), 1.0),
        "pre_mix_norm": gain(ks[1], (L, D_MODEL)),
        "w_in": nrm(ks[2], (L, D_MODEL, D_IN_PROJ), D_MODEL ** -0.5),
        "gdn_conv_w": nrm(ks[3], (L, CONV_K, GDN_CONV_CH), CONV_K ** -0.5),
        "gdn_a_log": a_log(ks[4], (L, GDN_HEADS)),
        "gdn_dt_bias": dt_bias(ks[5], (L, GDN_HEADS)),
        "gdn_norm_w": gain(ks[6], (L, GDN_DV)),
        "ssd_conv_w": nrm(ks[7], (L, CONV_K, SSD_CONV_CH), CONV_K ** -0.5),
        "ssd_conv_b": nrm(ks[8], (L, SSD_CONV_CH), 0.02),
        "ssd_a_log": a_log(ks[9], (L, SSD_HEADS)),
        "ssd_dt_bias": dt_bias(ks[10], (L, SSD_HEADS)),
        "ssd_d": gain(ks[11], (L, SSD_HEADS)),
        "ssd_norm_w": gain(ks[12], (L, SSD_D)),
        "w_out": nrm(ks[13], (L, MIX_WIDTH, D_MODEL), MIX_WIDTH ** -0.5),
        "post_mix_norm": gain(ks[14], (L, D_MODEL)),
        "pre_ffn_norm": gain(ks[15], (L, D_MODEL)),
        "w_up": nrm(ks[16], (L, D_MODEL, 2 * D_FF), D_MODEL ** -0.5),
        "ffn_conv_w": nrm(ks[17], (L, FFN_CONV_K, 2 * D_FF), FFN_CONV_K ** -0.5),
        "ffn_conv_b": nrm(ks[18], (L, 2 * D_FF), 0.02),
        "w_down": nrm(ks[19], (L, D_FF, D_MODEL), D_FF ** -0.5),
        "post_ffn_norm": gain(ks[20], (L, D_MODEL)),
    }


def _fwd_reference(x, pre_mix_norm, w_in, gdn_conv_w, gdn_a_log, gdn_dt_bias, gdn_norm_w,
              ssd_conv_w, ssd_conv_b, ssd_a_log, ssd_dt_bias, ssd_d, ssd_norm_w,
              w_out, post_mix_norm, pre_ffn_norm, w_up, ffn_conv_w, ffn_conv_b,
              w_down, post_ffn_norm):
    for l in range(DEPTH):
        x = hybrid_layer(x, pre_mix_norm[l], w_in[l], gdn_conv_w[l], gdn_a_log[l], gdn_dt_bias[l],
                         gdn_norm_w[l], ssd_conv_w[l], ssd_conv_b[l], ssd_a_log[l], ssd_dt_bias[l],
                         ssd_d[l], ssd_norm_w[l], w_out[l], post_mix_norm[l], pre_ffn_norm[l],
                         w_up[l], ffn_conv_w[l], ffn_conv_b[l], w_down[l], post_ffn_norm[l])
    return x


import jax as _jax
import jax.numpy as _jnp

TWIN_FORMAT = 'train_step'
FWD_PARAMS = ['x', 'pre_mix_norm', 'w_in', 'gdn_conv_w', 'gdn_a_log', 'gdn_dt_bias', 'gdn_norm_w', 'ssd_conv_w', 'ssd_conv_b', 'ssd_a_log', 'ssd_dt_bias', 'ssd_d', 'ssd_norm_w', 'w_out', 'post_mix_norm', 'pre_ffn_norm', 'w_up', 'ffn_conv_w', 'ffn_conv_b', 'w_down', 'post_ffn_norm']
TWIN_WEIGHTS = ['pre_mix_norm', 'w_in', 'gdn_conv_w', 'gdn_a_log', 'gdn_dt_bias', 'gdn_norm_w', 'ssd_conv_w', 'ssd_conv_b', 'ssd_a_log', 'ssd_dt_bias', 'ssd_d', 'ssd_norm_w', 'w_out', 'post_mix_norm', 'pre_ffn_norm', 'w_up', 'ffn_conv_w', 'ffn_conv_b', 'w_down', 'post_ffn_norm']
TWIN_DIFF_INPUT = 'x'
TWIN_INPUTS = ['x', 'pre_mix_norm', 'w_in', 'gdn_conv_w', 'gdn_a_log', 'gdn_dt_bias', 'gdn_norm_w', 'ssd_conv_w', 'ssd_conv_b', 'ssd_a_log', 'ssd_dt_bias', 'ssd_d', 'ssd_norm_w', 'w_out', 'post_mix_norm', 'pre_ffn_norm', 'w_up', 'ffn_conv_w', 'ffn_conv_b', 'w_down', 'post_ffn_norm', 'loss_target', 'm_pre_mix_norm', 'm_w_in', 'm_gdn_conv_w', 'm_gdn_a_log', 'm_gdn_dt_bias', 'm_gdn_norm_w', 'm_ssd_conv_w', 'm_ssd_conv_b', 'm_ssd_a_log', 'm_ssd_dt_bias', 'm_ssd_d', 'm_ssd_norm_w', 'm_w_out', 'm_post_mix_norm', 'm_pre_ffn_norm', 'm_w_up', 'm_ffn_conv_w', 'm_ffn_conv_b', 'm_w_down', 'm_post_ffn_norm', 'v_pre_mix_norm', 'v_w_in', 'v_gdn_conv_w', 'v_gdn_a_log', 'v_gdn_dt_bias', 'v_gdn_norm_w', 'v_ssd_conv_w', 'v_ssd_conv_b', 'v_ssd_a_log', 'v_ssd_dt_bias', 'v_ssd_d', 'v_ssd_norm_w', 'v_w_out', 'v_post_mix_norm', 'v_pre_ffn_norm', 'v_w_up', 'v_ffn_conv_w', 'v_ffn_conv_b', 'v_w_down', 'v_post_ffn_norm']
TWIN_OUTPUTS = ['loss', 'grad_x', 'grad_pre_mix_norm', 'grad_w_in', 'grad_gdn_conv_w', 'grad_gdn_a_log', 'grad_gdn_dt_bias', 'grad_gdn_norm_w', 'grad_ssd_conv_w', 'grad_ssd_conv_b', 'grad_ssd_a_log', 'grad_ssd_dt_bias', 'grad_ssd_d', 'grad_ssd_norm_w', 'grad_w_out', 'grad_post_mix_norm', 'grad_pre_ffn_norm', 'grad_w_up', 'grad_ffn_conv_w', 'grad_ffn_conv_b', 'grad_w_down', 'grad_post_ffn_norm', 'delta_pre_mix_norm', 'delta_w_in', 'delta_gdn_conv_w', 'delta_gdn_a_log', 'delta_gdn_dt_bias', 'delta_gdn_norm_w', 'delta_ssd_conv_w', 'delta_ssd_conv_b', 'delta_ssd_a_log', 'delta_ssd_dt_bias', 'delta_ssd_d', 'delta_ssd_norm_w', 'delta_w_out', 'delta_post_mix_norm', 'delta_pre_ffn_norm', 'delta_w_up', 'delta_ffn_conv_w', 'delta_ffn_conv_b', 'delta_w_down', 'delta_post_ffn_norm', 'new_m_pre_mix_norm', 'new_m_w_in', 'new_m_gdn_conv_w', 'new_m_gdn_a_log', 'new_m_gdn_dt_bias', 'new_m_gdn_norm_w', 'new_m_ssd_conv_w', 'new_m_ssd_conv_b', 'new_m_ssd_a_log', 'new_m_ssd_dt_bias', 'new_m_ssd_d', 'new_m_ssd_norm_w', 'new_m_w_out', 'new_m_post_mix_norm', 'new_m_pre_ffn_norm', 'new_m_w_up', 'new_m_ffn_conv_w', 'new_m_ffn_conv_b', 'new_m_w_down', 'new_m_post_ffn_norm', 'new_v_pre_mix_norm', 'new_v_w_in', 'new_v_gdn_conv_w', 'new_v_gdn_a_log', 'new_v_gdn_dt_bias', 'new_v_gdn_norm_w', 'new_v_ssd_conv_w', 'new_v_ssd_conv_b', 'new_v_ssd_a_log', 'new_v_ssd_dt_bias', 'new_v_ssd_d', 'new_v_ssd_norm_w', 'new_v_w_out', 'new_v_post_mix_norm', 'new_v_pre_ffn_norm', 'new_v_w_up', 'new_v_ffn_conv_w', 'new_v_ffn_conv_b', 'new_v_w_down', 'new_v_post_ffn_norm']
TWIN_LEAF_KINDS = {'loss': 'loss', 'grad_x': 'grad_x', 'grad_pre_mix_norm': 'grad_w', 'grad_w_in': 'grad_w', 'grad_gdn_conv_w': 'grad_w', 'grad_gdn_a_log': 'grad_w', 'grad_gdn_dt_bias': 'grad_w', 'grad_gdn_norm_w': 'grad_w', 'grad_ssd_conv_w': 'grad_w', 'grad_ssd_conv_b': 'grad_w', 'grad_ssd_a_log': 'grad_w', 'grad_ssd_dt_bias': 'grad_w', 'grad_ssd_d': 'grad_w', 'grad_ssd_norm_w': 'grad_w', 'grad_w_out': 'grad_w', 'grad_post_mix_norm': 'grad_w', 'grad_pre_ffn_norm': 'grad_w', 'grad_w_up': 'grad_w', 'grad_ffn_conv_w': 'grad_w', 'grad_ffn_conv_b': 'grad_w', 'grad_w_down': 'grad_w', 'grad_post_ffn_norm': 'grad_w', 'delta_pre_mix_norm': 'delta_w', 'delta_w_in': 'delta_w', 'delta_gdn_conv_w': 'delta_w', 'delta_gdn_a_log': 'delta_w', 'delta_gdn_dt_bias': 'delta_w', 'delta_gdn_norm_w': 'delta_w', 'delta_ssd_conv_w': 'delta_w', 'delta_ssd_conv_b': 'delta_w', 'delta_ssd_a_log': 'delta_w', 'delta_ssd_dt_bias': 'delta_w', 'delta_ssd_d': 'delta_w', 'delta_ssd_norm_w': 'delta_w', 'delta_w_out': 'delta_w', 'delta_post_mix_norm': 'delta_w', 'delta_pre_ffn_norm': 'delta_w', 'delta_w_up': 'delta_w', 'delta_ffn_conv_w': 'delta_w', 'delta_ffn_conv_b': 'delta_w', 'delta_w_down': 'delta_w', 'delta_post_ffn_norm': 'delta_w', 'new_m_pre_mix_norm': 'new_m', 'new_m_w_in': 'new_m', 'new_m_gdn_conv_w': 'new_m', 'new_m_gdn_a_log': 'new_m', 'new_m_gdn_dt_bias': 'new_m', 'new_m_gdn_norm_w': 'new_m', 'new_m_ssd_conv_w': 'new_m', 'new_m_ssd_conv_b': 'new_m', 'new_m_ssd_a_log': 'new_m', 'new_m_ssd_dt_bias': 'new_m', 'new_m_ssd_d': 'new_m', 'new_m_ssd_norm_w': 'new_m', 'new_m_w_out': 'new_m', 'new_m_post_mix_norm': 'new_m', 'new_m_pre_ffn_norm': 'new_m', 'new_m_w_up': 'new_m', 'new_m_ffn_conv_w': 'new_m', 'new_m_ffn_conv_b': 'new_m', 'new_m_w_down': 'new_m', 'new_m_post_ffn_norm': 'new_m', 'new_v_pre_mix_norm': 'new_v', 'new_v_w_in': 'new_v', 'new_v_gdn_conv_w': 'new_v', 'new_v_gdn_a_log': 'new_v', 'new_v_gdn_dt_bias': 'new_v', 'new_v_gdn_norm_w': 'new_v', 'new_v_ssd_conv_w': 'new_v', 'new_v_ssd_conv_b': 'new_v', 'new_v_ssd_a_log': 'new_v', 'new_v_ssd_dt_bias': 'new_v', 'new_v_ssd_d': 'new_v', 'new_v_ssd_norm_w': 'new_v', 'new_v_w_out': 'new_v', 'new_v_post_mix_norm': 'new_v', 'new_v_pre_ffn_norm': 'new_v', 'new_v_w_up': 'new_v', 'new_v_ffn_conv_w': 'new_v', 'new_v_ffn_conv_b': 'new_v', 'new_v_w_down': 'new_v', 'new_v_post_ffn_norm': 'new_v'}


def _forward(args):
    return _fwd_reference(*[args[k] for k in FWD_PARAMS])


def _output_shape():
    out = _jax.eval_shape(lambda: _forward(_fwd_setup_inputs(0)))
    return out.shape, out.dtype

N_MICROBATCH = 1
ADAM_LR = 0.001
ADAM_B1 = 0.9
ADAM_B2 = 0.999
ADAM_EPS = 1e-08
ADAM_WD = 0.01
ADAM_STEP = 10
PER_EXAMPLE_BATCH_AXIS = {'x': 0, 'loss_target': 0}
SHARED_INPUTS = []
_WEIGHT_DTYPES = {'pre_mix_norm': _jnp.float32, 'w_in': _jnp.float32, 'gdn_conv_w': _jnp.float32, 'gdn_a_log': _jnp.float32, 'gdn_dt_bias': _jnp.float32, 'gdn_norm_w': _jnp.float32, 'ssd_conv_w': _jnp.float32, 'ssd_conv_b': _jnp.float32, 'ssd_a_log': _jnp.float32, 'ssd_dt_bias': _jnp.float32, 'ssd_d': _jnp.float32, 'ssd_norm_w': _jnp.float32, 'w_out': _jnp.float32, 'post_mix_norm': _jnp.float32, 'pre_ffn_norm': _jnp.float32, 'w_up': _jnp.float32, 'ffn_conv_w': _jnp.float32, 'ffn_conv_b': _jnp.float32, 'w_down': _jnp.float32, 'post_ffn_norm': _jnp.float32}
MOMENT_SCALE = {'pre_mix_norm': 9.940586e-01, 'w_in': 4.043219e-01, 'gdn_conv_w': 2.823989e-01, 'gdn_a_log': 4.587088e+00, 'gdn_dt_bias': 4.530855e+00, 'gdn_norm_w': 1.551994e+00, 'ssd_conv_w': 5.633119e-01, 'ssd_conv_b': 1.410764e+00, 'ssd_a_log': 5.971761e+00, 'ssd_dt_bias': 1.423195e+00, 'ssd_d': 3.033066e+00, 'ssd_norm_w': 8.196592e-01, 'w_out': 9.359603e-01, 'post_mix_norm': 6.398618e+01, 'pre_ffn_norm': 8.510617e-01, 'w_up': 3.443318e-01, 'ffn_conv_w': 3.824397e-01, 'ffn_conv_b': 1.927220e+00, 'w_down': 6.566915e-01, 'post_ffn_norm': 6.421972e+01}


def _to_microbatches(a, axis):
    t = _jnp.moveaxis(a, axis, 0)
    t = t.reshape((N_MICROBATCH, t.shape[0] // N_MICROBATCH) + t.shape[1:])
    return _jnp.moveaxis(t, 1, axis + 1)


def setup_inputs(seed: int = 0) -> dict:
    inp = _fwd_setup_inputs(seed)
    key = _jax.random.fold_in(_jax.random.key(seed), 7919)
    shape, _ = _output_shape()
    out = dict(inp)
    out["loss_target"] = _jax.random.normal(_jax.random.fold_in(key, 0), shape, _jnp.float32)
    for i, name in enumerate(TWIN_WEIGHTS):
        w = inp[name].astype(_jnp.float32)
        if MOMENT_SCALE is None:
            s = _jnp.sqrt(_jnp.mean(_jnp.square(w)) + 1e-30)
        else:
            s = MOMENT_SCALE[name]
        km, kv = _jax.random.split(_jax.random.fold_in(key, i + 1))
        out[name] = w
        out["m_" + name] = s * _jax.random.normal(km, w.shape, _jnp.float32)
        out["v_" + name] = (s * s) * _jax.random.uniform(kv, w.shape, _jnp.float32, 0.5, 1.5)
    if N_MICROBATCH > 1:
        for name, axis in PER_EXAMPLE_BATCH_AXIS.items():
            out[name] = _to_microbatches(out[name], axis)
    return {'x': out['x'], 'pre_mix_norm': out['pre_mix_norm'], 'w_in': out['w_in'], 'gdn_conv_w': out['gdn_conv_w'], 'gdn_a_log': out['gdn_a_log'], 'gdn_dt_bias': out['gdn_dt_bias'], 'gdn_norm_w': out['gdn_norm_w'], 'ssd_conv_w': out['ssd_conv_w'], 'ssd_conv_b': out['ssd_conv_b'], 'ssd_a_log': out['ssd_a_log'], 'ssd_dt_bias': out['ssd_dt_bias'], 'ssd_d': out['ssd_d'], 'ssd_norm_w': out['ssd_norm_w'], 'w_out': out['w_out'], 'post_mix_norm': out['post_mix_norm'], 'pre_ffn_norm': out['pre_ffn_norm'], 'w_up': out['w_up'], 'ffn_conv_w': out['ffn_conv_w'], 'ffn_conv_b': out['ffn_conv_b'], 'w_down': out['w_down'], 'post_ffn_norm': out['post_ffn_norm'], 'loss_target': out['loss_target'], 'm_pre_mix_norm': out['m_pre_mix_norm'], 'm_w_in': out['m_w_in'], 'm_gdn_conv_w': out['m_gdn_conv_w'], 'm_gdn_a_log': out['m_gdn_a_log'], 'm_gdn_dt_bias': out['m_gdn_dt_bias'], 'm_gdn_norm_w': out['m_gdn_norm_w'], 'm_ssd_conv_w': out['m_ssd_conv_w'], 'm_ssd_conv_b': out['m_ssd_conv_b'], 'm_ssd_a_log': out['m_ssd_a_log'], 'm_ssd_dt_bias': out['m_ssd_dt_bias'], 'm_ssd_d': out['m_ssd_d'], 'm_ssd_norm_w': out['m_ssd_norm_w'], 'm_w_out': out['m_w_out'], 'm_post_mix_norm': out['m_post_mix_norm'], 'm_pre_ffn_norm': out['m_pre_ffn_norm'], 'm_w_up': out['m_w_up'], 'm_ffn_conv_w': out['m_ffn_conv_w'], 'm_ffn_conv_b': out['m_ffn_conv_b'], 'm_w_down': out['m_w_down'], 'm_post_ffn_norm': out['m_post_ffn_norm'], 'v_pre_mix_norm': out['v_pre_mix_norm'], 'v_w_in': out['v_w_in'], 'v_gdn_conv_w': out['v_gdn_conv_w'], 'v_gdn_a_log': out['v_gdn_a_log'], 'v_gdn_dt_bias': out['v_gdn_dt_bias'], 'v_gdn_norm_w': out['v_gdn_norm_w'], 'v_ssd_conv_w': out['v_ssd_conv_w'], 'v_ssd_conv_b': out['v_ssd_conv_b'], 'v_ssd_a_log': out['v_ssd_a_log'], 'v_ssd_dt_bias': out['v_ssd_dt_bias'], 'v_ssd_d': out['v_ssd_d'], 'v_ssd_norm_w': out['v_ssd_norm_w'], 'v_w_out': out['v_w_out'], 'v_post_mix_norm': out['v_post_mix_norm'], 'v_pre_ffn_norm': out['v_pre_ffn_norm'], 'v_w_up': out['v_w_up'], 'v_ffn_conv_w': out['v_ffn_conv_w'], 'v_ffn_conv_b': out['v_ffn_conv_b'], 'v_w_down': out['v_w_down'], 'v_post_ffn_norm': out['v_post_ffn_norm']}


def _loss(weights, diff, rest, loss_target):
    with _jax.named_scope("forward"):
        args = {**rest, TWIN_DIFF_INPUT: diff, **{k: w.astype(_WEIGHT_DTYPES[k]) for k, w in weights.items()}}
        y = _forward(args)
    with _jax.named_scope("loss_head"):
        err = _jnp.square(y.astype(_jnp.float32) - loss_target)
        return 0.5 * _jnp.sum(_jnp.mean(err, axis=-1)) if err.ndim else 0.5 * err


def _adamw(w, g, m, v):
    m = ADAM_B1 * m + (1.0 - ADAM_B1) * g
    v = ADAM_B2 * v + (1.0 - ADAM_B2) * _jnp.square(g)
    m_hat = m / (1.0 - ADAM_B1 ** ADAM_STEP)
    v_hat = v / (1.0 - ADAM_B2 ** ADAM_STEP)
    delta = -ADAM_LR * (m_hat / (_jnp.sqrt(v_hat) + ADAM_EPS) + ADAM_WD * w)
    return delta, m, v


def reference(x, pre_mix_norm, w_in, gdn_conv_w, gdn_a_log, gdn_dt_bias, gdn_norm_w, ssd_conv_w, ssd_conv_b, ssd_a_log, ssd_dt_bias, ssd_d, ssd_norm_w, w_out, post_mix_norm, pre_ffn_norm, w_up, ffn_conv_w, ffn_conv_b, w_down, post_ffn_norm, loss_target, m_pre_mix_norm, m_w_in, m_gdn_conv_w, m_gdn_a_log, m_gdn_dt_bias, m_gdn_norm_w, m_ssd_conv_w, m_ssd_conv_b, m_ssd_a_log, m_ssd_dt_bias, m_ssd_d, m_ssd_norm_w, m_w_out, m_post_mix_norm, m_pre_ffn_norm, m_w_up, m_ffn_conv_w, m_ffn_conv_b, m_w_down, m_post_ffn_norm, v_pre_mix_norm, v_w_in, v_gdn_conv_w, v_gdn_a_log, v_gdn_dt_bias, v_gdn_norm_w, v_ssd_conv_w, v_ssd_conv_b, v_ssd_a_log, v_ssd_dt_bias, v_ssd_d, v_ssd_norm_w, v_w_out, v_post_mix_norm, v_pre_ffn_norm, v_w_up, v_ffn_conv_w, v_ffn_conv_b, v_w_down, v_post_ffn_norm):
    given = dict(x=x, pre_mix_norm=pre_mix_norm, w_in=w_in, gdn_conv_w=gdn_conv_w, gdn_a_log=gdn_a_log, gdn_dt_bias=gdn_dt_bias, gdn_norm_w=gdn_norm_w, ssd_conv_w=ssd_conv_w, ssd_conv_b=ssd_conv_b, ssd_a_log=ssd_a_log, ssd_dt_bias=ssd_dt_bias, ssd_d=ssd_d, ssd_norm_w=ssd_norm_w, w_out=w_out, post_mix_norm=post_mix_norm, pre_ffn_norm=pre_ffn_norm, w_up=w_up, ffn_conv_w=ffn_conv_w, ffn_conv_b=ffn_conv_b, w_down=w_down, post_ffn_norm=post_ffn_norm, loss_target=loss_target, m_pre_mix_norm=m_pre_mix_norm, m_w_in=m_w_in, m_gdn_conv_w=m_gdn_conv_w, m_gdn_a_log=m_gdn_a_log, m_gdn_dt_bias=m_gdn_dt_bias, m_gdn_norm_w=m_gdn_norm_w, m_ssd_conv_w=m_ssd_conv_w, m_ssd_conv_b=m_ssd_conv_b, m_ssd_a_log=m_ssd_a_log, m_ssd_dt_bias=m_ssd_dt_bias, m_ssd_d=m_ssd_d, m_ssd_norm_w=m_ssd_norm_w, m_w_out=m_w_out, m_post_mix_norm=m_post_mix_norm, m_pre_ffn_norm=m_pre_ffn_norm, m_w_up=m_w_up, m_ffn_conv_w=m_ffn_conv_w, m_ffn_conv_b=m_ffn_conv_b, m_w_down=m_w_down, m_post_ffn_norm=m_post_ffn_norm, v_pre_mix_norm=v_pre_mix_norm, v_w_in=v_w_in, v_gdn_conv_w=v_gdn_conv_w, v_gdn_a_log=v_gdn_a_log, v_gdn_dt_bias=v_gdn_dt_bias, v_gdn_norm_w=v_gdn_norm_w, v_ssd_conv_w=v_ssd_conv_w, v_ssd_conv_b=v_ssd_conv_b, v_ssd_a_log=v_ssd_a_log, v_ssd_dt_bias=v_ssd_dt_bias, v_ssd_d=v_ssd_d, v_ssd_norm_w=v_ssd_norm_w, v_w_out=v_w_out, v_post_mix_norm=v_post_mix_norm, v_pre_ffn_norm=v_pre_ffn_norm, v_w_up=v_w_up, v_ffn_conv_w=v_ffn_conv_w, v_ffn_conv_b=v_ffn_conv_b, v_w_down=v_w_down, v_post_ffn_norm=v_post_ffn_norm)
    weights = {n: given[n] for n in TWIN_WEIGHTS}
    shared = {n: given[n] for n in SHARED_INPUTS}
    per_example = {n: given[n] for n in ['x']}
    grad_fn = _jax.value_and_grad(_loss, argnums=(0, 1))

    def one_microbatch(ex, loss_target):
        ex = dict(ex)
        diff = ex.pop(TWIN_DIFF_INPUT)
        return grad_fn(weights, diff, {**shared, **ex}, loss_target)

    if N_MICROBATCH == 1:
        loss, (grad_w, grad_x) = one_microbatch(per_example, given["loss_target"])
    else:
        def body(carry, xs):
            loss_sum, grad_sum = carry
            l_k, (gw_k, gx_k) = one_microbatch(xs[0], xs[1])
            with _jax.named_scope("update"):
                return (loss_sum + l_k, _jax.tree.map(_jnp.add, grad_sum, gw_k)), gx_k

        init = (_jnp.zeros((), _jnp.float32), _jax.tree.map(_jnp.zeros_like, weights))
        (loss, grad_w), grad_x = _jax.lax.scan(body, init, (per_example, given["loss_target"]))
    with _jax.named_scope("update"):
        delta_w, new_m, new_v = {}, {}, {}
        for n in TWIN_WEIGHTS:
            delta_w[n], new_m[n], new_v[n] = _adamw(weights[n], grad_w[n], given["m_" + n], given["v_" + n])
    return (loss, grad_x, *[grad_w[n] for n in TWIN_WEIGHTS], *[delta_w[n] for n in TWIN_WEIGHTS],
            *[new_m[n] for n in TWIN_WEIGHTS], *[new_v[n] for n in TWIN_WEIGHTS])
```

```python
import functools

import jax
import jax.numpy as jnp
from jax import lax
from jax.experimental import pallas as pl
from jax.experimental.pallas import tpu as pltpu

F32 = jnp.float32
BF16 = jnp.bfloat16

D_MODEL = 1024
GDN_HEADS = 8
GDN_DK = 128
SSD_HEADS = 16
SSD_HEADDIM = 64
SSD_GROUPS = 2
SSD_STATE = 128
CONV_K = 4
CHUNK = 64
D_FF = 2816
FFN_CONV_K = 3
EPS = 1e-6
GDN_QK = GDN_HEADS * GDN_DK
GDN_V = GDN_QK
SSD_D = SSD_HEADS * SSD_HEADDIM
SSD_BC = SSD_GROUPS * SSD_STATE
SSD_CONV_CH = SSD_D + 2 * SSD_BC
BIG = 4 * 1024 + 1024 + SSD_CONV_CH
SMALL = 128
D_IN_PROJ = 6688
LANE = 128
PAIR = 2 * CHUNK
NEG = -1e30
VMEM_LIMIT = 56 * 1024 * 1024

ADAM_LR = 0.001
ADAM_B1 = 0.9
ADAM_B2 = 0.999
ADAM_EPS = 1e-08
ADAM_WD = 0.01
ADAM_STEP = 10

N_CHIPS = 4
N_DEV = 8
MESH = pl.DeviceIdType.MESH

NN = ((1,), (0,))
NT = ((1,), (1,))
TN = ((0,), (0,))


def _dot(a, b, dims, exact):
    if exact:
        return lax.dot_general(a, b, (dims, ((), ())), precision=lax.Precision.HIGHEST, preferred_element_type=F32)
    return lax.dot_general(a.astype(BF16), b.astype(BF16), (dims, ((), ())), preferred_element_type=F32)


def _make_mm(exact):
    @jax.custom_vjp
    def nn(a, b):
        return _dot(a, b, NN, exact)

    @jax.custom_vjp
    def nt(a, b):
        return _dot(a, b, NT, exact)

    @jax.custom_vjp
    def tn(a, b):
        return _dot(a, b, TN, exact)

    nn.defvjp(lambda a, b: (nn(a, b), (a, b)), lambda r, g: (nt(g, r[1]), tn(r[0], g)))
    nt.defvjp(lambda a, b: (nt(a, b), (a, b)), lambda r, g: (nn(g, r[1]), tn(g, r[0])))
    tn.defvjp(lambda a, b: (tn(a, b), (a, b)), lambda r, g: (nt(r[1], g), nn(r[0], g)))
    return nn, nt, tn


_nn, _nt, _tn = _make_mm(False)
_nnx, _ntx, _tnx = _make_mm(True)


@jax.custom_vjp
def _cst_left(cst, x):
    return _dot(cst, x, NN, True)


_cst_left.defvjp(lambda cst, x: (_cst_left(cst, x), cst), lambda cst, g: (jnp.zeros_like(cst), _dot(cst, g, TN, True)))


@jax.custom_vjp
def _cst_right(x, cst):
    return _dot(x, cst, NN, True)


_cst_right.defvjp(lambda x, cst: (_cst_right(x, cst), cst), lambda cst, g: (_dot(g, cst, NT, True), jnp.zeros_like(cst)))


def _lin_left(cst):
    return functools.partial(_cst_left, cst)


def _lin_right(cst):
    return lambda x: _cst_right(x, cst)


@jax.custom_vjp
def _top(x):
    return x[: x.shape[0] // 2]


_top.defvjp(lambda x: (_top(x), None), lambda _, g: (jnp.concatenate([g, jnp.zeros_like(g)], axis=0),))


@jax.custom_vjp
def _bot(x):
    return x[x.shape[0] // 2:]


_bot.defvjp(lambda x: (_bot(x), None), lambda _, g: (jnp.concatenate([jnp.zeros_like(g), g], axis=0),))


@jax.custom_vjp
def _vstack(a, b):
    return jnp.concatenate([a, b], axis=0)


_vstack.defvjp(lambda a, b: (_vstack(a, b), None), lambda _, g: (g[: g.shape[0] // 2], g[g.shape[0] // 2:]))


def _shift_dn_raw(x, s):
    if s == 0:
        return x
    r = pltpu.roll(x, s, axis=0)
    ri = lax.broadcasted_iota(jnp.int32, x.shape, 0)
    return jnp.where(ri >= s, r, 0.0)


def _shift_up_raw(x, s):
    if s == 0:
        return x
    n = x.shape[0]
    r = pltpu.roll(x, n - s, axis=0)
    ri = lax.broadcasted_iota(jnp.int32, x.shape, 0)
    return jnp.where(ri < n - s, r, 0.0)


@functools.partial(jax.custom_vjp, nondiff_argnums=(1,))
def _shift_dn(x, s):
    return _shift_dn_raw(x, s)


_shift_dn.defvjp(lambda x, s: (_shift_dn_raw(x, s), None), lambda s, _, g: (_shift_up_raw(g, s),))


def _conv(x, wrows):
    k_w = len(wrows)
    acc = wrows[k_w - 1] * x
    for k in range(k_w - 1):
        acc = acc + wrows[k] * _shift_dn(x, k_w - 1 - k)
    return acc


def _silu(x):
    return x * jax.nn.sigmoid(x)


def _rms(x, w):
    return x * lax.rsqrt(jnp.mean(x * x, axis=-1, keepdims=True) + EPS) * w


def _l2n(x):
    return x * lax.rsqrt(jnp.sum(x * x, axis=-1, keepdims=True) + EPS)


def _iota2(shape):
    return lax.broadcasted_iota(jnp.int32, shape, 0), lax.broadcasted_iota(jnp.int32, shape, 1)


def _gdn_pair(qa, qb, ka, kb, va, vb, ga, gb, ba, bb, gr, sa, sb):
    q = _vstack(qa, qb)
    k = _vstack(ka, kb)
    v = _vstack(va, vb)
    g = _vstack(ga, gb)
    beta = _vstack(ba, bb)
    ri, ci = _iota2((PAIR, PAIR))
    blk = ((ri >= CHUNK) & (ci >= CHUNK)) | ((ri < CHUNK) & (ci < CHUNK))
    causal = blk & (ri >= ci)
    strict = blk & (ri > ci)
    gc = _lin_left(causal.astype(F32))(g)
    gcr = _lin_right((blk & (ri <= ci)).astype(F32))(jnp.broadcast_to(gr, (PAIR, PAIR)))
    glast = _lin_left(blk.astype(F32))(g)
    decay = jnp.exp(jnp.where(causal, gc - gcr, NEG))
    eg = jnp.exp(gc)
    kbeta = k * beta
    a = jnp.where(strict, _nt(kbeta, k) * decay, 0.0)
    p = jnp.where(ri == ci, 1.0, 0.0) - a
    ap = a
    for _ in range(5):
        ap = _nnx(ap, ap)
        p = p + _nnx(p, ap)
    u = _nnx(p, v * beta)
    w = _nnx(p, kbeta * eg)
    qk = _nt(q, k) * decay
    q_dec = q * eg
    k_dec = k * jnp.exp(glast - gc)
    v_new = u - _vstack(_nn(_top(w), sa), _nn(_bot(w), sb))
    o = _vstack(_nn(_top(q_dec), sa), _nn(_bot(q_dec), sb)) + _nn(qk, v_new)
    gl = jnp.exp(glast)
    sa2 = sa * _vstack(_top(gl), _top(gl)) + _tn(_top(k_dec), _top(v_new))
    sb2 = sb * _vstack(_bot(gl), _bot(gl)) + _tn(_bot(k_dec), _bot(v_new))
    return _top(o), _bot(o), sa2, sb2


def _ssd_pair(x, dt, adt, adt_r, bg, cg, st):
    ri, ci = _iota2((CHUNK, PAIR))
    cj = jnp.where(ci >= CHUNK, ci - CHUNK, ci)
    causal = ri >= cj
    r2, c2 = _iota2((PAIR, PAIR))
    blk2 = ((r2 >= CHUNK) & (c2 >= CHUNK)) | ((r2 < CHUNK) & (c2 < CHUNK))
    r6, c6 = _iota2((CHUNK, CHUNK))
    xdt = x * dt
    acs = _lin_left((r6 >= c6).astype(F32))(adt)
    acs_r = _lin_right((blk2 & (r2 <= c2)).astype(F32))(jnp.broadcast_to(adt_r, (CHUNK, PAIR)))
    alast = _lin_left(jnp.ones((CHUNK, CHUNK), F32))(adt)
    lmat = jnp.exp(jnp.where(causal, acs - acs_r, NEG))
    cb2 = _nt(cg, _vstack(bg, bg))
    xblk = _vstack(jnp.where(ci < CHUNK, xdt, 0.0), jnp.where(ci >= CHUNK, xdt, 0.0))
    y = _nn(cb2 * lmat, xblk) + _nn(cg, st) * jnp.exp(acs)
    el = jnp.exp(alast)
    st2 = st * _vstack(el, el) + _tn(bg, xdt * jnp.exp(alast - acs))
    return y, st2


def _pcall(name, body, grid, in_specs, out_specs, out_shape, scratch=(), sem=None, aliases=None):
    if sem is None:
        sem = ("arbitrary",) * len(grid)
    return pl.pallas_call(
        functools.partial(body),
        out_shape=out_shape,
        grid=grid,
        in_specs=in_specs,
        out_specs=out_specs,
        scratch_shapes=scratch,
        input_output_aliases=aliases or {},
        name=name,
        compiler_params=pltpu.CompilerParams(dimension_semantics=sem, vmem_limit_bytes=VMEM_LIMIT),
    )


def _sds(shape, dtype=F32):
    return jax.ShapeDtypeStruct(shape, dtype)


def _row_spec(tm, width, colblock=0):
    return pl.BlockSpec((tm, width), lambda i, _c=colblock: (i, _c))


def _full_spec(shape):
    nd = len(shape)
    return pl.BlockSpec(shape, lambda *_: (0,) * nd)


def _zero_at_first(refs, first):
    @pl.when(first)
    def _():
        for r in refs:
            r[...] = jnp.zeros(r.shape, r.dtype)


def _pick(n, prefs):
    for p in prefs:
        if n % p == 0:
            return p
    return n


def _matmul(name, a, b, mode, out_dtype):
    if mode == "tn":
        r, m = a.shape
        n = b.shape[1]
        tm = _pick(m, (1024, 1408))
        tn = _pick(n, (512, 256, 128))
        tk = _pick(r, (1024, 512, 256, 128, 64))

        def body(a_ref, b_ref, o_ref):
            _zero_at_first([o_ref], pl.program_id(2) == 0)
            o_ref[...] += _dot(a_ref[...], b_ref[...], TN, False)

        return _pcall(
            name, body, (m // tm, n // tn, r // tk),
            [pl.BlockSpec((tk, tm), lambda i, j, k: (k, i)), pl.BlockSpec((tk, tn), lambda i, j, k: (k, j))],
            pl.BlockSpec((tm, tn), lambda i, j, k: (i, j)), _sds((m, n), out_dtype),
            sem=("parallel", "parallel", "arbitrary"))(a, b)
    m, k = a.shape
    n = b.shape[1] if mode == "nn" else b.shape[0]
    tm = _pick(m, (512, 256, 128, 64))
    tn = _pick(n, (512, 256, 128))
    dims = NN if mode == "nn" else NT

    def body(a_ref, b_ref, o_ref):
        o_ref[...] = _dot(a_ref[...], b_ref[...], dims, False).astype(o_ref.dtype)

    b_spec = pl.BlockSpec((k, tn), lambda i, j: (0, j)) if mode == "nn" else pl.BlockSpec((tn, k), lambda i, j: (j, 0))
    return _pcall(
        name, body, (m // tm, n // tn), [pl.BlockSpec((tm, k), lambda i, j: (i, 0)), b_spec],
        pl.BlockSpec((tm, tn), lambda i, j: (i, j)), _sds((m, n), out_dtype), sem=("parallel", "parallel"))(a, b)


def _row_tile(t):
    return _pick(t, (256, 128, 64))


def _rms_fwd(name, x, g):
    t = x.shape[0]
    tm = _row_tile(t)

    def body(x_ref, g_ref, h_ref):
        h_ref[...] = _rms(x_ref[...], g_ref[...]).astype(BF16)

    return _pcall(name, body, (t // tm,), [_row_spec(tm, D_MODEL), _full_spec((1, D_MODEL))], _row_spec(tm, D_MODEL),
                  _sds((t, D_MODEL), BF16), sem=("parallel",))(x, g)


def _gate_consts():
    r, c = _iota2((SMALL, D_MODEL))
    e_b = (r == (c >> 7)).astype(F32)
    e_a = (r == GDN_HEADS + (c >> 7)).astype(F32)
    e_dt = (r == 2 * GDN_HEADS + (c >> 6)).astype(F32)
    r2, c2 = _iota2((D_MODEL, SMALL))
    p_g = (r2 == c2 * GDN_DK).astype(F32)
    p_s = (r2 == c2 * SSD_HEADDIM).astype(F32)
    return e_b, e_a, e_dt, p_g, p_s


def _gates_fn(small, a_log, dt_bias, s_a_log, s_dt_bias):
    e_b, e_a, e_dt, p_g, p_s = _gate_consts()
    beta = jax.nn.sigmoid(_lin_right(e_b)(small))
    g = -jnp.exp(a_log) * jax.nn.softplus(_lin_right(e_a)(small) + dt_bias)
    dt = jax.nn.softplus(_lin_right(e_dt)(small) + s_dt_bias)
    adt = dt * (-jnp.exp(s_a_log))
    return beta, g, dt, adt, _lin_right(p_g)(g), _lin_right(p_s)(adt)


def _gates_fwd(small, a_log, dt_bias, s_a_log, s_dt_bias):
    t = small.shape[0]
    tm = _row_tile(t)

    def body(s_ref, p0, p1, p2, p3, *outs):
        vals = _gates_fn(s_ref[...], p0[...], p1[...], p2[...], p3[...])
        for o, v in zip(outs, vals):
            o[...] = v

    pspec = _full_spec((1, D_MODEL))
    wide, nar = _sds((t, D_MODEL)), _sds((t, SMALL))
    return _pcall("gates_fwd", body, (t // tm,), [_row_spec(tm, SMALL)] + [pspec] * 4,
                  [_row_spec(tm, D_MODEL)] * 4 + [_row_spec(tm, SMALL)] * 2, [wide] * 4 + [nar] * 2,
                  sem=("parallel",))(small, a_log, dt_bias, s_a_log, s_dt_bias)


def _gates_bwd(small, a_log, dt_bias, s_a_log, s_dt_bias, cts):
    t = small.shape[0]
    tm = _row_tile(t)

    def body(s_ref, p0, p1, p2, p3, c0, c1, c2, c3, c4, c5, ds_ref, *accs):
        _zero_at_first(accs, pl.program_id(0) == 0)
        _, vjp = jax.vjp(_gates_fn, s_ref[...], p0[...], p1[...], p2[...], p3[...])
        grads = vjp((c0[...], c1[...], c2[...], c3[...], c4[...], c5[...]))
        ds_ref[...] = grads[0].astype(BF16)
        for a_ref, gval in zip(accs, grads[1:]):
            a_ref[...] += gval

    pspec = _full_spec((1, D_MODEL))
    return _pcall("gates_bwd", body, (t // tm,),
                  [_row_spec(tm, SMALL)] + [pspec] * 4 + [_row_spec(tm, D_MODEL)] * 4 + [_row_spec(tm, SMALL)] * 2,
                  [_row_spec(tm, SMALL)] + [pspec] * 4, [_sds((t, SMALL), BF16)] + [_sds((1, D_MODEL))] * 4)(
                      small, a_log, dt_bias, s_a_log, s_dt_bias, *cts)


def _gdn_out_fn(o, z, w):
    return _rms(o, w) * _silu(z)


def _gdn_out_fwd(o, proj, gn):
    t = o.shape[0]
    tm = _row_tile(t)

    def body(o_ref, z_ref, w_ref, y_ref):
        for h in range(GDN_HEADS):
            sl = slice(h * GDN_DK, (h + 1) * GDN_DK)
            y_ref[:, sl] = _gdn_out_fn(o_ref[:, sl], z_ref[:, sl], w_ref[...]).astype(BF16)

    return _pcall("gdn_out_fwd", body, (t // tm,), [_row_spec(tm, GDN_V), _row_spec(tm, GDN_V, 3), _full_spec((1, GDN_DK))],
                  _row_spec(tm, GDN_V), _sds((t, GDN_V), BF16), sem=("parallel",))(o, proj, gn)


def _gdn_out_bwd(o, proj, gn, d_ocat):
    t = o.shape[0]
    tm = _row_tile(t)

    def body(o_ref, z_ref, w_ref, dy_ref, do_ref, dz_ref, dw_ref):
        _zero_at_first([dw_ref], pl.program_id(0) == 0)
        for h in range(GDN_HEADS):
            sl = slice(h * GDN_DK, (h + 1) * GDN_DK)
            _, vjp = jax.vjp(_gdn_out_fn, o_ref[:, sl], z_ref[:, sl], w_ref[...])
            d_o, d_z, d_w = vjp(dy_ref[:, sl])
            do_ref[:, sl] = d_o
            dz_ref[:, sl] = d_z.astype(BF16)
            dw_ref[...] += d_w

    return _pcall("gdn_out_bwd", body, (t // tm,),
                  [_row_spec(tm, GDN_V), _row_spec(tm, GDN_V, 3), _full_spec((1, GDN_DK)), _row_spec(tm, GDN_V, 0)],
                  [_row_spec(tm, GDN_V), _row_spec(tm, GDN_V), _full_spec((1, GDN_DK))],
                  [_sds((t, GDN_V)), _sds((t, GDN_V), BF16), _sds((1, GDN_DK))])(o, proj, gn, d_ocat)


def _ssd_out_fn(y, xs, z, d_skip, w):
    return _rms((y + d_skip * xs) * _silu(z), w)


_SSD_GW = SSD_D // SSD_GROUPS


def _ssd_out_fwd(y, xbc, proj, d_skip, nw):
    t = y.shape[0]
    tm = _row_tile(t)

    def body(y_ref, x_ref, z_ref, d_ref, w_ref, o_ref):
        for gi in range(SSD_GROUPS):
            sl = slice(gi * _SSD_GW, (gi + 1) * _SSD_GW)
            o_ref[:, sl] = _ssd_out_fn(y_ref[:, sl], x_ref[:, sl], z_ref[:, sl], d_ref[:, sl], w_ref[:, sl]).astype(BF16)

    pspec = _full_spec((1, SSD_D))
    return _pcall("ssd_out_fwd", body, (t // tm,),
                  [_row_spec(tm, SSD_D), _row_spec(tm, SSD_D, 0), _row_spec(tm, SSD_D, 4), pspec, pspec],
                  _row_spec(tm, SSD_D), _sds((t, SSD_D), BF16), sem=("parallel",))(y, xbc, proj, d_skip, nw)


def _ssd_out_bwd(y, xbc, proj, d_skip, nw, d_ocat):
    t = y.shape[0]
    tm = _row_tile(t)

    def body(y_ref, x_ref, z_ref, d_ref, w_ref, do_ref, dy_ref, dx_ref, dz_ref, dd_ref, dw_ref):
        _zero_at_first([dd_ref, dw_ref], pl.program_id(0) == 0)
        for gi in range(SSD_GROUPS):
            sl = slice(gi * _SSD_GW, (gi + 1) * _SSD_GW)
            _, vjp = jax.vjp(_ssd_out_fn, y_ref[:, sl], x_ref[:, sl], z_ref[:, sl], d_ref[:, sl], w_ref[:, sl])
            d_y, d_x, d_z, d_d, d_w = vjp(do_ref[:, sl])
            dy_ref[:, sl] = d_y
            dx_ref[:, sl] = d_x
            dz_ref[:, sl] = d_z.astype(BF16)
            dd_ref[:, sl] += d_d
            dw_ref[:, sl] += d_w

    pspec = _full_spec((1, SSD_D))
    row = _row_spec(tm, SSD_D)
    return _pcall("ssd_out_bwd", body, (t // tm,),
                  [row, _row_spec(tm, SSD_D, 0), _row_spec(tm, SSD_D, 4), pspec, pspec, _row_spec(tm, SSD_D, 1)],
                  [row, row, row, pspec, pspec],
                  [_sds((t, SSD_D)), _sds((t, SSD_D)), _sds((t, SSD_D), BF16), _sds((1, SSD_D)), _sds((1, SSD_D))])(
                      y, xbc, proj, d_skip, nw, d_ocat)


def _res1_fn(x, mix, g_pm, g_pf):
    x1 = x + _rms(mix, g_pm)
    return x1, _rms(x1, g_pf)


def _res1_fwd(x, mix, g_pm, g_pf):
    t = x.shape[0]
    tm = _row_tile(t)

    def body(x_ref, m_ref, a_ref, b_ref, x1_ref, h2_ref):
        x1, h2 = _res1_fn(x_ref[...], m_ref[...], a_ref[...], b_ref[...])
        x1_ref[...] = x1
        h2_ref[...] = h2.astype(BF16)

    row, pspec = _row_spec(tm, D_MODEL), _full_spec((1, D_MODEL))
    return _pcall("res1_fwd", body, (t // tm,), [row, row, pspec, pspec], [row, row],
                  [_sds((t, D_MODEL)), _sds((t, D_MODEL), BF16)], sem=("parallel",))(x, mix, g_pm, g_pf)


def _res1_bwd(x, mix, g_pm, g_pf, d_x1, d_h2):
    t = x.shape[0]
    tm = _row_tile(t)

    def body(x_ref, m_ref, a_ref, b_ref, c1_ref, c2_ref, dx_ref, dm_ref, da_ref, db_ref):
        _zero_at_first([da_ref, db_ref], pl.program_id(0) == 0)
        _, vjp = jax.vjp(_res1_fn, x_ref[...], m_ref[...], a_ref[...], b_ref[...])
        d_x, d_m, d_a, d_b = vjp((c1_ref[...], c2_ref[...]))
        dx_ref[...] = d_x
        dm_ref[...] = d_m.astype(BF16)
        da_ref[...] += d_a
        db_ref[...] += d_b

    row, pspec = _row_spec(tm, D_MODEL), _full_spec((1, D_MODEL))
    return _pcall("res1_bwd", body, (t // tm,), [row, row, pspec, pspec, row, row], [row, row, pspec, pspec],
                  [_sds((t, D_MODEL)), _sds((t, D_MODEL), BF16), _sds((1, D_MODEL)), _sds((1, D_MODEL))])(
                      x, mix, g_pm, g_pf, d_x1, d_h2)


def _final_fn(x1, f, g_po, tgt):
    err = x1 + _rms(f, g_po) - tgt
    return 0.5 * jnp.sum(jnp.mean(err * err, axis=-1))


def _final(x1, f, g_po, tgt):
    t = x1.shape[0]
    tm = _row_tile(t)

    def body(x_ref, f_ref, g_ref, t_ref, loss_ref, dx_ref, df_ref, dg_ref):
        _zero_at_first([loss_ref, dg_ref], pl.program_id(0) == 0)
        loss, (d_x, d_f, d_g) = jax.value_and_grad(_final_fn, argnums=(0, 1, 2))(x_ref[...], f_ref[...], g_ref[...], t_ref[...])
        loss_ref[...] += jnp.broadcast_to(loss, loss_ref.shape)
        dx_ref[...] = d_x
        df_ref[...] = d_f.astype(BF16)
        dg_ref[...] += d_g

    row, pspec = _row_spec(tm, D_MODEL), _full_spec((1, D_MODEL))
    return _pcall("final", body, (t // tm,), [row, row, pspec, row], [_full_spec((8, LANE)), row, row, pspec],
                  [_sds((8, LANE)), _sds((t, D_MODEL)), _sds((t, D_MODEL), BF16), _sds((1, D_MODEL))])(x1, f, g_po, tgt)


def _rms1_bwd(x, g, d_h, d_x1):
    t = x.shape[0]
    tm = _row_tile(t)

    def body(x_ref, g_ref, dh_ref, dx1_ref, dx_ref, dg_ref):
        _zero_at_first([dg_ref], pl.program_id(0) == 0)
        _, vjp = jax.vjp(_rms, x_ref[...], g_ref[...])
        d_x, d_g = vjp(dh_ref[...])
        dx_ref[...] = d_x + dx1_ref[...]
        dg_ref[...] += d_g

    row, pspec = _row_spec(tm, D_MODEL), _full_spec((1, D_MODEL))
    return _pcall("rms1_bwd", body, (t // tm,), [row, pspec, row, row], [row, pspec],
                  [_sds((t, D_MODEL)), _sds((1, D_MODEL))])(x, g, d_h, d_x1)


def _qkv_fn(mode):
    def fn(x, *wrows):
        y = _silu(_conv(x, wrows))
        if mode == "q":
            return _l2n(y) * (GDN_DK ** -0.5)
        if mode == "k":
            return _l2n(y)
        return y
    return fn


def _seq_spec(s, tc, off):
    return pl.BlockSpec((s, tc), lambda j, b, _o=off: (b, _o + j))


def _par_spec(rows, tc, off):
    return pl.BlockSpec((rows, tc), lambda j, b, _o=off: (0, _o + j))


def _gdn_conv_fwd(mode, proj, w, bsz, s):
    off = {"q": 0, "k": GDN_HEADS, "v": 2 * GDN_HEADS}[mode]
    fn = _qkv_fn(mode)

    def body(x_ref, w_ref, y_ref):
        y_ref[...] = fn(x_ref[...], *[w_ref[k:k + 1, :] for k in range(CONV_K)])

    return _pcall("gdn_conv_fwd_" + mode, body, (GDN_HEADS, bsz),
                  [_seq_spec(s, GDN_DK, off), _par_spec(CONV_K, GDN_DK, off)], _seq_spec(s, GDN_DK, 0),
                  _sds((bsz * s, GDN_QK)), sem=("parallel", "parallel"))(proj, w)


def _gdn_conv_bwd(mode, proj, w, d_y, bsz, s):
    off = {"q": 0, "k": GDN_HEADS, "v": 2 * GDN_HEADS}[mode]
    fn = _qkv_fn(mode)

    def body(x_ref, w_ref, dy_ref, dx_ref, dw_ref):
        _zero_at_first([dw_ref], pl.program_id(1) == 0)
        _, vjp = jax.vjp(fn, x_ref[...], *[w_ref[k:k + 1, :] for k in range(CONV_K)])
        grads = vjp(dy_ref[...])
        dx_ref[...] = grads[0].astype(BF16)
        for k in range(CONV_K):
            dw_ref[k:k + 1, :] += grads[1 + k]

    return _pcall("gdn_conv_bwd_" + mode, body, (GDN_HEADS, bsz),
                  [_seq_spec(s, GDN_DK, off), _par_spec(CONV_K, GDN_DK, off), _seq_spec(s, GDN_DK, 0)],
                  [_seq_spec(s, GDN_DK, 0), _par_spec(CONV_K, GDN_DK, 0)],
                  [_sds((bsz * s, GDN_QK), BF16), _sds((CONV_K, GDN_QK))], sem=("parallel", "arbitrary"))(proj, w, d_y)


def _ssd_conv_fn(x, bias, *wrows):
    return _silu(_conv(x, wrows) + bias)


_XBC_OFF = (5 * 1024) // LANE


def _ssd_conv_fwd(proj, w, bias, bsz, s):
    nt_ = SSD_CONV_CH // LANE

    def body(x_ref, w_ref, b_ref, y_ref):
        y_ref[...] = _ssd_conv_fn(x_ref[...], b_ref[...], *[w_ref[k:k + 1, :] for k in range(CONV_K)])

    return _pcall("ssd_conv_fwd", body, (nt_, bsz),
                  [_seq_spec(s, LANE, _XBC_OFF), _par_spec(CONV_K, LANE, 0), _par_spec(1, LANE, 0)], _seq_spec(s, LANE, 0),
                  _sds((bsz * s, SSD_CONV_CH)), sem=("parallel", "parallel"))(proj, w, bias)


def _ssd_conv_bwd(proj, w, bias, d_y, bsz, s):
    nt_ = SSD_CONV_CH // LANE

    def body(x_ref, w_ref, b_ref, dy_ref, dx_ref, dw_ref, db_ref):
        _zero_at_first([dw_ref, db_ref], pl.program_id(1) == 0)
        _, vjp = jax.vjp(_ssd_conv_fn, x_ref[...], b_ref[...], *[w_ref[k:k + 1, :] for k in range(CONV_K)])
        grads = vjp(dy_ref[...])
        dx_ref[...] = grads[0].astype(BF16)
        db_ref[...] += grads[1]
        for k in range(CONV_K):
            dw_ref[k:k + 1, :] += grads[2 + k]

    return _pcall("ssd_conv_bwd", body, (nt_, bsz),
                  [_seq_spec(s, LANE, _XBC_OFF), _par_spec(CONV_K, LANE, 0), _par_spec(1, LANE, 0), _seq_spec(s, LANE, 0)],
                  [_seq_spec(s, LANE, 0), _par_spec(CONV_K, LANE, 0), _par_spec(1, LANE, 0)],
                  [_sds((bsz * s, SSD_CONV_CH), BF16), _sds((CONV_K, SSD_CONV_CH)), _sds((1, SSD_CONV_CH))],
                  sem=("parallel", "arbitrary"))(proj, w, bias, d_y)


_FFN_TC = 256
_FFN_NT = D_FF // _FFN_TC


def _ffn_act_fn(xg, xu, bg, bu, *wrows):
    k_w = FFN_CONV_K
    gate = _conv(xg, wrows[:k_w]) + bg
    up = _conv(xu, wrows[k_w:]) + bu
    return _silu(gate) * up


def _ffn_act_fwd(u_pre, w, bias, bsz, s):
    def body(xg_ref, xu_ref, wg_ref, wu_ref, bg_ref, bu_ref, a_ref):
        rows = [wg_ref[k:k + 1, :] for k in range(FFN_CONV_K)] + [wu_ref[k:k + 1, :] for k in range(FFN_CONV_K)]
        a_ref[...] = _ffn_act_fn(xg_ref[...], xu_ref[...], bg_ref[...], bu_ref[...], *rows).astype(BF16)

    return _pcall("ffn_act_fwd", body, (_FFN_NT, bsz),
                  [_seq_spec(s, _FFN_TC, 0), _seq_spec(s, _FFN_TC, _FFN_NT),
                   _par_spec(FFN_CONV_K, _FFN_TC, 0), _par_spec(FFN_CONV_K, _FFN_TC, _FFN_NT),
                   _par_spec(1, _FFN_TC, 0), _par_spec(1, _FFN_TC, _FFN_NT)],
                  _seq_spec(s, _FFN_TC, 0), _sds((bsz * s, D_FF), BF16), sem=("parallel", "parallel"))(
                      u_pre, u_pre, w, w, bias, bias)


def _ffn_act_bwd(u_pre, w, bias, d_a, bsz, s):
    def body(xg_ref, xu_ref, wg_ref, wu_ref, bg_ref, bu_ref, da_ref, dg_ref, du_ref, dwg_ref, dwu_ref, dbg_ref, dbu_ref):
        _zero_at_first([dwg_ref, dwu_ref, dbg_ref, dbu_ref], pl.program_id(1) == 0)
        rows = [wg_ref[k:k + 1, :] for k in range(FFN_CONV_K)] + [wu_ref[k:k + 1, :] for k in range(FFN_CONV_K)]
        _, vjp = jax.vjp(_ffn_act_fn, xg_ref[...], xu_ref[...], bg_ref[...], bu_ref[...], *rows)
        grads = vjp(da_ref[...])
        dg_ref[...] = grads[0].astype(BF16)
        du_ref[...] = grads[1].astype(BF16)
        dbg_ref[...] += grads[2]
        dbu_ref[...] += grads[3]
        for k in range(FFN_CONV_K):
            dwg_ref[k:k + 1, :] += grads[4 + k]
            dwu_ref[k:k + 1, :] += grads[4 + FFN_CONV_K + k]

    seq0, par3, par1 = _seq_spec(s, _FFN_TC, 0), _par_spec(FFN_CONV_K, _FFN_TC, 0), _par_spec(1, _FFN_TC, 0)
    return _pcall("ffn_act_bwd", body, (_FFN_NT, bsz),
                  [seq0, _seq_spec(s, _FFN_TC, _FFN_NT), par3, _par_spec(FFN_CONV_K, _FFN_TC, _FFN_NT),
                   par1, _par_spec(1, _FFN_TC, _FFN_NT), seq0],
                  [seq0, seq0, par3, par3, par1, par1],
                  [_sds((bsz * s, D_FF), BF16), _sds((bsz * s, D_FF), BF16), _sds((FFN_CONV_K, D_FF)), _sds((FFN_CONV_K, D_FF)),
                   _sds((1, D_FF)), _sds((1, D_FF))], sem=("parallel", "arbitrary"))(u_pre, u_pre, w, w, bias, bias, d_a)


_GP = GDN_HEADS // 2
_SP = SSD_HEADS // 2


def _halves(ref):
    return ref[:, :LANE], ref[:, LANE:]


def _gdn_fwd(q, k, v, g, beta, g_row, bsz, n):
    def tok(b, p, c):
        return (b * n + c, p)

    def body(q_ref, k_ref, v_ref, g_ref, b_ref, gr_ref, o_ref, sin_ref, s_scr):
        _zero_at_first([s_scr], pl.program_id(2) == 0)
        sa, sb = s_scr[0], s_scr[1]
        sin_ref[0] = sa
        sin_ref[1] = sb
        oa, ob, sa2, sb2 = _gdn_pair(*_halves(q_ref), *_halves(k_ref), *_halves(v_ref), *_halves(g_ref), *_halves(b_ref),
                                     gr_ref[0], sa, sb)
        o_ref[:, :LANE] = oa
        o_ref[:, LANE:] = ob
        s_scr[0] = sa2
        s_scr[1] = sb2

    tspec = pl.BlockSpec((CHUNK, 2 * LANE), tok)
    return _pcall("gdn_fwd", body, (bsz, _GP, n),
                  [tspec] * 5 + [pl.BlockSpec((1, 1, LANE), lambda b, p, c: ((b * n + c) * _GP + p, 0, 0))],
                  [tspec, pl.BlockSpec((2, LANE, LANE), lambda b, p, c: ((b * n + c) * _GP + p, 0, 0))],
                  [_sds((bsz * n * CHUNK, GDN_V)), _sds((bsz * n * GDN_HEADS, LANE, LANE))],
                  scratch=[pltpu.VMEM((2, LANE, LANE), F32)], sem=("parallel", "parallel", "arbitrary"))(q, k, v, g, beta, g_row)


def _gdn_bwd(q, k, v, g, beta, g_row, s_in, d_o, bsz, n):
    def tok(b, p, c):
        return (b * n + (n - 1 - c), p)

    def blk(b, p, c):
        return ((b * n + (n - 1 - c)) * _GP + p, 0, 0)

    def body(q_ref, k_ref, v_ref, g_ref, b_ref, gr_ref, sin_ref, do_ref, dq_ref, dk_ref, dv_ref, dg_ref, db_ref, dgr_ref, ds_scr):
        _zero_at_first([ds_scr], pl.program_id(2) == 0)
        args = (*_halves(q_ref), *_halves(k_ref), *_halves(v_ref), *_halves(g_ref), *_halves(b_ref), gr_ref[0],
                sin_ref[0], sin_ref[1])
        _, vjp = jax.vjp(_gdn_pair, *args)
        doa, dob = _halves(do_ref)
        cts = vjp((doa, dob, ds_scr[0], ds_scr[1]))
        for ref, i in ((dq_ref, 0), (dk_ref, 2), (dv_ref, 4), (dg_ref, 6), (db_ref, 8)):
            ref[:, :LANE] = cts[i]
            ref[:, LANE:] = cts[i + 1]
        dgr_ref[0] = cts[10]
        ds_scr[0] = cts[11]
        ds_scr[1] = cts[12]

    tspec = pl.BlockSpec((CHUNK, 2 * LANE), tok)
    rspec = pl.BlockSpec((1, 1, LANE), blk)
    tok_shape = _sds((bsz * n * CHUNK, GDN_V))
    return _pcall("gdn_bwd", body, (bsz, _GP, n),
                  [tspec] * 5 + [rspec, pl.BlockSpec((2, LANE, LANE), blk), tspec], [tspec] * 5 + [rspec],
                  [tok_shape] * 5 + [_sds((bsz * n * _GP, 1, LANE))],
                  scratch=[pltpu.VMEM((2, LANE, LANE), F32)], sem=("parallel", "parallel", "arbitrary"))(
                      q, k, v, g, beta, g_row, s_in, d_o)


_B_OFF = SSD_D // LANE
_C_OFF = (SSD_D + SSD_BC) // LANE
_PPG = _SP // SSD_GROUPS


def _ssd_fwd(xbc, dt, adt, adt_row, bsz, n):
    def tok(b, p, c):
        return (b * n + c, p)

    def body(x_ref, dt_ref, adt_ref, ar_ref, b_ref, c_ref, y_ref, sin_ref, s_scr):
        _zero_at_first([s_scr], pl.program_id(2) == 0)
        st = s_scr[...]
        sin_ref[0] = st
        y, st2 = _ssd_pair(x_ref[...], dt_ref[...], adt_ref[...], ar_ref[0], b_ref[...], c_ref[...], st)
        y_ref[...] = y
        s_scr[...] = st2

    tspec = pl.BlockSpec((CHUNK, LANE), tok)
    return _pcall("ssd_fwd", body, (bsz, _SP, n),
                  [tspec, tspec, tspec, pl.BlockSpec((1, 1, LANE), lambda b, p, c: ((b * n + c) * _SP + p, 0, 0)),
                   pl.BlockSpec((CHUNK, LANE), lambda b, p, c: (b * n + c, _B_OFF + p // _PPG)),
                   pl.BlockSpec((CHUNK, LANE), lambda b, p, c: (b * n + c, _C_OFF + p // _PPG))],
                  [tspec, pl.BlockSpec((1, LANE, LANE), lambda b, p, c: ((b * n + c) * _SP + p, 0, 0))],
                  [_sds((bsz * n * CHUNK, SSD_D)), _sds((bsz * n * _SP, LANE, LANE))],
                  scratch=[pltpu.VMEM((LANE, LANE), F32)], sem=("parallel", "parallel", "arbitrary"))(
                      xbc, dt, adt, adt_row, xbc, xbc)


def _ssd_bwd(xbc, dt, adt, adt_row, s_in, d_y, bsz, n):
    def tok(b, p, c):
        return (b * n + (n - 1 - c), p)

    def blk(b, p, c):
        return ((b * n + (n - 1 - c)) * _SP + p, 0, 0)

    nrow = bsz * n

    def body(x_ref, dt_ref, adt_ref, ar_ref, b_ref, c_ref, sin_ref, dy_ref, dx_ref, ddt_ref, dadt_ref, dar_ref, db_ref, dc_ref, ds_scr):
        _zero_at_first([ds_scr], pl.program_id(2) == 0)
        _, vjp = jax.vjp(_ssd_pair, x_ref[...], dt_ref[...], adt_ref[...], ar_ref[0], b_ref[...], c_ref[...], sin_ref[0])
        cts = vjp((dy_ref[...], ds_scr[...]))
        dx_ref[...] = cts[0]
        ddt_ref[...] = cts[1]
        dadt_ref[...] = cts[2]
        dar_ref[0] = cts[3]
        db_ref[...] = cts[4]
        dc_ref[...] = cts[5]
        ds_scr[...] = cts[6]

    tspec = pl.BlockSpec((CHUNK, LANE), tok)
    rspec = pl.BlockSpec((1, 1, LANE), blk)
    pspec = pl.BlockSpec((CHUNK, LANE), lambda b, p, c: (p * nrow + b * n + (n - 1 - c), 0))
    tok_shape = _sds((bsz * n * CHUNK, SSD_D))
    per_pair = _sds((_SP * bsz * n * CHUNK, LANE))
    return _pcall("ssd_bwd", body, (bsz, _SP, n),
                  [tspec, tspec, tspec, rspec,
                   pl.BlockSpec((CHUNK, LANE), lambda b, p, c: (b * n + (n - 1 - c), _B_OFF + p // _PPG)),
                   pl.BlockSpec((CHUNK, LANE), lambda b, p, c: (b * n + (n - 1 - c), _C_OFF + p // _PPG)),
                   pl.BlockSpec((1, LANE, LANE), blk), tspec],
                  [tspec, tspec, tspec, rspec, pspec, pspec],
                  [tok_shape] * 3 + [_sds((bsz * n * _SP, 1, LANE)), per_pair, per_pair],
                  scratch=[pltpu.VMEM((LANE, LANE), F32)], sem=("parallel", "parallel", "arbitrary"))(
                      xbc, dt, adt, adt_row, xbc, xbc, s_in, d_y)


def _sum_bc(parts, d_xs, bsz, s):
    t = bsz * s
    tm = _row_tile(t)
    db, dc = parts

    def body(dx_ref, db_ref, dc_ref, o_ref):
        o_ref[:, :SSD_D] = dx_ref[...]
        for gi in range(SSD_GROUPS):
            accb = db_ref[gi * _PPG]
            accc = dc_ref[gi * _PPG]
            for j in range(1, _PPG):
                accb = accb + db_ref[gi * _PPG + j]
                accc = accc + dc_ref[gi * _PPG + j]
            o_ref[:, SSD_D + gi * SSD_STATE:SSD_D + (gi + 1) * SSD_STATE] = accb
            o_ref[:, SSD_D + SSD_BC + gi * SSD_STATE:SSD_D + SSD_BC + (gi + 1) * SSD_STATE] = accc

    pp = pl.BlockSpec((_SP, tm, LANE), lambda i: (0, i, 0))
    return _pcall("ssd_sum_bc", body, (t // tm,), [_row_spec(tm, SSD_D), pp, pp], _row_spec(tm, SSD_CONV_CH),
                  _sds((t, SSD_CONV_CH)), sem=("parallel",))(d_xs, db.reshape(_SP, t, LANE), dc.reshape(_SP, t, LANE))


def _add2(name, a, b):
    t, c = a.shape
    tm = _row_tile(t)

    def body(a_ref, b_ref, o_ref):
        o_ref[...] = a_ref[...] + b_ref[...]

    return _pcall(name, body, (t // tm,), [_row_spec(tm, c)] * 2, _row_spec(tm, c), _sds((t, c)), sem=("parallel",))(a, b)


def _rep(p, width):
    return jnp.repeat(p.reshape(-1), width).reshape(1, -1)


def _to_rows(narrow, heads, bsz, n):
    a = narrow[:, :heads].reshape(bsz, n, CHUNK, heads)
    a = jnp.transpose(a, (0, 1, 3, 2)).reshape(bsz * n * (heads // 2), 1, 2 * CHUNK)
    return a


def _from_rows(rows, heads, bsz, n):
    a = rows.reshape(bsz, n, heads, CHUNK)
    a = jnp.transpose(a, (0, 1, 3, 2)).reshape(bsz * n * CHUNK, heads)
    return jnp.pad(a, ((0, 0), (0, SMALL - heads)))


def _local_step(x, tgt, p):
    bsz, s, _ = x.shape
    t = bsz * s
    n = s // CHUNK
    x2 = x.reshape(t, D_MODEL)
    tgt2 = tgt.reshape(t, D_MODEL)
    w_in = p["w_in"]
    w_big = jnp.concatenate([w_in[:, :4096], w_in[:, 4112:6672]], axis=1)
    w_small = jnp.concatenate([w_in[:, 4096:4112], w_in[:, 6672:6688], jnp.zeros((D_MODEL, SMALL - 32), BF16)], axis=1)
    a_log_g, dt_bias_g = _rep(p["gdn_a_log"], GDN_DK), _rep(p["gdn_dt_bias"], GDN_DK)
    a_log_s, dt_bias_s = _rep(p["ssd_a_log"], SSD_HEADDIM), _rep(p["ssd_dt_bias"], SSD_HEADDIM)
    d_skip = _rep(p["ssd_d"], SSD_HEADDIM)

    h = _rms_fwd("rms0_fwd", x2, p["pre_mix_norm"])
    proj = _matmul("mm_in_big", h, w_big, "nn", F32)
    small = _matmul("mm_in_small", h, w_small, "nn", F32)
    beta, g, dt, adt, g_n, adt_n = _gates_fwd(small, a_log_g, dt_bias_g, a_log_s, dt_bias_s)
    g_row = _to_rows(g_n, GDN_HEADS, bsz, n)
    adt_row = _to_rows(adt_n, SSD_HEADS, bsz, n)
    q = _gdn_conv_fwd("q", proj, p["gdn_conv_w"], bsz, s)
    k = _gdn_conv_fwd("k", proj, p["gdn_conv_w"], bsz, s)
    v = _gdn_conv_fwd("v", proj, p["gdn_conv_w"], bsz, s)
    o, gdn_s = _gdn_fwd(q, k, v, g, beta, g_row, bsz, n)
    o_a = _gdn_out_fwd(o, proj, p["gdn_norm_w"])
    xbc = _ssd_conv_fwd(proj, p["ssd_conv_w"], p["ssd_conv_b"], bsz, s)
    y, ssd_s = _ssd_fwd(xbc, dt, adt, adt_row, bsz, n)
    o_s = _ssd_out_fwd(y, xbc, proj, d_skip, p["ssd_norm_w"])
    ocat = jnp.concatenate([o_a, o_s], axis=1)
    mix = _matmul("mm_out", ocat, p["w_out"], "nn", F32)
    x1, h2 = _res1_fwd(x2, mix, p["post_mix_norm"], p["pre_ffn_norm"])
    u_pre = _matmul("mm_up", h2, p["w_up"], "nn", F32)
    act = _ffn_act_fwd(u_pre, p["ffn_conv_w"], p["ffn_conv_b"], bsz, s)
    f = _matmul("mm_down", act, p["w_down"], "nn", F32)
    loss_acc, d_out, d_f, g_post_ffn = _final(x1, f, p["post_ffn_norm"], tgt2)

    grads = {"post_ffn_norm": g_post_ffn}
    d_act = _matmul("mm_down_dx", d_f, p["w_down"], "nt", F32)
    grads["w_down"] = _matmul("mm_down_dw", act, d_f, "tn", F32)
    d_gate, d_up, dwg, dwu, dbg, dbu = _ffn_act_bwd(u_pre, p["ffn_conv_w"], p["ffn_conv_b"], d_act, bsz, s)
    grads["ffn_conv_w"] = jnp.concatenate([dwg, dwu], axis=1)
    grads["ffn_conv_b"] = jnp.concatenate([dbg, dbu], axis=1)
    d_u = jnp.concatenate([d_gate, d_up], axis=1)
    d_h2 = _matmul("mm_up_dx", d_u, p["w_up"], "nt", F32)
    grads["w_up"] = _matmul("mm_up_dw", h2, d_u, "tn", F32)
    d_x1, d_mix, grads["post_mix_norm"], grads["pre_ffn_norm"] = _res1_bwd(
        x2, mix, p["post_mix_norm"], p["pre_ffn_norm"], d_out, d_h2)
    d_ocat = _matmul("mm_out_dx", d_mix, p["w_out"], "nt", F32)
    grads["w_out"] = _matmul("mm_out_dw", ocat, d_mix, "tn", F32)

    d_o, d_za, grads["gdn_norm_w"] = _gdn_out_bwd(o, proj, p["gdn_norm_w"], d_ocat)
    d_q, d_k, d_v, d_g, d_beta, d_g_row = _gdn_bwd(q, k, v, g, beta, g_row, gdn_s, d_o, bsz, n)
    d_qp, dwq = _gdn_conv_bwd("q", proj, p["gdn_conv_w"], d_q, bsz, s)
    d_kp, dwk = _gdn_conv_bwd("k", proj, p["gdn_conv_w"], d_k, bsz, s)
    d_vp, dwv = _gdn_conv_bwd("v", proj, p["gdn_conv_w"], d_v, bsz, s)
    grads["gdn_conv_w"] = jnp.concatenate([dwq, dwk, dwv], axis=1)

    d_y, d_xs_skip, d_zs, d_dskip, grads["ssd_norm_w"] = _ssd_out_bwd(y, xbc, proj, d_skip, p["ssd_norm_w"], d_ocat)
    d_xs, d_dt, d_adt, d_adt_row, d_bp, d_cp = _ssd_bwd(xbc, dt, adt, adt_row, ssd_s, d_y, bsz, n)
    d_xs = _add2("ssd_add_dx", d_xs, d_xs_skip)
    d_xbc = _sum_bc((d_bp, d_cp), d_xs, bsz, s)
    d_xbcp, grads["ssd_conv_w"], grads["ssd_conv_b"] = _ssd_conv_bwd(proj, p["ssd_conv_w"], p["ssd_conv_b"], d_xbc, bsz, s)

    d_small, da_log_g, ddt_bias_g, da_log_s, ddt_bias_s = _gates_bwd(
        small, a_log_g, dt_bias_g, a_log_s, dt_bias_s,
        (d_beta, d_g, d_dt, d_adt, _from_rows(d_g_row, GDN_HEADS, bsz, n), _from_rows(d_adt_row, SSD_HEADS, bsz, n)))
    d_proj = jnp.concatenate([d_qp, d_kp, d_vp, d_za, d_zs, d_xbcp], axis=1)
    d_h = _add2("add_dh", _matmul("mm_in_big_dx", d_proj, w_big, "nt", F32),
                _matmul("mm_in_small_dx", d_small, w_small, "nt", F32))
    dw_big = _matmul("mm_in_big_dw", h, d_proj, "tn", F32)
    dw_small = _matmul("mm_in_small_dw", h, d_small, "tn", F32)
    grads["w_in"] = jnp.concatenate([dw_big[:, :4096], dw_small[:, :16], dw_big[:, 4096:], dw_small[:, 16:32]], axis=1)
    grad_x, grads["pre_mix_norm"] = _rms1_bwd(x2, p["pre_mix_norm"], d_h, d_x1)
    wide = {"gdn_a_log": (da_log_g, GDN_DK), "gdn_dt_bias": (ddt_bias_g, GDN_DK), "ssd_a_log": (da_log_s, SSD_HEADDIM),
            "ssd_dt_bias": (ddt_bias_s, SSD_HEADDIM), "ssd_d": (d_dskip, SSD_HEADDIM)}
    return loss_acc, grad_x.reshape(bsz, s, D_MODEL), grads, wide


def _group_sums(wide):
    def body(x_ref, o128_ref, o64_ref):
        r, c = _iota2((D_MODEL, SMALL))
        xv = x_ref[...]
        o128_ref[...] = _dot(xv, ((r >> 7) == c).astype(F32), NN, True)
        o64_ref[...] = _dot(xv, ((r >> 6) == c).astype(F32), NN, True)

    spec = _full_spec((8, D_MODEL))
    ospec = _full_spec((8, SMALL))
    return _pcall("group_sums", body, (1,), [spec], [ospec, ospec], [_sds((8, SMALL)), _sds((8, SMALL))])(wide)


def _adamw_fn(w, g, m, v):
    m = ADAM_B1 * m + (1.0 - ADAM_B1) * g
    v = ADAM_B2 * v + (1.0 - ADAM_B2) * (g * g)
    m_hat = m / (1.0 - ADAM_B1 ** ADAM_STEP)
    v_hat = v / (1.0 - ADAM_B2 ** ADAM_STEP)
    delta = -ADAM_LR * (m_hat / (jnp.sqrt(v_hat) + ADAM_EPS) + ADAM_WD * w)
    return delta, m, v


def _adamw(name, w, g, m, v):
    r, c = w.shape
    tr = _pick(r, (256, 176, 128, 64, 8))

    def body(w_ref, g_ref, m_ref, v_ref, d_ref, m2_ref, v2_ref):
        d, m2, v2 = _adamw_fn(w_ref[...], g_ref[...], m_ref[...], v_ref[...])
        d_ref[...] = d
        m2_ref[...] = m2
        v2_ref[...] = v2

    spec = pl.BlockSpec((tr, c), lambda i: (i, 0))
    return _pcall(name, body, (r // tr,), [spec] * 4, [spec] * 3, [_sds((r, c))] * 3, sem=("parallel",))(w, g, m, v)


_ANY = pl.BlockSpec(memory_space=pl.ANY)
_OTHER_CHIPS = ((1, 0), (0, 1), (1, 1))


def _coords():
    return lax.axis_index("x"), lax.axis_index("y"), lax.axis_index("c")


def _flip(v, f):
    return 1 - v if f else v


def _gather_chips(arrs):
    n = len(arrs)

    def body(*refs):
        ins, outs = refs[:n], refs[n:2 * n]
        send_sems, recv_sems, loc_sems = refs[2 * n:]
        x, y, c = _coords()
        me = 2 * x + y
        started = []
        for a in range(n):
            loc = pltpu.make_async_copy(ins[a], outs[a].at[me], loc_sems.at[a])
            loc.start()
            started.append(loc)
        sends = []
        for a in range(n):
            for j, (fx, fy) in enumerate(_OTHER_CHIPS):
                cp = pltpu.make_async_remote_copy(ins[a], outs[a].at[me], send_sems.at[a * 3 + j], recv_sems.at[a * 3 + j],
                                                  device_id=(_flip(x, fx), _flip(y, fy), c), device_id_type=MESH)
                cp.start()
                sends.append(cp)
        for a in range(n):
            for j, (fx, fy) in enumerate(_OTHER_CHIPS):
                src = 2 * _flip(x, fx) + _flip(y, fy)
                pltpu.make_async_remote_copy(ins[a], outs[a].at[src], send_sems.at[a * 3 + j], recv_sems.at[a * 3 + j],
                                             device_id=(_flip(x, fx), _flip(y, fy), c), device_id_type=MESH).wait_recv()
        for cp in sends:
            cp.wait_send()
        for loc in started:
            loc.wait()

    return pl.pallas_call(
        body, name="gather_chips", out_shape=[_sds((N_CHIPS,) + a.shape, a.dtype) for a in arrs],
        in_specs=[_ANY] * n, out_specs=[_ANY] * n,
        scratch_shapes=[pltpu.SemaphoreType.DMA((3 * n,)), pltpu.SemaphoreType.DMA((3 * n,)), pltpu.SemaphoreType.DMA((n,))],
        compiler_params=pltpu.CompilerParams(has_side_effects=True))(*arrs)


_PEERS = tuple((fx, fy, fc) for fx in (0, 1) for fy in (0, 1) for fc in (0, 1))[1:]


def _allreduce_small(x):
    r = x.shape[0]

    def body(x_ref, o_ref, buf, send_sems, recv_sems):
        cx, cy, cc = _coords()
        me = 4 * cx + 2 * cy + cc
        sends = []
        for j, (fx, fy, fc) in enumerate(_PEERS):
            cp = pltpu.make_async_remote_copy(x_ref, buf.at[me], send_sems.at[j], recv_sems.at[j],
                                              device_id=(_flip(cx, fx), _flip(cy, fy), _flip(cc, fc)), device_id_type=MESH)
            cp.start()
            sends.append(cp)
        buf[pl.ds(me, 1)] = x_ref[...][None]
        for j, (fx, fy, fc) in enumerate(_PEERS):
            src = 4 * _flip(cx, fx) + 2 * _flip(cy, fy) + _flip(cc, fc)
            pltpu.make_async_remote_copy(x_ref, buf.at[src], send_sems.at[j], recv_sems.at[j],
                                         device_id=(_flip(cx, fx), _flip(cy, fy), _flip(cc, fc)), device_id_type=MESH).wait_recv()
        for cp in sends:
            cp.wait_send()
        acc = buf[0]
        for d in range(1, N_DEV):
            acc = acc + buf[d]
        o_ref[...] = acc

    vm = pl.BlockSpec(memory_space=pltpu.VMEM)
    return pl.pallas_call(
        body, name="allreduce_small", out_shape=_sds((r, LANE)), in_specs=[vm], out_specs=vm,
        scratch_shapes=[pltpu.VMEM((N_DEV, r, LANE), F32), pltpu.SemaphoreType.DMA((7,)), pltpu.SemaphoreType.DMA((7,))],
        compiler_params=pltpu.CompilerParams(has_side_effects=True, vmem_limit_bytes=VMEM_LIMIT))(x)


def _pair_exchange(name, arrs, send_other_half):
    n = len(arrs)

    def body(*refs):
        ins, outs = refs[:n], refs[n:2 * n]
        send_sems, recv_sems = refs[2 * n:2 * n + 2]
        loc_sems = None if send_other_half else refs[2 * n + 2]
        x, y, c = _coords()
        sib = (x, y, 1 - c)
        local_copies, sends = [], []
        for a in range(n):
            if send_other_half:
                half = ins[a].shape[1] // 2
                src = ins[a].at[:, pl.ds((1 - c) * half, half), :]
                dst = outs[a]
            else:
                src = ins[a]
                dst = outs[a].at[c]
                loc = pltpu.make_async_copy(ins[a], outs[a].at[c], loc_sems.at[a])
                loc.start()
                local_copies.append(loc)
            cp = pltpu.make_async_remote_copy(src, dst, send_sems.at[a], recv_sems.at[a], device_id=sib, device_id_type=MESH)
            cp.start()
            sends.append(cp)
        for a in range(n):
            if send_other_half:
                half = ins[a].shape[1] // 2
                src, dst = ins[a].at[:, pl.ds((1 - c) * half, half), :], outs[a]
            else:
                src, dst = ins[a], outs[a].at[1 - c]
            pltpu.make_async_remote_copy(src, dst, send_sems.at[a], recv_sems.at[a], device_id=sib, device_id_type=MESH).wait_recv()
        for cp in sends:
            cp.wait_send()
        for loc in local_copies:
            loc.wait()

    if send_other_half:
        out_shape = [_sds((a.shape[0], a.shape[1] // 2, a.shape[2]), a.dtype) for a in arrs]
        scratch = [pltpu.SemaphoreType.DMA((n,)), pltpu.SemaphoreType.DMA((n,))]
    else:
        out_shape = [_sds((2,) + a.shape, a.dtype) for a in arrs]
        scratch = [pltpu.SemaphoreType.DMA((n,)), pltpu.SemaphoreType.DMA((n,)), pltpu.SemaphoreType.DMA((n,))]
    return pl.pallas_call(
        body, name=name, out_shape=out_shape, in_specs=[_ANY] * n, out_specs=[_ANY] * n, scratch_shapes=scratch,
        compiler_params=pltpu.CompilerParams(has_side_effects=True))(*arrs)


def _scatter_chips(arrs):
    n = len(arrs)

    def body(*refs):
        ins, outs = refs[:n], refs[n:2 * n]
        send_sems, recv_sems, loc_sems = refs[2 * n:]
        x, y, c = _coords()
        me = 2 * x + y
        local_copies, sends = [], []
        for a in range(n):
            loc = pltpu.make_async_copy(ins[a].at[me], outs[a].at[me], loc_sems.at[a])
            loc.start()
            local_copies.append(loc)
        for a in range(n):
            for j, (fx, fy) in enumerate(_OTHER_CHIPS):
                to = 2 * _flip(x, fx) + _flip(y, fy)
                cp = pltpu.make_async_remote_copy(ins[a].at[to], outs[a].at[me], send_sems.at[a * 3 + j], recv_sems.at[a * 3 + j],
                                                  device_id=(_flip(x, fx), _flip(y, fy), c), device_id_type=MESH)
                cp.start()
                sends.append(cp)
        for a in range(n):
            for j, (fx, fy) in enumerate(_OTHER_CHIPS):
                src = 2 * _flip(x, fx) + _flip(y, fy)
                pltpu.make_async_remote_copy(ins[a].at[src], outs[a].at[src], send_sems.at[a * 3 + j], recv_sems.at[a * 3 + j],
                                             device_id=(_flip(x, fx), _flip(y, fy), c), device_id_type=MESH).wait_recv()
        for cp in sends:
            cp.wait_send()
        for loc in local_copies:
            loc.wait()

    return pl.pallas_call(
        body, name="scatter_chips", out_shape=[_sds(a.shape, a.dtype) for a in arrs], in_specs=[_ANY] * n, out_specs=[_ANY] * n,
        scratch_shapes=[pltpu.SemaphoreType.DMA((3 * n,)), pltpu.SemaphoreType.DMA((3 * n,)), pltpu.SemaphoreType.DMA((n,))],
        compiler_params=pltpu.CompilerParams(has_side_effects=True))(*arrs)


def _pair_add(name, full, recv, core):
    _, r, c = full.shape
    half = r // 2
    tr = _pick(half, (256, 176, 128, 64, 8))
    nb = half // tr

    def body(c_ref, a_ref, b_ref, o_ref):
        o_ref[...] = a_ref[...] + b_ref[...]

    grid_spec = pltpu.PrefetchScalarGridSpec(
        num_scalar_prefetch=1, grid=(N_CHIPS, nb),
        in_specs=[pl.BlockSpec((1, tr, c), lambda k, i, cref: (k, cref[0] * nb + i, 0)),
                  pl.BlockSpec((1, tr, c), lambda k, i, cref: (k, i, 0))],
        out_specs=pl.BlockSpec((1, tr, c), lambda k, i, cref: (k, i, 0)))
    return pl.pallas_call(
        body, name=name, out_shape=_sds((N_CHIPS, half, c)), grid_spec=grid_spec,
        compiler_params=pltpu.CompilerParams(dimension_semantics=("parallel", "parallel"), vmem_limit_bytes=VMEM_LIMIT))(
            core, full, recv)


def _chip_sum(name, parts):
    _, r, c = parts.shape
    tr = _pick(r, (256, 176, 128, 64, 8))

    def body(p_ref, o_ref):
        o_ref[...] = ((p_ref[0] + p_ref[1]) + p_ref[2]) + p_ref[3]

    return _pcall(name, body, (r // tr,), [pl.BlockSpec((N_CHIPS, tr, c), lambda i: (0, i, 0))],
                  pl.BlockSpec((tr, c), lambda i: (i, 0)), _sds((r, c)), sem=("parallel",))(parts)


_WEIGHTS = ("pre_mix_norm", "w_in", "gdn_conv_w", "gdn_a_log", "gdn_dt_bias", "gdn_norm_w", "ssd_conv_w", "ssd_conv_b",
            "ssd_a_log", "ssd_dt_bias", "ssd_d", "ssd_norm_w", "w_out", "post_mix_norm", "pre_ffn_norm", "w_up",
            "ffn_conv_w", "ffn_conv_b", "w_down", "post_ffn_norm")
_BIG = ("w_in", "w_out", "w_up", "w_down")
_COL_SHARDED_SMALL = ("gdn_conv_w", "ssd_conv_w", "ffn_conv_w")
_SMALL = tuple(k for k in _WEIGHTS if k not in _BIG)


def _pack(arrs):
    flat = jnp.concatenate([a.reshape(-1) for a in arrs])
    rows = -(-flat.shape[0] // (8 * LANE)) * 8
    return jnp.pad(flat, (0, rows * LANE - flat.shape[0])).reshape(rows, LANE)


def _unpack(packed, shapes):
    flat = packed.reshape(-1)
    out, off = [], 0
    for shp in shapes:
        size = 1
        for d in shp:
            size *= d
        out.append(flat[off:off + size].reshape(shp))
        off += size
    return out


def _cols_to_chips(a):
    r, c4 = a.shape
    return jnp.transpose(a.reshape(r, N_CHIPS, c4 // N_CHIPS), (1, 0, 2))


def _chips_to_cols(a):
    k, r, c = a.shape
    return jnp.transpose(a, (1, 0, 2)).reshape(r, k * c)


def kernel(x, pre_mix_norm, w_in, gdn_conv_w, gdn_a_log, gdn_dt_bias, gdn_norm_w, ssd_conv_w, ssd_conv_b, ssd_a_log, ssd_dt_bias, ssd_d, ssd_norm_w, w_out, post_mix_norm, pre_ffn_norm, w_up, ffn_conv_w, ffn_conv_b, w_down, post_ffn_norm, loss_target, m_pre_mix_norm, m_w_in, m_gdn_conv_w, m_gdn_a_log, m_gdn_dt_bias, m_gdn_norm_w, m_ssd_conv_w, m_ssd_conv_b, m_ssd_a_log, m_ssd_dt_bias, m_ssd_d, m_ssd_norm_w, m_w_out, m_post_mix_norm, m_pre_ffn_norm, m_w_up, m_ffn_conv_w, m_ffn_conv_b, m_w_down, m_post_ffn_norm, v_pre_mix_norm, v_w_in, v_gdn_conv_w, v_gdn_a_log, v_gdn_dt_bias, v_gdn_norm_w, v_ssd_conv_w, v_ssd_conv_b, v_ssd_a_log, v_ssd_dt_bias, v_ssd_d, v_ssd_norm_w, v_w_out, v_post_mix_norm, v_pre_ffn_norm, v_w_up, v_ffn_conv_w, v_ffn_conv_b, v_w_down, v_post_ffn_norm):
    w = dict(zip(_WEIGHTS, (pre_mix_norm, w_in, gdn_conv_w, gdn_a_log, gdn_dt_bias, gdn_norm_w, ssd_conv_w, ssd_conv_b,
                            ssd_a_log, ssd_dt_bias, ssd_d, ssd_norm_w, w_out, post_mix_norm, pre_ffn_norm, w_up,
                            ffn_conv_w, ffn_conv_b, w_down, post_ffn_norm)))
    m = dict(zip(_WEIGHTS, (m_pre_mix_norm, m_w_in, m_gdn_conv_w, m_gdn_a_log, m_gdn_dt_bias, m_gdn_norm_w, m_ssd_conv_w,
                            m_ssd_conv_b, m_ssd_a_log, m_ssd_dt_bias, m_ssd_d, m_ssd_norm_w, m_w_out, m_post_mix_norm,
                            m_pre_ffn_norm, m_w_up, m_ffn_conv_w, m_ffn_conv_b, m_w_down, m_post_ffn_norm)))
    v = dict(zip(_WEIGHTS, (v_pre_mix_norm, v_w_in, v_gdn_conv_w, v_gdn_a_log, v_gdn_dt_bias, v_gdn_norm_w, v_ssd_conv_w,
                            v_ssd_conv_b, v_ssd_a_log, v_ssd_dt_bias, v_ssd_d, v_ssd_norm_w, v_w_out, v_post_mix_norm,
                            v_pre_ffn_norm, v_w_up, v_ffn_conv_w, v_ffn_conv_b, v_w_down, v_post_ffn_norm)))
    cx, cy, cc = _coords()
    chip = 2 * cx + cy

    shards = [w[k][0].astype(BF16) for k in _BIG] + [w[k][0] for k in _COL_SHARDED_SMALL]
    g_in, g_out, g_up, g_down, g_gcw, g_scw, g_fcw = _gather_chips(shards)
    p = {k: w[k] for k in _SMALL if k not in _COL_SHARDED_SMALL}
    p["w_in"] = _chips_to_cols(g_in)
    p["w_up"] = _chips_to_cols(g_up)
    p["w_out"] = g_out.reshape(-1, D_MODEL)
    p["w_down"] = g_down.reshape(-1, D_MODEL)
    p["gdn_conv_w"] = _chips_to_cols(g_gcw)
    p["ssd_conv_w"] = _chips_to_cols(g_scw)
    p["ffn_conv_w"] = _chips_to_cols(g_fcw)

    loss_acc, grad_x, grads, wide = _local_step(x, loss_target, p)
    order = list(wide)
    stacked = jnp.concatenate([wide[k][0] for k in order] + [jnp.zeros((8 - len(order), D_MODEL), F32)], axis=0)
    s128, s64 = _group_sums(stacked)
    for i, k in enumerate(order):
        heads = w[k].shape[1]
        grads[k] = (s128 if wide[k][1] == GDN_DK else s64)[i:i + 1, :heads]
    loss = lax.psum(loss_acc[0, 0], ("x", "y", "c"))

    small_full_shapes = [grads[k].shape for k in _SMALL]
    summed = _unpack(_allreduce_small(_pack([grads[k] for k in _SMALL])), small_full_shapes)
    g_small = dict(zip(_SMALL, summed))
    for k in _COL_SHARDED_SMALL:
        width = w[k].shape[2]
        g_small[k] = lax.dynamic_slice_in_dim(g_small[k], chip * width, width, axis=1)

    big = [_cols_to_chips(grads["w_in"]), grads["w_out"].reshape(N_CHIPS, -1, D_MODEL),
           _cols_to_chips(grads["w_up"]), grads["w_down"].reshape(N_CHIPS, -1, D_MODEL)]
    from_sibling = _pair_exchange("pair_reduce_send", big, True)
    core = cc.astype(jnp.int32).reshape(1)
    pair_sums = [_pair_add("pair_add_" + k, a, b, core) for k, a, b in zip(_BIG, big, from_sibling)]
    landed = _scatter_chips(pair_sums)
    mine = [_chip_sum("chip_sum_" + k, a) for k, a in zip(_BIG, landed)]
    both = _pair_exchange("pair_gather", mine, False)
    g_big = {k: a.reshape(-1, a.shape[2]) for k, a in zip(_BIG, both)}

    out_g, out_d, out_m, out_v = {}, {}, {}, {}
    for k in _BIG:
        out_g[k] = g_big[k][None]
        d_, m_, v_ = _adamw("adamw_" + k, w[k][0], g_big[k], m[k][0], v[k][0])
        out_d[k], out_m[k], out_v[k] = d_[None], m_[None], v_[None]
    shapes = [w[k].shape for k in _SMALL]
    for k in _SMALL:
        out_g[k] = g_small[k].reshape(w[k].shape)
    packed = [_pack([d[k] for k in _SMALL]) for d in (w, out_g, m, v)]
    d_p, m_p, v_p = _adamw("adamw_small", *packed)
    for dst, src in ((out_d, d_p), (out_m, m_p), (out_v, v_p)):
        dst.update(zip(_SMALL, _unpack(src, shapes)))
    return (loss, grad_x, *[out_g[k] for k in _WEIGHTS], *[out_d[k] for k in _WEIGHTS],
            *[out_m[k] for k in _WEIGHTS], *[out_v[k] for k in _WEIGHTS])
```

```python
import functools

import jax
import jax.numpy as jnp
from jax import lax
from jax.experimental import pallas as pl
from jax.experimental.pallas import tpu as pltpu

F32 = jnp.float32
BF16 = jnp.bfloat16

D_MODEL = 1024
GDN_HEADS = 8
GDN_DK = 128
SSD_HEADS = 16
SSD_HEADDIM = 64
SSD_GROUPS = 2
SSD_STATE = 128
CONV_K = 4
CHUNK = 64
D_FF = 2816
FFN_CONV_K = 3
EPS = 1e-6
GDN_QK = GDN_HEADS * GDN_DK
GDN_V = GDN_QK
SSD_D = SSD_HEADS * SSD_HEADDIM
SSD_BC = SSD_GROUPS * SSD_STATE
SSD_CONV_CH = SSD_D + 2 * SSD_BC
BIG = 4 * 1024 + 1024 + SSD_CONV_CH
SMALL = 128
D_IN_PROJ = 6688
LANE = 128
PAIR = 2 * CHUNK
NEG = -1e30
VMEM_LIMIT = 56 * 1024 * 1024

ADAM_LR = 0.001
ADAM_B1 = 0.9
ADAM_B2 = 0.999
ADAM_EPS = 1e-08
ADAM_WD = 0.01
ADAM_STEP = 10

N_CHIPS = 4
N_DEV = 8
MESH = pl.DeviceIdType.MESH

NN = ((1,), (0,))
NT = ((1,), (1,))
TN = ((0,), (0,))


def _bdot(a, b, dims):
    return lax.dot_general(a.astype(BF16), b.astype(BF16), (dims, ((), ())), preferred_element_type=F32)


def _split2(a):
    hi = a.astype(BF16)
    return hi, (a - hi.astype(F32)).astype(BF16)


def _split3(a):
    hi = a.astype(BF16)
    r1 = a - hi.astype(F32)
    mid = r1.astype(BF16)
    return hi, mid, (r1 - mid.astype(F32)).astype(BF16)


def _dot(a, b, dims, passes):
    if passes == 1:
        return _bdot(a, b, dims)
    ah, al = _split2(a)
    bh, bl = _split2(b)
    return (_bdot(ah, bh, dims) + _bdot(ah, bl, dims)) + _bdot(al, bh, dims)


def _make_mm(passes):
    @jax.custom_vjp
    def nn(a, b):
        return _dot(a, b, NN, passes)

    @jax.custom_vjp
    def nt(a, b):
        return _dot(a, b, NT, passes)

    @jax.custom_vjp
    def tn(a, b):
        return _dot(a, b, TN, passes)

    nn.defvjp(lambda a, b: (nn(a, b), (a, b)), lambda r, g: (nt(g, r[1]), tn(r[0], g)))
    nt.defvjp(lambda a, b: (nt(a, b), (a, b)), lambda r, g: (nn(g, r[1]), tn(g, r[0])))
    tn.defvjp(lambda a, b: (tn(a, b), (a, b)), lambda r, g: (nt(r[1], g), nn(r[0], g)))
    return nn, nt, tn


_nn, _nt, _tn = _make_mm(1)
_nnx, _ntx, _tnx = _make_mm(3)


def _mask_dot(x, mask, dims, x_first):
    acc = None
    for piece in _split3(x):
        term = _bdot(piece, mask, dims) if x_first else _bdot(mask, piece, dims)
        acc = term if acc is None else acc + term
    return acc


@jax.custom_vjp
def _cst_left(cst, x):
    return _mask_dot(x, cst, NN, False)


_cst_left.defvjp(lambda cst, x: (_cst_left(cst, x), cst), lambda cst, g: (jnp.zeros_like(cst), _mask_dot(g, cst, TN, False)))


@jax.custom_vjp
def _cst_right(x, cst):
    return _mask_dot(x, cst, NN, True)


_cst_right.defvjp(lambda x, cst: (_cst_right(x, cst), cst), lambda cst, g: (_mask_dot(g, cst, NT, True), jnp.zeros_like(cst)))


def _lin_left(cst):
    return functools.partial(_cst_left, cst)


def _lin_right(cst):
    return lambda x: _cst_right(x, cst)


@jax.custom_vjp
def _tri_inv(a):
    ri, ci = _iota2(a.shape)
    p = jnp.where(ri == ci, 1.0, 0.0) - a
    ap = a
    for _ in range(5):
        ap = _dot(ap, ap, NN, 3)
        p = p + _dot(p, ap, NN, 3)
    return p


def _tri_inv_bwd(p, g):
    return (-_dot(p, _dot(g, p, NT, 3), TN, 3),)


_tri_inv.defvjp(lambda a: (lambda p: (p, p))(_tri_inv(a)), _tri_inv_bwd)


@jax.custom_vjp
def _top(x):
    return x[: x.shape[0] // 2]


_top.defvjp(lambda x: (_top(x), None), lambda _, g: (jnp.concatenate([g, jnp.zeros_like(g)], axis=0),))


@jax.custom_vjp
def _bot(x):
    return x[x.shape[0] // 2:]


_bot.defvjp(lambda x: (_bot(x), None), lambda _, g: (jnp.concatenate([jnp.zeros_like(g), g], axis=0),))


@jax.custom_vjp
def _vstack(a, b):
    return jnp.concatenate([a, b], axis=0)


_vstack.defvjp(lambda a, b: (_vstack(a, b), None), lambda _, g: (g[: g.shape[0] // 2], g[g.shape[0] // 2:]))


def _shift_dn_raw(x, s):
    if s == 0:
        return x
    r = pltpu.roll(x, s, axis=0)
    ri = lax.broadcasted_iota(jnp.int32, x.shape, 0)
    return jnp.where(ri >= s, r, 0.0)


def _shift_up_raw(x, s):
    if s == 0:
        return x
    n = x.shape[0]
    r = pltpu.roll(x, n - s, axis=0)
    ri = lax.broadcasted_iota(jnp.int32, x.shape, 0)
    return jnp.where(ri < n - s, r, 0.0)


@functools.partial(jax.custom_vjp, nondiff_argnums=(1,))
def _shift_dn(x, s):
    return _shift_dn_raw(x, s)


_shift_dn.defvjp(lambda x, s: (_shift_dn_raw(x, s), None), lambda s, _, g: (_shift_up_raw(g, s),))


def _conv(x, wrows):
    k_w = len(wrows)
    acc = wrows[k_w - 1] * x
    for k in range(k_w - 1):
        acc = acc + wrows[k] * _shift_dn(x, k_w - 1 - k)
    return acc


def _silu(x):
    return x * jax.nn.sigmoid(x)


def _rms(x, w):
    return x * lax.rsqrt(jnp.mean(x * x, axis=-1, keepdims=True) + EPS) * w


def _l2n(x):
    return x * lax.rsqrt(jnp.sum(x * x, axis=-1, keepdims=True) + EPS)


def _iota2(shape):
    return lax.broadcasted_iota(jnp.int32, shape, 0), lax.broadcasted_iota(jnp.int32, shape, 1)


def _gdn_pair(qa, qb, ka, kb, va, vb, gca, gcb, ba, bb, gcr, gla, glb, sa, sb):
    q = _vstack(qa, qb)
    k = _vstack(ka, kb)
    v = _vstack(va, vb)
    gc = _vstack(gca, gcb)
    beta = _vstack(ba, bb)
    glast = _vstack(jnp.broadcast_to(gla, (CHUNK, LANE)), jnp.broadcast_to(glb, (CHUNK, LANE)))
    ri, ci = _iota2((PAIR, PAIR))
    blk = ((ri >= CHUNK) & (ci >= CHUNK)) | ((ri < CHUNK) & (ci < CHUNK))
    causal = blk & (ri >= ci)
    strict = blk & (ri > ci)
    decay = jnp.exp(jnp.where(causal, gc - jnp.broadcast_to(gcr, (PAIR, PAIR)), NEG))
    eg = jnp.exp(gc)
    kbeta = k * beta
    p = _tri_inv(jnp.where(strict, _nt(kbeta, k) * decay, 0.0))
    u = _nnx(p, v * beta)
    w = _nnx(p, kbeta * eg)
    qk = _nt(q, k) * decay
    q_dec = q * eg
    k_dec = k * jnp.exp(glast - gc)
    v_new = u - _vstack(_nn(_top(w), sa), _nn(_bot(w), sb))
    o = _vstack(_nn(_top(q_dec), sa), _nn(_bot(q_dec), sb)) + _nn(qk, v_new)
    gl = jnp.exp(glast)
    sa2 = sa * _vstack(_top(gl), _top(gl)) + _tn(_top(k_dec), _top(v_new))
    sb2 = sb * _vstack(_bot(gl), _bot(gl)) + _tn(_bot(k_dec), _bot(v_new))
    return _top(o), _bot(o), sa2, sb2


def _ssd_pair(x, dt, acs, acs_r, al, bg, cg, st):
    ri, ci = _iota2((CHUNK, PAIR))
    causal = ri >= jnp.where(ci >= CHUNK, ci - CHUNK, ci)
    xdt = x * dt
    alast = jnp.broadcast_to(al, (CHUNK, PAIR))
    lmat = jnp.exp(jnp.where(causal, acs - jnp.broadcast_to(acs_r, (CHUNK, PAIR)), NEG))
    cb2 = _nt(cg, _vstack(bg, bg))
    xblk = _vstack(jnp.where(ci < CHUNK, xdt, 0.0), jnp.where(ci >= CHUNK, xdt, 0.0))
    y = _nn(cb2 * lmat, xblk) + _nn(cg, st) * jnp.exp(acs)
    el = jnp.exp(alast)
    st2 = st * _vstack(el, el) + _tn(bg, xdt * jnp.exp(alast - acs))
    return y, st2


def _pcall(name, body, grid, in_specs, out_specs, out_shape, scratch=(), sem=None, aliases=None):
    if sem is None:
        sem = ("arbitrary",) * len(grid)
    return pl.pallas_call(
        functools.partial(body),
        out_shape=out_shape,
        grid=grid,
        in_specs=in_specs,
        out_specs=out_specs,
        scratch_shapes=scratch,
        input_output_aliases=aliases or {},
        name=name,
        compiler_params=pltpu.CompilerParams(dimension_semantics=sem, vmem_limit_bytes=VMEM_LIMIT),
    )


def _sds(shape, dtype=F32):
    return jax.ShapeDtypeStruct(shape, dtype)


def _row_spec(tm, width, colblock=0):
    return pl.BlockSpec((tm, width), lambda i, _c=colblock: (i, _c))


def _full_spec(shape):
    nd = len(shape)
    return pl.BlockSpec(shape, lambda *_: (0,) * nd)


def _zero_at_first(refs, first):
    @pl.when(first)
    def _():
        for r in refs:
            r[...] = jnp.zeros(r.shape, r.dtype)


def _pick(n, prefs):
    for p in prefs:
        if n % p == 0:
            return p
    return n


def _matmul(name, a, b, mode, out_dtype):
    if mode == "tn":
        r, m = a.shape
        n = b.shape[1]
        tm = _pick(m, (1024, 1408))
        tn = _pick(n, (512, 256, 128))
        tk = _pick(r, (1024, 512, 256, 128, 64))

        def body(a_ref, b_ref, o_ref):
            _zero_at_first([o_ref], pl.program_id(2) == 0)
            o_ref[...] += _dot(a_ref[...], b_ref[...], TN, False)

        return _pcall(
            name, body, (m // tm, n // tn, r // tk),
            [pl.BlockSpec((tk, tm), lambda i, j, k: (k, i)), pl.BlockSpec((tk, tn), lambda i, j, k: (k, j))],
            pl.BlockSpec((tm, tn), lambda i, j, k: (i, j)), _sds((m, n), out_dtype),
            sem=("parallel", "parallel", "arbitrary"))(a, b)
    m, k = a.shape
    n = b.shape[1] if mode == "nn" else b.shape[0]
    tm = _pick(m, (512, 256, 128, 64))
    tn = _pick(n, (512, 256, 128))
    dims = NN if mode == "nn" else NT

    def body(a_ref, b_ref, o_ref):
        o_ref[...] = _dot(a_ref[...], b_ref[...], dims, False).astype(o_ref.dtype)

    b_spec = pl.BlockSpec((k, tn), lambda i, j: (0, j)) if mode == "nn" else pl.BlockSpec((tn, k), lambda i, j: (j, 0))
    return _pcall(
        name, body, (m // tm, n // tn), [pl.BlockSpec((tm, k), lambda i, j: (i, 0)), b_spec],
        pl.BlockSpec((tm, tn), lambda i, j: (i, j)), _sds((m, n), out_dtype), sem=("parallel", "parallel"))(a, b)


def _row_tile(t):
    return _pick(t, (256, 128, 64))


def _rms_fwd(name, x, g):
    t = x.shape[0]
    tm = _row_tile(t)

    def body(x_ref, g_ref, h_ref):
        h_ref[...] = _rms(x_ref[...], g_ref[...]).astype(BF16)

    return _pcall(name, body, (t // tm,), [_row_spec(tm, D_MODEL), _full_spec((1, D_MODEL))], _row_spec(tm, D_MODEL),
                  _sds((t, D_MODEL), BF16), sem=("parallel",))(x, g)


_G_LO, _G_HI = GDN_HEADS, 2 * GDN_HEADS
_S_LO, _S_HI = 2 * GDN_HEADS, 2 * GDN_HEADS + SSD_HEADS


def _gates_fn(small, bias, a_log):
    tm = small.shape[0]
    r, c = _iota2((SMALL, D_MODEL))
    e_b = (r == (c >> 7)).astype(F32)
    e_a = (r == _G_LO + (c >> 7)).astype(F32)
    e_dt = (r == _S_LO + (c >> 6)).astype(F32)
    rr, cc = _iota2((tm, tm))
    in_chunk_tril = (((rr >> 6) == (cc >> 6)) & (rr >= cc)).astype(F32)
    sig = jax.nn.sigmoid(small)
    sp = jax.nn.softplus(small + bias)
    cum = _lin_left(in_chunk_tril)(-jnp.exp(a_log) * sp)
    return _lin_right(e_b)(sig), _lin_right(e_a)(cum), _lin_right(e_dt)(sp), _lin_right(e_dt)(cum), cum


def _gates_fwd(small, bias, a_log):
    t = small.shape[0]
    tm = _row_tile(t)

    def body(s_ref, p0, p1, *outs):
        for o, v in zip(outs, _gates_fn(s_ref[...], p0[...], p1[...])):
            o[...] = v

    pspec = _full_spec((1, SMALL))
    return _pcall("gates_fwd", body, (t // tm,), [_row_spec(tm, SMALL), pspec, pspec],
                  [_row_spec(tm, D_MODEL)] * 4 + [_row_spec(tm, SMALL)], [_sds((t, D_MODEL))] * 4 + [_sds((t, SMALL))],
                  sem=("parallel",))(small, bias, a_log)


def _gates_bwd(small, bias, a_log, cts):
    t = small.shape[0]
    tm = _row_tile(t)

    def body(s_ref, p0, p1, c0, c1, c2, c3, c4, ds_ref, db_ref, da_ref):
        _zero_at_first([db_ref, da_ref], pl.program_id(0) == 0)
        _, vjp = jax.vjp(_gates_fn, s_ref[...], p0[...], p1[...])
        d_s, d_b, d_a = vjp((c0[...], c1[...], c2[...], c3[...], c4[...]))
        ds_ref[...] = d_s.astype(BF16)
        db_ref[...] += d_b
        da_ref[...] += d_a

    pspec = _full_spec((1, SMALL))
    return _pcall("gates_bwd", body, (t // tm,),
                  [_row_spec(tm, SMALL), pspec, pspec] + [_row_spec(tm, D_MODEL)] * 4 + [_row_spec(tm, SMALL)],
                  [_row_spec(tm, SMALL), pspec, pspec], [_sds((t, SMALL), BF16), _sds((1, SMALL)), _sds((1, SMALL))])(
                      small, bias, a_log, *cts)


def _gdn_out_fn(o, z, w):
    return _rms(o, w) * _silu(z)


def _gdn_out_fwd(o, proj, gn):
    t = o.shape[0]
    tm = _row_tile(t)

    def body(o_ref, z_ref, w_ref, y_ref):
        for h in range(GDN_HEADS):
            sl = slice(h * GDN_DK, (h + 1) * GDN_DK)
            y_ref[:, sl] = _gdn_out_fn(o_ref[:, sl], z_ref[:, sl], w_ref[...]).astype(BF16)

    return _pcall("gdn_out_fwd", body, (t // tm,), [_row_spec(tm, GDN_V), _row_spec(tm, GDN_V, 3), _full_spec((1, GDN_DK))],
                  _row_spec(tm, GDN_V), _sds((t, GDN_V), BF16), sem=("parallel",))(o, proj, gn)


def _gdn_out_bwd(o, proj, gn, d_ocat):
    t = o.shape[0]
    tm = _row_tile(t)

    def body(o_ref, z_ref, w_ref, dy_ref, do_ref, dz_ref, dw_ref):
        _zero_at_first([dw_ref], pl.program_id(0) == 0)
        for h in range(GDN_HEADS):
            sl = slice(h * GDN_DK, (h + 1) * GDN_DK)
            _, vjp = jax.vjp(_gdn_out_fn, o_ref[:, sl], z_ref[:, sl], w_ref[...])
            d_o, d_z, d_w = vjp(dy_ref[:, sl])
            do_ref[:, sl] = d_o
            dz_ref[:, sl] = d_z.astype(BF16)
            dw_ref[...] += d_w

    return _pcall("gdn_out_bwd", body, (t // tm,),
                  [_row_spec(tm, GDN_V), _row_spec(tm, GDN_V, 3), _full_spec((1, GDN_DK)), _row_spec(tm, GDN_V, 0)],
                  [_row_spec(tm, GDN_V), _row_spec(tm, GDN_V), _full_spec((1, GDN_DK))],
                  [_sds((t, GDN_V)), _sds((t, GDN_V), BF16), _sds((1, GDN_DK))])(o, proj, gn, d_ocat)


def _ssd_out_fn(y, xs, z, d_skip, w):
    return _rms((y + d_skip * xs) * _silu(z), w)


_SSD_GW = SSD_D // SSD_GROUPS


def _ssd_out_fwd(y, xbc, proj, d_skip, nw):
    t = y.shape[0]
    tm = _row_tile(t)

    def body(y_ref, x_ref, z_ref, d_ref, w_ref, o_ref):
        for gi in range(SSD_GROUPS):
            sl = slice(gi * _SSD_GW, (gi + 1) * _SSD_GW)
            o_ref[:, sl] = _ssd_out_fn(y_ref[:, sl], x_ref[:, sl], z_ref[:, sl], d_ref[:, sl], w_ref[:, sl]).astype(BF16)

    pspec = _full_spec((1, SSD_D))
    return _pcall("ssd_out_fwd", body, (t // tm,),
                  [_row_spec(tm, SSD_D), _row_spec(tm, SSD_D, 0), _row_spec(tm, SSD_D, 4), pspec, pspec],
                  _row_spec(tm, SSD_D), _sds((t, SSD_D), BF16), sem=("parallel",))(y, xbc, proj, d_skip, nw)


def _ssd_out_bwd(y, xbc, proj, d_skip, nw, d_ocat):
    t = y.shape[0]
    tm = _row_tile(t)

    def body(y_ref, x_ref, z_ref, d_ref, w_ref, do_ref, dy_ref, dx_ref, dz_ref, dd_ref, dw_ref):
        _zero_at_first([dd_ref, dw_ref], pl.program_id(0) == 0)
        for gi in range(SSD_GROUPS):
            sl = slice(gi * _SSD_GW, (gi + 1) * _SSD_GW)
            _, vjp = jax.vjp(_ssd_out_fn, y_ref[:, sl], x_ref[:, sl], z_ref[:, sl], d_ref[:, sl], w_ref[:, sl])
            d_y, d_x, d_z, d_d, d_w = vjp(do_ref[:, sl])
            dy_ref[:, sl] = d_y
            dx_ref[:, sl] = d_x
            dz_ref[:, sl] = d_z.astype(BF16)
            dd_ref[:, sl] += d_d
            dw_ref[:, sl] += d_w

    pspec = _full_spec((1, SSD_D))
    row = _row_spec(tm, SSD_D)
    return _pcall("ssd_out_bwd", body, (t // tm,),
                  [row, _row_spec(tm, SSD_D, 0), _row_spec(tm, SSD_D, 4), pspec, pspec, _row_spec(tm, SSD_D, 1)],
                  [row, row, row, pspec, pspec],
                  [_sds((t, SSD_D)), _sds((t, SSD_D)), _sds((t, SSD_D), BF16), _sds((1, SSD_D)), _sds((1, SSD_D))])(
                      y, xbc, proj, d_skip, nw, d_ocat)


def _res1_fn(x, mix, g_pm, g_pf):
    x1 = x + _rms(mix, g_pm)
    return x1, _rms(x1, g_pf)


def _res1_fwd(x, mix, g_pm, g_pf):
    t = x.shape[0]
    tm = _row_tile(t)

    def body(x_ref, m_ref, a_ref, b_ref, x1_ref, h2_ref):
        x1, h2 = _res1_fn(x_ref[...], m_ref[...], a_ref[...], b_ref[...])
        x1_ref[...] = x1
        h2_ref[...] = h2.astype(BF16)

    row, pspec = _row_spec(tm, D_MODEL), _full_spec((1, D_MODEL))
    return _pcall("res1_fwd", body, (t // tm,), [row, row, pspec, pspec], [row, row],
                  [_sds((t, D_MODEL)), _sds((t, D_MODEL), BF16)], sem=("parallel",))(x, mix, g_pm, g_pf)


def _res1_bwd(x, mix, g_pm, g_pf, d_x1, d_h2):
    t = x.shape[0]
    tm = _row_tile(t)

    def body(x_ref, m_ref, a_ref, b_ref, c1_ref, c2_ref, dx_ref, dm_ref, da_ref, db_ref):
        _zero_at_first([da_ref, db_ref], pl.program_id(0) == 0)
        _, vjp = jax.vjp(_res1_fn, x_ref[...], m_ref[...], a_ref[...], b_ref[...])
        d_x, d_m, d_a, d_b = vjp((c1_ref[...], c2_ref[...]))
        dx_ref[...] = d_x
        dm_ref[...] = d_m.astype(BF16)
        da_ref[...] += d_a
        db_ref[...] += d_b

    row, pspec = _row_spec(tm, D_MODEL), _full_spec((1, D_MODEL))
    return _pcall("res1_bwd", body, (t // tm,), [row, row, pspec, pspec, row, row], [row, row, pspec, pspec],
                  [_sds((t, D_MODEL)), _sds((t, D_MODEL), BF16), _sds((1, D_MODEL)), _sds((1, D_MODEL))])(
                      x, mix, g_pm, g_pf, d_x1, d_h2)


def _final_fn(x1, f, g_po, tgt):
    err = x1 + _rms(f, g_po) - tgt
    return 0.5 * jnp.sum(jnp.mean(err * err, axis=-1))


def _final(x1, f, g_po, tgt):
    t = x1.shape[0]
    tm = _row_tile(t)

    def body(x_ref, f_ref, g_ref, t_ref, loss_ref, dx_ref, df_ref, dg_ref):
        _zero_at_first([loss_ref, dg_ref], pl.program_id(0) == 0)
        loss, (d_x, d_f, d_g) = jax.value_and_grad(_final_fn, argnums=(0, 1, 2))(x_ref[...], f_ref[...], g_ref[...], t_ref[...])
        loss_ref[...] += jnp.broadcast_to(loss, loss_ref.shape)
        dx_ref[...] = d_x
        df_ref[...] = d_f.astype(BF16)
        dg_ref[...] += d_g

    row, pspec = _row_spec(tm, D_MODEL), _full_spec((1, D_MODEL))
    return _pcall("final", body, (t // tm,), [row, row, pspec, row], [_full_spec((8, LANE)), row, row, pspec],
                  [_sds((8, LANE)), _sds((t, D_MODEL)), _sds((t, D_MODEL), BF16), _sds((1, D_MODEL))])(x1, f, g_po, tgt)


def _rms1_bwd(x, g, d_h, d_x1):
    t = x.shape[0]
    tm = _row_tile(t)

    def body(x_ref, g_ref, dh_ref, dx1_ref, dx_ref, dg_ref):
        _zero_at_first([dg_ref], pl.program_id(0) == 0)
        _, vjp = jax.vjp(_rms, x_ref[...], g_ref[...])
        d_x, d_g = vjp(dh_ref[...])
        dx_ref[...] = d_x + dx1_ref[...]
        dg_ref[...] += d_g

    row, pspec = _row_spec(tm, D_MODEL), _full_spec((1, D_MODEL))
    return _pcall("rms1_bwd", body, (t // tm,), [row, pspec, row, row], [row, pspec],
                  [_sds((t, D_MODEL)), _sds((1, D_MODEL))])(x, g, d_h, d_x1)


def _qkv_fn(mode):
    def fn(x, *wrows):
        y = _silu(_conv(x, wrows))
        if mode == "q":
            return _l2n(y) * (GDN_DK ** -0.5)
        if mode == "k":
            return _l2n(y)
        return y
    return fn


def _seq_spec(s, tc, off):
    return pl.BlockSpec((s, tc), lambda j, b, _o=off: (b, _o + j))


def _par_spec(rows, tc, off):
    return pl.BlockSpec((rows, tc), lambda j, b, _o=off: (0, _o + j))


def _gdn_conv_fwd(mode, proj, w, bsz, s):
    off = {"q": 0, "k": GDN_HEADS, "v": 2 * GDN_HEADS}[mode]
    fn = _qkv_fn(mode)

    def body(x_ref, w_ref, y_ref):
        y_ref[...] = fn(x_ref[...], *[w_ref[k:k + 1, :] for k in range(CONV_K)])

    return _pcall("gdn_conv_fwd_" + mode, body, (GDN_HEADS, bsz),
                  [_seq_spec(s, GDN_DK, off), _par_spec(CONV_K, GDN_DK, off)], _seq_spec(s, GDN_DK, 0),
                  _sds((bsz * s, GDN_QK)), sem=("parallel", "parallel"))(proj, w)


def _gdn_conv_bwd(mode, proj, w, d_y, bsz, s):
    off = {"q": 0, "k": GDN_HEADS, "v": 2 * GDN_HEADS}[mode]
    fn = _qkv_fn(mode)

    def body(x_ref, w_ref, dy_ref, dx_ref, dw_ref):
        _zero_at_first([dw_ref], pl.program_id(1) == 0)
        _, vjp = jax.vjp(fn, x_ref[...], *[w_ref[k:k + 1, :] for k in range(CONV_K)])
        grads = vjp(dy_ref[...])
        dx_ref[...] = grads[0].astype(BF16)
        for k in range(CONV_K):
            dw_ref[k:k + 1, :] += grads[1 + k]

    return _pcall("gdn_conv_bwd_" + mode, body, (GDN_HEADS, bsz),
                  [_seq_spec(s, GDN_DK, off), _par_spec(CONV_K, GDN_DK, off), _seq_spec(s, GDN_DK, 0)],
                  [_seq_spec(s, GDN_DK, 0), _par_spec(CONV_K, GDN_DK, 0)],
                  [_sds((bsz * s, GDN_QK), BF16), _sds((CONV_K, GDN_QK))], sem=("parallel", "arbitrary"))(proj, w, d_y)


def _ssd_conv_fn(x, bias, *wrows):
    return _silu(_conv(x, wrows) + bias)


_XBC_OFF = (5 * 1024) // LANE


def _ssd_conv_fwd(proj, w, bias, bsz, s):
    nt_ = SSD_CONV_CH // LANE

    def body(x_ref, w_ref, b_ref, y_ref):
        y_ref[...] = _ssd_conv_fn(x_ref[...], b_ref[...], *[w_ref[k:k + 1, :] for k in range(CONV_K)])

    return _pcall("ssd_conv_fwd", body, (nt_, bsz),
                  [_seq_spec(s, LANE, _XBC_OFF), _par_spec(CONV_K, LANE, 0), _par_spec(1, LANE, 0)], _seq_spec(s, LANE, 0),
                  _sds((bsz * s, SSD_CONV_CH)), sem=("parallel", "parallel"))(proj, w, bias)


def _ssd_conv_bwd(proj, w, bias, d_y, bsz, s):
    nt_ = SSD_CONV_CH // LANE

    def body(x_ref, w_ref, b_ref, dy_ref, dx_ref, dw_ref, db_ref):
        _zero_at_first([dw_ref, db_ref], pl.program_id(1) == 0)
        _, vjp = jax.vjp(_ssd_conv_fn, x_ref[...], b_ref[...], *[w_ref[k:k + 1, :] for k in range(CONV_K)])
        grads = vjp(dy_ref[...])
        dx_ref[...] = grads[0].astype(BF16)
        db_ref[...] += grads[1]
        for k in range(CONV_K):
            dw_ref[k:k + 1, :] += grads[2 + k]

    return _pcall("ssd_conv_bwd", body, (nt_, bsz),
                  [_seq_spec(s, LANE, _XBC_OFF), _par_spec(CONV_K, LANE, 0), _par_spec(1, LANE, 0), _seq_spec(s, LANE, 0)],
                  [_seq_spec(s, LANE, 0), _par_spec(CONV_K, LANE, 0), _par_spec(1, LANE, 0)],
                  [_sds((bsz * s, SSD_CONV_CH), BF16), _sds((CONV_K, SSD_CONV_CH)), _sds((1, SSD_CONV_CH))],
                  sem=("parallel", "arbitrary"))(proj, w, bias, d_y)


_FFN_TC = 256
_FFN_NT = D_FF // _FFN_TC


def _ffn_act_fn(xg, xu, bg, bu, *wrows):
    k_w = FFN_CONV_K
    gate = _conv(xg, wrows[:k_w]) + bg
    up = _conv(xu, wrows[k_w:]) + bu
    return _silu(gate) * up


def _ffn_act_fwd(u_pre, w, bias, bsz, s):
    def body(xg_ref, xu_ref, wg_ref, wu_ref, bg_ref, bu_ref, a_ref):
        rows = [wg_ref[k:k + 1, :] for k in range(FFN_CONV_K)] + [wu_ref[k:k + 1, :] for k in range(FFN_CONV_K)]
        a_ref[...] = _ffn_act_fn(xg_ref[...], xu_ref[...], bg_ref[...], bu_ref[...], *rows).astype(BF16)

    return _pcall("ffn_act_fwd", body, (_FFN_NT, bsz),
                  [_seq_spec(s, _FFN_TC, 0), _seq_spec(s, _FFN_TC, _FFN_NT),
                   _par_spec(FFN_CONV_K, _FFN_TC, 0), _par_spec(FFN_CONV_K, _FFN_TC, _FFN_NT),
                   _par_spec(1, _FFN_TC, 0), _par_spec(1, _FFN_TC, _FFN_NT)],
                  _seq_spec(s, _FFN_TC, 0), _sds((bsz * s, D_FF), BF16), sem=("parallel", "parallel"))(
                      u_pre, u_pre, w, w, bias, bias)


def _ffn_act_bwd(u_pre, w, bias, d_a, bsz, s):
    def body(xg_ref, xu_ref, wg_ref, wu_ref, bg_ref, bu_ref, da_ref, dg_ref, du_ref, dwg_ref, dwu_ref, dbg_ref, dbu_ref):
        _zero_at_first([dwg_ref, dwu_ref, dbg_ref, dbu_ref], pl.program_id(1) == 0)
        rows = [wg_ref[k:k + 1, :] for k in range(FFN_CONV_K)] + [wu_ref[k:k + 1, :] for k in range(FFN_CONV_K)]
        _, vjp = jax.vjp(_ffn_act_fn, xg_ref[...], xu_ref[...], bg_ref[...], bu_ref[...], *rows)
        grads = vjp(da_ref[...])
        dg_ref[...] = grads[0].astype(BF16)
        du_ref[...] = grads[1].astype(BF16)
        dbg_ref[...] += grads[2]
        dbu_ref[...] += grads[3]
        for k in range(FFN_CONV_K):
            dwg_ref[k:k + 1, :] += grads[4 + k]
            dwu_ref[k:k + 1, :] += grads[4 + FFN_CONV_K + k]

    seq0, par3, par1 = _seq_spec(s, _FFN_TC, 0), _par_spec(FFN_CONV_K, _FFN_TC, 0), _par_spec(1, _FFN_TC, 0)
    return _pcall("ffn_act_bwd", body, (_FFN_NT, bsz),
                  [seq0, _seq_spec(s, _FFN_TC, _FFN_NT), par3, _par_spec(FFN_CONV_K, _FFN_TC, _FFN_NT),
                   par1, _par_spec(1, _FFN_TC, _FFN_NT), seq0],
                  [seq0, seq0, par3, par3, par1, par1],
                  [_sds((bsz * s, D_FF), BF16), _sds((bsz * s, D_FF), BF16), _sds((FFN_CONV_K, D_FF)), _sds((FFN_CONV_K, D_FF)),
                   _sds((1, D_FF)), _sds((1, D_FF))], sem=("parallel", "arbitrary"))(u_pre, u_pre, w, w, bias, bias, d_a)


_GP = GDN_HEADS // 2
_SP = SSD_HEADS // 2


def _pair_lanes(p):
    return slice(2 * p * LANE, (2 * p + 1) * LANE), slice((2 * p + 1) * LANE, (2 * p + 2) * LANE)


_LAST = slice(CHUNK - 1, CHUNK)


def _gdn_args(p, q_ref, k_ref, v_ref, g_ref, b_ref, gr_ref):
    la, lb = _pair_lanes(p)
    return (q_ref[:, la], q_ref[:, lb], k_ref[:, la], k_ref[:, lb], v_ref[:, la], v_ref[:, lb], g_ref[:, la], g_ref[:, lb],
            b_ref[:, la], b_ref[:, lb], gr_ref[p], g_ref[_LAST, la], g_ref[_LAST, lb])


def _gdn_fwd(q, k, v, gc, beta, gc_row, bsz, n):
    def body(q_ref, k_ref, v_ref, g_ref, b_ref, gr_ref, o_ref, sin_ref, s_scr):
        _zero_at_first([s_scr], pl.program_id(1) == 0)
        for p in range(_GP):
            la, lb = _pair_lanes(p)
            sa, sb = s_scr[2 * p], s_scr[2 * p + 1]
            sin_ref[2 * p] = sa
            sin_ref[2 * p + 1] = sb
            oa, ob, sa2, sb2 = _gdn_pair(*_gdn_args(p, q_ref, k_ref, v_ref, g_ref, b_ref, gr_ref), sa, sb)
            o_ref[:, la] = oa
            o_ref[:, lb] = ob
            s_scr[2 * p] = sa2
            s_scr[2 * p + 1] = sb2

    tspec = pl.BlockSpec((CHUNK, GDN_V), lambda b, c: (b * n + c, 0))
    rspec = pl.BlockSpec((_GP, 1, LANE), lambda b, c: (b * n + c, 0, 0))
    sspec = pl.BlockSpec((GDN_HEADS, LANE, LANE), lambda b, c: (b * n + c, 0, 0))
    return _pcall("gdn_fwd", body, (bsz, n), [tspec] * 5 + [rspec], [tspec, sspec],
                  [_sds((bsz * n * CHUNK, GDN_V)), _sds((bsz * n * GDN_HEADS, LANE, LANE))],
                  scratch=[pltpu.VMEM((GDN_HEADS, LANE, LANE), F32)], sem=("parallel", "arbitrary"))(q, k, v, gc, beta, gc_row)


def _gdn_bwd(q, k, v, gc, beta, gc_row, s_in, d_o, bsz, n):
    def body(q_ref, k_ref, v_ref, g_ref, b_ref, gr_ref, sin_ref, do_ref, dq_ref, dk_ref, dv_ref, dg_ref, db_ref, dgr_ref, ds_scr):
        _zero_at_first([ds_scr], pl.program_id(1) == 0)
        for p in range(_GP):
            la, lb = _pair_lanes(p)
            _, vjp = jax.vjp(_gdn_pair, *_gdn_args(p, q_ref, k_ref, v_ref, g_ref, b_ref, gr_ref), sin_ref[2 * p], sin_ref[2 * p + 1])
            cts = vjp((do_ref[:, la], do_ref[:, lb], ds_scr[2 * p], ds_scr[2 * p + 1]))
            for ref, i in ((dq_ref, 0), (dk_ref, 2), (dv_ref, 4), (dg_ref, 6), (db_ref, 8)):
                ref[:, la] = cts[i]
                ref[:, lb] = cts[i + 1]
            dgr_ref[p] = cts[10]
            dg_ref[_LAST, la] += cts[11]
            dg_ref[_LAST, lb] += cts[12]
            ds_scr[2 * p] = cts[13]
            ds_scr[2 * p + 1] = cts[14]

    tspec = pl.BlockSpec((CHUNK, GDN_V), lambda b, c: (b * n + (n - 1 - c), 0))
    rspec = pl.BlockSpec((_GP, 1, LANE), lambda b, c: (b * n + (n - 1 - c), 0, 0))
    sspec = pl.BlockSpec((GDN_HEADS, LANE, LANE), lambda b, c: (b * n + (n - 1 - c), 0, 0))
    tok_shape = _sds((bsz * n * CHUNK, GDN_V))
    return _pcall("gdn_bwd", body, (bsz, n), [tspec] * 5 + [rspec, sspec, tspec], [tspec] * 5 + [rspec],
                  [tok_shape] * 5 + [_sds((bsz * n * _GP, 1, LANE))],
                  scratch=[pltpu.VMEM((GDN_HEADS, LANE, LANE), F32)], sem=("parallel", "arbitrary"))(
                      q, k, v, gc, beta, gc_row, s_in, d_o)


_B_OFF = SSD_D // LANE
_C_OFF = (SSD_D + SSD_BC) // LANE
_PPG = _SP // SSD_GROUPS


def _ssd_args(p, x_ref, dt_ref, a_ref, ar_ref):
    lp = slice(p * LANE, (p + 1) * LANE)
    gi = p // _PPG
    b_sl = slice((_B_OFF + gi) * LANE, (_B_OFF + gi + 1) * LANE)
    c_sl = slice((_C_OFF + gi) * LANE, (_C_OFF + gi + 1) * LANE)
    return (x_ref[:, lp], dt_ref[:, lp], a_ref[:, lp], ar_ref[p], a_ref[_LAST, lp], x_ref[:, b_sl], x_ref[:, c_sl])


def _ssd_fwd(xbc, dt, acs, acs_row, bsz, n):
    def body(x_ref, dt_ref, a_ref, ar_ref, y_ref, sin_ref, s_scr):
        _zero_at_first([s_scr], pl.program_id(1) == 0)
        for p in range(_SP):
            st = s_scr[p]
            sin_ref[p] = st
            y, st2 = _ssd_pair(*_ssd_args(p, x_ref, dt_ref, a_ref, ar_ref), st)
            y_ref[:, p * LANE:(p + 1) * LANE] = y
            s_scr[p] = st2

    tspec = pl.BlockSpec((CHUNK, SSD_D), lambda b, c: (b * n + c, 0))
    return _pcall("ssd_fwd", body, (bsz, n),
                  [pl.BlockSpec((CHUNK, SSD_CONV_CH), lambda b, c: (b * n + c, 0)), tspec, tspec,
                   pl.BlockSpec((_SP, 1, LANE), lambda b, c: (b * n + c, 0, 0))],
                  [tspec, pl.BlockSpec((_SP, LANE, LANE), lambda b, c: (b * n + c, 0, 0))],
                  [_sds((bsz * n * CHUNK, SSD_D)), _sds((bsz * n * _SP, LANE, LANE))],
                  scratch=[pltpu.VMEM((_SP, LANE, LANE), F32)], sem=("parallel", "arbitrary"))(xbc, dt, acs, acs_row)


def _ssd_bwd(xbc, dt, acs, acs_row, s_in, d_y, d_x_skip, bsz, n):
    def body(x_ref, dt_ref, a_ref, ar_ref, sin_ref, dy_ref, dsk_ref, dx_ref, ddt_ref, da_ref, dar_ref, ds_scr):
        _zero_at_first([ds_scr], pl.program_id(1) == 0)
        d_b = [None] * SSD_GROUPS
        d_c = [None] * SSD_GROUPS
        for p in range(_SP):
            lp = slice(p * LANE, (p + 1) * LANE)
            gi = p // _PPG
            _, vjp = jax.vjp(_ssd_pair, *_ssd_args(p, x_ref, dt_ref, a_ref, ar_ref), sin_ref[p])
            cts = vjp((dy_ref[:, lp], ds_scr[p]))
            dx_ref[:, lp] = cts[0] + dsk_ref[:, lp]
            ddt_ref[:, lp] = cts[1]
            da_ref[:, lp] = cts[2]
            dar_ref[p] = cts[3]
            da_ref[_LAST, lp] += cts[4]
            d_b[gi] = cts[5] if d_b[gi] is None else d_b[gi] + cts[5]
            d_c[gi] = cts[6] if d_c[gi] is None else d_c[gi] + cts[6]
            ds_scr[p] = cts[7]
        for gi in range(SSD_GROUPS):
            dx_ref[:, (_B_OFF + gi) * LANE:(_B_OFF + gi + 1) * LANE] = d_b[gi]
            dx_ref[:, (_C_OFF + gi) * LANE:(_C_OFF + gi + 1) * LANE] = d_c[gi]

    def rev(b, c):
        return b * n + (n - 1 - c)

    tspec = pl.BlockSpec((CHUNK, SSD_D), lambda b, c: (rev(b, c), 0))
    xspec = pl.BlockSpec((CHUNK, SSD_CONV_CH), lambda b, c: (rev(b, c), 0))
    rspec = pl.BlockSpec((_SP, 1, LANE), lambda b, c: (rev(b, c), 0, 0))
    tok_shape = _sds((bsz * n * CHUNK, SSD_D))
    return _pcall("ssd_bwd", body, (bsz, n),
                  [xspec, tspec, tspec, rspec, pl.BlockSpec((_SP, LANE, LANE), lambda b, c: (rev(b, c), 0, 0)), tspec, tspec],
                  [xspec, tspec, tspec, rspec],
                  [_sds((bsz * n * CHUNK, SSD_CONV_CH)), tok_shape, tok_shape, _sds((bsz * n * _SP, 1, LANE))],
                  scratch=[pltpu.VMEM((_SP, LANE, LANE), F32)], sem=("parallel", "arbitrary"))(
                      xbc, dt, acs, acs_row, s_in, d_y, d_x_skip)


def _add2(name, a, b):
    t, c = a.shape
    tm = _row_tile(t)

    def body(a_ref, b_ref, o_ref):
        o_ref[...] = a_ref[...] + b_ref[...]

    return _pcall(name, body, (t // tm,), [_row_spec(tm, c)] * 2, _row_spec(tm, c), _sds((t, c)), sem=("parallel",))(a, b)


def _rep(p, width):
    return jnp.repeat(p.reshape(-1), width).reshape(1, -1)


def _to_rows(narrow, lo, hi, bsz, n):
    heads = hi - lo
    a = narrow[:, lo:hi].reshape(bsz, n, CHUNK, heads)
    return jnp.transpose(a, (0, 1, 3, 2)).reshape(bsz * n * (heads // 2), 1, 2 * CHUNK)


def _from_rows(rows, heads, bsz, n):
    return jnp.transpose(rows.reshape(bsz, n, heads, CHUNK), (0, 1, 3, 2)).reshape(bsz * n * CHUNK, heads)


def _narrow_row(gdn_part, ssd_part):
    return jnp.pad(jnp.concatenate([gdn_part, ssd_part], axis=1), ((0, 0), (_G_LO, SMALL - _S_HI)))


def _local_step(x, tgt, p):
    bsz, s, _ = x.shape
    t = bsz * s
    n = s // CHUNK
    x2 = x.reshape(t, D_MODEL)
    tgt2 = tgt.reshape(t, D_MODEL)
    w_in = p["w_in"]
    w_big = jnp.concatenate([w_in[:, :4096], w_in[:, 4112:6672]], axis=1)
    w_small = jnp.concatenate([w_in[:, 4096:4112], w_in[:, 6672:6688], jnp.zeros((D_MODEL, SMALL - 32), BF16)], axis=1)
    gate_bias = _narrow_row(p["gdn_dt_bias"], p["ssd_dt_bias"])
    gate_a_log = _narrow_row(p["gdn_a_log"], p["ssd_a_log"])
    d_skip = _rep(p["ssd_d"], SSD_HEADDIM)

    h = _rms_fwd("rms0_fwd", x2, p["pre_mix_norm"])
    proj = _matmul("mm_in_big", h, w_big, "nn", F32)
    small = _matmul("mm_in_small", h, w_small, "nn", F32)
    beta, gc, dt, acs, cum = _gates_fwd(small, gate_bias, gate_a_log)
    gc_row = _to_rows(cum, _G_LO, _G_HI, bsz, n)
    acs_row = _to_rows(cum, _S_LO, _S_HI, bsz, n)
    q = _gdn_conv_fwd("q", proj, p["gdn_conv_w"], bsz, s)
    k = _gdn_conv_fwd("k", proj, p["gdn_conv_w"], bsz, s)
    v = _gdn_conv_fwd("v", proj, p["gdn_conv_w"], bsz, s)
    o, gdn_s = _gdn_fwd(q, k, v, gc, beta, gc_row, bsz, n)
    o_a = _gdn_out_fwd(o, proj, p["gdn_norm_w"])
    xbc = _ssd_conv_fwd(proj, p["ssd_conv_w"], p["ssd_conv_b"], bsz, s)
    y, ssd_s = _ssd_fwd(xbc, dt, acs, acs_row, bsz, n)
    o_s = _ssd_out_fwd(y, xbc, proj, d_skip, p["ssd_norm_w"])
    ocat = jnp.concatenate([o_a, o_s], axis=1)
    mix = _matmul("mm_out", ocat, p["w_out"], "nn", F32)
    x1, h2 = _res1_fwd(x2, mix, p["post_mix_norm"], p["pre_ffn_norm"])
    u_pre = _matmul("mm_up", h2, p["w_up"], "nn", F32)
    act = _ffn_act_fwd(u_pre, p["ffn_conv_w"], p["ffn_conv_b"], bsz, s)
    f = _matmul("mm_down", act, p["w_down"], "nn", F32)
    loss_acc, d_out, d_f, g_post_ffn = _final(x1, f, p["post_ffn_norm"], tgt2)

    grads = {"post_ffn_norm": g_post_ffn}
    d_act = _matmul("mm_down_dx", d_f, p["w_down"], "nt", F32)
    grads["w_down"] = _matmul("mm_down_dw", act, d_f, "tn", F32)
    d_gate, d_up, dwg, dwu, dbg, dbu = _ffn_act_bwd(u_pre, p["ffn_conv_w"], p["ffn_conv_b"], d_act, bsz, s)
    grads["ffn_conv_w"] = jnp.concatenate([dwg, dwu], axis=1)
    grads["ffn_conv_b"] = jnp.concatenate([dbg, dbu], axis=1)
    d_u = jnp.concatenate([d_gate, d_up], axis=1)
    d_h2 = _matmul("mm_up_dx", d_u, p["w_up"], "nt", F32)
    grads["w_up"] = _matmul("mm_up_dw", h2, d_u, "tn", F32)
    d_x1, d_mix, grads["post_mix_norm"], grads["pre_ffn_norm"] = _res1_bwd(
        x2, mix, p["post_mix_norm"], p["pre_ffn_norm"], d_out, d_h2)
    d_ocat = _matmul("mm_out_dx", d_mix, p["w_out"], "nt", F32)
    grads["w_out"] = _matmul("mm_out_dw", ocat, d_mix, "tn", F32)

    d_o, d_za, grads["gdn_norm_w"] = _gdn_out_bwd(o, proj, p["gdn_norm_w"], d_ocat)
    d_q, d_k, d_v, d_gc, d_beta, d_gc_row = _gdn_bwd(q, k, v, gc, beta, gc_row, gdn_s, d_o, bsz, n)
    d_qp, dwq = _gdn_conv_bwd("q", proj, p["gdn_conv_w"], d_q, bsz, s)
    d_kp, dwk = _gdn_conv_bwd("k", proj, p["gdn_conv_w"], d_k, bsz, s)
    d_vp, dwv = _gdn_conv_bwd("v", proj, p["gdn_conv_w"], d_v, bsz, s)
    grads["gdn_conv_w"] = jnp.concatenate([dwq, dwk, dwv], axis=1)

    d_y, d_xs_skip, d_zs, d_dskip, grads["ssd_norm_w"] = _ssd_out_bwd(y, xbc, proj, d_skip, p["ssd_norm_w"], d_ocat)
    d_xbc, d_dt, d_acs, d_acs_row = _ssd_bwd(xbc, dt, acs, acs_row, ssd_s, d_y, d_xs_skip, bsz, n)
    d_xbcp, grads["ssd_conv_w"], grads["ssd_conv_b"] = _ssd_conv_bwd(proj, p["ssd_conv_w"], p["ssd_conv_b"], d_xbc, bsz, s)

    d_cum = jnp.concatenate([jnp.zeros((t, _G_LO), F32), _from_rows(d_gc_row, GDN_HEADS, bsz, n),
                             _from_rows(d_acs_row, SSD_HEADS, bsz, n), jnp.zeros((t, SMALL - _S_HI), F32)], axis=1)
    d_small, d_gate_bias, d_gate_a_log = _gates_bwd(small, gate_bias, gate_a_log, (d_beta, d_gc, d_dt, d_acs, d_cum))
    grads["gdn_dt_bias"], grads["ssd_dt_bias"] = d_gate_bias[:, _G_LO:_G_HI], d_gate_bias[:, _S_LO:_S_HI]
    grads["gdn_a_log"], grads["ssd_a_log"] = d_gate_a_log[:, _G_LO:_G_HI], d_gate_a_log[:, _S_LO:_S_HI]
    d_proj = jnp.concatenate([d_qp, d_kp, d_vp, d_za, d_zs, d_xbcp], axis=1)
    d_h = _add2("add_dh", _matmul("mm_in_big_dx", d_proj, w_big, "nt", F32),
                _matmul("mm_in_small_dx", d_small, w_small, "nt", F32))
    dw_big = _matmul("mm_in_big_dw", h, d_proj, "tn", F32)
    dw_small = _matmul("mm_in_small_dw", h, d_small, "tn", F32)
    grads["w_in"] = jnp.concatenate([dw_big[:, :4096], dw_small[:, :16], dw_big[:, 4096:], dw_small[:, 16:32]], axis=1)
    grad_x, grads["pre_mix_norm"] = _rms1_bwd(x2, p["pre_mix_norm"], d_h, d_x1)
    grads["ssd_d"] = _head_sums(d_dskip)[:1, :SSD_HEADS]
    return loss_acc, grad_x.reshape(bsz, s, D_MODEL), grads


def _head_sums(wide):
    def body(x_ref, o_ref):
        r, c = _iota2((D_MODEL, SMALL))
        o_ref[...] = _mask_dot(jnp.broadcast_to(x_ref[...], (8, D_MODEL)), ((r >> 6) == c).astype(F32), NN, True)

    return _pcall("head_sums", body, (1,), [_full_spec((1, D_MODEL))], _full_spec((8, SMALL)), _sds((8, SMALL)))(wide)


def _adamw_fn(w, g, m, v):
    m = ADAM_B1 * m + (1.0 - ADAM_B1) * g
    v = ADAM_B2 * v + (1.0 - ADAM_B2) * (g * g)
    m_hat = m / (1.0 - ADAM_B1 ** ADAM_STEP)
    v_hat = v / (1.0 - ADAM_B2 ** ADAM_STEP)
    delta = -ADAM_LR * (m_hat / (jnp.sqrt(v_hat) + ADAM_EPS) + ADAM_WD * w)
    return delta, m, v


def _adamw(name, w, g, m, v):
    r, c = w.shape
    tr = _pick(r, (256, 176, 128, 64, 8))

    def body(w_ref, g_ref, m_ref, v_ref, d_ref, m2_ref, v2_ref):
        d, m2, v2 = _adamw_fn(w_ref[...], g_ref[...], m_ref[...], v_ref[...])
        d_ref[...] = d
        m2_ref[...] = m2
        v2_ref[...] = v2

    spec = pl.BlockSpec((tr, c), lambda i: (i, 0))
    return _pcall(name, body, (r // tr,), [spec] * 4, [spec] * 3, [_sds((r, c))] * 3, sem=("parallel",))(w, g, m, v)


_ANY = pl.BlockSpec(memory_space=pl.ANY)
_OTHER_CHIPS = ((1, 0), (0, 1), (1, 1))


def _coords():
    return lax.axis_index("x"), lax.axis_index("y"), lax.axis_index("c")


def _flip(v, f):
    return 1 - v if f else v


def _gather_chips(arrs):
    n = len(arrs)

    def body(*refs):
        ins, outs = refs[:n], refs[n:2 * n]
        send_sems, recv_sems, loc_sems = refs[2 * n:]
        x, y, c = _coords()
        me = 2 * x + y
        started = []
        for a in range(n):
            loc = pltpu.make_async_copy(ins[a], outs[a].at[me], loc_sems.at[a])
            loc.start()
            started.append(loc)
        sends = []
        for a in range(n):
            for j, (fx, fy) in enumerate(_OTHER_CHIPS):
                cp = pltpu.make_async_remote_copy(ins[a], outs[a].at[me], send_sems.at[a * 3 + j], recv_sems.at[a * 3 + j],
                                                  device_id=(_flip(x, fx), _flip(y, fy), c), device_id_type=MESH)
                cp.start()
                sends.append(cp)
        for a in range(n):
            for j, (fx, fy) in enumerate(_OTHER_CHIPS):
                src = 2 * _flip(x, fx) + _flip(y, fy)
                pltpu.make_async_remote_copy(ins[a], outs[a].at[src], send_sems.at[a * 3 + j], recv_sems.at[a * 3 + j],
                                             device_id=(_flip(x, fx), _flip(y, fy), c), device_id_type=MESH).wait_recv()
        for cp in sends:
            cp.wait_send()
        for loc in started:
            loc.wait()

    return pl.pallas_call(
        body, name="gather_chips", out_shape=[_sds((N_CHIPS,) + a.shape, a.dtype) for a in arrs],
        in_specs=[_ANY] * n, out_specs=[_ANY] * n,
        scratch_shapes=[pltpu.SemaphoreType.DMA((3 * n,)), pltpu.SemaphoreType.DMA((3 * n,)), pltpu.SemaphoreType.DMA((n,))],
        compiler_params=pltpu.CompilerParams(has_side_effects=True))(*arrs)


_PEERS = tuple((fx, fy, fc) for fx in (0, 1) for fy in (0, 1) for fc in (0, 1))[1:]


def _allreduce_small(x):
    r = x.shape[0]

    def body(x_ref, o_ref, buf, send_sems, recv_sems):
        cx, cy, cc = _coords()
        me = 4 * cx + 2 * cy + cc
        sends = []
        for j, (fx, fy, fc) in enumerate(_PEERS):
            cp = pltpu.make_async_remote_copy(x_ref, buf.at[me], send_sems.at[j], recv_sems.at[j],
                                              device_id=(_flip(cx, fx), _flip(cy, fy), _flip(cc, fc)), device_id_type=MESH)
            cp.start()
            sends.append(cp)
        buf[pl.ds(me, 1)] = x_ref[...][None]
        for j, (fx, fy, fc) in enumerate(_PEERS):
            src = 4 * _flip(cx, fx) + 2 * _flip(cy, fy) + _flip(cc, fc)
            pltpu.make_async_remote_copy(x_ref, buf.at[src], send_sems.at[j], recv_sems.at[j],
                                         device_id=(_flip(cx, fx), _flip(cy, fy), _flip(cc, fc)), device_id_type=MESH).wait_recv()
        for cp in sends:
            cp.wait_send()
        acc = buf[0]
        for d in range(1, N_DEV):
            acc = acc + buf[d]
        o_ref[...] = acc

    vm = pl.BlockSpec(memory_space=pltpu.VMEM)
    return pl.pallas_call(
        body, name="allreduce_small", out_shape=_sds((r, LANE)), in_specs=[vm], out_specs=vm,
        scratch_shapes=[pltpu.VMEM((N_DEV, r, LANE), F32), pltpu.SemaphoreType.DMA((7,)), pltpu.SemaphoreType.DMA((7,))],
        compiler_params=pltpu.CompilerParams(has_side_effects=True, vmem_limit_bytes=VMEM_LIMIT))(x)


def _pair_exchange(name, arrs, send_other_half):
    n = len(arrs)

    def body(*refs):
        ins, outs = refs[:n], refs[n:2 * n]
        send_sems, recv_sems = refs[2 * n:2 * n + 2]
        loc_sems = None if send_other_half else refs[2 * n + 2]
        x, y, c = _coords()
        sib = (x, y, 1 - c)
        local_copies, sends = [], []
        for a in range(n):
            if send_other_half:
                half = ins[a].shape[1] // 2
                src = ins[a].at[:, pl.ds((1 - c) * half, half), :]
                dst = outs[a]
            else:
                src = ins[a]
                dst = outs[a].at[c]
                loc = pltpu.make_async_copy(ins[a], outs[a].at[c], loc_sems.at[a])
                loc.start()
                local_copies.append(loc)
            cp = pltpu.make_async_remote_copy(src, dst, send_sems.at[a], recv_sems.at[a], device_id=sib, device_id_type=MESH)
            cp.start()
            sends.append(cp)
        for a in range(n):
            if send_other_half:
                half = ins[a].shape[1] // 2
                src, dst = ins[a].at[:, pl.ds((1 - c) * half, half), :], outs[a]
            else:
                src, dst = ins[a], outs[a].at[1 - c]
            pltpu.make_async_remote_copy(src, dst, send_sems.at[a], recv_sems.at[a], device_id=sib, device_id_type=MESH).wait_recv()
        for cp in sends:
            cp.wait_send()
        for loc in local_copies:
            loc.wait()

    if send_other_half:
        out_shape = [_sds((a.shape[0], a.shape[1] // 2, a.shape[2]), a.dtype) for a in arrs]
        scratch = [pltpu.SemaphoreType.DMA((n,)), pltpu.SemaphoreType.DMA((n,))]
    else:
        out_shape = [_sds((2,) + a.shape, a.dtype) for a in arrs]
        scratch = [pltpu.SemaphoreType.DMA((n,)), pltpu.SemaphoreType.DMA((n,)), pltpu.SemaphoreType.DMA((n,))]
    return pl.pallas_call(
        body, name=name, out_shape=out_shape, in_specs=[_ANY] * n, out_specs=[_ANY] * n, scratch_shapes=scratch,
        compiler_params=pltpu.CompilerParams(has_side_effects=True))(*arrs)


def _scatter_chips(arrs):
    n = len(arrs)

    def body(*refs):
        ins, outs = refs[:n], refs[n:2 * n]
        send_sems, recv_sems, loc_sems = refs[2 * n:]
        x, y, c = _coords()
        me = 2 * x + y
        local_copies, sends = [], []
        for a in range(n):
            loc = pltpu.make_async_copy(ins[a].at[me], outs[a].at[me], loc_sems.at[a])
            loc.start()
            local_copies.append(loc)
        for a in range(n):
            for j, (fx, fy) in enumerate(_OTHER_CHIPS):
                to = 2 * _flip(x, fx) + _flip(y, fy)
                cp = pltpu.make_async_remote_copy(ins[a].at[to], outs[a].at[me], send_sems.at[a * 3 + j], recv_sems.at[a * 3 + j],
                                                  device_id=(_flip(x, fx), _flip(y, fy), c), device_id_type=MESH)
                cp.start()
                sends.append(cp)
        for a in range(n):
            for j, (fx, fy) in enumerate(_OTHER_CHIPS):
                src = 2 * _flip(x, fx) + _flip(y, fy)
                pltpu.make_async_remote_copy(ins[a].at[src], outs[a].at[src], send_sems.at[a * 3 + j], recv_sems.at[a * 3 + j],
                                             device_id=(_flip(x, fx), _flip(y, fy), c), device_id_type=MESH).wait_recv()
        for cp in sends:
            cp.wait_send()
        for loc in local_copies:
            loc.wait()

    return pl.pallas_call(
        body, name="scatter_chips", out_shape=[_sds(a.shape, a.dtype) for a in arrs], in_specs=[_ANY] * n, out_specs=[_ANY] * n,
        scratch_shapes=[pltpu.SemaphoreType.DMA((3 * n,)), pltpu.SemaphoreType.DMA((3 * n,)), pltpu.SemaphoreType.DMA((n,))],
        compiler_params=pltpu.CompilerParams(has_side_effects=True))(*arrs)


def _pair_add(name, full, recv, core):
    _, r, c = full.shape
    half = r // 2
    tr = _pick(half, (256, 176, 128, 64, 8))
    nb = half // tr

    def body(c_ref, a_ref, b_ref, o_ref):
        o_ref[...] = a_ref[...] + b_ref[...]

    grid_spec = pltpu.PrefetchScalarGridSpec(
        num_scalar_prefetch=1, grid=(N_CHIPS, nb),
        in_specs=[pl.BlockSpec((1, tr, c), lambda k, i, cref: (k, cref[0] * nb + i, 0)),
                  pl.BlockSpec((1, tr, c), lambda k, i, cref: (k, i, 0))],
        out_specs=pl.BlockSpec((1, tr, c), lambda k, i, cref: (k, i, 0)))
    return pl.pallas_call(
        body, name=name, out_shape=_sds((N_CHIPS, half, c)), grid_spec=grid_spec,
        compiler_params=pltpu.CompilerParams(dimension_semantics=("parallel", "parallel"), vmem_limit_bytes=VMEM_LIMIT))(
            core, full, recv)


def _chip_sum(name, parts):
    _, r, c = parts.shape
    tr = _pick(r, (256, 176, 128, 64, 8))

    def body(p_ref, o_ref):
        o_ref[...] = ((p_ref[0] + p_ref[1]) + p_ref[2]) + p_ref[3]

    return _pcall(name, body, (r // tr,), [pl.BlockSpec((N_CHIPS, tr, c), lambda i: (0, i, 0))],
                  pl.BlockSpec((tr, c), lambda i: (i, 0)), _sds((r, c)), sem=("parallel",))(parts)


_WEIGHTS = ("pre_mix_norm", "w_in", "gdn_conv_w", "gdn_a_log", "gdn_dt_bias", "gdn_norm_w", "ssd_conv_w", "ssd_conv_b",
            "ssd_a_log", "ssd_dt_bias", "ssd_d", "ssd_norm_w", "w_out", "post_mix_norm", "pre_ffn_norm", "w_up",
            "ffn_conv_w", "ffn_conv_b", "w_down", "post_ffn_norm")
_BIG = ("w_in", "w_out", "w_up", "w_down")
_COL_SHARDED_SMALL = ("gdn_conv_w", "ssd_conv_w", "ffn_conv_w")
_SMALL = tuple(k for k in _WEIGHTS if k not in _BIG)


def _pack(arrs):
    flat = jnp.concatenate([a.reshape(-1) for a in arrs])
    rows = -(-flat.shape[0] // (8 * LANE)) * 8
    return jnp.pad(flat, (0, rows * LANE - flat.shape[0])).reshape(rows, LANE)


def _unpack(packed, shapes):
    flat = packed.reshape(-1)
    out, off = [], 0
    for shp in shapes:
        size = 1
        for d in shp:
            size *= d
        out.append(flat[off:off + size].reshape(shp))
        off += size
    return out


def _cols_to_chips(a):
    r, c4 = a.shape
    return jnp.transpose(a.reshape(r, N_CHIPS, c4 // N_CHIPS), (1, 0, 2))


def _chips_to_cols(a):
    k, r, c = a.shape
    return jnp.transpose(a, (1, 0, 2)).reshape(r, k * c)


def kernel(x, pre_mix_norm, w_in, gdn_conv_w, gdn_a_log, gdn_dt_bias, gdn_norm_w, ssd_conv_w, ssd_conv_b, ssd_a_log, ssd_dt_bias, ssd_d, ssd_norm_w, w_out, post_mix_norm, pre_ffn_norm, w_up, ffn_conv_w, ffn_conv_b, w_down, post_ffn_norm, loss_target, m_pre_mix_norm, m_w_in, m_gdn_conv_w, m_gdn_a_log, m_gdn_dt_bias, m_gdn_norm_w, m_ssd_conv_w, m_ssd_conv_b, m_ssd_a_log, m_ssd_dt_bias, m_ssd_d, m_ssd_norm_w, m_w_out, m_post_mix_norm, m_pre_ffn_norm, m_w_up, m_ffn_conv_w, m_ffn_conv_b, m_w_down, m_post_ffn_norm, v_pre_mix_norm, v_w_in, v_gdn_conv_w, v_gdn_a_log, v_gdn_dt_bias, v_gdn_norm_w, v_ssd_conv_w, v_ssd_conv_b, v_ssd_a_log, v_ssd_dt_bias, v_ssd_d, v_ssd_norm_w, v_w_out, v_post_mix_norm, v_pre_ffn_norm, v_w_up, v_ffn_conv_w, v_ffn_conv_b, v_w_down, v_post_ffn_norm):
    w = dict(zip(_WEIGHTS, (pre_mix_norm, w_in, gdn_conv_w, gdn_a_log, gdn_dt_bias, gdn_norm_w, ssd_conv_w, ssd_conv_b,
                            ssd_a_log, ssd_dt_bias, ssd_d, ssd_norm_w, w_out, post_mix_norm, pre_ffn_norm, w_up,
                            ffn_conv_w, ffn_conv_b, w_down, post_ffn_norm)))
    m = dict(zip(_WEIGHTS, (m_pre_mix_norm, m_w_in, m_gdn_conv_w, m_gdn_a_log, m_gdn_dt_bias, m_gdn_norm_w, m_ssd_conv_w,
                            m_ssd_conv_b, m_ssd_a_log, m_ssd_dt_bias, m_ssd_d, m_ssd_norm_w, m_w_out, m_post_mix_norm,
                            m_pre_ffn_norm, m_w_up, m_ffn_conv_w, m_ffn_conv_b, m_w_down, m_post_ffn_norm)))
    v = dict(zip(_WEIGHTS, (v_pre_mix_norm, v_w_in, v_gdn_conv_w, v_gdn_a_log, v_gdn_dt_bias, v_gdn_norm_w, v_ssd_conv_w,
                            v_ssd_conv_b, v_ssd_a_log, v_ssd_dt_bias, v_ssd_d, v_ssd_norm_w, v_w_out, v_post_mix_norm,
                            v_pre_ffn_norm, v_w_up, v_ffn_conv_w, v_ffn_conv_b, v_w_down, v_post_ffn_norm)))
    cx, cy, cc = _coords()
    chip = 2 * cx + cy

    shards = [w[k][0].astype(BF16) for k in _BIG] + [w[k][0] for k in _COL_SHARDED_SMALL]
    g_in, g_out, g_up, g_down, g_gcw, g_scw, g_fcw = _gather_chips(shards)
    p = {k: w[k] for k in _SMALL if k not in _COL_SHARDED_SMALL}
    p["w_in"] = _chips_to_cols(g_in)
    p["w_up"] = _chips_to_cols(g_up)
    p["w_out"] = g_out.reshape(-1, D_MODEL)
    p["w_down"] = g_down.reshape(-1, D_MODEL)
    p["gdn_conv_w"] = _chips_to_cols(g_gcw)
    p["ssd_conv_w"] = _chips_to_cols(g_scw)
    p["ffn_conv_w"] = _chips_to_cols(g_fcw)

    loss_acc, grad_x, grads = _local_step(x, loss_target, p)
    loss = lax.psum(loss_acc[0, 0], ("x", "y", "c"))

    small_full_shapes = [grads[k].shape for k in _SMALL]
    summed = _unpack(_allreduce_small(_pack([grads[k] for k in _SMALL])), small_full_shapes)
    g_small = dict(zip(_SMALL, summed))
    for k in _COL_SHARDED_SMALL:
        width = w[k].shape[2]
        g_small[k] = lax.dynamic_slice_in_dim(g_small[k], chip * width, width, axis=1)

    big = [_cols_to_chips(grads["w_in"]), grads["w_out"].reshape(N_CHIPS, -1, D_MODEL),
           _cols_to_chips(grads["w_up"]), grads["w_down"].reshape(N_CHIPS, -1, D_MODEL)]
    from_sibling = _pair_exchange("pair_reduce_send", big, True)
    core = cc.astype(jnp.int32).reshape(1)
    pair_sums = [_pair_add("pair_add_" + k, a, b, core) for k, a, b in zip(_BIG, big, from_sibling)]
    landed = _scatter_chips(pair_sums)
    mine = [_chip_sum("chip_sum_" + k, a) for k, a in zip(_BIG, landed)]
    both = _pair_exchange("pair_gather", mine, False)
    g_big = {k: a.reshape(-1, a.shape[2]) for k, a in zip(_BIG, both)}

    out_g, out_d, out_m, out_v = {}, {}, {}, {}
    for k in _BIG:
        out_g[k] = g_big[k][None]
        d_, m_, v_ = _adamw("adamw_" + k, w[k][0], g_big[k], m[k][0], v[k][0])
        out_d[k], out_m[k], out_v[k] = d_[None], m_[None], v_[None]
    shapes = [w[k].shape for k in _SMALL]
    for k in _SMALL:
        out_g[k] = g_small[k].reshape(w[k].shape)
    packed = [_pack([d[k] for k in _SMALL]) for d in (w, out_g, m, v)]
    d_p, m_p, v_p = _adamw("adamw_small", *packed)
    for dst, src in ((out_d, d_p), (out_m, m_p), (out_v, v_p)):
        dst.update(zip(_SMALL, _unpack(src, shapes)))
    return (loss, grad_x, *[out_g[k] for k in _WEIGHTS], *[out_d[k] for k in _WEIGHTS],
            *[out_m[k] for k in _WEIGHTS], *[out_v[k] for k in _WEIGHTS])
```

```python
import functools

import jax
import jax.numpy as jnp
from jax import lax
from jax.experimental import pallas as pl
from jax.experimental.pallas import tpu as pltpu

F32 = jnp.float32
BF16 = jnp.bfloat16

D_MODEL = 1024
GDN_HEADS = 8
GDN_DK = 128
SSD_HEADS = 16
SSD_HEADDIM = 64
SSD_GROUPS = 2
SSD_STATE = 128
CONV_K = 4
CHUNK = 64
D_FF = 2816
FFN_CONV_K = 3
EPS = 1e-6
GDN_QK = GDN_HEADS * GDN_DK
GDN_V = GDN_QK
SSD_D = SSD_HEADS * SSD_HEADDIM
SSD_BC = SSD_GROUPS * SSD_STATE
SSD_CONV_CH = SSD_D + 2 * SSD_BC
BIG = 4 * 1024 + 1024 + SSD_CONV_CH
SMALL = 128
D_IN_PROJ = 6688
LANE = 128
PAIR = 2 * CHUNK
NEG = -1e30
VMEM_LIMIT = 56 * 1024 * 1024

ADAM_LR = 0.001
ADAM_B1 = 0.9
ADAM_B2 = 0.999
ADAM_EPS = 1e-08
ADAM_WD = 0.01
ADAM_STEP = 10

N_CHIPS = 4
N_DEV = 8
MESH = pl.DeviceIdType.MESH

NN = ((1,), (0,))
NT = ((1,), (1,))
TN = ((0,), (0,))


def _bdot(a, b, dims):
    return lax.dot_general(a.astype(BF16), b.astype(BF16), (dims, ((), ())), preferred_element_type=F32)


def _split3(a):
    hi = a.astype(BF16)
    r1 = a - hi.astype(F32)
    mid = r1.astype(BF16)
    return hi, mid, (r1 - mid.astype(F32)).astype(BF16)


@jax.custom_vjp
def _nn(a, b):
    return _bdot(a, b, NN)


@jax.custom_vjp
def _nt(a, b):
    return _bdot(a, b, NT)


@jax.custom_vjp
def _tn(a, b):
    return _bdot(a, b, TN)


_nn.defvjp(lambda a, b: (_nn(a, b), (a, b)), lambda r, g: (_nt(g, r[1]), _tn(r[0], g)))
_nt.defvjp(lambda a, b: (_nt(a, b), (a, b)), lambda r, g: (_nn(g, r[1]), _tn(g, r[0])))
_tn.defvjp(lambda a, b: (_tn(a, b), (a, b)), lambda r, g: (_nt(r[1], g), _nn(r[0], g)))


def _mask_dot(x, mask, dims, x_first):
    acc = None
    for piece in _split3(x):
        term = _bdot(piece, mask, dims) if x_first else _bdot(mask, piece, dims)
        acc = term if acc is None else acc + term
    return acc


@jax.custom_vjp
def _cst_left(cst, x):
    return _mask_dot(x, cst, NN, False)


_cst_left.defvjp(lambda cst, x: (_cst_left(cst, x), cst), lambda cst, g: (jnp.zeros_like(cst), _mask_dot(g, cst, TN, False)))


@jax.custom_vjp
def _cst_right(x, cst):
    return _mask_dot(x, cst, NN, True)


_cst_right.defvjp(lambda x, cst: (_cst_right(x, cst), cst), lambda cst, g: (_mask_dot(g, cst, NT, True), jnp.zeros_like(cst)))


def _lin_left(cst):
    return functools.partial(_cst_left, cst)


def _lin_right(cst):
    return lambda x: _cst_right(x, cst)


@jax.custom_vjp
def _tri_inv_m1(a):
    pm = -a
    ap = a
    for _ in range(5):
        ap = _bdot(ap, ap, NN)
        pm = (pm + ap) + _bdot(pm, ap, NN)
    return pm


def _tri_inv_m1_bwd(pm, g):
    t = g + _bdot(pm, g, TN)
    return (-(t + _bdot(t, pm, NT)),)


_tri_inv_m1.defvjp(lambda a: (lambda pm: (pm, pm))(_tri_inv_m1(a)), _tri_inv_m1_bwd)


@jax.custom_vjp
def _top(x):
    return x[: x.shape[0] // 2]


_top.defvjp(lambda x: (_top(x), None), lambda _, g: (jnp.concatenate([g, jnp.zeros_like(g)], axis=0),))


@jax.custom_vjp
def _bot(x):
    return x[x.shape[0] // 2:]


_bot.defvjp(lambda x: (_bot(x), None), lambda _, g: (jnp.concatenate([jnp.zeros_like(g), g], axis=0),))


@jax.custom_vjp
def _vstack(a, b):
    return jnp.concatenate([a, b], axis=0)


_vstack.defvjp(lambda a, b: (_vstack(a, b), None), lambda _, g: (g[: g.shape[0] // 2], g[g.shape[0] // 2:]))


def _shift_dn_raw(x, s):
    if s == 0:
        return x
    r = pltpu.roll(x, s, axis=0)
    ri = lax.broadcasted_iota(jnp.int32, x.shape, 0)
    return jnp.where(ri >= s, r, 0.0)


def _shift_up_raw(x, s):
    if s == 0:
        return x
    n = x.shape[0]
    r = pltpu.roll(x, n - s, axis=0)
    ri = lax.broadcasted_iota(jnp.int32, x.shape, 0)
    return jnp.where(ri < n - s, r, 0.0)


@functools.partial(jax.custom_vjp, nondiff_argnums=(1,))
def _shift_dn(x, s):
    return _shift_dn_raw(x, s)


_shift_dn.defvjp(lambda x, s: (_shift_dn_raw(x, s), None), lambda s, _, g: (_shift_up_raw(g, s),))


def _conv(x, wrows):
    k_w = len(wrows)
    acc = wrows[k_w - 1] * x
    for k in range(k_w - 1):
        acc = acc + wrows[k] * _shift_dn(x, k_w - 1 - k)
    return acc


def _silu(x):
    return x * jax.nn.sigmoid(x)


def _rms(x, w):
    return x * lax.rsqrt(jnp.mean(x * x, axis=-1, keepdims=True) + EPS) * w


def _l2n(x):
    return x * lax.rsqrt(jnp.sum(x * x, axis=-1, keepdims=True) + EPS)


def _iota2(shape):
    return lax.broadcasted_iota(jnp.int32, shape, 0), lax.broadcasted_iota(jnp.int32, shape, 1)


def _gdn_pair(qa, qb, ka, kb, va, vb, gca, gcb, ba, bb, gcr, gla, glb, sa, sb):
    q = _vstack(qa, qb)
    k = _vstack(ka, kb)
    v = _vstack(va, vb)
    gc = _vstack(gca, gcb)
    beta = _vstack(ba, bb)
    glast = _vstack(jnp.broadcast_to(gla, (CHUNK, LANE)), jnp.broadcast_to(glb, (CHUNK, LANE)))
    ri, ci = _iota2((PAIR, PAIR))
    blk = ((ri >= CHUNK) & (ci >= CHUNK)) | ((ri < CHUNK) & (ci < CHUNK))
    causal = blk & (ri >= ci)
    strict = blk & (ri > ci)
    decay = jnp.exp(jnp.where(causal, gc - jnp.broadcast_to(gcr, (PAIR, PAIR)), NEG))
    eg = jnp.exp(gc)
    kbeta = k * beta
    pm = _tri_inv_m1(jnp.where(strict, _nt(kbeta, k) * decay, 0.0))
    rhs_v = v * beta
    rhs_k = kbeta * eg
    u = rhs_v + _nn(pm, rhs_v)
    w = rhs_k + _nn(pm, rhs_k)
    qk = _nt(q, k) * decay
    q_dec = q * eg
    k_dec = k * jnp.exp(glast - gc)
    v_new = u - _vstack(_nn(_top(w), sa), _nn(_bot(w), sb))
    o = _vstack(_nn(_top(q_dec), sa), _nn(_bot(q_dec), sb)) + _nn(qk, v_new)
    gl = jnp.exp(glast)
    sa2 = sa * _vstack(_top(gl), _top(gl)) + _tn(_top(k_dec), _top(v_new))
    sb2 = sb * _vstack(_bot(gl), _bot(gl)) + _tn(_bot(k_dec), _bot(v_new))
    return _top(o), _bot(o), sa2, sb2


def _ssd_pair(x, dt, acs, acs_r, al, bg, cg, st):
    ri, ci = _iota2((CHUNK, PAIR))
    causal = ri >= jnp.where(ci >= CHUNK, ci - CHUNK, ci)
    xdt = x * dt
    alast = jnp.broadcast_to(al, (CHUNK, PAIR))
    lmat = jnp.exp(jnp.where(causal, acs - jnp.broadcast_to(acs_r, (CHUNK, PAIR)), NEG))
    cb2 = _nt(cg, _vstack(bg, bg))
    xblk = _vstack(jnp.where(ci < CHUNK, xdt, 0.0), jnp.where(ci >= CHUNK, xdt, 0.0))
    y = _nn(cb2 * lmat, xblk) + _nn(cg, st) * jnp.exp(acs)
    el = jnp.exp(alast)
    st2 = st * _vstack(el, el) + _tn(bg, xdt * jnp.exp(alast - acs))
    return y, st2


def _pcall(name, body, grid, in_specs, out_specs, out_shape, scratch=(), sem=None, aliases=None):
    if sem is None:
        sem = ("arbitrary",) * len(grid)
    return pl.pallas_call(
        functools.partial(body),
        out_shape=out_shape,
        grid=grid,
        in_specs=in_specs,
        out_specs=out_specs,
        scratch_shapes=scratch,
        input_output_aliases=aliases or {},
        name=name,
        compiler_params=pltpu.CompilerParams(dimension_semantics=sem, vmem_limit_bytes=VMEM_LIMIT),
    )


def _sds(shape, dtype=F32):
    return jax.ShapeDtypeStruct(shape, dtype)


def _row_spec(tm, width, colblock=0):
    return pl.BlockSpec((tm, width), lambda i, _c=colblock: (i, _c))


def _full_spec(shape):
    nd = len(shape)
    return pl.BlockSpec(shape, lambda *_: (0,) * nd)


def _zero_at_first(refs, first):
    @pl.when(first)
    def _():
        for r in refs:
            r[...] = jnp.zeros(r.shape, r.dtype)


def _pick(n, prefs):
    for p in prefs:
        if n % p == 0:
            return p
    return n


def _matmul(name, a, b, mode, out_dtype):
    if mode == "tn":
        r, m = a.shape
        n = b.shape[1]
        tm = _pick(m, (1024, 1408))
        tn = _pick(n, (512, 256, 128))
        tk = _pick(r, (1024, 512, 256, 128, 64))

        def body(a_ref, b_ref, o_ref):
            _zero_at_first([o_ref], pl.program_id(2) == 0)
            o_ref[...] += _bdot(a_ref[...], b_ref[...], TN)

        return _pcall(
            name, body, (m // tm, n // tn, r // tk),
            [pl.BlockSpec((tk, tm), lambda i, j, k: (k, i)), pl.BlockSpec((tk, tn), lambda i, j, k: (k, j))],
            pl.BlockSpec((tm, tn), lambda i, j, k: (i, j)), _sds((m, n), out_dtype),
            sem=("parallel", "parallel", "arbitrary"))(a, b)
    m, k = a.shape
    n = b.shape[1] if mode == "nn" else b.shape[0]
    tm = _pick(m, (1024, 512, 256, 128, 64) if k <= 2816 else (512, 256, 128, 64))
    tn = _pick(n, (512, 256, 128))
    dims = NN if mode == "nn" else NT

    def body(a_ref, b_ref, o_ref):
        o_ref[...] = _bdot(a_ref[...], b_ref[...], dims).astype(o_ref.dtype)

    b_spec = pl.BlockSpec((k, tn), lambda i, j: (0, j)) if mode == "nn" else pl.BlockSpec((tn, k), lambda i, j: (j, 0))
    return _pcall(
        name, body, (m // tm, n // tn), [pl.BlockSpec((tm, k), lambda i, j: (i, 0)), b_spec],
        pl.BlockSpec((tm, tn), lambda i, j: (i, j)), _sds((m, n), out_dtype), sem=("parallel", "parallel"))(a, b)


def _row_tile(t):
    return _pick(t, (256, 128, 64))


def _rms_fwd(name, x, g):
    t = x.shape[0]
    tm = _row_tile(t)

    def body(x_ref, g_ref, h_ref):
        h_ref[...] = _rms(x_ref[...], g_ref[...]).astype(BF16)

    return _pcall(name, body, (t // tm,), [_row_spec(tm, D_MODEL), _full_spec((1, D_MODEL))], _row_spec(tm, D_MODEL),
                  _sds((t, D_MODEL), BF16), sem=("parallel",))(x, g)


_G_LO, _G_HI = GDN_HEADS, 2 * GDN_HEADS
_S_LO, _S_HI = 2 * GDN_HEADS, 2 * GDN_HEADS + SSD_HEADS


def _gates_fn(small, bias, a_log):
    tm = small.shape[0]
    r, c = _iota2((SMALL, D_MODEL))
    e_b = (r == (c >> 7)).astype(F32)
    e_a = (r == _G_LO + (c >> 7)).astype(F32)
    e_dt = (r == _S_LO + (c >> 6)).astype(F32)
    rr, cc = _iota2((tm, tm))
    in_chunk_tril = (((rr >> 6) == (cc >> 6)) & (rr >= cc)).astype(F32)
    sig = jax.nn.sigmoid(small)
    sp = jax.nn.softplus(small + bias)
    cum = _lin_left(in_chunk_tril)(-jnp.exp(a_log) * sp)
    return _lin_right(e_b)(sig), _lin_right(e_a)(cum), _lin_right(e_dt)(sp), _lin_right(e_dt)(cum), cum


def _gates_fwd(small, bias, a_log):
    t = small.shape[0]
    tm = _row_tile(t)

    def body(s_ref, p0, p1, *outs):
        for o, v in zip(outs, _gates_fn(s_ref[...], p0[...], p1[...])):
            o[...] = v

    pspec = _full_spec((1, SMALL))
    return _pcall("gates_fwd", body, (t // tm,), [_row_spec(tm, SMALL), pspec, pspec],
                  [_row_spec(tm, D_MODEL)] * 4 + [_row_spec(tm, SMALL)], [_sds((t, D_MODEL))] * 4 + [_sds((t, SMALL))],
                  sem=("parallel",))(small, bias, a_log)


def _gates_bwd(small, bias, a_log, cts):
    t = small.shape[0]
    tm = _row_tile(t)

    def body(s_ref, p0, p1, c0, c1, c2, c3, c4, ds_ref, db_ref, da_ref):
        _zero_at_first([db_ref, da_ref], pl.program_id(0) == 0)
        _, vjp = jax.vjp(_gates_fn, s_ref[...], p0[...], p1[...])
        d_s, d_b, d_a = vjp((c0[...], c1[...], c2[...], c3[...], c4[...]))
        ds_ref[...] = d_s.astype(BF16)
        db_ref[...] += d_b
        da_ref[...] += d_a

    pspec = _full_spec((1, SMALL))
    return _pcall("gates_bwd", body, (t // tm,),
                  [_row_spec(tm, SMALL), pspec, pspec] + [_row_spec(tm, D_MODEL)] * 4 + [_row_spec(tm, SMALL)],
                  [_row_spec(tm, SMALL), pspec, pspec], [_sds((t, SMALL), BF16), _sds((1, SMALL)), _sds((1, SMALL))])(
                      small, bias, a_log, *cts)


def _gdn_out_fn(o, z, w):
    return _rms(o, w) * _silu(z)


def _gdn_out_fwd(o, proj, gn):
    t = o.shape[0]
    tm = _row_tile(t)

    def body(o_ref, z_ref, w_ref, y_ref):
        for h in range(GDN_HEADS):
            sl = slice(h * GDN_DK, (h + 1) * GDN_DK)
            y_ref[:, sl] = _gdn_out_fn(o_ref[:, sl], z_ref[:, sl], w_ref[...]).astype(BF16)

    return _pcall("gdn_out_fwd", body, (t // tm,), [_row_spec(tm, GDN_V), _row_spec(tm, GDN_V, 3), _full_spec((1, GDN_DK))],
                  _row_spec(tm, GDN_V), _sds((t, GDN_V), BF16), sem=("parallel",))(o, proj, gn)


def _gdn_out_bwd(o, proj, gn, d_ocat):
    t = o.shape[0]
    tm = _row_tile(t)

    def body(o_ref, z_ref, w_ref, dy_ref, do_ref, dz_ref, dw_ref):
        _zero_at_first([dw_ref], pl.program_id(0) == 0)
        for h in range(GDN_HEADS):
            sl = slice(h * GDN_DK, (h + 1) * GDN_DK)
            _, vjp = jax.vjp(_gdn_out_fn, o_ref[:, sl], z_ref[:, sl], w_ref[...])
            d_o, d_z, d_w = vjp(dy_ref[:, sl])
            do_ref[:, sl] = d_o
            dz_ref[:, sl] = d_z.astype(BF16)
            dw_ref[...] += d_w

    return _pcall("gdn_out_bwd", body, (t // tm,),
                  [_row_spec(tm, GDN_V), _row_spec(tm, GDN_V, 3), _full_spec((1, GDN_DK)), _row_spec(tm, GDN_V, 0)],
                  [_row_spec(tm, GDN_V), _row_spec(tm, GDN_V), _full_spec((1, GDN_DK))],
                  [_sds((t, GDN_V)), _sds((t, GDN_V), BF16), _sds((1, GDN_DK))])(o, proj, gn, d_ocat)


def _ssd_out_fn(y, xs, z, d_skip, w):
    return _rms((y + d_skip * xs) * _silu(z), w)


_SSD_GW = SSD_D // SSD_GROUPS


def _ssd_out_fwd(y, xbc, proj, d_skip, nw):
    t = y.shape[0]
    tm = _row_tile(t)

    def body(y_ref, x_ref, z_ref, d_ref, w_ref, o_ref):
        for gi in range(SSD_GROUPS):
            sl = slice(gi * _SSD_GW, (gi + 1) * _SSD_GW)
            o_ref[:, sl] = _ssd_out_fn(y_ref[:, sl], x_ref[:, sl], z_ref[:, sl], d_ref[:, sl], w_ref[:, sl]).astype(BF16)

    pspec = _full_spec((1, SSD_D))
    return _pcall("ssd_out_fwd", body, (t // tm,),
                  [_row_spec(tm, SSD_D), _row_spec(tm, SSD_D, 0), _row_spec(tm, SSD_D, 4), pspec, pspec],
                  _row_spec(tm, SSD_D), _sds((t, SSD_D), BF16), sem=("parallel",))(y, xbc, proj, d_skip, nw)


def _ssd_out_bwd(y, xbc, proj, d_skip, nw, d_ocat):
    t = y.shape[0]
    tm = _row_tile(t)

    def body(y_ref, x_ref, z_ref, d_ref, w_ref, do_ref, dy_ref, dx_ref, dz_ref, dd_ref, dw_ref):
        _zero_at_first([dd_ref, dw_ref], pl.program_id(0) == 0)
        for gi in range(SSD_GROUPS):
            sl = slice(gi * _SSD_GW, (gi + 1) * _SSD_GW)
            _, vjp = jax.vjp(_ssd_out_fn, y_ref[:, sl], x_ref[:, sl], z_ref[:, sl], d_ref[:, sl], w_ref[:, sl])
            d_y, d_x, d_z, d_d, d_w = vjp(do_ref[:, sl])
            dy_ref[:, sl] = d_y
            dx_ref[:, sl] = d_x
            dz_ref[:, sl] = d_z.astype(BF16)
            dd_ref[:, sl] += d_d
            dw_ref[:, sl] += d_w

    pspec = _full_spec((1, SSD_D))
    row = _row_spec(tm, SSD_D)
    return _pcall("ssd_out_bwd", body, (t // tm,),
                  [row, _row_spec(tm, SSD_D, 0), _row_spec(tm, SSD_D, 4), pspec, pspec, _row_spec(tm, SSD_D, 1)],
                  [row, row, row, pspec, pspec],
                  [_sds((t, SSD_D)), _sds((t, SSD_D)), _sds((t, SSD_D), BF16), _sds((1, SSD_D)), _sds((1, SSD_D))])(
                      y, xbc, proj, d_skip, nw, d_ocat)


def _res1_fn(x, mix, g_pm, g_pf):
    x1 = x + _rms(mix, g_pm)
    return x1, _rms(x1, g_pf)


def _res1_fwd(x, mix, g_pm, g_pf):
    t = x.shape[0]
    tm = _row_tile(t)

    def body(x_ref, m_ref, a_ref, b_ref, x1_ref, h2_ref):
        x1, h2 = _res1_fn(x_ref[...], m_ref[...], a_ref[...], b_ref[...])
        x1_ref[...] = x1
        h2_ref[...] = h2.astype(BF16)

    row, pspec = _row_spec(tm, D_MODEL), _full_spec((1, D_MODEL))
    return _pcall("res1_fwd", body, (t // tm,), [row, row, pspec, pspec], [row, row],
                  [_sds((t, D_MODEL)), _sds((t, D_MODEL), BF16)], sem=("parallel",))(x, mix, g_pm, g_pf)


def _res1_bwd(x, mix, g_pm, g_pf, d_x1, d_h2):
    t = x.shape[0]
    tm = _row_tile(t)

    def body(x_ref, m_ref, a_ref, b_ref, c1_ref, c2_ref, dx_ref, dm_ref, da_ref, db_ref):
        _zero_at_first([da_ref, db_ref], pl.program_id(0) == 0)
        _, vjp = jax.vjp(_res1_fn, x_ref[...], m_ref[...], a_ref[...], b_ref[...])
        d_x, d_m, d_a, d_b = vjp((c1_ref[...], c2_ref[...]))
        dx_ref[...] = d_x
        dm_ref[...] = d_m.astype(BF16)
        da_ref[...] += d_a
        db_ref[...] += d_b

    row, pspec = _row_spec(tm, D_MODEL), _full_spec((1, D_MODEL))
    return _pcall("res1_bwd", body, (t // tm,), [row, row, pspec, pspec, row, row], [row, row, pspec, pspec],
                  [_sds((t, D_MODEL)), _sds((t, D_MODEL), BF16), _sds((1, D_MODEL)), _sds((1, D_MODEL))])(
                      x, mix, g_pm, g_pf, d_x1, d_h2)


def _final_fn(x1, f, g_po, tgt):
    err = x1 + _rms(f, g_po) - tgt
    return 0.5 * jnp.sum(jnp.mean(err * err, axis=-1))


def _final(x1, f, g_po, tgt):
    t = x1.shape[0]
    tm = _row_tile(t)

    def body(x_ref, f_ref, g_ref, t_ref, loss_ref, dx_ref, df_ref, dg_ref):
        _zero_at_first([loss_ref, dg_ref], pl.program_id(0) == 0)
        loss, (d_x, d_f, d_g) = jax.value_and_grad(_final_fn, argnums=(0, 1, 2))(x_ref[...], f_ref[...], g_ref[...], t_ref[...])
        loss_ref[...] += jnp.broadcast_to(loss, loss_ref.shape)
        dx_ref[...] = d_x
        df_ref[...] = d_f.astype(BF16)
        dg_ref[...] += d_g

    row, pspec = _row_spec(tm, D_MODEL), _full_spec((1, D_MODEL))
    return _pcall("final", body, (t // tm,), [row, row, pspec, row], [_full_spec((8, LANE)), row, row, pspec],
                  [_sds((8, LANE)), _sds((t, D_MODEL)), _sds((t, D_MODEL), BF16), _sds((1, D_MODEL))])(x1, f, g_po, tgt)


def _rms1_bwd(x, g, d_h, d_x1):
    t = x.shape[0]
    tm = _row_tile(t)

    def body(x_ref, g_ref, dh_ref, dx1_ref, dx_ref, dg_ref):
        _zero_at_first([dg_ref], pl.program_id(0) == 0)
        _, vjp = jax.vjp(_rms, x_ref[...], g_ref[...])
        d_x, d_g = vjp(dh_ref[...])
        dx_ref[...] = d_x + dx1_ref[...]
        dg_ref[...] += d_g

    row, pspec = _row_spec(tm, D_MODEL), _full_spec((1, D_MODEL))
    return _pcall("rms1_bwd", body, (t // tm,), [row, pspec, row, row], [row, pspec],
                  [_sds((t, D_MODEL)), _sds((1, D_MODEL))])(x, g, d_h, d_x1)


def _qkv_fn(mode):
    def fn(x, *wrows):
        y = _silu(_conv(x, wrows))
        if mode == "q":
            return _l2n(y) * (GDN_DK ** -0.5)
        if mode == "k":
            return _l2n(y)
        return y
    return fn


def _seq_spec(s, tc, off):
    return pl.BlockSpec((s, tc), lambda j, b, _o=off: (b, _o + j))


def _par_spec(rows, tc, off):
    return pl.BlockSpec((rows, tc), lambda j, b, _o=off: (0, _o + j))


def _gdn_conv_fwd(mode, proj, w, bsz, s):
    off = {"q": 0, "k": GDN_HEADS, "v": 2 * GDN_HEADS}[mode]
    fn = _qkv_fn(mode)

    def body(x_ref, w_ref, y_ref):
        y_ref[...] = fn(x_ref[...], *[w_ref[k:k + 1, :] for k in range(CONV_K)])

    return _pcall("gdn_conv_fwd_" + mode, body, (GDN_HEADS, bsz),
                  [_seq_spec(s, GDN_DK, off), _par_spec(CONV_K, GDN_DK, off)], _seq_spec(s, GDN_DK, 0),
                  _sds((bsz * s, GDN_QK)), sem=("parallel", "parallel"))(proj, w)


def _gdn_conv_bwd(mode, proj, w, d_y, bsz, s):
    off = {"q": 0, "k": GDN_HEADS, "v": 2 * GDN_HEADS}[mode]
    fn = _qkv_fn(mode)

    def body(x_ref, w_ref, dy_ref, dx_ref, dw_ref):
        _zero_at_first([dw_ref], pl.program_id(1) == 0)
        _, vjp = jax.vjp(fn, x_ref[...], *[w_ref[k:k + 1, :] for k in range(CONV_K)])
        grads = vjp(dy_ref[...])
        dx_ref[...] = grads[0].astype(BF16)
        for k in range(CONV_K):
            dw_ref[k:k + 1, :] += grads[1 + k]

    return _pcall("gdn_conv_bwd_" + mode, body, (GDN_HEADS, bsz),
                  [_seq_spec(s, GDN_DK, off), _par_spec(CONV_K, GDN_DK, off), _seq_spec(s, GDN_DK, 0)],
                  [_seq_spec(s, GDN_DK, 0), _par_spec(CONV_K, GDN_DK, 0)],
                  [_sds((bsz * s, GDN_QK), BF16), _sds((CONV_K, GDN_QK))], sem=("parallel", "arbitrary"))(proj, w, d_y)


def _ssd_conv_fn(x, bias, *wrows):
    return _silu(_conv(x, wrows) + bias)


_XBC_OFF = (5 * 1024) // LANE


def _ssd_conv_fwd(proj, w, bias, bsz, s):
    nt_ = SSD_CONV_CH // LANE

    def body(x_ref, w_ref, b_ref, y_ref):
        y_ref[...] = _ssd_conv_fn(x_ref[...], b_ref[...], *[w_ref[k:k + 1, :] for k in range(CONV_K)])

    return _pcall("ssd_conv_fwd", body, (nt_, bsz),
                  [_seq_spec(s, LANE, _XBC_OFF), _par_spec(CONV_K, LANE, 0), _par_spec(1, LANE, 0)], _seq_spec(s, LANE, 0),
                  _sds((bsz * s, SSD_CONV_CH)), sem=("parallel", "parallel"))(proj, w, bias)


def _ssd_conv_bwd(proj, w, bias, d_y, bsz, s):
    nt_ = SSD_CONV_CH // LANE

    def body(x_ref, w_ref, b_ref, dy_ref, dx_ref, dw_ref, db_ref):
        _zero_at_first([dw_ref, db_ref], pl.program_id(1) == 0)
        _, vjp = jax.vjp(_ssd_conv_fn, x_ref[...], b_ref[...], *[w_ref[k:k + 1, :] for k in range(CONV_K)])
        grads = vjp(dy_ref[...])
        dx_ref[...] = grads[0].astype(BF16)
        db_ref[...] += grads[1]
        for k in range(CONV_K):
            dw_ref[k:k + 1, :] += grads[2 + k]

    return _pcall("ssd_conv_bwd", body, (nt_, bsz),
                  [_seq_spec(s, LANE, _XBC_OFF), _par_spec(CONV_K, LANE, 0), _par_spec(1, LANE, 0), _seq_spec(s, LANE, 0)],
                  [_seq_spec(s, LANE, 0), _par_spec(CONV_K, LANE, 0), _par_spec(1, LANE, 0)],
                  [_sds((bsz * s, SSD_CONV_CH), BF16), _sds((CONV_K, SSD_CONV_CH)), _sds((1, SSD_CONV_CH))],
                  sem=("parallel", "arbitrary"))(proj, w, bias, d_y)


_FFN_TC = 256
_FFN_NT = D_FF // _FFN_TC


def _ffn_act_fn(xg, xu, bg, bu, *wrows):
    k_w = FFN_CONV_K
    gate = _conv(xg, wrows[:k_w]) + bg
    up = _conv(xu, wrows[k_w:]) + bu
    return _silu(gate) * up


def _ffn_act_fwd(u_pre, w, bias, bsz, s):
    def body(xg_ref, xu_ref, wg_ref, wu_ref, bg_ref, bu_ref, a_ref):
        rows = [wg_ref[k:k + 1, :] for k in range(FFN_CONV_K)] + [wu_ref[k:k + 1, :] for k in range(FFN_CONV_K)]
        a_ref[...] = _ffn_act_fn(xg_ref[...], xu_ref[...], bg_ref[...], bu_ref[...], *rows).astype(BF16)

    return _pcall("ffn_act_fwd", body, (_FFN_NT, bsz),
                  [_seq_spec(s, _FFN_TC, 0), _seq_spec(s, _FFN_TC, _FFN_NT),
                   _par_spec(FFN_CONV_K, _FFN_TC, 0), _par_spec(FFN_CONV_K, _FFN_TC, _FFN_NT),
                   _par_spec(1, _FFN_TC, 0), _par_spec(1, _FFN_TC, _FFN_NT)],
                  _seq_spec(s, _FFN_TC, 0), _sds((bsz * s, D_FF), BF16), sem=("parallel", "parallel"))(
                      u_pre, u_pre, w, w, bias, bias)


def _ffn_act_bwd(u_pre, w, bias, d_a, bsz, s):
    def body(xg_ref, xu_ref, wg_ref, wu_ref, bg_ref, bu_ref, da_ref, dg_ref, du_ref, dwg_ref, dwu_ref, dbg_ref, dbu_ref):
        _zero_at_first([dwg_ref, dwu_ref, dbg_ref, dbu_ref], pl.program_id(1) == 0)
        rows = [wg_ref[k:k + 1, :] for k in range(FFN_CONV_K)] + [wu_ref[k:k + 1, :] for k in range(FFN_CONV_K)]
        _, vjp = jax.vjp(_ffn_act_fn, xg_ref[...], xu_ref[...], bg_ref[...], bu_ref[...], *rows)
        grads = vjp(da_ref[...])
        dg_ref[...] = grads[0].astype(BF16)
        du_ref[...] = grads[1].astype(BF16)
        dbg_ref[...] += grads[2]
        dbu_ref[...] += grads[3]
        for k in range(FFN_CONV_K):
            dwg_ref[k:k + 1, :] += grads[4 + k]
            dwu_ref[k:k + 1, :] += grads[4 + FFN_CONV_K + k]

    seq0, par3, par1 = _seq_spec(s, _FFN_TC, 0), _par_spec(FFN_CONV_K, _FFN_TC, 0), _par_spec(1, _FFN_TC, 0)
    return _pcall("ffn_act_bwd", body, (_FFN_NT, bsz),
                  [seq0, _seq_spec(s, _FFN_TC, _FFN_NT), par3, _par_spec(FFN_CONV_K, _FFN_TC, _FFN_NT),
                   par1, _par_spec(1, _FFN_TC, _FFN_NT), seq0],
                  [seq0, seq0, par3, par3, par1, par1],
                  [_sds((bsz * s, D_FF), BF16), _sds((bsz * s, D_FF), BF16), _sds((FFN_CONV_K, D_FF)), _sds((FFN_CONV_K, D_FF)),
                   _sds((1, D_FF)), _sds((1, D_FF))], sem=("parallel", "arbitrary"))(u_pre, u_pre, w, w, bias, bias, d_a)


_GP = GDN_HEADS // 2
_SP = SSD_HEADS // 2


def _pair_lanes(p):
    return slice(2 * p * LANE, (2 * p + 1) * LANE), slice((2 * p + 1) * LANE, (2 * p + 2) * LANE)


_LAST = slice(CHUNK - 1, CHUNK)


def _gdn_args(p, q_ref, k_ref, v_ref, g_ref, b_ref, gr_ref):
    la, lb = _pair_lanes(p)
    return (q_ref[:, la], q_ref[:, lb], k_ref[:, la], k_ref[:, lb], v_ref[:, la], v_ref[:, lb], g_ref[:, la], g_ref[:, lb],
            b_ref[:, la], b_ref[:, lb], gr_ref[p], g_ref[_LAST, la], g_ref[_LAST, lb])


def _gdn_fwd(q, k, v, gc, beta, gc_row, bsz, n):
    def body(q_ref, k_ref, v_ref, g_ref, b_ref, gr_ref, o_ref, sin_ref, s_scr):
        _zero_at_first([s_scr], pl.program_id(1) == 0)
        for p in range(_GP):
            la, lb = _pair_lanes(p)
            sa, sb = s_scr[2 * p], s_scr[2 * p + 1]
            sin_ref[2 * p] = sa
            sin_ref[2 * p + 1] = sb
            oa, ob, sa2, sb2 = _gdn_pair(*_gdn_args(p, q_ref, k_ref, v_ref, g_ref, b_ref, gr_ref), sa, sb)
            o_ref[:, la] = oa
            o_ref[:, lb] = ob
            s_scr[2 * p] = sa2
            s_scr[2 * p + 1] = sb2

    tspec = pl.BlockSpec((CHUNK, GDN_V), lambda b, c: (b * n + c, 0))
    rspec = pl.BlockSpec((_GP, 1, LANE), lambda b, c: (b * n + c, 0, 0))
    sspec = pl.BlockSpec((GDN_HEADS, LANE, LANE), lambda b, c: (b * n + c, 0, 0))
    return _pcall("gdn_fwd", body, (bsz, n), [tspec] * 5 + [rspec], [tspec, sspec],
                  [_sds((bsz * n * CHUNK, GDN_V)), _sds((bsz * n * GDN_HEADS, LANE, LANE))],
                  scratch=[pltpu.VMEM((GDN_HEADS, LANE, LANE), F32)], sem=("parallel", "arbitrary"))(q, k, v, gc, beta, gc_row)


def _gdn_bwd(q, k, v, gc, beta, gc_row, s_in, d_o, bsz, n):
    def body(q_ref, k_ref, v_ref, g_ref, b_ref, gr_ref, sin_ref, do_ref, dq_ref, dk_ref, dv_ref, dg_ref, db_ref, dgr_ref, ds_scr):
        _zero_at_first([ds_scr], pl.program_id(1) == 0)
        for p in range(_GP):
            la, lb = _pair_lanes(p)
            _, vjp = jax.vjp(_gdn_pair, *_gdn_args(p, q_ref, k_ref, v_ref, g_ref, b_ref, gr_ref), sin_ref[2 * p], sin_ref[2 * p + 1])
            cts = vjp((do_ref[:, la], do_ref[:, lb], ds_scr[2 * p], ds_scr[2 * p + 1]))
            for ref, i in ((dq_ref, 0), (dk_ref, 2), (dv_ref, 4), (dg_ref, 6), (db_ref, 8)):
                ref[:, la] = cts[i]
                ref[:, lb] = cts[i + 1]
            dgr_ref[p] = cts[10]
            dg_ref[_LAST, la] += cts[11]
            dg_ref[_LAST, lb] += cts[12]
            ds_scr[2 * p] = cts[13]
            ds_scr[2 * p + 1] = cts[14]

    tspec = pl.BlockSpec((CHUNK, GDN_V), lambda b, c: (b * n + (n - 1 - c), 0))
    rspec = pl.BlockSpec((_GP, 1, LANE), lambda b, c: (b * n + (n - 1 - c), 0, 0))
    sspec = pl.BlockSpec((GDN_HEADS, LANE, LANE), lambda b, c: (b * n + (n - 1 - c), 0, 0))
    tok_shape = _sds((bsz * n * CHUNK, GDN_V))
    return _pcall("gdn_bwd", body, (bsz, n), [tspec] * 5 + [rspec, sspec, tspec], [tspec] * 5 + [rspec],
                  [tok_shape] * 5 + [_sds((bsz * n * _GP, 1, LANE))],
                  scratch=[pltpu.VMEM((GDN_HEADS, LANE, LANE), F32)], sem=("parallel", "arbitrary"))(
                      q, k, v, gc, beta, gc_row, s_in, d_o)


_B_OFF = SSD_D // LANE
_C_OFF = (SSD_D + SSD_BC) // LANE
_PPG = _SP // SSD_GROUPS


def _ssd_args(p, x_ref, dt_ref, a_ref, ar_ref):
    lp = slice(p * LANE, (p + 1) * LANE)
    gi = p // _PPG
    b_sl = slice((_B_OFF + gi) * LANE, (_B_OFF + gi + 1) * LANE)
    c_sl = slice((_C_OFF + gi) * LANE, (_C_OFF + gi + 1) * LANE)
    return (x_ref[:, lp], dt_ref[:, lp], a_ref[:, lp], ar_ref[p], a_ref[_LAST, lp], x_ref[:, b_sl], x_ref[:, c_sl])


def _ssd_fwd(xbc, dt, acs, acs_row, bsz, n):
    def body(x_ref, dt_ref, a_ref, ar_ref, y_ref, sin_ref, s_scr):
        _zero_at_first([s_scr], pl.program_id(1) == 0)
        for p in range(_SP):
            st = s_scr[p]
            sin_ref[p] = st
            y, st2 = _ssd_pair(*_ssd_args(p, x_ref, dt_ref, a_ref, ar_ref), st)
            y_ref[:, p * LANE:(p + 1) * LANE] = y
            s_scr[p] = st2

    tspec = pl.BlockSpec((CHUNK, SSD_D), lambda b, c: (b * n + c, 0))
    return _pcall("ssd_fwd", body, (bsz, n),
                  [pl.BlockSpec((CHUNK, SSD_CONV_CH), lambda b, c: (b * n + c, 0)), tspec, tspec,
                   pl.BlockSpec((_SP, 1, LANE), lambda b, c: (b * n + c, 0, 0))],
                  [tspec, pl.BlockSpec((_SP, LANE, LANE), lambda b, c: (b * n + c, 0, 0))],
                  [_sds((bsz * n * CHUNK, SSD_D)), _sds((bsz * n * _SP, LANE, LANE))],
                  scratch=[pltpu.VMEM((_SP, LANE, LANE), F32)], sem=("parallel", "arbitrary"))(xbc, dt, acs, acs_row)


def _ssd_bwd(xbc, dt, acs, acs_row, s_in, d_y, d_x_skip, bsz, n):
    def body(x_ref, dt_ref, a_ref, ar_ref, sin_ref, dy_ref, dsk_ref, dx_ref, ddt_ref, da_ref, dar_ref, ds_scr):
        _zero_at_first([ds_scr], pl.program_id(1) == 0)
        d_b = [None] * SSD_GROUPS
        d_c = [None] * SSD_GROUPS
        for p in range(_SP):
            lp = slice(p * LANE, (p + 1) * LANE)
            gi = p // _PPG
            _, vjp = jax.vjp(_ssd_pair, *_ssd_args(p, x_ref, dt_ref, a_ref, ar_ref), sin_ref[p])
            cts = vjp((dy_ref[:, lp], ds_scr[p]))
            dx_ref[:, lp] = cts[0] + dsk_ref[:, lp]
            ddt_ref[:, lp] = cts[1]
            da_ref[:, lp] = cts[2]
            dar_ref[p] = cts[3]
            da_ref[_LAST, lp] += cts[4]
            d_b[gi] = cts[5] if d_b[gi] is None else d_b[gi] + cts[5]
            d_c[gi] = cts[6] if d_c[gi] is None else d_c[gi] + cts[6]
            ds_scr[p] = cts[7]
        for gi in range(SSD_GROUPS):
            dx_ref[:, (_B_OFF + gi) * LANE:(_B_OFF + gi + 1) * LANE] = d_b[gi]
            dx_ref[:, (_C_OFF + gi) * LANE:(_C_OFF + gi + 1) * LANE] = d_c[gi]

    def rev(b, c):
        return b * n + (n - 1 - c)

    tspec = pl.BlockSpec((CHUNK, SSD_D), lambda b, c: (rev(b, c), 0))
    xspec = pl.BlockSpec((CHUNK, SSD_CONV_CH), lambda b, c: (rev(b, c), 0))
    rspec = pl.BlockSpec((_SP, 1, LANE), lambda b, c: (rev(b, c), 0, 0))
    tok_shape = _sds((bsz * n * CHUNK, SSD_D))
    return _pcall("ssd_bwd", body, (bsz, n),
                  [xspec, tspec, tspec, rspec, pl.BlockSpec((_SP, LANE, LANE), lambda b, c: (rev(b, c), 0, 0)), tspec, tspec],
                  [xspec, tspec, tspec, rspec],
                  [_sds((bsz * n * CHUNK, SSD_CONV_CH)), tok_shape, tok_shape, _sds((bsz * n * _SP, 1, LANE))],
                  scratch=[pltpu.VMEM((_SP, LANE, LANE), F32)], sem=("parallel", "arbitrary"))(
                      xbc, dt, acs, acs_row, s_in, d_y, d_x_skip)


def _add2(name, a, b):
    t, c = a.shape
    tm = _row_tile(t)

    def body(a_ref, b_ref, o_ref):
        o_ref[...] = a_ref[...] + b_ref[...]

    return _pcall(name, body, (t // tm,), [_row_spec(tm, c)] * 2, _row_spec(tm, c), _sds((t, c)), sem=("parallel",))(a, b)


def _rep(p, width):
    return jnp.repeat(p.reshape(-1), width).reshape(1, -1)


def _to_rows(narrow, lo, hi, bsz, n):
    heads = hi - lo
    a = narrow[:, lo:hi].reshape(bsz, n, CHUNK, heads)
    return jnp.transpose(a, (0, 1, 3, 2)).reshape(bsz * n * (heads // 2), 1, 2 * CHUNK)


def _from_rows(rows, heads, bsz, n):
    return jnp.transpose(rows.reshape(bsz, n, heads, CHUNK), (0, 1, 3, 2)).reshape(bsz * n * CHUNK, heads)


def _narrow_row(gdn_part, ssd_part):
    return jnp.pad(jnp.concatenate([gdn_part, ssd_part], axis=1), ((0, 0), (_G_LO, SMALL - _S_HI)))


def _local_step(x, tgt, p):
    bsz, s, _ = x.shape
    t = bsz * s
    n = s // CHUNK
    x2 = x.reshape(t, D_MODEL)
    tgt2 = tgt.reshape(t, D_MODEL)
    w_in = p["w_in"]
    w_big = jnp.concatenate([w_in[:, :4096], w_in[:, 4112:6672]], axis=1)
    w_small = jnp.concatenate([w_in[:, 4096:4112], w_in[:, 6672:6688], jnp.zeros((D_MODEL, SMALL - 32), BF16)], axis=1)
    gate_bias = _narrow_row(p["gdn_dt_bias"], p["ssd_dt_bias"])
    gate_a_log = _narrow_row(p["gdn_a_log"], p["ssd_a_log"])
    d_skip = _rep(p["ssd_d"], SSD_HEADDIM)

    h = _rms_fwd("rms0_fwd", x2, p["pre_mix_norm"])
    proj = _matmul("mm_in_big", h, w_big, "nn", F32)
    small = _matmul("mm_in_small", h, w_small, "nn", F32)
    beta, gc, dt, acs, cum = _gates_fwd(small, gate_bias, gate_a_log)
    gc_row = _to_rows(cum, _G_LO, _G_HI, bsz, n)
    acs_row = _to_rows(cum, _S_LO, _S_HI, bsz, n)
    q = _gdn_conv_fwd("q", proj, p["gdn_conv_w"], bsz, s)
    k = _gdn_conv_fwd("k", proj, p["gdn_conv_w"], bsz, s)
    v = _gdn_conv_fwd("v", proj, p["gdn_conv_w"], bsz, s)
    o, gdn_s = _gdn_fwd(q, k, v, gc, beta, gc_row, bsz, n)
    o_a = _gdn_out_fwd(o, proj, p["gdn_norm_w"])
    xbc = _ssd_conv_fwd(proj, p["ssd_conv_w"], p["ssd_conv_b"], bsz, s)
    y, ssd_s = _ssd_fwd(xbc, dt, acs, acs_row, bsz, n)
    o_s = _ssd_out_fwd(y, xbc, proj, d_skip, p["ssd_norm_w"])
    ocat = jnp.concatenate([o_a, o_s], axis=1)
    mix = _matmul("mm_out", ocat, p["w_out"], "nn", F32)
    x1, h2 = _res1_fwd(x2, mix, p["post_mix_norm"], p["pre_ffn_norm"])
    u_pre = _matmul("mm_up", h2, p["w_up"], "nn", F32)
    act = _ffn_act_fwd(u_pre, p["ffn_conv_w"], p["ffn_conv_b"], bsz, s)
    f = _matmul("mm_down", act, p["w_down"], "nn", F32)
    loss_acc, d_out, d_f, g_post_ffn = _final(x1, f, p["post_ffn_norm"], tgt2)

    grads = {"post_ffn_norm": g_post_ffn}
    d_act = _matmul("mm_down_dx", d_f, p["w_down"], "nt", F32)
    grads["w_down"] = _matmul("mm_down_dw", act, d_f, "tn", F32)
    d_gate, d_up, dwg, dwu, dbg, dbu = _ffn_act_bwd(u_pre, p["ffn_conv_w"], p["ffn_conv_b"], d_act, bsz, s)
    grads["ffn_conv_w"] = jnp.concatenate([dwg, dwu], axis=1)
    grads["ffn_conv_b"] = jnp.concatenate([dbg, dbu], axis=1)
    d_u = jnp.concatenate([d_gate, d_up], axis=1)
    d_h2 = _matmul("mm_up_dx", d_u, p["w_up"], "nt", F32)
    grads["w_up"] = _matmul("mm_up_dw", h2, d_u, "tn", F32)
    d_x1, d_mix, grads["post_mix_norm"], grads["pre_ffn_norm"] = _res1_bwd(
        x2, mix, p["post_mix_norm"], p["pre_ffn_norm"], d_out, d_h2)
    d_ocat = _matmul("mm_out_dx", d_mix, p["w_out"], "nt", F32)
    grads["w_out"] = _matmul("mm_out_dw", ocat, d_mix, "tn", F32)

    d_o, d_za, grads["gdn_norm_w"] = _gdn_out_bwd(o, proj, p["gdn_norm_w"], d_ocat)
    d_q, d_k, d_v, d_gc, d_beta, d_gc_row = _gdn_bwd(q, k, v, gc, beta, gc_row, gdn_s, d_o, bsz, n)
    d_qp, dwq = _gdn_conv_bwd("q", proj, p["gdn_conv_w"], d_q, bsz, s)
    d_kp, dwk = _gdn_conv_bwd("k", proj, p["gdn_conv_w"], d_k, bsz, s)
    d_vp, dwv = _gdn_conv_bwd("v", proj, p["gdn_conv_w"], d_v, bsz, s)
    grads["gdn_conv_w"] = jnp.concatenate([dwq, dwk, dwv], axis=1)

    d_y, d_xs_skip, d_zs, d_dskip, grads["ssd_norm_w"] = _ssd_out_bwd(y, xbc, proj, d_skip, p["ssd_norm_w"], d_ocat)
    d_xbc, d_dt, d_acs, d_acs_row = _ssd_bwd(xbc, dt, acs, acs_row, ssd_s, d_y, d_xs_skip, bsz, n)
    d_xbcp, grads["ssd_conv_w"], grads["ssd_conv_b"] = _ssd_conv_bwd(proj, p["ssd_conv_w"], p["ssd_conv_b"], d_xbc, bsz, s)

    d_cum = jnp.concatenate([jnp.zeros((t, _G_LO), F32), _from_rows(d_gc_row, GDN_HEADS, bsz, n),
                             _from_rows(d_acs_row, SSD_HEADS, bsz, n), jnp.zeros((t, SMALL - _S_HI), F32)], axis=1)
    d_small, d_gate_bias, d_gate_a_log = _gates_bwd(small, gate_bias, gate_a_log, (d_beta, d_gc, d_dt, d_acs, d_cum))
    grads["gdn_dt_bias"], grads["ssd_dt_bias"] = d_gate_bias[:, _G_LO:_G_HI], d_gate_bias[:, _S_LO:_S_HI]
    grads["gdn_a_log"], grads["ssd_a_log"] = d_gate_a_log[:, _G_LO:_G_HI], d_gate_a_log[:, _S_LO:_S_HI]
    d_proj = jnp.concatenate([d_qp, d_kp, d_vp, d_za, d_zs, d_xbcp], axis=1)
    d_h = _add2("add_dh", _matmul("mm_in_big_dx", d_proj, w_big, "nt", F32),
                _matmul("mm_in_small_dx", d_small, w_small, "nt", F32))
    dw_big = _matmul("mm_in_big_dw", h, d_proj, "tn", F32)
    dw_small = _matmul("mm_in_small_dw", h, d_small, "tn", F32)
    grads["w_in"] = jnp.concatenate([dw_big[:, :4096], dw_small[:, :16], dw_big[:, 4096:], dw_small[:, 16:32]], axis=1)
    grad_x, grads["pre_mix_norm"] = _rms1_bwd(x2, p["pre_mix_norm"], d_h, d_x1)
    grads["ssd_d"] = _head_sums(d_dskip)[:1, :SSD_HEADS]
    return loss_acc, grad_x.reshape(bsz, s, D_MODEL), grads


def _head_sums(wide):
    def body(x_ref, o_ref):
        r, c = _iota2((D_MODEL, SMALL))
        o_ref[...] = _mask_dot(jnp.broadcast_to(x_ref[...], (8, D_MODEL)), ((r >> 6) == c).astype(F32), NN, True)

    return _pcall("head_sums", body, (1,), [_full_spec((1, D_MODEL))], _full_spec((8, SMALL)), _sds((8, SMALL)))(wide)


def _adamw_fn(w, g, m, v):
    m = ADAM_B1 * m + (1.0 - ADAM_B1) * g
    v = ADAM_B2 * v + (1.0 - ADAM_B2) * (g * g)
    m_hat = m / (1.0 - ADAM_B1 ** ADAM_STEP)
    v_hat = v / (1.0 - ADAM_B2 ** ADAM_STEP)
    delta = -ADAM_LR * (m_hat / (jnp.sqrt(v_hat) + ADAM_EPS) + ADAM_WD * w)
    return delta, m, v


def _adamw(name, w, g, m, v):
    r, c = w.shape
    tr = _pick(r, (256, 176, 128, 64, 8))

    def body(w_ref, g_ref, m_ref, v_ref, d_ref, m2_ref, v2_ref):
        d, m2, v2 = _adamw_fn(w_ref[...], g_ref[...], m_ref[...], v_ref[...])
        d_ref[...] = d
        m2_ref[...] = m2
        v2_ref[...] = v2

    spec = pl.BlockSpec((tr, c), lambda i: (i, 0))
    return _pcall(name, body, (r // tr,), [spec] * 4, [spec] * 3, [_sds((r, c))] * 3, sem=("parallel",))(w, g, m, v)


_ANY = pl.BlockSpec(memory_space=pl.ANY)
_OTHER_CHIPS = ((1, 0), (0, 1), (1, 1))


def _coords():
    return lax.axis_index("x"), lax.axis_index("y"), lax.axis_index("c")


def _flip(v, f):
    return 1 - v if f else v


def _gather_chips(arrs, split):
    n = len(arrs)

    def body(*refs):
        ins, outs = refs[:n], refs[n:2 * n]
        send_sems, recv_sems, fwd_send_sems, fwd_recv_sems, loc_sems = refs[2 * n:]
        x, y, c = _coords()
        me = 2 * x + y
        sib = (x, y, 1 - c)

        def rows(a, core):
            if not split[a]:
                return slice(None)
            half = arrs[a].shape[0] // 2
            return pl.ds(core * half, half)

        local_copies, sends = [], []
        for a in range(n):
            loc = pltpu.make_async_copy(ins[a], outs[a].at[me], loc_sems.at[a])
            loc.start()
            local_copies.append(loc)
        for a in range(n):
            for j, (fx, fy) in enumerate(_OTHER_CHIPS):
                cp = pltpu.make_async_remote_copy(ins[a].at[rows(a, c)], outs[a].at[me, rows(a, c)],
                                                  send_sems.at[a * 3 + j], recv_sems.at[a * 3 + j],
                                                  device_id=(_flip(x, fx), _flip(y, fy), c), device_id_type=MESH)
                cp.start()
                sends.append(cp)
        for a in range(n):
            for j, (fx, fy) in enumerate(_OTHER_CHIPS):
                src = 2 * _flip(x, fx) + _flip(y, fy)
                landed = outs[a].at[src, rows(a, c)]
                pltpu.make_async_remote_copy(landed, landed, send_sems.at[a * 3 + j], recv_sems.at[a * 3 + j],
                                             device_id=(_flip(x, fx), _flip(y, fy), c), device_id_type=MESH).wait_recv()
                if split[a]:
                    fw = pltpu.make_async_remote_copy(landed, landed, fwd_send_sems.at[a * 3 + j], fwd_recv_sems.at[a * 3 + j],
                                                      device_id=sib, device_id_type=MESH)
                    fw.start()
                    sends.append(fw)
        for a in range(n):
            if split[a]:
                for j, (fx, fy) in enumerate(_OTHER_CHIPS):
                    src = 2 * _flip(x, fx) + _flip(y, fy)
                    theirs = outs[a].at[src, rows(a, 1 - c)]
                    pltpu.make_async_remote_copy(theirs, theirs, fwd_send_sems.at[a * 3 + j], fwd_recv_sems.at[a * 3 + j],
                                                 device_id=sib, device_id_type=MESH).wait_recv()
        for cp in sends:
            cp.wait_send()
        for loc in local_copies:
            loc.wait()

    sems = [pltpu.SemaphoreType.DMA((3 * n,))] * 4 + [pltpu.SemaphoreType.DMA((n,))]
    return pl.pallas_call(
        body, name="gather_chips", out_shape=[_sds((N_CHIPS,) + a.shape, a.dtype) for a in arrs],
        in_specs=[_ANY] * n, out_specs=[_ANY] * n, scratch_shapes=sems,
        compiler_params=pltpu.CompilerParams(has_side_effects=True))(*arrs)


_PEERS = tuple((fx, fy, fc) for fx in (0, 1) for fy in (0, 1) for fc in (0, 1))[1:]


def _allreduce_small(x):
    r = x.shape[0]

    def body(x_ref, o_ref, buf, send_sems, recv_sems):
        cx, cy, cc = _coords()
        me = 4 * cx + 2 * cy + cc
        sends = []
        for j, (fx, fy, fc) in enumerate(_PEERS):
            cp = pltpu.make_async_remote_copy(x_ref, buf.at[me], send_sems.at[j], recv_sems.at[j],
                                              device_id=(_flip(cx, fx), _flip(cy, fy), _flip(cc, fc)), device_id_type=MESH)
            cp.start()
            sends.append(cp)
        buf[pl.ds(me, 1)] = x_ref[...][None]
        for j, (fx, fy, fc) in enumerate(_PEERS):
            src = 4 * _flip(cx, fx) + 2 * _flip(cy, fy) + _flip(cc, fc)
            pltpu.make_async_remote_copy(x_ref, buf.at[src], send_sems.at[j], recv_sems.at[j],
                                         device_id=(_flip(cx, fx), _flip(cy, fy), _flip(cc, fc)), device_id_type=MESH).wait_recv()
        for cp in sends:
            cp.wait_send()
        acc = buf[0]
        for d in range(1, N_DEV):
            acc = acc + buf[d]
        o_ref[...] = acc

    vm = pl.BlockSpec(memory_space=pltpu.VMEM)
    return pl.pallas_call(
        body, name="allreduce_small", out_shape=_sds((r, LANE)), in_specs=[vm], out_specs=vm,
        scratch_shapes=[pltpu.VMEM((N_DEV, r, LANE), F32), pltpu.SemaphoreType.DMA((7,)), pltpu.SemaphoreType.DMA((7,))],
        compiler_params=pltpu.CompilerParams(has_side_effects=True, vmem_limit_bytes=VMEM_LIMIT))(x)


def _pair_exchange(name, arrs, send_other_half):
    n = len(arrs)

    def body(*refs):
        ins, outs = refs[:n], refs[n:2 * n]
        send_sems, recv_sems = refs[2 * n:2 * n + 2]
        loc_sems = None if send_other_half else refs[2 * n + 2]
        x, y, c = _coords()
        sib = (x, y, 1 - c)
        local_copies, sends = [], []
        for a in range(n):
            if send_other_half:
                half = ins[a].shape[1] // 2
                src = ins[a].at[:, pl.ds((1 - c) * half, half), :]
                dst = outs[a]
            else:
                src = ins[a]
                dst = outs[a].at[c]
                loc = pltpu.make_async_copy(ins[a], outs[a].at[c], loc_sems.at[a])
                loc.start()
                local_copies.append(loc)
            cp = pltpu.make_async_remote_copy(src, dst, send_sems.at[a], recv_sems.at[a], device_id=sib, device_id_type=MESH)
            cp.start()
            sends.append(cp)
        for a in range(n):
            if send_other_half:
                half = ins[a].shape[1] // 2
                src, dst = ins[a].at[:, pl.ds((1 - c) * half, half), :], outs[a]
            else:
                src, dst = ins[a], outs[a].at[1 - c]
            pltpu.make_async_remote_copy(src, dst, send_sems.at[a], recv_sems.at[a], device_id=sib, device_id_type=MESH).wait_recv()
        for cp in sends:
            cp.wait_send()
        for loc in local_copies:
            loc.wait()

    if send_other_half:
        out_shape = [_sds((a.shape[0], a.shape[1] // 2, a.shape[2]), a.dtype) for a in arrs]
        scratch = [pltpu.SemaphoreType.DMA((n,)), pltpu.SemaphoreType.DMA((n,))]
    else:
        out_shape = [_sds((2,) + a.shape, a.dtype) for a in arrs]
        scratch = [pltpu.SemaphoreType.DMA((n,)), pltpu.SemaphoreType.DMA((n,)), pltpu.SemaphoreType.DMA((n,))]
    return pl.pallas_call(
        body, name=name, out_shape=out_shape, in_specs=[_ANY] * n, out_specs=[_ANY] * n, scratch_shapes=scratch,
        compiler_params=pltpu.CompilerParams(has_side_effects=True))(*arrs)


def _scatter_chips(arrs):
    n = len(arrs)

    def body(*refs):
        ins, outs = refs[:n], refs[n:2 * n]
        send_sems, recv_sems, loc_sems = refs[2 * n:]
        x, y, c = _coords()
        me = 2 * x + y
        local_copies, sends = [], []
        for a in range(n):
            loc = pltpu.make_async_copy(ins[a].at[me], outs[a].at[me], loc_sems.at[a])
            loc.start()
            local_copies.append(loc)
        for a in range(n):
            for j, (fx, fy) in enumerate(_OTHER_CHIPS):
                to = 2 * _flip(x, fx) + _flip(y, fy)
                cp = pltpu.make_async_remote_copy(ins[a].at[to], outs[a].at[me], send_sems.at[a * 3 + j], recv_sems.at[a * 3 + j],
                                                  device_id=(_flip(x, fx), _flip(y, fy), c), device_id_type=MESH)
                cp.start()
                sends.append(cp)
        for a in range(n):
            for j, (fx, fy) in enumerate(_OTHER_CHIPS):
                src = 2 * _flip(x, fx) + _flip(y, fy)
                pltpu.make_async_remote_copy(ins[a].at[src], outs[a].at[src], send_sems.at[a * 3 + j], recv_sems.at[a * 3 + j],
                                             device_id=(_flip(x, fx), _flip(y, fy), c), device_id_type=MESH).wait_recv()
        for cp in sends:
            cp.wait_send()
        for loc in local_copies:
            loc.wait()

    return pl.pallas_call(
        body, name="scatter_chips", out_shape=[_sds(a.shape, a.dtype) for a in arrs], in_specs=[_ANY] * n, out_specs=[_ANY] * n,
        scratch_shapes=[pltpu.SemaphoreType.DMA((3 * n,)), pltpu.SemaphoreType.DMA((3 * n,)), pltpu.SemaphoreType.DMA((n,))],
        compiler_params=pltpu.CompilerParams(has_side_effects=True))(*arrs)


def _pair_add(name, full, recv, core):
    _, r, c = full.shape
    half = r // 2
    tr = _pick(half, (256, 176, 128, 64, 8))
    nb = half // tr

    def body(c_ref, a_ref, b_ref, o_ref, ob_ref):
        s = a_ref[...] + b_ref[...]
        o_ref[...] = s
        ob_ref[...] = s.astype(BF16)

    blk = pl.BlockSpec((1, tr, c), lambda k, i, cref: (k, i, 0))
    grid_spec = pltpu.PrefetchScalarGridSpec(
        num_scalar_prefetch=1, grid=(N_CHIPS, nb),
        in_specs=[pl.BlockSpec((1, tr, c), lambda k, i, cref: (k, cref[0] * nb + i, 0)), blk], out_specs=[blk, blk])
    return pl.pallas_call(
        body, name=name, out_shape=[_sds((N_CHIPS, half, c)), _sds((N_CHIPS, half, c), BF16)], grid_spec=grid_spec,
        compiler_params=pltpu.CompilerParams(dimension_semantics=("parallel", "parallel"), vmem_limit_bytes=VMEM_LIMIT))(
            core, full, recv)


def _chip_sum(name, landed, own, chip):
    _, r, c = landed.shape
    tr = _pick(r, (256, 176, 128, 64, 16))

    def body(k_ref, l_ref, o_ref, s_ref):
        mine = o_ref[0]
        acc = None
        for k in range(N_CHIPS):
            term = jnp.where(k_ref[0] == k, mine, l_ref[k].astype(F32))
            acc = term if acc is None else acc + term
        s_ref[...] = acc

    grid_spec = pltpu.PrefetchScalarGridSpec(
        num_scalar_prefetch=1, grid=(r // tr,),
        in_specs=[pl.BlockSpec((N_CHIPS, tr, c), lambda i, kref: (0, i, 0)),
                  pl.BlockSpec((1, tr, c), lambda i, kref: (kref[0], i, 0))],
        out_specs=pl.BlockSpec((tr, c), lambda i, kref: (i, 0)))
    return pl.pallas_call(
        body, name=name, out_shape=_sds((r, c)), grid_spec=grid_spec,
        compiler_params=pltpu.CompilerParams(dimension_semantics=("parallel",), vmem_limit_bytes=VMEM_LIMIT))(chip, landed, own)


_WEIGHTS = ("pre_mix_norm", "w_in", "gdn_conv_w", "gdn_a_log", "gdn_dt_bias", "gdn_norm_w", "ssd_conv_w", "ssd_conv_b",
            "ssd_a_log", "ssd_dt_bias", "ssd_d", "ssd_norm_w", "w_out", "post_mix_norm", "pre_ffn_norm", "w_up",
            "ffn_conv_w", "ffn_conv_b", "w_down", "post_ffn_norm")
_BIG = ("w_in", "w_out", "w_up", "w_down")
_COL_SHARDED_SMALL = ("gdn_conv_w", "ssd_conv_w", "ffn_conv_w")
_SMALL = tuple(k for k in _WEIGHTS if k not in _BIG)


def _pack(arrs):
    flat = jnp.concatenate([a.reshape(-1) for a in arrs])
    rows = -(-flat.shape[0] // (8 * LANE)) * 8
    return jnp.pad(flat, (0, rows * LANE - flat.shape[0])).reshape(rows, LANE)


def _unpack(packed, shapes):
    flat = packed.reshape(-1)
    out, off = [], 0
    for shp in shapes:
        size = 1
        for d in shp:
            size *= d
        out.append(flat[off:off + size].reshape(shp))
        off += size
    return out


def _cols_to_chips(a):
    r, c4 = a.shape
    return jnp.transpose(a.reshape(r, N_CHIPS, c4 // N_CHIPS), (1, 0, 2))


def _chips_to_cols(a):
    k, r, c = a.shape
    return jnp.transpose(a, (1, 0, 2)).reshape(r, k * c)


def kernel(x, pre_mix_norm, w_in, gdn_conv_w, gdn_a_log, gdn_dt_bias, gdn_norm_w, ssd_conv_w, ssd_conv_b, ssd_a_log, ssd_dt_bias, ssd_d, ssd_norm_w, w_out, post_mix_norm, pre_ffn_norm, w_up, ffn_conv_w, ffn_conv_b, w_down, post_ffn_norm, loss_target, m_pre_mix_norm, m_w_in, m_gdn_conv_w, m_gdn_a_log, m_gdn_dt_bias, m_gdn_norm_w, m_ssd_conv_w, m_ssd_conv_b, m_ssd_a_log, m_ssd_dt_bias, m_ssd_d, m_ssd_norm_w, m_w_out, m_post_mix_norm, m_pre_ffn_norm, m_w_up, m_ffn_conv_w, m_ffn_conv_b, m_w_down, m_post_ffn_norm, v_pre_mix_norm, v_w_in, v_gdn_conv_w, v_gdn_a_log, v_gdn_dt_bias, v_gdn_norm_w, v_ssd_conv_w, v_ssd_conv_b, v_ssd_a_log, v_ssd_dt_bias, v_ssd_d, v_ssd_norm_w, v_w_out, v_post_mix_norm, v_pre_ffn_norm, v_w_up, v_ffn_conv_w, v_ffn_conv_b, v_w_down, v_post_ffn_norm):
    w = dict(zip(_WEIGHTS, (pre_mix_norm, w_in, gdn_conv_w, gdn_a_log, gdn_dt_bias, gdn_norm_w, ssd_conv_w, ssd_conv_b,
                            ssd_a_log, ssd_dt_bias, ssd_d, ssd_norm_w, w_out, post_mix_norm, pre_ffn_norm, w_up,
                            ffn_conv_w, ffn_conv_b, w_down, post_ffn_norm)))
    m = dict(zip(_WEIGHTS, (m_pre_mix_norm, m_w_in, m_gdn_conv_w, m_gdn_a_log, m_gdn_dt_bias, m_gdn_norm_w, m_ssd_conv_w,
                            m_ssd_conv_b, m_ssd_a_log, m_ssd_dt_bias, m_ssd_d, m_ssd_norm_w, m_w_out, m_post_mix_norm,
                            m_pre_ffn_norm, m_w_up, m_ffn_conv_w, m_ffn_conv_b, m_w_down, m_post_ffn_norm)))
    v = dict(zip(_WEIGHTS, (v_pre_mix_norm, v_w_in, v_gdn_conv_w, v_gdn_a_log, v_gdn_dt_bias, v_gdn_norm_w, v_ssd_conv_w,
                            v_ssd_conv_b, v_ssd_a_log, v_ssd_dt_bias, v_ssd_d, v_ssd_norm_w, v_w_out, v_post_mix_norm,
                            v_pre_ffn_norm, v_w_up, v_ffn_conv_w, v_ffn_conv_b, v_w_down, v_post_ffn_norm)))
    cx, cy, cc = _coords()
    chip = 2 * cx + cy

    shards = [w[k][0].astype(BF16) for k in _BIG] + [w[k][0] for k in _COL_SHARDED_SMALL]
    g_in, g_out, g_up, g_down, g_gcw, g_scw, g_fcw = _gather_chips(shards, [True] * len(_BIG) + [False] * len(_COL_SHARDED_SMALL))
    p = {k: w[k] for k in _SMALL if k not in _COL_SHARDED_SMALL}
    p["w_in"] = _chips_to_cols(g_in)
    p["w_up"] = _chips_to_cols(g_up)
    p["w_out"] = g_out.reshape(-1, D_MODEL)
    p["w_down"] = g_down.reshape(-1, D_MODEL)
    p["gdn_conv_w"] = _chips_to_cols(g_gcw)
    p["ssd_conv_w"] = _chips_to_cols(g_scw)
    p["ffn_conv_w"] = _chips_to_cols(g_fcw)

    loss_acc, grad_x, grads = _local_step(x, loss_target, p)
    loss = lax.psum(loss_acc[0, 0], ("x", "y", "c"))

    small_full_shapes = [grads[k].shape for k in _SMALL]
    summed = _unpack(_allreduce_small(_pack([grads[k] for k in _SMALL])), small_full_shapes)
    g_small = dict(zip(_SMALL, summed))
    for k in _COL_SHARDED_SMALL:
        width = w[k].shape[2]
        g_small[k] = lax.dynamic_slice_in_dim(g_small[k], chip * width, width, axis=1)

    big = [_cols_to_chips(grads["w_in"]), grads["w_out"].reshape(N_CHIPS, -1, D_MODEL),
           _cols_to_chips(grads["w_up"]), grads["w_down"].reshape(N_CHIPS, -1, D_MODEL)]
    from_sibling = _pair_exchange("pair_reduce_send", big, True)
    core = cc.astype(jnp.int32).reshape(1)
    pair_sums = [_pair_add("pair_add_" + k, a, b, core) for k, a, b in zip(_BIG, big, from_sibling)]
    landed = _scatter_chips([ps[1] for ps in pair_sums])
    chip_id = chip.astype(jnp.int32).reshape(1)
    mine = [_chip_sum("chip_sum_" + k, a, ps[0], chip_id) for k, a, ps in zip(_BIG, landed, pair_sums)]
    both = _pair_exchange("pair_gather", mine, False)
    g_big = {k: a.reshape(-1, a.shape[2]) for k, a in zip(_BIG, both)}

    out_g, out_d, out_m, out_v = {}, {}, {}, {}
    for k in _BIG:
        out_g[k] = g_big[k][None]
        d_, m_, v_ = _adamw("adamw_" + k, w[k][0], g_big[k], m[k][0], v[k][0])
        out_d[k], out_m[k], out_v[k] = d_[None], m_[None], v_[None]
    shapes = [w[k].shape for k in _SMALL]
    for k in _SMALL:
        out_g[k] = g_small[k].reshape(w[k].shape)
    packed = [_pack([d[k] for k in _SMALL]) for d in (w, out_g, m, v)]
    d_p, m_p, v_p = _adamw("adamw_small", *packed)
    for dst, src in ((out_d, d_p), (out_m, m_p), (out_v, v_p)):
        dst.update(zip(_SMALL, _unpack(src, shapes)))
    return (loss, grad_x, *[out_g[k] for k in _WEIGHTS], *[out_d[k] for k in _WEIGHTS],
            *[out_m[k] for k in _WEIGHTS], *[out_v[k] for k in _WEIGHTS])
```

```python
import functools

import jax
import jax.numpy as jnp
from jax import lax
from jax.experimental import pallas as pl
from jax.experimental.pallas import tpu as pltpu

F32 = jnp.float32
BF16 = jnp.bfloat16

D_MODEL = 1024
GDN_HEADS = 8
GDN_DK = 128
SSD_HEADS = 16
SSD_HEADDIM = 64
SSD_GROUPS = 2
SSD_STATE = 128
CONV_K = 4
CHUNK = 64
D_FF = 2816
FFN_CONV_K = 3
EPS = 1e-6
GDN_QK = GDN_HEADS * GDN_DK
GDN_V = GDN_QK
SSD_D = SSD_HEADS * SSD_HEADDIM
SSD_BC = SSD_GROUPS * SSD_STATE
SSD_CONV_CH = SSD_D + 2 * SSD_BC
BIG = 4 * 1024 + 1024 + SSD_CONV_CH
SMALL = 128
D_IN_PROJ = 6688
LANE = 128
PAIR = 2 * CHUNK
NEG = -1e30
VMEM_LIMIT = 56 * 1024 * 1024

ADAM_LR = 0.001
ADAM_B1 = 0.9
ADAM_B2 = 0.999
ADAM_EPS = 1e-08
ADAM_WD = 0.01
ADAM_STEP = 10

N_CHIPS = 4
N_DEV = 8
MESH = pl.DeviceIdType.MESH

NN = ((1,), (0,))
NT = ((1,), (1,))
TN = ((0,), (0,))


def _bdot(a, b, dims):
    return lax.dot_general(a.astype(BF16), b.astype(BF16), (dims, ((), ())), preferred_element_type=F32)


def _split3(a):
    hi = a.astype(BF16)
    r1 = a - hi.astype(F32)
    mid = r1.astype(BF16)
    return hi, mid, (r1 - mid.astype(F32)).astype(BF16)


@jax.custom_vjp
def _nn(a, b):
    return _bdot(a, b, NN)


@jax.custom_vjp
def _nt(a, b):
    return _bdot(a, b, NT)


@jax.custom_vjp
def _tn(a, b):
    return _bdot(a, b, TN)


_nn.defvjp(lambda a, b: (_nn(a, b), (a, b)), lambda r, g: (_nt(g, r[1]), _tn(r[0], g)))
_nt.defvjp(lambda a, b: (_nt(a, b), (a, b)), lambda r, g: (_nn(g, r[1]), _tn(g, r[0])))
_tn.defvjp(lambda a, b: (_tn(a, b), (a, b)), lambda r, g: (_nt(r[1], g), _nn(r[0], g)))


def _mask_dot(x, mask, dims, x_first):
    acc = None
    for piece in _split3(x):
        term = _bdot(piece, mask, dims) if x_first else _bdot(mask, piece, dims)
        acc = term if acc is None else acc + term
    return acc


@jax.custom_vjp
def _cst_left(cst, x):
    return _mask_dot(x, cst, NN, False)


_cst_left.defvjp(lambda cst, x: (_cst_left(cst, x), cst), lambda cst, g: (jnp.zeros_like(cst), _mask_dot(g, cst, TN, False)))


@jax.custom_vjp
def _cst_right(x, cst):
    return _mask_dot(x, cst, NN, True)


_cst_right.defvjp(lambda x, cst: (_cst_right(x, cst), cst), lambda cst, g: (_mask_dot(g, cst, NT, True), jnp.zeros_like(cst)))


def _lin_left(cst):
    return functools.partial(_cst_left, cst)


def _lin_right(cst):
    return lambda x: _cst_right(x, cst)


@jax.custom_vjp
def _tri_inv_m1(a):
    pm = [-x for x in a]
    ap = list(a)
    for _ in range(5):
        ap = [_bdot(x, x, NN) for x in ap]
        pm = [(p + x) + _bdot(p, x, NN) for p, x in zip(pm, ap)]
    return pm


def _tri_inv_m1_bwd(pm, g):
    t = [gi + _bdot(p, gi, TN) for p, gi in zip(pm, g)]
    return ([-(ti + _bdot(ti, p, NT)) for p, ti in zip(pm, t)],)


_tri_inv_m1.defvjp(lambda a: (lambda pm: (pm, pm))(_tri_inv_m1(a)), _tri_inv_m1_bwd)


@jax.custom_vjp
def _top(x):
    return x[: x.shape[0] // 2]


_top.defvjp(lambda x: (_top(x), None), lambda _, g: (jnp.concatenate([g, jnp.zeros_like(g)], axis=0),))


@jax.custom_vjp
def _bot(x):
    return x[x.shape[0] // 2:]


_bot.defvjp(lambda x: (_bot(x), None), lambda _, g: (jnp.concatenate([jnp.zeros_like(g), g], axis=0),))


@jax.custom_vjp
def _vstack(a, b):
    return jnp.concatenate([a, b], axis=0)


_vstack.defvjp(lambda a, b: (_vstack(a, b), None), lambda _, g: (g[: g.shape[0] // 2], g[g.shape[0] // 2:]))


def _shift_dn_raw(x, s):
    if s == 0:
        return x
    r = pltpu.roll(x, s, axis=0)
    ri = lax.broadcasted_iota(jnp.int32, x.shape, 0)
    return jnp.where(ri >= s, r, 0.0)


def _shift_up_raw(x, s):
    if s == 0:
        return x
    n = x.shape[0]
    r = pltpu.roll(x, n - s, axis=0)
    ri = lax.broadcasted_iota(jnp.int32, x.shape, 0)
    return jnp.where(ri < n - s, r, 0.0)


@functools.partial(jax.custom_vjp, nondiff_argnums=(1,))
def _shift_dn(x, s):
    return _shift_dn_raw(x, s)


_shift_dn.defvjp(lambda x, s: (_shift_dn_raw(x, s), None), lambda s, _, g: (_shift_up_raw(g, s),))


def _conv(x, wrows):
    k_w = len(wrows)
    acc = wrows[k_w - 1] * x
    for k in range(k_w - 1):
        acc = acc + wrows[k] * _shift_dn(x, k_w - 1 - k)
    return acc


def _silu(x):
    return x * jax.nn.sigmoid(x)


def _rms(x, w):
    return x * lax.rsqrt(jnp.mean(x * x, axis=-1, keepdims=True) + EPS) * w


def _l2n(x):
    return x * lax.rsqrt(jnp.sum(x * x, axis=-1, keepdims=True) + EPS)


def _iota2(shape):
    return lax.broadcasted_iota(jnp.int32, shape, 0), lax.broadcasted_iota(jnp.int32, shape, 1)


_GDN_NARGS = 15
_SSD_NARGS = 8


def _gdn_multi(*flat):
    pairs = [flat[i:i + _GDN_NARGS] for i in range(0, len(flat), _GDN_NARGS)]
    idx = range(len(pairs))
    ri, ci = _iota2((PAIR, PAIR))
    blk = ((ri >= CHUNK) & (ci >= CHUNK)) | ((ri < CHUNK) & (ci < CHUNK))
    causal = blk & (ri >= ci)
    strict = blk & (ri > ci)
    q = [_vstack(p[0], p[1]) for p in pairs]
    k = [_vstack(p[2], p[3]) for p in pairs]
    v = [_vstack(p[4], p[5]) for p in pairs]
    gc = [_vstack(p[6], p[7]) for p in pairs]
    beta = [_vstack(p[8], p[9]) for p in pairs]
    glast = [_vstack(jnp.broadcast_to(p[11], (CHUNK, LANE)), jnp.broadcast_to(p[12], (CHUNK, LANE))) for p in pairs]
    sa = [p[13] for p in pairs]
    sb = [p[14] for p in pairs]
    decay = [jnp.exp(jnp.where(causal, gc[i] - jnp.broadcast_to(pairs[i][10], (PAIR, PAIR)), NEG)) for i in idx]
    eg = [jnp.exp(x) for x in gc]
    kbeta = [k[i] * beta[i] for i in idx]
    pm = _tri_inv_m1([jnp.where(strict, _nt(kbeta[i], k[i]) * decay[i], 0.0) for i in idx])
    qk = [_nt(q[i], k[i]) * decay[i] for i in idx]
    rhs_v = [v[i] * beta[i] for i in idx]
    rhs_k = [kbeta[i] * eg[i] for i in idx]
    u = [rhs_v[i] + _nn(pm[i], rhs_v[i]) for i in idx]
    w = [rhs_k[i] + _nn(pm[i], rhs_k[i]) for i in idx]
    q_dec = [q[i] * eg[i] for i in idx]
    k_dec = [k[i] * jnp.exp(glast[i] - gc[i]) for i in idx]
    gl = [jnp.exp(x) for x in glast]
    w_s = [_vstack(_nn(_top(w[i]), sa[i]), _nn(_bot(w[i]), sb[i])) for i in idx]
    q_s = [_vstack(_nn(_top(q_dec[i]), sa[i]), _nn(_bot(q_dec[i]), sb[i])) for i in idx]
    v_new = [u[i] - w_s[i] for i in idx]
    o = [q_s[i] + _nn(qk[i], v_new[i]) for i in idx]
    sa2 = [sa[i] * _vstack(_top(gl[i]), _top(gl[i])) + _tn(_top(k_dec[i]), _top(v_new[i])) for i in idx]
    sb2 = [sb[i] * _vstack(_bot(gl[i]), _bot(gl[i])) + _tn(_bot(k_dec[i]), _bot(v_new[i])) for i in idx]
    out = []
    for i in idx:
        out += [_top(o[i]), _bot(o[i]), sa2[i], sb2[i]]
    return tuple(out)


def _ssd_multi(*flat):
    pairs = [flat[i:i + _SSD_NARGS] for i in range(0, len(flat), _SSD_NARGS)]
    idx = range(len(pairs))
    ri, ci = _iota2((CHUNK, PAIR))
    causal = ri >= jnp.where(ci >= CHUNK, ci - CHUNK, ci)
    xdt = [p[0] * p[1] for p in pairs]
    acs = [p[2] for p in pairs]
    alast = [jnp.broadcast_to(p[4], (CHUNK, PAIR)) for p in pairs]
    lmat = [jnp.exp(jnp.where(causal, acs[i] - jnp.broadcast_to(pairs[i][3], (CHUNK, PAIR)), NEG)) for i in idx]
    cb2 = [_nt(p[6], _vstack(p[5], p[5])) for p in pairs]
    xblk = [_vstack(jnp.where(ci < CHUNK, x, 0.0), jnp.where(ci >= CHUNK, x, 0.0)) for x in xdt]
    y_off = [_nn(pairs[i][6], pairs[i][7]) * jnp.exp(acs[i]) for i in idx]
    y = [_nn(cb2[i] * lmat[i], xblk[i]) + y_off[i] for i in idx]
    el = [jnp.exp(x) for x in alast]
    st2 = [pairs[i][7] * _vstack(el[i], el[i]) + _tn(pairs[i][5], xdt[i] * jnp.exp(alast[i] - acs[i])) for i in idx]
    out = []
    for i in idx:
        out += [y[i], st2[i]]
    return tuple(out)


def _pcall(name, body, grid, in_specs, out_specs, out_shape, scratch=(), sem=None, aliases=None):
    if sem is None:
        sem = ("arbitrary",) * len(grid)
    return pl.pallas_call(
        functools.partial(body),
        out_shape=out_shape,
        grid=grid,
        in_specs=in_specs,
        out_specs=out_specs,
        scratch_shapes=scratch,
        input_output_aliases=aliases or {},
        name=name,
        compiler_params=pltpu.CompilerParams(dimension_semantics=sem, vmem_limit_bytes=VMEM_LIMIT),
    )


def _sds(shape, dtype=F32):
    return jax.ShapeDtypeStruct(shape, dtype)


def _row_spec(tm, width, colblock=0):
    return pl.BlockSpec((tm, width), lambda i, _c=colblock: (i, _c))


def _full_spec(shape):
    nd = len(shape)
    return pl.BlockSpec(shape, lambda *_: (0,) * nd)


def _zero_at_first(refs, first):
    @pl.when(first)
    def _():
        for r in refs:
            r[...] = jnp.zeros(r.shape, r.dtype)


def _pick(n, prefs):
    for p in prefs:
        if n % p == 0:
            return p
    return n


def _matmul(name, a, b, mode, out_dtype):
    if mode == "tn":
        r, m = a.shape
        n = b.shape[1]
        tm = _pick(m, (1024, 1408))
        tn = _pick(n, (512, 256, 128))
        tk = _pick(r, (1024, 512, 256, 128, 64))

        def body(a_ref, b_ref, o_ref):
            _zero_at_first([o_ref], pl.program_id(2) == 0)
            o_ref[...] += _bdot(a_ref[...], b_ref[...], TN)

        return _pcall(
            name, body, (m // tm, n // tn, r // tk),
            [pl.BlockSpec((tk, tm), lambda i, j, k: (k, i)), pl.BlockSpec((tk, tn), lambda i, j, k: (k, j))],
            pl.BlockSpec((tm, tn), lambda i, j, k: (i, j)), _sds((m, n), out_dtype),
            sem=("parallel", "parallel", "arbitrary"))(a, b)
    m, k = a.shape
    n = b.shape[1] if mode == "nn" else b.shape[0]
    tm = _pick(m, (1024, 512, 256, 128, 64) if k <= 2816 else (512, 256, 128, 64))
    tn = _pick(n, (512, 256, 128))
    dims = NN if mode == "nn" else NT

    def body(a_ref, b_ref, o_ref):
        o_ref[...] = _bdot(a_ref[...], b_ref[...], dims).astype(o_ref.dtype)

    b_spec = pl.BlockSpec((k, tn), lambda i, j: (0, j)) if mode == "nn" else pl.BlockSpec((tn, k), lambda i, j: (j, 0))
    return _pcall(
        name, body, (m // tm, n // tn), [pl.BlockSpec((tm, k), lambda i, j: (i, 0)), b_spec],
        pl.BlockSpec((tm, tn), lambda i, j: (i, j)), _sds((m, n), out_dtype), sem=("parallel", "parallel"))(a, b)


def _row_tile(t):
    return _pick(t, (256, 128, 64))


def _rms_fwd(name, x, g):
    t = x.shape[0]
    tm = _row_tile(t)

    def body(x_ref, g_ref, h_ref):
        h_ref[...] = _rms(x_ref[...], g_ref[...]).astype(BF16)

    return _pcall(name, body, (t // tm,), [_row_spec(tm, D_MODEL), _full_spec((1, D_MODEL))], _row_spec(tm, D_MODEL),
                  _sds((t, D_MODEL), BF16), sem=("parallel",))(x, g)


_G_LO, _G_HI = GDN_HEADS, 2 * GDN_HEADS
_S_LO, _S_HI = 2 * GDN_HEADS, 2 * GDN_HEADS + SSD_HEADS


def _gates_fn(small, bias, a_log):
    tm = small.shape[0]
    r, c = _iota2((SMALL, D_MODEL))
    e_b = (r == (c >> 7)).astype(F32)
    e_a = (r == _G_LO + (c >> 7)).astype(F32)
    e_dt = (r == _S_LO + (c >> 6)).astype(F32)
    rr, cc = _iota2((tm, tm))
    in_chunk_tril = (((rr >> 6) == (cc >> 6)) & (rr >= cc)).astype(F32)
    sig = jax.nn.sigmoid(small)
    sp = jax.nn.softplus(small + bias)
    cum = _lin_left(in_chunk_tril)(-jnp.exp(a_log) * sp)
    return _lin_right(e_b)(sig), _lin_right(e_a)(cum), _lin_right(e_dt)(sp), _lin_right(e_dt)(cum), cum


def _gates_fwd(small, bias, a_log):
    t = small.shape[0]
    tm = _row_tile(t)

    def body(s_ref, p0, p1, *outs):
        for o, v in zip(outs, _gates_fn(s_ref[...], p0[...], p1[...])):
            o[...] = v

    pspec = _full_spec((1, SMALL))
    return _pcall("gates_fwd", body, (t // tm,), [_row_spec(tm, SMALL), pspec, pspec],
                  [_row_spec(tm, D_MODEL)] * 4 + [_row_spec(tm, SMALL)], [_sds((t, D_MODEL))] * 4 + [_sds((t, SMALL))],
                  sem=("parallel",))(small, bias, a_log)


def _gates_bwd(small, bias, a_log, cts):
    t = small.shape[0]
    tm = _row_tile(t)

    def body(s_ref, p0, p1, c0, c1, c2, c3, c4, ds_ref, db_ref, da_ref):
        _zero_at_first([db_ref, da_ref], pl.program_id(0) == 0)
        _, vjp = jax.vjp(_gates_fn, s_ref[...], p0[...], p1[...])
        d_s, d_b, d_a = vjp((c0[...], c1[...], c2[...], c3[...], c4[...]))
        ds_ref[...] = d_s.astype(BF16)
        db_ref[...] += d_b
        da_ref[...] += d_a

    pspec = _full_spec((1, SMALL))
    return _pcall("gates_bwd", body, (t // tm,),
                  [_row_spec(tm, SMALL), pspec, pspec] + [_row_spec(tm, D_MODEL)] * 4 + [_row_spec(tm, SMALL)],
                  [_row_spec(tm, SMALL), pspec, pspec], [_sds((t, SMALL), BF16), _sds((1, SMALL)), _sds((1, SMALL))])(
                      small, bias, a_log, *cts)


def _gdn_out_fn(o, z, w):
    return _rms(o, w) * _silu(z)


def _gdn_out_fwd(o, proj, gn):
    t = o.shape[0]
    tm = _row_tile(t)

    def body(o_ref, z_ref, w_ref, y_ref):
        for h in range(GDN_HEADS):
            sl = slice(h * GDN_DK, (h + 1) * GDN_DK)
            y_ref[:, sl] = _gdn_out_fn(o_ref[:, sl], z_ref[:, sl], w_ref[...]).astype(BF16)

    return _pcall("gdn_out_fwd", body, (t // tm,), [_row_spec(tm, GDN_V), _row_spec(tm, GDN_V, 3), _full_spec((1, GDN_DK))],
                  _row_spec(tm, GDN_V), _sds((t, GDN_V), BF16), sem=("parallel",))(o, proj, gn)


def _gdn_out_bwd(o, proj, gn, d_ocat):
    t = o.shape[0]
    tm = _row_tile(t)

    def body(o_ref, z_ref, w_ref, dy_ref, do_ref, dz_ref, dw_ref):
        _zero_at_first([dw_ref], pl.program_id(0) == 0)
        for h in range(GDN_HEADS):
            sl = slice(h * GDN_DK, (h + 1) * GDN_DK)
            _, vjp = jax.vjp(_gdn_out_fn, o_ref[:, sl], z_ref[:, sl], w_ref[...])
            d_o, d_z, d_w = vjp(dy_ref[:, sl])
            do_ref[:, sl] = d_o
            dz_ref[:, sl] = d_z.astype(BF16)
            dw_ref[...] += d_w

    return _pcall("gdn_out_bwd", body, (t // tm,),
                  [_row_spec(tm, GDN_V), _row_spec(tm, GDN_V, 3), _full_spec((1, GDN_DK)), _row_spec(tm, GDN_V, 0)],
                  [_row_spec(tm, GDN_V), _row_spec(tm, GDN_V), _full_spec((1, GDN_DK))],
                  [_sds((t, GDN_V)), _sds((t, GDN_V), BF16), _sds((1, GDN_DK))])(o, proj, gn, d_ocat)


def _ssd_out_fn(y, xs, z, d_skip, w):
    return _rms((y + d_skip * xs) * _silu(z), w)


_SSD_GW = SSD_D // SSD_GROUPS


def _ssd_out_fwd(y, xbc, proj, d_skip, nw):
    t = y.shape[0]
    tm = _row_tile(t)

    def body(y_ref, x_ref, z_ref, d_ref, w_ref, o_ref):
        for gi in range(SSD_GROUPS):
            sl = slice(gi * _SSD_GW, (gi + 1) * _SSD_GW)
            o_ref[:, sl] = _ssd_out_fn(y_ref[:, sl], x_ref[:, sl], z_ref[:, sl], d_ref[:, sl], w_ref[:, sl]).astype(BF16)

    pspec = _full_spec((1, SSD_D))
    return _pcall("ssd_out_fwd", body, (t // tm,),
                  [_row_spec(tm, SSD_D), _row_spec(tm, SSD_D, 0), _row_spec(tm, SSD_D, 4), pspec, pspec],
                  _row_spec(tm, SSD_D), _sds((t, SSD_D), BF16), sem=("parallel",))(y, xbc, proj, d_skip, nw)


def _ssd_out_bwd(y, xbc, proj, d_skip, nw, d_ocat):
    t = y.shape[0]
    tm = _row_tile(t)

    def body(y_ref, x_ref, z_ref, d_ref, w_ref, do_ref, dy_ref, dx_ref, dz_ref, dd_ref, dw_ref):
        _zero_at_first([dd_ref, dw_ref], pl.program_id(0) == 0)
        for gi in range(SSD_GROUPS):
            sl = slice(gi * _SSD_GW, (gi + 1) * _SSD_GW)
            _, vjp = jax.vjp(_ssd_out_fn, y_ref[:, sl], x_ref[:, sl], z_ref[:, sl], d_ref[:, sl], w_ref[:, sl])
            d_y, d_x, d_z, d_d, d_w = vjp(do_ref[:, sl])
            dy_ref[:, sl] = d_y
            dx_ref[:, sl] = d_x
            dz_ref[:, sl] = d_z.astype(BF16)
            dd_ref[:, sl] += d_d
            dw_ref[:, sl] += d_w

    pspec = _full_spec((1, SSD_D))
    row = _row_spec(tm, SSD_D)
    return _pcall("ssd_out_bwd", body, (t // tm,),
                  [row, _row_spec(tm, SSD_D, 0), _row_spec(tm, SSD_D, 4), pspec, pspec, _row_spec(tm, SSD_D, 1)],
                  [row, row, row, pspec, pspec],
                  [_sds((t, SSD_D)), _sds((t, SSD_D)), _sds((t, SSD_D), BF16), _sds((1, SSD_D)), _sds((1, SSD_D))])(
                      y, xbc, proj, d_skip, nw, d_ocat)


def _res1_fn(x, mix, g_pm, g_pf):
    x1 = x + _rms(mix, g_pm)
    return x1, _rms(x1, g_pf)


def _res1_fwd(x, mix, g_pm, g_pf):
    t = x.shape[0]
    tm = _row_tile(t)

    def body(x_ref, m_ref, a_ref, b_ref, x1_ref, h2_ref):
        x1, h2 = _res1_fn(x_ref[...], m_ref[...], a_ref[...], b_ref[...])
        x1_ref[...] = x1
        h2_ref[...] = h2.astype(BF16)

    row, pspec = _row_spec(tm, D_MODEL), _full_spec((1, D_MODEL))
    return _pcall("res1_fwd", body, (t // tm,), [row, row, pspec, pspec], [row, row],
                  [_sds((t, D_MODEL)), _sds((t, D_MODEL), BF16)], sem=("parallel",))(x, mix, g_pm, g_pf)


def _res1_bwd(x, mix, g_pm, g_pf, d_x1, d_h2):
    t = x.shape[0]
    tm = _row_tile(t)

    def body(x_ref, m_ref, a_ref, b_ref, c1_ref, c2_ref, dx_ref, dm_ref, da_ref, db_ref):
        _zero_at_first([da_ref, db_ref], pl.program_id(0) == 0)
        _, vjp = jax.vjp(_res1_fn, x_ref[...], m_ref[...], a_ref[...], b_ref[...])
        d_x, d_m, d_a, d_b = vjp((c1_ref[...], c2_ref[...]))
        dx_ref[...] = d_x
        dm_ref[...] = d_m.astype(BF16)
        da_ref[...] += d_a
        db_ref[...] += d_b

    row, pspec = _row_spec(tm, D_MODEL), _full_spec((1, D_MODEL))
    return _pcall("res1_bwd", body, (t // tm,), [row, row, pspec, pspec, row, row], [row, row, pspec, pspec],
                  [_sds((t, D_MODEL)), _sds((t, D_MODEL), BF16), _sds((1, D_MODEL)), _sds((1, D_MODEL))])(
                      x, mix, g_pm, g_pf, d_x1, d_h2)


def _final_fn(x1, f, g_po, tgt):
    err = x1 + _rms(f, g_po) - tgt
    return 0.5 * jnp.sum(jnp.mean(err * err, axis=-1))


def _final(x1, f, g_po, tgt):
    t = x1.shape[0]
    tm = _row_tile(t)

    def body(x_ref, f_ref, g_ref, t_ref, loss_ref, dx_ref, df_ref, dg_ref):
        _zero_at_first([loss_ref, dg_ref], pl.program_id(0) == 0)
        loss, (d_x, d_f, d_g) = jax.value_and_grad(_final_fn, argnums=(0, 1, 2))(x_ref[...], f_ref[...], g_ref[...], t_ref[...])
        loss_ref[...] += jnp.broadcast_to(loss, loss_ref.shape)
        dx_ref[...] = d_x
        df_ref[...] = d_f.astype(BF16)
        dg_ref[...] += d_g

    row, pspec = _row_spec(tm, D_MODEL), _full_spec((1, D_MODEL))
    return _pcall("final", body, (t // tm,), [row, row, pspec, row], [_full_spec((8, LANE)), row, row, pspec],
                  [_sds((8, LANE)), _sds((t, D_MODEL)), _sds((t, D_MODEL), BF16), _sds((1, D_MODEL))])(x1, f, g_po, tgt)


def _rms1_bwd(x, g, d_h, d_x1):
    t = x.shape[0]
    tm = _row_tile(t)

    def body(x_ref, g_ref, dh_ref, dx1_ref, dx_ref, dg_ref):
        _zero_at_first([dg_ref], pl.program_id(0) == 0)
        _, vjp = jax.vjp(_rms, x_ref[...], g_ref[...])
        d_x, d_g = vjp(dh_ref[...])
        dx_ref[...] = d_x + dx1_ref[...]
        dg_ref[...] += d_g

    row, pspec = _row_spec(tm, D_MODEL), _full_spec((1, D_MODEL))
    return _pcall("rms1_bwd", body, (t // tm,), [row, pspec, row, row], [row, pspec],
                  [_sds((t, D_MODEL)), _sds((1, D_MODEL))])(x, g, d_h, d_x1)


def _qkv_fn(mode):
    def fn(x, *wrows):
        y = _silu(_conv(x, wrows))
        if mode == "q":
            return _l2n(y) * (GDN_DK ** -0.5)
        if mode == "k":
            return _l2n(y)
        return y
    return fn


def _seq_spec(s, tc, off):
    return pl.BlockSpec((s, tc), lambda j, b, _o=off: (b, _o + j))


def _par_spec(rows, tc, off):
    return pl.BlockSpec((rows, tc), lambda j, b, _o=off: (0, _o + j))


def _gdn_conv_fwd(mode, proj, w, bsz, s):
    off = {"q": 0, "k": GDN_HEADS, "v": 2 * GDN_HEADS}[mode]
    fn = _qkv_fn(mode)

    def body(x_ref, w_ref, y_ref):
        y_ref[...] = fn(x_ref[...], *[w_ref[k:k + 1, :] for k in range(CONV_K)])

    return _pcall("gdn_conv_fwd_" + mode, body, (GDN_HEADS, bsz),
                  [_seq_spec(s, GDN_DK, off), _par_spec(CONV_K, GDN_DK, off)], _seq_spec(s, GDN_DK, 0),
                  _sds((bsz * s, GDN_QK)), sem=("parallel", "parallel"))(proj, w)


def _gdn_conv_bwd(mode, proj, w, d_y, bsz, s):
    off = {"q": 0, "k": GDN_HEADS, "v": 2 * GDN_HEADS}[mode]
    fn = _qkv_fn(mode)

    def body(x_ref, w_ref, dy_ref, dx_ref, dw_ref):
        _zero_at_first([dw_ref], pl.program_id(1) == 0)
        _, vjp = jax.vjp(fn, x_ref[...], *[w_ref[k:k + 1, :] for k in range(CONV_K)])
        grads = vjp(dy_ref[...])
        dx_ref[...] = grads[0].astype(BF16)
        for k in range(CONV_K):
            dw_ref[k:k + 1, :] += grads[1 + k]

    return _pcall("gdn_conv_bwd_" + mode, body, (GDN_HEADS, bsz),
                  [_seq_spec(s, GDN_DK, off), _par_spec(CONV_K, GDN_DK, off), _seq_spec(s, GDN_DK, 0)],
                  [_seq_spec(s, GDN_DK, 0), _par_spec(CONV_K, GDN_DK, 0)],
                  [_sds((bsz * s, GDN_QK), BF16), _sds((CONV_K, GDN_QK))], sem=("parallel", "arbitrary"))(proj, w, d_y)


def _ssd_conv_fn(x, bias, *wrows):
    return _silu(_conv(x, wrows) + bias)


_XBC_OFF = (5 * 1024) // LANE


def _ssd_conv_fwd(proj, w, bias, bsz, s):
    nt_ = SSD_CONV_CH // LANE

    def body(x_ref, w_ref, b_ref, y_ref):
        y_ref[...] = _ssd_conv_fn(x_ref[...], b_ref[...], *[w_ref[k:k + 1, :] for k in range(CONV_K)])

    return _pcall("ssd_conv_fwd", body, (nt_, bsz),
                  [_seq_spec(s, LANE, _XBC_OFF), _par_spec(CONV_K, LANE, 0), _par_spec(1, LANE, 0)], _seq_spec(s, LANE, 0),
                  _sds((bsz * s, SSD_CONV_CH)), sem=("parallel", "parallel"))(proj, w, bias)


def _ssd_conv_bwd(proj, w, bias, d_y, bsz, s):
    nt_ = SSD_CONV_CH // LANE

    def body(x_ref, w_ref, b_ref, dy_ref, dx_ref, dw_ref, db_ref):
        _zero_at_first([dw_ref, db_ref], pl.program_id(1) == 0)
        _, vjp = jax.vjp(_ssd_conv_fn, x_ref[...], b_ref[...], *[w_ref[k:k + 1, :] for k in range(CONV_K)])
        grads = vjp(dy_ref[...])
        dx_ref[...] = grads[0].astype(BF16)
        db_ref[...] += grads[1]
        for k in range(CONV_K):
            dw_ref[k:k + 1, :] += grads[2 + k]

    return _pcall("ssd_conv_bwd", body, (nt_, bsz),
                  [_seq_spec(s, LANE, _XBC_OFF), _par_spec(CONV_K, LANE, 0), _par_spec(1, LANE, 0), _seq_spec(s, LANE, 0)],
                  [_seq_spec(s, LANE, 0), _par_spec(CONV_K, LANE, 0), _par_spec(1, LANE, 0)],
                  [_sds((bsz * s, SSD_CONV_CH), BF16), _sds((CONV_K, SSD_CONV_CH)), _sds((1, SSD_CONV_CH))],
                  sem=("parallel", "arbitrary"))(proj, w, bias, d_y)


_FFN_TC = 256
_FFN_NT = D_FF // _FFN_TC


def _ffn_act_fn(xg, xu, bg, bu, *wrows):
    k_w = FFN_CONV_K
    gate = _conv(xg, wrows[:k_w]) + bg
    up = _conv(xu, wrows[k_w:]) + bu
    return _silu(gate) * up


def _ffn_act_fwd(u_pre, w, bias, bsz, s):
    def body(xg_ref, xu_ref, wg_ref, wu_ref, bg_ref, bu_ref, a_ref):
        rows = [wg_ref[k:k + 1, :] for k in range(FFN_CONV_K)] + [wu_ref[k:k + 1, :] for k in range(FFN_CONV_K)]
        a_ref[...] = _ffn_act_fn(xg_ref[...], xu_ref[...], bg_ref[...], bu_ref[...], *rows).astype(BF16)

    return _pcall("ffn_act_fwd", body, (_FFN_NT, bsz),
                  [_seq_spec(s, _FFN_TC, 0), _seq_spec(s, _FFN_TC, _FFN_NT),
                   _par_spec(FFN_CONV_K, _FFN_TC, 0), _par_spec(FFN_CONV_K, _FFN_TC, _FFN_NT),
                   _par_spec(1, _FFN_TC, 0), _par_spec(1, _FFN_TC, _FFN_NT)],
                  _seq_spec(s, _FFN_TC, 0), _sds((bsz * s, D_FF), BF16), sem=("parallel", "parallel"))(
                      u_pre, u_pre, w, w, bias, bias)


def _ffn_act_bwd(u_pre, w, bias, d_a, bsz, s):
    def body(xg_ref, xu_ref, wg_ref, wu_ref, bg_ref, bu_ref, da_ref, dg_ref, du_ref, dwg_ref, dwu_ref, dbg_ref, dbu_ref):
        _zero_at_first([dwg_ref, dwu_ref, dbg_ref, dbu_ref], pl.program_id(1) == 0)
        rows = [wg_ref[k:k + 1, :] for k in range(FFN_CONV_K)] + [wu_ref[k:k + 1, :] for k in range(FFN_CONV_K)]
        _, vjp = jax.vjp(_ffn_act_fn, xg_ref[...], xu_ref[...], bg_ref[...], bu_ref[...], *rows)
        grads = vjp(da_ref[...])
        dg_ref[...] = grads[0].astype(BF16)
        du_ref[...] = grads[1].astype(BF16)
        dbg_ref[...] += grads[2]
        dbu_ref[...] += grads[3]
        for k in range(FFN_CONV_K):
            dwg_ref[k:k + 1, :] += grads[4 + k]
            dwu_ref[k:k + 1, :] += grads[4 + FFN_CONV_K + k]

    seq0, par3, par1 = _seq_spec(s, _FFN_TC, 0), _par_spec(FFN_CONV_K, _FFN_TC, 0), _par_spec(1, _FFN_TC, 0)
    return _pcall("ffn_act_bwd", body, (_FFN_NT, bsz),
                  [seq0, _seq_spec(s, _FFN_TC, _FFN_NT), par3, _par_spec(FFN_CONV_K, _FFN_TC, _FFN_NT),
                   par1, _par_spec(1, _FFN_TC, _FFN_NT), seq0],
                  [seq0, seq0, par3, par3, par1, par1],
                  [_sds((bsz * s, D_FF), BF16), _sds((bsz * s, D_FF), BF16), _sds((FFN_CONV_K, D_FF)), _sds((FFN_CONV_K, D_FF)),
                   _sds((1, D_FF)), _sds((1, D_FF))], sem=("parallel", "arbitrary"))(u_pre, u_pre, w, w, bias, bias, d_a)


_GP = GDN_HEADS // 2
_SP = SSD_HEADS // 2


def _pair_lanes(p):
    return slice(2 * p * LANE, (2 * p + 1) * LANE), slice((2 * p + 1) * LANE, (2 * p + 2) * LANE)


_LAST = slice(CHUNK - 1, CHUNK)


def _gdn_args(p, q_ref, k_ref, v_ref, g_ref, b_ref, gr_ref):
    la, lb = _pair_lanes(p)
    return (q_ref[:, la], q_ref[:, lb], k_ref[:, la], k_ref[:, lb], v_ref[:, la], v_ref[:, lb], g_ref[:, la], g_ref[:, lb],
            b_ref[:, la], b_ref[:, lb], gr_ref[p], g_ref[_LAST, la], g_ref[_LAST, lb])


def _gdn_fwd(q, k, v, gc, beta, gc_row, bsz, n):
    def body(q_ref, k_ref, v_ref, g_ref, b_ref, gr_ref, o_ref, sin_ref, s_scr):
        _zero_at_first([s_scr], pl.program_id(1) == 0)
        flat = []
        for p in range(_GP):
            flat += [*_gdn_args(p, q_ref, k_ref, v_ref, g_ref, b_ref, gr_ref), s_scr[2 * p], s_scr[2 * p + 1]]
        sin_ref[...] = s_scr[...]
        outs = _gdn_multi(*flat)
        for p in range(_GP):
            la, lb = _pair_lanes(p)
            o_ref[:, la], o_ref[:, lb], s_scr[2 * p], s_scr[2 * p + 1] = outs[4 * p:4 * p + 4]

    tspec = pl.BlockSpec((CHUNK, GDN_V), lambda b, c: (b * n + c, 0))
    rspec = pl.BlockSpec((_GP, 1, LANE), lambda b, c: (b * n + c, 0, 0))
    sspec = pl.BlockSpec((GDN_HEADS, LANE, LANE), lambda b, c: (b * n + c, 0, 0))
    return _pcall("gdn_fwd", body, (bsz, n), [tspec] * 5 + [rspec], [tspec, sspec],
                  [_sds((bsz * n * CHUNK, GDN_V)), _sds((bsz * n * GDN_HEADS, LANE, LANE))],
                  scratch=[pltpu.VMEM((GDN_HEADS, LANE, LANE), F32)], sem=("parallel", "arbitrary"))(q, k, v, gc, beta, gc_row)


def _gdn_bwd(q, k, v, gc, beta, gc_row, s_in, d_o, bsz, n):
    def body(q_ref, k_ref, v_ref, g_ref, b_ref, gr_ref, sin_ref, do_ref, dq_ref, dk_ref, dv_ref, dg_ref, db_ref, dgr_ref, ds_scr):
        _zero_at_first([ds_scr], pl.program_id(1) == 0)
        flat, cots = [], []
        for p in range(_GP):
            la, lb = _pair_lanes(p)
            flat += [*_gdn_args(p, q_ref, k_ref, v_ref, g_ref, b_ref, gr_ref), sin_ref[2 * p], sin_ref[2 * p + 1]]
            cots += [do_ref[:, la], do_ref[:, lb], ds_scr[2 * p], ds_scr[2 * p + 1]]
        _, vjp = jax.vjp(_gdn_multi, *flat)
        grads = vjp(tuple(cots))
        for p in range(_GP):
            la, lb = _pair_lanes(p)
            cts = grads[_GDN_NARGS * p:_GDN_NARGS * (p + 1)]
            for ref, i in ((dq_ref, 0), (dk_ref, 2), (dv_ref, 4), (dg_ref, 6), (db_ref, 8)):
                ref[:, la] = cts[i]
                ref[:, lb] = cts[i + 1]
            dgr_ref[p] = cts[10]
            dg_ref[_LAST, la] += cts[11]
            dg_ref[_LAST, lb] += cts[12]
            ds_scr[2 * p] = cts[13]
            ds_scr[2 * p + 1] = cts[14]

    tspec = pl.BlockSpec((CHUNK, GDN_V), lambda b, c: (b * n + (n - 1 - c), 0))
    rspec = pl.BlockSpec((_GP, 1, LANE), lambda b, c: (b * n + (n - 1 - c), 0, 0))
    sspec = pl.BlockSpec((GDN_HEADS, LANE, LANE), lambda b, c: (b * n + (n - 1 - c), 0, 0))
    tok_shape = _sds((bsz * n * CHUNK, GDN_V))
    return _pcall("gdn_bwd", body, (bsz, n), [tspec] * 5 + [rspec, sspec, tspec], [tspec] * 5 + [rspec],
                  [tok_shape] * 5 + [_sds((bsz * n * _GP, 1, LANE))],
                  scratch=[pltpu.VMEM((GDN_HEADS, LANE, LANE), F32)], sem=("parallel", "arbitrary"))(
                      q, k, v, gc, beta, gc_row, s_in, d_o)


_B_OFF = SSD_D // LANE
_C_OFF = (SSD_D + SSD_BC) // LANE
_PPG = _SP // SSD_GROUPS


def _ssd_args(p, x_ref, dt_ref, a_ref, ar_ref):
    lp = slice(p * LANE, (p + 1) * LANE)
    gi = p // _PPG
    b_sl = slice((_B_OFF + gi) * LANE, (_B_OFF + gi + 1) * LANE)
    c_sl = slice((_C_OFF + gi) * LANE, (_C_OFF + gi + 1) * LANE)
    return (x_ref[:, lp], dt_ref[:, lp], a_ref[:, lp], ar_ref[p], a_ref[_LAST, lp], x_ref[:, b_sl], x_ref[:, c_sl])


def _ssd_fwd(xbc, dt, acs, acs_row, bsz, n):
    def body(x_ref, dt_ref, a_ref, ar_ref, y_ref, sin_ref, s_scr):
        _zero_at_first([s_scr], pl.program_id(1) == 0)
        flat = []
        for p in range(_SP):
            flat += [*_ssd_args(p, x_ref, dt_ref, a_ref, ar_ref), s_scr[p]]
        sin_ref[...] = s_scr[...]
        outs = _ssd_multi(*flat)
        for p in range(_SP):
            y_ref[:, p * LANE:(p + 1) * LANE], s_scr[p] = outs[2 * p:2 * p + 2]

    tspec = pl.BlockSpec((CHUNK, SSD_D), lambda b, c: (b * n + c, 0))
    return _pcall("ssd_fwd", body, (bsz, n),
                  [pl.BlockSpec((CHUNK, SSD_CONV_CH), lambda b, c: (b * n + c, 0)), tspec, tspec,
                   pl.BlockSpec((_SP, 1, LANE), lambda b, c: (b * n + c, 0, 0))],
                  [tspec, pl.BlockSpec((_SP, LANE, LANE), lambda b, c: (b * n + c, 0, 0))],
                  [_sds((bsz * n * CHUNK, SSD_D)), _sds((bsz * n * _SP, LANE, LANE))],
                  scratch=[pltpu.VMEM((_SP, LANE, LANE), F32)], sem=("parallel", "arbitrary"))(xbc, dt, acs, acs_row)


def _ssd_bwd(xbc, dt, acs, acs_row, s_in, d_y, d_x_skip, bsz, n):
    def body(x_ref, dt_ref, a_ref, ar_ref, sin_ref, dy_ref, dsk_ref, dx_ref, ddt_ref, da_ref, dar_ref, ds_scr):
        _zero_at_first([ds_scr], pl.program_id(1) == 0)
        d_b = [None] * SSD_GROUPS
        d_c = [None] * SSD_GROUPS
        flat, cots = [], []
        for p in range(_SP):
            flat += [*_ssd_args(p, x_ref, dt_ref, a_ref, ar_ref), sin_ref[p]]
            cots += [dy_ref[:, p * LANE:(p + 1) * LANE], ds_scr[p]]
        _, vjp = jax.vjp(_ssd_multi, *flat)
        grads = vjp(tuple(cots))
        for p in range(_SP):
            lp = slice(p * LANE, (p + 1) * LANE)
            gi = p // _PPG
            cts = grads[_SSD_NARGS * p:_SSD_NARGS * (p + 1)]
            dx_ref[:, lp] = cts[0] + dsk_ref[:, lp]
            ddt_ref[:, lp] = cts[1]
            da_ref[:, lp] = cts[2]
            dar_ref[p] = cts[3]
            da_ref[_LAST, lp] += cts[4]
            d_b[gi] = cts[5] if d_b[gi] is None else d_b[gi] + cts[5]
            d_c[gi] = cts[6] if d_c[gi] is None else d_c[gi] + cts[6]
            ds_scr[p] = cts[7]
        for gi in range(SSD_GROUPS):
            dx_ref[:, (_B_OFF + gi) * LANE:(_B_OFF + gi + 1) * LANE] = d_b[gi]
            dx_ref[:, (_C_OFF + gi) * LANE:(_C_OFF + gi + 1) * LANE] = d_c[gi]

    def rev(b, c):
        return b * n + (n - 1 - c)

    tspec = pl.BlockSpec((CHUNK, SSD_D), lambda b, c: (rev(b, c), 0))
    xspec = pl.BlockSpec((CHUNK, SSD_CONV_CH), lambda b, c: (rev(b, c), 0))
    rspec = pl.BlockSpec((_SP, 1, LANE), lambda b, c: (rev(b, c), 0, 0))
    tok_shape = _sds((bsz * n * CHUNK, SSD_D))
    return _pcall("ssd_bwd", body, (bsz, n),
                  [xspec, tspec, tspec, rspec, pl.BlockSpec((_SP, LANE, LANE), lambda b, c: (rev(b, c), 0, 0)), tspec, tspec],
                  [xspec, tspec, tspec, rspec],
                  [_sds((bsz * n * CHUNK, SSD_CONV_CH)), tok_shape, tok_shape, _sds((bsz * n * _SP, 1, LANE))],
                  scratch=[pltpu.VMEM((_SP, LANE, LANE), F32)], sem=("parallel", "arbitrary"))(
                      xbc, dt, acs, acs_row, s_in, d_y, d_x_skip)


def _add2(name, a, b):
    t, c = a.shape
    tm = _row_tile(t)

    def body(a_ref, b_ref, o_ref):
        o_ref[...] = a_ref[...] + b_ref[...]

    return _pcall(name, body, (t // tm,), [_row_spec(tm, c)] * 2, _row_spec(tm, c), _sds((t, c)), sem=("parallel",))(a, b)


def _rep(p, width):
    return jnp.repeat(p.reshape(-1), width).reshape(1, -1)


def _to_rows(narrow, lo, hi, bsz, n):
    heads = hi - lo
    a = narrow[:, lo:hi].reshape(bsz, n, CHUNK, heads)
    return jnp.transpose(a, (0, 1, 3, 2)).reshape(bsz * n * (heads // 2), 1, 2 * CHUNK)


def _from_rows(rows, heads, bsz, n):
    return jnp.transpose(rows.reshape(bsz, n, heads, CHUNK), (0, 1, 3, 2)).reshape(bsz * n * CHUNK, heads)


def _narrow_row(gdn_part, ssd_part):
    return jnp.pad(jnp.concatenate([gdn_part, ssd_part], axis=1), ((0, 0), (_G_LO, SMALL - _S_HI)))


def _local_step(x, tgt, p):
    bsz, s, _ = x.shape
    t = bsz * s
    n = s // CHUNK
    x2 = x.reshape(t, D_MODEL)
    tgt2 = tgt.reshape(t, D_MODEL)
    w_in = p["w_in"]
    w_big = jnp.concatenate([w_in[:, :4096], w_in[:, 4112:6672]], axis=1)
    w_small = jnp.concatenate([w_in[:, 4096:4112], w_in[:, 6672:6688], jnp.zeros((D_MODEL, SMALL - 32), BF16)], axis=1)
    gate_bias = _narrow_row(p["gdn_dt_bias"], p["ssd_dt_bias"])
    gate_a_log = _narrow_row(p["gdn_a_log"], p["ssd_a_log"])
    d_skip = _rep(p["ssd_d"], SSD_HEADDIM)

    h = _rms_fwd("rms0_fwd", x2, p["pre_mix_norm"])
    proj = _matmul("mm_in_big", h, w_big, "nn", F32)
    small = _matmul("mm_in_small", h, w_small, "nn", F32)
    beta, gc, dt, acs, cum = _gates_fwd(small, gate_bias, gate_a_log)
    gc_row = _to_rows(cum, _G_LO, _G_HI, bsz, n)
    acs_row = _to_rows(cum, _S_LO, _S_HI, bsz, n)
    q = _gdn_conv_fwd("q", proj, p["gdn_conv_w"], bsz, s)
    k = _gdn_conv_fwd("k", proj, p["gdn_conv_w"], bsz, s)
    v = _gdn_conv_fwd("v", proj, p["gdn_conv_w"], bsz, s)
    o, gdn_s = _gdn_fwd(q, k, v, gc, beta, gc_row, bsz, n)
    o_a = _gdn_out_fwd(o, proj, p["gdn_norm_w"])
    xbc = _ssd_conv_fwd(proj, p["ssd_conv_w"], p["ssd_conv_b"], bsz, s)
    y, ssd_s = _ssd_fwd(xbc, dt, acs, acs_row, bsz, n)
    o_s = _ssd_out_fwd(y, xbc, proj, d_skip, p["ssd_norm_w"])
    ocat = jnp.concatenate([o_a, o_s], axis=1)
    mix = _matmul("mm_out", ocat, p["w_out"], "nn", F32)
    x1, h2 = _res1_fwd(x2, mix, p["post_mix_norm"], p["pre_ffn_norm"])
    u_pre = _matmul("mm_up", h2, p["w_up"], "nn", F32)
    act = _ffn_act_fwd(u_pre, p["ffn_conv_w"], p["ffn_conv_b"], bsz, s)
    f = _matmul("mm_down", act, p["w_down"], "nn", F32)
    loss_acc, d_out, d_f, g_post_ffn = _final(x1, f, p["post_ffn_norm"], tgt2)

    grads = {"post_ffn_norm": g_post_ffn}
    d_act = _matmul("mm_down_dx", d_f, p["w_down"], "nt", F32)
    grads["w_down"] = _matmul("mm_down_dw", act, d_f, "tn", F32)
    d_gate, d_up, dwg, dwu, dbg, dbu = _ffn_act_bwd(u_pre, p["ffn_conv_w"], p["ffn_conv_b"], d_act, bsz, s)
    grads["ffn_conv_w"] = jnp.concatenate([dwg, dwu], axis=1)
    grads["ffn_conv_b"] = jnp.concatenate([dbg, dbu], axis=1)
    d_u = jnp.concatenate([d_gate, d_up], axis=1)
    d_h2 = _matmul("mm_up_dx", d_u, p["w_up"], "nt", F32)
    grads["w_up"] = _matmul("mm_up_dw", h2, d_u, "tn", F32)
    d_x1, d_mix, grads["post_mix_norm"], grads["pre_ffn_norm"] = _res1_bwd(
        x2, mix, p["post_mix_norm"], p["pre_ffn_norm"], d_out, d_h2)
    d_ocat = _matmul("mm_out_dx", d_mix, p["w_out"], "nt", F32)
    grads["w_out"] = _matmul("mm_out_dw", ocat, d_mix, "tn", F32)

    d_o, d_za, grads["gdn_norm_w"] = _gdn_out_bwd(o, proj, p["gdn_norm_w"], d_ocat)
    d_q, d_k, d_v, d_gc, d_beta, d_gc_row = _gdn_bwd(q, k, v, gc, beta, gc_row, gdn_s, d_o, bsz, n)
    d_qp, dwq = _gdn_conv_bwd("q", proj, p["gdn_conv_w"], d_q, bsz, s)
    d_kp, dwk = _gdn_conv_bwd("k", proj, p["gdn_conv_w"], d_k, bsz, s)
    d_vp, dwv = _gdn_conv_bwd("v", proj, p["gdn_conv_w"], d_v, bsz, s)
    grads["gdn_conv_w"] = jnp.concatenate([dwq, dwk, dwv], axis=1)

    d_y, d_xs_skip, d_zs, d_dskip, grads["ssd_norm_w"] = _ssd_out_bwd(y, xbc, proj, d_skip, p["ssd_norm_w"], d_ocat)
    d_xbc, d_dt, d_acs, d_acs_row = _ssd_bwd(xbc, dt, acs, acs_row, ssd_s, d_y, d_xs_skip, bsz, n)
    d_xbcp, grads["ssd_conv_w"], grads["ssd_conv_b"] = _ssd_conv_bwd(proj, p["ssd_conv_w"], p["ssd_conv_b"], d_xbc, bsz, s)

    d_cum = jnp.concatenate([jnp.zeros((t, _G_LO), F32), _from_rows(d_gc_row, GDN_HEADS, bsz, n),
                             _from_rows(d_acs_row, SSD_HEADS, bsz, n), jnp.zeros((t, SMALL - _S_HI), F32)], axis=1)
    d_small, d_gate_bias, d_gate_a_log = _gates_bwd(small, gate_bias, gate_a_log, (d_beta, d_gc, d_dt, d_acs, d_cum))
    grads["gdn_dt_bias"], grads["ssd_dt_bias"] = d_gate_bias[:, _G_LO:_G_HI], d_gate_bias[:, _S_LO:_S_HI]
    grads["gdn_a_log"], grads["ssd_a_log"] = d_gate_a_log[:, _G_LO:_G_HI], d_gate_a_log[:, _S_LO:_S_HI]
    d_proj = jnp.concatenate([d_qp, d_kp, d_vp, d_za, d_zs, d_xbcp], axis=1)
    d_h = _add2("add_dh", _matmul("mm_in_big_dx", d_proj, w_big, "nt", F32),
                _matmul("mm_in_small_dx", d_small, w_small, "nt", F32))
    dw_big = _matmul("mm_in_big_dw", h, d_proj, "tn", F32)
    dw_small = _matmul("mm_in_small_dw", h, d_small, "tn", F32)
    grads["w_in"] = jnp.concatenate([dw_big[:, :4096], dw_small[:, :16], dw_big[:, 4096:], dw_small[:, 16:32]], axis=1)
    grad_x, grads["pre_mix_norm"] = _rms1_bwd(x2, p["pre_mix_norm"], d_h, d_x1)
    grads["ssd_d"] = _head_sums(d_dskip)[:1, :SSD_HEADS]
    return loss_acc, grad_x.reshape(bsz, s, D_MODEL), grads


def _head_sums(wide):
    def body(x_ref, o_ref):
        r, c = _iota2((D_MODEL, SMALL))
        o_ref[...] = _mask_dot(jnp.broadcast_to(x_ref[...], (8, D_MODEL)), ((r >> 6) == c).astype(F32), NN, True)

    return _pcall("head_sums", body, (1,), [_full_spec((1, D_MODEL))], _full_spec((8, SMALL)), _sds((8, SMALL)))(wide)


def _adamw_fn(w, g, m, v):
    m = ADAM_B1 * m + (1.0 - ADAM_B1) * g
    v = ADAM_B2 * v + (1.0 - ADAM_B2) * (g * g)
    m_hat = m / (1.0 - ADAM_B1 ** ADAM_STEP)
    v_hat = v / (1.0 - ADAM_B2 ** ADAM_STEP)
    delta = -ADAM_LR * (m_hat / (jnp.sqrt(v_hat) + ADAM_EPS) + ADAM_WD * w)
    return delta, m, v


def _adamw(name, w, g, m, v):
    r, c = w.shape
    tr = _pick(r, (256, 176, 128, 64, 8))

    def body(w_ref, g_ref, m_ref, v_ref, d_ref, m2_ref, v2_ref):
        d, m2, v2 = _adamw_fn(w_ref[...], g_ref[...], m_ref[...], v_ref[...])
        d_ref[...] = d
        m2_ref[...] = m2
        v2_ref[...] = v2

    spec = pl.BlockSpec((tr, c), lambda i: (i, 0))
    return _pcall(name, body, (r // tr,), [spec] * 4, [spec] * 3, [_sds((r, c))] * 3, sem=("parallel",))(w, g, m, v)


_ANY = pl.BlockSpec(memory_space=pl.ANY)
_OTHER_CHIPS = ((1, 0), (0, 1), (1, 1))


def _coords():
    return lax.axis_index("x"), lax.axis_index("y"), lax.axis_index("c")


def _flip(v, f):
    return 1 - v if f else v


def _gather_chips(arrs, split):
    n = len(arrs)

    def body(*refs):
        ins, outs = refs[:n], refs[n:2 * n]
        send_sems, recv_sems, fwd_send_sems, fwd_recv_sems = refs[2 * n:]
        x, y, c = _coords()
        me = 2 * x + y
        sib = (x, y, 1 - c)

        def rows(a, core):
            if not split[a]:
                return slice(None)
            half = arrs[a].shape[0] // 2
            return pl.ds(core * half, half)

        sends = []
        for a in range(n):
            for j, (fx, fy) in enumerate(_OTHER_CHIPS):
                cp = pltpu.make_async_remote_copy(ins[a].at[rows(a, c)], outs[a].at[me, rows(a, c)],
                                                  send_sems.at[a * 3 + j], recv_sems.at[a * 3 + j],
                                                  device_id=(_flip(x, fx), _flip(y, fy), c), device_id_type=MESH)
                cp.start()
                sends.append(cp)
        for a in range(n):
            for j, (fx, fy) in enumerate(_OTHER_CHIPS):
                src = 2 * _flip(x, fx) + _flip(y, fy)
                landed = outs[a].at[src, rows(a, c)]
                pltpu.make_async_remote_copy(landed, landed, send_sems.at[a * 3 + j], recv_sems.at[a * 3 + j],
                                             device_id=(_flip(x, fx), _flip(y, fy), c), device_id_type=MESH).wait_recv()
                if split[a]:
                    fw = pltpu.make_async_remote_copy(landed, landed, fwd_send_sems.at[a * 3 + j], fwd_recv_sems.at[a * 3 + j],
                                                      device_id=sib, device_id_type=MESH)
                    fw.start()
                    sends.append(fw)
        for a in range(n):
            if split[a]:
                for j, (fx, fy) in enumerate(_OTHER_CHIPS):
                    src = 2 * _flip(x, fx) + _flip(y, fy)
                    theirs = outs[a].at[src, rows(a, 1 - c)]
                    pltpu.make_async_remote_copy(theirs, theirs, fwd_send_sems.at[a * 3 + j], fwd_recv_sems.at[a * 3 + j],
                                                 device_id=sib, device_id_type=MESH).wait_recv()
        for cp in sends:
            cp.wait_send()

    gathered = pl.pallas_call(
        body, name="gather_chips", out_shape=[_sds((N_CHIPS,) + a.shape, a.dtype) for a in arrs],
        in_specs=[_ANY] * n, out_specs=[_ANY] * n, scratch_shapes=[pltpu.SemaphoreType.DMA((3 * n,))] * 4,
        compiler_params=pltpu.CompilerParams(has_side_effects=True))(*arrs)
    chip = 2 * lax.axis_index("x") + lax.axis_index("y")
    return [lax.dynamic_update_slice_in_dim(g, a[None], chip, axis=0) for g, a in zip(gathered, arrs)]


_PEERS = tuple((fx, fy, fc) for fx in (0, 1) for fy in (0, 1) for fc in (0, 1))[1:]


def _allreduce_small(x):
    r = x.shape[0]

    def body(x_ref, o_ref, buf, send_sems, recv_sems):
        cx, cy, cc = _coords()
        me = 4 * cx + 2 * cy + cc
        sends = []
        for j, (fx, fy, fc) in enumerate(_PEERS):
            cp = pltpu.make_async_remote_copy(x_ref, buf.at[me], send_sems.at[j], recv_sems.at[j],
                                              device_id=(_flip(cx, fx), _flip(cy, fy), _flip(cc, fc)), device_id_type=MESH)
            cp.start()
            sends.append(cp)
        buf[pl.ds(me, 1)] = x_ref[...][None]
        for j, (fx, fy, fc) in enumerate(_PEERS):
            src = 4 * _flip(cx, fx) + 2 * _flip(cy, fy) + _flip(cc, fc)
            pltpu.make_async_remote_copy(x_ref, buf.at[src], send_sems.at[j], recv_sems.at[j],
                                         device_id=(_flip(cx, fx), _flip(cy, fy), _flip(cc, fc)), device_id_type=MESH).wait_recv()
        for cp in sends:
            cp.wait_send()
        acc = buf[0]
        for d in range(1, N_DEV):
            acc = acc + buf[d]
        o_ref[...] = acc

    vm = pl.BlockSpec(memory_space=pltpu.VMEM)
    return pl.pallas_call(
        body, name="allreduce_small", out_shape=_sds((r, LANE)), in_specs=[vm], out_specs=vm,
        scratch_shapes=[pltpu.VMEM((N_DEV, r, LANE), F32), pltpu.SemaphoreType.DMA((7,)), pltpu.SemaphoreType.DMA((7,))],
        compiler_params=pltpu.CompilerParams(has_side_effects=True, vmem_limit_bytes=VMEM_LIMIT))(x)


def _pair_send_other_half(arrs):
    n = len(arrs)

    def body(*refs):
        ins, outs = refs[:n], refs[n:2 * n]
        send_sems, recv_sems = refs[2 * n:]
        x, y, c = _coords()
        sends = []
        for a in range(n):
            half = ins[a].shape[1] // 2
            cp = pltpu.make_async_remote_copy(ins[a].at[:, pl.ds((1 - c) * half, half), :], outs[a], send_sems.at[a], recv_sems.at[a],
                                              device_id=(x, y, 1 - c), device_id_type=MESH)
            cp.start()
            sends.append(cp)
        for cp in sends:
            cp.wait_recv()
        for cp in sends:
            cp.wait_send()

    return pl.pallas_call(
        body, name="pair_reduce_send", out_shape=[_sds((a.shape[0], a.shape[1] // 2, a.shape[2]), a.dtype) for a in arrs],
        in_specs=[_ANY] * n, out_specs=[_ANY] * n, scratch_shapes=[pltpu.SemaphoreType.DMA((n,))] * 2,
        compiler_params=pltpu.CompilerParams(has_side_effects=True))(*arrs)


def _pair_fill(arrs):
    n = len(arrs)

    def body(*refs):
        bufs = refs[n:2 * n]
        send_sems, recv_sems = refs[2 * n:]
        x, y, c = _coords()
        sends = []
        for a in range(n):
            cp = pltpu.make_async_remote_copy(bufs[a].at[c], bufs[a].at[c], send_sems.at[a], recv_sems.at[a],
                                              device_id=(x, y, 1 - c), device_id_type=MESH)
            cp.start()
            sends.append(cp)
        for a in range(n):
            theirs = bufs[a].at[1 - c]
            pltpu.make_async_remote_copy(theirs, theirs, send_sems.at[a], recv_sems.at[a],
                                         device_id=(x, y, 1 - c), device_id_type=MESH).wait_recv()
        for cp in sends:
            cp.wait_send()

    return pl.pallas_call(
        body, name="pair_gather", out_shape=[_sds(a.shape, a.dtype) for a in arrs], in_specs=[_ANY] * n, out_specs=[_ANY] * n,
        scratch_shapes=[pltpu.SemaphoreType.DMA((n,))] * 2, input_output_aliases={a: a for a in range(n)},
        compiler_params=pltpu.CompilerParams(has_side_effects=True))(*arrs)


def _scatter_chips(arrs):
    n = len(arrs)

    def body(*refs):
        ins, outs = refs[:n], refs[n:2 * n]
        send_sems, recv_sems = refs[2 * n:]
        x, y, c = _coords()
        sends = []
        for a in range(n):
            for j, (fx, fy) in enumerate(_OTHER_CHIPS):
                to = 2 * _flip(x, fx) + _flip(y, fy)
                cp = pltpu.make_async_remote_copy(ins[a].at[to], outs[a].at[j], send_sems.at[a * 3 + j], recv_sems.at[a * 3 + j],
                                                  device_id=(_flip(x, fx), _flip(y, fy), c), device_id_type=MESH)
                cp.start()
                sends.append(cp)
        for cp in sends:
            cp.wait_recv()
        for cp in sends:
            cp.wait_send()

    return pl.pallas_call(
        body, name="scatter_chips", out_shape=[_sds((3,) + a.shape[1:], a.dtype) for a in arrs],
        in_specs=[_ANY] * n, out_specs=[_ANY] * n, scratch_shapes=[pltpu.SemaphoreType.DMA((3 * n,))] * 2,
        compiler_params=pltpu.CompilerParams(has_side_effects=True))(*arrs)


def _pair_add(name, full, recv, core):
    _, r, c = full.shape
    half = r // 2
    tr = _pick(half, (256, 176, 128, 64, 8))
    nb = half // tr

    def body(c_ref, a_ref, b_ref, o_ref, ob_ref):
        s = a_ref[...] + b_ref[...]
        o_ref[...] = s
        ob_ref[...] = s.astype(BF16)

    blk = pl.BlockSpec((1, tr, c), lambda k, i, cref: (k, i, 0))
    grid_spec = pltpu.PrefetchScalarGridSpec(
        num_scalar_prefetch=1, grid=(N_CHIPS, nb),
        in_specs=[pl.BlockSpec((1, tr, c), lambda k, i, cref: (k, cref[0] * nb + i, 0)), blk], out_specs=[blk, blk])
    return pl.pallas_call(
        body, name=name, out_shape=[_sds((N_CHIPS, half, c)), _sds((N_CHIPS, half, c), BF16)], grid_spec=grid_spec,
        compiler_params=pltpu.CompilerParams(dimension_semantics=("parallel", "parallel"), vmem_limit_bytes=VMEM_LIMIT))(
            core, full, recv)


def _chip_sum(name, landed, own, where):
    _, r, c = landed.shape
    tr = _pick(r, (256, 176, 128, 64, 16))

    def body(w_ref, l_ref, o_ref, s_ref):
        s_ref[0] = ((o_ref[0] + l_ref[0].astype(F32)) + l_ref[1].astype(F32)) + l_ref[2].astype(F32)

    grid_spec = pltpu.PrefetchScalarGridSpec(
        num_scalar_prefetch=1, grid=(r // tr,),
        in_specs=[pl.BlockSpec((3, tr, c), lambda i, wref: (0, i, 0)),
                  pl.BlockSpec((1, tr, c), lambda i, wref: (wref[0], i, 0))],
        out_specs=pl.BlockSpec((1, tr, c), lambda i, wref: (wref[1], i, 0)))
    return pl.pallas_call(
        body, name=name, out_shape=_sds((2, r, c)), grid_spec=grid_spec,
        compiler_params=pltpu.CompilerParams(dimension_semantics=("parallel",), vmem_limit_bytes=VMEM_LIMIT))(where, landed, own)


_WEIGHTS = ("pre_mix_norm", "w_in", "gdn_conv_w", "gdn_a_log", "gdn_dt_bias", "gdn_norm_w", "ssd_conv_w", "ssd_conv_b",
            "ssd_a_log", "ssd_dt_bias", "ssd_d", "ssd_norm_w", "w_out", "post_mix_norm", "pre_ffn_norm", "w_up",
            "ffn_conv_w", "ffn_conv_b", "w_down", "post_ffn_norm")
_BIG = ("w_in", "w_out", "w_up", "w_down")
_COL_SHARDED_SMALL = ("gdn_conv_w", "ssd_conv_w", "ffn_conv_w")
_SMALL = tuple(k for k in _WEIGHTS if k not in _BIG)


def _pack(arrs):
    flat = jnp.concatenate([a.reshape(-1) for a in arrs])
    rows = -(-flat.shape[0] // (8 * LANE)) * 8
    return jnp.pad(flat, (0, rows * LANE - flat.shape[0])).reshape(rows, LANE)


def _unpack(packed, shapes):
    flat = packed.reshape(-1)
    out, off = [], 0
    for shp in shapes:
        size = 1
        for d in shp:
            size *= d
        out.append(flat[off:off + size].reshape(shp))
        off += size
    return out


def _cols_to_chips(a):
    r, c4 = a.shape
    return jnp.transpose(a.reshape(r, N_CHIPS, c4 // N_CHIPS), (1, 0, 2))


def _chips_to_cols(a):
    k, r, c = a.shape
    return jnp.transpose(a, (1, 0, 2)).reshape(r, k * c)


def kernel(x, pre_mix_norm, w_in, gdn_conv_w, gdn_a_log, gdn_dt_bias, gdn_norm_w, ssd_conv_w, ssd_conv_b, ssd_a_log, ssd_dt_bias, ssd_d, ssd_norm_w, w_out, post_mix_norm, pre_ffn_norm, w_up, ffn_conv_w, ffn_conv_b, w_down, post_ffn_norm, loss_target, m_pre_mix_norm, m_w_in, m_gdn_conv_w, m_gdn_a_log, m_gdn_dt_bias, m_gdn_norm_w, m_ssd_conv_w, m_ssd_conv_b, m_ssd_a_log, m_ssd_dt_bias, m_ssd_d, m_ssd_norm_w, m_w_out, m_post_mix_norm, m_pre_ffn_norm, m_w_up, m_ffn_conv_w, m_ffn_conv_b, m_w_down, m_post_ffn_norm, v_pre_mix_norm, v_w_in, v_gdn_conv_w, v_gdn_a_log, v_gdn_dt_bias, v_gdn_norm_w, v_ssd_conv_w, v_ssd_conv_b, v_ssd_a_log, v_ssd_dt_bias, v_ssd_d, v_ssd_norm_w, v_w_out, v_post_mix_norm, v_pre_ffn_norm, v_w_up, v_ffn_conv_w, v_ffn_conv_b, v_w_down, v_post_ffn_norm):
    w = dict(zip(_WEIGHTS, (pre_mix_norm, w_in, gdn_conv_w, gdn_a_log, gdn_dt_bias, gdn_norm_w, ssd_conv_w, ssd_conv_b,
                            ssd_a_log, ssd_dt_bias, ssd_d, ssd_norm_w, w_out, post_mix_norm, pre_ffn_norm, w_up,
                            ffn_conv_w, ffn_conv_b, w_down, post_ffn_norm)))
    m = dict(zip(_WEIGHTS, (m_pre_mix_norm, m_w_in, m_gdn_conv_w, m_gdn_a_log, m_gdn_dt_bias, m_gdn_norm_w, m_ssd_conv_w,
                            m_ssd_conv_b, m_ssd_a_log, m_ssd_dt_bias, m_ssd_d, m_ssd_norm_w, m_w_out, m_post_mix_norm,
                            m_pre_ffn_norm, m_w_up, m_ffn_conv_w, m_ffn_conv_b, m_w_down, m_post_ffn_norm)))
    v = dict(zip(_WEIGHTS, (v_pre_mix_norm, v_w_in, v_gdn_conv_w, v_gdn_a_log, v_gdn_dt_bias, v_gdn_norm_w, v_ssd_conv_w,
                            v_ssd_conv_b, v_ssd_a_log, v_ssd_dt_bias, v_ssd_d, v_ssd_norm_w, v_w_out, v_post_mix_norm,
                            v_pre_ffn_norm, v_w_up, v_ffn_conv_w, v_ffn_conv_b, v_w_down, v_post_ffn_norm)))
    cx, cy, cc = _coords()
    chip = 2 * cx + cy

    shards = [w[k][0].astype(BF16) for k in _BIG] + [w[k][0] for k in _COL_SHARDED_SMALL]
    g_in, g_out, g_up, g_down, g_gcw, g_scw, g_fcw = _gather_chips(shards, [True] * len(_BIG) + [False] * len(_COL_SHARDED_SMALL))
    p = {k: w[k] for k in _SMALL if k not in _COL_SHARDED_SMALL}
    p["w_in"] = _chips_to_cols(g_in)
    p["w_up"] = _chips_to_cols(g_up)
    p["w_out"] = g_out.reshape(-1, D_MODEL)
    p["w_down"] = g_down.reshape(-1, D_MODEL)
    p["gdn_conv_w"] = _chips_to_cols(g_gcw)
    p["ssd_conv_w"] = _chips_to_cols(g_scw)
    p["ffn_conv_w"] = _chips_to_cols(g_fcw)

    loss_acc, grad_x, grads = _local_step(x, loss_target, p)
    loss = lax.psum(loss_acc[0, 0], ("x", "y", "c"))

    small_full_shapes = [grads[k].shape for k in _SMALL]
    summed = _unpack(_allreduce_small(_pack([grads[k] for k in _SMALL])), small_full_shapes)
    g_small = dict(zip(_SMALL, summed))
    for k in _COL_SHARDED_SMALL:
        width = w[k].shape[2]
        g_small[k] = lax.dynamic_slice_in_dim(g_small[k], chip * width, width, axis=1)

    big = [_cols_to_chips(grads["w_in"]), grads["w_out"].reshape(N_CHIPS, -1, D_MODEL),
           _cols_to_chips(grads["w_up"]), grads["w_down"].reshape(N_CHIPS, -1, D_MODEL)]
    from_sibling = _pair_send_other_half(big)
    core = cc.astype(jnp.int32).reshape(1)
    pair_sums = [_pair_add("pair_add_" + k, a, b, core) for k, a, b in zip(_BIG, big, from_sibling)]
    landed = _scatter_chips([ps[1] for ps in pair_sums])
    where = jnp.stack([chip, cc]).astype(jnp.int32)
    mine = [_chip_sum("chip_sum_" + k, a, ps[0], where) for k, a, ps in zip(_BIG, landed, pair_sums)]
    both = _pair_fill(mine)
    g_big = {k: a.reshape(-1, a.shape[2]) for k, a in zip(_BIG, both)}

    out_g, out_d, out_m, out_v = {}, {}, {}, {}
    for k in _BIG:
        out_g[k] = g_big[k][None]
        d_, m_, v_ = _adamw("adamw_" + k, w[k][0], g_big[k], m[k][0], v[k][0])
        out_d[k], out_m[k], out_v[k] = d_[None], m_[None], v_[None]
    shapes = [w[k].shape for k in _SMALL]
    for k in _SMALL:
        out_g[k] = g_small[k].reshape(w[k].shape)
    packed = [_pack([d[k] for k in _SMALL]) for d in (w, out_g, m, v)]
    d_p, m_p, v_p = _adamw("adamw_small", *packed)
    for dst, src in ((out_d, d_p), (out_m, m_p), (out_v, v_p)):
        dst.update(zip(_SMALL, _unpack(src, shapes)))
    return (loss, grad_x, *[out_g[k] for k in _WEIGHTS], *[out_d[k] for k in _WEIGHTS],
            *[out_m[k] for k in _WEIGHTS], *[out_v[k] for k in _WEIGHTS])
```

```python
import functools

import jax
import jax.numpy as jnp
from jax import lax
from jax.experimental import pallas as pl
from jax.experimental.pallas import tpu as pltpu

F32 = jnp.float32
BF16 = jnp.bfloat16

D_MODEL = 1024
GDN_HEADS = 8
GDN_DK = 128
SSD_HEADS = 16
SSD_HEADDIM = 64
SSD_GROUPS = 2
SSD_STATE = 128
CONV_K = 4
CHUNK = 64
D_FF = 2816
FFN_CONV_K = 3
EPS = 1e-6
GDN_QK = GDN_HEADS * GDN_DK
GDN_V = GDN_QK
SSD_D = SSD_HEADS * SSD_HEADDIM
SSD_BC = SSD_GROUPS * SSD_STATE
SSD_CONV_CH = SSD_D + 2 * SSD_BC
BIG = 4 * 1024 + 1024 + SSD_CONV_CH
SMALL = 128
D_IN_PROJ = 6688
LANE = 128
PAIR = 2 * CHUNK
NEG = -1e30
VMEM_LIMIT = 56 * 1024 * 1024

ADAM_LR = 0.001
ADAM_B1 = 0.9
ADAM_B2 = 0.999
ADAM_EPS = 1e-08
ADAM_WD = 0.01
ADAM_STEP = 10

N_CHIPS = 4
N_DEV = 8
MESH = pl.DeviceIdType.MESH

NN = ((1,), (0,))
NT = ((1,), (1,))
TN = ((0,), (0,))


def _bdot(a, b, dims):
    return lax.dot_general(a.astype(BF16), b.astype(BF16), (dims, ((), ())), preferred_element_type=F32)


def _split3(a):
    hi = a.astype(BF16)
    r1 = a - hi.astype(F32)
    mid = r1.astype(BF16)
    return hi, mid, (r1 - mid.astype(F32)).astype(BF16)


@jax.custom_vjp
def _nn(a, b):
    return _bdot(a, b, NN)


@jax.custom_vjp
def _nt(a, b):
    return _bdot(a, b, NT)


@jax.custom_vjp
def _tn(a, b):
    return _bdot(a, b, TN)


_nn.defvjp(lambda a, b: (_nn(a, b), (a, b)), lambda r, g: (_nt(g, r[1]), _tn(r[0], g)))
_nt.defvjp(lambda a, b: (_nt(a, b), (a, b)), lambda r, g: (_nn(g, r[1]), _tn(g, r[0])))
_tn.defvjp(lambda a, b: (_tn(a, b), (a, b)), lambda r, g: (_nt(r[1], g), _nn(r[0], g)))


def _mask_dot(x, mask, dims, x_first):
    acc = None
    for piece in _split3(x):
        term = _bdot(piece, mask, dims) if x_first else _bdot(mask, piece, dims)
        acc = term if acc is None else acc + term
    return acc


@jax.custom_vjp
def _cst_left(cst, x):
    return _mask_dot(x, cst, NN, False)


_cst_left.defvjp(lambda cst, x: (_cst_left(cst, x), cst), lambda cst, g: (jnp.zeros_like(cst), _mask_dot(g, cst, TN, False)))


@jax.custom_vjp
def _cst_right(x, cst):
    return _mask_dot(x, cst, NN, True)


_cst_right.defvjp(lambda x, cst: (_cst_right(x, cst), cst), lambda cst, g: (_mask_dot(g, cst, NT, True), jnp.zeros_like(cst)))


def _lin_left(cst):
    return functools.partial(_cst_left, cst)


def _lin_right(cst):
    return lambda x: _cst_right(x, cst)


@jax.custom_vjp
def _tri_inv_m1(a):
    pm = [-x for x in a]
    ap = list(a)
    for _ in range(5):
        ap = [_bdot(x, x, NN) for x in ap]
        pm = [(p + x) + _bdot(p, x, NN) for p, x in zip(pm, ap)]
    return pm


def _tri_inv_m1_bwd(pm, g):
    t = [gi + _bdot(p, gi, TN) for p, gi in zip(pm, g)]
    return ([-(ti + _bdot(ti, p, NT)) for p, ti in zip(pm, t)],)


_tri_inv_m1.defvjp(lambda a: (lambda pm: (pm, pm))(_tri_inv_m1(a)), _tri_inv_m1_bwd)


@jax.custom_vjp
def _top(x):
    return x[: x.shape[0] // 2]


_top.defvjp(lambda x: (_top(x), None), lambda _, g: (jnp.concatenate([g, jnp.zeros_like(g)], axis=0),))


@jax.custom_vjp
def _bot(x):
    return x[x.shape[0] // 2:]


_bot.defvjp(lambda x: (_bot(x), None), lambda _, g: (jnp.concatenate([jnp.zeros_like(g), g], axis=0),))


@jax.custom_vjp
def _vstack(a, b):
    return jnp.concatenate([a, b], axis=0)


_vstack.defvjp(lambda a, b: (_vstack(a, b), None), lambda _, g: (g[: g.shape[0] // 2], g[g.shape[0] // 2:]))


def _shift_dn_raw(x, s):
    if s == 0:
        return x
    r = pltpu.roll(x, s, axis=0)
    ri = lax.broadcasted_iota(jnp.int32, x.shape, 0)
    return jnp.where(ri >= s, r, 0.0)


def _shift_up_raw(x, s):
    if s == 0:
        return x
    n = x.shape[0]
    r = pltpu.roll(x, n - s, axis=0)
    ri = lax.broadcasted_iota(jnp.int32, x.shape, 0)
    return jnp.where(ri < n - s, r, 0.0)


@functools.partial(jax.custom_vjp, nondiff_argnums=(1,))
def _shift_dn(x, s):
    return _shift_dn_raw(x, s)


_shift_dn.defvjp(lambda x, s: (_shift_dn_raw(x, s), None), lambda s, _, g: (_shift_up_raw(g, s),))


def _conv(x, wrows):
    k_w = len(wrows)
    acc = wrows[k_w - 1] * x
    for k in range(k_w - 1):
        acc = acc + wrows[k] * _shift_dn(x, k_w - 1 - k)
    return acc


def _silu(x):
    return x * jax.nn.sigmoid(x)


def _rms(x, w):
    return x * lax.rsqrt(jnp.mean(x * x, axis=-1, keepdims=True) + EPS) * w


def _l2n(x):
    return x * lax.rsqrt(jnp.sum(x * x, axis=-1, keepdims=True) + EPS)


def _iota2(shape):
    return lax.broadcasted_iota(jnp.int32, shape, 0), lax.broadcasted_iota(jnp.int32, shape, 1)


_GDN_NARGS = 15
_SSD_NARGS = 8


def _gdn_multi(*flat):
    pairs = [flat[i:i + _GDN_NARGS] for i in range(0, len(flat), _GDN_NARGS)]
    idx = range(len(pairs))
    ri, ci = _iota2((PAIR, PAIR))
    blk = ((ri >= CHUNK) & (ci >= CHUNK)) | ((ri < CHUNK) & (ci < CHUNK))
    causal = blk & (ri >= ci)
    strict = blk & (ri > ci)
    q = [_vstack(p[0], p[1]) for p in pairs]
    k = [_vstack(p[2], p[3]) for p in pairs]
    v = [_vstack(p[4], p[5]) for p in pairs]
    gc = [_vstack(p[6], p[7]) for p in pairs]
    beta = [_vstack(p[8], p[9]) for p in pairs]
    glast = [_vstack(jnp.broadcast_to(p[11], (CHUNK, LANE)), jnp.broadcast_to(p[12], (CHUNK, LANE))) for p in pairs]
    sa = [p[13] for p in pairs]
    sb = [p[14] for p in pairs]
    decay = [jnp.exp(jnp.where(causal, gc[i] - jnp.broadcast_to(pairs[i][10], (PAIR, PAIR)), NEG)) for i in idx]
    eg = [jnp.exp(x) for x in gc]
    kbeta = [k[i] * beta[i] for i in idx]
    pm = _tri_inv_m1([jnp.where(strict, _nt(kbeta[i], k[i]) * decay[i], 0.0) for i in idx])
    qk = [_nt(q[i], k[i]) * decay[i] for i in idx]
    rhs_v = [v[i] * beta[i] for i in idx]
    rhs_k = [kbeta[i] * eg[i] for i in idx]
    u = [rhs_v[i] + _nn(pm[i], rhs_v[i]) for i in idx]
    w = [rhs_k[i] + _nn(pm[i], rhs_k[i]) for i in idx]
    q_dec = [q[i] * eg[i] for i in idx]
    k_dec = [k[i] * jnp.exp(glast[i] - gc[i]) for i in idx]
    gl = [jnp.exp(x) for x in glast]
    w_s = [_vstack(_nn(_top(w[i]), sa[i]), _nn(_bot(w[i]), sb[i])) for i in idx]
    q_s = [_vstack(_nn(_top(q_dec[i]), sa[i]), _nn(_bot(q_dec[i]), sb[i])) for i in idx]
    v_new = [u[i] - w_s[i] for i in idx]
    o = [q_s[i] + _nn(qk[i], v_new[i]) for i in idx]
    sa2 = [sa[i] * _vstack(_top(gl[i]), _top(gl[i])) + _tn(_top(k_dec[i]), _top(v_new[i])) for i in idx]
    sb2 = [sb[i] * _vstack(_bot(gl[i]), _bot(gl[i])) + _tn(_bot(k_dec[i]), _bot(v_new[i])) for i in idx]
    out = []
    for i in idx:
        out += [_top(o[i]), _bot(o[i]), sa2[i], sb2[i]]
    return tuple(out)


def _ssd_multi(*flat):
    pairs = [flat[i:i + _SSD_NARGS] for i in range(0, len(flat), _SSD_NARGS)]
    idx = range(len(pairs))
    ri, ci = _iota2((CHUNK, PAIR))
    causal = ri >= jnp.where(ci >= CHUNK, ci - CHUNK, ci)
    xdt = [p[0] * p[1] for p in pairs]
    acs = [p[2] for p in pairs]
    alast = [jnp.broadcast_to(p[4], (CHUNK, PAIR)) for p in pairs]
    lmat = [jnp.exp(jnp.where(causal, acs[i] - jnp.broadcast_to(pairs[i][3], (CHUNK, PAIR)), NEG)) for i in idx]
    cb2 = [_nt(p[6], _vstack(p[5], p[5])) for p in pairs]
    xblk = [_vstack(jnp.where(ci < CHUNK, x, 0.0), jnp.where(ci >= CHUNK, x, 0.0)) for x in xdt]
    y_off = [_nn(pairs[i][6], pairs[i][7]) * jnp.exp(acs[i]) for i in idx]
    y = [_nn(cb2[i] * lmat[i], xblk[i]) + y_off[i] for i in idx]
    el = [jnp.exp(x) for x in alast]
    st2 = [pairs[i][7] * _vstack(el[i], el[i]) + _tn(pairs[i][5], xdt[i] * jnp.exp(alast[i] - acs[i])) for i in idx]
    out = []
    for i in idx:
        out += [y[i], st2[i]]
    return tuple(out)


def _pcall(name, body, grid, in_specs, out_specs, out_shape, scratch=(), sem=None, aliases=None):
    if sem is None:
        sem = ("arbitrary",) * len(grid)
    return pl.pallas_call(
        functools.partial(body),
        out_shape=out_shape,
        grid=grid,
        in_specs=in_specs,
        out_specs=out_specs,
        scratch_shapes=scratch,
        input_output_aliases=aliases or {},
        name=name,
        compiler_params=pltpu.CompilerParams(dimension_semantics=sem, vmem_limit_bytes=VMEM_LIMIT),
    )


def _sds(shape, dtype=F32):
    return jax.ShapeDtypeStruct(shape, dtype)


def _row_spec(tm, width, colblock=0):
    return pl.BlockSpec((tm, width), lambda i, _c=colblock: (i, _c))


def _full_spec(shape):
    nd = len(shape)
    return pl.BlockSpec(shape, lambda *_: (0,) * nd)


def _zero_at_first(refs, first):
    @pl.when(first)
    def _():
        for r in refs:
            r[...] = jnp.zeros(r.shape, r.dtype)


def _pick(n, prefs):
    for p in prefs:
        if n % p == 0:
            return p
    return n


def _matmul(name, a, b, mode, out_dtype):
    if mode == "tn":
        r, m = a.shape
        n = b.shape[1]
        tm = _pick(m, (1024, 1408))
        tn = _pick(n, (512, 256, 128))
        tk = _pick(r, (1024, 512, 256, 128, 64))

        def body(a_ref, b_ref, o_ref):
            _zero_at_first([o_ref], pl.program_id(2) == 0)
            o_ref[...] += _bdot(a_ref[...], b_ref[...], TN)

        return _pcall(
            name, body, (m // tm, n // tn, r // tk),
            [pl.BlockSpec((tk, tm), lambda i, j, k: (k, i)), pl.BlockSpec((tk, tn), lambda i, j, k: (k, j))],
            pl.BlockSpec((tm, tn), lambda i, j, k: (i, j)), _sds((m, n), out_dtype),
            sem=("parallel", "parallel", "arbitrary"))(a, b)
    m, k = a.shape
    n = b.shape[1] if mode == "nn" else b.shape[0]
    tm = _pick(m, (1024, 512, 256, 128, 64) if k <= 2816 else (512, 256, 128, 64))
    tn = _pick(n, (512, 256, 128))
    dims = NN if mode == "nn" else NT

    def body(a_ref, b_ref, o_ref):
        o_ref[...] = _bdot(a_ref[...], b_ref[...], dims).astype(o_ref.dtype)

    b_spec = pl.BlockSpec((k, tn), lambda i, j: (0, j)) if mode == "nn" else pl.BlockSpec((tn, k), lambda i, j: (j, 0))
    return _pcall(
        name, body, (m // tm, n // tn), [pl.BlockSpec((tm, k), lambda i, j: (i, 0)), b_spec],
        pl.BlockSpec((tm, tn), lambda i, j: (i, j)), _sds((m, n), out_dtype), sem=("parallel", "parallel"))(a, b)


def _row_tile(t):
    return _pick(t, (256, 128, 64))


def _rms_fwd(name, x, g):
    t = x.shape[0]
    tm = _row_tile(t)

    def body(x_ref, g_ref, h_ref):
        h_ref[...] = _rms(x_ref[...], g_ref[...]).astype(BF16)

    return _pcall(name, body, (t // tm,), [_row_spec(tm, D_MODEL), _full_spec((1, D_MODEL))], _row_spec(tm, D_MODEL),
                  _sds((t, D_MODEL), BF16), sem=("parallel",))(x, g)


_G_LO, _G_HI = GDN_HEADS, 2 * GDN_HEADS
_S_LO, _S_HI = 2 * GDN_HEADS, 2 * GDN_HEADS + SSD_HEADS


def _gates_fn(small, bias, a_log):
    tm = small.shape[0]
    r, c = _iota2((SMALL, D_MODEL))
    e_b = (r == (c >> 7)).astype(F32)
    e_a = (r == _G_LO + (c >> 7)).astype(F32)
    e_dt = (r == _S_LO + (c >> 6)).astype(F32)
    rr, cc = _iota2((tm, tm))
    in_chunk_tril = (((rr >> 6) == (cc >> 6)) & (rr >= cc)).astype(F32)
    sig = jax.nn.sigmoid(small)
    sp = jax.nn.softplus(small + bias)
    cum = _lin_left(in_chunk_tril)(-jnp.exp(a_log) * sp)
    return _lin_right(e_b)(sig), _lin_right(e_a)(cum), _lin_right(e_dt)(sp), _lin_right(e_dt)(cum), cum


def _gates_fwd(small, bias, a_log):
    t = small.shape[0]
    tm = _row_tile(t)

    def body(s_ref, p0, p1, *outs):
        for o, v in zip(outs, _gates_fn(s_ref[...], p0[...], p1[...])):
            o[...] = v

    pspec = _full_spec((1, SMALL))
    return _pcall("gates_fwd", body, (t // tm,), [_row_spec(tm, SMALL), pspec, pspec],
                  [_row_spec(tm, D_MODEL)] * 4 + [_row_spec(tm, SMALL)], [_sds((t, D_MODEL))] * 4 + [_sds((t, SMALL))],
                  sem=("parallel",))(small, bias, a_log)


def _gates_bwd(small, bias, a_log, cts):
    t = small.shape[0]
    tm = _row_tile(t)

    def body(s_ref, p0, p1, c0, c1, c2, c3, c4, ds_ref, db_ref, da_ref):
        _zero_at_first([db_ref, da_ref], pl.program_id(0) == 0)
        _, vjp = jax.vjp(_gates_fn, s_ref[...], p0[...], p1[...])
        d_s, d_b, d_a = vjp((c0[...], c1[...], c2[...], c3[...], c4[...]))
        ds_ref[...] = d_s.astype(BF16)
        db_ref[...] += d_b
        da_ref[...] += d_a

    pspec = _full_spec((1, SMALL))
    return _pcall("gates_bwd", body, (t // tm,),
                  [_row_spec(tm, SMALL), pspec, pspec] + [_row_spec(tm, D_MODEL)] * 4 + [_row_spec(tm, SMALL)],
                  [_row_spec(tm, SMALL), pspec, pspec], [_sds((t, SMALL), BF16), _sds((1, SMALL)), _sds((1, SMALL))])(
                      small, bias, a_log, *cts)


def _gdn_out_fn(o, z, w):
    return _rms(o, w) * _silu(z)


def _gdn_out_fwd(o, proj, gn):
    t = o.shape[0]
    tm = _row_tile(t)

    def body(o_ref, z_ref, w_ref, y_ref):
        for h in range(GDN_HEADS):
            sl = slice(h * GDN_DK, (h + 1) * GDN_DK)
            y_ref[:, sl] = _gdn_out_fn(o_ref[:, sl], z_ref[:, sl].astype(F32), w_ref[...]).astype(BF16)

    return _pcall("gdn_out_fwd", body, (t // tm,), [_row_spec(tm, GDN_V), _row_spec(tm, GDN_V, 3), _full_spec((1, GDN_DK))],
                  _row_spec(tm, GDN_V), _sds((t, GDN_V), BF16), sem=("parallel",))(o, proj, gn)


def _gdn_out_bwd(o, proj, gn, d_ocat):
    t = o.shape[0]
    tm = _row_tile(t)

    def body(o_ref, z_ref, w_ref, dy_ref, do_ref, dz_ref, dw_ref):
        _zero_at_first([dw_ref], pl.program_id(0) == 0)
        for h in range(GDN_HEADS):
            sl = slice(h * GDN_DK, (h + 1) * GDN_DK)
            _, vjp = jax.vjp(_gdn_out_fn, o_ref[:, sl], z_ref[:, sl].astype(F32), w_ref[...])
            d_o, d_z, d_w = vjp(dy_ref[:, sl])
            do_ref[:, sl] = d_o
            dz_ref[:, sl] = d_z.astype(BF16)
            dw_ref[...] += d_w

    return _pcall("gdn_out_bwd", body, (t // tm,),
                  [_row_spec(tm, GDN_V), _row_spec(tm, GDN_V, 3), _full_spec((1, GDN_DK)), _row_spec(tm, GDN_V, 0)],
                  [_row_spec(tm, GDN_V), _row_spec(tm, GDN_V), _full_spec((1, GDN_DK))],
                  [_sds((t, GDN_V)), _sds((t, GDN_V), BF16), _sds((1, GDN_DK))])(o, proj, gn, d_ocat)


def _ssd_out_fn(y, xs, z, d_skip, w):
    return _rms((y + d_skip * xs) * _silu(z), w)


_SSD_GW = SSD_D // SSD_GROUPS


def _ssd_out_fwd(y, xbc, proj, d_skip, nw):
    t = y.shape[0]
    tm = _row_tile(t)

    def body(y_ref, x_ref, z_ref, d_ref, w_ref, o_ref):
        for gi in range(SSD_GROUPS):
            sl = slice(gi * _SSD_GW, (gi + 1) * _SSD_GW)
            o_ref[:, sl] = _ssd_out_fn(y_ref[:, sl], x_ref[:, sl], z_ref[:, sl].astype(F32), d_ref[:, sl], w_ref[:, sl]).astype(BF16)

    pspec = _full_spec((1, SSD_D))
    return _pcall("ssd_out_fwd", body, (t // tm,),
                  [_row_spec(tm, SSD_D), _row_spec(tm, SSD_D, 0), _row_spec(tm, SSD_D, 4), pspec, pspec],
                  _row_spec(tm, SSD_D), _sds((t, SSD_D), BF16), sem=("parallel",))(y, xbc, proj, d_skip, nw)


def _ssd_out_bwd(y, xbc, proj, d_skip, nw, d_ocat):
    t = y.shape[0]
    tm = _row_tile(t)

    def body(y_ref, x_ref, z_ref, d_ref, w_ref, do_ref, dy_ref, dx_ref, dz_ref, dd_ref, dw_ref):
        _zero_at_first([dd_ref, dw_ref], pl.program_id(0) == 0)
        for gi in range(SSD_GROUPS):
            sl = slice(gi * _SSD_GW, (gi + 1) * _SSD_GW)
            _, vjp = jax.vjp(_ssd_out_fn, y_ref[:, sl], x_ref[:, sl], z_ref[:, sl].astype(F32), d_ref[:, sl], w_ref[:, sl])
            d_y, d_x, d_z, d_d, d_w = vjp(do_ref[:, sl])
            dy_ref[:, sl] = d_y
            dx_ref[:, sl] = d_x
            dz_ref[:, sl] = d_z.astype(BF16)
            dd_ref[:, sl] += d_d
            dw_ref[:, sl] += d_w

    pspec = _full_spec((1, SSD_D))
    row = _row_spec(tm, SSD_D)
    return _pcall("ssd_out_bwd", body, (t // tm,),
                  [row, _row_spec(tm, SSD_D, 0), _row_spec(tm, SSD_D, 4), pspec, pspec, _row_spec(tm, SSD_D, 1)],
                  [row, row, row, pspec, pspec],
                  [_sds((t, SSD_D)), _sds((t, SSD_D)), _sds((t, SSD_D), BF16), _sds((1, SSD_D)), _sds((1, SSD_D))])(
                      y, xbc, proj, d_skip, nw, d_ocat)


def _res1_fn(x, mix, g_pm, g_pf):
    x1 = x + _rms(mix, g_pm)
    return x1, _rms(x1, g_pf)


def _res1_fwd(x, mix, g_pm, g_pf):
    t = x.shape[0]
    tm = _row_tile(t)

    def body(x_ref, m_ref, a_ref, b_ref, x1_ref, h2_ref):
        x1, h2 = _res1_fn(x_ref[...], m_ref[...], a_ref[...], b_ref[...])
        x1_ref[...] = x1
        h2_ref[...] = h2.astype(BF16)

    row, pspec = _row_spec(tm, D_MODEL), _full_spec((1, D_MODEL))
    return _pcall("res1_fwd", body, (t // tm,), [row, row, pspec, pspec], [row, row],
                  [_sds((t, D_MODEL)), _sds((t, D_MODEL), BF16)], sem=("parallel",))(x, mix, g_pm, g_pf)


def _res1_bwd(x, mix, g_pm, g_pf, d_x1, d_h2):
    t = x.shape[0]
    tm = _row_tile(t)

    def body(x_ref, m_ref, a_ref, b_ref, c1_ref, c2_ref, dx_ref, dm_ref, da_ref, db_ref):
        _zero_at_first([da_ref, db_ref], pl.program_id(0) == 0)
        _, vjp = jax.vjp(_res1_fn, x_ref[...], m_ref[...], a_ref[...], b_ref[...])
        d_x, d_m, d_a, d_b = vjp((c1_ref[...], c2_ref[...]))
        dx_ref[...] = d_x
        dm_ref[...] = d_m.astype(BF16)
        da_ref[...] += d_a
        db_ref[...] += d_b

    row, pspec = _row_spec(tm, D_MODEL), _full_spec((1, D_MODEL))
    return _pcall("res1_bwd", body, (t // tm,), [row, row, pspec, pspec, row, row], [row, row, pspec, pspec],
                  [_sds((t, D_MODEL)), _sds((t, D_MODEL), BF16), _sds((1, D_MODEL)), _sds((1, D_MODEL))])(
                      x, mix, g_pm, g_pf, d_x1, d_h2)


def _final_fn(x1, f, g_po, tgt):
    err = x1 + _rms(f, g_po) - tgt
    return 0.5 * jnp.sum(jnp.mean(err * err, axis=-1))


def _final(x1, f, g_po, tgt):
    t = x1.shape[0]
    tm = _row_tile(t)

    def body(x_ref, f_ref, g_ref, t_ref, loss_ref, dx_ref, df_ref, dg_ref):
        _zero_at_first([loss_ref, dg_ref], pl.program_id(0) == 0)
        loss, (d_x, d_f, d_g) = jax.value_and_grad(_final_fn, argnums=(0, 1, 2))(x_ref[...], f_ref[...], g_ref[...], t_ref[...])
        loss_ref[...] += jnp.broadcast_to(loss, loss_ref.shape)
        dx_ref[...] = d_x
        df_ref[...] = d_f.astype(BF16)
        dg_ref[...] += d_g

    row, pspec = _row_spec(tm, D_MODEL), _full_spec((1, D_MODEL))
    return _pcall("final", body, (t // tm,), [row, row, pspec, row], [_full_spec((8, LANE)), row, row, pspec],
                  [_sds((8, LANE)), _sds((t, D_MODEL)), _sds((t, D_MODEL), BF16), _sds((1, D_MODEL))])(x1, f, g_po, tgt)


def _rms1_bwd(x, g, d_h_a, d_h_b, d_x1):
    t = x.shape[0]
    tm = _row_tile(t)

    def body(x_ref, g_ref, dha_ref, dhb_ref, dx1_ref, dx_ref, dg_ref):
        _zero_at_first([dg_ref], pl.program_id(0) == 0)
        _, vjp = jax.vjp(_rms, x_ref[...], g_ref[...])
        d_x, d_g = vjp(dha_ref[...] + dhb_ref[...])
        dx_ref[...] = d_x + dx1_ref[...]
        dg_ref[...] += d_g

    row, pspec = _row_spec(tm, D_MODEL), _full_spec((1, D_MODEL))
    return _pcall("rms1_bwd", body, (t // tm,), [row, pspec, row, row, row], [row, pspec],
                  [_sds((t, D_MODEL)), _sds((1, D_MODEL))])(x, g, d_h_a, d_h_b, d_x1)


def _qkv_fn(mode):
    def fn(x, *wrows):
        y = _silu(_conv(x, wrows))
        if mode == "q":
            return _l2n(y) * (GDN_DK ** -0.5)
        if mode == "k":
            return _l2n(y)
        return y
    return fn


def _seq_spec(s, tc, off):
    return pl.BlockSpec((s, tc), lambda j, b, _o=off: (b, _o + j))


def _par_spec(rows, tc, off):
    return pl.BlockSpec((rows, tc), lambda j, b, _o=off: (0, _o + j))


def _gdn_conv_fwd(mode, proj, w, bsz, s):
    off = {"q": 0, "k": GDN_HEADS, "v": 2 * GDN_HEADS}[mode]
    fn = _qkv_fn(mode)

    def body(x_ref, w_ref, y_ref):
        y_ref[...] = fn(x_ref[...].astype(F32), *[w_ref[k:k + 1, :] for k in range(CONV_K)])

    return _pcall("gdn_conv_fwd_" + mode, body, (GDN_HEADS, bsz),
                  [_seq_spec(s, GDN_DK, off), _par_spec(CONV_K, GDN_DK, off)], _seq_spec(s, GDN_DK, 0),
                  _sds((bsz * s, GDN_QK)), sem=("parallel", "parallel"))(proj, w)


def _gdn_conv_bwd(mode, proj, w, d_y, bsz, s):
    off = {"q": 0, "k": GDN_HEADS, "v": 2 * GDN_HEADS}[mode]
    fn = _qkv_fn(mode)

    def body(x_ref, w_ref, dy_ref, dx_ref, dw_ref):
        _zero_at_first([dw_ref], pl.program_id(1) == 0)
        _, vjp = jax.vjp(fn, x_ref[...].astype(F32), *[w_ref[k:k + 1, :] for k in range(CONV_K)])
        grads = vjp(dy_ref[...])
        dx_ref[...] = grads[0].astype(BF16)
        for k in range(CONV_K):
            dw_ref[k:k + 1, :] += grads[1 + k]

    return _pcall("gdn_conv_bwd_" + mode, body, (GDN_HEADS, bsz),
                  [_seq_spec(s, GDN_DK, off), _par_spec(CONV_K, GDN_DK, off), _seq_spec(s, GDN_DK, 0)],
                  [_seq_spec(s, GDN_DK, 0), _par_spec(CONV_K, GDN_DK, 0)],
                  [_sds((bsz * s, GDN_QK), BF16), _sds((CONV_K, GDN_QK))], sem=("parallel", "arbitrary"))(proj, w, d_y)


def _ssd_conv_fn(x, bias, *wrows):
    return _silu(_conv(x, wrows) + bias)


_XBC_OFF = (5 * 1024) // LANE


def _ssd_conv_fwd(proj, w, bias, bsz, s):
    nt_ = SSD_CONV_CH // LANE

    def body(x_ref, w_ref, b_ref, y_ref):
        y_ref[...] = _ssd_conv_fn(x_ref[...].astype(F32), b_ref[...], *[w_ref[k:k + 1, :] for k in range(CONV_K)])

    return _pcall("ssd_conv_fwd", body, (nt_, bsz),
                  [_seq_spec(s, LANE, _XBC_OFF), _par_spec(CONV_K, LANE, 0), _par_spec(1, LANE, 0)], _seq_spec(s, LANE, 0),
                  _sds((bsz * s, SSD_CONV_CH)), sem=("parallel", "parallel"))(proj, w, bias)


def _ssd_conv_bwd(proj, w, bias, d_y, bsz, s):
    nt_ = SSD_CONV_CH // LANE

    def body(x_ref, w_ref, b_ref, dy_ref, dx_ref, dw_ref, db_ref):
        _zero_at_first([dw_ref, db_ref], pl.program_id(1) == 0)
        _, vjp = jax.vjp(_ssd_conv_fn, x_ref[...].astype(F32), b_ref[...], *[w_ref[k:k + 1, :] for k in range(CONV_K)])
        grads = vjp(dy_ref[...])
        dx_ref[...] = grads[0].astype(BF16)
        db_ref[...] += grads[1]
        for k in range(CONV_K):
            dw_ref[k:k + 1, :] += grads[2 + k]

    return _pcall("ssd_conv_bwd", body, (nt_, bsz),
                  [_seq_spec(s, LANE, _XBC_OFF), _par_spec(CONV_K, LANE, 0), _par_spec(1, LANE, 0), _seq_spec(s, LANE, 0)],
                  [_seq_spec(s, LANE, 0), _par_spec(CONV_K, LANE, 0), _par_spec(1, LANE, 0)],
                  [_sds((bsz * s, SSD_CONV_CH), BF16), _sds((CONV_K, SSD_CONV_CH)), _sds((1, SSD_CONV_CH))],
                  sem=("parallel", "arbitrary"))(proj, w, bias, d_y)


_FFN_TC = 256
_FFN_NT = D_FF // _FFN_TC


def _ffn_act_fn(xg, xu, bg, bu, *wrows):
    k_w = FFN_CONV_K
    gate = _conv(xg, wrows[:k_w]) + bg
    up = _conv(xu, wrows[k_w:]) + bu
    return _silu(gate) * up


def _ffn_act_fwd(u_pre, w, bias, bsz, s):
    def body(xg_ref, xu_ref, wg_ref, wu_ref, bg_ref, bu_ref, a_ref):
        rows = [wg_ref[k:k + 1, :] for k in range(FFN_CONV_K)] + [wu_ref[k:k + 1, :] for k in range(FFN_CONV_K)]
        a_ref[...] = _ffn_act_fn(xg_ref[...].astype(F32), xu_ref[...].astype(F32), bg_ref[...], bu_ref[...], *rows).astype(BF16)

    return _pcall("ffn_act_fwd", body, (_FFN_NT, bsz),
                  [_seq_spec(s, _FFN_TC, 0), _seq_spec(s, _FFN_TC, _FFN_NT),
                   _par_spec(FFN_CONV_K, _FFN_TC, 0), _par_spec(FFN_CONV_K, _FFN_TC, _FFN_NT),
                   _par_spec(1, _FFN_TC, 0), _par_spec(1, _FFN_TC, _FFN_NT)],
                  _seq_spec(s, _FFN_TC, 0), _sds((bsz * s, D_FF), BF16), sem=("parallel", "parallel"))(
                      u_pre, u_pre, w, w, bias, bias)


def _ffn_act_bwd(u_pre, w, bias, d_a, bsz, s):
    def body(xg_ref, xu_ref, wg_ref, wu_ref, bg_ref, bu_ref, da_ref, dg_ref, du_ref, dwg_ref, dwu_ref, dbg_ref, dbu_ref):
        _zero_at_first([dwg_ref, dwu_ref, dbg_ref, dbu_ref], pl.program_id(1) == 0)
        rows = [wg_ref[k:k + 1, :] for k in range(FFN_CONV_K)] + [wu_ref[k:k + 1, :] for k in range(FFN_CONV_K)]
        _, vjp = jax.vjp(_ffn_act_fn, xg_ref[...].astype(F32), xu_ref[...].astype(F32), bg_ref[...], bu_ref[...], *rows)
        grads = vjp(da_ref[...])
        dg_ref[...] = grads[0].astype(BF16)
        du_ref[...] = grads[1].astype(BF16)
        dbg_ref[...] += grads[2]
        dbu_ref[...] += grads[3]
        for k in range(FFN_CONV_K):
            dwg_ref[k:k + 1, :] += grads[4 + k]
            dwu_ref[k:k + 1, :] += grads[4 + FFN_CONV_K + k]

    seq0, par3, par1 = _seq_spec(s, _FFN_TC, 0), _par_spec(FFN_CONV_K, _FFN_TC, 0), _par_spec(1, _FFN_TC, 0)
    return _pcall("ffn_act_bwd", body, (_FFN_NT, bsz),
                  [seq0, _seq_spec(s, _FFN_TC, _FFN_NT), par3, _par_spec(FFN_CONV_K, _FFN_TC, _FFN_NT),
                   par1, _par_spec(1, _FFN_TC, _FFN_NT), seq0],
                  [seq0, seq0, par3, par3, par1, par1],
                  [_sds((bsz * s, D_FF), BF16), _sds((bsz * s, D_FF), BF16), _sds((FFN_CONV_K, D_FF)), _sds((FFN_CONV_K, D_FF)),
                   _sds((1, D_FF)), _sds((1, D_FF))], sem=("parallel", "arbitrary"))(u_pre, u_pre, w, w, bias, bias, d_a)


_GP = GDN_HEADS // 2
_SP = SSD_HEADS // 2


def _pair_lanes(p):
    return slice(2 * p * LANE, (2 * p + 1) * LANE), slice((2 * p + 1) * LANE, (2 * p + 2) * LANE)


_LAST = slice(CHUNK - 1, CHUNK)


def _gdn_args(p, q_ref, k_ref, v_ref, g_ref, b_ref, gr_ref):
    la, lb = _pair_lanes(p)
    return (q_ref[:, la], q_ref[:, lb], k_ref[:, la], k_ref[:, lb], v_ref[:, la], v_ref[:, lb], g_ref[:, la], g_ref[:, lb],
            b_ref[:, la], b_ref[:, lb], gr_ref[p], g_ref[_LAST, la], g_ref[_LAST, lb])


def _gdn_fwd(q, k, v, gc, beta, gc_row, bsz, n):
    def body(q_ref, k_ref, v_ref, g_ref, b_ref, gr_ref, o_ref, sin_ref, s_scr):
        _zero_at_first([s_scr], pl.program_id(1) == 0)
        flat = []
        for p in range(_GP):
            flat += [*_gdn_args(p, q_ref, k_ref, v_ref, g_ref, b_ref, gr_ref), s_scr[2 * p], s_scr[2 * p + 1]]
        sin_ref[...] = s_scr[...]
        outs = _gdn_multi(*flat)
        for p in range(_GP):
            la, lb = _pair_lanes(p)
            o_ref[:, la], o_ref[:, lb], s_scr[2 * p], s_scr[2 * p + 1] = outs[4 * p:4 * p + 4]

    tspec = pl.BlockSpec((CHUNK, GDN_V), lambda b, c: (b * n + c, 0))
    rspec = pl.BlockSpec((_GP, 1, LANE), lambda b, c: (b * n + c, 0, 0))
    sspec = pl.BlockSpec((GDN_HEADS, LANE, LANE), lambda b, c: (b * n + c, 0, 0))
    return _pcall("gdn_fwd", body, (bsz, n), [tspec] * 5 + [rspec], [tspec, sspec],
                  [_sds((bsz * n * CHUNK, GDN_V)), _sds((bsz * n * GDN_HEADS, LANE, LANE))],
                  scratch=[pltpu.VMEM((GDN_HEADS, LANE, LANE), F32)], sem=("parallel", "arbitrary"))(q, k, v, gc, beta, gc_row)


def _gdn_bwd(q, k, v, gc, beta, gc_row, s_in, d_o, bsz, n):
    def body(q_ref, k_ref, v_ref, g_ref, b_ref, gr_ref, sin_ref, do_ref, dq_ref, dk_ref, dv_ref, dg_ref, db_ref, dgr_ref, ds_scr):
        _zero_at_first([ds_scr], pl.program_id(1) == 0)
        flat, cots = [], []
        for p in range(_GP):
            la, lb = _pair_lanes(p)
            flat += [*_gdn_args(p, q_ref, k_ref, v_ref, g_ref, b_ref, gr_ref), sin_ref[2 * p], sin_ref[2 * p + 1]]
            cots += [do_ref[:, la], do_ref[:, lb], ds_scr[2 * p], ds_scr[2 * p + 1]]
        _, vjp = jax.vjp(_gdn_multi, *flat)
        grads = vjp(tuple(cots))
        for p in range(_GP):
            la, lb = _pair_lanes(p)
            cts = grads[_GDN_NARGS * p:_GDN_NARGS * (p + 1)]
            for ref, i in ((dq_ref, 0), (dk_ref, 2), (dv_ref, 4), (dg_ref, 6), (db_ref, 8)):
                ref[:, la] = cts[i]
                ref[:, lb] = cts[i + 1]
            dgr_ref[p] = cts[10]
            dg_ref[_LAST, la] += cts[11]
            dg_ref[_LAST, lb] += cts[12]
            ds_scr[2 * p] = cts[13]
            ds_scr[2 * p + 1] = cts[14]

    tspec = pl.BlockSpec((CHUNK, GDN_V), lambda b, c: (b * n + (n - 1 - c), 0))
    rspec = pl.BlockSpec((_GP, 1, LANE), lambda b, c: (b * n + (n - 1 - c), 0, 0))
    sspec = pl.BlockSpec((GDN_HEADS, LANE, LANE), lambda b, c: (b * n + (n - 1 - c), 0, 0))
    tok_shape = _sds((bsz * n * CHUNK, GDN_V))
    return _pcall("gdn_bwd", body, (bsz, n), [tspec] * 5 + [rspec, sspec, tspec], [tspec] * 5 + [rspec],
                  [tok_shape] * 5 + [_sds((bsz * n * _GP, 1, LANE))],
                  scratch=[pltpu.VMEM((GDN_HEADS, LANE, LANE), F32)], sem=("parallel", "arbitrary"))(
                      q, k, v, gc, beta, gc_row, s_in, d_o)


_B_OFF = SSD_D // LANE
_C_OFF = (SSD_D + SSD_BC) // LANE
_PPG = _SP // SSD_GROUPS


def _ssd_args(p, x_ref, dt_ref, a_ref, ar_ref):
    lp = slice(p * LANE, (p + 1) * LANE)
    gi = p // _PPG
    b_sl = slice((_B_OFF + gi) * LANE, (_B_OFF + gi + 1) * LANE)
    c_sl = slice((_C_OFF + gi) * LANE, (_C_OFF + gi + 1) * LANE)
    return (x_ref[:, lp], dt_ref[:, lp], a_ref[:, lp], ar_ref[p], a_ref[_LAST, lp], x_ref[:, b_sl], x_ref[:, c_sl])


def _ssd_fwd(xbc, dt, acs, acs_row, bsz, n):
    def body(x_ref, dt_ref, a_ref, ar_ref, y_ref, sin_ref, s_scr):
        _zero_at_first([s_scr], pl.program_id(1) == 0)
        flat = []
        for p in range(_SP):
            flat += [*_ssd_args(p, x_ref, dt_ref, a_ref, ar_ref), s_scr[p]]
        sin_ref[...] = s_scr[...]
        outs = _ssd_multi(*flat)
        for p in range(_SP):
            y_ref[:, p * LANE:(p + 1) * LANE], s_scr[p] = outs[2 * p:2 * p + 2]

    tspec = pl.BlockSpec((CHUNK, SSD_D), lambda b, c: (b * n + c, 0))
    return _pcall("ssd_fwd", body, (bsz, n),
                  [pl.BlockSpec((CHUNK, SSD_CONV_CH), lambda b, c: (b * n + c, 0)), tspec, tspec,
                   pl.BlockSpec((_SP, 1, LANE), lambda b, c: (b * n + c, 0, 0))],
                  [tspec, pl.BlockSpec((_SP, LANE, LANE), lambda b, c: (b * n + c, 0, 0))],
                  [_sds((bsz * n * CHUNK, SSD_D)), _sds((bsz * n * _SP, LANE, LANE))],
                  scratch=[pltpu.VMEM((_SP, LANE, LANE), F32)], sem=("parallel", "arbitrary"))(xbc, dt, acs, acs_row)


def _ssd_bwd(xbc, dt, acs, acs_row, s_in, d_y, d_x_skip, bsz, n):
    def body(x_ref, dt_ref, a_ref, ar_ref, sin_ref, dy_ref, dsk_ref, dx_ref, ddt_ref, da_ref, dar_ref, ds_scr):
        _zero_at_first([ds_scr], pl.program_id(1) == 0)
        d_b = [None] * SSD_GROUPS
        d_c = [None] * SSD_GROUPS
        flat, cots = [], []
        for p in range(_SP):
            flat += [*_ssd_args(p, x_ref, dt_ref, a_ref, ar_ref), sin_ref[p]]
            cots += [dy_ref[:, p * LANE:(p + 1) * LANE], ds_scr[p]]
        _, vjp = jax.vjp(_ssd_multi, *flat)
        grads = vjp(tuple(cots))
        for p in range(_SP):
            lp = slice(p * LANE, (p + 1) * LANE)
            gi = p // _PPG
            cts = grads[_SSD_NARGS * p:_SSD_NARGS * (p + 1)]
            dx_ref[:, lp] = cts[0] + dsk_ref[:, lp]
            ddt_ref[:, lp] = cts[1]
            da_ref[:, lp] = cts[2]
            dar_ref[p] = cts[3]
            da_ref[_LAST, lp] += cts[4]
            d_b[gi] = cts[5] if d_b[gi] is None else d_b[gi] + cts[5]
            d_c[gi] = cts[6] if d_c[gi] is None else d_c[gi] + cts[6]
            ds_scr[p] = cts[7]
        for gi in range(SSD_GROUPS):
            dx_ref[:, (_B_OFF + gi) * LANE:(_B_OFF + gi + 1) * LANE] = d_b[gi]
            dx_ref[:, (_C_OFF + gi) * LANE:(_C_OFF + gi + 1) * LANE] = d_c[gi]

    def rev(b, c):
        return b * n + (n - 1 - c)

    tspec = pl.BlockSpec((CHUNK, SSD_D), lambda b, c: (rev(b, c), 0))
    xspec = pl.BlockSpec((CHUNK, SSD_CONV_CH), lambda b, c: (rev(b, c), 0))
    rspec = pl.BlockSpec((_SP, 1, LANE), lambda b, c: (rev(b, c), 0, 0))
    tok_shape = _sds((bsz * n * CHUNK, SSD_D))
    return _pcall("ssd_bwd", body, (bsz, n),
                  [xspec, tspec, tspec, rspec, pl.BlockSpec((_SP, LANE, LANE), lambda b, c: (rev(b, c), 0, 0)), tspec, tspec],
                  [xspec, tspec, tspec, rspec],
                  [_sds((bsz * n * CHUNK, SSD_CONV_CH)), tok_shape, tok_shape, _sds((bsz * n * _SP, 1, LANE))],
                  scratch=[pltpu.VMEM((_SP, LANE, LANE), F32)], sem=("parallel", "arbitrary"))(
                      xbc, dt, acs, acs_row, s_in, d_y, d_x_skip)


def _add2(name, a, b):
    t, c = a.shape
    tm = _row_tile(t)

    def body(a_ref, b_ref, o_ref):
        o_ref[...] = a_ref[...] + b_ref[...]

    return _pcall(name, body, (t // tm,), [_row_spec(tm, c)] * 2, _row_spec(tm, c), _sds((t, c)), sem=("parallel",))(a, b)


def _rep(p, width):
    return jnp.repeat(p.reshape(-1), width).reshape(1, -1)


def _to_rows(narrow, lo, hi, bsz, n):
    heads = hi - lo
    a = narrow[:, lo:hi].reshape(bsz, n, CHUNK, heads)
    return jnp.transpose(a, (0, 1, 3, 2)).reshape(bsz * n * (heads // 2), 1, 2 * CHUNK)


def _from_rows(rows, heads, bsz, n):
    return jnp.transpose(rows.reshape(bsz, n, heads, CHUNK), (0, 1, 3, 2)).reshape(bsz * n * CHUNK, heads)


def _narrow_row(gdn_part, ssd_part):
    return jnp.pad(jnp.concatenate([gdn_part, ssd_part], axis=1), ((0, 0), (_G_LO, SMALL - _S_HI)))


def _local_step(x, tgt, p):
    bsz, s, _ = x.shape
    t = bsz * s
    n = s // CHUNK
    x2 = x.reshape(t, D_MODEL)
    tgt2 = tgt.reshape(t, D_MODEL)
    w_in = p["w_in"]
    w_big = jnp.concatenate([w_in[:, :4096], w_in[:, 4112:6672]], axis=1)
    w_small = jnp.concatenate([w_in[:, 4096:4112], w_in[:, 6672:6688], jnp.zeros((D_MODEL, SMALL - 32), BF16)], axis=1)
    gate_bias = _narrow_row(p["gdn_dt_bias"], p["ssd_dt_bias"])
    gate_a_log = _narrow_row(p["gdn_a_log"], p["ssd_a_log"])
    d_skip = _rep(p["ssd_d"], SSD_HEADDIM)

    h = _rms_fwd("rms0_fwd", x2, p["pre_mix_norm"])
    proj = _matmul("mm_in_big", h, w_big, "nn", BF16)
    small = _matmul("mm_in_small", h, w_small, "nn", F32)
    beta, gc, dt, acs, cum = _gates_fwd(small, gate_bias, gate_a_log)
    gc_row = _to_rows(cum, _G_LO, _G_HI, bsz, n)
    acs_row = _to_rows(cum, _S_LO, _S_HI, bsz, n)
    q = _gdn_conv_fwd("q", proj, p["gdn_conv_w"], bsz, s)
    k = _gdn_conv_fwd("k", proj, p["gdn_conv_w"], bsz, s)
    v = _gdn_conv_fwd("v", proj, p["gdn_conv_w"], bsz, s)
    o, gdn_s = _gdn_fwd(q, k, v, gc, beta, gc_row, bsz, n)
    o_a = _gdn_out_fwd(o, proj, p["gdn_norm_w"])
    xbc = _ssd_conv_fwd(proj, p["ssd_conv_w"], p["ssd_conv_b"], bsz, s)
    y, ssd_s = _ssd_fwd(xbc, dt, acs, acs_row, bsz, n)
    o_s = _ssd_out_fwd(y, xbc, proj, d_skip, p["ssd_norm_w"])
    ocat = jnp.concatenate([o_a, o_s], axis=1)
    mix = _matmul("mm_out", ocat, p["w_out"], "nn", F32)
    x1, h2 = _res1_fwd(x2, mix, p["post_mix_norm"], p["pre_ffn_norm"])
    u_pre = _matmul("mm_up", h2, p["w_up"], "nn", BF16)
    act = _ffn_act_fwd(u_pre, p["ffn_conv_w"], p["ffn_conv_b"], bsz, s)
    f = _matmul("mm_down", act, p["w_down"], "nn", F32)
    loss_acc, d_out, d_f, g_post_ffn = _final(x1, f, p["post_ffn_norm"], tgt2)

    grads = {"post_ffn_norm": g_post_ffn}
    d_act = _matmul("mm_down_dx", d_f, p["w_down"], "nt", F32)
    grads["w_down"] = _matmul("mm_down_dw", act, d_f, "tn", F32)
    d_gate, d_up, dwg, dwu, dbg, dbu = _ffn_act_bwd(u_pre, p["ffn_conv_w"], p["ffn_conv_b"], d_act, bsz, s)
    grads["ffn_conv_w"] = jnp.concatenate([dwg, dwu], axis=1)
    grads["ffn_conv_b"] = jnp.concatenate([dbg, dbu], axis=1)
    d_u = jnp.concatenate([d_gate, d_up], axis=1)
    d_h2 = _matmul("mm_up_dx", d_u, p["w_up"], "nt", F32)
    grads["w_up"] = _matmul("mm_up_dw", h2, d_u, "tn", F32)
    d_x1, d_mix, grads["post_mix_norm"], grads["pre_ffn_norm"] = _res1_bwd(
        x2, mix, p["post_mix_norm"], p["pre_ffn_norm"], d_out, d_h2)
    d_ocat = _matmul("mm_out_dx", d_mix, p["w_out"], "nt", F32)
    grads["w_out"] = _matmul("mm_out_dw", ocat, d_mix, "tn", F32)

    d_o, d_za, grads["gdn_norm_w"] = _gdn_out_bwd(o, proj, p["gdn_norm_w"], d_ocat)
    d_q, d_k, d_v, d_gc, d_beta, d_gc_row = _gdn_bwd(q, k, v, gc, beta, gc_row, gdn_s, d_o, bsz, n)
    d_qp, dwq = _gdn_conv_bwd("q", proj, p["gdn_conv_w"], d_q, bsz, s)
    d_kp, dwk = _gdn_conv_bwd("k", proj, p["gdn_conv_w"], d_k, bsz, s)
    d_vp, dwv = _gdn_conv_bwd("v", proj, p["gdn_conv_w"], d_v, bsz, s)
    grads["gdn_conv_w"] = jnp.concatenate([dwq, dwk, dwv], axis=1)

    d_y, d_xs_skip, d_zs, d_dskip, grads["ssd_norm_w"] = _ssd_out_bwd(y, xbc, proj, d_skip, p["ssd_norm_w"], d_ocat)
    d_xbc, d_dt, d_acs, d_acs_row = _ssd_bwd(xbc, dt, acs, acs_row, ssd_s, d_y, d_xs_skip, bsz, n)
    d_xbcp, grads["ssd_conv_w"], grads["ssd_conv_b"] = _ssd_conv_bwd(proj, p["ssd_conv_w"], p["ssd_conv_b"], d_xbc, bsz, s)

    d_cum = jnp.concatenate([jnp.zeros((t, _G_LO), F32), _from_rows(d_gc_row, GDN_HEADS, bsz, n),
                             _from_rows(d_acs_row, SSD_HEADS, bsz, n), jnp.zeros((t, SMALL - _S_HI), F32)], axis=1)
    d_small, d_gate_bias, d_gate_a_log = _gates_bwd(small, gate_bias, gate_a_log, (d_beta, d_gc, d_dt, d_acs, d_cum))
    grads["gdn_dt_bias"], grads["ssd_dt_bias"] = d_gate_bias[:, _G_LO:_G_HI], d_gate_bias[:, _S_LO:_S_HI]
    grads["gdn_a_log"], grads["ssd_a_log"] = d_gate_a_log[:, _G_LO:_G_HI], d_gate_a_log[:, _S_LO:_S_HI]
    d_proj = jnp.concatenate([d_qp, d_kp, d_vp, d_za, d_zs, d_xbcp], axis=1)
    d_h_big = _matmul("mm_in_big_dx", d_proj, w_big, "nt", F32)
    d_h_small = _matmul("mm_in_small_dx", d_small, w_small, "nt", F32)
    dw_big = _matmul("mm_in_big_dw", h, d_proj, "tn", F32)
    dw_small = _matmul("mm_in_small_dw", h, d_small, "tn", F32)
    grads["w_in"] = jnp.concatenate([dw_big[:, :4096], dw_small[:, :16], dw_big[:, 4096:], dw_small[:, 16:32]], axis=1)
    grad_x, grads["pre_mix_norm"] = _rms1_bwd(x2, p["pre_mix_norm"], d_h_big, d_h_small, d_x1)
    grads["ssd_d"] = _head_sums(d_dskip)[:1, :SSD_HEADS]
    return loss_acc, grad_x.reshape(bsz, s, D_MODEL), grads


def _head_sums(wide):
    def body(x_ref, o_ref):
        r, c = _iota2((D_MODEL, SMALL))
        o_ref[...] = _mask_dot(jnp.broadcast_to(x_ref[...], (8, D_MODEL)), ((r >> 6) == c).astype(F32), NN, True)

    return _pcall("head_sums", body, (1,), [_full_spec((1, D_MODEL))], _full_spec((8, SMALL)), _sds((8, SMALL)))(wide)


def _adamw_fn(w, g, m, v):
    m = ADAM_B1 * m + (1.0 - ADAM_B1) * g
    v = ADAM_B2 * v + (1.0 - ADAM_B2) * (g * g)
    m_hat = m / (1.0 - ADAM_B1 ** ADAM_STEP)
    v_hat = v / (1.0 - ADAM_B2 ** ADAM_STEP)
    delta = -ADAM_LR * (m_hat / (jnp.sqrt(v_hat) + ADAM_EPS) + ADAM_WD * w)
    return delta, m, v


def _adamw(name, w, g, m, v):
    r, c = w.shape
    tr = _pick(r, (256, 176, 128, 64, 8))

    def body(w_ref, g_ref, m_ref, v_ref, d_ref, m2_ref, v2_ref):
        d, m2, v2 = _adamw_fn(w_ref[...], g_ref[...], m_ref[...], v_ref[...])
        d_ref[...] = d
        m2_ref[...] = m2
        v2_ref[...] = v2

    spec = pl.BlockSpec((tr, c), lambda i: (i, 0))
    return _pcall(name, body, (r // tr,), [spec] * 4, [spec] * 3, [_sds((r, c))] * 3, sem=("parallel",))(w, g, m, v)


_ANY = pl.BlockSpec(memory_space=pl.ANY)
_OTHER_CHIPS = ((1, 0), (0, 1), (1, 1))


def _coords():
    return lax.axis_index("x"), lax.axis_index("y"), lax.axis_index("c")


def _flip(v, f):
    return 1 - v if f else v


def _gather_chips(arrs, split):
    n = len(arrs)

    def body(*refs):
        ins, outs = refs[:n], refs[n:2 * n]
        send_sems, recv_sems, fwd_send_sems, fwd_recv_sems, own_send_sems, own_recv_sems = refs[2 * n:]
        x, y, c = _coords()
        me = 2 * x + y
        sib = (x, y, 1 - c)

        def rows(a, core):
            if not split[a]:
                return slice(None)
            half = arrs[a].shape[0] // 2
            return pl.ds(core * half, half)

        sends = []
        own = []
        for a in range(n):
            cp = pltpu.make_async_remote_copy(ins[a], outs[a].at[me], own_send_sems.at[a], own_recv_sems.at[a],
                                              device_id=sib, device_id_type=MESH)
            cp.start()
            own.append(cp)
        for a in range(n):
            for j, (fx, fy) in enumerate(_OTHER_CHIPS):
                cp = pltpu.make_async_remote_copy(ins[a].at[rows(a, c)], outs[a].at[me, rows(a, c)],
                                                  send_sems.at[a * 3 + j], recv_sems.at[a * 3 + j],
                                                  device_id=(_flip(x, fx), _flip(y, fy), c), device_id_type=MESH)
                cp.start()
                sends.append(cp)
        for a in range(n):
            for j, (fx, fy) in enumerate(_OTHER_CHIPS):
                src = 2 * _flip(x, fx) + _flip(y, fy)
                landed = outs[a].at[src, rows(a, c)]
                pltpu.make_async_remote_copy(landed, landed, send_sems.at[a * 3 + j], recv_sems.at[a * 3 + j],
                                             device_id=(_flip(x, fx), _flip(y, fy), c), device_id_type=MESH).wait_recv()
                if split[a]:
                    fw = pltpu.make_async_remote_copy(landed, landed, fwd_send_sems.at[a * 3 + j], fwd_recv_sems.at[a * 3 + j],
                                                      device_id=sib, device_id_type=MESH)
                    fw.start()
                    sends.append(fw)
        for a in range(n):
            if split[a]:
                for j, (fx, fy) in enumerate(_OTHER_CHIPS):
                    src = 2 * _flip(x, fx) + _flip(y, fy)
                    theirs = outs[a].at[src, rows(a, 1 - c)]
                    pltpu.make_async_remote_copy(theirs, theirs, fwd_send_sems.at[a * 3 + j], fwd_recv_sems.at[a * 3 + j],
                                                 device_id=sib, device_id_type=MESH).wait_recv()
        for cp in own:
            cp.wait_recv()
        for cp in sends + own:
            cp.wait_send()

    return pl.pallas_call(
        body, name="gather_chips", out_shape=[_sds((N_CHIPS,) + a.shape, a.dtype) for a in arrs],
        in_specs=[_ANY] * n, out_specs=[_ANY] * n,
        scratch_shapes=[pltpu.SemaphoreType.DMA((3 * n,))] * 4 + [pltpu.SemaphoreType.DMA((n,))] * 2,
        compiler_params=pltpu.CompilerParams(has_side_effects=True))(*arrs)


_PEERS = tuple((fx, fy, fc) for fx in (0, 1) for fy in (0, 1) for fc in (0, 1))[1:]


def _allreduce_small(x):
    r = x.shape[0]

    def body(x_ref, o_ref, buf, send_sems, recv_sems):
        cx, cy, cc = _coords()
        me = 4 * cx + 2 * cy + cc
        sends = []
        for j, (fx, fy, fc) in enumerate(_PEERS):
            cp = pltpu.make_async_remote_copy(x_ref, buf.at[me], send_sems.at[j], recv_sems.at[j],
                                              device_id=(_flip(cx, fx), _flip(cy, fy), _flip(cc, fc)), device_id_type=MESH)
            cp.start()
            sends.append(cp)
        buf[pl.ds(me, 1)] = x_ref[...][None]
        for j, (fx, fy, fc) in enumerate(_PEERS):
            src = 4 * _flip(cx, fx) + 2 * _flip(cy, fy) + _flip(cc, fc)
            pltpu.make_async_remote_copy(x_ref, buf.at[src], send_sems.at[j], recv_sems.at[j],
                                         device_id=(_flip(cx, fx), _flip(cy, fy), _flip(cc, fc)), device_id_type=MESH).wait_recv()
        for cp in sends:
            cp.wait_send()
        acc = buf[0]
        for d in range(1, N_DEV):
            acc = acc + buf[d]
        o_ref[...] = acc

    vm = pl.BlockSpec(memory_space=pltpu.VMEM)
    return pl.pallas_call(
        body, name="allreduce_small", out_shape=_sds((r, LANE)), in_specs=[vm], out_specs=vm,
        scratch_shapes=[pltpu.VMEM((N_DEV, r, LANE), F32), pltpu.SemaphoreType.DMA((7,)), pltpu.SemaphoreType.DMA((7,))],
        compiler_params=pltpu.CompilerParams(has_side_effects=True, vmem_limit_bytes=VMEM_LIMIT))(x)


def _pair_send_other_half(arrs):
    n = len(arrs)

    def body(*refs):
        ins, outs = refs[:n], refs[n:2 * n]
        send_sems, recv_sems = refs[2 * n:]
        x, y, c = _coords()
        sends = []
        for a in range(n):
            half = ins[a].shape[1] // 2
            cp = pltpu.make_async_remote_copy(ins[a].at[:, pl.ds((1 - c) * half, half), :], outs[a], send_sems.at[a], recv_sems.at[a],
                                              device_id=(x, y, 1 - c), device_id_type=MESH)
            cp.start()
            sends.append(cp)
        for cp in sends:
            cp.wait_recv()
        for cp in sends:
            cp.wait_send()

    return pl.pallas_call(
        body, name="pair_reduce_send", out_shape=[_sds((a.shape[0], a.shape[1] // 2, a.shape[2]), a.dtype) for a in arrs],
        in_specs=[_ANY] * n, out_specs=[_ANY] * n, scratch_shapes=[pltpu.SemaphoreType.DMA((n,))] * 2,
        compiler_params=pltpu.CompilerParams(has_side_effects=True))(*arrs)


def _pair_fill(arrs):
    n = len(arrs)

    def body(*refs):
        bufs = refs[n:2 * n]
        send_sems, recv_sems = refs[2 * n:]
        x, y, c = _coords()
        sends = []
        for a in range(n):
            cp = pltpu.make_async_remote_copy(bufs[a].at[c], bufs[a].at[c], send_sems.at[a], recv_sems.at[a],
                                              device_id=(x, y, 1 - c), device_id_type=MESH)
            cp.start()
            sends.append(cp)
        for a in range(n):
            theirs = bufs[a].at[1 - c]
            pltpu.make_async_remote_copy(theirs, theirs, send_sems.at[a], recv_sems.at[a],
                                         device_id=(x, y, 1 - c), device_id_type=MESH).wait_recv()
        for cp in sends:
            cp.wait_send()

    return pl.pallas_call(
        body, name="pair_gather", out_shape=[_sds(a.shape, a.dtype) for a in arrs], in_specs=[_ANY] * n, out_specs=[_ANY] * n,
        scratch_shapes=[pltpu.SemaphoreType.DMA((n,))] * 2, input_output_aliases={a: a for a in range(n)},
        compiler_params=pltpu.CompilerParams(has_side_effects=True))(*arrs)


def _scatter_chips(arrs):
    n = len(arrs)

    def body(*refs):
        ins, outs = refs[:n], refs[n:2 * n]
        send_sems, recv_sems = refs[2 * n:]
        x, y, c = _coords()
        sends = []
        for a in range(n):
            for j, (fx, fy) in enumerate(_OTHER_CHIPS):
                to = 2 * _flip(x, fx) + _flip(y, fy)
                cp = pltpu.make_async_remote_copy(ins[a].at[to], outs[a].at[j], send_sems.at[a * 3 + j], recv_sems.at[a * 3 + j],
                                                  device_id=(_flip(x, fx), _flip(y, fy), c), device_id_type=MESH)
                cp.start()
                sends.append(cp)
        for cp in sends:
            cp.wait_recv()
        for cp in sends:
            cp.wait_send()

    return pl.pallas_call(
        body, name="scatter_chips", out_shape=[_sds((3,) + a.shape[1:], a.dtype) for a in arrs],
        in_specs=[_ANY] * n, out_specs=[_ANY] * n, scratch_shapes=[pltpu.SemaphoreType.DMA((3 * n,))] * 2,
        compiler_params=pltpu.CompilerParams(has_side_effects=True))(*arrs)


def _pair_add(name, full, recv, core):
    _, r, c = full.shape
    half = r // 2
    tr = _pick(half, (256, 176, 128, 64, 8))
    nb = half // tr

    def body(c_ref, a_ref, b_ref, o_ref, ob_ref):
        s = a_ref[...] + b_ref[...]
        o_ref[...] = s
        ob_ref[...] = s.astype(BF16)

    blk = pl.BlockSpec((1, tr, c), lambda k, i, cref: (k, i, 0))
    grid_spec = pltpu.PrefetchScalarGridSpec(
        num_scalar_prefetch=1, grid=(N_CHIPS, nb),
        in_specs=[pl.BlockSpec((1, tr, c), lambda k, i, cref: (k, cref[0] * nb + i, 0)), blk], out_specs=[blk, blk])
    return pl.pallas_call(
        body, name=name, out_shape=[_sds((N_CHIPS, half, c)), _sds((N_CHIPS, half, c), BF16)], grid_spec=grid_spec,
        compiler_params=pltpu.CompilerParams(dimension_semantics=("parallel", "parallel"), vmem_limit_bytes=VMEM_LIMIT))(
            core, full, recv)


def _chip_sum(name, landed, own, where):
    _, r, c = landed.shape
    tr = _pick(r, (256, 176, 128, 64, 16))

    def body(w_ref, l_ref, o_ref, s_ref):
        s_ref[0] = ((o_ref[0] + l_ref[0].astype(F32)) + l_ref[1].astype(F32)) + l_ref[2].astype(F32)

    grid_spec = pltpu.PrefetchScalarGridSpec(
        num_scalar_prefetch=1, grid=(r // tr,),
        in_specs=[pl.BlockSpec((3, tr, c), lambda i, wref: (0, i, 0)),
                  pl.BlockSpec((1, tr, c), lambda i, wref: (wref[0], i, 0))],
        out_specs=pl.BlockSpec((1, tr, c), lambda i, wref: (wref[1], i, 0)))
    return pl.pallas_call(
        body, name=name, out_shape=_sds((2, r, c)), grid_spec=grid_spec,
        compiler_params=pltpu.CompilerParams(dimension_semantics=("parallel",), vmem_limit_bytes=VMEM_LIMIT))(where, landed, own)


_WEIGHTS = ("pre_mix_norm", "w_in", "gdn_conv_w", "gdn_a_log", "gdn_dt_bias", "gdn_norm_w", "ssd_conv_w", "ssd_conv_b",
            "ssd_a_log", "ssd_dt_bias", "ssd_d", "ssd_norm_w", "w_out", "post_mix_norm", "pre_ffn_norm", "w_up",
            "ffn_conv_w", "ffn_conv_b", "w_down", "post_ffn_norm")
_BIG = ("w_in", "w_out", "w_up", "w_down")
_COL_SHARDED_SMALL = ("gdn_conv_w", "ssd_conv_w", "ffn_conv_w")
_SMALL = tuple(k for k in _WEIGHTS if k not in _BIG)


def _pack(arrs):
    flat = jnp.concatenate([a.reshape(-1) for a in arrs])
    rows = -(-flat.shape[0] // (8 * LANE)) * 8
    return jnp.pad(flat, (0, rows * LANE - flat.shape[0])).reshape(rows, LANE)


def _unpack(packed, shapes):
    flat = packed.reshape(-1)
    out, off = [], 0
    for shp in shapes:
        size = 1
        for d in shp:
            size *= d
        out.append(flat[off:off + size].reshape(shp))
        off += size
    return out


def _cols_to_chips(a):
    r, c4 = a.shape
    return jnp.transpose(a.reshape(r, N_CHIPS, c4 // N_CHIPS), (1, 0, 2))


def _chips_to_cols(a):
    k, r, c = a.shape
    return jnp.transpose(a, (1, 0, 2)).reshape(r, k * c)


def kernel(x, pre_mix_norm, w_in, gdn_conv_w, gdn_a_log, gdn_dt_bias, gdn_norm_w, ssd_conv_w, ssd_conv_b, ssd_a_log, ssd_dt_bias, ssd_d, ssd_norm_w, w_out, post_mix_norm, pre_ffn_norm, w_up, ffn_conv_w, ffn_conv_b, w_down, post_ffn_norm, loss_target, m_pre_mix_norm, m_w_in, m_gdn_conv_w, m_gdn_a_log, m_gdn_dt_bias, m_gdn_norm_w, m_ssd_conv_w, m_ssd_conv_b, m_ssd_a_log, m_ssd_dt_bias, m_ssd_d, m_ssd_norm_w, m_w_out, m_post_mix_norm, m_pre_ffn_norm, m_w_up, m_ffn_conv_w, m_ffn_conv_b, m_w_down, m_post_ffn_norm, v_pre_mix_norm, v_w_in, v_gdn_conv_w, v_gdn_a_log, v_gdn_dt_bias, v_gdn_norm_w, v_ssd_conv_w, v_ssd_conv_b, v_ssd_a_log, v_ssd_dt_bias, v_ssd_d, v_ssd_norm_w, v_w_out, v_post_mix_norm, v_pre_ffn_norm, v_w_up, v_ffn_conv_w, v_ffn_conv_b, v_w_down, v_post_ffn_norm):
    w = dict(zip(_WEIGHTS, (pre_mix_norm, w_in, gdn_conv_w, gdn_a_log, gdn_dt_bias, gdn_norm_w, ssd_conv_w, ssd_conv_b,
                            ssd_a_log, ssd_dt_bias, ssd_d, ssd_norm_w, w_out, post_mix_norm, pre_ffn_norm, w_up,
                            ffn_conv_w, ffn_conv_b, w_down, post_ffn_norm)))
    m = dict(zip(_WEIGHTS, (m_pre_mix_norm, m_w_in, m_gdn_conv_w, m_gdn_a_log, m_gdn_dt_bias, m_gdn_norm_w, m_ssd_conv_w,
                            m_ssd_conv_b, m_ssd_a_log, m_ssd_dt_bias, m_ssd_d, m_ssd_norm_w, m_w_out, m_post_mix_norm,
                            m_pre_ffn_norm, m_w_up, m_ffn_conv_w, m_ffn_conv_b, m_w_down, m_post_ffn_norm)))
    v = dict(zip(_WEIGHTS, (v_pre_mix_norm, v_w_in, v_gdn_conv_w, v_gdn_a_log, v_gdn_dt_bias, v_gdn_norm_w, v_ssd_conv_w,
                            v_ssd_conv_b, v_ssd_a_log, v_ssd_dt_bias, v_ssd_d, v_ssd_norm_w, v_w_out, v_post_mix_norm,
                            v_pre_ffn_norm, v_w_up, v_ffn_conv_w, v_ffn_conv_b, v_w_down, v_post_ffn_norm)))
    cx, cy, cc = _coords()
    chip = 2 * cx + cy

    shards = [w[k][0].astype(BF16) for k in _BIG] + [w[k][0] for k in _COL_SHARDED_SMALL]
    g_in, g_out, g_up, g_down, g_gcw, g_scw, g_fcw = _gather_chips(shards, [True] * len(_BIG) + [False] * len(_COL_SHARDED_SMALL))
    p = {k: w[k] for k in _SMALL if k not in _COL_SHARDED_SMALL}
    p["w_in"] = _chips_to_cols(g_in)
    p["w_up"] = _chips_to_cols(g_up)
    p["w_out"] = g_out.reshape(-1, D_MODEL)
    p["w_down"] = g_down.reshape(-1, D_MODEL)
    p["gdn_conv_w"] = _chips_to_cols(g_gcw)
    p["ssd_conv_w"] = _chips_to_cols(g_scw)
    p["ffn_conv_w"] = _chips_to_cols(g_fcw)

    loss_acc, grad_x, grads = _local_step(x, loss_target, p)
    loss = lax.psum(loss_acc[0, 0], ("x", "y", "c"))

    small_full_shapes = [grads[k].shape for k in _SMALL]
    summed = _unpack(_allreduce_small(_pack([grads[k] for k in _SMALL])), small_full_shapes)
    g_small = dict(zip(_SMALL, summed))
    for k in _COL_SHARDED_SMALL:
        width = w[k].shape[2]
        g_small[k] = lax.dynamic_slice_in_dim(g_small[k], chip * width, width, axis=1)

    big = [_cols_to_chips(grads["w_in"]), grads["w_out"].reshape(N_CHIPS, -1, D_MODEL),
           _cols_to_chips(grads["w_up"]), grads["w_down"].reshape(N_CHIPS, -1, D_MODEL)]
    from_sibling = _pair_send_other_half(big)
    core = cc.astype(jnp.int32).reshape(1)
    pair_sums = [_pair_add("pair_add_" + k, a, b, core) for k, a, b in zip(_BIG, big, from_sibling)]
    landed = _scatter_chips([ps[1] for ps in pair_sums])
    where = jnp.stack([chip, cc]).astype(jnp.int32)
    mine = [_chip_sum("chip_sum_" + k, a, ps[0], where) for k, a, ps in zip(_BIG, landed, pair_sums)]
    both = _pair_fill(mine)
    g_big = {k: a.reshape(-1, a.shape[2]) for k, a in zip(_BIG, both)}

    out_g, out_d, out_m, out_v = {}, {}, {}, {}
    for k in _BIG:
        out_g[k] = g_big[k][None]
        d_, m_, v_ = _adamw("adamw_" + k, w[k][0], g_big[k], m[k][0], v[k][0])
        out_d[k], out_m[k], out_v[k] = d_[None], m_[None], v_[None]
    shapes = [w[k].shape for k in _SMALL]
    for k in _SMALL:
        out_g[k] = g_small[k].reshape(w[k].shape)
    packed = [_pack([d[k] for k in _SMALL]) for d in (w, out_g, m, v)]
    d_p, m_p, v_p = _adamw("adamw_small", *packed)
    for dst, src in ((out_d, d_p), (out_m, m_p), (out_v, v_p)):
        dst.update(zip(_SMALL, _unpack(src, shapes)))
    return (loss, grad_x, *[out_g[k] for k in _WEIGHTS], *[out_d[k] for k in _WEIGHTS],
            *[out_m[k] for k in _WEIGHTS], *[out_v[k] for k in _WEIGHTS])
```

```python
import functools

import jax
import jax.numpy as jnp
from jax import lax
from jax.experimental import pallas as pl
from jax.experimental.pallas import tpu as pltpu

F32 = jnp.float32
BF16 = jnp.bfloat16

D_MODEL = 1024
GDN_HEADS = 8
GDN_DK = 128
SSD_HEADS = 16
SSD_HEADDIM = 64
SSD_GROUPS = 2
SSD_STATE = 128
CONV_K = 4
CHUNK = 64
D_FF = 2816
FFN_CONV_K = 3
EPS = 1e-6
GDN_QK = GDN_HEADS * GDN_DK
GDN_V = GDN_QK
SSD_D = SSD_HEADS * SSD_HEADDIM
SSD_BC = SSD_GROUPS * SSD_STATE
SSD_CONV_CH = SSD_D + 2 * SSD_BC
BIG = 4 * 1024 + 1024 + SSD_CONV_CH
SMALL = 128
D_IN_PROJ = 6688
LANE = 128
PAIR = 2 * CHUNK
NEG = -1e30
VMEM_LIMIT = 56 * 1024 * 1024

ADAM_LR = 0.001
ADAM_B1 = 0.9
ADAM_B2 = 0.999
ADAM_EPS = 1e-08
ADAM_WD = 0.01
ADAM_STEP = 10

N_CHIPS = 4
N_DEV = 8
MESH = pl.DeviceIdType.MESH

NN = ((1,), (0,))
NT = ((1,), (1,))
TN = ((0,), (0,))


def _bdot(a, b, dims):
    return lax.dot_general(a.astype(BF16), b.astype(BF16), (dims, ((), ())), preferred_element_type=F32)


def _split3(a):
    hi = a.astype(BF16)
    r1 = a - hi.astype(F32)
    mid = r1.astype(BF16)
    return hi, mid, (r1 - mid.astype(F32)).astype(BF16)


@jax.custom_vjp
def _nn(a, b):
    return _bdot(a, b, NN)


@jax.custom_vjp
def _nt(a, b):
    return _bdot(a, b, NT)


@jax.custom_vjp
def _tn(a, b):
    return _bdot(a, b, TN)


_nn.defvjp(lambda a, b: (_nn(a, b), (a, b)), lambda r, g: (_nt(g, r[1]), _tn(r[0], g)))
_nt.defvjp(lambda a, b: (_nt(a, b), (a, b)), lambda r, g: (_nn(g, r[1]), _tn(g, r[0])))
_tn.defvjp(lambda a, b: (_tn(a, b), (a, b)), lambda r, g: (_nt(r[1], g), _nn(r[0], g)))


def _mask_dot(x, mask, dims, x_first):
    acc = None
    for piece in _split3(x):
        term = _bdot(piece, mask, dims) if x_first else _bdot(mask, piece, dims)
        acc = term if acc is None else acc + term
    return acc


@jax.custom_vjp
def _cst_left(cst, x):
    return _mask_dot(x, cst, NN, False)


_cst_left.defvjp(lambda cst, x: (_cst_left(cst, x), cst), lambda cst, g: (jnp.zeros_like(cst), _mask_dot(g, cst, TN, False)))


@jax.custom_vjp
def _cst_right(x, cst):
    return _mask_dot(x, cst, NN, True)


_cst_right.defvjp(lambda x, cst: (_cst_right(x, cst), cst), lambda cst, g: (_mask_dot(g, cst, NT, True), jnp.zeros_like(cst)))


def _lin_left(cst):
    return functools.partial(_cst_left, cst)


def _lin_right(cst):
    return lambda x: _cst_right(x, cst)


@jax.custom_vjp
def _tri_inv_m1(a):
    pm = [-x for x in a]
    ap = list(a)
    for _ in range(5):
        ap = [_bdot(x, x, NN) for x in ap]
        pm = [(p + x) + _bdot(p, x, NN) for p, x in zip(pm, ap)]
    return pm


def _tri_inv_m1_bwd(pm, g):
    t = [gi + _bdot(p, gi, TN) for p, gi in zip(pm, g)]
    return ([-(ti + _bdot(ti, p, NT)) for p, ti in zip(pm, t)],)


_tri_inv_m1.defvjp(lambda a: (lambda pm: (pm, pm))(_tri_inv_m1(a)), _tri_inv_m1_bwd)


@jax.custom_vjp
def _top(x):
    return x[: x.shape[0] // 2]


_top.defvjp(lambda x: (_top(x), None), lambda _, g: (jnp.concatenate([g, jnp.zeros_like(g)], axis=0),))


@jax.custom_vjp
def _bot(x):
    return x[x.shape[0] // 2:]


_bot.defvjp(lambda x: (_bot(x), None), lambda _, g: (jnp.concatenate([jnp.zeros_like(g), g], axis=0),))


@jax.custom_vjp
def _vstack(a, b):
    return jnp.concatenate([a, b], axis=0)


_vstack.defvjp(lambda a, b: (_vstack(a, b), None), lambda _, g: (g[: g.shape[0] // 2], g[g.shape[0] // 2:]))


def _shift_dn_raw(x, s):
    if s == 0:
        return x
    r = pltpu.roll(x, s, axis=0)
    ri = lax.broadcasted_iota(jnp.int32, x.shape, 0)
    return jnp.where(ri >= s, r, 0.0)


def _shift_up_raw(x, s):
    if s == 0:
        return x
    n = x.shape[0]
    r = pltpu.roll(x, n - s, axis=0)
    ri = lax.broadcasted_iota(jnp.int32, x.shape, 0)
    return jnp.where(ri < n - s, r, 0.0)


@functools.partial(jax.custom_vjp, nondiff_argnums=(1,))
def _shift_dn(x, s):
    return _shift_dn_raw(x, s)


_shift_dn.defvjp(lambda x, s: (_shift_dn_raw(x, s), None), lambda s, _, g: (_shift_up_raw(g, s),))


def _conv(x, wrows):
    k_w = len(wrows)
    acc = wrows[k_w - 1] * x
    for k in range(k_w - 1):
        acc = acc + wrows[k] * _shift_dn(x, k_w - 1 - k)
    return acc


def _silu(x):
    return x * jax.nn.sigmoid(x)


def _rms(x, w):
    return x * lax.rsqrt(jnp.mean(x * x, axis=-1, keepdims=True) + EPS) * w


def _l2n(x):
    return x * lax.rsqrt(jnp.sum(x * x, axis=-1, keepdims=True) + EPS)


def _iota2(shape):
    return lax.broadcasted_iota(jnp.int32, shape, 0), lax.broadcasted_iota(jnp.int32, shape, 1)


_GDN_NARGS = 15
_SSD_NARGS = 8


def _gdn_multi(*flat):
    pairs = [flat[i:i + _GDN_NARGS] for i in range(0, len(flat), _GDN_NARGS)]
    idx = range(len(pairs))
    ri, ci = _iota2((PAIR, PAIR))
    blk = ((ri >= CHUNK) & (ci >= CHUNK)) | ((ri < CHUNK) & (ci < CHUNK))
    causal = blk & (ri >= ci)
    strict = blk & (ri > ci)
    q = [_vstack(p[0], p[1]) for p in pairs]
    k = [_vstack(p[2], p[3]) for p in pairs]
    v = [_vstack(p[4], p[5]) for p in pairs]
    gc = [_vstack(p[6], p[7]) for p in pairs]
    beta = [_vstack(p[8], p[9]) for p in pairs]
    glast = [_vstack(jnp.broadcast_to(p[11], (CHUNK, LANE)), jnp.broadcast_to(p[12], (CHUNK, LANE))) for p in pairs]
    sa = [p[13] for p in pairs]
    sb = [p[14] for p in pairs]
    decay = [jnp.exp(jnp.where(causal, gc[i] - jnp.broadcast_to(pairs[i][10], (PAIR, PAIR)), NEG)) for i in idx]
    eg = [jnp.exp(x) for x in gc]
    kbeta = [k[i] * beta[i] for i in idx]
    pm = _tri_inv_m1([jnp.where(strict, _nt(kbeta[i], k[i]) * decay[i], 0.0) for i in idx])
    qk = [_nt(q[i], k[i]) * decay[i] for i in idx]
    rhs_v = [v[i] * beta[i] for i in idx]
    rhs_k = [kbeta[i] * eg[i] for i in idx]
    u = [rhs_v[i] + _nn(pm[i], rhs_v[i]) for i in idx]
    w = [rhs_k[i] + _nn(pm[i], rhs_k[i]) for i in idx]
    q_dec = [q[i] * eg[i] for i in idx]
    k_dec = [k[i] * jnp.exp(glast[i] - gc[i]) for i in idx]
    gl = [jnp.exp(x) for x in glast]
    w_s = [_vstack(_nn(_top(w[i]), sa[i]), _nn(_bot(w[i]), sb[i])) for i in idx]
    q_s = [_vstack(_nn(_top(q_dec[i]), sa[i]), _nn(_bot(q_dec[i]), sb[i])) for i in idx]
    v_new = [u[i] - w_s[i] for i in idx]
    o = [q_s[i] + _nn(qk[i], v_new[i]) for i in idx]
    sa2 = [sa[i] * _vstack(_top(gl[i]), _top(gl[i])) + _tn(_top(k_dec[i]), _top(v_new[i])) for i in idx]
    sb2 = [sb[i] * _vstack(_bot(gl[i]), _bot(gl[i])) + _tn(_bot(k_dec[i]), _bot(v_new[i])) for i in idx]
    out = []
    for i in idx:
        out += [_top(o[i]), _bot(o[i]), sa2[i], sb2[i]]
    return tuple(out)


def _ssd_multi(*flat):
    pairs = [flat[i:i + _SSD_NARGS] for i in range(0, len(flat), _SSD_NARGS)]
    idx = range(len(pairs))
    ri, ci = _iota2((CHUNK, PAIR))
    causal = ri >= jnp.where(ci >= CHUNK, ci - CHUNK, ci)
    xdt = [p[0] * p[1] for p in pairs]
    acs = [p[2] for p in pairs]
    alast = [jnp.broadcast_to(p[4], (CHUNK, PAIR)) for p in pairs]
    lmat = [jnp.exp(jnp.where(causal, acs[i] - jnp.broadcast_to(pairs[i][3], (CHUNK, PAIR)), NEG)) for i in idx]
    cb2 = [_nt(p[6], _vstack(p[5], p[5])) for p in pairs]
    xblk = [_vstack(jnp.where(ci < CHUNK, x, 0.0), jnp.where(ci >= CHUNK, x, 0.0)) for x in xdt]
    y_off = [_nn(pairs[i][6], pairs[i][7]) * jnp.exp(acs[i]) for i in idx]
    y = [_nn(cb2[i] * lmat[i], xblk[i]) + y_off[i] for i in idx]
    el = [jnp.exp(x) for x in alast]
    st2 = [pairs[i][7] * _vstack(el[i], el[i]) + _tn(pairs[i][5], xdt[i] * jnp.exp(alast[i] - acs[i])) for i in idx]
    out = []
    for i in idx:
        out += [y[i], st2[i]]
    return tuple(out)


def _pcall(name, body, grid, in_specs, out_specs, out_shape, scratch=(), sem=None, aliases=None):
    if sem is None:
        sem = ("arbitrary",) * len(grid)
    return pl.pallas_call(
        functools.partial(body),
        out_shape=out_shape,
        grid=grid,
        in_specs=in_specs,
        out_specs=out_specs,
        scratch_shapes=scratch,
        input_output_aliases=aliases or {},
        name=name,
        compiler_params=pltpu.CompilerParams(dimension_semantics=sem, vmem_limit_bytes=VMEM_LIMIT),
    )


def _sds(shape, dtype=F32):
    return jax.ShapeDtypeStruct(shape, dtype)


def _row_spec(tm, width, colblock=0):
    return pl.BlockSpec((tm, width), lambda i, _c=colblock: (i, _c))


def _full_spec(shape):
    nd = len(shape)
    return pl.BlockSpec(shape, lambda *_: (0,) * nd)


def _zero_at_first(refs, first):
    @pl.when(first)
    def _():
        for r in refs:
            r[...] = jnp.zeros(r.shape, r.dtype)


def _pick(n, prefs):
    for p in prefs:
        if n % p == 0:
            return p
    return n


def _matmul(name, a, b, mode, out_dtype, tiles=None):
    def want(i, dim):
        return [tiles[i]] if tiles is not None and dim % tiles[i] == 0 else []

    if mode == "tn":
        r, m = a.shape
        n = b.shape[1]
        tm = _pick(m, want(0, m) + [1024, 1408])
        tn = _pick(n, want(1, n) + [512, 256, 128])
        tk = _pick(r, want(2, r) + [1024, 512, 256, 128, 64])

        nc = _pick(tn, (512, 256, 128))

        def body(a_ref, b_ref, o_ref):
            _zero_at_first([o_ref], pl.program_id(2) == 0)
            for c0 in range(0, tn, nc):
                o_ref[:, c0:c0 + nc] += _bdot(a_ref[...], b_ref[:, c0:c0 + nc], TN)

        return _pcall(
            name, body, (m // tm, n // tn, r // tk),
            [pl.BlockSpec((tk, tm), lambda i, j, k: (k, i)), pl.BlockSpec((tk, tn), lambda i, j, k: (k, j))],
            pl.BlockSpec((tm, tn), lambda i, j, k: (i, j)), _sds((m, n), out_dtype),
            sem=("parallel", "parallel", "arbitrary"))(a, b)
    m, k = a.shape
    n = b.shape[1] if mode == "nn" else b.shape[0]
    tm = _pick(m, want(0, m) + ([1024, 512, 256, 128, 64] if k <= 2816 else [512, 256, 128, 64]))
    tn = _pick(n, want(1, n) + [512, 256, 128])
    dims = NN if mode == "nn" else NT

    nc = _pick(tn, (512, 256, 128))

    def body(a_ref, b_ref, o_ref):
        for c0 in range(0, tn, nc):
            b_blk = b_ref[:, c0:c0 + nc] if mode == "nn" else b_ref[c0:c0 + nc, :]
            o_ref[:, c0:c0 + nc] = _bdot(a_ref[...], b_blk, dims).astype(o_ref.dtype)

    b_spec = pl.BlockSpec((k, tn), lambda i, j: (0, j)) if mode == "nn" else pl.BlockSpec((tn, k), lambda i, j: (j, 0))
    return _pcall(
        name, body, (m // tm, n // tn), [pl.BlockSpec((tm, k), lambda i, j: (i, 0)), b_spec],
        pl.BlockSpec((tm, tn), lambda i, j: (i, j)), _sds((m, n), out_dtype), sem=("parallel", "parallel"))(a, b)


def _row_tile(t):
    return _pick(t, (256, 128, 64))


def _rms_fwd(name, x, g):
    t = x.shape[0]
    tm = _row_tile(t)

    def body(x_ref, g_ref, h_ref):
        h_ref[...] = _rms(x_ref[...], g_ref[...]).astype(BF16)

    return _pcall(name, body, (t // tm,), [_row_spec(tm, D_MODEL), _full_spec((1, D_MODEL))], _row_spec(tm, D_MODEL),
                  _sds((t, D_MODEL), BF16), sem=("parallel",))(x, g)


_G_LO, _G_HI = GDN_HEADS, 2 * GDN_HEADS
_S_LO, _S_HI = 2 * GDN_HEADS, 2 * GDN_HEADS + SSD_HEADS


def _gates_fn(small, bias, a_log):
    tm = small.shape[0]
    r, c = _iota2((SMALL, D_MODEL))
    e_b = (r == (c >> 7)).astype(F32)
    e_a = (r == _G_LO + (c >> 7)).astype(F32)
    e_dt = (r == _S_LO + (c >> 6)).astype(F32)
    rr, cc = _iota2((tm, tm))
    in_chunk_tril = (((rr >> 6) == (cc >> 6)) & (rr >= cc)).astype(F32)
    sig = jax.nn.sigmoid(small)
    sp = jax.nn.softplus(small + bias)
    cum = _lin_left(in_chunk_tril)(-jnp.exp(a_log) * sp)
    return _lin_right(e_b)(sig), _lin_right(e_a)(cum), _lin_right(e_dt)(sp), _lin_right(e_dt)(cum), cum


def _gates_fwd(small, bias, a_log):
    t = small.shape[0]
    tm = _row_tile(t)

    def body(s_ref, p0, p1, *outs):
        for o, v in zip(outs, _gates_fn(s_ref[...], p0[...], p1[...])):
            o[...] = v

    pspec = _full_spec((1, SMALL))
    return _pcall("gates_fwd", body, (t // tm,), [_row_spec(tm, SMALL), pspec, pspec],
                  [_row_spec(tm, D_MODEL)] * 4 + [_row_spec(tm, SMALL)], [_sds((t, D_MODEL))] * 4 + [_sds((t, SMALL))],
                  sem=("parallel",))(small, bias, a_log)


def _gates_bwd(small, bias, a_log, cts):
    t = small.shape[0]
    tm = _row_tile(t)

    def body(s_ref, p0, p1, c0, c1, c2, c3, c4, ds_ref, db_ref, da_ref):
        _zero_at_first([db_ref, da_ref], pl.program_id(0) == 0)
        _, vjp = jax.vjp(_gates_fn, s_ref[...], p0[...], p1[...])
        d_s, d_b, d_a = vjp((c0[...], c1[...], c2[...], c3[...], c4[...]))
        ds_ref[...] = d_s.astype(BF16)
        db_ref[...] += d_b
        da_ref[...] += d_a

    pspec = _full_spec((1, SMALL))
    return _pcall("gates_bwd", body, (t // tm,),
                  [_row_spec(tm, SMALL), pspec, pspec] + [_row_spec(tm, D_MODEL)] * 4 + [_row_spec(tm, SMALL)],
                  [_row_spec(tm, SMALL), pspec, pspec], [_sds((t, SMALL), BF16), _sds((1, SMALL)), _sds((1, SMALL))])(
                      small, bias, a_log, *cts)


def _gdn_out_fn(o, z, w):
    return _rms(o, w) * _silu(z)


def _gdn_out_fwd(o, proj, gn):
    t = o.shape[0]
    tm = _row_tile(t)

    def body(o_ref, z_ref, w_ref, y_ref):
        for h in range(GDN_HEADS):
            sl = slice(h * GDN_DK, (h + 1) * GDN_DK)
            y_ref[:, sl] = _gdn_out_fn(o_ref[:, sl], z_ref[:, sl].astype(F32), w_ref[...]).astype(BF16)

    return _pcall("gdn_out_fwd", body, (t // tm,), [_row_spec(tm, GDN_V), _row_spec(tm, GDN_V, 3), _full_spec((1, GDN_DK))],
                  _row_spec(tm, GDN_V), _sds((t, GDN_V + SSD_D), BF16), sem=("parallel",))(o, proj, gn)


def _gdn_out_bwd(o, proj, gn, d_ocat):
    t = o.shape[0]
    tm = _row_tile(t)

    def body(o_ref, z_ref, w_ref, dy_ref, do_ref, dz_ref, dw_ref):
        _zero_at_first([dw_ref], pl.program_id(0) == 0)
        for h in range(GDN_HEADS):
            sl = slice(h * GDN_DK, (h + 1) * GDN_DK)
            _, vjp = jax.vjp(_gdn_out_fn, o_ref[:, sl], z_ref[:, sl].astype(F32), w_ref[...])
            d_o, d_z, d_w = vjp(dy_ref[:, sl])
            do_ref[:, sl] = d_o
            dz_ref[:, sl] = d_z.astype(BF16)
            dw_ref[...] += d_w

    return _pcall("gdn_out_bwd", body, (t // tm,),
                  [_row_spec(tm, GDN_V), _row_spec(tm, GDN_V, 3), _full_spec((1, GDN_DK)), _row_spec(tm, GDN_V, 0)],
                  [_row_spec(tm, GDN_V), _row_spec(tm, GDN_V, 3), _full_spec((1, GDN_DK))],
                  [_sds((t, GDN_V)), _sds((t, BIG), BF16), _sds((1, GDN_DK))])(o, proj, gn, d_ocat)


def _ssd_out_fn(y, xs, z, d_skip, w):
    return _rms((y + d_skip * xs) * _silu(z), w)


_SSD_GW = SSD_D // SSD_GROUPS


def _ssd_out_fwd(y, xbc, proj, d_skip, nw, ocat):
    t = y.shape[0]
    tm = _row_tile(t)

    def body(y_ref, x_ref, z_ref, d_ref, w_ref, _, o_ref):
        for gi in range(SSD_GROUPS):
            sl = slice(gi * _SSD_GW, (gi + 1) * _SSD_GW)
            o_ref[:, sl] = _ssd_out_fn(y_ref[:, sl], x_ref[:, sl], z_ref[:, sl].astype(F32), d_ref[:, sl], w_ref[:, sl]).astype(BF16)

    pspec = _full_spec((1, SSD_D))
    return _pcall("ssd_out_fwd", body, (t // tm,),
                  [_row_spec(tm, SSD_D), _row_spec(tm, SSD_D, 0), _row_spec(tm, SSD_D, 4), pspec, pspec, _ANY],
                  _row_spec(tm, SSD_D, 1), _sds(ocat.shape, BF16), sem=("parallel",), aliases={5: 0})(
                      y, xbc, proj, d_skip, nw, ocat)


def _ssd_out_bwd(y, xbc, proj, d_skip, nw, d_ocat, d_proj):
    t = y.shape[0]
    tm = _row_tile(t)

    def body(y_ref, x_ref, z_ref, d_ref, w_ref, do_ref, _, dy_ref, dx_ref, dz_ref, dd_ref, dw_ref):
        _zero_at_first([dd_ref, dw_ref], pl.program_id(0) == 0)
        for gi in range(SSD_GROUPS):
            sl = slice(gi * _SSD_GW, (gi + 1) * _SSD_GW)
            _, vjp = jax.vjp(_ssd_out_fn, y_ref[:, sl], x_ref[:, sl], z_ref[:, sl].astype(F32), d_ref[:, sl], w_ref[:, sl])
            d_y, d_x, d_z, d_d, d_w = vjp(do_ref[:, sl])
            dy_ref[:, sl] = d_y
            dx_ref[:, sl] = d_x
            dz_ref[:, sl] = d_z.astype(BF16)
            dd_ref[:, sl] += d_d
            dw_ref[:, sl] += d_w

    pspec = _full_spec((1, SSD_D))
    row = _row_spec(tm, SSD_D)
    return _pcall("ssd_out_bwd", body, (t // tm,),
                  [row, _row_spec(tm, SSD_D, 0), _row_spec(tm, SSD_D, 4), pspec, pspec, _row_spec(tm, SSD_D, 1), _ANY],
                  [row, row, _row_spec(tm, SSD_D, 4), pspec, pspec],
                  [_sds((t, SSD_D)), _sds((t, SSD_D)), _sds((t, BIG), BF16), _sds((1, SSD_D)), _sds((1, SSD_D))],
                  aliases={6: 2})(y, xbc, proj, d_skip, nw, d_ocat, d_proj)


def _res1_fn(x, mix, g_pm, g_pf):
    x1 = x + _rms(mix, g_pm)
    return x1, _rms(x1, g_pf)


def _res1_fwd(x, mix, g_pm, g_pf):
    t = x.shape[0]
    tm = _row_tile(t)

    def body(x_ref, m_ref, a_ref, b_ref, x1_ref, h2_ref):
        x1, h2 = _res1_fn(x_ref[...], m_ref[...], a_ref[...], b_ref[...])
        x1_ref[...] = x1
        h2_ref[...] = h2.astype(BF16)

    row, pspec = _row_spec(tm, D_MODEL), _full_spec((1, D_MODEL))
    return _pcall("res1_fwd", body, (t // tm,), [row, row, pspec, pspec], [row, row],
                  [_sds((t, D_MODEL)), _sds((t, D_MODEL), BF16)], sem=("parallel",))(x, mix, g_pm, g_pf)


def _res1_bwd(x, mix, g_pm, g_pf, d_x1, d_h2):
    t = x.shape[0]
    tm = _row_tile(t)

    def body(x_ref, m_ref, a_ref, b_ref, c1_ref, c2_ref, dx_ref, dm_ref, da_ref, db_ref):
        _zero_at_first([da_ref, db_ref], pl.program_id(0) == 0)
        _, vjp = jax.vjp(_res1_fn, x_ref[...], m_ref[...], a_ref[...], b_ref[...])
        d_x, d_m, d_a, d_b = vjp((c1_ref[...], c2_ref[...]))
        dx_ref[...] = d_x
        dm_ref[...] = d_m.astype(BF16)
        da_ref[...] += d_a
        db_ref[...] += d_b

    row, pspec = _row_spec(tm, D_MODEL), _full_spec((1, D_MODEL))
    return _pcall("res1_bwd", body, (t // tm,), [row, row, pspec, pspec, row, row], [row, row, pspec, pspec],
                  [_sds((t, D_MODEL)), _sds((t, D_MODEL), BF16), _sds((1, D_MODEL)), _sds((1, D_MODEL))])(
                      x, mix, g_pm, g_pf, d_x1, d_h2)


def _final_fn(x1, f, g_po, tgt):
    err = x1 + _rms(f, g_po) - tgt
    return 0.5 * jnp.sum(jnp.mean(err * err, axis=-1))


def _final(x1, f, g_po, tgt):
    t = x1.shape[0]
    tm = _row_tile(t)

    def body(x_ref, f_ref, g_ref, t_ref, loss_ref, dx_ref, df_ref, dg_ref):
        _zero_at_first([loss_ref, dg_ref], pl.program_id(0) == 0)
        loss, (d_x, d_f, d_g) = jax.value_and_grad(_final_fn, argnums=(0, 1, 2))(x_ref[...], f_ref[...], g_ref[...], t_ref[...])
        loss_ref[...] += jnp.broadcast_to(loss, loss_ref.shape)
        dx_ref[...] = d_x
        df_ref[...] = d_f.astype(BF16)
        dg_ref[...] += d_g

    row, pspec = _row_spec(tm, D_MODEL), _full_spec((1, D_MODEL))
    return _pcall("final", body, (t // tm,), [row, row, pspec, row], [_full_spec((8, LANE)), row, row, pspec],
                  [_sds((8, LANE)), _sds((t, D_MODEL)), _sds((t, D_MODEL), BF16), _sds((1, D_MODEL))])(x1, f, g_po, tgt)


def _rms1_bwd(x, g, d_h_a, d_h_b, d_x1):
    t = x.shape[0]
    tm = _row_tile(t)

    def body(x_ref, g_ref, dha_ref, dhb_ref, dx1_ref, dx_ref, dg_ref):
        _zero_at_first([dg_ref], pl.program_id(0) == 0)
        _, vjp = jax.vjp(_rms, x_ref[...], g_ref[...])
        d_x, d_g = vjp(dha_ref[...] + dhb_ref[...])
        dx_ref[...] = d_x + dx1_ref[...]
        dg_ref[...] += d_g

    row, pspec = _row_spec(tm, D_MODEL), _full_spec((1, D_MODEL))
    return _pcall("rms1_bwd", body, (t // tm,), [row, pspec, row, row, row], [row, pspec],
                  [_sds((t, D_MODEL)), _sds((1, D_MODEL))])(x, g, d_h_a, d_h_b, d_x1)


def _qkv_fn(mode):
    def fn(x, *wrows):
        y = _silu(_conv(x, wrows))
        if mode == "q":
            return _l2n(y) * (GDN_DK ** -0.5)
        if mode == "k":
            return _l2n(y)
        return y
    return fn


def _seq_spec(s, tc, off):
    return pl.BlockSpec((s, tc), lambda j, b, _o=off: (b, _o + j))


def _par_spec(rows, tc, off):
    return pl.BlockSpec((rows, tc), lambda j, b, _o=off: (0, _o + j))


def _gdn_conv_fwd(mode, proj, w, bsz, s):
    off = {"q": 0, "k": GDN_HEADS, "v": 2 * GDN_HEADS}[mode]
    fn = _qkv_fn(mode)

    def body(x_ref, w_ref, y_ref):
        y_ref[...] = fn(x_ref[...].astype(F32), *[w_ref[k:k + 1, :] for k in range(CONV_K)])

    return _pcall("gdn_conv_fwd_" + mode, body, (GDN_HEADS, bsz),
                  [_seq_spec(s, GDN_DK, off), _par_spec(CONV_K, GDN_DK, off)], _seq_spec(s, GDN_DK, 0),
                  _sds((bsz * s, GDN_QK)), sem=("parallel", "parallel"))(proj, w)


def _gdn_conv_bwd(mode, proj, w, d_y, d_proj, bsz, s):
    off = {"q": 0, "k": GDN_HEADS, "v": 2 * GDN_HEADS}[mode]
    fn = _qkv_fn(mode)

    def body(x_ref, w_ref, dy_ref, _, dx_ref, dw_ref):
        _zero_at_first([dw_ref], pl.program_id(1) == 0)
        _, vjp = jax.vjp(fn, x_ref[...].astype(F32), *[w_ref[k:k + 1, :] for k in range(CONV_K)])
        grads = vjp(dy_ref[...])
        dx_ref[...] = grads[0].astype(BF16)
        for k in range(CONV_K):
            dw_ref[k:k + 1, :] += grads[1 + k]

    return _pcall("gdn_conv_bwd_" + mode, body, (GDN_HEADS, bsz),
                  [_seq_spec(s, GDN_DK, off), _par_spec(CONV_K, GDN_DK, off), _seq_spec(s, GDN_DK, 0), _ANY],
                  [_seq_spec(s, GDN_DK, off), _par_spec(CONV_K, GDN_DK, 0)],
                  [_sds(d_proj.shape, BF16), _sds((CONV_K, GDN_QK))], sem=("parallel", "arbitrary"), aliases={3: 0})(
                      proj, w, d_y, d_proj)


def _ssd_conv_fn(x, bias, *wrows):
    return _silu(_conv(x, wrows) + bias)


_XBC_OFF = (5 * 1024) // LANE


def _ssd_conv_fwd(proj, w, bias, bsz, s):
    nt_ = SSD_CONV_CH // LANE

    def body(x_ref, w_ref, b_ref, y_ref):
        y_ref[...] = _ssd_conv_fn(x_ref[...].astype(F32), b_ref[...], *[w_ref[k:k + 1, :] for k in range(CONV_K)])

    return _pcall("ssd_conv_fwd", body, (nt_, bsz),
                  [_seq_spec(s, LANE, _XBC_OFF), _par_spec(CONV_K, LANE, 0), _par_spec(1, LANE, 0)], _seq_spec(s, LANE, 0),
                  _sds((bsz * s, SSD_CONV_CH)), sem=("parallel", "parallel"))(proj, w, bias)


def _ssd_conv_bwd(proj, w, bias, d_y, d_proj, bsz, s):
    nt_ = SSD_CONV_CH // LANE

    def body(x_ref, w_ref, b_ref, dy_ref, _, dx_ref, dw_ref, db_ref):
        _zero_at_first([dw_ref, db_ref], pl.program_id(1) == 0)
        _, vjp = jax.vjp(_ssd_conv_fn, x_ref[...].astype(F32), b_ref[...], *[w_ref[k:k + 1, :] for k in range(CONV_K)])
        grads = vjp(dy_ref[...])
        dx_ref[...] = grads[0].astype(BF16)
        db_ref[...] += grads[1]
        for k in range(CONV_K):
            dw_ref[k:k + 1, :] += grads[2 + k]

    return _pcall("ssd_conv_bwd", body, (nt_, bsz),
                  [_seq_spec(s, LANE, _XBC_OFF), _par_spec(CONV_K, LANE, 0), _par_spec(1, LANE, 0), _seq_spec(s, LANE, 0), _ANY],
                  [_seq_spec(s, LANE, _XBC_OFF), _par_spec(CONV_K, LANE, 0), _par_spec(1, LANE, 0)],
                  [_sds(d_proj.shape, BF16), _sds((CONV_K, SSD_CONV_CH)), _sds((1, SSD_CONV_CH))],
                  sem=("parallel", "arbitrary"), aliases={4: 0})(proj, w, bias, d_y, d_proj)


_FFN_TC = 256
_FFN_NT = D_FF // _FFN_TC


def _ffn_act_fn(xg, xu, bg, bu, *wrows):
    k_w = FFN_CONV_K
    gate = _conv(xg, wrows[:k_w]) + bg
    up = _conv(xu, wrows[k_w:]) + bu
    return _silu(gate) * up


def _ffn_act_fwd(u_pre, w, bias, bsz, s):
    def body(xg_ref, xu_ref, wg_ref, wu_ref, bg_ref, bu_ref, a_ref):
        rows = [wg_ref[k:k + 1, :] for k in range(FFN_CONV_K)] + [wu_ref[k:k + 1, :] for k in range(FFN_CONV_K)]
        a_ref[...] = _ffn_act_fn(xg_ref[...].astype(F32), xu_ref[...].astype(F32), bg_ref[...], bu_ref[...], *rows).astype(BF16)

    return _pcall("ffn_act_fwd", body, (_FFN_NT, bsz),
                  [_seq_spec(s, _FFN_TC, 0), _seq_spec(s, _FFN_TC, _FFN_NT),
                   _par_spec(FFN_CONV_K, _FFN_TC, 0), _par_spec(FFN_CONV_K, _FFN_TC, _FFN_NT),
                   _par_spec(1, _FFN_TC, 0), _par_spec(1, _FFN_TC, _FFN_NT)],
                  _seq_spec(s, _FFN_TC, 0), _sds((bsz * s, D_FF), BF16), sem=("parallel", "parallel"))(
                      u_pre, u_pre, w, w, bias, bias)


def _ffn_act_bwd(u_pre, w, bias, d_a, bsz, s):
    def body(xg_ref, xu_ref, wg_ref, wu_ref, bg_ref, bu_ref, da_ref, dg_ref, du_ref, dwg_ref, dwu_ref, dbg_ref, dbu_ref):
        _zero_at_first([dwg_ref, dwu_ref, dbg_ref, dbu_ref], pl.program_id(1) == 0)
        rows = [wg_ref[k:k + 1, :] for k in range(FFN_CONV_K)] + [wu_ref[k:k + 1, :] for k in range(FFN_CONV_K)]
        _, vjp = jax.vjp(_ffn_act_fn, xg_ref[...].astype(F32), xu_ref[...].astype(F32), bg_ref[...], bu_ref[...], *rows)
        grads = vjp(da_ref[...])
        dg_ref[...] = grads[0].astype(BF16)
        du_ref[...] = grads[1].astype(BF16)
        dbg_ref[...] += grads[2]
        dbu_ref[...] += grads[3]
        for k in range(FFN_CONV_K):
            dwg_ref[k:k + 1, :] += grads[4 + k]
            dwu_ref[k:k + 1, :] += grads[4 + FFN_CONV_K + k]

    seq0, par3, par1 = _seq_spec(s, _FFN_TC, 0), _par_spec(FFN_CONV_K, _FFN_TC, 0), _par_spec(1, _FFN_TC, 0)
    return _pcall("ffn_act_bwd", body, (_FFN_NT, bsz),
                  [seq0, _seq_spec(s, _FFN_TC, _FFN_NT), par3, _par_spec(FFN_CONV_K, _FFN_TC, _FFN_NT),
                   par1, _par_spec(1, _FFN_TC, _FFN_NT), seq0],
                  [seq0, seq0, par3, par3, par1, par1],
                  [_sds((bsz * s, D_FF), BF16), _sds((bsz * s, D_FF), BF16), _sds((FFN_CONV_K, D_FF)), _sds((FFN_CONV_K, D_FF)),
                   _sds((1, D_FF)), _sds((1, D_FF))], sem=("parallel", "arbitrary"))(u_pre, u_pre, w, w, bias, bias, d_a)


_GP = GDN_HEADS // 2
_SP = SSD_HEADS // 2


def _pair_lanes(p):
    return slice(2 * p * LANE, (2 * p + 1) * LANE), slice((2 * p + 1) * LANE, (2 * p + 2) * LANE)


_LAST = slice(CHUNK - 1, CHUNK)


def _gdn_args(p, q_ref, k_ref, v_ref, g_ref, b_ref, gr_ref):
    la, lb = _pair_lanes(p)
    return (q_ref[:, la], q_ref[:, lb], k_ref[:, la], k_ref[:, lb], v_ref[:, la], v_ref[:, lb], g_ref[:, la], g_ref[:, lb],
            b_ref[:, la], b_ref[:, lb], gr_ref[p], g_ref[_LAST, la], g_ref[_LAST, lb])


def _gdn_fwd(q, k, v, gc, beta, gc_row, bsz, n):
    def body(q_ref, k_ref, v_ref, g_ref, b_ref, gr_ref, o_ref, sin_ref, s_scr):
        _zero_at_first([s_scr], pl.program_id(1) == 0)
        flat = []
        for p in range(_GP):
            flat += [*_gdn_args(p, q_ref, k_ref, v_ref, g_ref, b_ref, gr_ref), s_scr[2 * p], s_scr[2 * p + 1]]
        sin_ref[...] = s_scr[...]
        outs = _gdn_multi(*flat)
        for p in range(_GP):
            la, lb = _pair_lanes(p)
            o_ref[:, la], o_ref[:, lb], s_scr[2 * p], s_scr[2 * p + 1] = outs[4 * p:4 * p + 4]

    tspec = pl.BlockSpec((CHUNK, GDN_V), lambda b, c: (b * n + c, 0))
    rspec = pl.BlockSpec((_GP, 1, LANE), lambda b, c: (b * n + c, 0, 0))
    sspec = pl.BlockSpec((GDN_HEADS, LANE, LANE), lambda b, c: (b * n + c, 0, 0))
    return _pcall("gdn_fwd", body, (bsz, n), [tspec] * 5 + [rspec], [tspec, sspec],
                  [_sds((bsz * n * CHUNK, GDN_V)), _sds((bsz * n * GDN_HEADS, LANE, LANE))],
                  scratch=[pltpu.VMEM((GDN_HEADS, LANE, LANE), F32)], sem=("parallel", "arbitrary"))(q, k, v, gc, beta, gc_row)


def _gdn_bwd(q, k, v, gc, beta, gc_row, s_in, d_o, bsz, n):
    def body(q_ref, k_ref, v_ref, g_ref, b_ref, gr_ref, sin_ref, do_ref, dq_ref, dk_ref, dv_ref, dg_ref, db_ref, dgr_ref, ds_scr):
        _zero_at_first([ds_scr], pl.program_id(1) == 0)
        flat, cots = [], []
        for p in range(_GP):
            la, lb = _pair_lanes(p)
            flat += [*_gdn_args(p, q_ref, k_ref, v_ref, g_ref, b_ref, gr_ref), sin_ref[2 * p], sin_ref[2 * p + 1]]
            cots += [do_ref[:, la], do_ref[:, lb], ds_scr[2 * p], ds_scr[2 * p + 1]]
        _, vjp = jax.vjp(_gdn_multi, *flat)
        grads = vjp(tuple(cots))
        for p in range(_GP):
            la, lb = _pair_lanes(p)
            cts = grads[_GDN_NARGS * p:_GDN_NARGS * (p + 1)]
            for ref, i in ((dq_ref, 0), (dk_ref, 2), (dv_ref, 4), (dg_ref, 6), (db_ref, 8)):
                ref[:, la] = cts[i]
                ref[:, lb] = cts[i + 1]
            dgr_ref[p] = cts[10]
            dg_ref[_LAST, la] += cts[11]
            dg_ref[_LAST, lb] += cts[12]
            ds_scr[2 * p] = cts[13]
            ds_scr[2 * p + 1] = cts[14]

    tspec = pl.BlockSpec((CHUNK, GDN_V), lambda b, c: (b * n + (n - 1 - c), 0))
    rspec = pl.BlockSpec((_GP, 1, LANE), lambda b, c: (b * n + (n - 1 - c), 0, 0))
    sspec = pl.BlockSpec((GDN_HEADS, LANE, LANE), lambda b, c: (b * n + (n - 1 - c), 0, 0))
    tok_shape = _sds((bsz * n * CHUNK, GDN_V))
    return _pcall("gdn_bwd", body, (bsz, n), [tspec] * 5 + [rspec, sspec, tspec], [tspec] * 5 + [rspec],
                  [tok_shape] * 5 + [_sds((bsz * n * _GP, 1, LANE))],
                  scratch=[pltpu.VMEM((GDN_HEADS, LANE, LANE), F32)], sem=("parallel", "arbitrary"))(
                      q, k, v, gc, beta, gc_row, s_in, d_o)


_B_OFF = SSD_D // LANE
_C_OFF = (SSD_D + SSD_BC) // LANE
_PPG = _SP // SSD_GROUPS


def _ssd_args(p, x_ref, dt_ref, a_ref, ar_ref):
    lp = slice(p * LANE, (p + 1) * LANE)
    gi = p // _PPG
    b_sl = slice((_B_OFF + gi) * LANE, (_B_OFF + gi + 1) * LANE)
    c_sl = slice((_C_OFF + gi) * LANE, (_C_OFF + gi + 1) * LANE)
    return (x_ref[:, lp], dt_ref[:, lp], a_ref[:, lp], ar_ref[p], a_ref[_LAST, lp], x_ref[:, b_sl], x_ref[:, c_sl])


def _ssd_fwd(xbc, dt, acs, acs_row, bsz, n):
    def body(x_ref, dt_ref, a_ref, ar_ref, y_ref, sin_ref, s_scr):
        _zero_at_first([s_scr], pl.program_id(1) == 0)
        flat = []
        for p in range(_SP):
            flat += [*_ssd_args(p, x_ref, dt_ref, a_ref, ar_ref), s_scr[p]]
        sin_ref[...] = s_scr[...]
        outs = _ssd_multi(*flat)
        for p in range(_SP):
            y_ref[:, p * LANE:(p + 1) * LANE], s_scr[p] = outs[2 * p:2 * p + 2]

    tspec = pl.BlockSpec((CHUNK, SSD_D), lambda b, c: (b * n + c, 0))
    return _pcall("ssd_fwd", body, (bsz, n),
                  [pl.BlockSpec((CHUNK, SSD_CONV_CH), lambda b, c: (b * n + c, 0)), tspec, tspec,
                   pl.BlockSpec((_SP, 1, LANE), lambda b, c: (b * n + c, 0, 0))],
                  [tspec, pl.BlockSpec((_SP, LANE, LANE), lambda b, c: (b * n + c, 0, 0))],
                  [_sds((bsz * n * CHUNK, SSD_D)), _sds((bsz * n * _SP, LANE, LANE))],
                  scratch=[pltpu.VMEM((_SP, LANE, LANE), F32)], sem=("parallel", "arbitrary"))(xbc, dt, acs, acs_row)


def _ssd_bwd(xbc, dt, acs, acs_row, s_in, d_y, d_x_skip, bsz, n):
    def body(x_ref, dt_ref, a_ref, ar_ref, sin_ref, dy_ref, dsk_ref, dx_ref, ddt_ref, da_ref, dar_ref, ds_scr):
        _zero_at_first([ds_scr], pl.program_id(1) == 0)
        d_b = [None] * SSD_GROUPS
        d_c = [None] * SSD_GROUPS
        flat, cots = [], []
        for p in range(_SP):
            flat += [*_ssd_args(p, x_ref, dt_ref, a_ref, ar_ref), sin_ref[p]]
            cots += [dy_ref[:, p * LANE:(p + 1) * LANE], ds_scr[p]]
        _, vjp = jax.vjp(_ssd_multi, *flat)
        grads = vjp(tuple(cots))
        for p in range(_SP):
            lp = slice(p * LANE, (p + 1) * LANE)
            gi = p // _PPG
            cts = grads[_SSD_NARGS * p:_SSD_NARGS * (p + 1)]
            dx_ref[:, lp] = cts[0] + dsk_ref[:, lp]
            ddt_ref[:, lp] = cts[1]
            da_ref[:, lp] = cts[2]
            dar_ref[p] = cts[3]
            da_ref[_LAST, lp] += cts[4]
            d_b[gi] = cts[5] if d_b[gi] is None else d_b[gi] + cts[5]
            d_c[gi] = cts[6] if d_c[gi] is None else d_c[gi] + cts[6]
            ds_scr[p] = cts[7]
        for gi in range(SSD_GROUPS):
            dx_ref[:, (_B_OFF + gi) * LANE:(_B_OFF + gi + 1) * LANE] = d_b[gi]
            dx_ref[:, (_C_OFF + gi) * LANE:(_C_OFF + gi + 1) * LANE] = d_c[gi]

    def rev(b, c):
        return b * n + (n - 1 - c)

    tspec = pl.BlockSpec((CHUNK, SSD_D), lambda b, c: (rev(b, c), 0))
    xspec = pl.BlockSpec((CHUNK, SSD_CONV_CH), lambda b, c: (rev(b, c), 0))
    rspec = pl.BlockSpec((_SP, 1, LANE), lambda b, c: (rev(b, c), 0, 0))
    tok_shape = _sds((bsz * n * CHUNK, SSD_D))
    return _pcall("ssd_bwd", body, (bsz, n),
                  [xspec, tspec, tspec, rspec, pl.BlockSpec((_SP, LANE, LANE), lambda b, c: (rev(b, c), 0, 0)), tspec, tspec],
                  [xspec, tspec, tspec, rspec],
                  [_sds((bsz * n * CHUNK, SSD_CONV_CH)), tok_shape, tok_shape, _sds((bsz * n * _SP, 1, LANE))],
                  scratch=[pltpu.VMEM((_SP, LANE, LANE), F32)], sem=("parallel", "arbitrary"))(
                      xbc, dt, acs, acs_row, s_in, d_y, d_x_skip)


def _add2(name, a, b):
    t, c = a.shape
    tm = _row_tile(t)

    def body(a_ref, b_ref, o_ref):
        o_ref[...] = a_ref[...] + b_ref[...]

    return _pcall(name, body, (t // tm,), [_row_spec(tm, c)] * 2, _row_spec(tm, c), _sds((t, c)), sem=("parallel",))(a, b)


def _rep(p, width):
    return jnp.repeat(p.reshape(-1), width).reshape(1, -1)


def _to_rows(narrow, lo, hi, bsz, n):
    heads = hi - lo
    a = narrow[:, lo:hi].reshape(bsz, n, CHUNK, heads)
    return jnp.transpose(a, (0, 1, 3, 2)).reshape(bsz * n * (heads // 2), 1, 2 * CHUNK)


def _from_rows(rows, heads, bsz, n):
    return jnp.transpose(rows.reshape(bsz, n, heads, CHUNK), (0, 1, 3, 2)).reshape(bsz * n * CHUNK, heads)


def _narrow_row(gdn_part, ssd_part):
    return jnp.pad(jnp.concatenate([gdn_part, ssd_part], axis=1), ((0, 0), (_G_LO, SMALL - _S_HI)))


def _local_step(x, tgt, p):
    bsz, s, _ = x.shape
    t = bsz * s
    n = s // CHUNK
    x2 = x.reshape(t, D_MODEL)
    tgt2 = tgt.reshape(t, D_MODEL)
    w_in = p["w_in"]
    w_big = jnp.concatenate([w_in[:, :4096], w_in[:, 4112:6672]], axis=1)
    w_small = jnp.concatenate([w_in[:, 4096:4112], w_in[:, 6672:6688], jnp.zeros((D_MODEL, SMALL - 32), BF16)], axis=1)
    gate_bias = _narrow_row(p["gdn_dt_bias"], p["ssd_dt_bias"])
    gate_a_log = _narrow_row(p["gdn_a_log"], p["ssd_a_log"])
    d_skip = _rep(p["ssd_d"], SSD_HEADDIM)

    h = _rms_fwd("rms0_fwd", x2, p["pre_mix_norm"])
    proj = _matmul("mm_in_big", h, w_big, "nn", BF16, (1024, 3328))
    small = _matmul("mm_in_small", h, w_small, "nn", F32, (1024, 128))
    beta, gc, dt, acs, cum = _gates_fwd(small, gate_bias, gate_a_log)
    gc_row = _to_rows(cum, _G_LO, _G_HI, bsz, n)
    acs_row = _to_rows(cum, _S_LO, _S_HI, bsz, n)
    q = _gdn_conv_fwd("q", proj, p["gdn_conv_w"], bsz, s)
    k = _gdn_conv_fwd("k", proj, p["gdn_conv_w"], bsz, s)
    v = _gdn_conv_fwd("v", proj, p["gdn_conv_w"], bsz, s)
    o, gdn_s = _gdn_fwd(q, k, v, gc, beta, gc_row, bsz, n)
    ocat = _gdn_out_fwd(o, proj, p["gdn_norm_w"])
    xbc = _ssd_conv_fwd(proj, p["ssd_conv_w"], p["ssd_conv_b"], bsz, s)
    y, ssd_s = _ssd_fwd(xbc, dt, acs, acs_row, bsz, n)
    ocat = _ssd_out_fwd(y, xbc, proj, d_skip, p["ssd_norm_w"], ocat)
    mix = _matmul("mm_out", ocat, p["w_out"], "nn", F32, (1024, 1024))
    x1, h2 = _res1_fwd(x2, mix, p["post_mix_norm"], p["pre_ffn_norm"])
    u_pre = _matmul("mm_up", h2, p["w_up"], "nn", BF16, (1024, 2816))
    act = _ffn_act_fwd(u_pre, p["ffn_conv_w"], p["ffn_conv_b"], bsz, s)
    f = _matmul("mm_down", act, p["w_down"], "nn", F32, (1024, 1024))
    loss_acc, d_out, d_f, g_post_ffn = _final(x1, f, p["post_ffn_norm"], tgt2)

    grads = {"post_ffn_norm": g_post_ffn}
    d_act = _matmul("mm_down_dx", d_f, p["w_down"], "nt", F32, (1024, 2816))
    grads["w_down"] = _matmul("mm_down_dw", act, d_f, "tn", F32, (2816, 1024, 1024))
    d_gate, d_up, dwg, dwu, dbg, dbu = _ffn_act_bwd(u_pre, p["ffn_conv_w"], p["ffn_conv_b"], d_act, bsz, s)
    grads["ffn_conv_w"] = jnp.concatenate([dwg, dwu], axis=1)
    grads["ffn_conv_b"] = jnp.concatenate([dbg, dbu], axis=1)
    d_u = jnp.concatenate([d_gate, d_up], axis=1)
    d_h2 = _matmul("mm_up_dx", d_u, p["w_up"], "nt", F32, (512, 1024))
    grads["w_up"] = _matmul("mm_up_dw", h2, d_u, "tn", F32, (1024, 2816, 1024))
    d_x1, d_mix, grads["post_mix_norm"], grads["pre_ffn_norm"] = _res1_bwd(
        x2, mix, p["post_mix_norm"], p["pre_ffn_norm"], d_out, d_h2)
    d_ocat = _matmul("mm_out_dx", d_mix, p["w_out"], "nt", F32, (1024, 2048))
    grads["w_out"] = _matmul("mm_out_dw", ocat, d_mix, "tn", F32, (2048, 1024, 1024))

    d_o, d_proj, grads["gdn_norm_w"] = _gdn_out_bwd(o, proj, p["gdn_norm_w"], d_ocat)
    d_q, d_k, d_v, d_gc, d_beta, d_gc_row = _gdn_bwd(q, k, v, gc, beta, gc_row, gdn_s, d_o, bsz, n)
    d_proj, dwq = _gdn_conv_bwd("q", proj, p["gdn_conv_w"], d_q, d_proj, bsz, s)
    d_proj, dwk = _gdn_conv_bwd("k", proj, p["gdn_conv_w"], d_k, d_proj, bsz, s)
    d_proj, dwv = _gdn_conv_bwd("v", proj, p["gdn_conv_w"], d_v, d_proj, bsz, s)
    grads["gdn_conv_w"] = jnp.concatenate([dwq, dwk, dwv], axis=1)

    d_y, d_xs_skip, d_proj, d_dskip, grads["ssd_norm_w"] = _ssd_out_bwd(y, xbc, proj, d_skip, p["ssd_norm_w"], d_ocat, d_proj)
    d_xbc, d_dt, d_acs, d_acs_row = _ssd_bwd(xbc, dt, acs, acs_row, ssd_s, d_y, d_xs_skip, bsz, n)
    d_proj, grads["ssd_conv_w"], grads["ssd_conv_b"] = _ssd_conv_bwd(proj, p["ssd_conv_w"], p["ssd_conv_b"], d_xbc, d_proj, bsz, s)

    d_cum = jnp.concatenate([jnp.zeros((t, _G_LO), F32), _from_rows(d_gc_row, GDN_HEADS, bsz, n),
                             _from_rows(d_acs_row, SSD_HEADS, bsz, n), jnp.zeros((t, SMALL - _S_HI), F32)], axis=1)
    d_small, d_gate_bias, d_gate_a_log = _gates_bwd(small, gate_bias, gate_a_log, (d_beta, d_gc, d_dt, d_acs, d_cum))
    grads["gdn_dt_bias"], grads["ssd_dt_bias"] = d_gate_bias[:, _G_LO:_G_HI], d_gate_bias[:, _S_LO:_S_HI]
    grads["gdn_a_log"], grads["ssd_a_log"] = d_gate_a_log[:, _G_LO:_G_HI], d_gate_a_log[:, _S_LO:_S_HI]
    d_h_big = _matmul("mm_in_big_dx", d_proj, w_big, "nt", F32, (512, 1024))
    d_h_small = _matmul("mm_in_small_dx", d_small, w_small, "nt", F32, (1024, 1024))
    dw_big = _matmul("mm_in_big_dw", h, d_proj, "tn", F32, (1024, 3328, 1024))
    dw_small = _matmul("mm_in_small_dw", h, d_small, "tn", F32)
    grads["w_in"] = jnp.concatenate([dw_big[:, :4096], dw_small[:, :16], dw_big[:, 4096:], dw_small[:, 16:32]], axis=1)
    grad_x, grads["pre_mix_norm"] = _rms1_bwd(x2, p["pre_mix_norm"], d_h_big, d_h_small, d_x1)
    grads["ssd_d"] = _head_sums(d_dskip)[:1, :SSD_HEADS]
    return loss_acc, grad_x.reshape(bsz, s, D_MODEL), grads


def _head_sums(wide):
    def body(x_ref, o_ref):
        r, c = _iota2((D_MODEL, SMALL))
        o_ref[...] = _mask_dot(jnp.broadcast_to(x_ref[...], (8, D_MODEL)), ((r >> 6) == c).astype(F32), NN, True)

    return _pcall("head_sums", body, (1,), [_full_spec((1, D_MODEL))], _full_spec((8, SMALL)), _sds((8, SMALL)))(wide)


def _adamw_fn(w, g, m, v):
    m = ADAM_B1 * m + (1.0 - ADAM_B1) * g
    v = ADAM_B2 * v + (1.0 - ADAM_B2) * (g * g)
    m_hat = m / (1.0 - ADAM_B1 ** ADAM_STEP)
    v_hat = v / (1.0 - ADAM_B2 ** ADAM_STEP)
    delta = -ADAM_LR * (m_hat / (jnp.sqrt(v_hat) + ADAM_EPS) + ADAM_WD * w)
    return delta, m, v


def _adamw(name, w, g, m, v):
    r, c = w.shape
    tr = _pick(r, (256, 176, 128, 64, 8))

    def body(w_ref, g_ref, m_ref, v_ref, d_ref, m2_ref, v2_ref):
        d, m2, v2 = _adamw_fn(w_ref[...], g_ref[...], m_ref[...], v_ref[...])
        d_ref[...] = d
        m2_ref[...] = m2
        v2_ref[...] = v2

    spec = pl.BlockSpec((tr, c), lambda i: (i, 0))
    return _pcall(name, body, (r // tr,), [spec] * 4, [spec] * 3, [_sds((r, c))] * 3, sem=("parallel",))(w, g, m, v)


_ANY = pl.BlockSpec(memory_space=pl.ANY)
_OTHER_CHIPS = ((1, 0), (0, 1), (1, 1))


def _coords():
    return lax.axis_index("x"), lax.axis_index("y"), lax.axis_index("c")


def _flip(v, f):
    return 1 - v if f else v


def _gather_chips(arrs, split):
    n = len(arrs)

    def body(*refs):
        ins, outs = refs[:n], refs[n:2 * n]
        send_sems, recv_sems, fwd_send_sems, fwd_recv_sems, own_send_sems, own_recv_sems = refs[2 * n:]
        x, y, c = _coords()
        me = 2 * x + y
        sib = (x, y, 1 - c)

        def rows(a, core):
            if not split[a]:
                return slice(None)
            half = arrs[a].shape[0] // 2
            return pl.ds(core * half, half)

        sends = []
        own = []
        for a in range(n):
            cp = pltpu.make_async_remote_copy(ins[a], outs[a].at[me], own_send_sems.at[a], own_recv_sems.at[a],
                                              device_id=sib, device_id_type=MESH)
            cp.start()
            own.append(cp)
        for a in range(n):
            for j, (fx, fy) in enumerate(_OTHER_CHIPS):
                cp = pltpu.make_async_remote_copy(ins[a].at[rows(a, c)], outs[a].at[me, rows(a, c)],
                                                  send_sems.at[a * 3 + j], recv_sems.at[a * 3 + j],
                                                  device_id=(_flip(x, fx), _flip(y, fy), c), device_id_type=MESH)
                cp.start()
                sends.append(cp)
        for a in range(n):
            for j, (fx, fy) in enumerate(_OTHER_CHIPS):
                src = 2 * _flip(x, fx) + _flip(y, fy)
                landed = outs[a].at[src, rows(a, c)]
                pltpu.make_async_remote_copy(landed, landed, send_sems.at[a * 3 + j], recv_sems.at[a * 3 + j],
                                             device_id=(_flip(x, fx), _flip(y, fy), c), device_id_type=MESH).wait_recv()
                if split[a]:
                    fw = pltpu.make_async_remote_copy(landed, landed, fwd_send_sems.at[a * 3 + j], fwd_recv_sems.at[a * 3 + j],
                                                      device_id=sib, device_id_type=MESH)
                    fw.start()
                    sends.append(fw)
        for a in range(n):
            if split[a]:
                for j, (fx, fy) in enumerate(_OTHER_CHIPS):
                    src = 2 * _flip(x, fx) + _flip(y, fy)
                    theirs = outs[a].at[src, rows(a, 1 - c)]
                    pltpu.make_async_remote_copy(theirs, theirs, fwd_send_sems.at[a * 3 + j], fwd_recv_sems.at[a * 3 + j],
                                                 device_id=sib, device_id_type=MESH).wait_recv()
        for cp in own:
            cp.wait_recv()
        for cp in sends + own:
            cp.wait_send()

    return pl.pallas_call(
        body, name="gather_chips", out_shape=[_sds((N_CHIPS,) + a.shape, a.dtype) for a in arrs],
        in_specs=[_ANY] * n, out_specs=[_ANY] * n,
        scratch_shapes=[pltpu.SemaphoreType.DMA((3 * n,))] * 4 + [pltpu.SemaphoreType.DMA((n,))] * 2,
        compiler_params=pltpu.CompilerParams(has_side_effects=True))(*arrs)


_PEERS = tuple((fx, fy, fc) for fx in (0, 1) for fy in (0, 1) for fc in (0, 1))[1:]


def _allreduce_small(x):
    r = x.shape[0]

    def body(x_ref, o_ref, buf, send_sems, recv_sems):
        cx, cy, cc = _coords()
        me = 4 * cx + 2 * cy + cc
        sends = []
        for j, (fx, fy, fc) in enumerate(_PEERS):
            cp = pltpu.make_async_remote_copy(x_ref, buf.at[me], send_sems.at[j], recv_sems.at[j],
                                              device_id=(_flip(cx, fx), _flip(cy, fy), _flip(cc, fc)), device_id_type=MESH)
            cp.start()
            sends.append(cp)
        buf[pl.ds(me, 1)] = x_ref[...][None]
        for j, (fx, fy, fc) in enumerate(_PEERS):
            src = 4 * _flip(cx, fx) + 2 * _flip(cy, fy) + _flip(cc, fc)
            pltpu.make_async_remote_copy(x_ref, buf.at[src], send_sems.at[j], recv_sems.at[j],
                                         device_id=(_flip(cx, fx), _flip(cy, fy), _flip(cc, fc)), device_id_type=MESH).wait_recv()
        for cp in sends:
            cp.wait_send()
        acc = buf[0]
        for d in range(1, N_DEV):
            acc = acc + buf[d]
        o_ref[...] = acc

    vm = pl.BlockSpec(memory_space=pltpu.VMEM)
    return pl.pallas_call(
        body, name="allreduce_small", out_shape=_sds((r, LANE)), in_specs=[vm], out_specs=vm,
        scratch_shapes=[pltpu.VMEM((N_DEV, r, LANE), F32), pltpu.SemaphoreType.DMA((7,)), pltpu.SemaphoreType.DMA((7,))],
        compiler_params=pltpu.CompilerParams(has_side_effects=True, vmem_limit_bytes=VMEM_LIMIT))(x)


def _pair_send_other_half(arrs):
    n = len(arrs)

    def body(*refs):
        ins, outs = refs[:n], refs[n:2 * n]
        send_sems, recv_sems = refs[2 * n:]
        x, y, c = _coords()
        sends = []
        for a in range(n):
            half = ins[a].shape[1] // 2
            cp = pltpu.make_async_remote_copy(ins[a].at[:, pl.ds((1 - c) * half, half), :], outs[a], send_sems.at[a], recv_sems.at[a],
                                              device_id=(x, y, 1 - c), device_id_type=MESH)
            cp.start()
            sends.append(cp)
        for cp in sends:
            cp.wait_recv()
        for cp in sends:
            cp.wait_send()

    return pl.pallas_call(
        body, name="pair_reduce_send", out_shape=[_sds((a.shape[0], a.shape[1] // 2, a.shape[2]), a.dtype) for a in arrs],
        in_specs=[_ANY] * n, out_specs=[_ANY] * n, scratch_shapes=[pltpu.SemaphoreType.DMA((n,))] * 2,
        compiler_params=pltpu.CompilerParams(has_side_effects=True))(*arrs)


def _pair_fill(arrs):
    n = len(arrs)

    def body(*refs):
        bufs = refs[n:2 * n]
        send_sems, recv_sems = refs[2 * n:]
        x, y, c = _coords()
        sends = []
        for a in range(n):
            cp = pltpu.make_async_remote_copy(bufs[a].at[c], bufs[a].at[c], send_sems.at[a], recv_sems.at[a],
                                              device_id=(x, y, 1 - c), device_id_type=MESH)
            cp.start()
            sends.append(cp)
        for a in range(n):
            theirs = bufs[a].at[1 - c]
            pltpu.make_async_remote_copy(theirs, theirs, send_sems.at[a], recv_sems.at[a],
                                         device_id=(x, y, 1 - c), device_id_type=MESH).wait_recv()
        for cp in sends:
            cp.wait_send()

    return pl.pallas_call(
        body, name="pair_gather", out_shape=[_sds(a.shape, a.dtype) for a in arrs], in_specs=[_ANY] * n, out_specs=[_ANY] * n,
        scratch_shapes=[pltpu.SemaphoreType.DMA((n,))] * 2, input_output_aliases={a: a for a in range(n)},
        compiler_params=pltpu.CompilerParams(has_side_effects=True))(*arrs)


def _scatter_chips(arrs):
    n = len(arrs)

    def body(*refs):
        ins, outs = refs[:n], refs[n:2 * n]
        send_sems, recv_sems = refs[2 * n:]
        x, y, c = _coords()
        sends = []
        for a in range(n):
            for j, (fx, fy) in enumerate(_OTHER_CHIPS):
                to = 2 * _flip(x, fx) + _flip(y, fy)
                cp = pltpu.make_async_remote_copy(ins[a].at[to], outs[a].at[j], send_sems.at[a * 3 + j], recv_sems.at[a * 3 + j],
                                                  device_id=(_flip(x, fx), _flip(y, fy), c), device_id_type=MESH)
                cp.start()
                sends.append(cp)
        for cp in sends:
            cp.wait_recv()
        for cp in sends:
            cp.wait_send()

    return pl.pallas_call(
        body, name="scatter_chips", out_shape=[_sds((3,) + a.shape[1:], a.dtype) for a in arrs],
        in_specs=[_ANY] * n, out_specs=[_ANY] * n, scratch_shapes=[pltpu.SemaphoreType.DMA((3 * n,))] * 2,
        compiler_params=pltpu.CompilerParams(has_side_effects=True))(*arrs)


def _pair_add(name, full, recv, core):
    _, r, c = full.shape
    half = r // 2
    tr = _pick(half, (256, 176, 128, 64, 8))
    nb = half // tr

    def body(c_ref, a_ref, b_ref, o_ref, ob_ref):
        s = a_ref[...] + b_ref[...]
        o_ref[...] = s
        ob_ref[...] = s.astype(BF16)

    blk = pl.BlockSpec((1, tr, c), lambda k, i, cref: (k, i, 0))
    grid_spec = pltpu.PrefetchScalarGridSpec(
        num_scalar_prefetch=1, grid=(N_CHIPS, nb),
        in_specs=[pl.BlockSpec((1, tr, c), lambda k, i, cref: (k, cref[0] * nb + i, 0)), blk], out_specs=[blk, blk])
    return pl.pallas_call(
        body, name=name, out_shape=[_sds((N_CHIPS, half, c)), _sds((N_CHIPS, half, c), BF16)], grid_spec=grid_spec,
        compiler_params=pltpu.CompilerParams(dimension_semantics=("parallel", "parallel"), vmem_limit_bytes=VMEM_LIMIT))(
            core, full, recv)


def _chip_sum(name, landed, own, where):
    _, r, c = landed.shape
    tr = _pick(r, (256, 176, 128, 64, 16))

    def body(w_ref, l_ref, o_ref, s_ref):
        s_ref[0] = ((o_ref[0] + l_ref[0].astype(F32)) + l_ref[1].astype(F32)) + l_ref[2].astype(F32)

    grid_spec = pltpu.PrefetchScalarGridSpec(
        num_scalar_prefetch=1, grid=(r // tr,),
        in_specs=[pl.BlockSpec((3, tr, c), lambda i, wref: (0, i, 0)),
                  pl.BlockSpec((1, tr, c), lambda i, wref: (wref[0], i, 0))],
        out_specs=pl.BlockSpec((1, tr, c), lambda i, wref: (wref[1], i, 0)))
    return pl.pallas_call(
        body, name=name, out_shape=_sds((2, r, c)), grid_spec=grid_spec,
        compiler_params=pltpu.CompilerParams(dimension_semantics=("parallel",), vmem_limit_bytes=VMEM_LIMIT))(where, landed, own)


_WEIGHTS = ("pre_mix_norm", "w_in", "gdn_conv_w", "gdn_a_log", "gdn_dt_bias", "gdn_norm_w", "ssd_conv_w", "ssd_conv_b",
            "ssd_a_log", "ssd_dt_bias", "ssd_d", "ssd_norm_w", "w_out", "post_mix_norm", "pre_ffn_norm", "w_up",
            "ffn_conv_w", "ffn_conv_b", "w_down", "post_ffn_norm")
_BIG = ("w_in", "w_out", "w_up", "w_down")
_COL_SHARDED_SMALL = ("gdn_conv_w", "ssd_conv_w", "ffn_conv_w")
_SMALL = tuple(k for k in _WEIGHTS if k not in _BIG)


def _pack(arrs):
    flat = jnp.concatenate([a.reshape(-1) for a in arrs])
    rows = -(-flat.shape[0] // (8 * LANE)) * 8
    return jnp.pad(flat, (0, rows * LANE - flat.shape[0])).reshape(rows, LANE)


def _unpack(packed, shapes):
    flat = packed.reshape(-1)
    out, off = [], 0
    for shp in shapes:
        size = 1
        for d in shp:
            size *= d
        out.append(flat[off:off + size].reshape(shp))
        off += size
    return out


def _cols_to_chips(a):
    r, c4 = a.shape
    return jnp.transpose(a.reshape(r, N_CHIPS, c4 // N_CHIPS), (1, 0, 2))


def _chips_to_cols(a):
    k, r, c = a.shape
    return jnp.transpose(a, (1, 0, 2)).reshape(r, k * c)


def kernel(x, pre_mix_norm, w_in, gdn_conv_w, gdn_a_log, gdn_dt_bias, gdn_norm_w, ssd_conv_w, ssd_conv_b, ssd_a_log, ssd_dt_bias, ssd_d, ssd_norm_w, w_out, post_mix_norm, pre_ffn_norm, w_up, ffn_conv_w, ffn_conv_b, w_down, post_ffn_norm, loss_target, m_pre_mix_norm, m_w_in, m_gdn_conv_w, m_gdn_a_log, m_gdn_dt_bias, m_gdn_norm_w, m_ssd_conv_w, m_ssd_conv_b, m_ssd_a_log, m_ssd_dt_bias, m_ssd_d, m_ssd_norm_w, m_w_out, m_post_mix_norm, m_pre_ffn_norm, m_w_up, m_ffn_conv_w, m_ffn_conv_b, m_w_down, m_post_ffn_norm, v_pre_mix_norm, v_w_in, v_gdn_conv_w, v_gdn_a_log, v_gdn_dt_bias, v_gdn_norm_w, v_ssd_conv_w, v_ssd_conv_b, v_ssd_a_log, v_ssd_dt_bias, v_ssd_d, v_ssd_norm_w, v_w_out, v_post_mix_norm, v_pre_ffn_norm, v_w_up, v_ffn_conv_w, v_ffn_conv_b, v_w_down, v_post_ffn_norm):
    w = dict(zip(_WEIGHTS, (pre_mix_norm, w_in, gdn_conv_w, gdn_a_log, gdn_dt_bias, gdn_norm_w, ssd_conv_w, ssd_conv_b,
                            ssd_a_log, ssd_dt_bias, ssd_d, ssd_norm_w, w_out, post_mix_norm, pre_ffn_norm, w_up,
                            ffn_conv_w, ffn_conv_b, w_down, post_ffn_norm)))
    m = dict(zip(_WEIGHTS, (m_pre_mix_norm, m_w_in, m_gdn_conv_w, m_gdn_a_log, m_gdn_dt_bias, m_gdn_norm_w, m_ssd_conv_w,
                            m_ssd_conv_b, m_ssd_a_log, m_ssd_dt_bias, m_ssd_d, m_ssd_norm_w, m_w_out, m_post_mix_norm,
                            m_pre_ffn_norm, m_w_up, m_ffn_conv_w, m_ffn_conv_b, m_w_down, m_post_ffn_norm)))
    v = dict(zip(_WEIGHTS, (v_pre_mix_norm, v_w_in, v_gdn_conv_w, v_gdn_a_log, v_gdn_dt_bias, v_gdn_norm_w, v_ssd_conv_w,
                            v_ssd_conv_b, v_ssd_a_log, v_ssd_dt_bias, v_ssd_d, v_ssd_norm_w, v_w_out, v_post_mix_norm,
                            v_pre_ffn_norm, v_w_up, v_ffn_conv_w, v_ffn_conv_b, v_w_down, v_post_ffn_norm)))
    cx, cy, cc = _coords()
    chip = 2 * cx + cy

    shards = [w[k][0].astype(BF16) for k in _BIG] + [w[k][0] for k in _COL_SHARDED_SMALL]
    g_in, g_out, g_up, g_down, g_gcw, g_scw, g_fcw = _gather_chips(shards, [True] * len(_BIG) + [False] * len(_COL_SHARDED_SMALL))
    p = {k: w[k] for k in _SMALL if k not in _COL_SHARDED_SMALL}
    p["w_in"] = _chips_to_cols(g_in)
    p["w_up"] = _chips_to_cols(g_up)
    p["w_out"] = g_out.reshape(-1, D_MODEL)
    p["w_down"] = g_down.reshape(-1, D_MODEL)
    p["gdn_conv_w"] = _chips_to_cols(g_gcw)
    p["ssd_conv_w"] = _chips_to_cols(g_scw)
    p["ffn_conv_w"] = _chips_to_cols(g_fcw)

    loss_acc, grad_x, grads = _local_step(x, loss_target, p)
    loss = lax.psum(loss_acc[0, 0], ("x", "y", "c"))

    small_full_shapes = [grads[k].shape for k in _SMALL]
    summed = _unpack(_allreduce_small(_pack([grads[k] for k in _SMALL])), small_full_shapes)
    g_small = dict(zip(_SMALL, summed))
    for k in _COL_SHARDED_SMALL:
        width = w[k].shape[2]
        g_small[k] = lax.dynamic_slice_in_dim(g_small[k], chip * width, width, axis=1)

    big = [_cols_to_chips(grads["w_in"]), grads["w_out"].reshape(N_CHIPS, -1, D_MODEL),
           _cols_to_chips(grads["w_up"]), grads["w_down"].reshape(N_CHIPS, -1, D_MODEL)]
    from_sibling = _pair_send_other_half(big)
    core = cc.astype(jnp.int32).reshape(1)
    pair_sums = [_pair_add("pair_add_" + k, a, b, core) for k, a, b in zip(_BIG, big, from_sibling)]
    landed = _scatter_chips([ps[1] for ps in pair_sums])
    where = jnp.stack([chip, cc]).astype(jnp.int32)
    mine = [_chip_sum("chip_sum_" + k, a, ps[0], where) for k, a, ps in zip(_BIG, landed, pair_sums)]
    both = _pair_fill(mine)
    g_big = {k: a.reshape(-1, a.shape[2]) for k, a in zip(_BIG, both)}

    out_g, out_d, out_m, out_v = {}, {}, {}, {}
    for k in _BIG:
        out_g[k] = g_big[k][None]
        d_, m_, v_ = _adamw("adamw_" + k, w[k][0], g_big[k], m[k][0], v[k][0])
        out_d[k], out_m[k], out_v[k] = d_[None], m_[None], v_[None]
    shapes = [w[k].shape for k in _SMALL]
    for k in _SMALL:
        out_g[k] = g_small[k].reshape(w[k].shape)
    packed = [_pack([d[k] for k in _SMALL]) for d in (w, out_g, m, v)]
    d_p, m_p, v_p = _adamw("adamw_small", *packed)
    for dst, src in ((out_d, d_p), (out_m, m_p), (out_v, v_p)):
        dst.update(zip(_SMALL, _unpack(src, shapes)))
    return (loss, grad_x, *[out_g[k] for k in _WEIGHTS], *[out_d[k] for k in _WEIGHTS],
            *[out_m[k] for k in _WEIGHTS], *[out_v[k] for k in _WEIGHTS])
```

```python
import functools

import jax
import jax.numpy as jnp
from jax import lax
from jax.experimental import pallas as pl
from jax.experimental.pallas import tpu as pltpu

F32 = jnp.float32
BF16 = jnp.bfloat16

D_MODEL = 1024
GDN_HEADS = 8
GDN_DK = 128
SSD_HEADS = 16
SSD_HEADDIM = 64
SSD_GROUPS = 2
SSD_STATE = 128
CONV_K = 4
CHUNK = 64
D_FF = 2816
FFN_CONV_K = 3
EPS = 1e-6
GDN_QK = GDN_HEADS * GDN_DK
GDN_V = GDN_QK
SSD_D = SSD_HEADS * SSD_HEADDIM
SSD_BC = SSD_GROUPS * SSD_STATE
SSD_CONV_CH = SSD_D + 2 * SSD_BC
BIG = 4 * 1024 + 1024 + SSD_CONV_CH
SMALL = 128
D_IN_PROJ = 6688
LANE = 128
PAIR = 2 * CHUNK
NEG = -1e30
VMEM_LIMIT = 56 * 1024 * 1024

ADAM_LR = 0.001
ADAM_B1 = 0.9
ADAM_B2 = 0.999
ADAM_EPS = 1e-08
ADAM_WD = 0.01
ADAM_STEP = 10

N_CHIPS = 4
N_DEV = 8
MESH = pl.DeviceIdType.MESH

NN = ((1,), (0,))
NT = ((1,), (1,))
TN = ((0,), (0,))


def _bdot(a, b, dims):
    return lax.dot_general(a.astype(BF16), b.astype(BF16), (dims, ((), ())), preferred_element_type=F32)


def _split3(a):
    hi = a.astype(BF16)
    r1 = a - hi.astype(F32)
    mid = r1.astype(BF16)
    return hi, mid, (r1 - mid.astype(F32)).astype(BF16)


@jax.custom_vjp
def _nn(a, b):
    return _bdot(a, b, NN)


@jax.custom_vjp
def _nt(a, b):
    return _bdot(a, b, NT)


@jax.custom_vjp
def _tn(a, b):
    return _bdot(a, b, TN)


_nn.defvjp(lambda a, b: (_nn(a, b), (a, b)), lambda r, g: (_nt(g, r[1]), _tn(r[0], g)))
_nt.defvjp(lambda a, b: (_nt(a, b), (a, b)), lambda r, g: (_nn(g, r[1]), _tn(g, r[0])))
_tn.defvjp(lambda a, b: (_tn(a, b), (a, b)), lambda r, g: (_nt(r[1], g), _nn(r[0], g)))


def _mask_dot(x, mask, dims, x_first):
    acc = None
    for piece in _split3(x):
        term = _bdot(piece, mask, dims) if x_first else _bdot(mask, piece, dims)
        acc = term if acc is None else acc + term
    return acc


@jax.custom_vjp
def _cst_left(cst, x):
    return _mask_dot(x, cst, NN, False)


_cst_left.defvjp(lambda cst, x: (_cst_left(cst, x), cst), lambda cst, g: (jnp.zeros_like(cst), _mask_dot(g, cst, TN, False)))


@jax.custom_vjp
def _cst_right(x, cst):
    return _mask_dot(x, cst, NN, True)


_cst_right.defvjp(lambda x, cst: (_cst_right(x, cst), cst), lambda cst, g: (_mask_dot(g, cst, NT, True), jnp.zeros_like(cst)))


def _lin_left(cst):
    return functools.partial(_cst_left, cst)


def _lin_right(cst):
    return lambda x: _cst_right(x, cst)


@jax.custom_vjp
def _tri_inv_m1(a):
    pm = [-x for x in a]
    ap = list(a)
    for _ in range(5):
        ap = [_bdot(x, x, NN) for x in ap]
        pm = [(p + x) + _bdot(p, x, NN) for p, x in zip(pm, ap)]
    return pm


def _tri_inv_m1_bwd(pm, g):
    t = [gi + _bdot(p, gi, TN) for p, gi in zip(pm, g)]
    return ([-(ti + _bdot(ti, p, NT)) for p, ti in zip(pm, t)],)


_tri_inv_m1.defvjp(lambda a: (lambda pm: (pm, pm))(_tri_inv_m1(a)), _tri_inv_m1_bwd)


@jax.custom_vjp
def _top(x):
    return x[: x.shape[0] // 2]


_top.defvjp(lambda x: (_top(x), None), lambda _, g: (jnp.concatenate([g, jnp.zeros_like(g)], axis=0),))


@jax.custom_vjp
def _bot(x):
    return x[x.shape[0] // 2:]


_bot.defvjp(lambda x: (_bot(x), None), lambda _, g: (jnp.concatenate([jnp.zeros_like(g), g], axis=0),))


@jax.custom_vjp
def _vstack(a, b):
    return jnp.concatenate([a, b], axis=0)


_vstack.defvjp(lambda a, b: (_vstack(a, b), None), lambda _, g: (g[: g.shape[0] // 2], g[g.shape[0] // 2:]))


def _shift_dn_raw(x, s):
    if s == 0:
        return x
    r = pltpu.roll(x, s, axis=0)
    ri = lax.broadcasted_iota(jnp.int32, x.shape, 0)
    return jnp.where(ri >= s, r, 0.0)


def _shift_up_raw(x, s):
    if s == 0:
        return x
    n = x.shape[0]
    r = pltpu.roll(x, n - s, axis=0)
    ri = lax.broadcasted_iota(jnp.int32, x.shape, 0)
    return jnp.where(ri < n - s, r, 0.0)


@functools.partial(jax.custom_vjp, nondiff_argnums=(1,))
def _shift_dn(x, s):
    return _shift_dn_raw(x, s)


_shift_dn.defvjp(lambda x, s: (_shift_dn_raw(x, s), None), lambda s, _, g: (_shift_up_raw(g, s),))


def _conv(x, wrows):
    k_w = len(wrows)
    acc = wrows[k_w - 1] * x
    for k in range(k_w - 1):
        acc = acc + wrows[k] * _shift_dn(x, k_w - 1 - k)
    return acc


def _silu(x):
    return x * jax.nn.sigmoid(x)


def _rms(x, w):
    return x * lax.rsqrt(jnp.mean(x * x, axis=-1, keepdims=True) + EPS) * w


def _l2n(x):
    return x * lax.rsqrt(jnp.sum(x * x, axis=-1, keepdims=True) + EPS)


def _iota2(shape):
    return lax.broadcasted_iota(jnp.int32, shape, 0), lax.broadcasted_iota(jnp.int32, shape, 1)


_GDN_NARGS = 15
_SSD_NARGS = 8


def _gdn_multi(*flat):
    pairs = [flat[i:i + _GDN_NARGS] for i in range(0, len(flat), _GDN_NARGS)]
    idx = range(len(pairs))
    ri, ci = _iota2((PAIR, PAIR))
    blk = ((ri >= CHUNK) & (ci >= CHUNK)) | ((ri < CHUNK) & (ci < CHUNK))
    causal = blk & (ri >= ci)
    strict = blk & (ri > ci)
    q = [_vstack(p[0], p[1]) for p in pairs]
    k = [_vstack(p[2], p[3]) for p in pairs]
    v = [_vstack(p[4], p[5]) for p in pairs]
    gc = [_vstack(p[6], p[7]) for p in pairs]
    beta = [_vstack(p[8], p[9]) for p in pairs]
    glast = [_vstack(jnp.broadcast_to(p[11], (CHUNK, LANE)), jnp.broadcast_to(p[12], (CHUNK, LANE))) for p in pairs]
    sa = [p[13] for p in pairs]
    sb = [p[14] for p in pairs]
    decay = [jnp.exp(jnp.where(causal, gc[i] - jnp.broadcast_to(pairs[i][10], (PAIR, PAIR)), NEG)) for i in idx]
    eg = [jnp.exp(x) for x in gc]
    kbeta = [k[i] * beta[i] for i in idx]
    pm = _tri_inv_m1([jnp.where(strict, _nt(kbeta[i], k[i]) * decay[i], 0.0) for i in idx])
    qk = [_nt(q[i], k[i]) * decay[i] for i in idx]
    rhs_v = [v[i] * beta[i] for i in idx]
    rhs_k = [kbeta[i] * eg[i] for i in idx]
    u = [rhs_v[i] + _nn(pm[i], rhs_v[i]) for i in idx]
    w = [rhs_k[i] + _nn(pm[i], rhs_k[i]) for i in idx]
    q_dec = [q[i] * eg[i] for i in idx]
    k_dec = [k[i] * jnp.exp(glast[i] - gc[i]) for i in idx]
    gl = [jnp.exp(x) for x in glast]
    w_s = [_vstack(_nn(_top(w[i]), sa[i]), _nn(_bot(w[i]), sb[i])) for i in idx]
    q_s = [_vstack(_nn(_top(q_dec[i]), sa[i]), _nn(_bot(q_dec[i]), sb[i])) for i in idx]
    v_new = [u[i] - w_s[i] for i in idx]
    o = [q_s[i] + _nn(qk[i], v_new[i]) for i in idx]
    sa2 = [sa[i] * _vstack(_top(gl[i]), _top(gl[i])) + _tn(_top(k_dec[i]), _top(v_new[i])) for i in idx]
    sb2 = [sb[i] * _vstack(_bot(gl[i]), _bot(gl[i])) + _tn(_bot(k_dec[i]), _bot(v_new[i])) for i in idx]
    out = []
    for i in idx:
        out += [_top(o[i]), _bot(o[i]), sa2[i], sb2[i]]
    return tuple(out)


def _ssd_multi(*flat):
    pairs = [flat[i:i + _SSD_NARGS] for i in range(0, len(flat), _SSD_NARGS)]
    idx = range(len(pairs))
    ri, ci = _iota2((CHUNK, PAIR))
    causal = ri >= jnp.where(ci >= CHUNK, ci - CHUNK, ci)
    xdt = [p[0] * p[1] for p in pairs]
    acs = [p[2] for p in pairs]
    alast = [jnp.broadcast_to(p[4], (CHUNK, PAIR)) for p in pairs]
    lmat = [jnp.exp(jnp.where(causal, acs[i] - jnp.broadcast_to(pairs[i][3], (CHUNK, PAIR)), NEG)) for i in idx]
    cb2 = [_nt(p[6], _vstack(p[5], p[5])) for p in pairs]
    xblk = [_vstack(jnp.where(ci < CHUNK, x, 0.0), jnp.where(ci >= CHUNK, x, 0.0)) for x in xdt]
    y_off = [_nn(pairs[i][6], pairs[i][7]) * jnp.exp(acs[i]) for i in idx]
    y = [_nn(cb2[i] * lmat[i], xblk[i]) + y_off[i] for i in idx]
    el = [jnp.exp(x) for x in alast]
    st2 = [pairs[i][7] * _vstack(el[i], el[i]) + _tn(pairs[i][5], xdt[i] * jnp.exp(alast[i] - acs[i])) for i in idx]
    out = []
    for i in idx:
        out += [y[i], st2[i]]
    return tuple(out)


def _pcall(name, body, grid, in_specs, out_specs, out_shape, scratch=(), sem=None, aliases=None):
    if sem is None:
        sem = ("arbitrary",) * len(grid)
    return pl.pallas_call(
        functools.partial(body),
        out_shape=out_shape,
        grid=grid,
        in_specs=in_specs,
        out_specs=out_specs,
        scratch_shapes=scratch,
        input_output_aliases=aliases or {},
        name=name,
        compiler_params=pltpu.CompilerParams(dimension_semantics=sem, vmem_limit_bytes=VMEM_LIMIT),
    )


def _sds(shape, dtype=F32):
    return jax.ShapeDtypeStruct(shape, dtype)


def _row_spec(tm, width, colblock=0):
    return pl.BlockSpec((tm, width), lambda i, _c=colblock: (i, _c))


def _full_spec(shape):
    nd = len(shape)
    return pl.BlockSpec(shape, lambda *_: (0,) * nd)


def _zero_at_first(refs, first):
    @pl.when(first)
    def _():
        for r in refs:
            r[...] = jnp.zeros(r.shape, r.dtype)


def _pick(n, prefs):
    for p in prefs:
        if n % p == 0:
            return p
    return n


def _matmul(name, a, b, mode, out_dtype, tiles=None, part=None):
    def want(i, dim):
        return [tiles[i]] if tiles is not None and dim % tiles[i] == 0 else []

    if mode == "tn":
        r, m = a.shape
        n = b.shape[1]
        tm = _pick(m, want(0, m) + [1024, 1408])
        tn = _pick(n, want(1, n) + [512, 256, 128])
        tk = _pick(r, want(2, r) + [1024, 512, 256, 128, 64])
        nc = _pick(tn, (512, 256, 128))
        n_total, col_off, into = part if part is not None else (n, 0, None)
        off = col_off // tn

        def body(a_ref, b_ref, *rest):
            o_ref = rest[-1]
            _zero_at_first([o_ref], pl.program_id(2) == 0)
            for c0 in range(0, tn, nc):
                o_ref[:, c0:c0 + nc] += _bdot(a_ref[...], b_ref[:, c0:c0 + nc], TN)

        in_specs = [pl.BlockSpec((tk, tm), lambda i, j, k: (k, i)), pl.BlockSpec((tk, tn), lambda i, j, k: (k, j))]
        args = (a, b) if into is None else (a, b, into)
        return _pcall(
            name, body, (m // tm, n // tn, r // tk), in_specs + ([] if into is None else [_ANY]),
            pl.BlockSpec((tm, tn), lambda i, j, k: (i, j + off)), _sds((m, n_total), out_dtype),
            sem=("parallel", "parallel", "arbitrary"), aliases=None if into is None else {2: 0})(*args)
    m, k = a.shape
    n = b.shape[1] if mode == "nn" else b.shape[0]
    tm = _pick(m, want(0, m) + ([1024, 512, 256, 128, 64] if k <= 2816 else [512, 256, 128, 64]))
    tn = _pick(n, want(1, n) + [512, 256, 128])
    dims = NN if mode == "nn" else NT

    nc = _pick(tn, (512, 256, 128))

    def body(a_ref, b_ref, o_ref):
        for c0 in range(0, tn, nc):
            b_blk = b_ref[:, c0:c0 + nc] if mode == "nn" else b_ref[c0:c0 + nc, :]
            o_ref[:, c0:c0 + nc] = _bdot(a_ref[...], b_blk, dims).astype(o_ref.dtype)

    b_spec = pl.BlockSpec((k, tn), lambda i, j: (0, j)) if mode == "nn" else pl.BlockSpec((tn, k), lambda i, j: (j, 0))
    return _pcall(
        name, body, (m // tm, n // tn), [pl.BlockSpec((tm, k), lambda i, j: (i, 0)), b_spec],
        pl.BlockSpec((tm, tn), lambda i, j: (i, j)), _sds((m, n), out_dtype), sem=("parallel", "parallel"))(a, b)


def _matmul_nt_split(name, a1, a2, b, tm_pref):
    m, kh = a1.shape
    n = b.shape[0]
    tm = _pick(m, (tm_pref, 512, 256, 128, 64))
    nc = _pick(n, (512, 256, 128))

    def body(a1_ref, a2_ref, b_ref, o_ref):
        for c0 in range(0, n, nc):
            o_ref[:, c0:c0 + nc] = (_bdot(a1_ref[...], b_ref[c0:c0 + nc, :kh], NT)
                                    + _bdot(a2_ref[...], b_ref[c0:c0 + nc, kh:], NT))

    aspec = pl.BlockSpec((tm, kh), lambda i: (i, 0))
    return _pcall(name, body, (m // tm,), [aspec, aspec, _full_spec(b.shape)], pl.BlockSpec((tm, n), lambda i: (i, 0)),
                  _sds((m, n)), sem=("parallel",))(a1, a2, b)


def _row_tile(t):
    return _pick(t, (256, 128, 64))


def _rms_fwd(name, x, g):
    t = x.shape[0]
    tm = _row_tile(t)

    def body(x_ref, g_ref, h_ref):
        h_ref[...] = _rms(x_ref[...], g_ref[...]).astype(BF16)

    return _pcall(name, body, (t // tm,), [_row_spec(tm, D_MODEL), _full_spec((1, D_MODEL))], _row_spec(tm, D_MODEL),
                  _sds((t, D_MODEL), BF16), sem=("parallel",))(x, g)


_G_LO, _G_HI = GDN_HEADS, 2 * GDN_HEADS
_S_LO, _S_HI = 2 * GDN_HEADS, 2 * GDN_HEADS + SSD_HEADS


def _gates_fn(small, bias, a_log):
    tm = small.shape[0]
    rr, cc = _iota2((tm, tm))
    in_chunk_tril = (((rr >> 6) == (cc >> 6)) & (rr >= cc)).astype(F32)
    lane = lax.broadcasted_iota(jnp.int32, small.shape, 1)
    sp = jax.nn.softplus(small + bias)
    act = jnp.where(lane < _G_LO, jax.nn.sigmoid(small), sp)
    return act, _lin_left(in_chunk_tril)(-jnp.exp(a_log) * sp)


def _expanders():
    r, c = _iota2((SMALL, D_MODEL))
    return (r == (c >> 7)).astype(F32), (r == _G_LO + (c >> 7)).astype(F32), (r == _S_LO + (c >> 6)).astype(F32)


def _gates_fwd(small, bias, a_log):
    t = small.shape[0]
    tm = _row_tile(t)

    def body(s_ref, p0, p1, act_ref, cum_ref):
        act_ref[...], cum_ref[...] = _gates_fn(s_ref[...], p0[...], p1[...])

    pspec, nspec = _full_spec((1, SMALL)), _row_spec(tm, SMALL)
    return _pcall("gates_fwd", body, (t // tm,), [nspec, pspec, pspec], [nspec, nspec], [_sds((t, SMALL))] * 2,
                  sem=("parallel",))(small, bias, a_log)


def _gates_bwd(small, bias, a_log, d_acts, d_cums):
    t = small.shape[0]
    tm = _row_tile(t)
    na, nc = len(d_acts), len(d_cums)

    def body(*refs):
        s_ref, p0, p1 = refs[:3]
        cts = refs[3:3 + na + nc]
        ds_ref, db_ref, da_ref = refs[3 + na + nc:]
        _zero_at_first([db_ref, da_ref], pl.program_id(0) == 0)
        _, vjp = jax.vjp(_gates_fn, s_ref[...], p0[...], p1[...])
        d_act = sum(c[...] for c in cts[1:na]) + cts[0][...]
        d_cum = sum(c[...] for c in cts[na + 1:]) + cts[na][...]
        d_s, d_b, d_a = vjp((d_act, d_cum))
        ds_ref[...] = d_s.astype(BF16)
        db_ref[...] += d_b
        da_ref[...] += d_a

    pspec, nspec = _full_spec((1, SMALL)), _row_spec(tm, SMALL)
    return _pcall("gates_bwd", body, (t // tm,), [nspec, pspec, pspec] + [nspec] * (na + nc), [nspec, pspec, pspec],
                  [_sds((t, SMALL), BF16), _sds((1, SMALL)), _sds((1, SMALL))])(small, bias, a_log, *d_acts, *d_cums)


def _gdn_out_fn(o, z, w):
    return _rms(o, w) * _silu(z)


def _gdn_out_fwd(o, proj, gn):
    t = o.shape[0]
    tm = _row_tile(t)

    def body(o_ref, z_ref, w_ref, y_ref):
        for h in range(GDN_HEADS):
            sl = slice(h * GDN_DK, (h + 1) * GDN_DK)
            y_ref[:, sl] = _gdn_out_fn(o_ref[:, sl], z_ref[:, sl].astype(F32), w_ref[...]).astype(BF16)

    return _pcall("gdn_out_fwd", body, (t // tm,), [_row_spec(tm, GDN_V), _row_spec(tm, GDN_V, 3), _full_spec((1, GDN_DK))],
                  _row_spec(tm, GDN_V), _sds((t, GDN_V + SSD_D), BF16), sem=("parallel",))(o, proj, gn)


def _gdn_out_bwd(o, proj, gn, d_ocat):
    t = o.shape[0]
    tm = _row_tile(t)

    def body(o_ref, z_ref, w_ref, dy_ref, do_ref, dz_ref, dw_ref):
        _zero_at_first([dw_ref], pl.program_id(0) == 0)
        for h in range(GDN_HEADS):
            sl = slice(h * GDN_DK, (h + 1) * GDN_DK)
            _, vjp = jax.vjp(_gdn_out_fn, o_ref[:, sl], z_ref[:, sl].astype(F32), w_ref[...])
            d_o, d_z, d_w = vjp(dy_ref[:, sl])
            do_ref[:, sl] = d_o
            dz_ref[:, sl] = d_z.astype(BF16)
            dw_ref[...] += d_w

    return _pcall("gdn_out_bwd", body, (t // tm,),
                  [_row_spec(tm, GDN_V), _row_spec(tm, GDN_V, 3), _full_spec((1, GDN_DK)), _row_spec(tm, GDN_V, 0)],
                  [_row_spec(tm, GDN_V), _row_spec(tm, GDN_V, 3), _full_spec((1, GDN_DK))],
                  [_sds((t, GDN_V)), _sds((t, BIG), BF16), _sds((1, GDN_DK))])(o, proj, gn, d_ocat)


def _ssd_out_fn(y, xs, z, d_skip, w):
    return _rms((y + d_skip * xs) * _silu(z), w)


_SSD_GW = SSD_D // SSD_GROUPS


def _ssd_out_fwd(y, xbc, proj, d_skip, nw, ocat):
    t = y.shape[0]
    tm = _row_tile(t)

    def body(y_ref, x_ref, z_ref, d_ref, w_ref, _, o_ref):
        for gi in range(SSD_GROUPS):
            sl = slice(gi * _SSD_GW, (gi + 1) * _SSD_GW)
            o_ref[:, sl] = _ssd_out_fn(y_ref[:, sl], x_ref[:, sl], z_ref[:, sl].astype(F32), d_ref[:, sl], w_ref[:, sl]).astype(BF16)

    pspec = _full_spec((1, SSD_D))
    return _pcall("ssd_out_fwd", body, (t // tm,),
                  [_row_spec(tm, SSD_D), _row_spec(tm, SSD_D, 0), _row_spec(tm, SSD_D, 4), pspec, pspec, _ANY],
                  _row_spec(tm, SSD_D, 1), _sds(ocat.shape, BF16), sem=("parallel",), aliases={5: 0})(
                      y, xbc, proj, d_skip, nw, ocat)


def _ssd_out_bwd(y, xbc, proj, d_skip, nw, d_ocat, d_proj):
    t = y.shape[0]
    tm = _row_tile(t)

    def body(y_ref, x_ref, z_ref, d_ref, w_ref, do_ref, _, dy_ref, dx_ref, dz_ref, dd_ref, dw_ref):
        _zero_at_first([dd_ref, dw_ref], pl.program_id(0) == 0)
        for gi in range(SSD_GROUPS):
            sl = slice(gi * _SSD_GW, (gi + 1) * _SSD_GW)
            _, vjp = jax.vjp(_ssd_out_fn, y_ref[:, sl], x_ref[:, sl], z_ref[:, sl].astype(F32), d_ref[:, sl], w_ref[:, sl])
            d_y, d_x, d_z, d_d, d_w = vjp(do_ref[:, sl])
            dy_ref[:, sl] = d_y
            dx_ref[:, sl] = d_x
            dz_ref[:, sl] = d_z.astype(BF16)
            dd_ref[:, sl] += d_d
            dw_ref[:, sl] += d_w

    pspec = _full_spec((1, SSD_D))
    row = _row_spec(tm, SSD_D)
    return _pcall("ssd_out_bwd", body, (t // tm,),
                  [row, _row_spec(tm, SSD_D, 0), _row_spec(tm, SSD_D, 4), pspec, pspec, _row_spec(tm, SSD_D, 1), _ANY],
                  [row, row, _row_spec(tm, SSD_D, 4), pspec, pspec],
                  [_sds((t, SSD_D)), _sds((t, SSD_D)), _sds((t, BIG), BF16), _sds((1, SSD_D)), _sds((1, SSD_D))],
                  aliases={6: 2})(y, xbc, proj, d_skip, nw, d_ocat, d_proj)


def _res1_fn(x, mix, g_pm, g_pf):
    x1 = x + _rms(mix, g_pm)
    return x1, _rms(x1, g_pf)


def _res1_fwd(x, mix, g_pm, g_pf):
    t = x.shape[0]
    tm = _row_tile(t)

    def body(x_ref, m_ref, a_ref, b_ref, x1_ref, h2_ref):
        x1, h2 = _res1_fn(x_ref[...], m_ref[...], a_ref[...], b_ref[...])
        x1_ref[...] = x1
        h2_ref[...] = h2.astype(BF16)

    row, pspec = _row_spec(tm, D_MODEL), _full_spec((1, D_MODEL))
    return _pcall("res1_fwd", body, (t // tm,), [row, row, pspec, pspec], [row, row],
                  [_sds((t, D_MODEL)), _sds((t, D_MODEL), BF16)], sem=("parallel",))(x, mix, g_pm, g_pf)


def _res1_bwd(x, mix, g_pm, g_pf, d_x1, d_h2):
    t = x.shape[0]
    tm = _row_tile(t)

    def body(x_ref, m_ref, a_ref, b_ref, c1_ref, c2_ref, dx_ref, dm_ref, da_ref, db_ref):
        _zero_at_first([da_ref, db_ref], pl.program_id(0) == 0)
        _, vjp = jax.vjp(_res1_fn, x_ref[...], m_ref[...], a_ref[...], b_ref[...])
        d_x, d_m, d_a, d_b = vjp((c1_ref[...], c2_ref[...]))
        dx_ref[...] = d_x
        dm_ref[...] = d_m.astype(BF16)
        da_ref[...] += d_a
        db_ref[...] += d_b

    row, pspec = _row_spec(tm, D_MODEL), _full_spec((1, D_MODEL))
    return _pcall("res1_bwd", body, (t // tm,), [row, row, pspec, pspec, row, row], [row, row, pspec, pspec],
                  [_sds((t, D_MODEL)), _sds((t, D_MODEL), BF16), _sds((1, D_MODEL)), _sds((1, D_MODEL))])(
                      x, mix, g_pm, g_pf, d_x1, d_h2)


def _final_fn(x1, f, g_po, tgt):
    err = x1 + _rms(f, g_po) - tgt
    return 0.5 * jnp.sum(jnp.mean(err * err, axis=-1))


def _final(x1, f, g_po, tgt):
    t = x1.shape[0]
    tm = _row_tile(t)

    def body(x_ref, f_ref, g_ref, t_ref, loss_ref, dx_ref, df_ref, dg_ref):
        _zero_at_first([loss_ref, dg_ref], pl.program_id(0) == 0)
        loss, (d_x, d_f, d_g) = jax.value_and_grad(_final_fn, argnums=(0, 1, 2))(x_ref[...], f_ref[...], g_ref[...], t_ref[...])
        loss_ref[...] += jnp.broadcast_to(loss, loss_ref.shape)
        dx_ref[...] = d_x
        df_ref[...] = d_f.astype(BF16)
        dg_ref[...] += d_g

    row, pspec = _row_spec(tm, D_MODEL), _full_spec((1, D_MODEL))
    return _pcall("final", body, (t // tm,), [row, row, pspec, row], [_full_spec((8, LANE)), row, row, pspec],
                  [_sds((8, LANE)), _sds((t, D_MODEL)), _sds((t, D_MODEL), BF16), _sds((1, D_MODEL))])(x1, f, g_po, tgt)


def _rms1_bwd(x, g, d_h_a, d_h_b, d_x1):
    t = x.shape[0]
    tm = _row_tile(t)

    def body(x_ref, g_ref, dha_ref, dhb_ref, dx1_ref, dx_ref, dg_ref):
        _zero_at_first([dg_ref], pl.program_id(0) == 0)
        _, vjp = jax.vjp(_rms, x_ref[...], g_ref[...])
        d_x, d_g = vjp(dha_ref[...] + dhb_ref[...])
        dx_ref[...] = d_x + dx1_ref[...]
        dg_ref[...] += d_g

    row, pspec = _row_spec(tm, D_MODEL), _full_spec((1, D_MODEL))
    return _pcall("rms1_bwd", body, (t // tm,), [row, pspec, row, row, row], [row, pspec],
                  [_sds((t, D_MODEL)), _sds((1, D_MODEL))])(x, g, d_h_a, d_h_b, d_x1)


def _qkv_fn(mode):
    def fn(x, *wrows):
        y = _silu(_conv(x, wrows))
        if mode == "q":
            return _l2n(y) * (GDN_DK ** -0.5)
        if mode == "k":
            return _l2n(y)
        return y
    return fn


def _seq_spec(s, tc, off):
    return pl.BlockSpec((s, tc), lambda j, b, _o=off: (b, _o + j))


def _par_spec(rows, tc, off):
    return pl.BlockSpec((rows, tc), lambda j, b, _o=off: (0, _o + j))


def _gdn_conv_fwd(mode, proj, w, bsz, s):
    off = {"q": 0, "k": GDN_HEADS, "v": 2 * GDN_HEADS}[mode]
    fn = _qkv_fn(mode)

    def body(x_ref, w_ref, y_ref):
        y_ref[...] = fn(x_ref[...].astype(F32), *[w_ref[k:k + 1, :] for k in range(CONV_K)])

    return _pcall("gdn_conv_fwd_" + mode, body, (GDN_HEADS, bsz),
                  [_seq_spec(s, GDN_DK, off), _par_spec(CONV_K, GDN_DK, off)], _seq_spec(s, GDN_DK, 0),
                  _sds((bsz * s, GDN_QK)), sem=("parallel", "parallel"))(proj, w)


def _gdn_conv_bwd(mode, proj, w, d_y, d_proj, bsz, s):
    off = {"q": 0, "k": GDN_HEADS, "v": 2 * GDN_HEADS}[mode]
    fn = _qkv_fn(mode)

    def body(x_ref, w_ref, dy_ref, _, dx_ref, dw_ref):
        _zero_at_first([dw_ref], pl.program_id(1) == 0)
        _, vjp = jax.vjp(fn, x_ref[...].astype(F32), *[w_ref[k:k + 1, :] for k in range(CONV_K)])
        grads = vjp(dy_ref[...])
        dx_ref[...] = grads[0].astype(BF16)
        for k in range(CONV_K):
            dw_ref[k:k + 1, :] += grads[1 + k]

    return _pcall("gdn_conv_bwd_" + mode, body, (GDN_HEADS, bsz),
                  [_seq_spec(s, GDN_DK, off), _par_spec(CONV_K, GDN_DK, off), _seq_spec(s, GDN_DK, 0), _ANY],
                  [_seq_spec(s, GDN_DK, off), _par_spec(CONV_K, GDN_DK, 0)],
                  [_sds(d_proj.shape, BF16), _sds((CONV_K, GDN_QK))], sem=("parallel", "arbitrary"), aliases={3: 0})(
                      proj, w, d_y, d_proj)


def _ssd_conv_fn(x, bias, *wrows):
    return _silu(_conv(x, wrows) + bias)


_XBC_OFF = (5 * 1024) // LANE


def _ssd_conv_fwd(proj, w, bias, bsz, s):
    nt_ = SSD_CONV_CH // LANE

    def body(x_ref, w_ref, b_ref, y_ref):
        y_ref[...] = _ssd_conv_fn(x_ref[...].astype(F32), b_ref[...], *[w_ref[k:k + 1, :] for k in range(CONV_K)])

    return _pcall("ssd_conv_fwd", body, (nt_, bsz),
                  [_seq_spec(s, LANE, _XBC_OFF), _par_spec(CONV_K, LANE, 0), _par_spec(1, LANE, 0)], _seq_spec(s, LANE, 0),
                  _sds((bsz * s, SSD_CONV_CH)), sem=("parallel", "parallel"))(proj, w, bias)


def _ssd_conv_bwd(proj, w, bias, d_y, d_proj, bsz, s):
    nt_ = SSD_CONV_CH // LANE

    def body(x_ref, w_ref, b_ref, dy_ref, _, dx_ref, dw_ref, db_ref):
        _zero_at_first([dw_ref, db_ref], pl.program_id(1) == 0)
        _, vjp = jax.vjp(_ssd_conv_fn, x_ref[...].astype(F32), b_ref[...], *[w_ref[k:k + 1, :] for k in range(CONV_K)])
        grads = vjp(dy_ref[...])
        dx_ref[...] = grads[0].astype(BF16)
        db_ref[...] += grads[1]
        for k in range(CONV_K):
            dw_ref[k:k + 1, :] += grads[2 + k]

    return _pcall("ssd_conv_bwd", body, (nt_, bsz),
                  [_seq_spec(s, LANE, _XBC_OFF), _par_spec(CONV_K, LANE, 0), _par_spec(1, LANE, 0), _seq_spec(s, LANE, 0), _ANY],
                  [_seq_spec(s, LANE, _XBC_OFF), _par_spec(CONV_K, LANE, 0), _par_spec(1, LANE, 0)],
                  [_sds(d_proj.shape, BF16), _sds((CONV_K, SSD_CONV_CH)), _sds((1, SSD_CONV_CH))],
                  sem=("parallel", "arbitrary"), aliases={4: 0})(proj, w, bias, d_y, d_proj)


_FFN_TC = 256
_FFN_NT = D_FF // _FFN_TC


def _ffn_act_fn(xg, xu, bg, bu, *wrows):
    k_w = FFN_CONV_K
    gate = _conv(xg, wrows[:k_w]) + bg
    up = _conv(xu, wrows[k_w:]) + bu
    return _silu(gate) * up


def _ffn_act_fwd(u_pre, w, bias, bsz, s):
    def body(xg_ref, xu_ref, wg_ref, wu_ref, bg_ref, bu_ref, a_ref):
        rows = [wg_ref[k:k + 1, :] for k in range(FFN_CONV_K)] + [wu_ref[k:k + 1, :] for k in range(FFN_CONV_K)]
        a_ref[...] = _ffn_act_fn(xg_ref[...].astype(F32), xu_ref[...].astype(F32), bg_ref[...], bu_ref[...], *rows).astype(BF16)

    return _pcall("ffn_act_fwd", body, (_FFN_NT, bsz),
                  [_seq_spec(s, _FFN_TC, 0), _seq_spec(s, _FFN_TC, _FFN_NT),
                   _par_spec(FFN_CONV_K, _FFN_TC, 0), _par_spec(FFN_CONV_K, _FFN_TC, _FFN_NT),
                   _par_spec(1, _FFN_TC, 0), _par_spec(1, _FFN_TC, _FFN_NT)],
                  _seq_spec(s, _FFN_TC, 0), _sds((bsz * s, D_FF), BF16), sem=("parallel", "parallel"))(
                      u_pre, u_pre, w, w, bias, bias)


def _ffn_act_bwd(u_pre, w, bias, d_a, bsz, s):
    def body(xg_ref, xu_ref, wg_ref, wu_ref, bg_ref, bu_ref, da_ref, dg_ref, du_ref, dwg_ref, dwu_ref, dbg_ref, dbu_ref):
        _zero_at_first([dwg_ref, dwu_ref, dbg_ref, dbu_ref], pl.program_id(1) == 0)
        rows = [wg_ref[k:k + 1, :] for k in range(FFN_CONV_K)] + [wu_ref[k:k + 1, :] for k in range(FFN_CONV_K)]
        _, vjp = jax.vjp(_ffn_act_fn, xg_ref[...].astype(F32), xu_ref[...].astype(F32), bg_ref[...], bu_ref[...], *rows)
        grads = vjp(da_ref[...])
        dg_ref[...] = grads[0].astype(BF16)
        du_ref[...] = grads[1].astype(BF16)
        dbg_ref[...] += grads[2]
        dbu_ref[...] += grads[3]
        for k in range(FFN_CONV_K):
            dwg_ref[k:k + 1, :] += grads[4 + k]
            dwu_ref[k:k + 1, :] += grads[4 + FFN_CONV_K + k]

    seq0, par3, par1 = _seq_spec(s, _FFN_TC, 0), _par_spec(FFN_CONV_K, _FFN_TC, 0), _par_spec(1, _FFN_TC, 0)
    return _pcall("ffn_act_bwd", body, (_FFN_NT, bsz),
                  [seq0, _seq_spec(s, _FFN_TC, _FFN_NT), par3, _par_spec(FFN_CONV_K, _FFN_TC, _FFN_NT),
                   par1, _par_spec(1, _FFN_TC, _FFN_NT), seq0],
                  [seq0, seq0, par3, par3, par1, par1],
                  [_sds((bsz * s, D_FF), BF16), _sds((bsz * s, D_FF), BF16), _sds((FFN_CONV_K, D_FF)), _sds((FFN_CONV_K, D_FF)),
                   _sds((1, D_FF)), _sds((1, D_FF))], sem=("parallel", "arbitrary"))(u_pre, u_pre, w, w, bias, bias, d_a)


_GP = GDN_HEADS // 2
_SP = SSD_HEADS // 2


def _pair_lanes(p):
    return slice(2 * p * LANE, (2 * p + 1) * LANE), slice((2 * p + 1) * LANE, (2 * p + 2) * LANE)


_LAST = slice(CHUNK - 1, CHUNK)


def _gdn_args(p, q_ref, k_ref, v_ref, g_ref, b_ref, gr_ref):
    la, lb = _pair_lanes(p)
    return (q_ref[:, la], q_ref[:, lb], k_ref[:, la], k_ref[:, lb], v_ref[:, la], v_ref[:, lb], g_ref[:, la], g_ref[:, lb],
            b_ref[:, la], b_ref[:, lb], gr_ref[p], g_ref[_LAST, la], g_ref[_LAST, lb])


def _gdn_fwd(q, k, v, act, cum, gc_row, bsz, n):
    def body(q_ref, k_ref, v_ref, act_ref, cum_ref, gr_ref, o_ref, sin_ref, s_scr, g_ref, b_ref):
        _zero_at_first([s_scr], pl.program_id(1) == 0)
        e_b, e_a, _ = _expanders()
        b_ref[...] = _mask_dot(act_ref[...], e_b, NN, True)
        g_ref[...] = _mask_dot(cum_ref[...], e_a, NN, True)
        flat = []
        for p in range(_GP):
            flat += [*_gdn_args(p, q_ref, k_ref, v_ref, g_ref, b_ref, gr_ref), s_scr[2 * p], s_scr[2 * p + 1]]
        sin_ref[...] = s_scr[...]
        outs = _gdn_multi(*flat)
        for p in range(_GP):
            la, lb = _pair_lanes(p)
            o_ref[:, la], o_ref[:, lb], s_scr[2 * p], s_scr[2 * p + 1] = outs[4 * p:4 * p + 4]

    tspec = pl.BlockSpec((CHUNK, GDN_V), lambda b, c: (b * n + c, 0))
    rspec = pl.BlockSpec((_GP, 1, LANE), lambda b, c: (b * n + c, 0, 0))
    sspec = pl.BlockSpec((GDN_HEADS, LANE, LANE), lambda b, c: (b * n + c, 0, 0))
    nspec = pl.BlockSpec((CHUNK, SMALL), lambda b, c: (b * n + c, 0))
    wide = pltpu.VMEM((CHUNK, GDN_V), F32)
    return _pcall("gdn_fwd", body, (bsz, n), [tspec] * 3 + [nspec, nspec, rspec], [tspec, sspec],
                  [_sds((bsz * n * CHUNK, GDN_V)), _sds((bsz * n * GDN_HEADS, LANE, LANE))],
                  scratch=[pltpu.VMEM((GDN_HEADS, LANE, LANE), F32), wide, wide], sem=("parallel", "arbitrary"))(
                      q, k, v, act, cum, gc_row)


def _gdn_bwd(q, k, v, act, cum, gc_row, s_in, d_o, bsz, n):
    def body(q_ref, k_ref, v_ref, act_ref, cum_ref, gr_ref, sin_ref, do_ref, dq_ref, dk_ref, dv_ref, dact_ref, dcum_ref, dgr_ref,
             ds_scr, g_ref, b_ref, dg_ref, db_ref):
        _zero_at_first([ds_scr], pl.program_id(1) == 0)
        e_b, e_a, _ = _expanders()
        b_ref[...] = _mask_dot(act_ref[...], e_b, NN, True)
        g_ref[...] = _mask_dot(cum_ref[...], e_a, NN, True)
        flat, cots = [], []
        for p in range(_GP):
            la, lb = _pair_lanes(p)
            flat += [*_gdn_args(p, q_ref, k_ref, v_ref, g_ref, b_ref, gr_ref), sin_ref[2 * p], sin_ref[2 * p + 1]]
            cots += [do_ref[:, la], do_ref[:, lb], ds_scr[2 * p], ds_scr[2 * p + 1]]
        _, vjp = jax.vjp(_gdn_multi, *flat)
        grads = vjp(tuple(cots))
        for p in range(_GP):
            la, lb = _pair_lanes(p)
            cts = grads[_GDN_NARGS * p:_GDN_NARGS * (p + 1)]
            for ref, i in ((dq_ref, 0), (dk_ref, 2), (dv_ref, 4), (dg_ref, 6), (db_ref, 8)):
                ref[:, la] = cts[i]
                ref[:, lb] = cts[i + 1]
            dgr_ref[p] = cts[10]
            dg_ref[_LAST, la] += cts[11]
            dg_ref[_LAST, lb] += cts[12]
            ds_scr[2 * p] = cts[13]
            ds_scr[2 * p + 1] = cts[14]
        dact_ref[...] = _mask_dot(db_ref[...], e_b, NT, True)
        dcum_ref[...] = _mask_dot(dg_ref[...], e_a, NT, True)

    tspec = pl.BlockSpec((CHUNK, GDN_V), lambda b, c: (b * n + (n - 1 - c), 0))
    nspec = pl.BlockSpec((CHUNK, SMALL), lambda b, c: (b * n + (n - 1 - c), 0))
    rspec = pl.BlockSpec((_GP, 1, LANE), lambda b, c: (b * n + (n - 1 - c), 0, 0))
    sspec = pl.BlockSpec((GDN_HEADS, LANE, LANE), lambda b, c: (b * n + (n - 1 - c), 0, 0))
    tok_shape, nar_shape = _sds((bsz * n * CHUNK, GDN_V)), _sds((bsz * n * CHUNK, SMALL))
    wide = pltpu.VMEM((CHUNK, GDN_V), F32)
    return _pcall("gdn_bwd", body, (bsz, n), [tspec] * 3 + [nspec, nspec, rspec, sspec, tspec], [tspec] * 3 + [nspec, nspec, rspec],
                  [tok_shape] * 3 + [nar_shape, nar_shape, _sds((bsz * n * _GP, 1, LANE))],
                  scratch=[pltpu.VMEM((GDN_HEADS, LANE, LANE), F32), wide, wide, wide, wide], sem=("parallel", "arbitrary"))(
                      q, k, v, act, cum, gc_row, s_in, d_o)


_B_OFF = SSD_D // LANE
_C_OFF = (SSD_D + SSD_BC) // LANE
_PPG = _SP // SSD_GROUPS


def _ssd_args(p, x_ref, dt_ref, a_ref, ar_ref):
    lp = slice(p * LANE, (p + 1) * LANE)
    gi = p // _PPG
    b_sl = slice((_B_OFF + gi) * LANE, (_B_OFF + gi + 1) * LANE)
    c_sl = slice((_C_OFF + gi) * LANE, (_C_OFF + gi + 1) * LANE)
    return (x_ref[:, lp], dt_ref[:, lp], a_ref[:, lp], ar_ref[p], a_ref[_LAST, lp], x_ref[:, b_sl], x_ref[:, c_sl])


def _ssd_fwd(xbc, act, cum, acs_row, bsz, n):
    def body(x_ref, act_ref, cum_ref, ar_ref, y_ref, sin_ref, s_scr, dt_ref, a_ref):
        _zero_at_first([s_scr], pl.program_id(1) == 0)
        _, _, e_dt = _expanders()
        dt_ref[...] = _mask_dot(act_ref[...], e_dt, NN, True)
        a_ref[...] = _mask_dot(cum_ref[...], e_dt, NN, True)
        flat = []
        for p in range(_SP):
            flat += [*_ssd_args(p, x_ref, dt_ref, a_ref, ar_ref), s_scr[p]]
        sin_ref[...] = s_scr[...]
        outs = _ssd_multi(*flat)
        for p in range(_SP):
            y_ref[:, p * LANE:(p + 1) * LANE], s_scr[p] = outs[2 * p:2 * p + 2]

    tspec = pl.BlockSpec((CHUNK, SSD_D), lambda b, c: (b * n + c, 0))
    nspec = pl.BlockSpec((CHUNK, SMALL), lambda b, c: (b * n + c, 0))
    wide = pltpu.VMEM((CHUNK, SSD_D), F32)
    return _pcall("ssd_fwd", body, (bsz, n),
                  [pl.BlockSpec((CHUNK, SSD_CONV_CH), lambda b, c: (b * n + c, 0)), nspec, nspec,
                   pl.BlockSpec((_SP, 1, LANE), lambda b, c: (b * n + c, 0, 0))],
                  [tspec, pl.BlockSpec((_SP, LANE, LANE), lambda b, c: (b * n + c, 0, 0))],
                  [_sds((bsz * n * CHUNK, SSD_D)), _sds((bsz * n * _SP, LANE, LANE))],
                  scratch=[pltpu.VMEM((_SP, LANE, LANE), F32), wide, wide], sem=("parallel", "arbitrary"))(xbc, act, cum, acs_row)


def _ssd_bwd(xbc, act, cum, acs_row, s_in, d_y, d_x_skip, bsz, n):
    def body(x_ref, act_ref, cum_ref, ar_ref, sin_ref, dy_ref, dsk_ref, dx_ref, dact_ref, dcum_ref, dar_ref,
             ds_scr, dt_ref, a_ref, ddt_ref, da_ref):
        _zero_at_first([ds_scr], pl.program_id(1) == 0)
        _, _, e_dt = _expanders()
        dt_ref[...] = _mask_dot(act_ref[...], e_dt, NN, True)
        a_ref[...] = _mask_dot(cum_ref[...], e_dt, NN, True)
        d_b = [None] * SSD_GROUPS
        d_c = [None] * SSD_GROUPS
        flat, cots = [], []
        for p in range(_SP):
            flat += [*_ssd_args(p, x_ref, dt_ref, a_ref, ar_ref), sin_ref[p]]
            cots += [dy_ref[:, p * LANE:(p + 1) * LANE], ds_scr[p]]
        _, vjp = jax.vjp(_ssd_multi, *flat)
        grads = vjp(tuple(cots))
        for p in range(_SP):
            lp = slice(p * LANE, (p + 1) * LANE)
            gi = p // _PPG
            cts = grads[_SSD_NARGS * p:_SSD_NARGS * (p + 1)]
            dx_ref[:, lp] = cts[0] + dsk_ref[:, lp]
            ddt_ref[:, lp] = cts[1]
            da_ref[:, lp] = cts[2]
            dar_ref[p] = cts[3]
            da_ref[_LAST, lp] += cts[4]
            d_b[gi] = cts[5] if d_b[gi] is None else d_b[gi] + cts[5]
            d_c[gi] = cts[6] if d_c[gi] is None else d_c[gi] + cts[6]
            ds_scr[p] = cts[7]
        for gi in range(SSD_GROUPS):
            dx_ref[:, (_B_OFF + gi) * LANE:(_B_OFF + gi + 1) * LANE] = d_b[gi]
            dx_ref[:, (_C_OFF + gi) * LANE:(_C_OFF + gi + 1) * LANE] = d_c[gi]
        dact_ref[...] = _mask_dot(ddt_ref[...], e_dt, NT, True)
        dcum_ref[...] = _mask_dot(da_ref[...], e_dt, NT, True)

    def rev(b, c):
        return b * n + (n - 1 - c)

    tspec = pl.BlockSpec((CHUNK, SSD_D), lambda b, c: (rev(b, c), 0))
    nspec = pl.BlockSpec((CHUNK, SMALL), lambda b, c: (rev(b, c), 0))
    xspec = pl.BlockSpec((CHUNK, SSD_CONV_CH), lambda b, c: (rev(b, c), 0))
    rspec = pl.BlockSpec((_SP, 1, LANE), lambda b, c: (rev(b, c), 0, 0))
    nar_shape = _sds((bsz * n * CHUNK, SMALL))
    wide = pltpu.VMEM((CHUNK, SSD_D), F32)
    return _pcall("ssd_bwd", body, (bsz, n),
                  [xspec, nspec, nspec, rspec, pl.BlockSpec((_SP, LANE, LANE), lambda b, c: (rev(b, c), 0, 0)), tspec, tspec],
                  [xspec, nspec, nspec, rspec],
                  [_sds((bsz * n * CHUNK, SSD_CONV_CH)), nar_shape, nar_shape, _sds((bsz * n * _SP, 1, LANE))],
                  scratch=[pltpu.VMEM((_SP, LANE, LANE), F32), wide, wide, wide, wide], sem=("parallel", "arbitrary"))(
                      xbc, act, cum, acs_row, s_in, d_y, d_x_skip)


def _add2(name, a, b):
    t, c = a.shape
    tm = _row_tile(t)

    def body(a_ref, b_ref, o_ref):
        o_ref[...] = a_ref[...] + b_ref[...]

    return _pcall(name, body, (t // tm,), [_row_spec(tm, c)] * 2, _row_spec(tm, c), _sds((t, c)), sem=("parallel",))(a, b)


def _rep(p, width):
    return jnp.repeat(p.reshape(-1), width).reshape(1, -1)


def _to_rows(narrow, lo, hi, bsz, n):
    heads = hi - lo
    a = narrow[:, lo:hi].reshape(bsz, n, CHUNK, heads)
    return jnp.transpose(a, (0, 1, 3, 2)).reshape(bsz * n * (heads // 2), 1, 2 * CHUNK)


def _from_rows(rows, heads, bsz, n):
    return jnp.transpose(rows.reshape(bsz, n, heads, CHUNK), (0, 1, 3, 2)).reshape(bsz * n * CHUNK, heads)


def _narrow_row(gdn_part, ssd_part):
    return jnp.pad(jnp.concatenate([gdn_part, ssd_part], axis=1), ((0, 0), (_G_LO, SMALL - _S_HI)))


def _local_step(x, tgt, p):
    bsz, s, _ = x.shape
    t = bsz * s
    n = s // CHUNK
    x2 = x.reshape(t, D_MODEL)
    tgt2 = tgt.reshape(t, D_MODEL)
    w_in = p["w_in"]
    w_big = jnp.concatenate([w_in[:, :4096], w_in[:, 4112:6672]], axis=1)
    w_small = jnp.concatenate([w_in[:, 4096:4112], w_in[:, 6672:6688], jnp.zeros((D_MODEL, SMALL - 32), BF16)], axis=1)
    gate_bias = _narrow_row(p["gdn_dt_bias"], p["ssd_dt_bias"])
    gate_a_log = _narrow_row(p["gdn_a_log"], p["ssd_a_log"])
    d_skip = _rep(p["ssd_d"], SSD_HEADDIM)

    h = _rms_fwd("rms0_fwd", x2, p["pre_mix_norm"])
    proj = _matmul("mm_in_big", h, w_big, "nn", BF16, (1024, 3328))
    small = _matmul("mm_in_small", h, w_small, "nn", F32, (1024, 128))
    gact, cum = _gates_fwd(small, gate_bias, gate_a_log)
    gc_row = _to_rows(cum, _G_LO, _G_HI, bsz, n)
    acs_row = _to_rows(cum, _S_LO, _S_HI, bsz, n)
    q = _gdn_conv_fwd("q", proj, p["gdn_conv_w"], bsz, s)
    k = _gdn_conv_fwd("k", proj, p["gdn_conv_w"], bsz, s)
    v = _gdn_conv_fwd("v", proj, p["gdn_conv_w"], bsz, s)
    o, gdn_s = _gdn_fwd(q, k, v, gact, cum, gc_row, bsz, n)
    ocat = _gdn_out_fwd(o, proj, p["gdn_norm_w"])
    xbc = _ssd_conv_fwd(proj, p["ssd_conv_w"], p["ssd_conv_b"], bsz, s)
    y, ssd_s = _ssd_fwd(xbc, gact, cum, acs_row, bsz, n)
    ocat = _ssd_out_fwd(y, xbc, proj, d_skip, p["ssd_norm_w"], ocat)
    mix = _matmul("mm_out", ocat, p["w_out"], "nn", F32, (1024, 1024))
    x1, h2 = _res1_fwd(x2, mix, p["post_mix_norm"], p["pre_ffn_norm"])
    u_pre = _matmul("mm_up", h2, p["w_up"], "nn", BF16, (1024, 2816))
    act = _ffn_act_fwd(u_pre, p["ffn_conv_w"], p["ffn_conv_b"], bsz, s)
    f = _matmul("mm_down", act, p["w_down"], "nn", F32, (1024, 1024))
    loss_acc, d_out, d_f, g_post_ffn = _final(x1, f, p["post_ffn_norm"], tgt2)

    grads = {"post_ffn_norm": g_post_ffn}
    d_act = _matmul("mm_down_dx", d_f, p["w_down"], "nt", F32, (1024, 2816))
    grads["w_down"] = _matmul("mm_down_dw", act, d_f, "tn", F32, (2816, 1024, 1024))
    d_gate, d_up, dwg, dwu, dbg, dbu = _ffn_act_bwd(u_pre, p["ffn_conv_w"], p["ffn_conv_b"], d_act, bsz, s)
    grads["ffn_conv_w"] = jnp.concatenate([dwg, dwu], axis=1)
    grads["ffn_conv_b"] = jnp.concatenate([dbg, dbu], axis=1)
    d_h2 = _matmul_nt_split("mm_up_dx", d_gate, d_up, p["w_up"], 512)
    dw_up = _matmul("mm_up_dw_gate", h2, d_gate, "tn", F32, (1024, 2816, 1024), part=(2 * D_FF, 0, None))
    grads["w_up"] = _matmul("mm_up_dw_up", h2, d_up, "tn", F32, (1024, 2816, 1024), part=(2 * D_FF, D_FF, dw_up))
    d_x1, d_mix, grads["post_mix_norm"], grads["pre_ffn_norm"] = _res1_bwd(
        x2, mix, p["post_mix_norm"], p["pre_ffn_norm"], d_out, d_h2)
    d_ocat = _matmul("mm_out_dx", d_mix, p["w_out"], "nt", F32, (1024, 2048))
    grads["w_out"] = _matmul("mm_out_dw", ocat, d_mix, "tn", F32, (2048, 1024, 1024))

    d_o, d_proj, grads["gdn_norm_w"] = _gdn_out_bwd(o, proj, p["gdn_norm_w"], d_ocat)
    d_q, d_k, d_v, d_act_g, d_cum_g, d_gc_row = _gdn_bwd(q, k, v, gact, cum, gc_row, gdn_s, d_o, bsz, n)
    d_proj, dwq = _gdn_conv_bwd("q", proj, p["gdn_conv_w"], d_q, d_proj, bsz, s)
    d_proj, dwk = _gdn_conv_bwd("k", proj, p["gdn_conv_w"], d_k, d_proj, bsz, s)
    d_proj, dwv = _gdn_conv_bwd("v", proj, p["gdn_conv_w"], d_v, d_proj, bsz, s)
    grads["gdn_conv_w"] = jnp.concatenate([dwq, dwk, dwv], axis=1)

    d_y, d_xs_skip, d_proj, d_dskip, grads["ssd_norm_w"] = _ssd_out_bwd(y, xbc, proj, d_skip, p["ssd_norm_w"], d_ocat, d_proj)
    d_xbc, d_act_s, d_cum_s, d_acs_row = _ssd_bwd(xbc, gact, cum, acs_row, ssd_s, d_y, d_xs_skip, bsz, n)
    d_proj, grads["ssd_conv_w"], grads["ssd_conv_b"] = _ssd_conv_bwd(proj, p["ssd_conv_w"], p["ssd_conv_b"], d_xbc, d_proj, bsz, s)

    d_cum_rows = jnp.concatenate([jnp.zeros((t, _G_LO), F32), _from_rows(d_gc_row, GDN_HEADS, bsz, n),
                                  _from_rows(d_acs_row, SSD_HEADS, bsz, n), jnp.zeros((t, SMALL - _S_HI), F32)], axis=1)
    d_small, d_gate_bias, d_gate_a_log = _gates_bwd(small, gate_bias, gate_a_log, [d_act_g, d_act_s],
                                                    [d_cum_g, d_cum_s, d_cum_rows])
    grads["gdn_dt_bias"], grads["ssd_dt_bias"] = d_gate_bias[:, _G_LO:_G_HI], d_gate_bias[:, _S_LO:_S_HI]
    grads["gdn_a_log"], grads["ssd_a_log"] = d_gate_a_log[:, _G_LO:_G_HI], d_gate_a_log[:, _S_LO:_S_HI]
    d_h_big = _matmul("mm_in_big_dx", d_proj, w_big, "nt", F32, (512, 1024))
    d_h_small = _matmul("mm_in_small_dx", d_small, w_small, "nt", F32, (1024, 1024))
    dw_big = _matmul("mm_in_big_dw", h, d_proj, "tn", F32, (1024, 3328, 1024))
    dw_small = _matmul("mm_in_small_dw", h, d_small, "tn", F32)
    grads["w_in"] = jnp.concatenate([dw_big[:, :4096], dw_small[:, :16], dw_big[:, 4096:], dw_small[:, 16:32]], axis=1)
    grad_x, grads["pre_mix_norm"] = _rms1_bwd(x2, p["pre_mix_norm"], d_h_big, d_h_small, d_x1)
    grads["ssd_d"] = _head_sums(d_dskip)[:1, :SSD_HEADS]
    return loss_acc, grad_x.reshape(bsz, s, D_MODEL), grads


def _head_sums(wide):
    def body(x_ref, o_ref):
        r, c = _iota2((D_MODEL, SMALL))
        o_ref[...] = _mask_dot(jnp.broadcast_to(x_ref[...], (8, D_MODEL)), ((r >> 6) == c).astype(F32), NN, True)

    return _pcall("head_sums", body, (1,), [_full_spec((1, D_MODEL))], _full_spec((8, SMALL)), _sds((8, SMALL)))(wide)


def _adamw_fn(w, g, m, v):
    m = ADAM_B1 * m + (1.0 - ADAM_B1) * g
    v = ADAM_B2 * v + (1.0 - ADAM_B2) * (g * g)
    m_hat = m / (1.0 - ADAM_B1 ** ADAM_STEP)
    v_hat = v / (1.0 - ADAM_B2 ** ADAM_STEP)
    delta = -ADAM_LR * (m_hat / (jnp.sqrt(v_hat) + ADAM_EPS) + ADAM_WD * w)
    return delta, m, v


def _adamw(name, w, g, m, v):
    r, c = w.shape
    tr = _pick(r, (256, 176, 128, 64, 8))

    def body(w_ref, g_ref, m_ref, v_ref, d_ref, m2_ref, v2_ref):
        d, m2, v2 = _adamw_fn(w_ref[...], g_ref[...], m_ref[...], v_ref[...])
        d_ref[...] = d
        m2_ref[...] = m2
        v2_ref[...] = v2

    spec = pl.BlockSpec((tr, c), lambda i: (i, 0))
    return _pcall(name, body, (r // tr,), [spec] * 4, [spec] * 3, [_sds((r, c))] * 3, sem=("parallel",))(w, g, m, v)


_ANY = pl.BlockSpec(memory_space=pl.ANY)
_OTHER_CHIPS = ((1, 0), (0, 1), (1, 1))


def _coords():
    return lax.axis_index("x"), lax.axis_index("y"), lax.axis_index("c")


def _flip(v, f):
    return 1 - v if f else v


def _gather_chips(arrs, split):
    n = len(arrs)

    def body(*refs):
        ins, outs = refs[:n], refs[n:2 * n]
        send_sems, recv_sems, fwd_send_sems, fwd_recv_sems, own_send_sems, own_recv_sems = refs[2 * n:]
        x, y, c = _coords()
        me = 2 * x + y
        sib = (x, y, 1 - c)

        def rows(a, core):
            if not split[a]:
                return slice(None)
            half = arrs[a].shape[0] // 2
            return pl.ds(core * half, half)

        sends = []
        own = []
        for a in range(n):
            cp = pltpu.make_async_remote_copy(ins[a], outs[a].at[me], own_send_sems.at[a], own_recv_sems.at[a],
                                              device_id=sib, device_id_type=MESH)
            cp.start()
            own.append(cp)
        for a in range(n):
            for j, (fx, fy) in enumerate(_OTHER_CHIPS):
                cp = pltpu.make_async_remote_copy(ins[a].at[rows(a, c)], outs[a].at[me, rows(a, c)],
                                                  send_sems.at[a * 3 + j], recv_sems.at[a * 3 + j],
                                                  device_id=(_flip(x, fx), _flip(y, fy), c), device_id_type=MESH)
                cp.start()
                sends.append(cp)
        for a in range(n):
            for j, (fx, fy) in enumerate(_OTHER_CHIPS):
                src = 2 * _flip(x, fx) + _flip(y, fy)
                landed = outs[a].at[src, rows(a, c)]
                pltpu.make_async_remote_copy(landed, landed, send_sems.at[a * 3 + j], recv_sems.at[a * 3 + j],
                                             device_id=(_flip(x, fx), _flip(y, fy), c), device_id_type=MESH).wait_recv()
                if split[a]:
                    fw = pltpu.make_async_remote_copy(landed, landed, fwd_send_sems.at[a * 3 + j], fwd_recv_sems.at[a * 3 + j],
                                                      device_id=sib, device_id_type=MESH)
                    fw.start()
                    sends.append(fw)
        for a in range(n):
            if split[a]:
                for j, (fx, fy) in enumerate(_OTHER_CHIPS):
                    src = 2 * _flip(x, fx) + _flip(y, fy)
                    theirs = outs[a].at[src, rows(a, 1 - c)]
                    pltpu.make_async_remote_copy(theirs, theirs, fwd_send_sems.at[a * 3 + j], fwd_recv_sems.at[a * 3 + j],
                                                 device_id=sib, device_id_type=MESH).wait_recv()
        for cp in own:
            cp.wait_recv()
        for cp in sends + own:
            cp.wait_send()

    return pl.pallas_call(
        body, name="gather_chips", out_shape=[_sds((N_CHIPS,) + a.shape, a.dtype) for a in arrs],
        in_specs=[_ANY] * n, out_specs=[_ANY] * n,
        scratch_shapes=[pltpu.SemaphoreType.DMA((3 * n,))] * 4 + [pltpu.SemaphoreType.DMA((n,))] * 2,
        compiler_params=pltpu.CompilerParams(has_side_effects=True))(*arrs)


_PEERS = tuple((fx, fy, fc) for fx in (0, 1) for fy in (0, 1) for fc in (0, 1))[1:]


def _allreduce_small(x):
    r = x.shape[0]

    def body(x_ref, o_ref, buf, send_sems, recv_sems):
        cx, cy, cc = _coords()
        me = 4 * cx + 2 * cy + cc
        sends = []
        for j, (fx, fy, fc) in enumerate(_PEERS):
            cp = pltpu.make_async_remote_copy(x_ref, buf.at[me], send_sems.at[j], recv_sems.at[j],
                                              device_id=(_flip(cx, fx), _flip(cy, fy), _flip(cc, fc)), device_id_type=MESH)
            cp.start()
            sends.append(cp)
        buf[pl.ds(me, 1)] = x_ref[...][None]
        for j, (fx, fy, fc) in enumerate(_PEERS):
            src = 4 * _flip(cx, fx) + 2 * _flip(cy, fy) + _flip(cc, fc)
            pltpu.make_async_remote_copy(x_ref, buf.at[src], send_sems.at[j], recv_sems.at[j],
                                         device_id=(_flip(cx, fx), _flip(cy, fy), _flip(cc, fc)), device_id_type=MESH).wait_recv()
        for cp in sends:
            cp.wait_send()
        acc = buf[0]
        for d in range(1, N_DEV):
            acc = acc + buf[d]
        o_ref[...] = acc

    vm = pl.BlockSpec(memory_space=pltpu.VMEM)
    return pl.pallas_call(
        body, name="allreduce_small", out_shape=_sds((r, LANE)), in_specs=[vm], out_specs=vm,
        scratch_shapes=[pltpu.VMEM((N_DEV, r, LANE), F32), pltpu.SemaphoreType.DMA((7,)), pltpu.SemaphoreType.DMA((7,))],
        compiler_params=pltpu.CompilerParams(has_side_effects=True, vmem_limit_bytes=VMEM_LIMIT))(x)


def _pair_send_other_half(arrs):
    n = len(arrs)

    def body(*refs):
        ins, outs = refs[:n], refs[n:2 * n]
        send_sems, recv_sems = refs[2 * n:]
        x, y, c = _coords()
        sends = []
        for a in range(n):
            half = ins[a].shape[1] // 2
            cp = pltpu.make_async_remote_copy(ins[a].at[:, pl.ds((1 - c) * half, half), :], outs[a], send_sems.at[a], recv_sems.at[a],
                                              device_id=(x, y, 1 - c), device_id_type=MESH)
            cp.start()
            sends.append(cp)
        for cp in sends:
            cp.wait_recv()
        for cp in sends:
            cp.wait_send()

    return pl.pallas_call(
        body, name="pair_reduce_send", out_shape=[_sds((a.shape[0], a.shape[1] // 2, a.shape[2]), a.dtype) for a in arrs],
        in_specs=[_ANY] * n, out_specs=[_ANY] * n, scratch_shapes=[pltpu.SemaphoreType.DMA((n,))] * 2,
        compiler_params=pltpu.CompilerParams(has_side_effects=True))(*arrs)


def _pair_fill(arrs):
    n = len(arrs)

    def body(*refs):
        bufs = refs[n:2 * n]
        send_sems, recv_sems = refs[2 * n:]
        x, y, c = _coords()
        sends = []
        for a in range(n):
            cp = pltpu.make_async_remote_copy(bufs[a].at[c], bufs[a].at[c], send_sems.at[a], recv_sems.at[a],
                                              device_id=(x, y, 1 - c), device_id_type=MESH)
            cp.start()
            sends.append(cp)
        for a in range(n):
            theirs = bufs[a].at[1 - c]
            pltpu.make_async_remote_copy(theirs, theirs, send_sems.at[a], recv_sems.at[a],
                                         device_id=(x, y, 1 - c), device_id_type=MESH).wait_recv()
        for cp in sends:
            cp.wait_send()

    return pl.pallas_call(
        body, name="pair_gather", out_shape=[_sds(a.shape, a.dtype) for a in arrs], in_specs=[_ANY] * n, out_specs=[_ANY] * n,
        scratch_shapes=[pltpu.SemaphoreType.DMA((n,))] * 2, input_output_aliases={a: a for a in range(n)},
        compiler_params=pltpu.CompilerParams(has_side_effects=True))(*arrs)


def _scatter_chips(arrs):
    n = len(arrs)

    def body(*refs):
        ins, outs = refs[:n], refs[n:2 * n]
        send_sems, recv_sems = refs[2 * n:]
        x, y, c = _coords()
        sends = []
        for a in range(n):
            for j, (fx, fy) in enumerate(_OTHER_CHIPS):
                to = 2 * _flip(x, fx) + _flip(y, fy)
                cp = pltpu.make_async_remote_copy(ins[a].at[to], outs[a].at[j], send_sems.at[a * 3 + j], recv_sems.at[a * 3 + j],
                                                  device_id=(_flip(x, fx), _flip(y, fy), c), device_id_type=MESH)
                cp.start()
                sends.append(cp)
        for cp in sends:
            cp.wait_recv()
        for cp in sends:
            cp.wait_send()

    return pl.pallas_call(
        body, name="scatter_chips", out_shape=[_sds((3,) + a.shape[1:], a.dtype) for a in arrs],
        in_specs=[_ANY] * n, out_specs=[_ANY] * n, scratch_shapes=[pltpu.SemaphoreType.DMA((3 * n,))] * 2,
        compiler_params=pltpu.CompilerParams(has_side_effects=True))(*arrs)


def _pair_add(name, full, recv, core):
    _, r, c = full.shape
    half = r // 2
    tr = _pick(half, (256, 176, 128, 64, 8))
    nb = half // tr

    def body(c_ref, a_ref, b_ref, o_ref, ob_ref):
        s = a_ref[...] + b_ref[...]
        o_ref[...] = s
        ob_ref[...] = s.astype(BF16)

    blk = pl.BlockSpec((1, tr, c), lambda k, i, cref: (k, i, 0))
    grid_spec = pltpu.PrefetchScalarGridSpec(
        num_scalar_prefetch=1, grid=(N_CHIPS, nb),
        in_specs=[pl.BlockSpec((1, tr, c), lambda k, i, cref: (k, cref[0] * nb + i, 0)), blk], out_specs=[blk, blk])
    return pl.pallas_call(
        body, name=name, out_shape=[_sds((N_CHIPS, half, c)), _sds((N_CHIPS, half, c), BF16)], grid_spec=grid_spec,
        compiler_params=pltpu.CompilerParams(dimension_semantics=("parallel", "parallel"), vmem_limit_bytes=VMEM_LIMIT))(
            core, full, recv)


def _chip_sum(name, landed, own, where):
    _, r, c = landed.shape
    tr = _pick(r, (256, 176, 128, 64, 16))

    def body(w_ref, l_ref, o_ref, s_ref):
        s_ref[0] = ((o_ref[0] + l_ref[0].astype(F32)) + l_ref[1].astype(F32)) + l_ref[2].astype(F32)

    grid_spec = pltpu.PrefetchScalarGridSpec(
        num_scalar_prefetch=1, grid=(r // tr,),
        in_specs=[pl.BlockSpec((3, tr, c), lambda i, wref: (0, i, 0)),
                  pl.BlockSpec((1, tr, c), lambda i, wref: (wref[0], i, 0))],
        out_specs=pl.BlockSpec((1, tr, c), lambda i, wref: (wref[1], i, 0)))
    return pl.pallas_call(
        body, name=name, out_shape=_sds((2, r, c)), grid_spec=grid_spec,
        compiler_params=pltpu.CompilerParams(dimension_semantics=("parallel",), vmem_limit_bytes=VMEM_LIMIT))(where, landed, own)


_WEIGHTS = ("pre_mix_norm", "w_in", "gdn_conv_w", "gdn_a_log", "gdn_dt_bias", "gdn_norm_w", "ssd_conv_w", "ssd_conv_b",
            "ssd_a_log", "ssd_dt_bias", "ssd_d", "ssd_norm_w", "w_out", "post_mix_norm", "pre_ffn_norm", "w_up",
            "ffn_conv_w", "ffn_conv_b", "w_down", "post_ffn_norm")
_BIG = ("w_in", "w_out", "w_up", "w_down")
_COL_SHARDED_SMALL = ("gdn_conv_w", "ssd_conv_w", "ffn_conv_w")
_SMALL = tuple(k for k in _WEIGHTS if k not in _BIG)


def _pack(arrs):
    flat = jnp.concatenate([a.reshape(-1) for a in arrs])
    rows = -(-flat.shape[0] // (8 * LANE)) * 8
    return jnp.pad(flat, (0, rows * LANE - flat.shape[0])).reshape(rows, LANE)


def _unpack(packed, shapes):
    flat = packed.reshape(-1)
    out, off = [], 0
    for shp in shapes:
        size = 1
        for d in shp:
            size *= d
        out.append(flat[off:off + size].reshape(shp))
        off += size
    return out


def _cols_to_chips(a):
    r, c4 = a.shape
    return jnp.transpose(a.reshape(r, N_CHIPS, c4 // N_CHIPS), (1, 0, 2))


def _chips_to_cols(a):
    k, r, c = a.shape
    return jnp.transpose(a, (1, 0, 2)).reshape(r, k * c)


def kernel(x, pre_mix_norm, w_in, gdn_conv_w, gdn_a_log, gdn_dt_bias, gdn_norm_w, ssd_conv_w, ssd_conv_b, ssd_a_log, ssd_dt_bias, ssd_d, ssd_norm_w, w_out, post_mix_norm, pre_ffn_norm, w_up, ffn_conv_w, ffn_conv_b, w_down, post_ffn_norm, loss_target, m_pre_mix_norm, m_w_in, m_gdn_conv_w, m_gdn_a_log, m_gdn_dt_bias, m_gdn_norm_w, m_ssd_conv_w, m_ssd_conv_b, m_ssd_a_log, m_ssd_dt_bias, m_ssd_d, m_ssd_norm_w, m_w_out, m_post_mix_norm, m_pre_ffn_norm, m_w_up, m_ffn_conv_w, m_ffn_conv_b, m_w_down, m_post_ffn_norm, v_pre_mix_norm, v_w_in, v_gdn_conv_w, v_gdn_a_log, v_gdn_dt_bias, v_gdn_norm_w, v_ssd_conv_w, v_ssd_conv_b, v_ssd_a_log, v_ssd_dt_bias, v_ssd_d, v_ssd_norm_w, v_w_out, v_post_mix_norm, v_pre_ffn_norm, v_w_up, v_ffn_conv_w, v_ffn_conv_b, v_w_down, v_post_ffn_norm):
    w = dict(zip(_WEIGHTS, (pre_mix_norm, w_in, gdn_conv_w, gdn_a_log, gdn_dt_bias, gdn_norm_w, ssd_conv_w, ssd_conv_b,
                            ssd_a_log, ssd_dt_bias, ssd_d, ssd_norm_w, w_out, post_mix_norm, pre_ffn_norm, w_up,
                            ffn_conv_w, ffn_conv_b, w_down, post_ffn_norm)))
    m = dict(zip(_WEIGHTS, (m_pre_mix_norm, m_w_in, m_gdn_conv_w, m_gdn_a_log, m_gdn_dt_bias, m_gdn_norm_w, m_ssd_conv_w,
                            m_ssd_conv_b, m_ssd_a_log, m_ssd_dt_bias, m_ssd_d, m_ssd_norm_w, m_w_out, m_post_mix_norm,
                            m_pre_ffn_norm, m_w_up, m_ffn_conv_w, m_ffn_conv_b, m_w_down, m_post_ffn_norm)))
    v = dict(zip(_WEIGHTS, (v_pre_mix_norm, v_w_in, v_gdn_conv_w, v_gdn_a_log, v_gdn_dt_bias, v_gdn_norm_w, v_ssd_conv_w,
                            v_ssd_conv_b, v_ssd_a_log, v_ssd_dt_bias, v_ssd_d, v_ssd_norm_w, v_w_out, v_post_mix_norm,
                            v_pre_ffn_norm, v_w_up, v_ffn_conv_w, v_ffn_conv_b, v_w_down, v_post_ffn_norm)))
    cx, cy, cc = _coords()
    chip = 2 * cx + cy

    shards = [w[k][0].astype(BF16) for k in _BIG] + [w[k][0] for k in _COL_SHARDED_SMALL]
    g_in, g_out, g_up, g_down, g_gcw, g_scw, g_fcw = _gather_chips(shards, [True] * len(_BIG) + [False] * len(_COL_SHARDED_SMALL))
    p = {k: w[k] for k in _SMALL if k not in _COL_SHARDED_SMALL}
    p["w_in"] = _chips_to_cols(g_in)
    p["w_up"] = _chips_to_cols(g_up)
    p["w_out"] = g_out.reshape(-1, D_MODEL)
    p["w_down"] = g_down.reshape(-1, D_MODEL)
    p["gdn_conv_w"] = _chips_to_cols(g_gcw)
    p["ssd_conv_w"] = _chips_to_cols(g_scw)
    p["ffn_conv_w"] = _chips_to_cols(g_fcw)

    loss_acc, grad_x, grads = _local_step(x, loss_target, p)
    loss = lax.psum(loss_acc[0, 0], ("x", "y", "c"))

    small_full_shapes = [grads[k].shape for k in _SMALL]
    summed = _unpack(_allreduce_small(_pack([grads[k] for k in _SMALL])), small_full_shapes)
    g_small = dict(zip(_SMALL, summed))
    for k in _COL_SHARDED_SMALL:
        width = w[k].shape[2]
        g_small[k] = lax.dynamic_slice_in_dim(g_small[k], chip * width, width, axis=1)

    big = [_cols_to_chips(grads["w_in"]), grads["w_out"].reshape(N_CHIPS, -1, D_MODEL),
           _cols_to_chips(grads["w_up"]), grads["w_down"].reshape(N_CHIPS, -1, D_MODEL)]
    from_sibling = _pair_send_other_half(big)
    core = cc.astype(jnp.int32).reshape(1)
    pair_sums = [_pair_add("pair_add_" + k, a, b, core) for k, a, b in zip(_BIG, big, from_sibling)]
    landed = _scatter_chips([ps[1] for ps in pair_sums])
    where = jnp.stack([chip, cc]).astype(jnp.int32)
    mine = [_chip_sum("chip_sum_" + k, a, ps[0], where) for k, a, ps in zip(_BIG, landed, pair_sums)]
    both = _pair_fill(mine)
    g_big = {k: a.reshape(-1, a.shape[2]) for k, a in zip(_BIG, both)}

    out_g, out_d, out_m, out_v = {}, {}, {}, {}
    for k in _BIG:
        out_g[k] = g_big[k][None]
        d_, m_, v_ = _adamw("adamw_" + k, w[k][0], g_big[k], m[k][0], v[k][0])
        out_d[k], out_m[k], out_v[k] = d_[None], m_[None], v_[None]
    shapes = [w[k].shape for k in _SMALL]
    for k in _SMALL:
        out_g[k] = g_small[k].reshape(w[k].shape)
    packed = [_pack([d[k] for k in _SMALL]) for d in (w, out_g, m, v)]
    d_p, m_p, v_p = _adamw("adamw_small", *packed)
    for dst, src in ((out_d, d_p), (out_m, m_p), (out_v, v_p)):
        dst.update(zip(_SMALL, _unpack(src, shapes)))
    return (loss, grad_x, *[out_g[k] for k in _WEIGHTS], *[out_d[k] for k in _WEIGHTS],
            *[out_m[k] for k in _WEIGHTS], *[out_v[k] for k in _WEIGHTS])
```

```python
import functools

import jax
import jax.numpy as jnp
from jax import lax
from jax.experimental import pallas as pl
from jax.experimental.pallas import tpu as pltpu

F32 = jnp.float32
BF16 = jnp.bfloat16

D_MODEL = 1024
GDN_HEADS = 8
GDN_DK = 128
SSD_HEADS = 16
SSD_HEADDIM = 64
SSD_GROUPS = 2
SSD_STATE = 128
CONV_K = 4
CHUNK = 64
D_FF = 2816
FFN_CONV_K = 3
EPS = 1e-6
GDN_QK = GDN_HEADS * GDN_DK
GDN_V = GDN_QK
SSD_D = SSD_HEADS * SSD_HEADDIM
SSD_BC = SSD_GROUPS * SSD_STATE
SSD_CONV_CH = SSD_D + 2 * SSD_BC
BIG = 4 * 1024 + 1024 + SSD_CONV_CH
SMALL = 128
D_IN_PROJ = 6688
LANE = 128
PAIR = 2 * CHUNK
NEG = -1e30
VMEM_LIMIT = 56 * 1024 * 1024

ADAM_LR = 0.001
ADAM_B1 = 0.9
ADAM_B2 = 0.999
ADAM_EPS = 1e-08
ADAM_WD = 0.01
ADAM_STEP = 10

N_CHIPS = 4
N_DEV = 8
MESH = pl.DeviceIdType.MESH

NN = ((1,), (0,))
NT = ((1,), (1,))
TN = ((0,), (0,))


def _bdot(a, b, dims):
    return lax.dot_general(a.astype(BF16), b.astype(BF16), (dims, ((), ())), preferred_element_type=F32)


def _split3(a):
    hi = a.astype(BF16)
    r1 = a - hi.astype(F32)
    mid = r1.astype(BF16)
    return hi, mid, (r1 - mid.astype(F32)).astype(BF16)


@jax.custom_vjp
def _nn(a, b):
    return _bdot(a, b, NN)


@jax.custom_vjp
def _nt(a, b):
    return _bdot(a, b, NT)


@jax.custom_vjp
def _tn(a, b):
    return _bdot(a, b, TN)


_nn.defvjp(lambda a, b: (_nn(a, b), (a, b)), lambda r, g: (_nt(g, r[1]), _tn(r[0], g)))
_nt.defvjp(lambda a, b: (_nt(a, b), (a, b)), lambda r, g: (_nn(g, r[1]), _tn(g, r[0])))
_tn.defvjp(lambda a, b: (_tn(a, b), (a, b)), lambda r, g: (_nt(r[1], g), _nn(r[0], g)))


def _mask_dot(x, mask, dims, x_first):
    acc = None
    for piece in _split3(x):
        term = _bdot(piece, mask, dims) if x_first else _bdot(mask, piece, dims)
        acc = term if acc is None else acc + term
    return acc


@jax.custom_vjp
def _cst_left(cst, x):
    return _mask_dot(x, cst, NN, False)


_cst_left.defvjp(lambda cst, x: (_cst_left(cst, x), cst), lambda cst, g: (jnp.zeros_like(cst), _mask_dot(g, cst, TN, False)))


@jax.custom_vjp
def _cst_right(x, cst):
    return _mask_dot(x, cst, NN, True)


_cst_right.defvjp(lambda x, cst: (_cst_right(x, cst), cst), lambda cst, g: (_mask_dot(g, cst, NT, True), jnp.zeros_like(cst)))


def _lin_left(cst):
    return functools.partial(_cst_left, cst)


def _lin_right(cst):
    return lambda x: _cst_right(x, cst)


@jax.custom_vjp
def _tri_inv_m1(a):
    pm = [-x for x in a]
    ap = list(a)
    for _ in range(5):
        ap = [_bdot(x, x, NN) for x in ap]
        pm = [(p + x) + _bdot(p, x, NN) for p, x in zip(pm, ap)]
    return pm


def _tri_inv_m1_bwd(pm, g):
    t = [gi + _bdot(p, gi, TN) for p, gi in zip(pm, g)]
    return ([-(ti + _bdot(ti, p, NT)) for p, ti in zip(pm, t)],)


_tri_inv_m1.defvjp(lambda a: (lambda pm: (pm, pm))(_tri_inv_m1(a)), _tri_inv_m1_bwd)


@jax.custom_vjp
def _top(x):
    return x[: x.shape[0] // 2]


_top.defvjp(lambda x: (_top(x), None), lambda _, g: (jnp.concatenate([g, jnp.zeros_like(g)], axis=0),))


@jax.custom_vjp
def _bot(x):
    return x[x.shape[0] // 2:]


_bot.defvjp(lambda x: (_bot(x), None), lambda _, g: (jnp.concatenate([jnp.zeros_like(g), g], axis=0),))


@jax.custom_vjp
def _vstack(a, b):
    return jnp.concatenate([a, b], axis=0)


_vstack.defvjp(lambda a, b: (_vstack(a, b), None), lambda _, g: (g[: g.shape[0] // 2], g[g.shape[0] // 2:]))


def _shift_dn_raw(x, s):
    if s == 0:
        return x
    r = pltpu.roll(x, s, axis=0)
    ri = lax.broadcasted_iota(jnp.int32, x.shape, 0)
    return jnp.where(ri >= s, r, 0.0)


def _shift_up_raw(x, s):
    if s == 0:
        return x
    n = x.shape[0]
    r = pltpu.roll(x, n - s, axis=0)
    ri = lax.broadcasted_iota(jnp.int32, x.shape, 0)
    return jnp.where(ri < n - s, r, 0.0)


@functools.partial(jax.custom_vjp, nondiff_argnums=(1,))
def _shift_dn(x, s):
    return _shift_dn_raw(x, s)


_shift_dn.defvjp(lambda x, s: (_shift_dn_raw(x, s), None), lambda s, _, g: (_shift_up_raw(g, s),))


def _conv(x, wrows):
    k_w = len(wrows)
    acc = wrows[k_w - 1] * x
    for k in range(k_w - 1):
        acc = acc + wrows[k] * _shift_dn(x, k_w - 1 - k)
    return acc


def _silu(x):
    return x * jax.nn.sigmoid(x)


def _rms(x, w):
    return x * lax.rsqrt(jnp.mean(x * x, axis=-1, keepdims=True) + EPS) * w


def _l2n(x):
    return x * lax.rsqrt(jnp.sum(x * x, axis=-1, keepdims=True) + EPS)


def _iota2(shape):
    return lax.broadcasted_iota(jnp.int32, shape, 0), lax.broadcasted_iota(jnp.int32, shape, 1)


_GDN_NARGS = 15
_SSD_NARGS = 8


def _gdn_multi(*flat):
    pairs = [flat[i:i + _GDN_NARGS] for i in range(0, len(flat), _GDN_NARGS)]
    idx = range(len(pairs))
    ri, ci = _iota2((PAIR, PAIR))
    blk = ((ri >= CHUNK) & (ci >= CHUNK)) | ((ri < CHUNK) & (ci < CHUNK))
    causal = blk & (ri >= ci)
    strict = blk & (ri > ci)
    q = [_vstack(p[0], p[1]) for p in pairs]
    k = [_vstack(p[2], p[3]) for p in pairs]
    v = [_vstack(p[4], p[5]) for p in pairs]
    gc = [_vstack(p[6], p[7]) for p in pairs]
    beta = [_vstack(p[8], p[9]) for p in pairs]
    glast = [_vstack(jnp.broadcast_to(p[11], (CHUNK, LANE)), jnp.broadcast_to(p[12], (CHUNK, LANE))) for p in pairs]
    sa = [p[13] for p in pairs]
    sb = [p[14] for p in pairs]
    decay = [jnp.exp(jnp.where(causal, gc[i] - jnp.broadcast_to(pairs[i][10], (PAIR, PAIR)), NEG)) for i in idx]
    eg = [jnp.exp(x) for x in gc]
    kbeta = [k[i] * beta[i] for i in idx]
    pm = _tri_inv_m1([jnp.where(strict, _nt(kbeta[i], k[i]) * decay[i], 0.0) for i in idx])
    qk = [_nt(q[i], k[i]) * decay[i] for i in idx]
    rhs_v = [v[i] * beta[i] for i in idx]
    rhs_k = [kbeta[i] * eg[i] for i in idx]
    u = [rhs_v[i] + _nn(pm[i], rhs_v[i]) for i in idx]
    w = [rhs_k[i] + _nn(pm[i], rhs_k[i]) for i in idx]
    q_dec = [q[i] * eg[i] for i in idx]
    k_dec = [k[i] * jnp.exp(glast[i] - gc[i]) for i in idx]
    gl = [jnp.exp(x) for x in glast]
    w_s = [_vstack(_nn(_top(w[i]), sa[i]), _nn(_bot(w[i]), sb[i])) for i in idx]
    q_s = [_vstack(_nn(_top(q_dec[i]), sa[i]), _nn(_bot(q_dec[i]), sb[i])) for i in idx]
    v_new = [u[i] - w_s[i] for i in idx]
    o = [q_s[i] + _nn(qk[i], v_new[i]) for i in idx]
    sa2 = [sa[i] * _vstack(_top(gl[i]), _top(gl[i])) + _tn(_top(k_dec[i]), _top(v_new[i])) for i in idx]
    sb2 = [sb[i] * _vstack(_bot(gl[i]), _bot(gl[i])) + _tn(_bot(k_dec[i]), _bot(v_new[i])) for i in idx]
    out = []
    for i in idx:
        out += [_top(o[i]), _bot(o[i]), sa2[i], sb2[i]]
    return tuple(out)


def _ssd_multi(*flat):
    pairs = [flat[i:i + _SSD_NARGS] for i in range(0, len(flat), _SSD_NARGS)]
    idx = range(len(pairs))
    ri, ci = _iota2((CHUNK, PAIR))
    causal = ri >= jnp.where(ci >= CHUNK, ci - CHUNK, ci)
    xdt = [p[0] * p[1] for p in pairs]
    acs = [p[2] for p in pairs]
    alast = [jnp.broadcast_to(p[4], (CHUNK, PAIR)) for p in pairs]
    lmat = [jnp.exp(jnp.where(causal, acs[i] - jnp.broadcast_to(pairs[i][3], (CHUNK, PAIR)), NEG)) for i in idx]
    cb2 = [_nt(p[6], _vstack(p[5], p[5])) for p in pairs]
    xblk = [_vstack(jnp.where(ci < CHUNK, x, 0.0), jnp.where(ci >= CHUNK, x, 0.0)) for x in xdt]
    y_off = [_nn(pairs[i][6], pairs[i][7]) * jnp.exp(acs[i]) for i in idx]
    y = [_nn(cb2[i] * lmat[i], xblk[i]) + y_off[i] for i in idx]
    el = [jnp.exp(x) for x in alast]
    st2 = [pairs[i][7] * _vstack(el[i], el[i]) + _tn(pairs[i][5], xdt[i] * jnp.exp(alast[i] - acs[i])) for i in idx]
    out = []
    for i in idx:
        out += [y[i], st2[i]]
    return tuple(out)


def _pcall(name, body, grid, in_specs, out_specs, out_shape, scratch=(), sem=None, aliases=None):
    if sem is None:
        sem = ("arbitrary",) * len(grid)
    return pl.pallas_call(
        functools.partial(body),
        out_shape=out_shape,
        grid=grid,
        in_specs=in_specs,
        out_specs=out_specs,
        scratch_shapes=scratch,
        input_output_aliases=aliases or {},
        name=name,
        compiler_params=pltpu.CompilerParams(dimension_semantics=sem, vmem_limit_bytes=VMEM_LIMIT),
    )


def _sds(shape, dtype=F32):
    return jax.ShapeDtypeStruct(shape, dtype)


def _row_spec(tm, width, colblock=0):
    return pl.BlockSpec((tm, width), lambda i, _c=colblock: (i, _c))


def _full_spec(shape):
    nd = len(shape)
    return pl.BlockSpec(shape, lambda *_: (0,) * nd)


def _zero_at_first(refs, first):
    @pl.when(first)
    def _():
        for r in refs:
            r[...] = jnp.zeros(r.shape, r.dtype)


def _pick(n, prefs):
    for p in prefs:
        if n % p == 0:
            return p
    return n


def _matmul(name, a, b, mode, out_dtype, tiles=None, part=None):
    def want(i, dim):
        return [tiles[i]] if tiles is not None and dim % tiles[i] == 0 else []

    if mode == "tn":
        r, m = a.shape
        n = b.shape[1]
        tm = _pick(m, want(0, m) + [1024, 1408])
        tn = _pick(n, want(1, n) + [512, 256, 128])
        tk = _pick(r, want(2, r) + [1024, 512, 256, 128, 64])
        nc = _pick(tn, (512, 256, 128))
        n_total, col_off, into = part if part is not None else (n, 0, None)
        off = col_off // tn

        def body(a_ref, b_ref, *rest):
            o_ref = rest[-1]
            _zero_at_first([o_ref], pl.program_id(2) == 0)
            for c0 in range(0, tn, nc):
                o_ref[:, c0:c0 + nc] += _bdot(a_ref[...], b_ref[:, c0:c0 + nc], TN)

        in_specs = [pl.BlockSpec((tk, tm), lambda i, j, k: (k, i)), pl.BlockSpec((tk, tn), lambda i, j, k: (k, j))]
        args = (a, b) if into is None else (a, b, into)
        return _pcall(
            name, body, (m // tm, n // tn, r // tk), in_specs + ([] if into is None else [_ANY]),
            pl.BlockSpec((tm, tn), lambda i, j, k: (i, j + off)), _sds((m, n_total), out_dtype),
            sem=("parallel", "parallel", "arbitrary"), aliases=None if into is None else {2: 0})(*args)
    m, k = a.shape
    n = b.shape[1] if mode == "nn" else b.shape[0]
    tm = _pick(m, want(0, m) + ([1024, 512, 256, 128, 64] if k <= 2816 else [512, 256, 128, 64]))
    tn = _pick(n, want(1, n) + [512, 256, 128])
    dims = NN if mode == "nn" else NT

    nc = _pick(tn, (512, 256, 128))

    def body(a_ref, b_ref, o_ref):
        for c0 in range(0, tn, nc):
            b_blk = b_ref[:, c0:c0 + nc] if mode == "nn" else b_ref[c0:c0 + nc, :]
            o_ref[:, c0:c0 + nc] = _bdot(a_ref[...], b_blk, dims).astype(o_ref.dtype)

    b_spec = pl.BlockSpec((k, tn), lambda i, j: (0, j)) if mode == "nn" else pl.BlockSpec((tn, k), lambda i, j: (j, 0))
    return _pcall(
        name, body, (m // tm, n // tn), [pl.BlockSpec((tm, k), lambda i, j: (i, 0)), b_spec],
        pl.BlockSpec((tm, tn), lambda i, j: (i, j)), _sds((m, n), out_dtype), sem=("parallel", "parallel"))(a, b)


def _matmul_nt_split(name, a1, a2, b, tm_pref):
    m, kh = a1.shape
    n = b.shape[0]
    tm = _pick(m, (tm_pref, 512, 256, 128, 64))
    nc = _pick(n, (512, 256, 128))

    def body(a1_ref, a2_ref, b_ref, o_ref):
        for c0 in range(0, n, nc):
            o_ref[:, c0:c0 + nc] = (_bdot(a1_ref[...], b_ref[c0:c0 + nc, :kh], NT)
                                    + _bdot(a2_ref[...], b_ref[c0:c0 + nc, kh:], NT))

    aspec = pl.BlockSpec((tm, kh), lambda i: (i, 0))
    return _pcall(name, body, (m // tm,), [aspec, aspec, _full_spec(b.shape)], pl.BlockSpec((tm, n), lambda i: (i, 0)),
                  _sds((m, n)), sem=("parallel",))(a1, a2, b)


def _row_tile(t):
    return _pick(t, (256, 128, 64))


def _rms_fwd(name, x, g):
    t = x.shape[0]
    tm = _row_tile(t)

    def body(x_ref, g_ref, h_ref):
        h_ref[...] = _rms(x_ref[...], g_ref[...]).astype(BF16)

    return _pcall(name, body, (t // tm,), [_row_spec(tm, D_MODEL), _full_spec((1, D_MODEL))], _row_spec(tm, D_MODEL),
                  _sds((t, D_MODEL), BF16), sem=("parallel",))(x, g)


_G_LO, _G_HI = GDN_HEADS, 2 * GDN_HEADS
_S_LO, _S_HI = 2 * GDN_HEADS, 2 * GDN_HEADS + SSD_HEADS


def _gates_fn(small, bias, a_log):
    tm = small.shape[0]
    rr, cc = _iota2((tm, tm))
    in_chunk_tril = (((rr >> 6) == (cc >> 6)) & (rr >= cc)).astype(F32)
    lane = lax.broadcasted_iota(jnp.int32, small.shape, 1)
    sp = jax.nn.softplus(small + bias)
    act = jnp.where(lane < _G_LO, jax.nn.sigmoid(small), sp)
    cum = _lin_left(in_chunk_tril)(-jnp.exp(a_log) * sp)
    r, c = _iota2((SMALL, D_MODEL))
    to_ssd_lanes = _lin_right((r == _S_LO + (c >> 6)).astype(F32))
    return act, cum, to_ssd_lanes(act), to_ssd_lanes(cum)


def _expand_lanes(src_ref, dst_ref, lo, heads, width):
    rows = src_ref.shape[0]
    for h in range(heads):
        dst_ref[:, h * width:(h + 1) * width] = jnp.broadcast_to(src_ref[:, lo + h:lo + h + 1], (rows, width))


def _reduce_lanes(wide_ref, lo, heads, width):
    rows = wide_ref.shape[0]
    lane = lax.broadcasted_iota(jnp.int32, (rows, SMALL), 1)
    acc = jnp.zeros((rows, SMALL), F32)
    for h in range(heads):
        col = jnp.sum(wide_ref[:, h * width:(h + 1) * width], axis=-1, keepdims=True)
        acc = jnp.where(lane == lo + h, jnp.broadcast_to(col, (rows, SMALL)), acc)
    return acc


def _gates_fwd(small, bias, a_log):
    t = small.shape[0]
    tm = _row_tile(t)

    def body(s_ref, p0, p1, act_ref, cum_ref, dt_ref, acs_ref):
        act_ref[...], cum_ref[...], dt_ref[...], acs_ref[...] = _gates_fn(s_ref[...], p0[...], p1[...])

    pspec, nspec, wspec = _full_spec((1, SMALL)), _row_spec(tm, SMALL), _row_spec(tm, D_MODEL)
    return _pcall("gates_fwd", body, (t // tm,), [nspec, pspec, pspec], [nspec, nspec, wspec, wspec],
                  [_sds((t, SMALL))] * 2 + [_sds((t, D_MODEL))] * 2, sem=("parallel",))(small, bias, a_log)


def _gates_bwd(small, bias, a_log, d_act, d_cums, d_dt, d_acs):
    t = small.shape[0]
    tm = _row_tile(t)
    nc = len(d_cums)

    def body(*refs):
        s_ref, p0, p1, dact_ref = refs[:4]
        dcum_refs = refs[4:4 + nc]
        ddt_ref, dacs_ref, ds_ref, db_ref, da_ref = refs[4 + nc:]
        _zero_at_first([db_ref, da_ref], pl.program_id(0) == 0)
        _, vjp = jax.vjp(_gates_fn, s_ref[...], p0[...], p1[...])
        d_cum = dcum_refs[0][...]
        for c in dcum_refs[1:]:
            d_cum = d_cum + c[...]
        d_s, d_b, d_a = vjp((dact_ref[...], d_cum, ddt_ref[...], dacs_ref[...]))
        ds_ref[...] = d_s.astype(BF16)
        db_ref[...] += d_b
        da_ref[...] += d_a

    pspec, nspec, wspec = _full_spec((1, SMALL)), _row_spec(tm, SMALL), _row_spec(tm, D_MODEL)
    return _pcall("gates_bwd", body, (t // tm,), [nspec, pspec, pspec] + [nspec] * (1 + nc) + [wspec, wspec],
                  [nspec, pspec, pspec], [_sds((t, SMALL), BF16), _sds((1, SMALL)), _sds((1, SMALL))])(
                      small, bias, a_log, d_act, *d_cums, d_dt, d_acs)


def _gdn_out_fn(o, z, w):
    return _rms(o, w) * _silu(z)


def _gdn_out_fwd(o, proj, gn):
    t = o.shape[0]
    tm = _row_tile(t)

    def body(o_ref, z_ref, w_ref, y_ref):
        for h in range(GDN_HEADS):
            sl = slice(h * GDN_DK, (h + 1) * GDN_DK)
            y_ref[:, sl] = _gdn_out_fn(o_ref[:, sl], z_ref[:, sl].astype(F32), w_ref[...]).astype(BF16)

    return _pcall("gdn_out_fwd", body, (t // tm,), [_row_spec(tm, GDN_V), _row_spec(tm, GDN_V, 3), _full_spec((1, GDN_DK))],
                  _row_spec(tm, GDN_V), _sds((t, GDN_V + SSD_D), BF16), sem=("parallel",))(o, proj, gn)


def _gdn_out_bwd(o, proj, gn, d_ocat):
    t = o.shape[0]
    tm = _row_tile(t)

    def body(o_ref, z_ref, w_ref, dy_ref, do_ref, dz_ref, dw_ref):
        _zero_at_first([dw_ref], pl.program_id(0) == 0)
        for h in range(GDN_HEADS):
            sl = slice(h * GDN_DK, (h + 1) * GDN_DK)
            _, vjp = jax.vjp(_gdn_out_fn, o_ref[:, sl], z_ref[:, sl].astype(F32), w_ref[...])
            d_o, d_z, d_w = vjp(dy_ref[:, sl])
            do_ref[:, sl] = d_o
            dz_ref[:, sl] = d_z.astype(BF16)
            dw_ref[...] += d_w

    return _pcall("gdn_out_bwd", body, (t // tm,),
                  [_row_spec(tm, GDN_V), _row_spec(tm, GDN_V, 3), _full_spec((1, GDN_DK)), _row_spec(tm, GDN_V, 0)],
                  [_row_spec(tm, GDN_V), _row_spec(tm, GDN_V, 3), _full_spec((1, GDN_DK))],
                  [_sds((t, GDN_V)), _sds((t, BIG), BF16), _sds((1, GDN_DK))])(o, proj, gn, d_ocat)


def _ssd_out_fn(y, xs, z, d_skip, w):
    return _rms((y + d_skip * xs) * _silu(z), w)


_SSD_GW = SSD_D // SSD_GROUPS


def _ssd_out_fwd(y, xbc, proj, d_skip, nw, ocat):
    t = y.shape[0]
    tm = _row_tile(t)

    def body(y_ref, x_ref, z_ref, d_ref, w_ref, _, o_ref):
        for gi in range(SSD_GROUPS):
            sl = slice(gi * _SSD_GW, (gi + 1) * _SSD_GW)
            o_ref[:, sl] = _ssd_out_fn(y_ref[:, sl], x_ref[:, sl], z_ref[:, sl].astype(F32), d_ref[:, sl], w_ref[:, sl]).astype(BF16)

    pspec = _full_spec((1, SSD_D))
    return _pcall("ssd_out_fwd", body, (t // tm,),
                  [_row_spec(tm, SSD_D), _row_spec(tm, SSD_D, 0), _row_spec(tm, SSD_D, 4), pspec, pspec, _ANY],
                  _row_spec(tm, SSD_D, 1), _sds(ocat.shape, BF16), sem=("parallel",), aliases={5: 0})(
                      y, xbc, proj, d_skip, nw, ocat)


def _ssd_out_bwd(y, xbc, proj, d_skip, nw, d_ocat, d_proj):
    t = y.shape[0]
    tm = _row_tile(t)

    def body(y_ref, x_ref, z_ref, d_ref, w_ref, do_ref, _, dy_ref, dx_ref, dz_ref, dd_ref, dw_ref):
        _zero_at_first([dd_ref, dw_ref], pl.program_id(0) == 0)
        for gi in range(SSD_GROUPS):
            sl = slice(gi * _SSD_GW, (gi + 1) * _SSD_GW)
            _, vjp = jax.vjp(_ssd_out_fn, y_ref[:, sl], x_ref[:, sl], z_ref[:, sl].astype(F32), d_ref[:, sl], w_ref[:, sl])
            d_y, d_x, d_z, d_d, d_w = vjp(do_ref[:, sl])
            dy_ref[:, sl] = d_y
            dx_ref[:, sl] = d_x
            dz_ref[:, sl] = d_z.astype(BF16)
            dd_ref[:, sl] += d_d
            dw_ref[:, sl] += d_w

    pspec = _full_spec((1, SSD_D))
    row = _row_spec(tm, SSD_D)
    return _pcall("ssd_out_bwd", body, (t // tm,),
                  [row, _row_spec(tm, SSD_D, 0), _row_spec(tm, SSD_D, 4), pspec, pspec, _row_spec(tm, SSD_D, 1), _ANY],
                  [row, row, _row_spec(tm, SSD_D, 4), pspec, pspec],
                  [_sds((t, SSD_D)), _sds((t, SSD_D)), _sds((t, BIG), BF16), _sds((1, SSD_D)), _sds((1, SSD_D))],
                  aliases={6: 2})(y, xbc, proj, d_skip, nw, d_ocat, d_proj)


def _res1_fn(x, mix, g_pm, g_pf):
    x1 = x + _rms(mix, g_pm)
    return x1, _rms(x1, g_pf)


def _res1_fwd(x, mix, g_pm, g_pf):
    t = x.shape[0]
    tm = _row_tile(t)

    def body(x_ref, m_ref, a_ref, b_ref, x1_ref, h2_ref):
        x1, h2 = _res1_fn(x_ref[...], m_ref[...], a_ref[...], b_ref[...])
        x1_ref[...] = x1
        h2_ref[...] = h2.astype(BF16)

    row, pspec = _row_spec(tm, D_MODEL), _full_spec((1, D_MODEL))
    return _pcall("res1_fwd", body, (t // tm,), [row, row, pspec, pspec], [row, row],
                  [_sds((t, D_MODEL)), _sds((t, D_MODEL), BF16)], sem=("parallel",))(x, mix, g_pm, g_pf)


def _res1_bwd(x, mix, g_pm, g_pf, d_x1, d_h2):
    t = x.shape[0]
    tm = _row_tile(t)

    def body(x_ref, m_ref, a_ref, b_ref, c1_ref, c2_ref, dx_ref, dm_ref, da_ref, db_ref):
        _zero_at_first([da_ref, db_ref], pl.program_id(0) == 0)
        _, vjp = jax.vjp(_res1_fn, x_ref[...], m_ref[...], a_ref[...], b_ref[...])
        d_x, d_m, d_a, d_b = vjp((c1_ref[...], c2_ref[...]))
        dx_ref[...] = d_x
        dm_ref[...] = d_m.astype(BF16)
        da_ref[...] += d_a
        db_ref[...] += d_b

    row, pspec = _row_spec(tm, D_MODEL), _full_spec((1, D_MODEL))
    return _pcall("res1_bwd", body, (t // tm,), [row, row, pspec, pspec, row, row], [row, row, pspec, pspec],
                  [_sds((t, D_MODEL)), _sds((t, D_MODEL), BF16), _sds((1, D_MODEL)), _sds((1, D_MODEL))])(
                      x, mix, g_pm, g_pf, d_x1, d_h2)


def _final_fn(x1, f, g_po, tgt):
    err = x1 + _rms(f, g_po) - tgt
    return 0.5 * jnp.sum(jnp.mean(err * err, axis=-1))


def _final(x1, f, g_po, tgt):
    t = x1.shape[0]
    tm = _row_tile(t)

    def body(x_ref, f_ref, g_ref, t_ref, loss_ref, dx_ref, df_ref, dg_ref):
        _zero_at_first([loss_ref, dg_ref], pl.program_id(0) == 0)
        loss, (d_x, d_f, d_g) = jax.value_and_grad(_final_fn, argnums=(0, 1, 2))(x_ref[...], f_ref[...], g_ref[...], t_ref[...])
        loss_ref[...] += jnp.broadcast_to(loss, loss_ref.shape)
        dx_ref[...] = d_x
        df_ref[...] = d_f.astype(BF16)
        dg_ref[...] += d_g

    row, pspec = _row_spec(tm, D_MODEL), _full_spec((1, D_MODEL))
    return _pcall("final", body, (t // tm,), [row, row, pspec, row], [_full_spec((8, LANE)), row, row, pspec],
                  [_sds((8, LANE)), _sds((t, D_MODEL)), _sds((t, D_MODEL), BF16), _sds((1, D_MODEL))])(x1, f, g_po, tgt)


def _rms1_bwd(x, g, d_h_a, d_h_b, d_x1):
    t = x.shape[0]
    tm = _row_tile(t)

    def body(x_ref, g_ref, dha_ref, dhb_ref, dx1_ref, dx_ref, dg_ref):
        _zero_at_first([dg_ref], pl.program_id(0) == 0)
        _, vjp = jax.vjp(_rms, x_ref[...], g_ref[...])
        d_x, d_g = vjp(dha_ref[...] + dhb_ref[...])
        dx_ref[...] = d_x + dx1_ref[...]
        dg_ref[...] += d_g

    row, pspec = _row_spec(tm, D_MODEL), _full_spec((1, D_MODEL))
    return _pcall("rms1_bwd", body, (t // tm,), [row, pspec, row, row, row], [row, pspec],
                  [_sds((t, D_MODEL)), _sds((1, D_MODEL))])(x, g, d_h_a, d_h_b, d_x1)


def _qkv_fn(mode):
    def fn(x, *wrows):
        y = _silu(_conv(x, wrows))
        if mode == "q":
            return _l2n(y) * (GDN_DK ** -0.5)
        if mode == "k":
            return _l2n(y)
        return y
    return fn


def _seq_spec(s, tc, off):
    return pl.BlockSpec((s, tc), lambda j, b, _o=off: (b, _o + j))


def _par_spec(rows, tc, off):
    return pl.BlockSpec((rows, tc), lambda j, b, _o=off: (0, _o + j))


def _gdn_conv_fwd(mode, proj, w, bsz, s):
    off = {"q": 0, "k": GDN_HEADS, "v": 2 * GDN_HEADS}[mode]
    fn = _qkv_fn(mode)

    def body(x_ref, w_ref, y_ref):
        y_ref[...] = fn(x_ref[...].astype(F32), *[w_ref[k:k + 1, :] for k in range(CONV_K)])

    return _pcall("gdn_conv_fwd_" + mode, body, (GDN_HEADS, bsz),
                  [_seq_spec(s, GDN_DK, off), _par_spec(CONV_K, GDN_DK, off)], _seq_spec(s, GDN_DK, 0),
                  _sds((bsz * s, GDN_QK)), sem=("parallel", "parallel"))(proj, w)


def _gdn_conv_bwd(mode, proj, w, d_y, d_proj, bsz, s):
    off = {"q": 0, "k": GDN_HEADS, "v": 2 * GDN_HEADS}[mode]
    fn = _qkv_fn(mode)

    def body(x_ref, w_ref, dy_ref, _, dx_ref, dw_ref):
        _zero_at_first([dw_ref], pl.program_id(1) == 0)
        _, vjp = jax.vjp(fn, x_ref[...].astype(F32), *[w_ref[k:k + 1, :] for k in range(CONV_K)])
        grads = vjp(dy_ref[...])
        dx_ref[...] = grads[0].astype(BF16)
        for k in range(CONV_K):
            dw_ref[k:k + 1, :] += grads[1 + k]

    return _pcall("gdn_conv_bwd_" + mode, body, (GDN_HEADS, bsz),
                  [_seq_spec(s, GDN_DK, off), _par_spec(CONV_K, GDN_DK, off), _seq_spec(s, GDN_DK, 0), _ANY],
                  [_seq_spec(s, GDN_DK, off), _par_spec(CONV_K, GDN_DK, 0)],
                  [_sds(d_proj.shape, BF16), _sds((CONV_K, GDN_QK))], sem=("parallel", "arbitrary"), aliases={3: 0})(
                      proj, w, d_y, d_proj)


def _ssd_conv_fn(x, bias, *wrows):
    return _silu(_conv(x, wrows) + bias)


_XBC_OFF = (5 * 1024) // LANE


def _ssd_conv_fwd(proj, w, bias, bsz, s):
    nt_ = SSD_CONV_CH // LANE

    def body(x_ref, w_ref, b_ref, y_ref):
        y_ref[...] = _ssd_conv_fn(x_ref[...].astype(F32), b_ref[...], *[w_ref[k:k + 1, :] for k in range(CONV_K)])

    return _pcall("ssd_conv_fwd", body, (nt_, bsz),
                  [_seq_spec(s, LANE, _XBC_OFF), _par_spec(CONV_K, LANE, 0), _par_spec(1, LANE, 0)], _seq_spec(s, LANE, 0),
                  _sds((bsz * s, SSD_CONV_CH)), sem=("parallel", "parallel"))(proj, w, bias)


def _ssd_conv_bwd(proj, w, bias, d_y, d_proj, bsz, s):
    nt_ = SSD_CONV_CH // LANE

    def body(x_ref, w_ref, b_ref, dy_ref, _, dx_ref, dw_ref, db_ref):
        _zero_at_first([dw_ref, db_ref], pl.program_id(1) == 0)
        _, vjp = jax.vjp(_ssd_conv_fn, x_ref[...].astype(F32), b_ref[...], *[w_ref[k:k + 1, :] for k in range(CONV_K)])
        grads = vjp(dy_ref[...])
        dx_ref[...] = grads[0].astype(BF16)
        db_ref[...] += grads[1]
        for k in range(CONV_K):
            dw_ref[k:k + 1, :] += grads[2 + k]

    return _pcall("ssd_conv_bwd", body, (nt_, bsz),
                  [_seq_spec(s, LANE, _XBC_OFF), _par_spec(CONV_K, LANE, 0), _par_spec(1, LANE, 0), _seq_spec(s, LANE, 0), _ANY],
                  [_seq_spec(s, LANE, _XBC_OFF), _par_spec(CONV_K, LANE, 0), _par_spec(1, LANE, 0)],
                  [_sds(d_proj.shape, BF16), _sds((CONV_K, SSD_CONV_CH)), _sds((1, SSD_CONV_CH))],
                  sem=("parallel", "arbitrary"), aliases={4: 0})(proj, w, bias, d_y, d_proj)


_FFN_TC = 256
_FFN_NT = D_FF // _FFN_TC


def _ffn_act_fn(xg, xu, bg, bu, *wrows):
    k_w = FFN_CONV_K
    gate = _conv(xg, wrows[:k_w]) + bg
    up = _conv(xu, wrows[k_w:]) + bu
    return _silu(gate) * up


def _ffn_act_fwd(u_pre, w, bias, bsz, s):
    def body(xg_ref, xu_ref, wg_ref, wu_ref, bg_ref, bu_ref, a_ref):
        rows = [wg_ref[k:k + 1, :] for k in range(FFN_CONV_K)] + [wu_ref[k:k + 1, :] for k in range(FFN_CONV_K)]
        a_ref[...] = _ffn_act_fn(xg_ref[...].astype(F32), xu_ref[...].astype(F32), bg_ref[...], bu_ref[...], *rows).astype(BF16)

    return _pcall("ffn_act_fwd", body, (_FFN_NT, bsz),
                  [_seq_spec(s, _FFN_TC, 0), _seq_spec(s, _FFN_TC, _FFN_NT),
                   _par_spec(FFN_CONV_K, _FFN_TC, 0), _par_spec(FFN_CONV_K, _FFN_TC, _FFN_NT),
                   _par_spec(1, _FFN_TC, 0), _par_spec(1, _FFN_TC, _FFN_NT)],
                  _seq_spec(s, _FFN_TC, 0), _sds((bsz * s, D_FF), BF16), sem=("parallel", "parallel"))(
                      u_pre, u_pre, w, w, bias, bias)


def _ffn_act_bwd(u_pre, w, bias, d_a, bsz, s):
    def body(xg_ref, xu_ref, wg_ref, wu_ref, bg_ref, bu_ref, da_ref, dg_ref, du_ref, dwg_ref, dwu_ref, dbg_ref, dbu_ref):
        _zero_at_first([dwg_ref, dwu_ref, dbg_ref, dbu_ref], pl.program_id(1) == 0)
        rows = [wg_ref[k:k + 1, :] for k in range(FFN_CONV_K)] + [wu_ref[k:k + 1, :] for k in range(FFN_CONV_K)]
        _, vjp = jax.vjp(_ffn_act_fn, xg_ref[...].astype(F32), xu_ref[...].astype(F32), bg_ref[...], bu_ref[...], *rows)
        grads = vjp(da_ref[...])
        dg_ref[...] = grads[0].astype(BF16)
        du_ref[...] = grads[1].astype(BF16)
        dbg_ref[...] += grads[2]
        dbu_ref[...] += grads[3]
        for k in range(FFN_CONV_K):
            dwg_ref[k:k + 1, :] += grads[4 + k]
            dwu_ref[k:k + 1, :] += grads[4 + FFN_CONV_K + k]

    seq0, par3, par1 = _seq_spec(s, _FFN_TC, 0), _par_spec(FFN_CONV_K, _FFN_TC, 0), _par_spec(1, _FFN_TC, 0)
    return _pcall("ffn_act_bwd", body, (_FFN_NT, bsz),
                  [seq0, _seq_spec(s, _FFN_TC, _FFN_NT), par3, _par_spec(FFN_CONV_K, _FFN_TC, _FFN_NT),
                   par1, _par_spec(1, _FFN_TC, _FFN_NT), seq0],
                  [seq0, seq0, par3, par3, par1, par1],
                  [_sds((bsz * s, D_FF), BF16), _sds((bsz * s, D_FF), BF16), _sds((FFN_CONV_K, D_FF)), _sds((FFN_CONV_K, D_FF)),
                   _sds((1, D_FF)), _sds((1, D_FF))], sem=("parallel", "arbitrary"))(u_pre, u_pre, w, w, bias, bias, d_a)


_GP = GDN_HEADS // 2
_SP = SSD_HEADS // 2


def _pair_lanes(p):
    return slice(2 * p * LANE, (2 * p + 1) * LANE), slice((2 * p + 1) * LANE, (2 * p + 2) * LANE)


_LAST = slice(CHUNK - 1, CHUNK)


def _gdn_args(p, q_ref, k_ref, v_ref, g_ref, b_ref, gr_ref):
    la, lb = _pair_lanes(p)
    return (q_ref[:, la], q_ref[:, lb], k_ref[:, la], k_ref[:, lb], v_ref[:, la], v_ref[:, lb], g_ref[:, la], g_ref[:, lb],
            b_ref[:, la], b_ref[:, lb], gr_ref[p], g_ref[_LAST, la], g_ref[_LAST, lb])


def _gdn_fwd(q, k, v, act, cum, gc_row, bsz, n):
    def body(q_ref, k_ref, v_ref, act_ref, cum_ref, gr_ref, o_ref, sin_ref, s_scr, g_ref, b_ref):
        _zero_at_first([s_scr], pl.program_id(1) == 0)
        _expand_lanes(act_ref, b_ref, 0, GDN_HEADS, GDN_DK)
        _expand_lanes(cum_ref, g_ref, _G_LO, GDN_HEADS, GDN_DK)
        flat = []
        for p in range(_GP):
            flat += [*_gdn_args(p, q_ref, k_ref, v_ref, g_ref, b_ref, gr_ref), s_scr[2 * p], s_scr[2 * p + 1]]
        sin_ref[...] = s_scr[...]
        outs = _gdn_multi(*flat)
        for p in range(_GP):
            la, lb = _pair_lanes(p)
            o_ref[:, la], o_ref[:, lb], s_scr[2 * p], s_scr[2 * p + 1] = outs[4 * p:4 * p + 4]

    tspec = pl.BlockSpec((CHUNK, GDN_V), lambda b, c: (b * n + c, 0))
    rspec = pl.BlockSpec((_GP, 1, LANE), lambda b, c: (b * n + c, 0, 0))
    sspec = pl.BlockSpec((GDN_HEADS, LANE, LANE), lambda b, c: (b * n + c, 0, 0))
    nspec = pl.BlockSpec((CHUNK, SMALL), lambda b, c: (b * n + c, 0))
    wide = pltpu.VMEM((CHUNK, GDN_V), F32)
    return _pcall("gdn_fwd", body, (bsz, n), [tspec] * 3 + [nspec, nspec, rspec], [tspec, sspec],
                  [_sds((bsz * n * CHUNK, GDN_V)), _sds((bsz * n * GDN_HEADS, LANE, LANE))],
                  scratch=[pltpu.VMEM((GDN_HEADS, LANE, LANE), F32), wide, wide], sem=("parallel", "arbitrary"))(
                      q, k, v, act, cum, gc_row)


def _gdn_bwd(q, k, v, act, cum, gc_row, s_in, d_o, bsz, n):
    def body(q_ref, k_ref, v_ref, act_ref, cum_ref, gr_ref, sin_ref, do_ref, dq_ref, dk_ref, dv_ref, dact_ref, dcum_ref, dgr_ref,
             ds_scr, g_ref, b_ref, dg_ref, db_ref):
        _zero_at_first([ds_scr], pl.program_id(1) == 0)
        _expand_lanes(act_ref, b_ref, 0, GDN_HEADS, GDN_DK)
        _expand_lanes(cum_ref, g_ref, _G_LO, GDN_HEADS, GDN_DK)
        flat, cots = [], []
        for p in range(_GP):
            la, lb = _pair_lanes(p)
            flat += [*_gdn_args(p, q_ref, k_ref, v_ref, g_ref, b_ref, gr_ref), sin_ref[2 * p], sin_ref[2 * p + 1]]
            cots += [do_ref[:, la], do_ref[:, lb], ds_scr[2 * p], ds_scr[2 * p + 1]]
        _, vjp = jax.vjp(_gdn_multi, *flat)
        grads = vjp(tuple(cots))
        for p in range(_GP):
            la, lb = _pair_lanes(p)
            cts = grads[_GDN_NARGS * p:_GDN_NARGS * (p + 1)]
            for ref, i in ((dq_ref, 0), (dk_ref, 2), (dv_ref, 4), (dg_ref, 6), (db_ref, 8)):
                ref[:, la] = cts[i]
                ref[:, lb] = cts[i + 1]
            dgr_ref[p] = cts[10]
            dg_ref[_LAST, la] += cts[11]
            dg_ref[_LAST, lb] += cts[12]
            ds_scr[2 * p] = cts[13]
            ds_scr[2 * p + 1] = cts[14]
        dact_ref[...] = _reduce_lanes(db_ref, 0, GDN_HEADS, GDN_DK)
        dcum_ref[...] = _reduce_lanes(dg_ref, _G_LO, GDN_HEADS, GDN_DK)

    tspec = pl.BlockSpec((CHUNK, GDN_V), lambda b, c: (b * n + (n - 1 - c), 0))
    nspec = pl.BlockSpec((CHUNK, SMALL), lambda b, c: (b * n + (n - 1 - c), 0))
    rspec = pl.BlockSpec((_GP, 1, LANE), lambda b, c: (b * n + (n - 1 - c), 0, 0))
    sspec = pl.BlockSpec((GDN_HEADS, LANE, LANE), lambda b, c: (b * n + (n - 1 - c), 0, 0))
    tok_shape, nar_shape = _sds((bsz * n * CHUNK, GDN_V)), _sds((bsz * n * CHUNK, SMALL))
    wide = pltpu.VMEM((CHUNK, GDN_V), F32)
    return _pcall("gdn_bwd", body, (bsz, n), [tspec] * 3 + [nspec, nspec, rspec, sspec, tspec], [tspec] * 3 + [nspec, nspec, rspec],
                  [tok_shape] * 3 + [nar_shape, nar_shape, _sds((bsz * n * _GP, 1, LANE))],
                  scratch=[pltpu.VMEM((GDN_HEADS, LANE, LANE), F32), wide, wide, wide, wide], sem=("parallel", "arbitrary"))(
                      q, k, v, act, cum, gc_row, s_in, d_o)


_B_OFF = SSD_D // LANE
_C_OFF = (SSD_D + SSD_BC) // LANE
_PPG = _SP // SSD_GROUPS


def _ssd_args(p, x_ref, dt_ref, a_ref, ar_ref):
    lp = slice(p * LANE, (p + 1) * LANE)
    gi = p // _PPG
    b_sl = slice((_B_OFF + gi) * LANE, (_B_OFF + gi + 1) * LANE)
    c_sl = slice((_C_OFF + gi) * LANE, (_C_OFF + gi + 1) * LANE)
    return (x_ref[:, lp], dt_ref[:, lp], a_ref[:, lp], ar_ref[p], a_ref[_LAST, lp], x_ref[:, b_sl], x_ref[:, c_sl])


def _ssd_fwd(xbc, dt, acs, acs_row, bsz, n):
    def body(x_ref, dt_ref, a_ref, ar_ref, y_ref, sin_ref, s_scr):
        _zero_at_first([s_scr], pl.program_id(1) == 0)
        flat = []
        for p in range(_SP):
            flat += [*_ssd_args(p, x_ref, dt_ref, a_ref, ar_ref), s_scr[p]]
        sin_ref[...] = s_scr[...]
        outs = _ssd_multi(*flat)
        for p in range(_SP):
            y_ref[:, p * LANE:(p + 1) * LANE], s_scr[p] = outs[2 * p:2 * p + 2]

    tspec = pl.BlockSpec((CHUNK, SSD_D), lambda b, c: (b * n + c, 0))
    return _pcall("ssd_fwd", body, (bsz, n),
                  [pl.BlockSpec((CHUNK, SSD_CONV_CH), lambda b, c: (b * n + c, 0)), tspec, tspec,
                   pl.BlockSpec((_SP, 1, LANE), lambda b, c: (b * n + c, 0, 0))],
                  [tspec, pl.BlockSpec((_SP, LANE, LANE), lambda b, c: (b * n + c, 0, 0))],
                  [_sds((bsz * n * CHUNK, SSD_D)), _sds((bsz * n * _SP, LANE, LANE))],
                  scratch=[pltpu.VMEM((_SP, LANE, LANE), F32)], sem=("parallel", "arbitrary"))(xbc, dt, acs, acs_row)


def _ssd_bwd(xbc, dt, acs, acs_row, s_in, d_y, d_x_skip, bsz, n):
    def body(x_ref, dt_ref, a_ref, ar_ref, sin_ref, dy_ref, dsk_ref, dx_ref, ddt_ref, da_ref, dar_ref, ds_scr):
        _zero_at_first([ds_scr], pl.program_id(1) == 0)
        d_b = [None] * SSD_GROUPS
        d_c = [None] * SSD_GROUPS
        flat, cots = [], []
        for p in range(_SP):
            flat += [*_ssd_args(p, x_ref, dt_ref, a_ref, ar_ref), sin_ref[p]]
            cots += [dy_ref[:, p * LANE:(p + 1) * LANE], ds_scr[p]]
        _, vjp = jax.vjp(_ssd_multi, *flat)
        grads = vjp(tuple(cots))
        for p in range(_SP):
            lp = slice(p * LANE, (p + 1) * LANE)
            gi = p // _PPG
            cts = grads[_SSD_NARGS * p:_SSD_NARGS * (p + 1)]
            dx_ref[:, lp] = cts[0] + dsk_ref[:, lp]
            ddt_ref[:, lp] = cts[1]
            da_ref[:, lp] = cts[2]
            dar_ref[p] = cts[3]
            da_ref[_LAST, lp] += cts[4]
            d_b[gi] = cts[5] if d_b[gi] is None else d_b[gi] + cts[5]
            d_c[gi] = cts[6] if d_c[gi] is None else d_c[gi] + cts[6]
            ds_scr[p] = cts[7]
        for gi in range(SSD_GROUPS):
            dx_ref[:, (_B_OFF + gi) * LANE:(_B_OFF + gi + 1) * LANE] = d_b[gi]
            dx_ref[:, (_C_OFF + gi) * LANE:(_C_OFF + gi + 1) * LANE] = d_c[gi]

    def rev(b, c):
        return b * n + (n - 1 - c)

    tspec = pl.BlockSpec((CHUNK, SSD_D), lambda b, c: (rev(b, c), 0))
    xspec = pl.BlockSpec((CHUNK, SSD_CONV_CH), lambda b, c: (rev(b, c), 0))
    rspec = pl.BlockSpec((_SP, 1, LANE), lambda b, c: (rev(b, c), 0, 0))
    tok_shape = _sds((bsz * n * CHUNK, SSD_D))
    return _pcall("ssd_bwd", body, (bsz, n),
                  [xspec, tspec, tspec, rspec, pl.BlockSpec((_SP, LANE, LANE), lambda b, c: (rev(b, c), 0, 0)), tspec, tspec],
                  [xspec, tspec, tspec, rspec],
                  [_sds((bsz * n * CHUNK, SSD_CONV_CH)), tok_shape, tok_shape, _sds((bsz * n * _SP, 1, LANE))],
                  scratch=[pltpu.VMEM((_SP, LANE, LANE), F32)], sem=("parallel", "arbitrary"))(
                      xbc, dt, acs, acs_row, s_in, d_y, d_x_skip)


def _add2(name, a, b):
    t, c = a.shape
    tm = _row_tile(t)

    def body(a_ref, b_ref, o_ref):
        o_ref[...] = a_ref[...] + b_ref[...]

    return _pcall(name, body, (t // tm,), [_row_spec(tm, c)] * 2, _row_spec(tm, c), _sds((t, c)), sem=("parallel",))(a, b)


def _rep(p, width):
    return jnp.repeat(p.reshape(-1), width).reshape(1, -1)


def _to_rows(narrow, lo, hi, bsz, n):
    heads = hi - lo
    a = narrow[:, lo:hi].reshape(bsz, n, CHUNK, heads)
    return jnp.transpose(a, (0, 1, 3, 2)).reshape(bsz * n * (heads // 2), 1, 2 * CHUNK)


def _from_rows(rows, heads, bsz, n):
    return jnp.transpose(rows.reshape(bsz, n, heads, CHUNK), (0, 1, 3, 2)).reshape(bsz * n * CHUNK, heads)


def _narrow_row(gdn_part, ssd_part):
    return jnp.pad(jnp.concatenate([gdn_part, ssd_part], axis=1), ((0, 0), (_G_LO, SMALL - _S_HI)))


def _local_step(x, tgt, p):
    bsz, s, _ = x.shape
    t = bsz * s
    n = s // CHUNK
    x2 = x.reshape(t, D_MODEL)
    tgt2 = tgt.reshape(t, D_MODEL)
    w_in = p["w_in"]
    w_big = jnp.concatenate([w_in[:, :4096], w_in[:, 4112:6672]], axis=1)
    w_small = jnp.concatenate([w_in[:, 4096:4112], w_in[:, 6672:6688], jnp.zeros((D_MODEL, SMALL - 32), BF16)], axis=1)
    gate_bias = _narrow_row(p["gdn_dt_bias"], p["ssd_dt_bias"])
    gate_a_log = _narrow_row(p["gdn_a_log"], p["ssd_a_log"])
    d_skip = _rep(p["ssd_d"], SSD_HEADDIM)

    h = _rms_fwd("rms0_fwd", x2, p["pre_mix_norm"])
    proj = _matmul("mm_in_big", h, w_big, "nn", BF16, (1024, 3328))
    small = _matmul("mm_in_small", h, w_small, "nn", F32, (1024, 128))
    gact, cum, dt, acs = _gates_fwd(small, gate_bias, gate_a_log)
    gc_row = _to_rows(cum, _G_LO, _G_HI, bsz, n)
    acs_row = _to_rows(cum, _S_LO, _S_HI, bsz, n)
    q = _gdn_conv_fwd("q", proj, p["gdn_conv_w"], bsz, s)
    k = _gdn_conv_fwd("k", proj, p["gdn_conv_w"], bsz, s)
    v = _gdn_conv_fwd("v", proj, p["gdn_conv_w"], bsz, s)
    o, gdn_s = _gdn_fwd(q, k, v, gact, cum, gc_row, bsz, n)
    ocat = _gdn_out_fwd(o, proj, p["gdn_norm_w"])
    xbc = _ssd_conv_fwd(proj, p["ssd_conv_w"], p["ssd_conv_b"], bsz, s)
    y, ssd_s = _ssd_fwd(xbc, dt, acs, acs_row, bsz, n)
    ocat = _ssd_out_fwd(y, xbc, proj, d_skip, p["ssd_norm_w"], ocat)
    mix = _matmul("mm_out", ocat, p["w_out"], "nn", F32, (1024, 1024))
    x1, h2 = _res1_fwd(x2, mix, p["post_mix_norm"], p["pre_ffn_norm"])
    u_pre = _matmul("mm_up", h2, p["w_up"], "nn", BF16, (1024, 2816))
    act = _ffn_act_fwd(u_pre, p["ffn_conv_w"], p["ffn_conv_b"], bsz, s)
    f = _matmul("mm_down", act, p["w_down"], "nn", F32, (1024, 1024))
    loss_acc, d_out, d_f, g_post_ffn = _final(x1, f, p["post_ffn_norm"], tgt2)

    grads = {"post_ffn_norm": g_post_ffn}
    d_act = _matmul("mm_down_dx", d_f, p["w_down"], "nt", F32, (1024, 2816))
    grads["w_down"] = _matmul("mm_down_dw", act, d_f, "tn", F32, (2816, 1024, 1024))
    d_gate, d_up, dwg, dwu, dbg, dbu = _ffn_act_bwd(u_pre, p["ffn_conv_w"], p["ffn_conv_b"], d_act, bsz, s)
    grads["ffn_conv_w"] = jnp.concatenate([dwg, dwu], axis=1)
    grads["ffn_conv_b"] = jnp.concatenate([dbg, dbu], axis=1)
    d_h2 = _matmul_nt_split("mm_up_dx", d_gate, d_up, p["w_up"], 512)
    dw_up = _matmul("mm_up_dw_gate", h2, d_gate, "tn", F32, (1024, 2816, 1024), part=(2 * D_FF, 0, None))
    grads["w_up"] = _matmul("mm_up_dw_up", h2, d_up, "tn", F32, (1024, 2816, 1024), part=(2 * D_FF, D_FF, dw_up))
    d_x1, d_mix, grads["post_mix_norm"], grads["pre_ffn_norm"] = _res1_bwd(
        x2, mix, p["post_mix_norm"], p["pre_ffn_norm"], d_out, d_h2)
    d_ocat = _matmul("mm_out_dx", d_mix, p["w_out"], "nt", F32, (1024, 2048))
    grads["w_out"] = _matmul("mm_out_dw", ocat, d_mix, "tn", F32, (2048, 1024, 1024))

    d_o, d_proj, grads["gdn_norm_w"] = _gdn_out_bwd(o, proj, p["gdn_norm_w"], d_ocat)
    d_q, d_k, d_v, d_act_g, d_cum_g, d_gc_row = _gdn_bwd(q, k, v, gact, cum, gc_row, gdn_s, d_o, bsz, n)
    d_proj, dwq = _gdn_conv_bwd("q", proj, p["gdn_conv_w"], d_q, d_proj, bsz, s)
    d_proj, dwk = _gdn_conv_bwd("k", proj, p["gdn_conv_w"], d_k, d_proj, bsz, s)
    d_proj, dwv = _gdn_conv_bwd("v", proj, p["gdn_conv_w"], d_v, d_proj, bsz, s)
    grads["gdn_conv_w"] = jnp.concatenate([dwq, dwk, dwv], axis=1)

    d_y, d_xs_skip, d_proj, d_dskip, grads["ssd_norm_w"] = _ssd_out_bwd(y, xbc, proj, d_skip, p["ssd_norm_w"], d_ocat, d_proj)
    d_xbc, d_dt, d_acs, d_acs_row = _ssd_bwd(xbc, dt, acs, acs_row, ssd_s, d_y, d_xs_skip, bsz, n)
    d_proj, grads["ssd_conv_w"], grads["ssd_conv_b"] = _ssd_conv_bwd(proj, p["ssd_conv_w"], p["ssd_conv_b"], d_xbc, d_proj, bsz, s)

    d_cum_rows = jnp.concatenate([jnp.zeros((t, _G_LO), F32), _from_rows(d_gc_row, GDN_HEADS, bsz, n),
                                  _from_rows(d_acs_row, SSD_HEADS, bsz, n), jnp.zeros((t, SMALL - _S_HI), F32)], axis=1)
    d_small, d_gate_bias, d_gate_a_log = _gates_bwd(small, gate_bias, gate_a_log, d_act_g, [d_cum_g, d_cum_rows], d_dt, d_acs)
    grads["gdn_dt_bias"], grads["ssd_dt_bias"] = d_gate_bias[:, _G_LO:_G_HI], d_gate_bias[:, _S_LO:_S_HI]
    grads["gdn_a_log"], grads["ssd_a_log"] = d_gate_a_log[:, _G_LO:_G_HI], d_gate_a_log[:, _S_LO:_S_HI]
    d_h_big = _matmul("mm_in_big_dx", d_proj, w_big, "nt", F32, (512, 1024))
    d_h_small = _matmul("mm_in_small_dx", d_small, w_small, "nt", F32, (1024, 1024))
    dw_big = _matmul("mm_in_big_dw", h, d_proj, "tn", F32, (1024, 3328, 1024))
    dw_small = _matmul("mm_in_small_dw", h, d_small, "tn", F32)
    grads["w_in"] = jnp.concatenate([dw_big[:, :4096], dw_small[:, :16], dw_big[:, 4096:], dw_small[:, 16:32]], axis=1)
    grad_x, grads["pre_mix_norm"] = _rms1_bwd(x2, p["pre_mix_norm"], d_h_big, d_h_small, d_x1)
    grads["ssd_d"] = _head_sums(d_dskip)[:1, :SSD_HEADS]
    return loss_acc, grad_x.reshape(bsz, s, D_MODEL), grads


def _head_sums(wide):
    def body(x_ref, o_ref):
        r, c = _iota2((D_MODEL, SMALL))
        o_ref[...] = _mask_dot(jnp.broadcast_to(x_ref[...], (8, D_MODEL)), ((r >> 6) == c).astype(F32), NN, True)

    return _pcall("head_sums", body, (1,), [_full_spec((1, D_MODEL))], _full_spec((8, SMALL)), _sds((8, SMALL)))(wide)


def _adamw_fn(w, g, m, v):
    m = ADAM_B1 * m + (1.0 - ADAM_B1) * g
    v = ADAM_B2 * v + (1.0 - ADAM_B2) * (g * g)
    m_hat = m / (1.0 - ADAM_B1 ** ADAM_STEP)
    v_hat = v / (1.0 - ADAM_B2 ** ADAM_STEP)
    delta = -ADAM_LR * (m_hat / (jnp.sqrt(v_hat) + ADAM_EPS) + ADAM_WD * w)
    return delta, m, v


def _adamw(name, w, g, m, v):
    r, c = w.shape
    tr = _pick(r, (256, 176, 128, 64, 8))

    def body(w_ref, g_ref, m_ref, v_ref, d_ref, m2_ref, v2_ref):
        d, m2, v2 = _adamw_fn(w_ref[...], g_ref[...], m_ref[...], v_ref[...])
        d_ref[...] = d
        m2_ref[...] = m2
        v2_ref[...] = v2

    spec = pl.BlockSpec((tr, c), lambda i: (i, 0))
    return _pcall(name, body, (r // tr,), [spec] * 4, [spec] * 3, [_sds((r, c))] * 3, sem=("parallel",))(w, g, m, v)


_ANY = pl.BlockSpec(memory_space=pl.ANY)
_OTHER_CHIPS = ((1, 0), (0, 1), (1, 1))


def _coords():
    return lax.axis_index("x"), lax.axis_index("y"), lax.axis_index("c")


def _flip(v, f):
    return 1 - v if f else v


def _gather_chips(arrs, split):
    n = len(arrs)

    def body(*refs):
        ins, outs = refs[:n], refs[n:2 * n]
        send_sems, recv_sems, fwd_send_sems, fwd_recv_sems, own_send_sems, own_recv_sems = refs[2 * n:]
        x, y, c = _coords()
        me = 2 * x + y
        sib = (x, y, 1 - c)

        def rows(a, core):
            if not split[a]:
                return slice(None)
            half = arrs[a].shape[0] // 2
            return pl.ds(core * half, half)

        sends = []
        own = []
        for a in range(n):
            cp = pltpu.make_async_remote_copy(ins[a], outs[a].at[me], own_send_sems.at[a], own_recv_sems.at[a],
                                              device_id=sib, device_id_type=MESH)
            cp.start()
            own.append(cp)
        for a in range(n):
            for j, (fx, fy) in enumerate(_OTHER_CHIPS):
                cp = pltpu.make_async_remote_copy(ins[a].at[rows(a, c)], outs[a].at[me, rows(a, c)],
                                                  send_sems.at[a * 3 + j], recv_sems.at[a * 3 + j],
                                                  device_id=(_flip(x, fx), _flip(y, fy), c), device_id_type=MESH)
                cp.start()
                sends.append(cp)
        for a in range(n):
            for j, (fx, fy) in enumerate(_OTHER_CHIPS):
                src = 2 * _flip(x, fx) + _flip(y, fy)
                landed = outs[a].at[src, rows(a, c)]
                pltpu.make_async_remote_copy(landed, landed, send_sems.at[a * 3 + j], recv_sems.at[a * 3 + j],
                                             device_id=(_flip(x, fx), _flip(y, fy), c), device_id_type=MESH).wait_recv()
                if split[a]:
                    fw = pltpu.make_async_remote_copy(landed, landed, fwd_send_sems.at[a * 3 + j], fwd_recv_sems.at[a * 3 + j],
                                                      device_id=sib, device_id_type=MESH)
                    fw.start()
                    sends.append(fw)
        for a in range(n):
            if split[a]:
                for j, (fx, fy) in enumerate(_OTHER_CHIPS):
                    src = 2 * _flip(x, fx) + _flip(y, fy)
                    theirs = outs[a].at[src, rows(a, 1 - c)]
                    pltpu.make_async_remote_copy(theirs, theirs, fwd_send_sems.at[a * 3 + j], fwd_recv_sems.at[a * 3 + j],
                                                 device_id=sib, device_id_type=MESH).wait_recv()
        for cp in own:
            cp.wait_recv()
        for cp in sends + own:
            cp.wait_send()

    return pl.pallas_call(
        body, name="gather_chips", out_shape=[_sds((N_CHIPS,) + a.shape, a.dtype) for a in arrs],
        in_specs=[_ANY] * n, out_specs=[_ANY] * n,
        scratch_shapes=[pltpu.SemaphoreType.DMA((3 * n,))] * 4 + [pltpu.SemaphoreType.DMA((n,))] * 2,
        compiler_params=pltpu.CompilerParams(has_side_effects=True))(*arrs)


_PEERS = tuple((fx, fy, fc) for fx in (0, 1) for fy in (0, 1) for fc in (0, 1))[1:]


def _allreduce_small(x):
    r = x.shape[0]

    def body(x_ref, o_ref, buf, send_sems, recv_sems):
        cx, cy, cc = _coords()
        me = 4 * cx + 2 * cy + cc
        sends = []
        for j, (fx, fy, fc) in enumerate(_PEERS):
            cp = pltpu.make_async_remote_copy(x_ref, buf.at[me], send_sems.at[j], recv_sems.at[j],
                                              device_id=(_flip(cx, fx), _flip(cy, fy), _flip(cc, fc)), device_id_type=MESH)
            cp.start()
            sends.append(cp)
        buf[pl.ds(me, 1)] = x_ref[...][None]
        for j, (fx, fy, fc) in enumerate(_PEERS):
            src = 4 * _flip(cx, fx) + 2 * _flip(cy, fy) + _flip(cc, fc)
            pltpu.make_async_remote_copy(x_ref, buf.at[src], send_sems.at[j], recv_sems.at[j],
                                         device_id=(_flip(cx, fx), _flip(cy, fy), _flip(cc, fc)), device_id_type=MESH).wait_recv()
        for cp in sends:
            cp.wait_send()
        acc = buf[0]
        for d in range(1, N_DEV):
            acc = acc + buf[d]
        o_ref[...] = acc

    vm = pl.BlockSpec(memory_space=pltpu.VMEM)
    return pl.pallas_call(
        body, name="allreduce_small", out_shape=_sds((r, LANE)), in_specs=[vm], out_specs=vm,
        scratch_shapes=[pltpu.VMEM((N_DEV, r, LANE), F32), pltpu.SemaphoreType.DMA((7,)), pltpu.SemaphoreType.DMA((7,))],
        compiler_params=pltpu.CompilerParams(has_side_effects=True, vmem_limit_bytes=VMEM_LIMIT))(x)


def _pair_send_other_half(arrs):
    n = len(arrs)

    def body(*refs):
        ins, outs = refs[:n], refs[n:2 * n]
        send_sems, recv_sems = refs[2 * n:]
        x, y, c = _coords()
        sends = []
        for a in range(n):
            half = ins[a].shape[1] // 2
            cp = pltpu.make_async_remote_copy(ins[a].at[:, pl.ds((1 - c) * half, half), :], outs[a], send_sems.at[a], recv_sems.at[a],
                                              device_id=(x, y, 1 - c), device_id_type=MESH)
            cp.start()
            sends.append(cp)
        for cp in sends:
            cp.wait_recv()
        for cp in sends:
            cp.wait_send()

    return pl.pallas_call(
        body, name="pair_reduce_send", out_shape=[_sds((a.shape[0], a.shape[1] // 2, a.shape[2]), a.dtype) for a in arrs],
        in_specs=[_ANY] * n, out_specs=[_ANY] * n, scratch_shapes=[pltpu.SemaphoreType.DMA((n,))] * 2,
        compiler_params=pltpu.CompilerParams(has_side_effects=True))(*arrs)


def _pair_fill(arrs):
    n = len(arrs)

    def body(*refs):
        bufs = refs[n:2 * n]
        send_sems, recv_sems = refs[2 * n:]
        x, y, c = _coords()
        sends = []
        for a in range(n):
            cp = pltpu.make_async_remote_copy(bufs[a].at[c], bufs[a].at[c], send_sems.at[a], recv_sems.at[a],
                                              device_id=(x, y, 1 - c), device_id_type=MESH)
            cp.start()
            sends.append(cp)
        for a in range(n):
            theirs = bufs[a].at[1 - c]
            pltpu.make_async_remote_copy(theirs, theirs, send_sems.at[a], recv_sems.at[a],
                                         device_id=(x, y, 1 - c), device_id_type=MESH).wait_recv()
        for cp in sends:
            cp.wait_send()

    return pl.pallas_call(
        body, name="pair_gather", out_shape=[_sds(a.shape, a.dtype) for a in arrs], in_specs=[_ANY] * n, out_specs=[_ANY] * n,
        scratch_shapes=[pltpu.SemaphoreType.DMA((n,))] * 2, input_output_aliases={a: a for a in range(n)},
        compiler_params=pltpu.CompilerParams(has_side_effects=True))(*arrs)


def _scatter_chips(arrs):
    n = len(arrs)

    def body(*refs):
        ins, outs = refs[:n], refs[n:2 * n]
        send_sems, recv_sems = refs[2 * n:]
        x, y, c = _coords()
        sends = []
        for a in range(n):
            for j, (fx, fy) in enumerate(_OTHER_CHIPS):
                to = 2 * _flip(x, fx) + _flip(y, fy)
                cp = pltpu.make_async_remote_copy(ins[a].at[to], outs[a].at[j], send_sems.at[a * 3 + j], recv_sems.at[a * 3 + j],
                                                  device_id=(_flip(x, fx), _flip(y, fy), c), device_id_type=MESH)
                cp.start()
                sends.append(cp)
        for cp in sends:
            cp.wait_recv()
        for cp in sends:
            cp.wait_send()

    return pl.pallas_call(
        body, name="scatter_chips", out_shape=[_sds((3,) + a.shape[1:], a.dtype) for a in arrs],
        in_specs=[_ANY] * n, out_specs=[_ANY] * n, scratch_shapes=[pltpu.SemaphoreType.DMA((3 * n,))] * 2,
        compiler_params=pltpu.CompilerParams(has_side_effects=True))(*arrs)


def _pair_add(name, full, recv, core):
    _, r, c = full.shape
    half = r // 2
    tr = _pick(half, (256, 176, 128, 64, 8))
    nb = half // tr

    def body(c_ref, a_ref, b_ref, o_ref, ob_ref):
        s = a_ref[...] + b_ref[...]
        o_ref[...] = s
        ob_ref[...] = s.astype(BF16)

    blk = pl.BlockSpec((1, tr, c), lambda k, i, cref: (k, i, 0))
    grid_spec = pltpu.PrefetchScalarGridSpec(
        num_scalar_prefetch=1, grid=(N_CHIPS, nb),
        in_specs=[pl.BlockSpec((1, tr, c), lambda k, i, cref: (k, cref[0] * nb + i, 0)), blk], out_specs=[blk, blk])
    return pl.pallas_call(
        body, name=name, out_shape=[_sds((N_CHIPS, half, c)), _sds((N_CHIPS, half, c), BF16)], grid_spec=grid_spec,
        compiler_params=pltpu.CompilerParams(dimension_semantics=("parallel", "parallel"), vmem_limit_bytes=VMEM_LIMIT))(
            core, full, recv)


def _chip_sum(name, landed, own, where):
    _, r, c = landed.shape
    tr = _pick(r, (256, 176, 128, 64, 16))

    def body(w_ref, l_ref, o_ref, s_ref):
        s_ref[0] = ((o_ref[0] + l_ref[0].astype(F32)) + l_ref[1].astype(F32)) + l_ref[2].astype(F32)

    grid_spec = pltpu.PrefetchScalarGridSpec(
        num_scalar_prefetch=1, grid=(r // tr,),
        in_specs=[pl.BlockSpec((3, tr, c), lambda i, wref: (0, i, 0)),
                  pl.BlockSpec((1, tr, c), lambda i, wref: (wref[0], i, 0))],
        out_specs=pl.BlockSpec((1, tr, c), lambda i, wref: (wref[1], i, 0)))
    return pl.pallas_call(
        body, name=name, out_shape=_sds((2, r, c)), grid_spec=grid_spec,
        compiler_params=pltpu.CompilerParams(dimension_semantics=("parallel",), vmem_limit_bytes=VMEM_LIMIT))(where, landed, own)


_WEIGHTS = ("pre_mix_norm", "w_in", "gdn_conv_w", "gdn_a_log", "gdn_dt_bias", "gdn_norm_w", "ssd_conv_w", "ssd_conv_b",
            "ssd_a_log", "ssd_dt_bias", "ssd_d", "ssd_norm_w", "w_out", "post_mix_norm", "pre_ffn_norm", "w_up",
            "ffn_conv_w", "ffn_conv_b", "w_down", "post_ffn_norm")
_BIG = ("w_in", "w_out", "w_up", "w_down")
_COL_SHARDED_SMALL = ("gdn_conv_w", "ssd_conv_w", "ffn_conv_w")
_SMALL = tuple(k for k in _WEIGHTS if k not in _BIG)


def _pack(arrs):
    flat = jnp.concatenate([a.reshape(-1) for a in arrs])
    rows = -(-flat.shape[0] // (8 * LANE)) * 8
    return jnp.pad(flat, (0, rows * LANE - flat.shape[0])).reshape(rows, LANE)


def _unpack(packed, shapes):
    flat = packed.reshape(-1)
    out, off = [], 0
    for shp in shapes:
        size = 1
        for d in shp:
            size *= d
        out.append(flat[off:off + size].reshape(shp))
        off += size
    return out


def _cols_to_chips(a):
    r, c4 = a.shape
    return jnp.transpose(a.reshape(r, N_CHIPS, c4 // N_CHIPS), (1, 0, 2))


def _chips_to_cols(a):
    k, r, c = a.shape
    return jnp.transpose(a, (1, 0, 2)).reshape(r, k * c)


def kernel(x, pre_mix_norm, w_in, gdn_conv_w, gdn_a_log, gdn_dt_bias, gdn_norm_w, ssd_conv_w, ssd_conv_b, ssd_a_log, ssd_dt_bias, ssd_d, ssd_norm_w, w_out, post_mix_norm, pre_ffn_norm, w_up, ffn_conv_w, ffn_conv_b, w_down, post_ffn_norm, loss_target, m_pre_mix_norm, m_w_in, m_gdn_conv_w, m_gdn_a_log, m_gdn_dt_bias, m_gdn_norm_w, m_ssd_conv_w, m_ssd_conv_b, m_ssd_a_log, m_ssd_dt_bias, m_ssd_d, m_ssd_norm_w, m_w_out, m_post_mix_norm, m_pre_ffn_norm, m_w_up, m_ffn_conv_w, m_ffn_conv_b, m_w_down, m_post_ffn_norm, v_pre_mix_norm, v_w_in, v_gdn_conv_w, v_gdn_a_log, v_gdn_dt_bias, v_gdn_norm_w, v_ssd_conv_w, v_ssd_conv_b, v_ssd_a_log, v_ssd_dt_bias, v_ssd_d, v_ssd_norm_w, v_w_out, v_post_mix_norm, v_pre_ffn_norm, v_w_up, v_ffn_conv_w, v_ffn_conv_b, v_w_down, v_post_ffn_norm):
    w = dict(zip(_WEIGHTS, (pre_mix_norm, w_in, gdn_conv_w, gdn_a_log, gdn_dt_bias, gdn_norm_w, ssd_conv_w, ssd_conv_b,
                            ssd_a_log, ssd_dt_bias, ssd_d, ssd_norm_w, w_out, post_mix_norm, pre_ffn_norm, w_up,
                            ffn_conv_w, ffn_conv_b, w_down, post_ffn_norm)))
    m = dict(zip(_WEIGHTS, (m_pre_mix_norm, m_w_in, m_gdn_conv_w, m_gdn_a_log, m_gdn_dt_bias, m_gdn_norm_w, m_ssd_conv_w,
                            m_ssd_conv_b, m_ssd_a_log, m_ssd_dt_bias, m_ssd_d, m_ssd_norm_w, m_w_out, m_post_mix_norm,
                            m_pre_ffn_norm, m_w_up, m_ffn_conv_w, m_ffn_conv_b, m_w_down, m_post_ffn_norm)))
    v = dict(zip(_WEIGHTS, (v_pre_mix_norm, v_w_in, v_gdn_conv_w, v_gdn_a_log, v_gdn_dt_bias, v_gdn_norm_w, v_ssd_conv_w,
                            v_ssd_conv_b, v_ssd_a_log, v_ssd_dt_bias, v_ssd_d, v_ssd_norm_w, v_w_out, v_post_mix_norm,
                            v_pre_ffn_norm, v_w_up, v_ffn_conv_w, v_ffn_conv_b, v_w_down, v_post_ffn_norm)))
    cx, cy, cc = _coords()
    chip = 2 * cx + cy

    shards = [w[k][0].astype(BF16) for k in _BIG] + [w[k][0] for k in _COL_SHARDED_SMALL]
    g_in, g_out, g_up, g_down, g_gcw, g_scw, g_fcw = _gather_chips(shards, [True] * len(_BIG) + [False] * len(_COL_SHARDED_SMALL))
    p = {k: w[k] for k in _SMALL if k not in _COL_SHARDED_SMALL}
    p["w_in"] = _chips_to_cols(g_in)
    p["w_up"] = _chips_to_cols(g_up)
    p["w_out"] = g_out.reshape(-1, D_MODEL)
    p["w_down"] = g_down.reshape(-1, D_MODEL)
    p["gdn_conv_w"] = _chips_to_cols(g_gcw)
    p["ssd_conv_w"] = _chips_to_cols(g_scw)
    p["ffn_conv_w"] = _chips_to_cols(g_fcw)

    loss_acc, grad_x, grads = _local_step(x, loss_target, p)
    loss = lax.psum(loss_acc[0, 0], ("x", "y", "c"))

    small_full_shapes = [grads[k].shape for k in _SMALL]
    summed = _unpack(_allreduce_small(_pack([grads[k] for k in _SMALL])), small_full_shapes)
    g_small = dict(zip(_SMALL, summed))
    for k in _COL_SHARDED_SMALL:
        width = w[k].shape[2]
        g_small[k] = lax.dynamic_slice_in_dim(g_small[k], chip * width, width, axis=1)

    big = [_cols_to_chips(grads["w_in"]), grads["w_out"].reshape(N_CHIPS, -1, D_MODEL),
           _cols_to_chips(grads["w_up"]), grads["w_down"].reshape(N_CHIPS, -1, D_MODEL)]
    from_sibling = _pair_send_other_half(big)
    core = cc.astype(jnp.int32).reshape(1)
    pair_sums = [_pair_add("pair_add_" + k, a, b, core) for k, a, b in zip(_BIG, big, from_sibling)]
    landed = _scatter_chips([ps[1] for ps in pair_sums])
    where = jnp.stack([chip, cc]).astype(jnp.int32)
    mine = [_chip_sum("chip_sum_" + k, a, ps[0], where) for k, a, ps in zip(_BIG, landed, pair_sums)]
    both = _pair_fill(mine)
    g_big = {k: a.reshape(-1, a.shape[2]) for k, a in zip(_BIG, both)}

    out_g, out_d, out_m, out_v = {}, {}, {}, {}
    for k in _BIG:
        out_g[k] = g_big[k][None]
        d_, m_, v_ = _adamw("adamw_" + k, w[k][0], g_big[k], m[k][0], v[k][0])
        out_d[k], out_m[k], out_v[k] = d_[None], m_[None], v_[None]
    shapes = [w[k].shape for k in _SMALL]
    for k in _SMALL:
        out_g[k] = g_small[k].reshape(w[k].shape)
    packed = [_pack([d[k] for k in _SMALL]) for d in (w, out_g, m, v)]
    d_p, m_p, v_p = _adamw("adamw_small", *packed)
    for dst, src in ((out_d, d_p), (out_m, m_p), (out_v, v_p)):
        dst.update(zip(_SMALL, _unpack(src, shapes)))
    return (loss, grad_x, *[out_g[k] for k in _WEIGHTS], *[out_d[k] for k in _WEIGHTS],
            *[out_m[k] for k in _WEIGHTS], *[out_v[k] for k in _WEIGHTS])
```

```python
import functools

import jax
import jax.numpy as jnp
from jax import lax
from jax.experimental import pallas as pl
from jax.experimental.pallas import tpu as pltpu

F32 = jnp.float32
BF16 = jnp.bfloat16

D_MODEL = 1024
GDN_HEADS = 8
GDN_DK = 128
SSD_HEADS = 16
SSD_HEADDIM = 64
SSD_GROUPS = 2
SSD_STATE = 128
CONV_K = 4
CHUNK = 64
D_FF = 2816
FFN_CONV_K = 3
EPS = 1e-6
GDN_QK = GDN_HEADS * GDN_DK
GDN_V = GDN_QK
SSD_D = SSD_HEADS * SSD_HEADDIM
SSD_BC = SSD_GROUPS * SSD_STATE
SSD_CONV_CH = SSD_D + 2 * SSD_BC
BIG = 4 * 1024 + 1024 + SSD_CONV_CH
SMALL = 128
D_IN_PROJ = 6688
LANE = 128
PAIR = 2 * CHUNK
NEG = -1e30
VMEM_LIMIT = 56 * 1024 * 1024

ADAM_LR = 0.001
ADAM_B1 = 0.9
ADAM_B2 = 0.999
ADAM_EPS = 1e-08
ADAM_WD = 0.01
ADAM_STEP = 10

N_CHIPS = 4
N_DEV = 8
MESH = pl.DeviceIdType.MESH

NN = ((1,), (0,))
NT = ((1,), (1,))
TN = ((0,), (0,))


def _bdot(a, b, dims):
    return lax.dot_general(a.astype(BF16), b.astype(BF16), (dims, ((), ())), preferred_element_type=F32)


def _split3(a):
    hi = a.astype(BF16)
    r1 = a - hi.astype(F32)
    mid = r1.astype(BF16)
    return hi, mid, (r1 - mid.astype(F32)).astype(BF16)


@jax.custom_vjp
def _nn(a, b):
    return _bdot(a, b, NN)


@jax.custom_vjp
def _nt(a, b):
    return _bdot(a, b, NT)


@jax.custom_vjp
def _tn(a, b):
    return _bdot(a, b, TN)


_nn.defvjp(lambda a, b: (_nn(a, b), (a, b)), lambda r, g: (_nt(g, r[1]), _tn(r[0], g)))
_nt.defvjp(lambda a, b: (_nt(a, b), (a, b)), lambda r, g: (_nn(g, r[1]), _tn(g, r[0])))
_tn.defvjp(lambda a, b: (_tn(a, b), (a, b)), lambda r, g: (_nt(r[1], g), _nn(r[0], g)))


def _mask_dot(x, mask, dims, x_first):
    acc = None
    for piece in _split3(x):
        term = _bdot(piece, mask, dims) if x_first else _bdot(mask, piece, dims)
        acc = term if acc is None else acc + term
    return acc


@jax.custom_vjp
def _cst_left(cst, x):
    return _mask_dot(x, cst, NN, False)


_cst_left.defvjp(lambda cst, x: (_cst_left(cst, x), cst), lambda cst, g: (jnp.zeros_like(cst), _mask_dot(g, cst, TN, False)))


@jax.custom_vjp
def _cst_right(x, cst):
    return _mask_dot(x, cst, NN, True)


_cst_right.defvjp(lambda x, cst: (_cst_right(x, cst), cst), lambda cst, g: (_mask_dot(g, cst, NT, True), jnp.zeros_like(cst)))


def _lin_left(cst):
    return functools.partial(_cst_left, cst)


def _lin_right(cst):
    return lambda x: _cst_right(x, cst)


@jax.custom_vjp
def _tri_inv_m1(a):
    pm = [-x for x in a]
    ap = list(a)
    for _ in range(5):
        ap = [_bdot(x, x, NN) for x in ap]
        pm = [(p + x) + _bdot(p, x, NN) for p, x in zip(pm, ap)]
    return pm


def _tri_inv_m1_bwd(pm, g):
    t = [gi + _bdot(p, gi, TN) for p, gi in zip(pm, g)]
    return ([-(ti + _bdot(ti, p, NT)) for p, ti in zip(pm, t)],)


_tri_inv_m1.defvjp(lambda a: (lambda pm: (pm, pm))(_tri_inv_m1(a)), _tri_inv_m1_bwd)


@jax.custom_vjp
def _top(x):
    return x[: x.shape[0] // 2]


_top.defvjp(lambda x: (_top(x), None), lambda _, g: (jnp.concatenate([g, jnp.zeros_like(g)], axis=0),))


@jax.custom_vjp
def _bot(x):
    return x[x.shape[0] // 2:]


_bot.defvjp(lambda x: (_bot(x), None), lambda _, g: (jnp.concatenate([jnp.zeros_like(g), g], axis=0),))


@jax.custom_vjp
def _vstack(a, b):
    return jnp.concatenate([a, b], axis=0)


_vstack.defvjp(lambda a, b: (_vstack(a, b), None), lambda _, g: (g[: g.shape[0] // 2], g[g.shape[0] // 2:]))


def _shift_dn_raw(x, s):
    if s == 0:
        return x
    r = pltpu.roll(x, s, axis=0)
    ri = lax.broadcasted_iota(jnp.int32, x.shape, 0)
    return jnp.where(ri >= s, r, 0.0)


def _shift_up_raw(x, s):
    if s == 0:
        return x
    n = x.shape[0]
    r = pltpu.roll(x, n - s, axis=0)
    ri = lax.broadcasted_iota(jnp.int32, x.shape, 0)
    return jnp.where(ri < n - s, r, 0.0)


@functools.partial(jax.custom_vjp, nondiff_argnums=(1,))
def _shift_dn(x, s):
    return _shift_dn_raw(x, s)


_shift_dn.defvjp(lambda x, s: (_shift_dn_raw(x, s), None), lambda s, _, g: (_shift_up_raw(g, s),))


def _conv(x, wrows):
    k_w = len(wrows)
    acc = wrows[k_w - 1] * x
    for k in range(k_w - 1):
        acc = acc + wrows[k] * _shift_dn(x, k_w - 1 - k)
    return acc


def _silu(x):
    return x * jax.nn.sigmoid(x)


def _rms(x, w):
    return x * lax.rsqrt(jnp.mean(x * x, axis=-1, keepdims=True) + EPS) * w


def _l2n(x):
    return x * lax.rsqrt(jnp.sum(x * x, axis=-1, keepdims=True) + EPS)


def _iota2(shape):
    return lax.broadcasted_iota(jnp.int32, shape, 0), lax.broadcasted_iota(jnp.int32, shape, 1)


_GDN_NARGS = 15
_SSD_NARGS = 8


def _gdn_multi(*flat):
    pairs = [flat[i:i + _GDN_NARGS] for i in range(0, len(flat), _GDN_NARGS)]
    idx = range(len(pairs))
    ri, ci = _iota2((PAIR, PAIR))
    blk = ((ri >= CHUNK) & (ci >= CHUNK)) | ((ri < CHUNK) & (ci < CHUNK))
    causal = blk & (ri >= ci)
    strict = blk & (ri > ci)
    q = [_vstack(p[0], p[1]) for p in pairs]
    k = [_vstack(p[2], p[3]) for p in pairs]
    v = [_vstack(p[4], p[5]) for p in pairs]
    gc = [_vstack(p[6], p[7]) for p in pairs]
    beta = [_vstack(p[8], p[9]) for p in pairs]
    glast = [_vstack(jnp.broadcast_to(p[11], (CHUNK, LANE)), jnp.broadcast_to(p[12], (CHUNK, LANE))) for p in pairs]
    sa = [p[13] for p in pairs]
    sb = [p[14] for p in pairs]
    decay = [jnp.exp(jnp.where(causal, gc[i] - jnp.broadcast_to(pairs[i][10], (PAIR, PAIR)), NEG)) for i in idx]
    eg = [jnp.exp(x) for x in gc]
    kbeta = [k[i] * beta[i] for i in idx]
    pm = _tri_inv_m1([jnp.where(strict, _nt(kbeta[i], k[i]) * decay[i], 0.0) for i in idx])
    qk = [_nt(q[i], k[i]) * decay[i] for i in idx]
    rhs_v = [v[i] * beta[i] for i in idx]
    rhs_k = [kbeta[i] * eg[i] for i in idx]
    u = [rhs_v[i] + _nn(pm[i], rhs_v[i]) for i in idx]
    w = [rhs_k[i] + _nn(pm[i], rhs_k[i]) for i in idx]
    q_dec = [q[i] * eg[i] for i in idx]
    k_dec = [k[i] * jnp.exp(glast[i] - gc[i]) for i in idx]
    gl = [jnp.exp(x) for x in glast]
    w_s = [_vstack(_nn(_top(w[i]), sa[i]), _nn(_bot(w[i]), sb[i])) for i in idx]
    q_s = [_vstack(_nn(_top(q_dec[i]), sa[i]), _nn(_bot(q_dec[i]), sb[i])) for i in idx]
    v_new = [u[i] - w_s[i] for i in idx]
    o = [q_s[i] + _nn(qk[i], v_new[i]) for i in idx]
    sa2 = [sa[i] * _vstack(_top(gl[i]), _top(gl[i])) + _tn(_top(k_dec[i]), _top(v_new[i])) for i in idx]
    sb2 = [sb[i] * _vstack(_bot(gl[i]), _bot(gl[i])) + _tn(_bot(k_dec[i]), _bot(v_new[i])) for i in idx]
    out = []
    for i in idx:
        out += [_top(o[i]), _bot(o[i]), sa2[i], sb2[i]]
    return tuple(out)


def _ssd_multi(*flat):
    pairs = [flat[i:i + _SSD_NARGS] for i in range(0, len(flat), _SSD_NARGS)]
    idx = range(len(pairs))
    ri, ci = _iota2((CHUNK, PAIR))
    causal = ri >= jnp.where(ci >= CHUNK, ci - CHUNK, ci)
    xdt = [p[0] * p[1] for p in pairs]
    acs = [p[2] for p in pairs]
    alast = [jnp.broadcast_to(p[4], (CHUNK, PAIR)) for p in pairs]
    lmat = [jnp.exp(jnp.where(causal, acs[i] - jnp.broadcast_to(pairs[i][3], (CHUNK, PAIR)), NEG)) for i in idx]
    cb2 = [_nt(p[6], _vstack(p[5], p[5])) for p in pairs]
    xblk = [_vstack(jnp.where(ci < CHUNK, x, 0.0), jnp.where(ci >= CHUNK, x, 0.0)) for x in xdt]
    y_off = [_nn(pairs[i][6], pairs[i][7]) * jnp.exp(acs[i]) for i in idx]
    y = [_nn(cb2[i] * lmat[i], xblk[i]) + y_off[i] for i in idx]
    el = [jnp.exp(x) for x in alast]
    st2 = [pairs[i][7] * _vstack(el[i], el[i]) + _tn(pairs[i][5], xdt[i] * jnp.exp(alast[i] - acs[i])) for i in idx]
    out = []
    for i in idx:
        out += [y[i], st2[i]]
    return tuple(out)


def _pcall(name, body, grid, in_specs, out_specs, out_shape, scratch=(), sem=None, aliases=None):
    if sem is None:
        sem = ("arbitrary",) * len(grid)
    return pl.pallas_call(
        functools.partial(body),
        out_shape=out_shape,
        grid=grid,
        in_specs=in_specs,
        out_specs=out_specs,
        scratch_shapes=scratch,
        input_output_aliases=aliases or {},
        name=name,
        compiler_params=pltpu.CompilerParams(dimension_semantics=sem, vmem_limit_bytes=VMEM_LIMIT),
    )


def _sds(shape, dtype=F32):
    return jax.ShapeDtypeStruct(shape, dtype)


def _row_spec(tm, width, colblock=0):
    return pl.BlockSpec((tm, width), lambda i, _c=colblock: (i, _c))


def _full_spec(shape):
    nd = len(shape)
    return pl.BlockSpec(shape, lambda *_: (0,) * nd)


def _zero_at_first(refs, first):
    @pl.when(first)
    def _():
        for r in refs:
            r[...] = jnp.zeros(r.shape, r.dtype)


def _pick(n, prefs):
    for p in prefs:
        if n % p == 0:
            return p
    return n


def _matmul(name, a, b, mode, out_dtype, tiles=None, part=None):
    def want(i, dim):
        return [tiles[i]] if tiles is not None and dim % tiles[i] == 0 else []

    if mode == "tn":
        r, m = a.shape
        n = b.shape[1]
        tm = _pick(m, want(0, m) + [1024, 1408])
        tn = _pick(n, want(1, n) + [512, 256, 128])
        tk = _pick(r, want(2, r) + [1024, 512, 256, 128, 64])
        nc = _pick(tn, (512, 256, 128))
        n_total, col_off, into = part if part is not None else (n, 0, None)
        off = col_off // tn

        def body(a_ref, b_ref, *rest):
            o_ref = rest[-1]
            _zero_at_first([o_ref], pl.program_id(2) == 0)
            for c0 in range(0, tn, nc):
                o_ref[:, c0:c0 + nc] += _bdot(a_ref[...], b_ref[:, c0:c0 + nc], TN)

        in_specs = [pl.BlockSpec((tk, tm), lambda i, j, k: (k, i)), pl.BlockSpec((tk, tn), lambda i, j, k: (k, j))]
        args = (a, b) if into is None else (a, b, into)
        return _pcall(
            name, body, (m // tm, n // tn, r // tk), in_specs + ([] if into is None else [_ANY]),
            pl.BlockSpec((tm, tn), lambda i, j, k: (i, j + off)), _sds((m, n_total), out_dtype),
            sem=("parallel", "parallel", "arbitrary"), aliases=None if into is None else {2: 0})(*args)
    m, k = a.shape
    n = b.shape[1] if mode == "nn" else b.shape[0]
    tm = _pick(m, want(0, m) + ([1024, 512, 256, 128, 64] if k <= 2816 else [512, 256, 128, 64]))
    tn = _pick(n, want(1, n) + [512, 256, 128])
    dims = NN if mode == "nn" else NT

    nc = _pick(tn, (512, 256, 128))

    def body(a_ref, b_ref, o_ref):
        for c0 in range(0, tn, nc):
            b_blk = b_ref[:, c0:c0 + nc] if mode == "nn" else b_ref[c0:c0 + nc, :]
            o_ref[:, c0:c0 + nc] = _bdot(a_ref[...], b_blk, dims).astype(o_ref.dtype)

    b_spec = pl.BlockSpec((k, tn), lambda i, j: (0, j)) if mode == "nn" else pl.BlockSpec((tn, k), lambda i, j: (j, 0))
    return _pcall(
        name, body, (m // tm, n // tn), [pl.BlockSpec((tm, k), lambda i, j: (i, 0)), b_spec],
        pl.BlockSpec((tm, tn), lambda i, j: (i, j)), _sds((m, n), out_dtype), sem=("parallel", "parallel"))(a, b)


def _matmul_nt_split(name, a1, a2, b, tm_pref):
    m, kh = a1.shape
    n = b.shape[0]
    tm = _pick(m, (tm_pref, 512, 256, 128, 64))
    nc = _pick(n, (512, 256, 128))

    def body(a1_ref, a2_ref, b_ref, o_ref):
        for c0 in range(0, n, nc):
            o_ref[:, c0:c0 + nc] = (_bdot(a1_ref[...], b_ref[c0:c0 + nc, :kh], NT)
                                    + _bdot(a2_ref[...], b_ref[c0:c0 + nc, kh:], NT))

    aspec = pl.BlockSpec((tm, kh), lambda i: (i, 0))
    return _pcall(name, body, (m // tm,), [aspec, aspec, _full_spec(b.shape)], pl.BlockSpec((tm, n), lambda i: (i, 0)),
                  _sds((m, n)), sem=("parallel",))(a1, a2, b)


def _row_tile(t):
    return _pick(t, (256, 128, 64))


def _rms_fwd(name, x, g):
    t = x.shape[0]
    tm = _row_tile(t)

    def body(x_ref, g_ref, h_ref):
        h_ref[...] = _rms(x_ref[...], g_ref[...]).astype(BF16)

    return _pcall(name, body, (t // tm,), [_row_spec(tm, D_MODEL), _full_spec((1, D_MODEL))], _row_spec(tm, D_MODEL),
                  _sds((t, D_MODEL), BF16), sem=("parallel",))(x, g)


_G_LO, _G_HI = GDN_HEADS, 2 * GDN_HEADS
_S_LO, _S_HI = 2 * GDN_HEADS, 2 * GDN_HEADS + SSD_HEADS


def _gates_fn(small, bias, a_log):
    tm = small.shape[0]
    rr, cc = _iota2((tm, tm))
    in_chunk_tril = (((rr >> 6) == (cc >> 6)) & (rr >= cc)).astype(F32)
    lane = lax.broadcasted_iota(jnp.int32, small.shape, 1)
    sp = jax.nn.softplus(small + bias)
    act = jnp.where(lane < _G_LO, jax.nn.sigmoid(small), sp)
    cum = _lin_left(in_chunk_tril)(-jnp.exp(a_log) * sp)
    r, c = _iota2((SMALL, D_MODEL))
    to_ssd_lanes = _lin_right((r == _S_LO + (c >> 6)).astype(F32))
    return act, cum, to_ssd_lanes(act), to_ssd_lanes(cum)


def _expand_lanes(src_ref, dst_ref, lo, heads, width):
    rows = src_ref.shape[0]
    for h in range(heads):
        dst_ref[:, h * width:(h + 1) * width] = jnp.broadcast_to(src_ref[:, lo + h:lo + h + 1], (rows, width))


def _reduce_lanes(wide_ref, lo, heads, width):
    rows = wide_ref.shape[0]
    lane = lax.broadcasted_iota(jnp.int32, (rows, SMALL), 1)
    acc = jnp.zeros((rows, SMALL), F32)
    for h in range(heads):
        col = jnp.sum(wide_ref[:, h * width:(h + 1) * width], axis=-1, keepdims=True)
        acc = jnp.where(lane == lo + h, jnp.broadcast_to(col, (rows, SMALL)), acc)
    return acc


def _gates_fwd(small, bias, a_log):
    t = small.shape[0]
    tm = _row_tile(t)

    def body(s_ref, p0, p1, act_ref, cum_ref, dt_ref, acs_ref):
        act_ref[...], cum_ref[...], dt_ref[...], acs_ref[...] = _gates_fn(s_ref[...], p0[...], p1[...])

    pspec, nspec, wspec = _full_spec((1, SMALL)), _row_spec(tm, SMALL), _row_spec(tm, D_MODEL)
    return _pcall("gates_fwd", body, (t // tm,), [nspec, pspec, pspec], [nspec, nspec, wspec, wspec],
                  [_sds((t, SMALL))] * 2 + [_sds((t, D_MODEL))] * 2, sem=("parallel",))(small, bias, a_log)


def _gates_bwd(small, bias, a_log, d_act, d_cums, d_dt, d_acs):
    t = small.shape[0]
    tm = _row_tile(t)
    nc = len(d_cums)

    def body(*refs):
        s_ref, p0, p1, dact_ref = refs[:4]
        dcum_refs = refs[4:4 + nc]
        ddt_ref, dacs_ref, ds_ref, db_ref, da_ref = refs[4 + nc:]
        _zero_at_first([db_ref, da_ref], pl.program_id(0) == 0)
        _, vjp = jax.vjp(_gates_fn, s_ref[...], p0[...], p1[...])
        d_cum = dcum_refs[0][...]
        for c in dcum_refs[1:]:
            d_cum = d_cum + c[...]
        d_s, d_b, d_a = vjp((dact_ref[...], d_cum, ddt_ref[...], dacs_ref[...]))
        ds_ref[...] = d_s.astype(BF16)
        db_ref[...] += d_b
        da_ref[...] += d_a

    pspec, nspec, wspec = _full_spec((1, SMALL)), _row_spec(tm, SMALL), _row_spec(tm, D_MODEL)
    return _pcall("gates_bwd", body, (t // tm,), [nspec, pspec, pspec] + [nspec] * (1 + nc) + [wspec, wspec],
                  [nspec, pspec, pspec], [_sds((t, SMALL), BF16), _sds((1, SMALL)), _sds((1, SMALL))])(
                      small, bias, a_log, d_act, *d_cums, d_dt, d_acs)


def _gdn_out_fn(o, z, w):
    return _rms(o, w) * _silu(z)


def _gdn_out_fwd(o, proj, gn):
    t = o.shape[0]
    tm = _row_tile(t)

    def body(o_ref, z_ref, w_ref, y_ref):
        for h in range(GDN_HEADS):
            sl = slice(h * GDN_DK, (h + 1) * GDN_DK)
            y_ref[:, sl] = _gdn_out_fn(o_ref[:, sl], z_ref[:, sl].astype(F32), w_ref[...]).astype(BF16)

    return _pcall("gdn_out_fwd", body, (t // tm,), [_row_spec(tm, GDN_V), _row_spec(tm, GDN_V, 3), _full_spec((1, GDN_DK))],
                  _row_spec(tm, GDN_V), _sds((t, GDN_V + SSD_D), BF16), sem=("parallel",))(o, proj, gn)


def _gdn_out_bwd(o, proj, gn, d_ocat):
    t = o.shape[0]
    tm = _row_tile(t)

    def body(o_ref, z_ref, w_ref, dy_ref, do_ref, dz_ref, dw_ref):
        _zero_at_first([dw_ref], pl.program_id(0) == 0)
        for h in range(GDN_HEADS):
            sl = slice(h * GDN_DK, (h + 1) * GDN_DK)
            _, vjp = jax.vjp(_gdn_out_fn, o_ref[:, sl], z_ref[:, sl].astype(F32), w_ref[...])
            d_o, d_z, d_w = vjp(dy_ref[:, sl])
            do_ref[:, sl] = d_o
            dz_ref[:, sl] = d_z.astype(BF16)
            dw_ref[...] += d_w

    return _pcall("gdn_out_bwd", body, (t // tm,),
                  [_row_spec(tm, GDN_V), _row_spec(tm, GDN_V, 3), _full_spec((1, GDN_DK)), _row_spec(tm, GDN_V, 0)],
                  [_row_spec(tm, GDN_V), _row_spec(tm, GDN_V, 3), _full_spec((1, GDN_DK))],
                  [_sds((t, GDN_V)), _sds((t, BIG), BF16), _sds((1, GDN_DK))])(o, proj, gn, d_ocat)


def _ssd_out_fn(y, xs, z, d_skip, w):
    return _rms((y + d_skip * xs) * _silu(z), w)


_SSD_GW = SSD_D // SSD_GROUPS


def _ssd_out_fwd(y, xbc, proj, d_skip, nw, ocat):
    t = y.shape[0]
    tm = _row_tile(t)

    def body(y_ref, x_ref, z_ref, d_ref, w_ref, _, o_ref):
        for gi in range(SSD_GROUPS):
            sl = slice(gi * _SSD_GW, (gi + 1) * _SSD_GW)
            o_ref[:, sl] = _ssd_out_fn(y_ref[:, sl], x_ref[:, sl], z_ref[:, sl].astype(F32), d_ref[:, sl], w_ref[:, sl]).astype(BF16)

    pspec = _full_spec((1, SSD_D))
    return _pcall("ssd_out_fwd", body, (t // tm,),
                  [_row_spec(tm, SSD_D), _row_spec(tm, SSD_D, 0), _row_spec(tm, SSD_D, 4), pspec, pspec, _ANY],
                  _row_spec(tm, SSD_D, 1), _sds(ocat.shape, BF16), sem=("parallel",), aliases={5: 0})(
                      y, xbc, proj, d_skip, nw, ocat)


def _ssd_out_bwd(y, xbc, proj, d_skip, nw, d_ocat, d_proj):
    t = y.shape[0]
    tm = _row_tile(t)

    def body(y_ref, x_ref, z_ref, d_ref, w_ref, do_ref, _, dy_ref, dx_ref, dz_ref, dd_ref, dw_ref):
        _zero_at_first([dd_ref, dw_ref], pl.program_id(0) == 0)
        for gi in range(SSD_GROUPS):
            sl = slice(gi * _SSD_GW, (gi + 1) * _SSD_GW)
            _, vjp = jax.vjp(_ssd_out_fn, y_ref[:, sl], x_ref[:, sl], z_ref[:, sl].astype(F32), d_ref[:, sl], w_ref[:, sl])
            d_y, d_x, d_z, d_d, d_w = vjp(do_ref[:, sl])
            dy_ref[:, sl] = d_y
            dx_ref[:, sl] = d_x
            dz_ref[:, sl] = d_z.astype(BF16)
            dd_ref[:, sl] += d_d
            dw_ref[:, sl] += d_w

    pspec = _full_spec((1, SSD_D))
    row = _row_spec(tm, SSD_D)
    return _pcall("ssd_out_bwd", body, (t // tm,),
                  [row, _row_spec(tm, SSD_D, 0), _row_spec(tm, SSD_D, 4), pspec, pspec, _row_spec(tm, SSD_D, 1), _ANY],
                  [row, row, _row_spec(tm, SSD_D, 4), pspec, pspec],
                  [_sds((t, SSD_D)), _sds((t, SSD_D)), _sds((t, BIG), BF16), _sds((1, SSD_D)), _sds((1, SSD_D))],
                  aliases={6: 2})(y, xbc, proj, d_skip, nw, d_ocat, d_proj)


def _res1_fn(x, mix, g_pm, g_pf):
    x1 = x + _rms(mix, g_pm)
    return x1, _rms(x1, g_pf)


def _res1_fwd(x, mix, g_pm, g_pf):
    t = x.shape[0]
    tm = _row_tile(t)

    def body(x_ref, m_ref, a_ref, b_ref, x1_ref, h2_ref):
        x1, h2 = _res1_fn(x_ref[...], m_ref[...], a_ref[...], b_ref[...])
        x1_ref[...] = x1
        h2_ref[...] = h2.astype(BF16)

    row, pspec = _row_spec(tm, D_MODEL), _full_spec((1, D_MODEL))
    return _pcall("res1_fwd", body, (t // tm,), [row, row, pspec, pspec], [row, row],
                  [_sds((t, D_MODEL)), _sds((t, D_MODEL), BF16)], sem=("parallel",))(x, mix, g_pm, g_pf)


def _res1_bwd(x, mix, g_pm, g_pf, d_x1, d_h2):
    t = x.shape[0]
    tm = _row_tile(t)

    def body(x_ref, m_ref, a_ref, b_ref, c1_ref, c2_ref, dx_ref, dm_ref, da_ref, db_ref):
        _zero_at_first([da_ref, db_ref], pl.program_id(0) == 0)
        _, vjp = jax.vjp(_res1_fn, x_ref[...], m_ref[...], a_ref[...], b_ref[...])
        d_x, d_m, d_a, d_b = vjp((c1_ref[...], c2_ref[...]))
        dx_ref[...] = d_x
        dm_ref[...] = d_m.astype(BF16)
        da_ref[...] += d_a
        db_ref[...] += d_b

    row, pspec = _row_spec(tm, D_MODEL), _full_spec((1, D_MODEL))
    return _pcall("res1_bwd", body, (t // tm,), [row, row, pspec, pspec, row, row], [row, row, pspec, pspec],
                  [_sds((t, D_MODEL)), _sds((t, D_MODEL), BF16), _sds((1, D_MODEL)), _sds((1, D_MODEL))])(
                      x, mix, g_pm, g_pf, d_x1, d_h2)


def _final_fn(x1, f, g_po, tgt):
    err = x1 + _rms(f, g_po) - tgt
    return 0.5 * jnp.sum(jnp.mean(err * err, axis=-1))


def _final(x1, f, g_po, tgt):
    t = x1.shape[0]
    tm = _row_tile(t)

    def body(x_ref, f_ref, g_ref, t_ref, loss_ref, dx_ref, df_ref, dg_ref):
        _zero_at_first([loss_ref, dg_ref], pl.program_id(0) == 0)
        loss, (d_x, d_f, d_g) = jax.value_and_grad(_final_fn, argnums=(0, 1, 2))(x_ref[...], f_ref[...], g_ref[...], t_ref[...])
        loss_ref[...] += jnp.broadcast_to(loss, loss_ref.shape)
        dx_ref[...] = d_x
        df_ref[...] = d_f.astype(BF16)
        dg_ref[...] += d_g

    row, pspec = _row_spec(tm, D_MODEL), _full_spec((1, D_MODEL))
    return _pcall("final", body, (t // tm,), [row, row, pspec, row], [_full_spec((8, LANE)), row, row, pspec],
                  [_sds((8, LANE)), _sds((t, D_MODEL)), _sds((t, D_MODEL), BF16), _sds((1, D_MODEL))])(x1, f, g_po, tgt)


def _rms1_bwd(x, g, d_h_a, d_h_b, d_x1):
    t = x.shape[0]
    tm = _row_tile(t)

    def body(x_ref, g_ref, dha_ref, dhb_ref, dx1_ref, dx_ref, dg_ref):
        _zero_at_first([dg_ref], pl.program_id(0) == 0)
        _, vjp = jax.vjp(_rms, x_ref[...], g_ref[...])
        d_x, d_g = vjp(dha_ref[...] + dhb_ref[...])
        dx_ref[...] = d_x + dx1_ref[...]
        dg_ref[...] += d_g

    row, pspec = _row_spec(tm, D_MODEL), _full_spec((1, D_MODEL))
    return _pcall("rms1_bwd", body, (t // tm,), [row, pspec, row, row, row], [row, pspec],
                  [_sds((t, D_MODEL)), _sds((1, D_MODEL))])(x, g, d_h_a, d_h_b, d_x1)


def _qkv_fn(mode):
    def fn(x, *wrows):
        y = _silu(_conv(x, wrows))
        if mode == "q":
            return _l2n(y) * (GDN_DK ** -0.5)
        if mode == "k":
            return _l2n(y)
        return y
    return fn


def _seq_spec(s, tc, off):
    return pl.BlockSpec((s, tc), lambda j, b, _o=off: (b, _o + j))


def _par_spec(rows, tc, off):
    return pl.BlockSpec((rows, tc), lambda j, b, _o=off: (0, _o + j))


def _gdn_conv_fwd(mode, proj, w, bsz, s):
    off = {"q": 0, "k": GDN_HEADS, "v": 2 * GDN_HEADS}[mode]
    fn = _qkv_fn(mode)

    def body(x_ref, w_ref, y_ref):
        y_ref[...] = fn(x_ref[...].astype(F32), *[w_ref[k:k + 1, :] for k in range(CONV_K)])

    return _pcall("gdn_conv_fwd_" + mode, body, (GDN_HEADS, bsz),
                  [_seq_spec(s, GDN_DK, off), _par_spec(CONV_K, GDN_DK, off)], _seq_spec(s, GDN_DK, 0),
                  _sds((bsz * s, GDN_QK)), sem=("parallel", "parallel"))(proj, w)


def _gdn_conv_bwd(mode, proj, w, d_y, d_proj, bsz, s):
    off = {"q": 0, "k": GDN_HEADS, "v": 2 * GDN_HEADS}[mode]
    fn = _qkv_fn(mode)

    def body(x_ref, w_ref, dy_ref, _, dx_ref, dw_ref):
        _zero_at_first([dw_ref], pl.program_id(1) == 0)
        _, vjp = jax.vjp(fn, x_ref[...].astype(F32), *[w_ref[k:k + 1, :] for k in range(CONV_K)])
        grads = vjp(dy_ref[...])
        dx_ref[...] = grads[0].astype(BF16)
        for k in range(CONV_K):
            dw_ref[k:k + 1, :] += grads[1 + k]

    return _pcall("gdn_conv_bwd_" + mode, body, (GDN_HEADS, bsz),
                  [_seq_spec(s, GDN_DK, off), _par_spec(CONV_K, GDN_DK, off), _seq_spec(s, GDN_DK, 0), _ANY],
                  [_seq_spec(s, GDN_DK, off), _par_spec(CONV_K, GDN_DK, 0)],
                  [_sds(d_proj.shape, BF16), _sds((CONV_K, GDN_QK))], sem=("parallel", "arbitrary"), aliases={3: 0})(
                      proj, w, d_y, d_proj)


def _ssd_conv_fn(x, bias, *wrows):
    return _silu(_conv(x, wrows) + bias)


_XBC_OFF = (5 * 1024) // LANE


def _ssd_conv_fwd(proj, w, bias, bsz, s):
    nt_ = SSD_CONV_CH // LANE

    def body(x_ref, w_ref, b_ref, y_ref):
        y_ref[...] = _ssd_conv_fn(x_ref[...].astype(F32), b_ref[...], *[w_ref[k:k + 1, :] for k in range(CONV_K)])

    return _pcall("ssd_conv_fwd", body, (nt_, bsz),
                  [_seq_spec(s, LANE, _XBC_OFF), _par_spec(CONV_K, LANE, 0), _par_spec(1, LANE, 0)], _seq_spec(s, LANE, 0),
                  _sds((bsz * s, SSD_CONV_CH)), sem=("parallel", "parallel"))(proj, w, bias)


def _ssd_conv_bwd(proj, w, bias, d_y, d_proj, bsz, s):
    nt_ = SSD_CONV_CH // LANE

    def body(x_ref, w_ref, b_ref, dy_ref, _, dx_ref, dw_ref, db_ref):
        _zero_at_first([dw_ref, db_ref], pl.program_id(1) == 0)
        _, vjp = jax.vjp(_ssd_conv_fn, x_ref[...].astype(F32), b_ref[...], *[w_ref[k:k + 1, :] for k in range(CONV_K)])
        grads = vjp(dy_ref[...])
        dx_ref[...] = grads[0].astype(BF16)
        db_ref[...] += grads[1]
        for k in range(CONV_K):
            dw_ref[k:k + 1, :] += grads[2 + k]

    return _pcall("ssd_conv_bwd", body, (nt_, bsz),
                  [_seq_spec(s, LANE, _XBC_OFF), _par_spec(CONV_K, LANE, 0), _par_spec(1, LANE, 0), _seq_spec(s, LANE, 0), _ANY],
                  [_seq_spec(s, LANE, _XBC_OFF), _par_spec(CONV_K, LANE, 0), _par_spec(1, LANE, 0)],
                  [_sds(d_proj.shape, BF16), _sds((CONV_K, SSD_CONV_CH)), _sds((1, SSD_CONV_CH))],
                  sem=("parallel", "arbitrary"), aliases={4: 0})(proj, w, bias, d_y, d_proj)


_FFN_TC = 256
_FFN_NT = D_FF // _FFN_TC


def _ffn_act_fn(xg, xu, bg, bu, *wrows):
    k_w = FFN_CONV_K
    gate = _conv(xg, wrows[:k_w]) + bg
    up = _conv(xu, wrows[k_w:]) + bu
    return _silu(gate) * up


def _ffn_act_fwd(u_pre, w, bias, bsz, s):
    def body(xg_ref, xu_ref, wg_ref, wu_ref, bg_ref, bu_ref, a_ref):
        rows = [wg_ref[k:k + 1, :] for k in range(FFN_CONV_K)] + [wu_ref[k:k + 1, :] for k in range(FFN_CONV_K)]
        a_ref[...] = _ffn_act_fn(xg_ref[...].astype(F32), xu_ref[...].astype(F32), bg_ref[...], bu_ref[...], *rows).astype(BF16)

    return _pcall("ffn_act_fwd", body, (_FFN_NT, bsz),
                  [_seq_spec(s, _FFN_TC, 0), _seq_spec(s, _FFN_TC, _FFN_NT),
                   _par_spec(FFN_CONV_K, _FFN_TC, 0), _par_spec(FFN_CONV_K, _FFN_TC, _FFN_NT),
                   _par_spec(1, _FFN_TC, 0), _par_spec(1, _FFN_TC, _FFN_NT)],
                  _seq_spec(s, _FFN_TC, 0), _sds((bsz * s, D_FF), BF16), sem=("parallel", "parallel"))(
                      u_pre, u_pre, w, w, bias, bias)


def _ffn_act_bwd(u_pre, w, bias, d_a, bsz, s):
    def body(xg_ref, xu_ref, wg_ref, wu_ref, bg_ref, bu_ref, da_ref, dg_ref, du_ref, dwg_ref, dwu_ref, dbg_ref, dbu_ref):
        _zero_at_first([dwg_ref, dwu_ref, dbg_ref, dbu_ref], pl.program_id(1) == 0)
        rows = [wg_ref[k:k + 1, :] for k in range(FFN_CONV_K)] + [wu_ref[k:k + 1, :] for k in range(FFN_CONV_K)]
        _, vjp = jax.vjp(_ffn_act_fn, xg_ref[...].astype(F32), xu_ref[...].astype(F32), bg_ref[...], bu_ref[...], *rows)
        grads = vjp(da_ref[...])
        dg_ref[...] = grads[0].astype(BF16)
        du_ref[...] = grads[1].astype(BF16)
        dbg_ref[...] += grads[2]
        dbu_ref[...] += grads[3]
        for k in range(FFN_CONV_K):
            dwg_ref[k:k + 1, :] += grads[4 + k]
            dwu_ref[k:k + 1, :] += grads[4 + FFN_CONV_K + k]

    seq0, par3, par1 = _seq_spec(s, _FFN_TC, 0), _par_spec(FFN_CONV_K, _FFN_TC, 0), _par_spec(1, _FFN_TC, 0)
    return _pcall("ffn_act_bwd", body, (_FFN_NT, bsz),
                  [seq0, _seq_spec(s, _FFN_TC, _FFN_NT), par3, _par_spec(FFN_CONV_K, _FFN_TC, _FFN_NT),
                   par1, _par_spec(1, _FFN_TC, _FFN_NT), seq0],
                  [seq0, seq0, par3, par3, par1, par1],
                  [_sds((bsz * s, D_FF), BF16), _sds((bsz * s, D_FF), BF16), _sds((FFN_CONV_K, D_FF)), _sds((FFN_CONV_K, D_FF)),
                   _sds((1, D_FF)), _sds((1, D_FF))], sem=("parallel", "arbitrary"))(u_pre, u_pre, w, w, bias, bias, d_a)


_GP = GDN_HEADS // 2
_SP = SSD_HEADS // 2


def _pair_lanes(p):
    return slice(2 * p * LANE, (2 * p + 1) * LANE), slice((2 * p + 1) * LANE, (2 * p + 2) * LANE)


_LAST = slice(CHUNK - 1, CHUNK)


def _gdn_args(p, q_ref, k_ref, v_ref, g_ref, b_ref, gr_ref):
    la, lb = _pair_lanes(p)
    return (q_ref[:, la], q_ref[:, lb], k_ref[:, la], k_ref[:, lb], v_ref[:, la], v_ref[:, lb], g_ref[:, la], g_ref[:, lb],
            b_ref[:, la], b_ref[:, lb], gr_ref[p], g_ref[_LAST, la], g_ref[_LAST, lb])


def _gdn_fwd(q, k, v, act, cum, gc_row, bsz, n):
    def body(q_ref, k_ref, v_ref, act_ref, cum_ref, gr_ref, o_ref, sin_ref, s_scr, g_ref, b_ref):
        _zero_at_first([s_scr], pl.program_id(1) == 0)
        _expand_lanes(act_ref, b_ref, 0, GDN_HEADS, GDN_DK)
        _expand_lanes(cum_ref, g_ref, _G_LO, GDN_HEADS, GDN_DK)
        flat = []
        for p in range(_GP):
            flat += [*_gdn_args(p, q_ref, k_ref, v_ref, g_ref, b_ref, gr_ref), s_scr[2 * p], s_scr[2 * p + 1]]
        sin_ref[...] = s_scr[...]
        outs = _gdn_multi(*flat)
        for p in range(_GP):
            la, lb = _pair_lanes(p)
            o_ref[:, la], o_ref[:, lb], s_scr[2 * p], s_scr[2 * p + 1] = outs[4 * p:4 * p + 4]

    tspec = pl.BlockSpec((CHUNK, GDN_V), lambda b, c: (b * n + c, 0))
    rspec = pl.BlockSpec((_GP, 1, LANE), lambda b, c: (b * n + c, 0, 0))
    sspec = pl.BlockSpec((GDN_HEADS, LANE, LANE), lambda b, c: (b * n + c, 0, 0))
    nspec = pl.BlockSpec((CHUNK, SMALL), lambda b, c: (b * n + c, 0))
    wide = pltpu.VMEM((CHUNK, GDN_V), F32)
    return _pcall("gdn_fwd", body, (bsz, n), [tspec] * 3 + [nspec, nspec, rspec], [tspec, sspec],
                  [_sds((bsz * n * CHUNK, GDN_V)), _sds((bsz * n * GDN_HEADS, LANE, LANE))],
                  scratch=[pltpu.VMEM((GDN_HEADS, LANE, LANE), F32), wide, wide], sem=("parallel", "arbitrary"))(
                      q, k, v, act, cum, gc_row)


def _gdn_bwd(q, k, v, act, cum, gc_row, s_in, d_o, bsz, n, scatter):
    ns = len(scatter)

    def body(q_ref, k_ref, v_ref, act_ref, cum_ref, gr_ref, sin_ref, do_ref, *rest):
        sc_in, rest = rest[:ns], rest[ns:]
        dq_ref, dk_ref, dv_ref, dact_ref, dcum_ref, dgr_ref = rest[:6]
        sc_out, rest = rest[6:6 + ns], rest[6 + ns:]
        ds_scr, g_ref, b_ref, dg_ref, db_ref, send_sems, recv_sems = rest
        step = pl.program_id(0) * n + pl.program_id(1)
        sc_start, sc_finish = _scatter_ops(sc_in, sc_out, (send_sems, recv_sems))
        pl.when(step == 0)(sc_start)
        _zero_at_first([ds_scr], pl.program_id(1) == 0)
        _expand_lanes(act_ref, b_ref, 0, GDN_HEADS, GDN_DK)
        _expand_lanes(cum_ref, g_ref, _G_LO, GDN_HEADS, GDN_DK)
        flat, cots = [], []
        for p in range(_GP):
            la, lb = _pair_lanes(p)
            flat += [*_gdn_args(p, q_ref, k_ref, v_ref, g_ref, b_ref, gr_ref), sin_ref[2 * p], sin_ref[2 * p + 1]]
            cots += [do_ref[:, la], do_ref[:, lb], ds_scr[2 * p], ds_scr[2 * p + 1]]
        _, vjp = jax.vjp(_gdn_multi, *flat)
        grads = vjp(tuple(cots))
        for p in range(_GP):
            la, lb = _pair_lanes(p)
            cts = grads[_GDN_NARGS * p:_GDN_NARGS * (p + 1)]
            for ref, i in ((dq_ref, 0), (dk_ref, 2), (dv_ref, 4), (dg_ref, 6), (db_ref, 8)):
                ref[:, la] = cts[i]
                ref[:, lb] = cts[i + 1]
            dgr_ref[p] = cts[10]
            dg_ref[_LAST, la] += cts[11]
            dg_ref[_LAST, lb] += cts[12]
            ds_scr[2 * p] = cts[13]
            ds_scr[2 * p + 1] = cts[14]
        dact_ref[...] = _reduce_lanes(db_ref, 0, GDN_HEADS, GDN_DK)
        dcum_ref[...] = _reduce_lanes(dg_ref, _G_LO, GDN_HEADS, GDN_DK)
        pl.when(step == bsz * n - 1)(sc_finish)

    tspec = pl.BlockSpec((CHUNK, GDN_V), lambda b, c: (b * n + (n - 1 - c), 0))
    nspec = pl.BlockSpec((CHUNK, SMALL), lambda b, c: (b * n + (n - 1 - c), 0))
    rspec = pl.BlockSpec((_GP, 1, LANE), lambda b, c: (b * n + (n - 1 - c), 0, 0))
    sspec = pl.BlockSpec((GDN_HEADS, LANE, LANE), lambda b, c: (b * n + (n - 1 - c), 0, 0))
    tok_shape, nar_shape = _sds((bsz * n * CHUNK, GDN_V)), _sds((bsz * n * CHUNK, SMALL))
    wide = pltpu.VMEM((CHUNK, GDN_V), F32)
    outs = pl.pallas_call(
        body, name="gdn_bwd", grid=(bsz, n),
        in_specs=[tspec] * 3 + [nspec, nspec, rspec, sspec, tspec] + [_ANY] * ns,
        out_specs=[tspec] * 3 + [nspec, nspec, rspec] + [_ANY] * ns,
        out_shape=[tok_shape] * 3 + [nar_shape, nar_shape, _sds((bsz * n * _GP, 1, LANE))]
        + [_sds((3,) + a.shape[1:], a.dtype) for a in scatter],
        scratch_shapes=[pltpu.VMEM((GDN_HEADS, LANE, LANE), F32), wide, wide, wide, wide] + _scatter_sems(ns),
        compiler_params=pltpu.CompilerParams(dimension_semantics=("arbitrary", "arbitrary"), vmem_limit_bytes=VMEM_LIMIT,
                                             has_side_effects=True))(q, k, v, act, cum, gc_row, s_in, d_o, *scatter)
    return outs[:6], outs[6:]


_B_OFF = SSD_D // LANE
_C_OFF = (SSD_D + SSD_BC) // LANE
_PPG = _SP // SSD_GROUPS


def _ssd_args(p, x_ref, dt_ref, a_ref, ar_ref):
    lp = slice(p * LANE, (p + 1) * LANE)
    gi = p // _PPG
    b_sl = slice((_B_OFF + gi) * LANE, (_B_OFF + gi + 1) * LANE)
    c_sl = slice((_C_OFF + gi) * LANE, (_C_OFF + gi + 1) * LANE)
    return (x_ref[:, lp], dt_ref[:, lp], a_ref[:, lp], ar_ref[p], a_ref[_LAST, lp], x_ref[:, b_sl], x_ref[:, c_sl])


def _ssd_fwd(xbc, dt, acs, acs_row, bsz, n):
    def body(x_ref, dt_ref, a_ref, ar_ref, y_ref, sin_ref, s_scr):
        _zero_at_first([s_scr], pl.program_id(1) == 0)
        flat = []
        for p in range(_SP):
            flat += [*_ssd_args(p, x_ref, dt_ref, a_ref, ar_ref), s_scr[p]]
        sin_ref[...] = s_scr[...]
        outs = _ssd_multi(*flat)
        for p in range(_SP):
            y_ref[:, p * LANE:(p + 1) * LANE], s_scr[p] = outs[2 * p:2 * p + 2]

    tspec = pl.BlockSpec((CHUNK, SSD_D), lambda b, c: (b * n + c, 0))
    return _pcall("ssd_fwd", body, (bsz, n),
                  [pl.BlockSpec((CHUNK, SSD_CONV_CH), lambda b, c: (b * n + c, 0)), tspec, tspec,
                   pl.BlockSpec((_SP, 1, LANE), lambda b, c: (b * n + c, 0, 0))],
                  [tspec, pl.BlockSpec((_SP, LANE, LANE), lambda b, c: (b * n + c, 0, 0))],
                  [_sds((bsz * n * CHUNK, SSD_D)), _sds((bsz * n * _SP, LANE, LANE))],
                  scratch=[pltpu.VMEM((_SP, LANE, LANE), F32)], sem=("parallel", "arbitrary"))(xbc, dt, acs, acs_row)


def _ssd_bwd(xbc, dt, acs, acs_row, s_in, d_y, d_x_skip, bsz, n):
    def body(x_ref, dt_ref, a_ref, ar_ref, sin_ref, dy_ref, dsk_ref, dx_ref, ddt_ref, da_ref, dar_ref, ds_scr):
        _zero_at_first([ds_scr], pl.program_id(1) == 0)
        d_b = [None] * SSD_GROUPS
        d_c = [None] * SSD_GROUPS
        flat, cots = [], []
        for p in range(_SP):
            flat += [*_ssd_args(p, x_ref, dt_ref, a_ref, ar_ref), sin_ref[p]]
            cots += [dy_ref[:, p * LANE:(p + 1) * LANE], ds_scr[p]]
        _, vjp = jax.vjp(_ssd_multi, *flat)
        grads = vjp(tuple(cots))
        for p in range(_SP):
            lp = slice(p * LANE, (p + 1) * LANE)
            gi = p // _PPG
            cts = grads[_SSD_NARGS * p:_SSD_NARGS * (p + 1)]
            dx_ref[:, lp] = cts[0] + dsk_ref[:, lp]
            ddt_ref[:, lp] = cts[1]
            da_ref[:, lp] = cts[2]
            dar_ref[p] = cts[3]
            da_ref[_LAST, lp] += cts[4]
            d_b[gi] = cts[5] if d_b[gi] is None else d_b[gi] + cts[5]
            d_c[gi] = cts[6] if d_c[gi] is None else d_c[gi] + cts[6]
            ds_scr[p] = cts[7]
        for gi in range(SSD_GROUPS):
            dx_ref[:, (_B_OFF + gi) * LANE:(_B_OFF + gi + 1) * LANE] = d_b[gi]
            dx_ref[:, (_C_OFF + gi) * LANE:(_C_OFF + gi + 1) * LANE] = d_c[gi]

    def rev(b, c):
        return b * n + (n - 1 - c)

    tspec = pl.BlockSpec((CHUNK, SSD_D), lambda b, c: (rev(b, c), 0))
    xspec = pl.BlockSpec((CHUNK, SSD_CONV_CH), lambda b, c: (rev(b, c), 0))
    rspec = pl.BlockSpec((_SP, 1, LANE), lambda b, c: (rev(b, c), 0, 0))
    tok_shape = _sds((bsz * n * CHUNK, SSD_D))
    return _pcall("ssd_bwd", body, (bsz, n),
                  [xspec, tspec, tspec, rspec, pl.BlockSpec((_SP, LANE, LANE), lambda b, c: (rev(b, c), 0, 0)), tspec, tspec],
                  [xspec, tspec, tspec, rspec],
                  [_sds((bsz * n * CHUNK, SSD_CONV_CH)), tok_shape, tok_shape, _sds((bsz * n * _SP, 1, LANE))],
                  scratch=[pltpu.VMEM((_SP, LANE, LANE), F32)], sem=("parallel", "arbitrary"))(
                      xbc, dt, acs, acs_row, s_in, d_y, d_x_skip)


def _add2(name, a, b):
    t, c = a.shape
    tm = _row_tile(t)

    def body(a_ref, b_ref, o_ref):
        o_ref[...] = a_ref[...] + b_ref[...]

    return _pcall(name, body, (t // tm,), [_row_spec(tm, c)] * 2, _row_spec(tm, c), _sds((t, c)), sem=("parallel",))(a, b)


def _rep(p, width):
    return jnp.repeat(p.reshape(-1), width).reshape(1, -1)


def _to_rows(narrow, lo, hi, bsz, n):
    heads = hi - lo
    a = narrow[:, lo:hi].reshape(bsz, n, CHUNK, heads)
    return jnp.transpose(a, (0, 1, 3, 2)).reshape(bsz * n * (heads // 2), 1, 2 * CHUNK)


def _from_rows(rows, heads, bsz, n):
    return jnp.transpose(rows.reshape(bsz, n, heads, CHUNK), (0, 1, 3, 2)).reshape(bsz * n * CHUNK, heads)


def _narrow_row(gdn_part, ssd_part):
    return jnp.pad(jnp.concatenate([gdn_part, ssd_part], axis=1), ((0, 0), (_G_LO, SMALL - _S_HI)))


def _local_step(x, tgt, p):
    bsz, s, _ = x.shape
    t = bsz * s
    n = s // CHUNK
    x2 = x.reshape(t, D_MODEL)
    tgt2 = tgt.reshape(t, D_MODEL)
    w_in = p["w_in"]
    w_big = jnp.concatenate([w_in[:, :4096], w_in[:, 4112:6672]], axis=1)
    w_small = jnp.concatenate([w_in[:, 4096:4112], w_in[:, 6672:6688], jnp.zeros((D_MODEL, SMALL - 32), BF16)], axis=1)
    gate_bias = _narrow_row(p["gdn_dt_bias"], p["ssd_dt_bias"])
    gate_a_log = _narrow_row(p["gdn_a_log"], p["ssd_a_log"])
    d_skip = _rep(p["ssd_d"], SSD_HEADDIM)

    h = _rms_fwd("rms0_fwd", x2, p["pre_mix_norm"])
    proj, g_out, g_up, g_down = _matmul_nn_gathering(
        "mm_in_big", h, w_big, BF16, (_pick(t, (1024, 512, 256, 128, 64)), BIG // 2), [p["w_out"], p["w_up"], p["w_down"]])
    w_out, w_up, w_down = g_out.reshape(-1, D_MODEL), _chips_to_cols(g_up), g_down.reshape(-1, D_MODEL)
    small = _matmul("mm_in_small", h, w_small, "nn", F32, (1024, 128))
    gact, cum, dt, acs = _gates_fwd(small, gate_bias, gate_a_log)
    gc_row = _to_rows(cum, _G_LO, _G_HI, bsz, n)
    acs_row = _to_rows(cum, _S_LO, _S_HI, bsz, n)
    q = _gdn_conv_fwd("q", proj, p["gdn_conv_w"], bsz, s)
    k = _gdn_conv_fwd("k", proj, p["gdn_conv_w"], bsz, s)
    v = _gdn_conv_fwd("v", proj, p["gdn_conv_w"], bsz, s)
    o, gdn_s = _gdn_fwd(q, k, v, gact, cum, gc_row, bsz, n)
    ocat = _gdn_out_fwd(o, proj, p["gdn_norm_w"])
    xbc = _ssd_conv_fwd(proj, p["ssd_conv_w"], p["ssd_conv_b"], bsz, s)
    y, ssd_s = _ssd_fwd(xbc, dt, acs, acs_row, bsz, n)
    ocat = _ssd_out_fwd(y, xbc, proj, d_skip, p["ssd_norm_w"], ocat)
    mix = _matmul("mm_out", ocat, w_out, "nn", F32, (1024, 1024))
    x1, h2 = _res1_fwd(x2, mix, p["post_mix_norm"], p["pre_ffn_norm"])
    u_pre = _matmul("mm_up", h2, w_up, "nn", BF16, (1024, 2816))
    act = _ffn_act_fwd(u_pre, p["ffn_conv_w"], p["ffn_conv_b"], bsz, s)
    f = _matmul("mm_down", act, w_down, "nn", F32, (1024, 1024))
    loss_acc, d_out, d_f, g_post_ffn = _final(x1, f, p["post_ffn_norm"], tgt2)

    grads = {"post_ffn_norm": g_post_ffn}
    d_act = _matmul("mm_down_dx", d_f, w_down, "nt", F32, (1024, 2816))
    dw_down = _matmul("mm_down_dw", act, d_f, "tn", F32, (2816, 1024, 1024))
    d_gate, d_up, dwg, dwu, dbg, dbu = _ffn_act_bwd(u_pre, p["ffn_conv_w"], p["ffn_conv_b"], d_act, bsz, s)
    grads["ffn_conv_w"] = jnp.concatenate([dwg, dwu], axis=1)
    grads["ffn_conv_b"] = jnp.concatenate([dbg, dbu], axis=1)
    d_h2 = _matmul_nt_split("mm_up_dx", d_gate, d_up, w_up, 512)
    dw_up = _matmul("mm_up_dw_gate", h2, d_gate, "tn", F32, (1024, 2816, 1024), part=(2 * D_FF, 0, None))
    dw_up = _matmul("mm_up_dw_up", h2, d_up, "tn", F32, (1024, 2816, 1024), part=(2 * D_FF, D_FF, dw_up))
    d_x1, d_mix, grads["post_mix_norm"], grads["pre_ffn_norm"] = _res1_bwd(
        x2, mix, p["post_mix_norm"], p["pre_ffn_norm"], d_out, d_h2)
    d_ocat = _matmul("mm_out_dx", d_mix, w_out, "nt", F32, (1024, 2048))
    dw_out = _matmul("mm_out_dw", ocat, d_mix, "tn", F32, (2048, 1024, 1024))

    early = _pair_sums("early", [dw_out.reshape(N_CHIPS, -1, D_MODEL), _cols_to_chips(dw_up), dw_down.reshape(N_CHIPS, -1, D_MODEL)])

    d_o, d_proj, grads["gdn_norm_w"] = _gdn_out_bwd(o, proj, p["gdn_norm_w"], d_ocat)
    (d_q, d_k, d_v, d_act_g, d_cum_g, d_gc_row), early_landed = _gdn_bwd(
        q, k, v, gact, cum, gc_row, gdn_s, d_o, bsz, n, [ps[1] for ps in early])
    d_proj, dwq = _gdn_conv_bwd("q", proj, p["gdn_conv_w"], d_q, d_proj, bsz, s)
    d_proj, dwk = _gdn_conv_bwd("k", proj, p["gdn_conv_w"], d_k, d_proj, bsz, s)
    d_proj, dwv = _gdn_conv_bwd("v", proj, p["gdn_conv_w"], d_v, d_proj, bsz, s)
    grads["gdn_conv_w"] = jnp.concatenate([dwq, dwk, dwv], axis=1)

    d_y, d_xs_skip, d_proj, d_dskip, grads["ssd_norm_w"] = _ssd_out_bwd(y, xbc, proj, d_skip, p["ssd_norm_w"], d_ocat, d_proj)
    d_xbc, d_dt, d_acs, d_acs_row = _ssd_bwd(xbc, dt, acs, acs_row, ssd_s, d_y, d_xs_skip, bsz, n)
    d_proj, grads["ssd_conv_w"], grads["ssd_conv_b"] = _ssd_conv_bwd(proj, p["ssd_conv_w"], p["ssd_conv_b"], d_xbc, d_proj, bsz, s)

    d_cum_rows = jnp.concatenate([jnp.zeros((t, _G_LO), F32), _from_rows(d_gc_row, GDN_HEADS, bsz, n),
                                  _from_rows(d_acs_row, SSD_HEADS, bsz, n), jnp.zeros((t, SMALL - _S_HI), F32)], axis=1)
    d_small, d_gate_bias, d_gate_a_log = _gates_bwd(small, gate_bias, gate_a_log, d_act_g, [d_cum_g, d_cum_rows], d_dt, d_acs)
    grads["gdn_dt_bias"], grads["ssd_dt_bias"] = d_gate_bias[:, _G_LO:_G_HI], d_gate_bias[:, _S_LO:_S_HI]
    grads["gdn_a_log"], grads["ssd_a_log"] = d_gate_a_log[:, _G_LO:_G_HI], d_gate_a_log[:, _S_LO:_S_HI]
    d_h_big = _matmul("mm_in_big_dx", d_proj, w_big, "nt", F32, (512, 1024))
    d_h_small = _matmul("mm_in_small_dx", d_small, w_small, "nt", F32, (1024, 1024))
    dw_big = _matmul("mm_in_big_dw", h, d_proj, "tn", F32, (1024, 3328, 1024))
    dw_small = _matmul("mm_in_small_dw", h, d_small, "tn", F32)
    grads["w_in"] = jnp.concatenate([dw_big[:, :4096], dw_small[:, :16], dw_big[:, 4096:], dw_small[:, 16:32]], axis=1)
    grad_x, grads["pre_mix_norm"] = _rms1_bwd(x2, p["pre_mix_norm"], d_h_big, d_h_small, d_x1)
    grads["ssd_d"] = _head_sums(d_dskip)[:1, :SSD_HEADS]
    return loss_acc, grad_x.reshape(bsz, s, D_MODEL), grads, (early, early_landed)


def _head_sums(wide):
    def body(x_ref, o_ref):
        r, c = _iota2((D_MODEL, SMALL))
        o_ref[...] = _mask_dot(jnp.broadcast_to(x_ref[...], (8, D_MODEL)), ((r >> 6) == c).astype(F32), NN, True)

    return _pcall("head_sums", body, (1,), [_full_spec((1, D_MODEL))], _full_spec((8, SMALL)), _sds((8, SMALL)))(wide)


def _adamw_fn(w, g, m, v):
    m = ADAM_B1 * m + (1.0 - ADAM_B1) * g
    v = ADAM_B2 * v + (1.0 - ADAM_B2) * (g * g)
    m_hat = m / (1.0 - ADAM_B1 ** ADAM_STEP)
    v_hat = v / (1.0 - ADAM_B2 ** ADAM_STEP)
    delta = -ADAM_LR * (m_hat / (jnp.sqrt(v_hat) + ADAM_EPS) + ADAM_WD * w)
    return delta, m, v


def _adamw(name, w, g, m, v):
    r, c = w.shape
    tr = _pick(r, (256, 176, 128, 64, 8))

    def body(w_ref, g_ref, m_ref, v_ref, d_ref, m2_ref, v2_ref):
        d, m2, v2 = _adamw_fn(w_ref[...], g_ref[...], m_ref[...], v_ref[...])
        d_ref[...] = d
        m2_ref[...] = m2
        v2_ref[...] = v2

    spec = pl.BlockSpec((tr, c), lambda i: (i, 0))
    return _pcall(name, body, (r // tr,), [spec] * 4, [spec] * 3, [_sds((r, c))] * 3, sem=("parallel",))(w, g, m, v)


_ANY = pl.BlockSpec(memory_space=pl.ANY)
_OTHER_CHIPS = ((1, 0), (0, 1), (1, 1))


def _coords():
    return lax.axis_index("x"), lax.axis_index("y"), lax.axis_index("c")


def _flip(v, f):
    return 1 - v if f else v


def _gather_ops(ins, outs, sems, split):
    send_sems, recv_sems, fwd_send_sems, fwd_recv_sems, own_send_sems, own_recv_sems = sems
    n = len(ins)
    x, y, c = _coords()
    me = 2 * x + y
    sib = (x, y, 1 - c)

    def rows(a, core):
        if not split[a]:
            return slice(None)
        half = ins[a].shape[0] // 2
        return pl.ds(core * half, half)

    def chip(j):
        fx, fy = _OTHER_CHIPS[j]
        return _flip(x, fx), _flip(y, fy)

    def own_cp(a):
        return pltpu.make_async_remote_copy(ins[a], outs[a].at[me], own_send_sems.at[a], own_recv_sems.at[a],
                                            device_id=sib, device_id_type=MESH)

    def ici_cp(a, j):
        return pltpu.make_async_remote_copy(ins[a].at[rows(a, c)], outs[a].at[me, rows(a, c)],
                                            send_sems.at[a * 3 + j], recv_sems.at[a * 3 + j],
                                            device_id=(*chip(j), c), device_id_type=MESH)

    def landed_cp(a, j, sem_a, sem_b, core, to):
        cx, cy = chip(j)
        blk = outs[a].at[2 * cx + cy, rows(a, core)]
        return pltpu.make_async_remote_copy(blk, blk, sem_a.at[a * 3 + j], sem_b.at[a * 3 + j], device_id=to, device_id_type=MESH)

    pairs = [(a, j) for a in range(n) for j in range(3)]

    def start():
        for a in range(n):
            own_cp(a).start()
        for a, j in pairs:
            ici_cp(a, j).start()

    def forward():
        for a, j in pairs:
            landed_cp(a, j, send_sems, recv_sems, c, (*chip(j), c)).wait_recv()
            if split[a]:
                landed_cp(a, j, fwd_send_sems, fwd_recv_sems, c, sib).start()

    def finish():
        for a, j in pairs:
            if split[a]:
                landed_cp(a, j, fwd_send_sems, fwd_recv_sems, 1 - c, sib).wait_recv()
        for a in range(n):
            own_cp(a).wait_recv()
        for a, j in pairs:
            ici_cp(a, j).wait_send()
            if split[a]:
                landed_cp(a, j, fwd_send_sems, fwd_recv_sems, c, sib).wait_send()
        for a in range(n):
            own_cp(a).wait_send()

    return start, forward, finish


def _gather_sems(n):
    return [pltpu.SemaphoreType.DMA((3 * n,))] * 4 + [pltpu.SemaphoreType.DMA((n,))] * 2


def _gather_chips(arrs, split):
    n = len(arrs)

    def body(*refs):
        start, forward, finish = _gather_ops(refs[:n], refs[n:2 * n], refs[2 * n:], split)
        start()
        forward()
        finish()

    return pl.pallas_call(
        body, name="gather_chips", out_shape=[_sds((N_CHIPS,) + a.shape, a.dtype) for a in arrs],
        in_specs=[_ANY] * n, out_specs=[_ANY] * n, scratch_shapes=_gather_sems(n),
        compiler_params=pltpu.CompilerParams(has_side_effects=True))(*arrs)


def _matmul_nn_gathering(name, a, b, out_dtype, tiles, shards):
    m, k = a.shape
    n = b.shape[1]
    tm, tn = tiles
    nc = _pick(tn, (512, 256, 128))
    ns = len(shards)
    gi, gj = m // tm, n // tn
    steps = gi * gj

    def body(a_ref, b_ref, *rest):
        ins, o_ref, outs, sems = rest[:ns], rest[ns], rest[ns + 1:2 * ns + 1], rest[2 * ns + 1:]
        step = pl.program_id(0) * gj + pl.program_id(1)
        start, forward, finish = _gather_ops(ins, outs, sems, [True] * ns)
        pl.when(step == 0)(start)
        for c0 in range(0, tn, nc):
            o_ref[:, c0:c0 + nc] = _bdot(a_ref[...], b_ref[:, c0:c0 + nc], NN).astype(o_ref.dtype)
        pl.when(step == steps - 2)(forward)
        pl.when(step == steps - 1)(finish)

    assert steps >= 2
    return pl.pallas_call(
        body, name=name, grid=(gi, gj),
        in_specs=[pl.BlockSpec((tm, k), lambda i, j: (i, 0)), pl.BlockSpec((k, tn), lambda i, j: (0, j))] + [_ANY] * ns,
        out_specs=[pl.BlockSpec((tm, tn), lambda i, j: (i, j))] + [_ANY] * ns,
        out_shape=[_sds((m, n), out_dtype)] + [_sds((N_CHIPS,) + s.shape, s.dtype) for s in shards],
        scratch_shapes=_gather_sems(ns),
        compiler_params=pltpu.CompilerParams(dimension_semantics=("arbitrary", "arbitrary"), vmem_limit_bytes=VMEM_LIMIT,
                                             has_side_effects=True))(a, b, *shards)


_PEERS = tuple((fx, fy, fc) for fx in (0, 1) for fy in (0, 1) for fc in (0, 1))[1:]


def _allreduce_small(x):
    r = x.shape[0]

    def body(x_ref, o_ref, buf, send_sems, recv_sems):
        cx, cy, cc = _coords()
        me = 4 * cx + 2 * cy + cc
        sends = []
        for j, (fx, fy, fc) in enumerate(_PEERS):
            cp = pltpu.make_async_remote_copy(x_ref, buf.at[me], send_sems.at[j], recv_sems.at[j],
                                              device_id=(_flip(cx, fx), _flip(cy, fy), _flip(cc, fc)), device_id_type=MESH)
            cp.start()
            sends.append(cp)
        buf[pl.ds(me, 1)] = x_ref[...][None]
        for j, (fx, fy, fc) in enumerate(_PEERS):
            src = 4 * _flip(cx, fx) + 2 * _flip(cy, fy) + _flip(cc, fc)
            pltpu.make_async_remote_copy(x_ref, buf.at[src], send_sems.at[j], recv_sems.at[j],
                                         device_id=(_flip(cx, fx), _flip(cy, fy), _flip(cc, fc)), device_id_type=MESH).wait_recv()
        for cp in sends:
            cp.wait_send()
        acc = buf[0]
        for d in range(1, N_DEV):
            acc = acc + buf[d]
        o_ref[...] = acc

    vm = pl.BlockSpec(memory_space=pltpu.VMEM)
    return pl.pallas_call(
        body, name="allreduce_small", out_shape=_sds((r, LANE)), in_specs=[vm], out_specs=vm,
        scratch_shapes=[pltpu.VMEM((N_DEV, r, LANE), F32), pltpu.SemaphoreType.DMA((7,)), pltpu.SemaphoreType.DMA((7,))],
        compiler_params=pltpu.CompilerParams(has_side_effects=True, vmem_limit_bytes=VMEM_LIMIT))(x)


def _pair_sums(tag, arrs):
    received = _pair_send_other_half("pair_reduce_send_" + tag, arrs)
    core = lax.axis_index("c").astype(jnp.int32).reshape(1)
    return [_pair_add("pair_add_%s_%d" % (tag, i), a, b, core) for i, (a, b) in enumerate(zip(arrs, received))]


def _pair_send_other_half(name, arrs):
    n = len(arrs)

    def body(*refs):
        ins, outs = refs[:n], refs[n:2 * n]
        send_sems, recv_sems = refs[2 * n:]
        x, y, c = _coords()
        sends = []
        for a in range(n):
            half = ins[a].shape[1] // 2
            cp = pltpu.make_async_remote_copy(ins[a].at[:, pl.ds((1 - c) * half, half), :], outs[a], send_sems.at[a], recv_sems.at[a],
                                              device_id=(x, y, 1 - c), device_id_type=MESH)
            cp.start()
            sends.append(cp)
        for cp in sends:
            cp.wait_recv()
        for cp in sends:
            cp.wait_send()

    return pl.pallas_call(
        body, name=name, out_shape=[_sds((a.shape[0], a.shape[1] // 2, a.shape[2]), a.dtype) for a in arrs],
        in_specs=[_ANY] * n, out_specs=[_ANY] * n, scratch_shapes=[pltpu.SemaphoreType.DMA((n,))] * 2,
        compiler_params=pltpu.CompilerParams(has_side_effects=True))(*arrs)


def _pair_fill(arrs):
    n = len(arrs)

    def body(*refs):
        bufs = refs[n:2 * n]
        send_sems, recv_sems = refs[2 * n:]
        x, y, c = _coords()
        sends = []
        for a in range(n):
            cp = pltpu.make_async_remote_copy(bufs[a].at[c], bufs[a].at[c], send_sems.at[a], recv_sems.at[a],
                                              device_id=(x, y, 1 - c), device_id_type=MESH)
            cp.start()
            sends.append(cp)
        for a in range(n):
            theirs = bufs[a].at[1 - c]
            pltpu.make_async_remote_copy(theirs, theirs, send_sems.at[a], recv_sems.at[a],
                                         device_id=(x, y, 1 - c), device_id_type=MESH).wait_recv()
        for cp in sends:
            cp.wait_send()

    return pl.pallas_call(
        body, name="pair_gather", out_shape=[_sds(a.shape, a.dtype) for a in arrs], in_specs=[_ANY] * n, out_specs=[_ANY] * n,
        scratch_shapes=[pltpu.SemaphoreType.DMA((n,))] * 2, input_output_aliases={a: a for a in range(n)},
        compiler_params=pltpu.CompilerParams(has_side_effects=True))(*arrs)


def _scatter_ops(ins, outs, sems):
    send_sems, recv_sems = sems
    x, y, c = _coords()

    def cp(a, j):
        fx, fy = _OTHER_CHIPS[j]
        to = 2 * _flip(x, fx) + _flip(y, fy)
        return pltpu.make_async_remote_copy(ins[a].at[to], outs[a].at[j], send_sems.at[a * 3 + j], recv_sems.at[a * 3 + j],
                                            device_id=(_flip(x, fx), _flip(y, fy), c), device_id_type=MESH)

    pairs = [(a, j) for a in range(len(ins)) for j in range(3)]

    def start():
        for a, j in pairs:
            cp(a, j).start()

    def finish():
        for a, j in pairs:
            cp(a, j).wait_recv()
        for a, j in pairs:
            cp(a, j).wait_send()

    return start, finish


def _scatter_sems(n):
    return [pltpu.SemaphoreType.DMA((3 * n,))] * 2


def _scatter_chips(arrs):
    n = len(arrs)

    def body(*refs):
        start, finish = _scatter_ops(refs[:n], refs[n:2 * n], refs[2 * n:])
        start()
        finish()

    return pl.pallas_call(
        body, name="scatter_chips", out_shape=[_sds((3,) + a.shape[1:], a.dtype) for a in arrs],
        in_specs=[_ANY] * n, out_specs=[_ANY] * n, scratch_shapes=_scatter_sems(n),
        compiler_params=pltpu.CompilerParams(has_side_effects=True))(*arrs)


def _pair_add(name, full, recv, core):
    _, r, c = full.shape
    half = r // 2
    tr = _pick(half, (256, 176, 128, 64, 8))
    nb = half // tr

    def body(c_ref, a_ref, b_ref, o_ref, ob_ref):
        s = a_ref[...] + b_ref[...]
        o_ref[...] = s
        ob_ref[...] = s.astype(BF16)

    blk = pl.BlockSpec((1, tr, c), lambda k, i, cref: (k, i, 0))
    grid_spec = pltpu.PrefetchScalarGridSpec(
        num_scalar_prefetch=1, grid=(N_CHIPS, nb),
        in_specs=[pl.BlockSpec((1, tr, c), lambda k, i, cref: (k, cref[0] * nb + i, 0)), blk], out_specs=[blk, blk])
    return pl.pallas_call(
        body, name=name, out_shape=[_sds((N_CHIPS, half, c)), _sds((N_CHIPS, half, c), BF16)], grid_spec=grid_spec,
        compiler_params=pltpu.CompilerParams(dimension_semantics=("parallel", "parallel"), vmem_limit_bytes=VMEM_LIMIT))(
            core, full, recv)


def _chip_sum(name, landed, own, where):
    _, r, c = landed.shape
    tr = _pick(r, (256, 176, 128, 64, 16))

    def body(w_ref, l_ref, o_ref, s_ref):
        s_ref[0] = ((o_ref[0] + l_ref[0].astype(F32)) + l_ref[1].astype(F32)) + l_ref[2].astype(F32)

    grid_spec = pltpu.PrefetchScalarGridSpec(
        num_scalar_prefetch=1, grid=(r // tr,),
        in_specs=[pl.BlockSpec((3, tr, c), lambda i, wref: (0, i, 0)),
                  pl.BlockSpec((1, tr, c), lambda i, wref: (wref[0], i, 0))],
        out_specs=pl.BlockSpec((1, tr, c), lambda i, wref: (wref[1], i, 0)))
    return pl.pallas_call(
        body, name=name, out_shape=_sds((2, r, c)), grid_spec=grid_spec,
        compiler_params=pltpu.CompilerParams(dimension_semantics=("parallel",), vmem_limit_bytes=VMEM_LIMIT))(where, landed, own)


_WEIGHTS = ("pre_mix_norm", "w_in", "gdn_conv_w", "gdn_a_log", "gdn_dt_bias", "gdn_norm_w", "ssd_conv_w", "ssd_conv_b",
            "ssd_a_log", "ssd_dt_bias", "ssd_d", "ssd_norm_w", "w_out", "post_mix_norm", "pre_ffn_norm", "w_up",
            "ffn_conv_w", "ffn_conv_b", "w_down", "post_ffn_norm")
_BIG = ("w_in", "w_out", "w_up", "w_down")
_COL_SHARDED_SMALL = ("gdn_conv_w", "ssd_conv_w", "ffn_conv_w")
_SMALL = tuple(k for k in _WEIGHTS if k not in _BIG)


def _pack(arrs):
    flat = jnp.concatenate([a.reshape(-1) for a in arrs])
    rows = -(-flat.shape[0] // (8 * LANE)) * 8
    return jnp.pad(flat, (0, rows * LANE - flat.shape[0])).reshape(rows, LANE)


def _unpack(packed, shapes):
    flat = packed.reshape(-1)
    out, off = [], 0
    for shp in shapes:
        size = 1
        for d in shp:
            size *= d
        out.append(flat[off:off + size].reshape(shp))
        off += size
    return out


def _cols_to_chips(a):
    r, c4 = a.shape
    return jnp.transpose(a.reshape(r, N_CHIPS, c4 // N_CHIPS), (1, 0, 2))


def _chips_to_cols(a):
    k, r, c = a.shape
    return jnp.transpose(a, (1, 0, 2)).reshape(r, k * c)


def kernel(x, pre_mix_norm, w_in, gdn_conv_w, gdn_a_log, gdn_dt_bias, gdn_norm_w, ssd_conv_w, ssd_conv_b, ssd_a_log, ssd_dt_bias, ssd_d, ssd_norm_w, w_out, post_mix_norm, pre_ffn_norm, w_up, ffn_conv_w, ffn_conv_b, w_down, post_ffn_norm, loss_target, m_pre_mix_norm, m_w_in, m_gdn_conv_w, m_gdn_a_log, m_gdn_dt_bias, m_gdn_norm_w, m_ssd_conv_w, m_ssd_conv_b, m_ssd_a_log, m_ssd_dt_bias, m_ssd_d, m_ssd_norm_w, m_w_out, m_post_mix_norm, m_pre_ffn_norm, m_w_up, m_ffn_conv_w, m_ffn_conv_b, m_w_down, m_post_ffn_norm, v_pre_mix_norm, v_w_in, v_gdn_conv_w, v_gdn_a_log, v_gdn_dt_bias, v_gdn_norm_w, v_ssd_conv_w, v_ssd_conv_b, v_ssd_a_log, v_ssd_dt_bias, v_ssd_d, v_ssd_norm_w, v_w_out, v_post_mix_norm, v_pre_ffn_norm, v_w_up, v_ffn_conv_w, v_ffn_conv_b, v_w_down, v_post_ffn_norm):
    w = dict(zip(_WEIGHTS, (pre_mix_norm, w_in, gdn_conv_w, gdn_a_log, gdn_dt_bias, gdn_norm_w, ssd_conv_w, ssd_conv_b,
                            ssd_a_log, ssd_dt_bias, ssd_d, ssd_norm_w, w_out, post_mix_norm, pre_ffn_norm, w_up,
                            ffn_conv_w, ffn_conv_b, w_down, post_ffn_norm)))
    m = dict(zip(_WEIGHTS, (m_pre_mix_norm, m_w_in, m_gdn_conv_w, m_gdn_a_log, m_gdn_dt_bias, m_gdn_norm_w, m_ssd_conv_w,
                            m_ssd_conv_b, m_ssd_a_log, m_ssd_dt_bias, m_ssd_d, m_ssd_norm_w, m_w_out, m_post_mix_norm,
                            m_pre_ffn_norm, m_w_up, m_ffn_conv_w, m_ffn_conv_b, m_w_down, m_post_ffn_norm)))
    v = dict(zip(_WEIGHTS, (v_pre_mix_norm, v_w_in, v_gdn_conv_w, v_gdn_a_log, v_gdn_dt_bias, v_gdn_norm_w, v_ssd_conv_w,
                            v_ssd_conv_b, v_ssd_a_log, v_ssd_dt_bias, v_ssd_d, v_ssd_norm_w, v_w_out, v_post_mix_norm,
                            v_pre_ffn_norm, v_w_up, v_ffn_conv_w, v_ffn_conv_b, v_w_down, v_post_ffn_norm)))
    cx, cy, cc = _coords()
    chip = 2 * cx + cy

    g_in, g_gcw, g_scw, g_fcw = _gather_chips([w["w_in"][0].astype(BF16)] + [w[k][0] for k in _COL_SHARDED_SMALL],
                                              [True] + [False] * len(_COL_SHARDED_SMALL))
    p = {k: w[k] for k in _SMALL if k not in _COL_SHARDED_SMALL}
    p["w_in"] = _chips_to_cols(g_in)
    for k in ("w_out", "w_up", "w_down"):
        p[k] = w[k][0].astype(BF16)
    p["gdn_conv_w"] = _chips_to_cols(g_gcw)
    p["ssd_conv_w"] = _chips_to_cols(g_scw)
    p["ffn_conv_w"] = _chips_to_cols(g_fcw)

    loss_acc, grad_x, grads, (early, early_landed) = _local_step(x, loss_target, p)
    loss = lax.psum(loss_acc[0, 0], ("x", "y", "c"))

    small_full_shapes = [grads[k].shape for k in _SMALL]
    summed = _unpack(_allreduce_small(_pack([grads[k] for k in _SMALL])), small_full_shapes)
    g_small = dict(zip(_SMALL, summed))
    for k in _COL_SHARDED_SMALL:
        width = w[k].shape[2]
        g_small[k] = lax.dynamic_slice_in_dim(g_small[k], chip * width, width, axis=1)

    late = _pair_sums("late", [_cols_to_chips(grads["w_in"])])
    late_landed = _scatter_chips([ps[1] for ps in late])
    pair_sums = dict(zip(_BIG, late + early))
    landed = dict(zip(_BIG, list(late_landed) + list(early_landed)))
    where = jnp.stack([chip, cc]).astype(jnp.int32)
    mine = [_chip_sum("chip_sum_" + k, landed[k], pair_sums[k][0], where) for k in _BIG]
    both = _pair_fill(mine)
    g_big = {k: a.reshape(-1, a.shape[2]) for k, a in zip(_BIG, both)}

    out_g, out_d, out_m, out_v = {}, {}, {}, {}
    for k in _BIG:
        out_g[k] = g_big[k][None]
        d_, m_, v_ = _adamw("adamw_" + k, w[k][0], g_big[k], m[k][0], v[k][0])
        out_d[k], out_m[k], out_v[k] = d_[None], m_[None], v_[None]
    shapes = [w[k].shape for k in _SMALL]
    for k in _SMALL:
        out_g[k] = g_small[k].reshape(w[k].shape)
    packed = [_pack([d[k] for k in _SMALL]) for d in (w, out_g, m, v)]
    d_p, m_p, v_p = _adamw("adamw_small", *packed)
    for dst, src in ((out_d, d_p), (out_m, m_p), (out_v, v_p)):
        dst.update(zip(_SMALL, _unpack(src, shapes)))
    return (loss, grad_x, *[out_g[k] for k in _WEIGHTS], *[out_d[k] for k in _WEIGHTS],
            *[out_m[k] for k in _WEIGHTS], *[out_v[k] for k in _WEIGHTS])
```

```python
import functools

import jax
import jax.numpy as jnp
from jax import lax
from jax.experimental import pallas as pl
from jax.experimental.pallas import tpu as pltpu

F32 = jnp.float32
BF16 = jnp.bfloat16

D_MODEL = 1024
GDN_HEADS = 8
GDN_DK = 128
SSD_HEADS = 16
SSD_HEADDIM = 64
SSD_GROUPS = 2
SSD_STATE = 128
CONV_K = 4
CHUNK = 64
D_FF = 2816
FFN_CONV_K = 3
EPS = 1e-6
GDN_QK = GDN_HEADS * GDN_DK
GDN_V = GDN_QK
SSD_D = SSD_HEADS * SSD_HEADDIM
SSD_BC = SSD_GROUPS * SSD_STATE
SSD_CONV_CH = SSD_D + 2 * SSD_BC
BIG = 4 * 1024 + 1024 + SSD_CONV_CH
SMALL = 128
D_IN_PROJ = 6688
LANE = 128
PAIR = 2 * CHUNK
NEG = -1e30
VMEM_LIMIT = 56 * 1024 * 1024

ADAM_LR = 0.001
ADAM_B1 = 0.9
ADAM_B2 = 0.999
ADAM_EPS = 1e-08
ADAM_WD = 0.01
ADAM_STEP = 10

N_CHIPS = 4
N_DEV = 8
MESH = pl.DeviceIdType.MESH

NN = ((1,), (0,))
NT = ((1,), (1,))
TN = ((0,), (0,))


def _bdot(a, b, dims):
    return lax.dot_general(a.astype(BF16), b.astype(BF16), (dims, ((), ())), preferred_element_type=F32)


def _split3(a):
    hi = a.astype(BF16)
    r1 = a - hi.astype(F32)
    mid = r1.astype(BF16)
    return hi, mid, (r1 - mid.astype(F32)).astype(BF16)


@jax.custom_vjp
def _nn(a, b):
    return _bdot(a, b, NN)


@jax.custom_vjp
def _nt(a, b):
    return _bdot(a, b, NT)


@jax.custom_vjp
def _tn(a, b):
    return _bdot(a, b, TN)


_nn.defvjp(lambda a, b: (_nn(a, b), (a, b)), lambda r, g: (_nt(g, r[1]), _tn(r[0], g)))
_nt.defvjp(lambda a, b: (_nt(a, b), (a, b)), lambda r, g: (_nn(g, r[1]), _tn(g, r[0])))
_tn.defvjp(lambda a, b: (_tn(a, b), (a, b)), lambda r, g: (_nt(r[1], g), _nn(r[0], g)))


def _mask_dot(x, mask, dims, x_first):
    acc = None
    for piece in _split3(x):
        term = _bdot(piece, mask, dims) if x_first else _bdot(mask, piece, dims)
        acc = term if acc is None else acc + term
    return acc


@jax.custom_vjp
def _cst_left(cst, x):
    return _mask_dot(x, cst, NN, False)


_cst_left.defvjp(lambda cst, x: (_cst_left(cst, x), cst), lambda cst, g: (jnp.zeros_like(cst), _mask_dot(g, cst, TN, False)))


@jax.custom_vjp
def _cst_right(x, cst):
    return _mask_dot(x, cst, NN, True)


_cst_right.defvjp(lambda x, cst: (_cst_right(x, cst), cst), lambda cst, g: (_mask_dot(g, cst, NT, True), jnp.zeros_like(cst)))


def _lin_left(cst):
    return functools.partial(_cst_left, cst)


def _lin_right(cst):
    return lambda x: _cst_right(x, cst)


@jax.custom_vjp
def _tri_inv_m1(a):
    pm = [-x for x in a]
    ap = list(a)
    for _ in range(5):
        ap = [_bdot(x, x, NN) for x in ap]
        pm = [(p + x) + _bdot(p, x, NN) for p, x in zip(pm, ap)]
    return pm


def _tri_inv_m1_bwd(pm, g):
    t = [gi + _bdot(p, gi, TN) for p, gi in zip(pm, g)]
    return ([-(ti + _bdot(ti, p, NT)) for p, ti in zip(pm, t)],)


_tri_inv_m1.defvjp(lambda a: (lambda pm: (pm, pm))(_tri_inv_m1(a)), _tri_inv_m1_bwd)


@jax.custom_vjp
def _top(x):
    return x[: x.shape[0] // 2]


_top.defvjp(lambda x: (_top(x), None), lambda _, g: (jnp.concatenate([g, jnp.zeros_like(g)], axis=0),))


@jax.custom_vjp
def _bot(x):
    return x[x.shape[0] // 2:]


_bot.defvjp(lambda x: (_bot(x), None), lambda _, g: (jnp.concatenate([jnp.zeros_like(g), g], axis=0),))


@jax.custom_vjp
def _vstack(a, b):
    return jnp.concatenate([a, b], axis=0)


_vstack.defvjp(lambda a, b: (_vstack(a, b), None), lambda _, g: (g[: g.shape[0] // 2], g[g.shape[0] // 2:]))


def _shift_dn_raw(x, s):
    if s == 0:
        return x
    r = pltpu.roll(x, s, axis=0)
    ri = lax.broadcasted_iota(jnp.int32, x.shape, 0)
    return jnp.where(ri >= s, r, 0.0)


def _shift_up_raw(x, s):
    if s == 0:
        return x
    n = x.shape[0]
    r = pltpu.roll(x, n - s, axis=0)
    ri = lax.broadcasted_iota(jnp.int32, x.shape, 0)
    return jnp.where(ri < n - s, r, 0.0)


@functools.partial(jax.custom_vjp, nondiff_argnums=(1,))
def _shift_dn(x, s):
    return _shift_dn_raw(x, s)


_shift_dn.defvjp(lambda x, s: (_shift_dn_raw(x, s), None), lambda s, _, g: (_shift_up_raw(g, s),))


def _conv(x, wrows):
    k_w = len(wrows)
    acc = wrows[k_w - 1] * x
    for k in range(k_w - 1):
        acc = acc + wrows[k] * _shift_dn(x, k_w - 1 - k)
    return acc


def _silu(x):
    return x * jax.nn.sigmoid(x)


def _rms(x, w):
    return x * lax.rsqrt(jnp.mean(x * x, axis=-1, keepdims=True) + EPS) * w


def _l2n(x):
    return x * lax.rsqrt(jnp.sum(x * x, axis=-1, keepdims=True) + EPS)


def _iota2(shape):
    return lax.broadcasted_iota(jnp.int32, shape, 0), lax.broadcasted_iota(jnp.int32, shape, 1)


_GDN_NARGS = 15
_SSD_NARGS = 8


def _gdn_multi(*flat):
    pairs = [flat[i:i + _GDN_NARGS] for i in range(0, len(flat), _GDN_NARGS)]
    idx = range(len(pairs))
    ri, ci = _iota2((PAIR, PAIR))
    blk = ((ri >= CHUNK) & (ci >= CHUNK)) | ((ri < CHUNK) & (ci < CHUNK))
    causal = blk & (ri >= ci)
    strict = blk & (ri > ci)
    q = [_vstack(p[0], p[1]) for p in pairs]
    k = [_vstack(p[2], p[3]) for p in pairs]
    v = [_vstack(p[4], p[5]) for p in pairs]
    gc = [_vstack(p[6], p[7]) for p in pairs]
    beta = [_vstack(p[8], p[9]) for p in pairs]
    glast = [_vstack(jnp.broadcast_to(p[11], (CHUNK, LANE)), jnp.broadcast_to(p[12], (CHUNK, LANE))) for p in pairs]
    sa = [p[13] for p in pairs]
    sb = [p[14] for p in pairs]
    decay = [jnp.exp(jnp.where(causal, gc[i] - jnp.broadcast_to(pairs[i][10], (PAIR, PAIR)), NEG)) for i in idx]
    eg = [jnp.exp(x) for x in gc]
    kbeta = [k[i] * beta[i] for i in idx]
    pm = _tri_inv_m1([jnp.where(strict, _nt(kbeta[i], k[i]) * decay[i], 0.0) for i in idx])
    qk = [_nt(q[i], k[i]) * decay[i] for i in idx]
    rhs_v = [v[i] * beta[i] for i in idx]
    rhs_k = [kbeta[i] * eg[i] for i in idx]
    u = [rhs_v[i] + _nn(pm[i], rhs_v[i]) for i in idx]
    w = [rhs_k[i] + _nn(pm[i], rhs_k[i]) for i in idx]
    q_dec = [q[i] * eg[i] for i in idx]
    k_dec = [k[i] * jnp.exp(glast[i] - gc[i]) for i in idx]
    gl = [jnp.exp(x) for x in glast]
    w_s = [_vstack(_nn(_top(w[i]), sa[i]), _nn(_bot(w[i]), sb[i])) for i in idx]
    q_s = [_vstack(_nn(_top(q_dec[i]), sa[i]), _nn(_bot(q_dec[i]), sb[i])) for i in idx]
    v_new = [u[i] - w_s[i] for i in idx]
    o = [q_s[i] + _nn(qk[i], v_new[i]) for i in idx]
    sa2 = [sa[i] * _vstack(_top(gl[i]), _top(gl[i])) + _tn(_top(k_dec[i]), _top(v_new[i])) for i in idx]
    sb2 = [sb[i] * _vstack(_bot(gl[i]), _bot(gl[i])) + _tn(_bot(k_dec[i]), _bot(v_new[i])) for i in idx]
    out = []
    for i in idx:
        out += [_top(o[i]), _bot(o[i]), sa2[i], sb2[i]]
    return tuple(out)


def _ssd_multi(*flat):
    pairs = [flat[i:i + _SSD_NARGS] for i in range(0, len(flat), _SSD_NARGS)]
    idx = range(len(pairs))
    ri, ci = _iota2((CHUNK, PAIR))
    causal = ri >= jnp.where(ci >= CHUNK, ci - CHUNK, ci)
    xdt = [p[0] * p[1] for p in pairs]
    acs = [p[2] for p in pairs]
    alast = [jnp.broadcast_to(p[4], (CHUNK, PAIR)) for p in pairs]
    lmat = [jnp.exp(jnp.where(causal, acs[i] - jnp.broadcast_to(pairs[i][3], (CHUNK, PAIR)), NEG)) for i in idx]
    cb2 = [_nt(p[6], _vstack(p[5], p[5])) for p in pairs]
    xblk = [_vstack(jnp.where(ci < CHUNK, x, 0.0), jnp.where(ci >= CHUNK, x, 0.0)) for x in xdt]
    y_off = [_nn(pairs[i][6], pairs[i][7]) * jnp.exp(acs[i]) for i in idx]
    y = [_nn(cb2[i] * lmat[i], xblk[i]) + y_off[i] for i in idx]
    el = [jnp.exp(x) for x in alast]
    st2 = [pairs[i][7] * _vstack(el[i], el[i]) + _tn(pairs[i][5], xdt[i] * jnp.exp(alast[i] - acs[i])) for i in idx]
    out = []
    for i in idx:
        out += [y[i], st2[i]]
    return tuple(out)


def _pcall(name, body, grid, in_specs, out_specs, out_shape, scratch=(), sem=None, aliases=None):
    if sem is None:
        sem = ("arbitrary",) * len(grid)
    return pl.pallas_call(
        functools.partial(body),
        out_shape=out_shape,
        grid=grid,
        in_specs=in_specs,
        out_specs=out_specs,
        scratch_shapes=scratch,
        input_output_aliases=aliases or {},
        name=name,
        compiler_params=pltpu.CompilerParams(dimension_semantics=sem, vmem_limit_bytes=VMEM_LIMIT),
    )


def _sds(shape, dtype=F32):
    return jax.ShapeDtypeStruct(shape, dtype)


def _row_spec(tm, width, colblock=0):
    return pl.BlockSpec((tm, width), lambda i, _c=colblock: (i, _c))


def _full_spec(shape):
    nd = len(shape)
    return pl.BlockSpec(shape, lambda *_: (0,) * nd)


def _zero_at_first(refs, first):
    @pl.when(first)
    def _():
        for r in refs:
            r[...] = jnp.zeros(r.shape, r.dtype)


def _pick(n, prefs):
    for p in prefs:
        if n % p == 0:
            return p
    return n


def _matmul(name, a, b, mode, out_dtype, tiles=None, part=None):
    def want(i, dim):
        return [tiles[i]] if tiles is not None and dim % tiles[i] == 0 else []

    if mode == "tn":
        r, m = a.shape
        n = b.shape[1]
        tm = _pick(m, want(0, m) + [1024, 1408])
        tn = _pick(n, want(1, n) + [512, 256, 128])
        tk = _pick(r, want(2, r) + [1024, 512, 256, 128, 64])
        nc = _pick(tn, (512, 256, 128))
        n_total, col_off, into = part if part is not None else (n, 0, None)
        off = col_off // tn

        def body(a_ref, b_ref, *rest):
            o_ref = rest[-1]
            _zero_at_first([o_ref], pl.program_id(2) == 0)
            for c0 in range(0, tn, nc):
                o_ref[:, c0:c0 + nc] += _bdot(a_ref[...], b_ref[:, c0:c0 + nc], TN)

        in_specs = [pl.BlockSpec((tk, tm), lambda i, j, k: (k, i)), pl.BlockSpec((tk, tn), lambda i, j, k: (k, j))]
        args = (a, b) if into is None else (a, b, into)
        return _pcall(
            name, body, (m // tm, n // tn, r // tk), in_specs + ([] if into is None else [_ANY]),
            pl.BlockSpec((tm, tn), lambda i, j, k: (i, j + off)), _sds((m, n_total), out_dtype),
            sem=("parallel", "parallel", "arbitrary"), aliases=None if into is None else {2: 0})(*args)
    m, k = a.shape
    n = b.shape[1] if mode == "nn" else b.shape[0]
    tm = _pick(m, want(0, m) + ([1024, 512, 256, 128, 64] if k <= 2816 else [512, 256, 128, 64]))
    tn = _pick(n, want(1, n) + [512, 256, 128])
    dims = NN if mode == "nn" else NT

    nc = _pick(tn, (512, 256, 128))

    def body(a_ref, b_ref, o_ref):
        for c0 in range(0, tn, nc):
            b_blk = b_ref[:, c0:c0 + nc] if mode == "nn" else b_ref[c0:c0 + nc, :]
            o_ref[:, c0:c0 + nc] = _bdot(a_ref[...], b_blk, dims).astype(o_ref.dtype)

    b_spec = pl.BlockSpec((k, tn), lambda i, j: (0, j)) if mode == "nn" else pl.BlockSpec((tn, k), lambda i, j: (j, 0))
    return _pcall(
        name, body, (m // tm, n // tn), [pl.BlockSpec((tm, k), lambda i, j: (i, 0)), b_spec],
        pl.BlockSpec((tm, tn), lambda i, j: (i, j)), _sds((m, n), out_dtype), sem=("parallel", "parallel"))(a, b)


def _matmul_nt_scattering(name, a, b, tm_pref, scatter):
    m, k = a.shape
    n = b.shape[0]
    tm = _pick(m, (tm_pref, 256, 128, 64))
    nc = _pick(n, (512, 256, 128))
    ns = len(scatter)
    steps = m // tm

    def body(a_ref, b_ref, *rest):
        sc_in, o_ref, sc_out, sems = rest[:ns], rest[ns], rest[ns + 1:2 * ns + 1], rest[2 * ns + 1:]
        sc_start, sc_finish = _scatter_ops(sc_in, sc_out, sems)
        pl.when(pl.program_id(0) == 0)(sc_start)
        for c0 in range(0, n, nc):
            o_ref[:, c0:c0 + nc] = _bdot(a_ref[...], b_ref[c0:c0 + nc, :], NT)
        pl.when(pl.program_id(0) == steps - 1)(sc_finish)

    outs = pl.pallas_call(
        body, name=name, grid=(steps,),
        in_specs=[pl.BlockSpec((tm, k), lambda i: (i, 0)), _full_spec(b.shape)] + [_ANY] * ns,
        out_specs=[pl.BlockSpec((tm, n), lambda i: (i, 0))] + [_ANY] * ns,
        out_shape=[_sds((m, n))] + [_sds((3,) + s.shape[1:], s.dtype) for s in scatter],
        scratch_shapes=_scatter_sems(ns),
        compiler_params=pltpu.CompilerParams(dimension_semantics=("arbitrary",), vmem_limit_bytes=VMEM_LIMIT,
                                             has_side_effects=True))(a, b, *scatter)
    return outs[0], outs[1:]


def _matmul_nt_split(name, a1, a2, b, tm_pref):
    m, kh = a1.shape
    n = b.shape[0]
    tm = _pick(m, (tm_pref, 512, 256, 128, 64))
    nc = _pick(n, (512, 256, 128))

    def body(a1_ref, a2_ref, b_ref, o_ref):
        for c0 in range(0, n, nc):
            o_ref[:, c0:c0 + nc] = (_bdot(a1_ref[...], b_ref[c0:c0 + nc, :kh], NT)
                                    + _bdot(a2_ref[...], b_ref[c0:c0 + nc, kh:], NT))

    aspec = pl.BlockSpec((tm, kh), lambda i: (i, 0))
    return _pcall(name, body, (m // tm,), [aspec, aspec, _full_spec(b.shape)], pl.BlockSpec((tm, n), lambda i: (i, 0)),
                  _sds((m, n)), sem=("parallel",))(a1, a2, b)


def _row_tile(t):
    return _pick(t, (256, 128, 64))


def _rms_fwd_gathering(name, x, g, shards, split):
    t = x.shape[0]
    tm = next(c for c in (128, 64, 32, 16) if t % c == 0 and t // c >= 2)
    steps = t // tm
    ns = len(shards)

    def body(x_ref, g_ref, *rest):
        ins, h_ref, outs, sems = rest[:ns], rest[ns], rest[ns + 1:2 * ns + 1], rest[2 * ns + 1:]
        start, forward, finish = _gather_ops(ins, outs, sems, split)
        pl.when(pl.program_id(0) == 0)(start)
        h_ref[...] = _rms(x_ref[...], g_ref[...]).astype(BF16)
        pl.when(pl.program_id(0) == steps - 2)(forward)
        pl.when(pl.program_id(0) == steps - 1)(finish)

    assert steps >= 2
    outs = pl.pallas_call(
        body, name=name, grid=(steps,),
        in_specs=[_row_spec(tm, D_MODEL), _full_spec((1, D_MODEL))] + [_ANY] * ns,
        out_specs=[_row_spec(tm, D_MODEL)] + [_ANY] * ns,
        out_shape=[_sds((t, D_MODEL), BF16)] + [_sds((N_CHIPS,) + s.shape, s.dtype) for s in shards],
        scratch_shapes=_gather_sems(ns),
        compiler_params=pltpu.CompilerParams(dimension_semantics=("arbitrary",), vmem_limit_bytes=VMEM_LIMIT,
                                             has_side_effects=True))(x, g, *shards)
    return outs[0], outs[1:]


_G_LO, _G_HI = GDN_HEADS, 2 * GDN_HEADS
_S_LO, _S_HI = 2 * GDN_HEADS, 2 * GDN_HEADS + SSD_HEADS


def _gates_fn(small, bias, a_log):
    tm = small.shape[0]
    rr, cc = _iota2((tm, tm))
    in_chunk_tril = (((rr >> 6) == (cc >> 6)) & (rr >= cc)).astype(F32)
    lane = lax.broadcasted_iota(jnp.int32, small.shape, 1)
    sp = jax.nn.softplus(small + bias)
    act = jnp.where(lane < _G_LO, jax.nn.sigmoid(small), sp)
    cum = _lin_left(in_chunk_tril)(-jnp.exp(a_log) * sp)
    r, c = _iota2((SMALL, D_MODEL))
    to_ssd_lanes = _lin_right((r == _S_LO + (c >> 6)).astype(F32))
    return act, cum, to_ssd_lanes(act), to_ssd_lanes(cum)


def _expand_lanes(src_ref, dst_ref, lo, heads, width):
    rows = src_ref.shape[0]
    for h in range(heads):
        dst_ref[:, h * width:(h + 1) * width] = jnp.broadcast_to(src_ref[:, lo + h:lo + h + 1], (rows, width))


def _reduce_lanes(wide_ref, lo, heads, width):
    rows = wide_ref.shape[0]
    lane = lax.broadcasted_iota(jnp.int32, (rows, SMALL), 1)
    acc = jnp.zeros((rows, SMALL), F32)
    for h in range(heads):
        col = jnp.sum(wide_ref[:, h * width:(h + 1) * width], axis=-1, keepdims=True)
        acc = jnp.where(lane == lo + h, jnp.broadcast_to(col, (rows, SMALL)), acc)
    return acc


def _gates_fwd(small, bias, a_log):
    t = small.shape[0]
    tm = _row_tile(t)

    def body(s_ref, p0, p1, act_ref, cum_ref, dt_ref, acs_ref):
        act_ref[...], cum_ref[...], dt_ref[...], acs_ref[...] = _gates_fn(s_ref[...], p0[...], p1[...])

    pspec, nspec, wspec = _full_spec((1, SMALL)), _row_spec(tm, SMALL), _row_spec(tm, D_MODEL)
    return _pcall("gates_fwd", body, (t // tm,), [nspec, pspec, pspec], [nspec, nspec, wspec, wspec],
                  [_sds((t, SMALL))] * 2 + [_sds((t, D_MODEL))] * 2, sem=("parallel",))(small, bias, a_log)


def _gates_bwd(small, bias, a_log, d_act, d_cums, d_dt, d_acs):
    t = small.shape[0]
    tm = _row_tile(t)
    nc = len(d_cums)

    def body(*refs):
        s_ref, p0, p1, dact_ref = refs[:4]
        dcum_refs = refs[4:4 + nc]
        ddt_ref, dacs_ref, ds_ref, db_ref, da_ref = refs[4 + nc:]
        _zero_at_first([db_ref, da_ref], pl.program_id(0) == 0)
        _, vjp = jax.vjp(_gates_fn, s_ref[...], p0[...], p1[...])
        d_cum = dcum_refs[0][...]
        for c in dcum_refs[1:]:
            d_cum = d_cum + c[...]
        d_s, d_b, d_a = vjp((dact_ref[...], d_cum, ddt_ref[...], dacs_ref[...]))
        ds_ref[...] = d_s.astype(BF16)
        db_ref[...] += d_b
        da_ref[...] += d_a

    pspec, nspec, wspec = _full_spec((1, SMALL)), _row_spec(tm, SMALL), _row_spec(tm, D_MODEL)
    return _pcall("gates_bwd", body, (t // tm,), [nspec, pspec, pspec] + [nspec] * (1 + nc) + [wspec, wspec],
                  [nspec, pspec, pspec], [_sds((t, SMALL), BF16), _sds((1, SMALL)), _sds((1, SMALL))])(
                      small, bias, a_log, d_act, *d_cums, d_dt, d_acs)


def _gdn_out_fn(o, z, w):
    return _rms(o, w) * _silu(z)


def _gdn_out_fwd(o, proj, gn):
    t = o.shape[0]
    tm = _row_tile(t)

    def body(o_ref, z_ref, w_ref, y_ref):
        for h in range(GDN_HEADS):
            sl = slice(h * GDN_DK, (h + 1) * GDN_DK)
            y_ref[:, sl] = _gdn_out_fn(o_ref[:, sl], z_ref[:, sl].astype(F32), w_ref[...]).astype(BF16)

    return _pcall("gdn_out_fwd", body, (t // tm,), [_row_spec(tm, GDN_V), _row_spec(tm, GDN_V, 3), _full_spec((1, GDN_DK))],
                  _row_spec(tm, GDN_V), _sds((t, GDN_V + SSD_D), BF16), sem=("parallel",))(o, proj, gn)


def _gdn_out_bwd(o, proj, gn, d_ocat):
    t = o.shape[0]
    tm = _row_tile(t)

    def body(o_ref, z_ref, w_ref, dy_ref, do_ref, dz_ref, dw_ref):
        _zero_at_first([dw_ref], pl.program_id(0) == 0)
        for h in range(GDN_HEADS):
            sl = slice(h * GDN_DK, (h + 1) * GDN_DK)
            _, vjp = jax.vjp(_gdn_out_fn, o_ref[:, sl], z_ref[:, sl].astype(F32), w_ref[...])
            d_o, d_z, d_w = vjp(dy_ref[:, sl])
            do_ref[:, sl] = d_o
            dz_ref[:, sl] = d_z.astype(BF16)
            dw_ref[...] += d_w

    return _pcall("gdn_out_bwd", body, (t // tm,),
                  [_row_spec(tm, GDN_V), _row_spec(tm, GDN_V, 3), _full_spec((1, GDN_DK)), _row_spec(tm, GDN_V, 0)],
                  [_row_spec(tm, GDN_V), _row_spec(tm, GDN_V, 3), _full_spec((1, GDN_DK))],
                  [_sds((t, GDN_V)), _sds((t, BIG), BF16), _sds((1, GDN_DK))])(o, proj, gn, d_ocat)


def _ssd_out_fn(y, xs, z, d_skip, w):
    return _rms((y + d_skip * xs) * _silu(z), w)


_SSD_GW = SSD_D // SSD_GROUPS


def _ssd_out_fwd(y, xbc, proj, d_skip, nw, ocat):
    t = y.shape[0]
    tm = _row_tile(t)

    def body(y_ref, x_ref, z_ref, d_ref, w_ref, _, o_ref):
        for gi in range(SSD_GROUPS):
            sl = slice(gi * _SSD_GW, (gi + 1) * _SSD_GW)
            o_ref[:, sl] = _ssd_out_fn(y_ref[:, sl], x_ref[:, sl], z_ref[:, sl].astype(F32), d_ref[:, sl], w_ref[:, sl]).astype(BF16)

    pspec = _full_spec((1, SSD_D))
    return _pcall("ssd_out_fwd", body, (t // tm,),
                  [_row_spec(tm, SSD_D), _row_spec(tm, SSD_D, 0), _row_spec(tm, SSD_D, 4), pspec, pspec, _ANY],
                  _row_spec(tm, SSD_D, 1), _sds(ocat.shape, BF16), sem=("parallel",), aliases={5: 0})(
                      y, xbc, proj, d_skip, nw, ocat)


def _ssd_out_bwd(y, xbc, proj, d_skip, nw, d_ocat, d_proj):
    t = y.shape[0]
    tm = _row_tile(t)

    def body(y_ref, x_ref, z_ref, d_ref, w_ref, do_ref, _, dy_ref, dx_ref, dz_ref, dd_ref, dw_ref):
        _zero_at_first([dd_ref, dw_ref], pl.program_id(0) == 0)
        for gi in range(SSD_GROUPS):
            sl = slice(gi * _SSD_GW, (gi + 1) * _SSD_GW)
            _, vjp = jax.vjp(_ssd_out_fn, y_ref[:, sl], x_ref[:, sl], z_ref[:, sl].astype(F32), d_ref[:, sl], w_ref[:, sl])
            d_y, d_x, d_z, d_d, d_w = vjp(do_ref[:, sl])
            dy_ref[:, sl] = d_y
            dx_ref[:, sl] = d_x
            dz_ref[:, sl] = d_z.astype(BF16)
            dd_ref[:, sl] += d_d
            dw_ref[:, sl] += d_w

    pspec = _full_spec((1, SSD_D))
    row = _row_spec(tm, SSD_D)
    return _pcall("ssd_out_bwd", body, (t // tm,),
                  [row, _row_spec(tm, SSD_D, 0), _row_spec(tm, SSD_D, 4), pspec, pspec, _row_spec(tm, SSD_D, 1), _ANY],
                  [row, row, _row_spec(tm, SSD_D, 4), pspec, pspec],
                  [_sds((t, SSD_D)), _sds((t, SSD_D)), _sds((t, BIG), BF16), _sds((1, SSD_D)), _sds((1, SSD_D))],
                  aliases={6: 2})(y, xbc, proj, d_skip, nw, d_ocat, d_proj)


def _res1_fn(x, mix, g_pm, g_pf):
    x1 = x + _rms(mix, g_pm)
    return x1, _rms(x1, g_pf)


def _res1_fwd(x, mix, g_pm, g_pf):
    t = x.shape[0]
    tm = _row_tile(t)

    def body(x_ref, m_ref, a_ref, b_ref, x1_ref, h2_ref):
        x1, h2 = _res1_fn(x_ref[...], m_ref[...], a_ref[...], b_ref[...])
        x1_ref[...] = x1
        h2_ref[...] = h2.astype(BF16)

    row, pspec = _row_spec(tm, D_MODEL), _full_spec((1, D_MODEL))
    return _pcall("res1_fwd", body, (t // tm,), [row, row, pspec, pspec], [row, row],
                  [_sds((t, D_MODEL)), _sds((t, D_MODEL), BF16)], sem=("parallel",))(x, mix, g_pm, g_pf)


def _res1_bwd(x, mix, g_pm, g_pf, d_x1, d_h2):
    t = x.shape[0]
    tm = _row_tile(t)

    def body(x_ref, m_ref, a_ref, b_ref, c1_ref, c2_ref, dx_ref, dm_ref, da_ref, db_ref):
        _zero_at_first([da_ref, db_ref], pl.program_id(0) == 0)
        _, vjp = jax.vjp(_res1_fn, x_ref[...], m_ref[...], a_ref[...], b_ref[...])
        d_x, d_m, d_a, d_b = vjp((c1_ref[...], c2_ref[...]))
        dx_ref[...] = d_x
        dm_ref[...] = d_m.astype(BF16)
        da_ref[...] += d_a
        db_ref[...] += d_b

    row, pspec = _row_spec(tm, D_MODEL), _full_spec((1, D_MODEL))
    return _pcall("res1_bwd", body, (t // tm,), [row, row, pspec, pspec, row, row], [row, row, pspec, pspec],
                  [_sds((t, D_MODEL)), _sds((t, D_MODEL), BF16), _sds((1, D_MODEL)), _sds((1, D_MODEL))])(
                      x, mix, g_pm, g_pf, d_x1, d_h2)


def _final_fn(x1, f, g_po, tgt):
    err = x1 + _rms(f, g_po) - tgt
    return 0.5 * jnp.sum(jnp.mean(err * err, axis=-1))


def _final(x1, f, g_po, tgt):
    t = x1.shape[0]
    tm = _row_tile(t)

    def body(x_ref, f_ref, g_ref, t_ref, loss_ref, dx_ref, df_ref, dg_ref):
        _zero_at_first([loss_ref, dg_ref], pl.program_id(0) == 0)
        loss, (d_x, d_f, d_g) = jax.value_and_grad(_final_fn, argnums=(0, 1, 2))(x_ref[...], f_ref[...], g_ref[...], t_ref[...])
        loss_ref[...] += jnp.broadcast_to(loss, loss_ref.shape)
        dx_ref[...] = d_x
        df_ref[...] = d_f.astype(BF16)
        dg_ref[...] += d_g

    row, pspec = _row_spec(tm, D_MODEL), _full_spec((1, D_MODEL))
    return _pcall("final", body, (t // tm,), [row, row, pspec, row], [_full_spec((8, LANE)), row, row, pspec],
                  [_sds((8, LANE)), _sds((t, D_MODEL)), _sds((t, D_MODEL), BF16), _sds((1, D_MODEL))])(x1, f, g_po, tgt)


def _rms1_bwd(x, g, d_h_a, d_h_b, d_x1):
    t = x.shape[0]
    tm = _row_tile(t)

    def body(x_ref, g_ref, dha_ref, dhb_ref, dx1_ref, dx_ref, dg_ref):
        _zero_at_first([dg_ref], pl.program_id(0) == 0)
        _, vjp = jax.vjp(_rms, x_ref[...], g_ref[...])
        d_x, d_g = vjp(dha_ref[...] + dhb_ref[...])
        dx_ref[...] = d_x + dx1_ref[...]
        dg_ref[...] += d_g

    row, pspec = _row_spec(tm, D_MODEL), _full_spec((1, D_MODEL))
    return _pcall("rms1_bwd", body, (t // tm,), [row, pspec, row, row, row], [row, pspec],
                  [_sds((t, D_MODEL)), _sds((1, D_MODEL))])(x, g, d_h_a, d_h_b, d_x1)


def _qkv_fn(mode):
    def fn(x, *wrows):
        y = _silu(_conv(x, wrows))
        if mode == "q":
            return _l2n(y) * (GDN_DK ** -0.5)
        if mode == "k":
            return _l2n(y)
        return y
    return fn


def _seq_spec(s, tc, off):
    return pl.BlockSpec((s, tc), lambda j, b, _o=off: (b, _o + j))


def _par_spec(rows, tc, off):
    return pl.BlockSpec((rows, tc), lambda j, b, _o=off: (0, _o + j))


def _gdn_conv_fwd(mode, proj, w, bsz, s):
    off = {"q": 0, "k": GDN_HEADS, "v": 2 * GDN_HEADS}[mode]
    fn = _qkv_fn(mode)

    def body(x_ref, w_ref, y_ref):
        y_ref[...] = fn(x_ref[...].astype(F32), *[w_ref[k:k + 1, :] for k in range(CONV_K)])

    return _pcall("gdn_conv_fwd_" + mode, body, (GDN_HEADS, bsz),
                  [_seq_spec(s, GDN_DK, off), _par_spec(CONV_K, GDN_DK, off)], _seq_spec(s, GDN_DK, 0),
                  _sds((bsz * s, GDN_QK)), sem=("parallel", "parallel"))(proj, w)


def _gdn_conv_bwd(mode, proj, w, d_y, d_proj, bsz, s):
    off = {"q": 0, "k": GDN_HEADS, "v": 2 * GDN_HEADS}[mode]
    fn = _qkv_fn(mode)

    def body(x_ref, w_ref, dy_ref, _, dx_ref, dw_ref):
        _zero_at_first([dw_ref], pl.program_id(1) == 0)
        _, vjp = jax.vjp(fn, x_ref[...].astype(F32), *[w_ref[k:k + 1, :] for k in range(CONV_K)])
        grads = vjp(dy_ref[...])
        dx_ref[...] = grads[0].astype(BF16)
        for k in range(CONV_K):
            dw_ref[k:k + 1, :] += grads[1 + k]

    return _pcall("gdn_conv_bwd_" + mode, body, (GDN_HEADS, bsz),
                  [_seq_spec(s, GDN_DK, off), _par_spec(CONV_K, GDN_DK, off), _seq_spec(s, GDN_DK, 0), _ANY],
                  [_seq_spec(s, GDN_DK, off), _par_spec(CONV_K, GDN_DK, 0)],
                  [_sds(d_proj.shape, BF16), _sds((CONV_K, GDN_QK))], sem=("parallel", "arbitrary"), aliases={3: 0})(
                      proj, w, d_y, d_proj)


def _ssd_conv_fn(x, bias, *wrows):
    return _silu(_conv(x, wrows) + bias)


_XBC_OFF = (5 * 1024) // LANE


def _ssd_conv_fwd(proj, w, bias, bsz, s):
    nt_ = SSD_CONV_CH // LANE

    def body(x_ref, w_ref, b_ref, y_ref):
        y_ref[...] = _ssd_conv_fn(x_ref[...].astype(F32), b_ref[...], *[w_ref[k:k + 1, :] for k in range(CONV_K)])

    return _pcall("ssd_conv_fwd", body, (nt_, bsz),
                  [_seq_spec(s, LANE, _XBC_OFF), _par_spec(CONV_K, LANE, 0), _par_spec(1, LANE, 0)], _seq_spec(s, LANE, 0),
                  _sds((bsz * s, SSD_CONV_CH)), sem=("parallel", "parallel"))(proj, w, bias)


def _ssd_conv_bwd(proj, w, bias, d_y, d_proj, bsz, s):
    nt_ = SSD_CONV_CH // LANE

    def body(x_ref, w_ref, b_ref, dy_ref, _, dx_ref, dw_ref, db_ref):
        _zero_at_first([dw_ref, db_ref], pl.program_id(1) == 0)
        _, vjp = jax.vjp(_ssd_conv_fn, x_ref[...].astype(F32), b_ref[...], *[w_ref[k:k + 1, :] for k in range(CONV_K)])
        grads = vjp(dy_ref[...])
        dx_ref[...] = grads[0].astype(BF16)
        db_ref[...] += grads[1]
        for k in range(CONV_K):
            dw_ref[k:k + 1, :] += grads[2 + k]

    return _pcall("ssd_conv_bwd", body, (nt_, bsz),
                  [_seq_spec(s, LANE, _XBC_OFF), _par_spec(CONV_K, LANE, 0), _par_spec(1, LANE, 0), _seq_spec(s, LANE, 0), _ANY],
                  [_seq_spec(s, LANE, _XBC_OFF), _par_spec(CONV_K, LANE, 0), _par_spec(1, LANE, 0)],
                  [_sds(d_proj.shape, BF16), _sds((CONV_K, SSD_CONV_CH)), _sds((1, SSD_CONV_CH))],
                  sem=("parallel", "arbitrary"), aliases={4: 0})(proj, w, bias, d_y, d_proj)


_FFN_TC = 256
_FFN_NT = D_FF // _FFN_TC


def _ffn_act_fn(xg, xu, bg, bu, *wrows):
    k_w = FFN_CONV_K
    gate = _conv(xg, wrows[:k_w]) + bg
    up = _conv(xu, wrows[k_w:]) + bu
    return _silu(gate) * up


def _ffn_act_fwd(u_pre, w, bias, bsz, s):
    def body(xg_ref, xu_ref, wg_ref, wu_ref, bg_ref, bu_ref, a_ref):
        rows = [wg_ref[k:k + 1, :] for k in range(FFN_CONV_K)] + [wu_ref[k:k + 1, :] for k in range(FFN_CONV_K)]
        a_ref[...] = _ffn_act_fn(xg_ref[...].astype(F32), xu_ref[...].astype(F32), bg_ref[...], bu_ref[...], *rows).astype(BF16)

    return _pcall("ffn_act_fwd", body, (_FFN_NT, bsz),
                  [_seq_spec(s, _FFN_TC, 0), _seq_spec(s, _FFN_TC, _FFN_NT),
                   _par_spec(FFN_CONV_K, _FFN_TC, 0), _par_spec(FFN_CONV_K, _FFN_TC, _FFN_NT),
                   _par_spec(1, _FFN_TC, 0), _par_spec(1, _FFN_TC, _FFN_NT)],
                  _seq_spec(s, _FFN_TC, 0), _sds((bsz * s, D_FF), BF16), sem=("parallel", "parallel"))(
                      u_pre, u_pre, w, w, bias, bias)


def _ffn_act_bwd(u_pre, w, bias, d_a, bsz, s):
    def body(xg_ref, xu_ref, wg_ref, wu_ref, bg_ref, bu_ref, da_ref, dg_ref, du_ref, dwg_ref, dwu_ref, dbg_ref, dbu_ref):
        _zero_at_first([dwg_ref, dwu_ref, dbg_ref, dbu_ref], pl.program_id(1) == 0)
        rows = [wg_ref[k:k + 1, :] for k in range(FFN_CONV_K)] + [wu_ref[k:k + 1, :] for k in range(FFN_CONV_K)]
        _, vjp = jax.vjp(_ffn_act_fn, xg_ref[...].astype(F32), xu_ref[...].astype(F32), bg_ref[...], bu_ref[...], *rows)
        grads = vjp(da_ref[...])
        dg_ref[...] = grads[0].astype(BF16)
        du_ref[...] = grads[1].astype(BF16)
        dbg_ref[...] += grads[2]
        dbu_ref[...] += grads[3]
        for k in range(FFN_CONV_K):
            dwg_ref[k:k + 1, :] += grads[4 + k]
            dwu_ref[k:k + 1, :] += grads[4 + FFN_CONV_K + k]

    seq0, par3, par1 = _seq_spec(s, _FFN_TC, 0), _par_spec(FFN_CONV_K, _FFN_TC, 0), _par_spec(1, _FFN_TC, 0)
    return _pcall("ffn_act_bwd", body, (_FFN_NT, bsz),
                  [seq0, _seq_spec(s, _FFN_TC, _FFN_NT), par3, _par_spec(FFN_CONV_K, _FFN_TC, _FFN_NT),
                   par1, _par_spec(1, _FFN_TC, _FFN_NT), seq0],
                  [seq0, seq0, par3, par3, par1, par1],
                  [_sds((bsz * s, D_FF), BF16), _sds((bsz * s, D_FF), BF16), _sds((FFN_CONV_K, D_FF)), _sds((FFN_CONV_K, D_FF)),
                   _sds((1, D_FF)), _sds((1, D_FF))], sem=("parallel", "arbitrary"))(u_pre, u_pre, w, w, bias, bias, d_a)


_GP = GDN_HEADS // 2
_SP = SSD_HEADS // 2


def _pair_lanes(p):
    return slice(2 * p * LANE, (2 * p + 1) * LANE), slice((2 * p + 1) * LANE, (2 * p + 2) * LANE)


_LAST = slice(CHUNK - 1, CHUNK)


def _gdn_args(p, q_ref, k_ref, v_ref, g_ref, b_ref, gr_ref):
    la, lb = _pair_lanes(p)
    return (q_ref[:, la], q_ref[:, lb], k_ref[:, la], k_ref[:, lb], v_ref[:, la], v_ref[:, lb], g_ref[:, la], g_ref[:, lb],
            b_ref[:, la], b_ref[:, lb], gr_ref[p], g_ref[_LAST, la], g_ref[_LAST, lb])


def _gdn_fwd(q, k, v, act, cum, gc_row, bsz, n):
    def body(q_ref, k_ref, v_ref, act_ref, cum_ref, gr_ref, o_ref, sin_ref, s_scr, g_ref, b_ref):
        _zero_at_first([s_scr], pl.program_id(1) == 0)
        _expand_lanes(act_ref, b_ref, 0, GDN_HEADS, GDN_DK)
        _expand_lanes(cum_ref, g_ref, _G_LO, GDN_HEADS, GDN_DK)
        flat = []
        for p in range(_GP):
            flat += [*_gdn_args(p, q_ref, k_ref, v_ref, g_ref, b_ref, gr_ref), s_scr[2 * p], s_scr[2 * p + 1]]
        sin_ref[...] = s_scr[...]
        outs = _gdn_multi(*flat)
        for p in range(_GP):
            la, lb = _pair_lanes(p)
            o_ref[:, la], o_ref[:, lb], s_scr[2 * p], s_scr[2 * p + 1] = outs[4 * p:4 * p + 4]

    tspec = pl.BlockSpec((CHUNK, GDN_V), lambda b, c: (b * n + c, 0))
    rspec = pl.BlockSpec((_GP, 1, LANE), lambda b, c: (b * n + c, 0, 0))
    sspec = pl.BlockSpec((GDN_HEADS, LANE, LANE), lambda b, c: (b * n + c, 0, 0))
    nspec = pl.BlockSpec((CHUNK, SMALL), lambda b, c: (b * n + c, 0))
    wide = pltpu.VMEM((CHUNK, GDN_V), F32)
    return _pcall("gdn_fwd", body, (bsz, n), [tspec] * 3 + [nspec, nspec, rspec], [tspec, sspec],
                  [_sds((bsz * n * CHUNK, GDN_V)), _sds((bsz * n * GDN_HEADS, LANE, LANE))],
                  scratch=[pltpu.VMEM((GDN_HEADS, LANE, LANE), F32), wide, wide], sem=("parallel", "arbitrary"))(
                      q, k, v, act, cum, gc_row)


def _gdn_bwd(q, k, v, act, cum, gc_row, s_in, d_o, bsz, n, scatter):
    ns = len(scatter)

    def body(q_ref, k_ref, v_ref, act_ref, cum_ref, gr_ref, sin_ref, do_ref, *rest):
        sc_in, rest = rest[:ns], rest[ns:]
        dq_ref, dk_ref, dv_ref, dact_ref, dcum_ref, dgr_ref = rest[:6]
        sc_out, rest = rest[6:6 + ns], rest[6 + ns:]
        ds_scr, g_ref, b_ref, dg_ref, db_ref, send_sems, recv_sems = rest
        step = pl.program_id(0) * n + pl.program_id(1)
        sc_start, sc_finish = _scatter_ops(sc_in, sc_out, (send_sems, recv_sems))
        pl.when(step == 0)(sc_start)
        _zero_at_first([ds_scr], pl.program_id(1) == 0)
        _expand_lanes(act_ref, b_ref, 0, GDN_HEADS, GDN_DK)
        _expand_lanes(cum_ref, g_ref, _G_LO, GDN_HEADS, GDN_DK)
        flat, cots = [], []
        for p in range(_GP):
            la, lb = _pair_lanes(p)
            flat += [*_gdn_args(p, q_ref, k_ref, v_ref, g_ref, b_ref, gr_ref), sin_ref[2 * p], sin_ref[2 * p + 1]]
            cots += [do_ref[:, la], do_ref[:, lb], ds_scr[2 * p], ds_scr[2 * p + 1]]
        _, vjp = jax.vjp(_gdn_multi, *flat)
        grads = vjp(tuple(cots))
        for p in range(_GP):
            la, lb = _pair_lanes(p)
            cts = grads[_GDN_NARGS * p:_GDN_NARGS * (p + 1)]
            for ref, i in ((dq_ref, 0), (dk_ref, 2), (dv_ref, 4), (dg_ref, 6), (db_ref, 8)):
                ref[:, la] = cts[i]
                ref[:, lb] = cts[i + 1]
            dgr_ref[p] = cts[10]
            dg_ref[_LAST, la] += cts[11]
            dg_ref[_LAST, lb] += cts[12]
            ds_scr[2 * p] = cts[13]
            ds_scr[2 * p + 1] = cts[14]
        dact_ref[...] = _reduce_lanes(db_ref, 0, GDN_HEADS, GDN_DK)
        dcum_ref[...] = _reduce_lanes(dg_ref, _G_LO, GDN_HEADS, GDN_DK)
        pl.when(step == bsz * n - 1)(sc_finish)

    tspec = pl.BlockSpec((CHUNK, GDN_V), lambda b, c: (b * n + (n - 1 - c), 0))
    nspec = pl.BlockSpec((CHUNK, SMALL), lambda b, c: (b * n + (n - 1 - c), 0))
    rspec = pl.BlockSpec((_GP, 1, LANE), lambda b, c: (b * n + (n - 1 - c), 0, 0))
    sspec = pl.BlockSpec((GDN_HEADS, LANE, LANE), lambda b, c: (b * n + (n - 1 - c), 0, 0))
    tok_shape, nar_shape = _sds((bsz * n * CHUNK, GDN_V)), _sds((bsz * n * CHUNK, SMALL))
    wide = pltpu.VMEM((CHUNK, GDN_V), F32)
    outs = pl.pallas_call(
        body, name="gdn_bwd", grid=(bsz, n),
        in_specs=[tspec] * 3 + [nspec, nspec, rspec, sspec, tspec] + [_ANY] * ns,
        out_specs=[tspec] * 3 + [nspec, nspec, rspec] + [_ANY] * ns,
        out_shape=[tok_shape] * 3 + [nar_shape, nar_shape, _sds((bsz * n * _GP, 1, LANE))]
        + [_sds((3,) + a.shape[1:], a.dtype) for a in scatter],
        scratch_shapes=[pltpu.VMEM((GDN_HEADS, LANE, LANE), F32), wide, wide, wide, wide] + _scatter_sems(ns),
        compiler_params=pltpu.CompilerParams(dimension_semantics=("arbitrary", "arbitrary"), vmem_limit_bytes=VMEM_LIMIT,
                                             has_side_effects=True))(q, k, v, act, cum, gc_row, s_in, d_o, *scatter)
    return outs[:6], outs[6:]


_B_OFF = SSD_D // LANE
_C_OFF = (SSD_D + SSD_BC) // LANE
_PPG = _SP // SSD_GROUPS


def _ssd_args(p, x_ref, dt_ref, a_ref, ar_ref):
    lp = slice(p * LANE, (p + 1) * LANE)
    gi = p // _PPG
    b_sl = slice((_B_OFF + gi) * LANE, (_B_OFF + gi + 1) * LANE)
    c_sl = slice((_C_OFF + gi) * LANE, (_C_OFF + gi + 1) * LANE)
    return (x_ref[:, lp], dt_ref[:, lp], a_ref[:, lp], ar_ref[p], a_ref[_LAST, lp], x_ref[:, b_sl], x_ref[:, c_sl])


def _ssd_fwd(xbc, dt, acs, acs_row, bsz, n):
    def body(x_ref, dt_ref, a_ref, ar_ref, y_ref, sin_ref, s_scr):
        _zero_at_first([s_scr], pl.program_id(1) == 0)
        flat = []
        for p in range(_SP):
            flat += [*_ssd_args(p, x_ref, dt_ref, a_ref, ar_ref), s_scr[p]]
        sin_ref[...] = s_scr[...]
        outs = _ssd_multi(*flat)
        for p in range(_SP):
            y_ref[:, p * LANE:(p + 1) * LANE], s_scr[p] = outs[2 * p:2 * p + 2]

    tspec = pl.BlockSpec((CHUNK, SSD_D), lambda b, c: (b * n + c, 0))
    return _pcall("ssd_fwd", body, (bsz, n),
                  [pl.BlockSpec((CHUNK, SSD_CONV_CH), lambda b, c: (b * n + c, 0)), tspec, tspec,
                   pl.BlockSpec((_SP, 1, LANE), lambda b, c: (b * n + c, 0, 0))],
                  [tspec, pl.BlockSpec((_SP, LANE, LANE), lambda b, c: (b * n + c, 0, 0))],
                  [_sds((bsz * n * CHUNK, SSD_D)), _sds((bsz * n * _SP, LANE, LANE))],
                  scratch=[pltpu.VMEM((_SP, LANE, LANE), F32)], sem=("parallel", "arbitrary"))(xbc, dt, acs, acs_row)


def _ssd_bwd(xbc, dt, acs, acs_row, s_in, d_y, d_x_skip, bsz, n):
    def body(x_ref, dt_ref, a_ref, ar_ref, sin_ref, dy_ref, dsk_ref, dx_ref, ddt_ref, da_ref, dar_ref, ds_scr):
        _zero_at_first([ds_scr], pl.program_id(1) == 0)
        d_b = [None] * SSD_GROUPS
        d_c = [None] * SSD_GROUPS
        flat, cots = [], []
        for p in range(_SP):
            flat += [*_ssd_args(p, x_ref, dt_ref, a_ref, ar_ref), sin_ref[p]]
            cots += [dy_ref[:, p * LANE:(p + 1) * LANE], ds_scr[p]]
        _, vjp = jax.vjp(_ssd_multi, *flat)
        grads = vjp(tuple(cots))
        for p in range(_SP):
            lp = slice(p * LANE, (p + 1) * LANE)
            gi = p // _PPG
            cts = grads[_SSD_NARGS * p:_SSD_NARGS * (p + 1)]
            dx_ref[:, lp] = cts[0] + dsk_ref[:, lp]
            ddt_ref[:, lp] = cts[1]
            da_ref[:, lp] = cts[2]
            dar_ref[p] = cts[3]
            da_ref[_LAST, lp] += cts[4]
            d_b[gi] = cts[5] if d_b[gi] is None else d_b[gi] + cts[5]
            d_c[gi] = cts[6] if d_c[gi] is None else d_c[gi] + cts[6]
            ds_scr[p] = cts[7]
        for gi in range(SSD_GROUPS):
            dx_ref[:, (_B_OFF + gi) * LANE:(_B_OFF + gi + 1) * LANE] = d_b[gi]
            dx_ref[:, (_C_OFF + gi) * LANE:(_C_OFF + gi + 1) * LANE] = d_c[gi]

    def rev(b, c):
        return b * n + (n - 1 - c)

    tspec = pl.BlockSpec((CHUNK, SSD_D), lambda b, c: (rev(b, c), 0))
    xspec = pl.BlockSpec((CHUNK, SSD_CONV_CH), lambda b, c: (rev(b, c), 0))
    rspec = pl.BlockSpec((_SP, 1, LANE), lambda b, c: (rev(b, c), 0, 0))
    tok_shape = _sds((bsz * n * CHUNK, SSD_D))
    return _pcall("ssd_bwd", body, (bsz, n),
                  [xspec, tspec, tspec, rspec, pl.BlockSpec((_SP, LANE, LANE), lambda b, c: (rev(b, c), 0, 0)), tspec, tspec],
                  [xspec, tspec, tspec, rspec],
                  [_sds((bsz * n * CHUNK, SSD_CONV_CH)), tok_shape, tok_shape, _sds((bsz * n * _SP, 1, LANE))],
                  scratch=[pltpu.VMEM((_SP, LANE, LANE), F32)], sem=("parallel", "arbitrary"))(
                      xbc, dt, acs, acs_row, s_in, d_y, d_x_skip)


def _add2(name, a, b):
    t, c = a.shape
    tm = _row_tile(t)

    def body(a_ref, b_ref, o_ref):
        o_ref[...] = a_ref[...] + b_ref[...]

    return _pcall(name, body, (t // tm,), [_row_spec(tm, c)] * 2, _row_spec(tm, c), _sds((t, c)), sem=("parallel",))(a, b)


def _rep(p, width):
    return jnp.repeat(p.reshape(-1), width).reshape(1, -1)


def _to_rows(narrow, lo, hi, bsz, n):
    heads = hi - lo
    a = narrow[:, lo:hi].reshape(bsz, n, CHUNK, heads)
    return jnp.transpose(a, (0, 1, 3, 2)).reshape(bsz * n * (heads // 2), 1, 2 * CHUNK)


def _from_rows(rows, heads, bsz, n):
    return jnp.transpose(rows.reshape(bsz, n, heads, CHUNK), (0, 1, 3, 2)).reshape(bsz * n * CHUNK, heads)


def _narrow_row(gdn_part, ssd_part):
    return jnp.pad(jnp.concatenate([gdn_part, ssd_part], axis=1), ((0, 0), (_G_LO, SMALL - _S_HI)))


def _local_step(x, tgt, p):
    bsz, s, _ = x.shape
    t = bsz * s
    n = s // CHUNK
    x2 = x.reshape(t, D_MODEL)
    tgt2 = tgt.reshape(t, D_MODEL)
    gate_bias = _narrow_row(p["gdn_dt_bias"], p["ssd_dt_bias"])
    gate_a_log = _narrow_row(p["gdn_a_log"], p["ssd_a_log"])
    d_skip = _rep(p["ssd_d"], SSD_HEADDIM)

    h, (g_in, g_gcw, g_scw, g_fcw) = _rms_fwd_gathering(
        "rms0_fwd", x2, p["pre_mix_norm"], [p["w_in"]] + [p[k] for k in _COL_SHARDED_SMALL], [True, False, False, False])
    w_in = _chips_to_cols(g_in)
    p = dict(p, gdn_conv_w=_chips_to_cols(g_gcw), ssd_conv_w=_chips_to_cols(g_scw), ffn_conv_w=_chips_to_cols(g_fcw))
    w_big = jnp.concatenate([w_in[:, :4096], w_in[:, 4112:6672]], axis=1)
    w_small = jnp.concatenate([w_in[:, 4096:4112], w_in[:, 6672:6688], jnp.zeros((D_MODEL, SMALL - 32), BF16)], axis=1)
    proj, g_out, g_up, g_down = _matmul_nn_gathering(
        "mm_in_big", h, w_big, BF16, (_pick(t, (1024, 512, 256, 128, 64)), BIG // 2), [p["w_out"], p["w_up"], p["w_down"]])
    w_out, w_up, w_down = g_out.reshape(-1, D_MODEL), _chips_to_cols(g_up), g_down.reshape(-1, D_MODEL)
    small = _matmul("mm_in_small", h, w_small, "nn", F32, (1024, 128))
    gact, cum, dt, acs = _gates_fwd(small, gate_bias, gate_a_log)
    gc_row = _to_rows(cum, _G_LO, _G_HI, bsz, n)
    acs_row = _to_rows(cum, _S_LO, _S_HI, bsz, n)
    q = _gdn_conv_fwd("q", proj, p["gdn_conv_w"], bsz, s)
    k = _gdn_conv_fwd("k", proj, p["gdn_conv_w"], bsz, s)
    v = _gdn_conv_fwd("v", proj, p["gdn_conv_w"], bsz, s)
    o, gdn_s = _gdn_fwd(q, k, v, gact, cum, gc_row, bsz, n)
    ocat = _gdn_out_fwd(o, proj, p["gdn_norm_w"])
    xbc = _ssd_conv_fwd(proj, p["ssd_conv_w"], p["ssd_conv_b"], bsz, s)
    y, ssd_s = _ssd_fwd(xbc, dt, acs, acs_row, bsz, n)
    ocat = _ssd_out_fwd(y, xbc, proj, d_skip, p["ssd_norm_w"], ocat)
    mix = _matmul("mm_out", ocat, w_out, "nn", F32, (1024, 1024))
    x1, h2 = _res1_fwd(x2, mix, p["post_mix_norm"], p["pre_ffn_norm"])
    u_pre = _matmul("mm_up", h2, w_up, "nn", BF16, (1024, 2816))
    act = _ffn_act_fwd(u_pre, p["ffn_conv_w"], p["ffn_conv_b"], bsz, s)
    f = _matmul("mm_down", act, w_down, "nn", F32, (1024, 1024))
    loss_acc, d_out, d_f, g_post_ffn = _final(x1, f, p["post_ffn_norm"], tgt2)

    grads = {"post_ffn_norm": g_post_ffn}
    d_act = _matmul("mm_down_dx", d_f, w_down, "nt", F32, (1024, 2816))
    dw_down = _matmul("mm_down_dw", act, d_f, "tn", F32, (2816, 1024, 1024))
    d_gate, d_up, dwg, dwu, dbg, dbu = _ffn_act_bwd(u_pre, p["ffn_conv_w"], p["ffn_conv_b"], d_act, bsz, s)
    grads["ffn_conv_w"] = jnp.concatenate([dwg, dwu], axis=1)
    grads["ffn_conv_b"] = jnp.concatenate([dbg, dbu], axis=1)
    d_h2 = _matmul_nt_split("mm_up_dx", d_gate, d_up, w_up, 512)
    dw_up = _matmul("mm_up_dw_gate", h2, d_gate, "tn", F32, (1024, 2816, 1024), part=(2 * D_FF, 0, None))
    dw_up = _matmul("mm_up_dw_up", h2, d_up, "tn", F32, (1024, 2816, 1024), part=(2 * D_FF, D_FF, dw_up))
    d_x1, d_mix, grads["post_mix_norm"], grads["pre_ffn_norm"] = _res1_bwd(
        x2, mix, p["post_mix_norm"], p["pre_ffn_norm"], d_out, d_h2)
    d_ocat = _matmul("mm_out_dx", d_mix, w_out, "nt", F32, (1024, 2048))
    dw_out = _matmul("mm_out_dw", ocat, d_mix, "tn", F32, (2048, 1024, 1024))

    early = _pair_sums("early", [dw_out.reshape(N_CHIPS, -1, D_MODEL), _cols_to_chips(dw_up), dw_down.reshape(N_CHIPS, -1, D_MODEL)])

    d_o, d_proj, grads["gdn_norm_w"] = _gdn_out_bwd(o, proj, p["gdn_norm_w"], d_ocat)
    (d_q, d_k, d_v, d_act_g, d_cum_g, d_gc_row), early_landed = _gdn_bwd(
        q, k, v, gact, cum, gc_row, gdn_s, d_o, bsz, n, [ps[1] for ps in early])
    d_proj, dwq = _gdn_conv_bwd("q", proj, p["gdn_conv_w"], d_q, d_proj, bsz, s)
    d_proj, dwk = _gdn_conv_bwd("k", proj, p["gdn_conv_w"], d_k, d_proj, bsz, s)
    d_proj, dwv = _gdn_conv_bwd("v", proj, p["gdn_conv_w"], d_v, d_proj, bsz, s)
    grads["gdn_conv_w"] = jnp.concatenate([dwq, dwk, dwv], axis=1)

    d_y, d_xs_skip, d_proj, d_dskip, grads["ssd_norm_w"] = _ssd_out_bwd(y, xbc, proj, d_skip, p["ssd_norm_w"], d_ocat, d_proj)
    d_xbc, d_dt, d_acs, d_acs_row = _ssd_bwd(xbc, dt, acs, acs_row, ssd_s, d_y, d_xs_skip, bsz, n)
    d_proj, grads["ssd_conv_w"], grads["ssd_conv_b"] = _ssd_conv_bwd(proj, p["ssd_conv_w"], p["ssd_conv_b"], d_xbc, d_proj, bsz, s)

    d_cum_rows = jnp.concatenate([jnp.zeros((t, _G_LO), F32), _from_rows(d_gc_row, GDN_HEADS, bsz, n),
                                  _from_rows(d_acs_row, SSD_HEADS, bsz, n), jnp.zeros((t, SMALL - _S_HI), F32)], axis=1)
    d_small, d_gate_bias, d_gate_a_log = _gates_bwd(small, gate_bias, gate_a_log, d_act_g, [d_cum_g, d_cum_rows], d_dt, d_acs)
    grads["gdn_dt_bias"], grads["ssd_dt_bias"] = d_gate_bias[:, _G_LO:_G_HI], d_gate_bias[:, _S_LO:_S_HI]
    grads["gdn_a_log"], grads["ssd_a_log"] = d_gate_a_log[:, _G_LO:_G_HI], d_gate_a_log[:, _S_LO:_S_HI]
    dw_big = _matmul("mm_in_big_dw", h, d_proj, "tn", F32, (1024, 3328, 1024))
    dw_small = _matmul("mm_in_small_dw", h, d_small, "tn", F32)
    dw_in = jnp.concatenate([dw_big[:, :4096], dw_small[:, :16], dw_big[:, 4096:], dw_small[:, 16:32]], axis=1)
    late = _pair_sums("late", [_cols_to_chips(dw_in)])
    d_h_big, late_landed = _matmul_nt_scattering("mm_in_big_dx", d_proj, w_big, 512, [ps[1] for ps in late])
    d_h_small = _matmul("mm_in_small_dx", d_small, w_small, "nt", F32, (1024, 1024))
    grad_x, grads["pre_mix_norm"] = _rms1_bwd(x2, p["pre_mix_norm"], d_h_big, d_h_small, d_x1)
    grads["ssd_d"] = _head_sums(d_dskip)[:1, :SSD_HEADS]
    return loss_acc, grad_x.reshape(bsz, s, D_MODEL), grads, (late + early, list(late_landed) + list(early_landed))


def _head_sums(wide):
    def body(x_ref, o_ref):
        r, c = _iota2((D_MODEL, SMALL))
        o_ref[...] = _mask_dot(jnp.broadcast_to(x_ref[...], (8, D_MODEL)), ((r >> 6) == c).astype(F32), NN, True)

    return _pcall("head_sums", body, (1,), [_full_spec((1, D_MODEL))], _full_spec((8, SMALL)), _sds((8, SMALL)))(wide)


def _adamw_fn(w, g, m, v):
    m = ADAM_B1 * m + (1.0 - ADAM_B1) * g
    v = ADAM_B2 * v + (1.0 - ADAM_B2) * (g * g)
    m_hat = m / (1.0 - ADAM_B1 ** ADAM_STEP)
    v_hat = v / (1.0 - ADAM_B2 ** ADAM_STEP)
    delta = -ADAM_LR * (m_hat / (jnp.sqrt(v_hat) + ADAM_EPS) + ADAM_WD * w)
    return delta, m, v


def _adamw(name, w, g, m, v):
    r, c = w.shape
    tr = _pick(r, (256, 176, 128, 64, 8))

    def body(w_ref, g_ref, m_ref, v_ref, d_ref, m2_ref, v2_ref):
        d, m2, v2 = _adamw_fn(w_ref[...], g_ref[...], m_ref[...], v_ref[...])
        d_ref[...] = d
        m2_ref[...] = m2
        v2_ref[...] = v2

    spec = pl.BlockSpec((tr, c), lambda i: (i, 0))
    return _pcall(name, body, (r // tr,), [spec] * 4, [spec] * 3, [_sds((r, c))] * 3, sem=("parallel",))(w, g, m, v)


_ANY = pl.BlockSpec(memory_space=pl.ANY)
_OTHER_CHIPS = ((1, 0), (0, 1), (1, 1))


def _coords():
    return lax.axis_index("x"), lax.axis_index("y"), lax.axis_index("c")


def _flip(v, f):
    return 1 - v if f else v


def _gather_ops(ins, outs, sems, split):
    send_sems, recv_sems, fwd_send_sems, fwd_recv_sems, own_send_sems, own_recv_sems = sems
    n = len(ins)
    x, y, c = _coords()
    me = 2 * x + y
    sib = (x, y, 1 - c)

    def rows(a, core):
        if not split[a]:
            return slice(None)
        half = ins[a].shape[0] // 2
        return pl.ds(core * half, half)

    def chip(j):
        fx, fy = _OTHER_CHIPS[j]
        return _flip(x, fx), _flip(y, fy)

    def own_cp(a):
        return pltpu.make_async_remote_copy(ins[a], outs[a].at[me], own_send_sems.at[a], own_recv_sems.at[a],
                                            device_id=sib, device_id_type=MESH)

    def ici_cp(a, j):
        return pltpu.make_async_remote_copy(ins[a].at[rows(a, c)], outs[a].at[me, rows(a, c)],
                                            send_sems.at[a * 3 + j], recv_sems.at[a * 3 + j],
                                            device_id=(*chip(j), c), device_id_type=MESH)

    def landed_cp(a, j, sem_a, sem_b, core, to):
        cx, cy = chip(j)
        blk = outs[a].at[2 * cx + cy, rows(a, core)]
        return pltpu.make_async_remote_copy(blk, blk, sem_a.at[a * 3 + j], sem_b.at[a * 3 + j], device_id=to, device_id_type=MESH)

    pairs = [(a, j) for a in range(n) for j in range(3)]

    def start():
        for a in range(n):
            own_cp(a).start()
        for a, j in pairs:
            ici_cp(a, j).start()

    def forward():
        for a, j in pairs:
            landed_cp(a, j, send_sems, recv_sems, c, (*chip(j), c)).wait_recv()
            if split[a]:
                landed_cp(a, j, fwd_send_sems, fwd_recv_sems, c, sib).start()

    def finish():
        for a, j in pairs:
            if split[a]:
                landed_cp(a, j, fwd_send_sems, fwd_recv_sems, 1 - c, sib).wait_recv()
        for a in range(n):
            own_cp(a).wait_recv()
        for a, j in pairs:
            ici_cp(a, j).wait_send()
            if split[a]:
                landed_cp(a, j, fwd_send_sems, fwd_recv_sems, c, sib).wait_send()
        for a in range(n):
            own_cp(a).wait_send()

    return start, forward, finish


def _gather_sems(n):
    return [pltpu.SemaphoreType.DMA((3 * n,))] * 4 + [pltpu.SemaphoreType.DMA((n,))] * 2


def _gather_chips(arrs, split):
    n = len(arrs)

    def body(*refs):
        start, forward, finish = _gather_ops(refs[:n], refs[n:2 * n], refs[2 * n:], split)
        start()
        forward()
        finish()

    return pl.pallas_call(
        body, name="gather_chips", out_shape=[_sds((N_CHIPS,) + a.shape, a.dtype) for a in arrs],
        in_specs=[_ANY] * n, out_specs=[_ANY] * n, scratch_shapes=_gather_sems(n),
        compiler_params=pltpu.CompilerParams(has_side_effects=True))(*arrs)


def _matmul_nn_gathering(name, a, b, out_dtype, tiles, shards):
    m, k = a.shape
    n = b.shape[1]
    tm, tn = tiles
    nc = _pick(tn, (512, 256, 128))
    ns = len(shards)
    gi, gj = m // tm, n // tn
    steps = gi * gj

    def body(a_ref, b_ref, *rest):
        ins, o_ref, outs, sems = rest[:ns], rest[ns], rest[ns + 1:2 * ns + 1], rest[2 * ns + 1:]
        step = pl.program_id(0) * gj + pl.program_id(1)
        start, forward, finish = _gather_ops(ins, outs, sems, [True] * ns)
        pl.when(step == 0)(start)
        for c0 in range(0, tn, nc):
            o_ref[:, c0:c0 + nc] = _bdot(a_ref[...], b_ref[:, c0:c0 + nc], NN).astype(o_ref.dtype)
        pl.when(step == steps - 2)(forward)
        pl.when(step == steps - 1)(finish)

    assert steps >= 2
    return pl.pallas_call(
        body, name=name, grid=(gi, gj),
        in_specs=[pl.BlockSpec((tm, k), lambda i, j: (i, 0)), pl.BlockSpec((k, tn), lambda i, j: (0, j))] + [_ANY] * ns,
        out_specs=[pl.BlockSpec((tm, tn), lambda i, j: (i, j))] + [_ANY] * ns,
        out_shape=[_sds((m, n), out_dtype)] + [_sds((N_CHIPS,) + s.shape, s.dtype) for s in shards],
        scratch_shapes=_gather_sems(ns),
        compiler_params=pltpu.CompilerParams(dimension_semantics=("arbitrary", "arbitrary"), vmem_limit_bytes=VMEM_LIMIT,
                                             has_side_effects=True))(a, b, *shards)


_PEERS = tuple((fx, fy, fc) for fx in (0, 1) for fy in (0, 1) for fc in (0, 1))[1:]


def _allreduce_small(x):
    r = x.shape[0]

    def body(x_ref, o_ref, buf, send_sems, recv_sems):
        cx, cy, cc = _coords()
        me = 4 * cx + 2 * cy + cc
        sends = []
        for j, (fx, fy, fc) in enumerate(_PEERS):
            cp = pltpu.make_async_remote_copy(x_ref, buf.at[me], send_sems.at[j], recv_sems.at[j],
                                              device_id=(_flip(cx, fx), _flip(cy, fy), _flip(cc, fc)), device_id_type=MESH)
            cp.start()
            sends.append(cp)
        buf[pl.ds(me, 1)] = x_ref[...][None]
        for j, (fx, fy, fc) in enumerate(_PEERS):
            src = 4 * _flip(cx, fx) + 2 * _flip(cy, fy) + _flip(cc, fc)
            pltpu.make_async_remote_copy(x_ref, buf.at[src], send_sems.at[j], recv_sems.at[j],
                                         device_id=(_flip(cx, fx), _flip(cy, fy), _flip(cc, fc)), device_id_type=MESH).wait_recv()
        for cp in sends:
            cp.wait_send()
        acc = buf[0]
        for d in range(1, N_DEV):
            acc = acc + buf[d]
        o_ref[...] = acc

    vm = pl.BlockSpec(memory_space=pltpu.VMEM)
    return pl.pallas_call(
        body, name="allreduce_small", out_shape=_sds((r, LANE)), in_specs=[vm], out_specs=vm,
        scratch_shapes=[pltpu.VMEM((N_DEV, r, LANE), F32), pltpu.SemaphoreType.DMA((7,)), pltpu.SemaphoreType.DMA((7,))],
        compiler_params=pltpu.CompilerParams(has_side_effects=True, vmem_limit_bytes=VMEM_LIMIT))(x)


def _pair_sums(tag, arrs):
    received = _pair_send_other_half("pair_reduce_send_" + tag, arrs)
    core = lax.axis_index("c").astype(jnp.int32).reshape(1)
    return [_pair_add("pair_add_%s_%d" % (tag, i), a, b, core) for i, (a, b) in enumerate(zip(arrs, received))]


def _pair_send_other_half(name, arrs):
    n = len(arrs)

    def body(*refs):
        ins, outs = refs[:n], refs[n:2 * n]
        send_sems, recv_sems = refs[2 * n:]
        x, y, c = _coords()
        sends = []
        for a in range(n):
            half = ins[a].shape[1] // 2
            cp = pltpu.make_async_remote_copy(ins[a].at[:, pl.ds((1 - c) * half, half), :], outs[a], send_sems.at[a], recv_sems.at[a],
                                              device_id=(x, y, 1 - c), device_id_type=MESH)
            cp.start()
            sends.append(cp)
        for cp in sends:
            cp.wait_recv()
        for cp in sends:
            cp.wait_send()

    return pl.pallas_call(
        body, name=name, out_shape=[_sds((a.shape[0], a.shape[1] // 2, a.shape[2]), a.dtype) for a in arrs],
        in_specs=[_ANY] * n, out_specs=[_ANY] * n, scratch_shapes=[pltpu.SemaphoreType.DMA((n,))] * 2,
        compiler_params=pltpu.CompilerParams(has_side_effects=True))(*arrs)


def _pair_fill(arrs):
    n = len(arrs)

    def body(*refs):
        bufs = refs[n:2 * n]
        send_sems, recv_sems = refs[2 * n:]
        x, y, c = _coords()
        sends = []
        for a in range(n):
            cp = pltpu.make_async_remote_copy(bufs[a].at[c], bufs[a].at[c], send_sems.at[a], recv_sems.at[a],
                                              device_id=(x, y, 1 - c), device_id_type=MESH)
            cp.start()
            sends.append(cp)
        for a in range(n):
            theirs = bufs[a].at[1 - c]
            pltpu.make_async_remote_copy(theirs, theirs, send_sems.at[a], recv_sems.at[a],
                                         device_id=(x, y, 1 - c), device_id_type=MESH).wait_recv()
        for cp in sends:
            cp.wait_send()

    return pl.pallas_call(
        body, name="pair_gather", out_shape=[_sds(a.shape, a.dtype) for a in arrs], in_specs=[_ANY] * n, out_specs=[_ANY] * n,
        scratch_shapes=[pltpu.SemaphoreType.DMA((n,))] * 2, input_output_aliases={a: a for a in range(n)},
        compiler_params=pltpu.CompilerParams(has_side_effects=True))(*arrs)


def _scatter_ops(ins, outs, sems):
    send_sems, recv_sems = sems
    x, y, c = _coords()

    def cp(a, j):
        fx, fy = _OTHER_CHIPS[j]
        to = 2 * _flip(x, fx) + _flip(y, fy)
        return pltpu.make_async_remote_copy(ins[a].at[to], outs[a].at[j], send_sems.at[a * 3 + j], recv_sems.at[a * 3 + j],
                                            device_id=(_flip(x, fx), _flip(y, fy), c), device_id_type=MESH)

    pairs = [(a, j) for a in range(len(ins)) for j in range(3)]

    def start():
        for a, j in pairs:
            cp(a, j).start()

    def finish():
        for a, j in pairs:
            cp(a, j).wait_recv()
        for a, j in pairs:
            cp(a, j).wait_send()

    return start, finish


def _scatter_sems(n):
    return [pltpu.SemaphoreType.DMA((3 * n,))] * 2


def _scatter_chips(arrs):
    n = len(arrs)

    def body(*refs):
        start, finish = _scatter_ops(refs[:n], refs[n:2 * n], refs[2 * n:])
        start()
        finish()

    return pl.pallas_call(
        body, name="scatter_chips", out_shape=[_sds((3,) + a.shape[1:], a.dtype) for a in arrs],
        in_specs=[_ANY] * n, out_specs=[_ANY] * n, scratch_shapes=_scatter_sems(n),
        compiler_params=pltpu.CompilerParams(has_side_effects=True))(*arrs)


def _pair_add(name, full, recv, core):
    _, r, c = full.shape
    half = r // 2
    tr = _pick(half, (256, 176, 128, 64, 8))
    nb = half // tr

    def body(c_ref, a_ref, b_ref, o_ref, ob_ref):
        s = a_ref[...] + b_ref[...]
        o_ref[...] = s
        ob_ref[...] = s.astype(BF16)

    blk = pl.BlockSpec((1, tr, c), lambda k, i, cref: (k, i, 0))
    grid_spec = pltpu.PrefetchScalarGridSpec(
        num_scalar_prefetch=1, grid=(N_CHIPS, nb),
        in_specs=[pl.BlockSpec((1, tr, c), lambda k, i, cref: (k, cref[0] * nb + i, 0)), blk], out_specs=[blk, blk])
    return pl.pallas_call(
        body, name=name, out_shape=[_sds((N_CHIPS, half, c)), _sds((N_CHIPS, half, c), BF16)], grid_spec=grid_spec,
        compiler_params=pltpu.CompilerParams(dimension_semantics=("parallel", "parallel"), vmem_limit_bytes=VMEM_LIMIT))(
            core, full, recv)


def _chip_sum(name, landed, own, where):
    _, r, c = landed.shape
    tr = _pick(r, (256, 176, 128, 64, 16))

    def body(w_ref, l_ref, o_ref, s_ref):
        s_ref[0] = ((o_ref[0] + l_ref[0].astype(F32)) + l_ref[1].astype(F32)) + l_ref[2].astype(F32)

    grid_spec = pltpu.PrefetchScalarGridSpec(
        num_scalar_prefetch=1, grid=(r // tr,),
        in_specs=[pl.BlockSpec((3, tr, c), lambda i, wref: (0, i, 0)),
                  pl.BlockSpec((1, tr, c), lambda i, wref: (wref[0], i, 0))],
        out_specs=pl.BlockSpec((1, tr, c), lambda i, wref: (wref[1], i, 0)))
    return pl.pallas_call(
        body, name=name, out_shape=_sds((2, r, c)), grid_spec=grid_spec,
        compiler_params=pltpu.CompilerParams(dimension_semantics=("parallel",), vmem_limit_bytes=VMEM_LIMIT))(where, landed, own)


_WEIGHTS = ("pre_mix_norm", "w_in", "gdn_conv_w", "gdn_a_log", "gdn_dt_bias", "gdn_norm_w", "ssd_conv_w", "ssd_conv_b",
            "ssd_a_log", "ssd_dt_bias", "ssd_d", "ssd_norm_w", "w_out", "post_mix_norm", "pre_ffn_norm", "w_up",
            "ffn_conv_w", "ffn_conv_b", "w_down", "post_ffn_norm")
_BIG = ("w_in", "w_out", "w_up", "w_down")
_COL_SHARDED_SMALL = ("gdn_conv_w", "ssd_conv_w", "ffn_conv_w")
_SMALL = tuple(k for k in _WEIGHTS if k not in _BIG)


def _pack(arrs):
    flat = jnp.concatenate([a.reshape(-1) for a in arrs])
    rows = -(-flat.shape[0] // (8 * LANE)) * 8
    return jnp.pad(flat, (0, rows * LANE - flat.shape[0])).reshape(rows, LANE)


def _unpack(packed, shapes):
    flat = packed.reshape(-1)
    out, off = [], 0
    for shp in shapes:
        size = 1
        for d in shp:
            size *= d
        out.append(flat[off:off + size].reshape(shp))
        off += size
    return out


def _cols_to_chips(a):
    r, c4 = a.shape
    return jnp.transpose(a.reshape(r, N_CHIPS, c4 // N_CHIPS), (1, 0, 2))


def _chips_to_cols(a):
    k, r, c = a.shape
    return jnp.transpose(a, (1, 0, 2)).reshape(r, k * c)


def kernel(x, pre_mix_norm, w_in, gdn_conv_w, gdn_a_log, gdn_dt_bias, gdn_norm_w, ssd_conv_w, ssd_conv_b, ssd_a_log, ssd_dt_bias, ssd_d, ssd_norm_w, w_out, post_mix_norm, pre_ffn_norm, w_up, ffn_conv_w, ffn_conv_b, w_down, post_ffn_norm, loss_target, m_pre_mix_norm, m_w_in, m_gdn_conv_w, m_gdn_a_log, m_gdn_dt_bias, m_gdn_norm_w, m_ssd_conv_w, m_ssd_conv_b, m_ssd_a_log, m_ssd_dt_bias, m_ssd_d, m_ssd_norm_w, m_w_out, m_post_mix_norm, m_pre_ffn_norm, m_w_up, m_ffn_conv_w, m_ffn_conv_b, m_w_down, m_post_ffn_norm, v_pre_mix_norm, v_w_in, v_gdn_conv_w, v_gdn_a_log, v_gdn_dt_bias, v_gdn_norm_w, v_ssd_conv_w, v_ssd_conv_b, v_ssd_a_log, v_ssd_dt_bias, v_ssd_d, v_ssd_norm_w, v_w_out, v_post_mix_norm, v_pre_ffn_norm, v_w_up, v_ffn_conv_w, v_ffn_conv_b, v_w_down, v_post_ffn_norm):
    w = dict(zip(_WEIGHTS, (pre_mix_norm, w_in, gdn_conv_w, gdn_a_log, gdn_dt_bias, gdn_norm_w, ssd_conv_w, ssd_conv_b,
                            ssd_a_log, ssd_dt_bias, ssd_d, ssd_norm_w, w_out, post_mix_norm, pre_ffn_norm, w_up,
                            ffn_conv_w, ffn_conv_b, w_down, post_ffn_norm)))
    m = dict(zip(_WEIGHTS, (m_pre_mix_norm, m_w_in, m_gdn_conv_w, m_gdn_a_log, m_gdn_dt_bias, m_gdn_norm_w, m_ssd_conv_w,
                            m_ssd_conv_b, m_ssd_a_log, m_ssd_dt_bias, m_ssd_d, m_ssd_norm_w, m_w_out, m_post_mix_norm,
                            m_pre_ffn_norm, m_w_up, m_ffn_conv_w, m_ffn_conv_b, m_w_down, m_post_ffn_norm)))
    v = dict(zip(_WEIGHTS, (v_pre_mix_norm, v_w_in, v_gdn_conv_w, v_gdn_a_log, v_gdn_dt_bias, v_gdn_norm_w, v_ssd_conv_w,
                            v_ssd_conv_b, v_ssd_a_log, v_ssd_dt_bias, v_ssd_d, v_ssd_norm_w, v_w_out, v_post_mix_norm,
                            v_pre_ffn_norm, v_w_up, v_ffn_conv_w, v_ffn_conv_b, v_w_down, v_post_ffn_norm)))
    cx, cy, cc = _coords()
    chip = 2 * cx + cy

    p = {k: w[k] for k in _SMALL if k not in _COL_SHARDED_SMALL}
    for k in _BIG:
        p[k] = w[k][0].astype(BF16)
    for k in _COL_SHARDED_SMALL:
        p[k] = w[k][0]
    loss_acc, grad_x, grads, (pair_sum_list, landed_list) = _local_step(x, loss_target, p)
    loss = lax.psum(loss_acc[0, 0], ("x", "y", "c"))

    small_full_shapes = [grads[k].shape for k in _SMALL]
    summed = _unpack(_allreduce_small(_pack([grads[k] for k in _SMALL])), small_full_shapes)
    g_small = dict(zip(_SMALL, summed))
    for k in _COL_SHARDED_SMALL:
        width = w[k].shape[2]
        g_small[k] = lax.dynamic_slice_in_dim(g_small[k], chip * width, width, axis=1)

    pair_sums = dict(zip(_BIG, pair_sum_list))
    landed = dict(zip(_BIG, landed_list))
    where = jnp.stack([chip, cc]).astype(jnp.int32)
    mine = [_chip_sum("chip_sum_" + k, landed[k], pair_sums[k][0], where) for k in _BIG]
    both = _pair_fill(mine)
    g_big = {k: a.reshape(-1, a.shape[2]) for k, a in zip(_BIG, both)}

    out_g, out_d, out_m, out_v = {}, {}, {}, {}
    for k in _BIG:
        out_g[k] = g_big[k][None]
        d_, m_, v_ = _adamw("adamw_" + k, w[k][0], g_big[k], m[k][0], v[k][0])
        out_d[k], out_m[k], out_v[k] = d_[None], m_[None], v_[None]
    shapes = [w[k].shape for k in _SMALL]
    for k in _SMALL:
        out_g[k] = g_small[k].reshape(w[k].shape)
    packed = [_pack([d[k] for k in _SMALL]) for d in (w, out_g, m, v)]
    d_p, m_p, v_p = _adamw("adamw_small", *packed)
    for dst, src in ((out_d, d_p), (out_m, m_p), (out_v, v_p)):
        dst.update(zip(_SMALL, _unpack(src, shapes)))
    return (loss, grad_x, *[out_g[k] for k in _WEIGHTS], *[out_d[k] for k in _WEIGHTS],
            *[out_m[k] for k in _WEIGHTS], *[out_v[k] for k in _WEIGHTS])
```

```python
import functools

import jax
import jax.numpy as jnp
from jax import lax
from jax.experimental import pallas as pl
from jax.experimental.pallas import tpu as pltpu

F32 = jnp.float32
BF16 = jnp.bfloat16

D_MODEL = 1024
GDN_HEADS = 8
GDN_DK = 128
SSD_HEADS = 16
SSD_HEADDIM = 64
SSD_GROUPS = 2
SSD_STATE = 128
CONV_K = 4
CHUNK = 64
D_FF = 2816
FFN_CONV_K = 3
EPS = 1e-6
GDN_QK = GDN_HEADS * GDN_DK
GDN_V = GDN_QK
SSD_D = SSD_HEADS * SSD_HEADDIM
SSD_BC = SSD_GROUPS * SSD_STATE
SSD_CONV_CH = SSD_D + 2 * SSD_BC
BIG = 4 * 1024 + 1024 + SSD_CONV_CH
SMALL = 128
D_IN_PROJ = 6688
LANE = 128
PAIR = 2 * CHUNK
NEG = -1e30
VMEM_LIMIT = 56 * 1024 * 1024

ADAM_LR = 0.001
ADAM_B1 = 0.9
ADAM_B2 = 0.999
ADAM_EPS = 1e-08
ADAM_WD = 0.01
ADAM_STEP = 10

N_CHIPS = 4
N_DEV = 8
MESH = pl.DeviceIdType.MESH

NN = ((1,), (0,))
NT = ((1,), (1,))
TN = ((0,), (0,))


def _bdot(a, b, dims):
    return lax.dot_general(a.astype(BF16), b.astype(BF16), (dims, ((), ())), preferred_element_type=F32)


def _split3(a):
    hi = a.astype(BF16)
    r1 = a - hi.astype(F32)
    mid = r1.astype(BF16)
    return hi, mid, (r1 - mid.astype(F32)).astype(BF16)


@jax.custom_vjp
def _nn(a, b):
    return _bdot(a, b, NN)


@jax.custom_vjp
def _nt(a, b):
    return _bdot(a, b, NT)


@jax.custom_vjp
def _tn(a, b):
    return _bdot(a, b, TN)


_nn.defvjp(lambda a, b: (_nn(a, b), (a, b)), lambda r, g: (_nt(g, r[1]), _tn(r[0], g)))
_nt.defvjp(lambda a, b: (_nt(a, b), (a, b)), lambda r, g: (_nn(g, r[1]), _tn(g, r[0])))
_tn.defvjp(lambda a, b: (_tn(a, b), (a, b)), lambda r, g: (_nt(r[1], g), _nn(r[0], g)))


def _mask_dot(x, mask, dims, x_first):
    acc = None
    for piece in _split3(x):
        term = _bdot(piece, mask, dims) if x_first else _bdot(mask, piece, dims)
        acc = term if acc is None else acc + term
    return acc


@jax.custom_vjp
def _cst_left(cst, x):
    return _mask_dot(x, cst, NN, False)


_cst_left.defvjp(lambda cst, x: (_cst_left(cst, x), cst), lambda cst, g: (jnp.zeros_like(cst), _mask_dot(g, cst, TN, False)))


@jax.custom_vjp
def _cst_right(x, cst):
    return _mask_dot(x, cst, NN, True)


_cst_right.defvjp(lambda x, cst: (_cst_right(x, cst), cst), lambda cst, g: (_mask_dot(g, cst, NT, True), jnp.zeros_like(cst)))


def _lin_left(cst):
    return functools.partial(_cst_left, cst)


def _lin_right(cst):
    return lambda x: _cst_right(x, cst)


@jax.custom_vjp
def _tri_inv_m1(a):
    pm = [-x for x in a]
    ap = list(a)
    for _ in range(5):
        ap = [_bdot(x, x, NN) for x in ap]
        pm = [(p + x) + _bdot(p, x, NN) for p, x in zip(pm, ap)]
    return pm


def _tri_inv_m1_bwd(pm, g):
    t = [gi + _bdot(p, gi, TN) for p, gi in zip(pm, g)]
    return ([-(ti + _bdot(ti, p, NT)) for p, ti in zip(pm, t)],)


_tri_inv_m1.defvjp(lambda a: (lambda pm: (pm, pm))(_tri_inv_m1(a)), _tri_inv_m1_bwd)


@jax.custom_vjp
def _top(x):
    return x[: x.shape[0] // 2]


_top.defvjp(lambda x: (_top(x), None), lambda _, g: (jnp.concatenate([g, jnp.zeros_like(g)], axis=0),))


@jax.custom_vjp
def _bot(x):
    return x[x.shape[0] // 2:]


_bot.defvjp(lambda x: (_bot(x), None), lambda _, g: (jnp.concatenate([jnp.zeros_like(g), g], axis=0),))


@jax.custom_vjp
def _vstack(a, b):
    return jnp.concatenate([a, b], axis=0)


_vstack.defvjp(lambda a, b: (_vstack(a, b), None), lambda _, g: (g[: g.shape[0] // 2], g[g.shape[0] // 2:]))


def _shift_dn_raw(x, s):
    if s == 0:
        return x
    r = pltpu.roll(x, s, axis=0)
    ri = lax.broadcasted_iota(jnp.int32, x.shape, 0)
    return jnp.where(ri >= s, r, 0.0)


def _shift_up_raw(x, s):
    if s == 0:
        return x
    n = x.shape[0]
    r = pltpu.roll(x, n - s, axis=0)
    ri = lax.broadcasted_iota(jnp.int32, x.shape, 0)
    return jnp.where(ri < n - s, r, 0.0)


@functools.partial(jax.custom_vjp, nondiff_argnums=(1,))
def _shift_dn(x, s):
    return _shift_dn_raw(x, s)


_shift_dn.defvjp(lambda x, s: (_shift_dn_raw(x, s), None), lambda s, _, g: (_shift_up_raw(g, s),))


def _conv(x, wrows):
    k_w = len(wrows)
    acc = wrows[k_w - 1] * x
    for k in range(k_w - 1):
        acc = acc + wrows[k] * _shift_dn(x, k_w - 1 - k)
    return acc


def _silu(x):
    return x * jax.nn.sigmoid(x)


def _rms(x, w):
    return x * lax.rsqrt(jnp.mean(x * x, axis=-1, keepdims=True) + EPS) * w


def _l2n(x):
    return x * lax.rsqrt(jnp.sum(x * x, axis=-1, keepdims=True) + EPS)


def _iota2(shape):
    return lax.broadcasted_iota(jnp.int32, shape, 0), lax.broadcasted_iota(jnp.int32, shape, 1)


_GDN_NARGS = 15
_SSD_NARGS = 8


def _gdn_multi(*flat):
    pairs = [flat[i:i + _GDN_NARGS] for i in range(0, len(flat), _GDN_NARGS)]
    idx = range(len(pairs))
    ri, ci = _iota2((PAIR, PAIR))
    blk = ((ri >= CHUNK) & (ci >= CHUNK)) | ((ri < CHUNK) & (ci < CHUNK))
    causal = blk & (ri >= ci)
    strict = blk & (ri > ci)
    q = [_vstack(p[0], p[1]) for p in pairs]
    k = [_vstack(p[2], p[3]) for p in pairs]
    v = [_vstack(p[4], p[5]) for p in pairs]
    gc = [_vstack(p[6], p[7]) for p in pairs]
    beta = [_vstack(p[8], p[9]) for p in pairs]
    glast = [_vstack(jnp.broadcast_to(p[11], (CHUNK, LANE)), jnp.broadcast_to(p[12], (CHUNK, LANE))) for p in pairs]
    sa = [p[13] for p in pairs]
    sb = [p[14] for p in pairs]
    decay = [jnp.exp(jnp.where(causal, gc[i] - jnp.broadcast_to(pairs[i][10], (PAIR, PAIR)), NEG)) for i in idx]
    eg = [jnp.exp(x) for x in gc]
    kbeta = [k[i] * beta[i] for i in idx]
    pm = _tri_inv_m1([jnp.where(strict, _nt(kbeta[i], k[i]) * decay[i], 0.0) for i in idx])
    qk = [_nt(q[i], k[i]) * decay[i] for i in idx]
    rhs_v = [v[i] * beta[i] for i in idx]
    rhs_k = [kbeta[i] * eg[i] for i in idx]
    u = [rhs_v[i] + _nn(pm[i], rhs_v[i]) for i in idx]
    w = [rhs_k[i] + _nn(pm[i], rhs_k[i]) for i in idx]
    q_dec = [q[i] * eg[i] for i in idx]
    k_dec = [k[i] * jnp.exp(glast[i] - gc[i]) for i in idx]
    gl = [jnp.exp(x) for x in glast]
    w_s = [_vstack(_nn(_top(w[i]), sa[i]), _nn(_bot(w[i]), sb[i])) for i in idx]
    q_s = [_vstack(_nn(_top(q_dec[i]), sa[i]), _nn(_bot(q_dec[i]), sb[i])) for i in idx]
    v_new = [u[i] - w_s[i] for i in idx]
    o = [q_s[i] + _nn(qk[i], v_new[i]) for i in idx]
    sa2 = [sa[i] * _vstack(_top(gl[i]), _top(gl[i])) + _tn(_top(k_dec[i]), _top(v_new[i])) for i in idx]
    sb2 = [sb[i] * _vstack(_bot(gl[i]), _bot(gl[i])) + _tn(_bot(k_dec[i]), _bot(v_new[i])) for i in idx]
    out = []
    for i in idx:
        out += [_top(o[i]), _bot(o[i]), sa2[i], sb2[i]]
    return tuple(out)


def _ssd_multi(*flat):
    pairs = [flat[i:i + _SSD_NARGS] for i in range(0, len(flat), _SSD_NARGS)]
    idx = range(len(pairs))
    ri, ci = _iota2((CHUNK, PAIR))
    causal = ri >= jnp.where(ci >= CHUNK, ci - CHUNK, ci)
    xdt = [p[0] * p[1] for p in pairs]
    acs = [p[2] for p in pairs]
    alast = [jnp.broadcast_to(p[4], (CHUNK, PAIR)) for p in pairs]
    lmat = [jnp.exp(jnp.where(causal, acs[i] - jnp.broadcast_to(pairs[i][3], (CHUNK, PAIR)), NEG)) for i in idx]
    cb2 = [_nt(p[6], _vstack(p[5], p[5])) for p in pairs]
    xblk = [_vstack(jnp.where(ci < CHUNK, x, 0.0), jnp.where(ci >= CHUNK, x, 0.0)) for x in xdt]
    y_off = [_nn(pairs[i][6], pairs[i][7]) * jnp.exp(acs[i]) for i in idx]
    y = [_nn(cb2[i] * lmat[i], xblk[i]) + y_off[i] for i in idx]
    el = [jnp.exp(x) for x in alast]
    st2 = [pairs[i][7] * _vstack(el[i], el[i]) + _tn(pairs[i][5], xdt[i] * jnp.exp(alast[i] - acs[i])) for i in idx]
    out = []
    for i in idx:
        out += [y[i], st2[i]]
    return tuple(out)


def _pcall(name, body, grid, in_specs, out_specs, out_shape, scratch=(), sem=None, aliases=None):
    if sem is None:
        sem = ("arbitrary",) * len(grid)
    return pl.pallas_call(
        functools.partial(body),
        out_shape=out_shape,
        grid=grid,
        in_specs=in_specs,
        out_specs=out_specs,
        scratch_shapes=scratch,
        input_output_aliases=aliases or {},
        name=name,
        compiler_params=pltpu.CompilerParams(dimension_semantics=sem, vmem_limit_bytes=VMEM_LIMIT),
    )


def _sds(shape, dtype=F32):
    return jax.ShapeDtypeStruct(shape, dtype)


def _row_spec(tm, width, colblock=0):
    return pl.BlockSpec((tm, width), lambda i, _c=colblock: (i, _c))


def _full_spec(shape):
    nd = len(shape)
    return pl.BlockSpec(shape, lambda *_: (0,) * nd)


def _zero_at_first(refs, first):
    @pl.when(first)
    def _():
        for r in refs:
            r[...] = jnp.zeros(r.shape, r.dtype)


def _pick(n, prefs):
    for p in prefs:
        if n % p == 0:
            return p
    return n


def _matmul(name, a, b, mode, out_dtype, tiles=None, part=None):
    def want(i, dim):
        return [tiles[i]] if tiles is not None and dim % tiles[i] == 0 else []

    if mode == "tn":
        r, m = a.shape
        n = b.shape[1]
        tm = _pick(m, want(0, m) + [1024, 1408])
        tn = _pick(n, want(1, n) + [512, 256, 128])
        tk = _pick(r, want(2, r) + [1024, 512, 256, 128, 64])
        in_specs = [pl.BlockSpec((tk, tm), lambda i, j, k: (k, i)), pl.BlockSpec((tk, tn), lambda i, j, k: (k, j))]
        sem = ("parallel", "parallel", "arbitrary")
        if part is None:
            nc = _pick(tn, (512, 256, 128))

            def body(a_ref, b_ref, o_ref):
                _zero_at_first([o_ref], pl.program_id(2) == 0)
                for c0 in range(0, tn, nc):
                    o_ref[:, c0:c0 + nc] += _bdot(a_ref[...], b_ref[:, c0:c0 + nc], TN)

            return _pcall(name, body, (m // tm, n // tn, r // tk), in_specs, pl.BlockSpec((tm, tn), lambda i, j, k: (i, j)),
                          _sds((m, n), out_dtype), sem=sem)(a, b)
        n_total, col_off, into = part
        width = n_total // N_CHIPS
        per_step = tn // width
        off = col_off // tn
        assert tn % width == 0 and col_off % tn == 0

        def body(a_ref, b_ref, *rest):
            o_ref = rest[-1]
            _zero_at_first([o_ref], pl.program_id(2) == 0)
            for q in range(per_step):
                o_ref[q] += _bdot(a_ref[...], b_ref[:, q * width:(q + 1) * width], TN)

        args = (a, b) if into is None else (a, b, into)
        return _pcall(
            name, body, (m // tm, n // tn, r // tk), in_specs + ([] if into is None else [_ANY]),
            pl.BlockSpec((per_step, tm, width), lambda i, j, k: (j + off, i, 0)), _sds((N_CHIPS, m, width), out_dtype),
            sem=sem, aliases=None if into is None else {2: 0})(*args)
    m, k = a.shape
    n = b.shape[1] if mode == "nn" else b.shape[0]
    tm = _pick(m, want(0, m) + ([1024, 512, 256, 128, 64] if k <= 2816 else [512, 256, 128, 64]))
    tn = _pick(n, want(1, n) + [512, 256, 128])
    dims = NN if mode == "nn" else NT

    nc = _pick(tn, (512, 256, 128))

    def body(a_ref, b_ref, o_ref):
        for c0 in range(0, tn, nc):
            b_blk = b_ref[:, c0:c0 + nc] if mode == "nn" else b_ref[c0:c0 + nc, :]
            o_ref[:, c0:c0 + nc] = _bdot(a_ref[...], b_blk, dims).astype(o_ref.dtype)

    b_spec = pl.BlockSpec((k, tn), lambda i, j: (0, j)) if mode == "nn" else pl.BlockSpec((tn, k), lambda i, j: (j, 0))
    return _pcall(
        name, body, (m // tm, n // tn), [pl.BlockSpec((tm, k), lambda i, j: (i, 0)), b_spec],
        pl.BlockSpec((tm, tn), lambda i, j: (i, j)), _sds((m, n), out_dtype), sem=("parallel", "parallel"))(a, b)


def _matmul_nt_scattering(name, a, b, tm_pref, scatter):
    m, k = a.shape
    n = b.shape[0]
    tm = _pick(m, (tm_pref, 256, 128, 64))
    nc = _pick(n, (512, 256, 128))
    ns = len(scatter)
    steps = m // tm

    def body(a_ref, b_ref, *rest):
        sc_in, o_ref, sc_out, sems = rest[:ns], rest[ns], rest[ns + 1:2 * ns + 1], rest[2 * ns + 1:]
        sc_start, sc_finish = _scatter_ops(sc_in, sc_out, sems)
        pl.when(pl.program_id(0) == 0)(sc_start)
        for c0 in range(0, n, nc):
            o_ref[:, c0:c0 + nc] = _bdot(a_ref[...], b_ref[c0:c0 + nc, :], NT)
        pl.when(pl.program_id(0) == steps - 1)(sc_finish)

    outs = pl.pallas_call(
        body, name=name, grid=(steps,),
        in_specs=[pl.BlockSpec((tm, k), lambda i: (i, 0)), _full_spec(b.shape)] + [_ANY] * ns,
        out_specs=[pl.BlockSpec((tm, n), lambda i: (i, 0))] + [_ANY] * ns,
        out_shape=[_sds((m, n))] + [_sds((3,) + s.shape[1:], s.dtype) for s in scatter],
        scratch_shapes=_scatter_sems(ns),
        compiler_params=pltpu.CompilerParams(dimension_semantics=("arbitrary",), vmem_limit_bytes=VMEM_LIMIT,
                                             has_side_effects=True))(a, b, *scatter)
    return outs[0], outs[1:]


def _matmul_nt_split(name, a1, a2, b, tm_pref):
    m, kh = a1.shape
    n = b.shape[0]
    tm = _pick(m, (tm_pref, 512, 256, 128, 64))
    nc = _pick(n, (512, 256, 128))

    def body(a1_ref, a2_ref, b_ref, o_ref):
        for c0 in range(0, n, nc):
            o_ref[:, c0:c0 + nc] = (_bdot(a1_ref[...], b_ref[c0:c0 + nc, :kh], NT)
                                    + _bdot(a2_ref[...], b_ref[c0:c0 + nc, kh:], NT))

    aspec = pl.BlockSpec((tm, kh), lambda i: (i, 0))
    return _pcall(name, body, (m // tm,), [aspec, aspec, _full_spec(b.shape)], pl.BlockSpec((tm, n), lambda i: (i, 0)),
                  _sds((m, n)), sem=("parallel",))(a1, a2, b)


def _row_tile(t):
    return _pick(t, (512, 256, 128, 64))


def _rms_fwd_gathering(name, x, g, shards, split):
    t = x.shape[0]
    tm = next(c for c in (128, 64, 32, 16) if t % c == 0 and t // c >= 2)
    steps = t // tm
    ns = len(shards)

    def body(x_ref, g_ref, *rest):
        ins, h_ref, outs, sems = rest[:ns], rest[ns], rest[ns + 1:2 * ns + 1], rest[2 * ns + 1:]
        start, forward, finish = _gather_ops(ins, outs, sems, split)
        pl.when(pl.program_id(0) == 0)(start)
        h_ref[...] = _rms(x_ref[...], g_ref[...]).astype(BF16)
        pl.when(pl.program_id(0) == steps - 2)(forward)
        pl.when(pl.program_id(0) == steps - 1)(finish)

    assert steps >= 2
    outs = pl.pallas_call(
        body, name=name, grid=(steps,),
        in_specs=[_row_spec(tm, D_MODEL), _full_spec((1, D_MODEL))] + [_ANY] * ns,
        out_specs=[_row_spec(tm, D_MODEL)] + [_ANY] * ns,
        out_shape=[_sds((t, D_MODEL), BF16)] + [_sds((N_CHIPS,) + s.shape, s.dtype) for s in shards],
        scratch_shapes=_gather_sems(ns),
        compiler_params=pltpu.CompilerParams(dimension_semantics=("arbitrary",), vmem_limit_bytes=VMEM_LIMIT,
                                             has_side_effects=True))(x, g, *shards)
    return outs[0], outs[1:]


_G_LO, _G_HI = GDN_HEADS, 2 * GDN_HEADS
_S_LO, _S_HI = 2 * GDN_HEADS, 2 * GDN_HEADS + SSD_HEADS


def _gates_fn(small, bias, a_log):
    tm = small.shape[0]
    rr, cc = _iota2((tm, tm))
    in_chunk_tril = (((rr >> 6) == (cc >> 6)) & (rr >= cc)).astype(F32)
    lane = lax.broadcasted_iota(jnp.int32, small.shape, 1)
    sp = jax.nn.softplus(small + bias)
    act = jnp.where(lane < _G_LO, jax.nn.sigmoid(small), sp)
    cum = _lin_left(in_chunk_tril)(-jnp.exp(a_log) * sp)
    r, c = _iota2((SMALL, D_MODEL))
    to_ssd_lanes = _lin_right((r == _S_LO + (c >> 6)).astype(F32))
    return act, cum, to_ssd_lanes(act), to_ssd_lanes(cum)


def _expand_lanes(src_ref, dst_ref, lo, heads, width):
    rows = src_ref.shape[0]
    for h in range(heads):
        dst_ref[:, h * width:(h + 1) * width] = jnp.broadcast_to(src_ref[:, lo + h:lo + h + 1], (rows, width))


def _reduce_lanes(wide_ref, lo, heads, width):
    rows = wide_ref.shape[0]
    lane = lax.broadcasted_iota(jnp.int32, (rows, SMALL), 1)
    acc = jnp.zeros((rows, SMALL), F32)
    for h in range(heads):
        col = jnp.sum(wide_ref[:, h * width:(h + 1) * width], axis=-1, keepdims=True)
        acc = jnp.where(lane == lo + h, jnp.broadcast_to(col, (rows, SMALL)), acc)
    return acc


def _gates_fwd(small, bias, a_log):
    t = small.shape[0]
    tm = _row_tile(t)

    def body(s_ref, p0, p1, act_ref, cum_ref, dt_ref, acs_ref):
        act_ref[...], cum_ref[...], dt_ref[...], acs_ref[...] = _gates_fn(s_ref[...], p0[...], p1[...])

    pspec, nspec, wspec = _full_spec((1, SMALL)), _row_spec(tm, SMALL), _row_spec(tm, D_MODEL)
    return _pcall("gates_fwd", body, (t // tm,), [nspec, pspec, pspec], [nspec, nspec, wspec, wspec],
                  [_sds((t, SMALL))] * 2 + [_sds((t, D_MODEL))] * 2, sem=("parallel",))(small, bias, a_log)


def _gates_bwd(small, bias, a_log, d_act, d_cums, d_dt, d_acs):
    t = small.shape[0]
    tm = _row_tile(t)
    nc = len(d_cums)

    def body(*refs):
        s_ref, p0, p1, dact_ref = refs[:4]
        dcum_refs = refs[4:4 + nc]
        ddt_ref, dacs_ref, ds_ref, db_ref, da_ref = refs[4 + nc:]
        _zero_at_first([db_ref, da_ref], pl.program_id(0) == 0)
        _, vjp = jax.vjp(_gates_fn, s_ref[...], p0[...], p1[...])
        d_cum = dcum_refs[0][...]
        for c in dcum_refs[1:]:
            d_cum = d_cum + c[...]
        d_s, d_b, d_a = vjp((dact_ref[...], d_cum, ddt_ref[...], dacs_ref[...]))
        ds_ref[...] = d_s.astype(BF16)
        db_ref[...] += d_b
        da_ref[...] += d_a

    pspec, nspec, wspec = _full_spec((1, SMALL)), _row_spec(tm, SMALL), _row_spec(tm, D_MODEL)
    return _pcall("gates_bwd", body, (t // tm,), [nspec, pspec, pspec] + [nspec] * (1 + nc) + [wspec, wspec],
                  [nspec, pspec, pspec], [_sds((t, SMALL), BF16), _sds((1, SMALL)), _sds((1, SMALL))])(
                      small, bias, a_log, d_act, *d_cums, d_dt, d_acs)


def _gdn_out_fn(o, z, w):
    return _rms(o, w) * _silu(z)


def _gdn_out_fwd(o, proj, gn):
    t = o.shape[0]
    tm = _row_tile(t)

    def body(o_ref, z_ref, w_ref, y_ref):
        for h in range(GDN_HEADS):
            sl = slice(h * GDN_DK, (h + 1) * GDN_DK)
            y_ref[:, sl] = _gdn_out_fn(o_ref[:, sl], z_ref[:, sl].astype(F32), w_ref[...]).astype(BF16)

    return _pcall("gdn_out_fwd", body, (t // tm,), [_row_spec(tm, GDN_V), _row_spec(tm, GDN_V, 3), _full_spec((1, GDN_DK))],
                  _row_spec(tm, GDN_V), _sds((t, GDN_V + SSD_D), BF16), sem=("parallel",))(o, proj, gn)


def _gdn_out_bwd(o, proj, gn, d_ocat):
    t = o.shape[0]
    tm = _row_tile(t)

    def body(o_ref, z_ref, w_ref, dy_ref, do_ref, dz_ref, dw_ref):
        _zero_at_first([dw_ref], pl.program_id(0) == 0)
        for h in range(GDN_HEADS):
            sl = slice(h * GDN_DK, (h + 1) * GDN_DK)
            _, vjp = jax.vjp(_gdn_out_fn, o_ref[:, sl], z_ref[:, sl].astype(F32), w_ref[...])
            d_o, d_z, d_w = vjp(dy_ref[:, sl])
            do_ref[:, sl] = d_o
            dz_ref[:, sl] = d_z.astype(BF16)
            dw_ref[...] += d_w

    return _pcall("gdn_out_bwd", body, (t // tm,),
                  [_row_spec(tm, GDN_V), _row_spec(tm, GDN_V, 3), _full_spec((1, GDN_DK)), _row_spec(tm, GDN_V, 0)],
                  [_row_spec(tm, GDN_V), _row_spec(tm, GDN_V, 3), _full_spec((1, GDN_DK))],
                  [_sds((t, GDN_V)), _sds((t, BIG), BF16), _sds((1, GDN_DK))])(o, proj, gn, d_ocat)


def _ssd_out_fn(y, xs, z, d_skip, w):
    return _rms((y + d_skip * xs) * _silu(z), w)


_SSD_GW = SSD_D // SSD_GROUPS


def _ssd_out_fwd(y, xbc, proj, d_skip, nw, ocat):
    t = y.shape[0]
    tm = _row_tile(t)

    def body(y_ref, x_ref, z_ref, d_ref, w_ref, _, o_ref):
        for gi in range(SSD_GROUPS):
            sl = slice(gi * _SSD_GW, (gi + 1) * _SSD_GW)
            o_ref[:, sl] = _ssd_out_fn(y_ref[:, sl], x_ref[:, sl], z_ref[:, sl].astype(F32), d_ref[:, sl], w_ref[:, sl]).astype(BF16)

    pspec = _full_spec((1, SSD_D))
    return _pcall("ssd_out_fwd", body, (t // tm,),
                  [_row_spec(tm, SSD_D), _row_spec(tm, SSD_D, 0), _row_spec(tm, SSD_D, 4), pspec, pspec, _ANY],
                  _row_spec(tm, SSD_D, 1), _sds(ocat.shape, BF16), sem=("parallel",), aliases={5: 0})(
                      y, xbc, proj, d_skip, nw, ocat)


def _ssd_out_bwd(y, xbc, proj, d_skip, nw, d_ocat, d_proj):
    t = y.shape[0]
    tm = _row_tile(t)

    def body(y_ref, x_ref, z_ref, d_ref, w_ref, do_ref, _, dy_ref, dx_ref, dz_ref, dd_ref, dw_ref):
        _zero_at_first([dd_ref, dw_ref], pl.program_id(0) == 0)
        for gi in range(SSD_GROUPS):
            sl = slice(gi * _SSD_GW, (gi + 1) * _SSD_GW)
            _, vjp = jax.vjp(_ssd_out_fn, y_ref[:, sl], x_ref[:, sl], z_ref[:, sl].astype(F32), d_ref[:, sl], w_ref[:, sl])
            d_y, d_x, d_z, d_d, d_w = vjp(do_ref[:, sl])
            dy_ref[:, sl] = d_y
            dx_ref[:, sl] = d_x
            dz_ref[:, sl] = d_z.astype(BF16)
            dd_ref[:, sl] += d_d
            dw_ref[:, sl] += d_w

    pspec = _full_spec((1, SSD_D))
    row = _row_spec(tm, SSD_D)
    return _pcall("ssd_out_bwd", body, (t // tm,),
                  [row, _row_spec(tm, SSD_D, 0), _row_spec(tm, SSD_D, 4), pspec, pspec, _row_spec(tm, SSD_D, 1), _ANY],
                  [row, row, _row_spec(tm, SSD_D, 4), pspec, pspec],
                  [_sds((t, SSD_D)), _sds((t, SSD_D)), _sds((t, BIG), BF16), _sds((1, SSD_D)), _sds((1, SSD_D))],
                  aliases={6: 2})(y, xbc, proj, d_skip, nw, d_ocat, d_proj)


def _res1_fn(x, mix, g_pm, g_pf):
    x1 = x + _rms(mix, g_pm)
    return x1, _rms(x1, g_pf)


def _res1_fwd(x, mix, g_pm, g_pf):
    t = x.shape[0]
    tm = _row_tile(t)

    def body(x_ref, m_ref, a_ref, b_ref, x1_ref, h2_ref):
        x1, h2 = _res1_fn(x_ref[...], m_ref[...], a_ref[...], b_ref[...])
        x1_ref[...] = x1
        h2_ref[...] = h2.astype(BF16)

    row, pspec = _row_spec(tm, D_MODEL), _full_spec((1, D_MODEL))
    return _pcall("res1_fwd", body, (t // tm,), [row, row, pspec, pspec], [row, row],
                  [_sds((t, D_MODEL)), _sds((t, D_MODEL), BF16)], sem=("parallel",))(x, mix, g_pm, g_pf)


def _res1_bwd(x, mix, g_pm, g_pf, d_x1, d_h2):
    t = x.shape[0]
    tm = _row_tile(t)

    def body(x_ref, m_ref, a_ref, b_ref, c1_ref, c2_ref, dx_ref, dm_ref, da_ref, db_ref):
        _zero_at_first([da_ref, db_ref], pl.program_id(0) == 0)
        _, vjp = jax.vjp(_res1_fn, x_ref[...], m_ref[...], a_ref[...], b_ref[...])
        d_x, d_m, d_a, d_b = vjp((c1_ref[...], c2_ref[...]))
        dx_ref[...] = d_x
        dm_ref[...] = d_m.astype(BF16)
        da_ref[...] += d_a
        db_ref[...] += d_b

    row, pspec = _row_spec(tm, D_MODEL), _full_spec((1, D_MODEL))
    return _pcall("res1_bwd", body, (t // tm,), [row, row, pspec, pspec, row, row], [row, row, pspec, pspec],
                  [_sds((t, D_MODEL)), _sds((t, D_MODEL), BF16), _sds((1, D_MODEL)), _sds((1, D_MODEL))])(
                      x, mix, g_pm, g_pf, d_x1, d_h2)


def _final_fn(x1, f, g_po, tgt):
    err = x1 + _rms(f, g_po) - tgt
    return 0.5 * jnp.sum(jnp.mean(err * err, axis=-1))


def _final(x1, f, g_po, tgt):
    t = x1.shape[0]
    tm = _row_tile(t)

    def body(x_ref, f_ref, g_ref, t_ref, loss_ref, dx_ref, df_ref, dg_ref):
        _zero_at_first([loss_ref, dg_ref], pl.program_id(0) == 0)
        loss, (d_x, d_f, d_g) = jax.value_and_grad(_final_fn, argnums=(0, 1, 2))(x_ref[...], f_ref[...], g_ref[...], t_ref[...])
        loss_ref[...] += jnp.broadcast_to(loss, loss_ref.shape)
        dx_ref[...] = d_x
        df_ref[...] = d_f.astype(BF16)
        dg_ref[...] += d_g

    row, pspec = _row_spec(tm, D_MODEL), _full_spec((1, D_MODEL))
    return _pcall("final", body, (t // tm,), [row, row, pspec, row], [_full_spec((8, LANE)), row, row, pspec],
                  [_sds((8, LANE)), _sds((t, D_MODEL)), _sds((t, D_MODEL), BF16), _sds((1, D_MODEL))])(x1, f, g_po, tgt)


def _rms1_bwd(x, g, d_h_a, d_h_b, d_x1):
    t = x.shape[0]
    tm = _row_tile(t)

    def body(x_ref, g_ref, dha_ref, dhb_ref, dx1_ref, dx_ref, dg_ref):
        _zero_at_first([dg_ref], pl.program_id(0) == 0)
        _, vjp = jax.vjp(_rms, x_ref[...], g_ref[...])
        d_x, d_g = vjp(dha_ref[...] + dhb_ref[...])
        dx_ref[...] = d_x + dx1_ref[...]
        dg_ref[...] += d_g

    row, pspec = _row_spec(tm, D_MODEL), _full_spec((1, D_MODEL))
    return _pcall("rms1_bwd", body, (t // tm,), [row, pspec, row, row, row], [row, pspec],
                  [_sds((t, D_MODEL)), _sds((1, D_MODEL))])(x, g, d_h_a, d_h_b, d_x1)


def _qkv_fn(mode):
    def fn(x, *wrows):
        y = _silu(_conv(x, wrows))
        if mode == "q":
            return _l2n(y) * (GDN_DK ** -0.5)
        if mode == "k":
            return _l2n(y)
        return y
    return fn


def _seq_spec(s, tc, off):
    return pl.BlockSpec((s, tc), lambda j, b, _o=off: (b, _o + j))


def _par_spec(rows, tc, off):
    return pl.BlockSpec((rows, tc), lambda j, b, _o=off: (0, _o + j))


def _gdn_conv_fwd(mode, proj, w, bsz, s):
    off = {"q": 0, "k": GDN_HEADS, "v": 2 * GDN_HEADS}[mode]
    fn = _qkv_fn(mode)

    def body(x_ref, w_ref, y_ref):
        y_ref[...] = fn(x_ref[...].astype(F32), *[w_ref[k:k + 1, :] for k in range(CONV_K)])

    return _pcall("gdn_conv_fwd_" + mode, body, (GDN_HEADS, bsz),
                  [_seq_spec(s, GDN_DK, off), _par_spec(CONV_K, GDN_DK, off)], _seq_spec(s, GDN_DK, 0),
                  _sds((bsz * s, GDN_QK)), sem=("parallel", "parallel"))(proj, w)


def _gdn_conv_bwd(mode, proj, w, d_y, d_proj, bsz, s):
    off = {"q": 0, "k": GDN_HEADS, "v": 2 * GDN_HEADS}[mode]
    fn = _qkv_fn(mode)

    def body(x_ref, w_ref, dy_ref, _, dx_ref, dw_ref):
        _zero_at_first([dw_ref], pl.program_id(1) == 0)
        _, vjp = jax.vjp(fn, x_ref[...].astype(F32), *[w_ref[k:k + 1, :] for k in range(CONV_K)])
        grads = vjp(dy_ref[...])
        dx_ref[...] = grads[0].astype(BF16)
        for k in range(CONV_K):
            dw_ref[k:k + 1, :] += grads[1 + k]

    return _pcall("gdn_conv_bwd_" + mode, body, (GDN_HEADS, bsz),
                  [_seq_spec(s, GDN_DK, off), _par_spec(CONV_K, GDN_DK, off), _seq_spec(s, GDN_DK, 0), _ANY],
                  [_seq_spec(s, GDN_DK, off), _par_spec(CONV_K, GDN_DK, 0)],
                  [_sds(d_proj.shape, BF16), _sds((CONV_K, GDN_QK))], sem=("parallel", "arbitrary"), aliases={3: 0})(
                      proj, w, d_y, d_proj)


def _ssd_conv_fn(x, bias, *wrows):
    return _silu(_conv(x, wrows) + bias)


_XBC_OFF = (5 * 1024) // LANE


def _ssd_conv_fwd(proj, w, bias, bsz, s):
    nt_ = SSD_CONV_CH // LANE

    def body(x_ref, w_ref, b_ref, y_ref):
        y_ref[...] = _ssd_conv_fn(x_ref[...].astype(F32), b_ref[...], *[w_ref[k:k + 1, :] for k in range(CONV_K)])

    return _pcall("ssd_conv_fwd", body, (nt_, bsz),
                  [_seq_spec(s, LANE, _XBC_OFF), _par_spec(CONV_K, LANE, 0), _par_spec(1, LANE, 0)], _seq_spec(s, LANE, 0),
                  _sds((bsz * s, SSD_CONV_CH)), sem=("parallel", "parallel"))(proj, w, bias)


def _ssd_conv_bwd(proj, w, bias, d_y, d_proj, bsz, s):
    nt_ = SSD_CONV_CH // LANE

    def body(x_ref, w_ref, b_ref, dy_ref, _, dx_ref, dw_ref, db_ref):
        _zero_at_first([dw_ref, db_ref], pl.program_id(1) == 0)
        _, vjp = jax.vjp(_ssd_conv_fn, x_ref[...].astype(F32), b_ref[...], *[w_ref[k:k + 1, :] for k in range(CONV_K)])
        grads = vjp(dy_ref[...])
        dx_ref[...] = grads[0].astype(BF16)
        db_ref[...] += grads[1]
        for k in range(CONV_K):
            dw_ref[k:k + 1, :] += grads[2 + k]

    return _pcall("ssd_conv_bwd", body, (nt_, bsz),
                  [_seq_spec(s, LANE, _XBC_OFF), _par_spec(CONV_K, LANE, 0), _par_spec(1, LANE, 0), _seq_spec(s, LANE, 0), _ANY],
                  [_seq_spec(s, LANE, _XBC_OFF), _par_spec(CONV_K, LANE, 0), _par_spec(1, LANE, 0)],
                  [_sds(d_proj.shape, BF16), _sds((CONV_K, SSD_CONV_CH)), _sds((1, SSD_CONV_CH))],
                  sem=("parallel", "arbitrary"), aliases={4: 0})(proj, w, bias, d_y, d_proj)


_FFN_TC = 256
_FFN_NT = D_FF // _FFN_TC


def _ffn_act_fn(xg, xu, bg, bu, *wrows):
    k_w = FFN_CONV_K
    gate = _conv(xg, wrows[:k_w]) + bg
    up = _conv(xu, wrows[k_w:]) + bu
    return _silu(gate) * up


def _ffn_act_fwd(u_pre, w, bias, bsz, s):
    def body(xg_ref, xu_ref, wg_ref, wu_ref, bg_ref, bu_ref, a_ref):
        rows = [wg_ref[k:k + 1, :] for k in range(FFN_CONV_K)] + [wu_ref[k:k + 1, :] for k in range(FFN_CONV_K)]
        a_ref[...] = _ffn_act_fn(xg_ref[...].astype(F32), xu_ref[...].astype(F32), bg_ref[...], bu_ref[...], *rows).astype(BF16)

    return _pcall("ffn_act_fwd", body, (_FFN_NT, bsz),
                  [_seq_spec(s, _FFN_TC, 0), _seq_spec(s, _FFN_TC, _FFN_NT),
                   _par_spec(FFN_CONV_K, _FFN_TC, 0), _par_spec(FFN_CONV_K, _FFN_TC, _FFN_NT),
                   _par_spec(1, _FFN_TC, 0), _par_spec(1, _FFN_TC, _FFN_NT)],
                  _seq_spec(s, _FFN_TC, 0), _sds((bsz * s, D_FF), BF16), sem=("parallel", "parallel"))(
                      u_pre, u_pre, w, w, bias, bias)


def _ffn_act_bwd(u_pre, w, bias, d_a, bsz, s):
    def body(xg_ref, xu_ref, wg_ref, wu_ref, bg_ref, bu_ref, da_ref, dg_ref, du_ref, dwg_ref, dwu_ref, dbg_ref, dbu_ref):
        _zero_at_first([dwg_ref, dwu_ref, dbg_ref, dbu_ref], pl.program_id(1) == 0)
        rows = [wg_ref[k:k + 1, :] for k in range(FFN_CONV_K)] + [wu_ref[k:k + 1, :] for k in range(FFN_CONV_K)]
        _, vjp = jax.vjp(_ffn_act_fn, xg_ref[...].astype(F32), xu_ref[...].astype(F32), bg_ref[...], bu_ref[...], *rows)
        grads = vjp(da_ref[...])
        dg_ref[...] = grads[0].astype(BF16)
        du_ref[...] = grads[1].astype(BF16)
        dbg_ref[...] += grads[2]
        dbu_ref[...] += grads[3]
        for k in range(FFN_CONV_K):
            dwg_ref[k:k + 1, :] += grads[4 + k]
            dwu_ref[k:k + 1, :] += grads[4 + FFN_CONV_K + k]

    seq0, par3, par1 = _seq_spec(s, _FFN_TC, 0), _par_spec(FFN_CONV_K, _FFN_TC, 0), _par_spec(1, _FFN_TC, 0)
    return _pcall("ffn_act_bwd", body, (_FFN_NT, bsz),
                  [seq0, _seq_spec(s, _FFN_TC, _FFN_NT), par3, _par_spec(FFN_CONV_K, _FFN_TC, _FFN_NT),
                   par1, _par_spec(1, _FFN_TC, _FFN_NT), seq0],
                  [seq0, seq0, par3, par3, par1, par1],
                  [_sds((bsz * s, D_FF), BF16), _sds((bsz * s, D_FF), BF16), _sds((FFN_CONV_K, D_FF)), _sds((FFN_CONV_K, D_FF)),
                   _sds((1, D_FF)), _sds((1, D_FF))], sem=("parallel", "arbitrary"))(u_pre, u_pre, w, w, bias, bias, d_a)


_GP = GDN_HEADS // 2
_SP = SSD_HEADS // 2


def _pair_lanes(p):
    return slice(2 * p * LANE, (2 * p + 1) * LANE), slice((2 * p + 1) * LANE, (2 * p + 2) * LANE)


_LAST = slice(CHUNK - 1, CHUNK)


def _gdn_args(p, q_ref, k_ref, v_ref, g_ref, b_ref, gr_ref):
    la, lb = _pair_lanes(p)
    return (q_ref[:, la], q_ref[:, lb], k_ref[:, la], k_ref[:, lb], v_ref[:, la], v_ref[:, lb], g_ref[:, la], g_ref[:, lb],
            b_ref[:, la], b_ref[:, lb], gr_ref[p], g_ref[_LAST, la], g_ref[_LAST, lb])


def _gdn_fwd(q, k, v, act, cum, gc_row, bsz, n):
    def body(q_ref, k_ref, v_ref, act_ref, cum_ref, gr_ref, o_ref, sin_ref, s_scr, g_ref, b_ref):
        _zero_at_first([s_scr], pl.program_id(1) == 0)
        _expand_lanes(act_ref, b_ref, 0, GDN_HEADS, GDN_DK)
        _expand_lanes(cum_ref, g_ref, _G_LO, GDN_HEADS, GDN_DK)
        flat = []
        for p in range(_GP):
            flat += [*_gdn_args(p, q_ref, k_ref, v_ref, g_ref, b_ref, gr_ref), s_scr[2 * p], s_scr[2 * p + 1]]
        sin_ref[...] = s_scr[...]
        outs = _gdn_multi(*flat)
        for p in range(_GP):
            la, lb = _pair_lanes(p)
            o_ref[:, la], o_ref[:, lb], s_scr[2 * p], s_scr[2 * p + 1] = outs[4 * p:4 * p + 4]

    tspec = pl.BlockSpec((CHUNK, GDN_V), lambda b, c: (b * n + c, 0))
    rspec = pl.BlockSpec((_GP, 1, LANE), lambda b, c: (b * n + c, 0, 0))
    sspec = pl.BlockSpec((GDN_HEADS, LANE, LANE), lambda b, c: (b * n + c, 0, 0))
    nspec = pl.BlockSpec((CHUNK, SMALL), lambda b, c: (b * n + c, 0))
    wide = pltpu.VMEM((CHUNK, GDN_V), F32)
    return _pcall("gdn_fwd", body, (bsz, n), [tspec] * 3 + [nspec, nspec, rspec], [tspec, sspec],
                  [_sds((bsz * n * CHUNK, GDN_V)), _sds((bsz * n * GDN_HEADS, LANE, LANE))],
                  scratch=[pltpu.VMEM((GDN_HEADS, LANE, LANE), F32), wide, wide], sem=("parallel", "arbitrary"))(
                      q, k, v, act, cum, gc_row)


def _gdn_bwd(q, k, v, act, cum, gc_row, s_in, d_o, bsz, n, scatter):
    ns = len(scatter)

    def body(q_ref, k_ref, v_ref, act_ref, cum_ref, gr_ref, sin_ref, do_ref, *rest):
        sc_in, rest = rest[:ns], rest[ns:]
        dq_ref, dk_ref, dv_ref, dact_ref, dcum_ref, dgr_ref = rest[:6]
        sc_out, rest = rest[6:6 + ns], rest[6 + ns:]
        ds_scr, g_ref, b_ref, dg_ref, db_ref, send_sems, recv_sems = rest
        step = pl.program_id(0) * n + pl.program_id(1)
        sc_start, sc_finish = _scatter_ops(sc_in, sc_out, (send_sems, recv_sems))
        pl.when(step == 0)(sc_start)
        _zero_at_first([ds_scr], pl.program_id(1) == 0)
        _expand_lanes(act_ref, b_ref, 0, GDN_HEADS, GDN_DK)
        _expand_lanes(cum_ref, g_ref, _G_LO, GDN_HEADS, GDN_DK)
        flat, cots = [], []
        for p in range(_GP):
            la, lb = _pair_lanes(p)
            flat += [*_gdn_args(p, q_ref, k_ref, v_ref, g_ref, b_ref, gr_ref), sin_ref[2 * p], sin_ref[2 * p + 1]]
            cots += [do_ref[:, la], do_ref[:, lb], ds_scr[2 * p], ds_scr[2 * p + 1]]
        _, vjp = jax.vjp(_gdn_multi, *flat)
        grads = vjp(tuple(cots))
        for p in range(_GP):
            la, lb = _pair_lanes(p)
            cts = grads[_GDN_NARGS * p:_GDN_NARGS * (p + 1)]
            for ref, i in ((dq_ref, 0), (dk_ref, 2), (dv_ref, 4), (dg_ref, 6), (db_ref, 8)):
                ref[:, la] = cts[i]
                ref[:, lb] = cts[i + 1]
            dgr_ref[p] = cts[10]
            dg_ref[_LAST, la] += cts[11]
            dg_ref[_LAST, lb] += cts[12]
            ds_scr[2 * p] = cts[13]
            ds_scr[2 * p + 1] = cts[14]
        dact_ref[...] = _reduce_lanes(db_ref, 0, GDN_HEADS, GDN_DK)
        dcum_ref[...] = _reduce_lanes(dg_ref, _G_LO, GDN_HEADS, GDN_DK)
        pl.when(step == bsz * n - 1)(sc_finish)

    tspec = pl.BlockSpec((CHUNK, GDN_V), lambda b, c: (b * n + (n - 1 - c), 0))
    nspec = pl.BlockSpec((CHUNK, SMALL), lambda b, c: (b * n + (n - 1 - c), 0))
    rspec = pl.BlockSpec((_GP, 1, LANE), lambda b, c: (b * n + (n - 1 - c), 0, 0))
    sspec = pl.BlockSpec((GDN_HEADS, LANE, LANE), lambda b, c: (b * n + (n - 1 - c), 0, 0))
    tok_shape, nar_shape = _sds((bsz * n * CHUNK, GDN_V)), _sds((bsz * n * CHUNK, SMALL))
    wide = pltpu.VMEM((CHUNK, GDN_V), F32)
    outs = pl.pallas_call(
        body, name="gdn_bwd", grid=(bsz, n),
        in_specs=[tspec] * 3 + [nspec, nspec, rspec, sspec, tspec] + [_ANY] * ns,
        out_specs=[tspec] * 3 + [nspec, nspec, rspec] + [_ANY] * ns,
        out_shape=[tok_shape] * 3 + [nar_shape, nar_shape, _sds((bsz * n * _GP, 1, LANE))]
        + [_sds((3,) + a.shape[1:], a.dtype) for a in scatter],
        scratch_shapes=[pltpu.VMEM((GDN_HEADS, LANE, LANE), F32), wide, wide, wide, wide] + _scatter_sems(ns),
        compiler_params=pltpu.CompilerParams(dimension_semantics=("arbitrary", "arbitrary"), vmem_limit_bytes=VMEM_LIMIT,
                                             has_side_effects=True))(q, k, v, act, cum, gc_row, s_in, d_o, *scatter)
    return outs[:6], outs[6:]


_B_OFF = SSD_D // LANE
_C_OFF = (SSD_D + SSD_BC) // LANE
_PPG = _SP // SSD_GROUPS


def _ssd_args(p, x_ref, dt_ref, a_ref, ar_ref):
    lp = slice(p * LANE, (p + 1) * LANE)
    gi = p // _PPG
    b_sl = slice((_B_OFF + gi) * LANE, (_B_OFF + gi + 1) * LANE)
    c_sl = slice((_C_OFF + gi) * LANE, (_C_OFF + gi + 1) * LANE)
    return (x_ref[:, lp], dt_ref[:, lp], a_ref[:, lp], ar_ref[p], a_ref[_LAST, lp], x_ref[:, b_sl], x_ref[:, c_sl])


def _ssd_fwd(xbc, dt, acs, acs_row, bsz, n):
    def body(x_ref, dt_ref, a_ref, ar_ref, y_ref, sin_ref, s_scr):
        _zero_at_first([s_scr], pl.program_id(1) == 0)
        flat = []
        for p in range(_SP):
            flat += [*_ssd_args(p, x_ref, dt_ref, a_ref, ar_ref), s_scr[p]]
        sin_ref[...] = s_scr[...]
        outs = _ssd_multi(*flat)
        for p in range(_SP):
            y_ref[:, p * LANE:(p + 1) * LANE], s_scr[p] = outs[2 * p:2 * p + 2]

    tspec = pl.BlockSpec((CHUNK, SSD_D), lambda b, c: (b * n + c, 0))
    return _pcall("ssd_fwd", body, (bsz, n),
                  [pl.BlockSpec((CHUNK, SSD_CONV_CH), lambda b, c: (b * n + c, 0)), tspec, tspec,
                   pl.BlockSpec((_SP, 1, LANE), lambda b, c: (b * n + c, 0, 0))],
                  [tspec, pl.BlockSpec((_SP, LANE, LANE), lambda b, c: (b * n + c, 0, 0))],
                  [_sds((bsz * n * CHUNK, SSD_D)), _sds((bsz * n * _SP, LANE, LANE))],
                  scratch=[pltpu.VMEM((_SP, LANE, LANE), F32)], sem=("parallel", "arbitrary"))(xbc, dt, acs, acs_row)


def _ssd_bwd(xbc, dt, acs, acs_row, s_in, d_y, d_x_skip, bsz, n):
    def body(x_ref, dt_ref, a_ref, ar_ref, sin_ref, dy_ref, dsk_ref, dx_ref, ddt_ref, da_ref, dar_ref, ds_scr):
        _zero_at_first([ds_scr], pl.program_id(1) == 0)
        d_b = [None] * SSD_GROUPS
        d_c = [None] * SSD_GROUPS
        flat, cots = [], []
        for p in range(_SP):
            flat += [*_ssd_args(p, x_ref, dt_ref, a_ref, ar_ref), sin_ref[p]]
            cots += [dy_ref[:, p * LANE:(p + 1) * LANE], ds_scr[p]]
        _, vjp = jax.vjp(_ssd_multi, *flat)
        grads = vjp(tuple(cots))
        for p in range(_SP):
            lp = slice(p * LANE, (p + 1) * LANE)
            gi = p // _PPG
            cts = grads[_SSD_NARGS * p:_SSD_NARGS * (p + 1)]
            dx_ref[:, lp] = cts[0] + dsk_ref[:, lp]
            ddt_ref[:, lp] = cts[1]
            da_ref[:, lp] = cts[2]
            dar_ref[p] = cts[3]
            da_ref[_LAST, lp] += cts[4]
            d_b[gi] = cts[5] if d_b[gi] is None else d_b[gi] + cts[5]
            d_c[gi] = cts[6] if d_c[gi] is None else d_c[gi] + cts[6]
            ds_scr[p] = cts[7]
        for gi in range(SSD_GROUPS):
            dx_ref[:, (_B_OFF + gi) * LANE:(_B_OFF + gi + 1) * LANE] = d_b[gi]
            dx_ref[:, (_C_OFF + gi) * LANE:(_C_OFF + gi + 1) * LANE] = d_c[gi]

    def rev(b, c):
        return b * n + (n - 1 - c)

    tspec = pl.BlockSpec((CHUNK, SSD_D), lambda b, c: (rev(b, c), 0))
    xspec = pl.BlockSpec((CHUNK, SSD_CONV_CH), lambda b, c: (rev(b, c), 0))
    rspec = pl.BlockSpec((_SP, 1, LANE), lambda b, c: (rev(b, c), 0, 0))
    tok_shape = _sds((bsz * n * CHUNK, SSD_D))
    return _pcall("ssd_bwd", body, (bsz, n),
                  [xspec, tspec, tspec, rspec, pl.BlockSpec((_SP, LANE, LANE), lambda b, c: (rev(b, c), 0, 0)), tspec, tspec],
                  [xspec, tspec, tspec, rspec],
                  [_sds((bsz * n * CHUNK, SSD_CONV_CH)), tok_shape, tok_shape, _sds((bsz * n * _SP, 1, LANE))],
                  scratch=[pltpu.VMEM((_SP, LANE, LANE), F32)], sem=("parallel", "arbitrary"))(
                      xbc, dt, acs, acs_row, s_in, d_y, d_x_skip)


def _add2(name, a, b):
    t, c = a.shape
    tm = _row_tile(t)

    def body(a_ref, b_ref, o_ref):
        o_ref[...] = a_ref[...] + b_ref[...]

    return _pcall(name, body, (t // tm,), [_row_spec(tm, c)] * 2, _row_spec(tm, c), _sds((t, c)), sem=("parallel",))(a, b)


def _rep(p, width):
    return jnp.repeat(p.reshape(-1), width).reshape(1, -1)


def _to_rows(narrow, lo, hi, bsz, n):
    heads = hi - lo
    a = narrow[:, lo:hi].reshape(bsz, n, CHUNK, heads)
    return jnp.transpose(a, (0, 1, 3, 2)).reshape(bsz * n * (heads // 2), 1, 2 * CHUNK)


def _from_rows(rows, heads, bsz, n):
    return jnp.transpose(rows.reshape(bsz, n, heads, CHUNK), (0, 1, 3, 2)).reshape(bsz * n * CHUNK, heads)


def _narrow_row(gdn_part, ssd_part):
    return jnp.pad(jnp.concatenate([gdn_part, ssd_part], axis=1), ((0, 0), (_G_LO, SMALL - _S_HI)))


def _local_step(x, tgt, p):
    bsz, s, _ = x.shape
    t = bsz * s
    n = s // CHUNK
    x2 = x.reshape(t, D_MODEL)
    tgt2 = tgt.reshape(t, D_MODEL)
    gate_bias = _narrow_row(p["gdn_dt_bias"], p["ssd_dt_bias"])
    gate_a_log = _narrow_row(p["gdn_a_log"], p["ssd_a_log"])
    d_skip = _rep(p["ssd_d"], SSD_HEADDIM)

    h, (g_in, g_gcw, g_scw, g_fcw) = _rms_fwd_gathering(
        "rms0_fwd", x2, p["pre_mix_norm"], [p["w_in"]] + [p[k] for k in _COL_SHARDED_SMALL], [True, False, False, False])
    w_in = _chips_to_cols(g_in)
    p = dict(p, gdn_conv_w=_chips_to_cols(g_gcw), ssd_conv_w=_chips_to_cols(g_scw), ffn_conv_w=_chips_to_cols(g_fcw))
    w_big = jnp.concatenate([w_in[:, :4096], w_in[:, 4112:6672]], axis=1)
    w_small = jnp.concatenate([w_in[:, 4096:4112], w_in[:, 6672:6688], jnp.zeros((D_MODEL, SMALL - 32), BF16)], axis=1)
    proj, g_out, g_up, g_down = _matmul_nn_gathering(
        "mm_in_big", h, w_big, BF16, (_pick(t, (1024, 512, 256, 128, 64)), BIG // 2), [p["w_out"], p["w_up"], p["w_down"]])
    w_out, w_up, w_down = g_out.reshape(-1, D_MODEL), _chips_to_cols(g_up), g_down.reshape(-1, D_MODEL)
    small = _matmul("mm_in_small", h, w_small, "nn", F32, (1024, 128))
    gact, cum, dt, acs = _gates_fwd(small, gate_bias, gate_a_log)
    gc_row = _to_rows(cum, _G_LO, _G_HI, bsz, n)
    acs_row = _to_rows(cum, _S_LO, _S_HI, bsz, n)
    q = _gdn_conv_fwd("q", proj, p["gdn_conv_w"], bsz, s)
    k = _gdn_conv_fwd("k", proj, p["gdn_conv_w"], bsz, s)
    v = _gdn_conv_fwd("v", proj, p["gdn_conv_w"], bsz, s)
    o, gdn_s = _gdn_fwd(q, k, v, gact, cum, gc_row, bsz, n)
    ocat = _gdn_out_fwd(o, proj, p["gdn_norm_w"])
    xbc = _ssd_conv_fwd(proj, p["ssd_conv_w"], p["ssd_conv_b"], bsz, s)
    y, ssd_s = _ssd_fwd(xbc, dt, acs, acs_row, bsz, n)
    ocat = _ssd_out_fwd(y, xbc, proj, d_skip, p["ssd_norm_w"], ocat)
    mix = _matmul("mm_out", ocat, w_out, "nn", F32, (1024, 1024))
    x1, h2 = _res1_fwd(x2, mix, p["post_mix_norm"], p["pre_ffn_norm"])
    u_pre = _matmul("mm_up", h2, w_up, "nn", BF16, (1024, 2816))
    act = _ffn_act_fwd(u_pre, p["ffn_conv_w"], p["ffn_conv_b"], bsz, s)
    f = _matmul("mm_down", act, w_down, "nn", F32, (1024, 1024))
    loss_acc, d_out, d_f, g_post_ffn = _final(x1, f, p["post_ffn_norm"], tgt2)

    grads = {"post_ffn_norm": g_post_ffn}
    d_act = _matmul("mm_down_dx", d_f, w_down, "nt", F32, (1024, 2816))
    dw_down = _matmul("mm_down_dw", act, d_f, "tn", F32, (2816, 1024, 1024))
    d_gate, d_up, dwg, dwu, dbg, dbu = _ffn_act_bwd(u_pre, p["ffn_conv_w"], p["ffn_conv_b"], d_act, bsz, s)
    grads["ffn_conv_w"] = jnp.concatenate([dwg, dwu], axis=1)
    grads["ffn_conv_b"] = jnp.concatenate([dbg, dbu], axis=1)
    d_h2 = _matmul_nt_split("mm_up_dx", d_gate, d_up, w_up, 512)
    dw_up = _matmul("mm_up_dw_gate", h2, d_gate, "tn", F32, (1024, 2816, 1024), part=(2 * D_FF, 0, None))
    dw_up = _matmul("mm_up_dw_up", h2, d_up, "tn", F32, (1024, 2816, 1024), part=(2 * D_FF, D_FF, dw_up))
    d_x1, d_mix, grads["post_mix_norm"], grads["pre_ffn_norm"] = _res1_bwd(
        x2, mix, p["post_mix_norm"], p["pre_ffn_norm"], d_out, d_h2)
    d_ocat = _matmul("mm_out_dx", d_mix, w_out, "nt", F32, (1024, 2048))
    dw_out = _matmul("mm_out_dw", ocat, d_mix, "tn", F32, (2048, 1024, 1024))

    early = _pair_sums("early", [dw_out.reshape(N_CHIPS, -1, D_MODEL), dw_up, dw_down.reshape(N_CHIPS, -1, D_MODEL)])

    d_o, d_proj, grads["gdn_norm_w"] = _gdn_out_bwd(o, proj, p["gdn_norm_w"], d_ocat)
    (d_q, d_k, d_v, d_act_g, d_cum_g, d_gc_row), early_landed = _gdn_bwd(
        q, k, v, gact, cum, gc_row, gdn_s, d_o, bsz, n, [ps[1] for ps in early])
    d_proj, dwq = _gdn_conv_bwd("q", proj, p["gdn_conv_w"], d_q, d_proj, bsz, s)
    d_proj, dwk = _gdn_conv_bwd("k", proj, p["gdn_conv_w"], d_k, d_proj, bsz, s)
    d_proj, dwv = _gdn_conv_bwd("v", proj, p["gdn_conv_w"], d_v, d_proj, bsz, s)
    grads["gdn_conv_w"] = jnp.concatenate([dwq, dwk, dwv], axis=1)

    d_y, d_xs_skip, d_proj, d_dskip, grads["ssd_norm_w"] = _ssd_out_bwd(y, xbc, proj, d_skip, p["ssd_norm_w"], d_ocat, d_proj)
    d_xbc, d_dt, d_acs, d_acs_row = _ssd_bwd(xbc, dt, acs, acs_row, ssd_s, d_y, d_xs_skip, bsz, n)
    d_proj, grads["ssd_conv_w"], grads["ssd_conv_b"] = _ssd_conv_bwd(proj, p["ssd_conv_w"], p["ssd_conv_b"], d_xbc, d_proj, bsz, s)

    d_cum_rows = jnp.concatenate([jnp.zeros((t, _G_LO), F32), _from_rows(d_gc_row, GDN_HEADS, bsz, n),
                                  _from_rows(d_acs_row, SSD_HEADS, bsz, n), jnp.zeros((t, SMALL - _S_HI), F32)], axis=1)
    d_small, d_gate_bias, d_gate_a_log = _gates_bwd(small, gate_bias, gate_a_log, d_act_g, [d_cum_g, d_cum_rows], d_dt, d_acs)
    grads["gdn_dt_bias"], grads["ssd_dt_bias"] = d_gate_bias[:, _G_LO:_G_HI], d_gate_bias[:, _S_LO:_S_HI]
    grads["gdn_a_log"], grads["ssd_a_log"] = d_gate_a_log[:, _G_LO:_G_HI], d_gate_a_log[:, _S_LO:_S_HI]
    dw_big = _matmul("mm_in_big_dw", h, d_proj, "tn", F32, (1024, 3328, 1024))
    dw_small = _matmul("mm_in_small_dw", h, d_small, "tn", F32)
    dw_in = jnp.concatenate([dw_big[:, :4096], dw_small[:, :16], dw_big[:, 4096:], dw_small[:, 16:32]], axis=1)
    late = _pair_sums("late", [_cols_to_chips(dw_in)])
    d_h_big, late_landed = _matmul_nt_scattering("mm_in_big_dx", d_proj, w_big, 512, [ps[1] for ps in late])
    d_h_small = _matmul("mm_in_small_dx", d_small, w_small, "nt", F32, (1024, 1024))
    grad_x, grads["pre_mix_norm"] = _rms1_bwd(x2, p["pre_mix_norm"], d_h_big, d_h_small, d_x1)
    grads["ssd_d"] = _head_sums(d_dskip)[:1, :SSD_HEADS]
    return loss_acc, grad_x.reshape(bsz, s, D_MODEL), grads, (late + early, list(late_landed) + list(early_landed))


def _head_sums(wide):
    def body(x_ref, o_ref):
        r, c = _iota2((D_MODEL, SMALL))
        o_ref[...] = _mask_dot(jnp.broadcast_to(x_ref[...], (8, D_MODEL)), ((r >> 6) == c).astype(F32), NN, True)

    return _pcall("head_sums", body, (1,), [_full_spec((1, D_MODEL))], _full_spec((8, SMALL)), _sds((8, SMALL)))(wide)


def _adamw_fn(w, g, m, v):
    m = ADAM_B1 * m + (1.0 - ADAM_B1) * g
    v = ADAM_B2 * v + (1.0 - ADAM_B2) * (g * g)
    m_hat = m / (1.0 - ADAM_B1 ** ADAM_STEP)
    v_hat = v / (1.0 - ADAM_B2 ** ADAM_STEP)
    delta = -ADAM_LR * (m_hat / (jnp.sqrt(v_hat) + ADAM_EPS) + ADAM_WD * w)
    return delta, m, v


def _adamw(name, w, g, m, v):
    r, c = w.shape
    tr = _pick(r, (256, 176, 128, 64, 8))

    def body(w_ref, g_ref, m_ref, v_ref, d_ref, m2_ref, v2_ref):
        d, m2, v2 = _adamw_fn(w_ref[...], g_ref[...], m_ref[...], v_ref[...])
        d_ref[...] = d
        m2_ref[...] = m2
        v2_ref[...] = v2

    spec = pl.BlockSpec((tr, c), lambda i: (i, 0))
    return _pcall(name, body, (r // tr,), [spec] * 4, [spec] * 3, [_sds((r, c))] * 3, sem=("parallel",))(w, g, m, v)


_ANY = pl.BlockSpec(memory_space=pl.ANY)
_OTHER_CHIPS = ((1, 0), (0, 1), (1, 1))


def _coords():
    return lax.axis_index("x"), lax.axis_index("y"), lax.axis_index("c")


def _flip(v, f):
    return 1 - v if f else v


def _gather_ops(ins, outs, sems, split):
    send_sems, recv_sems, fwd_send_sems, fwd_recv_sems, own_send_sems, own_recv_sems = sems
    n = len(ins)
    x, y, c = _coords()
    me = 2 * x + y
    sib = (x, y, 1 - c)

    def rows(a, core):
        if not split[a]:
            return slice(None)
        half = ins[a].shape[0] // 2
        return pl.ds(core * half, half)

    def chip(j):
        fx, fy = _OTHER_CHIPS[j]
        return _flip(x, fx), _flip(y, fy)

    def own_cp(a):
        return pltpu.make_async_remote_copy(ins[a], outs[a].at[me], own_send_sems.at[a], own_recv_sems.at[a],
                                            device_id=sib, device_id_type=MESH)

    def ici_cp(a, j):
        return pltpu.make_async_remote_copy(ins[a].at[rows(a, c)], outs[a].at[me, rows(a, c)],
                                            send_sems.at[a * 3 + j], recv_sems.at[a * 3 + j],
                                            device_id=(*chip(j), c), device_id_type=MESH)

    def landed_cp(a, j, sem_a, sem_b, core, to):
        cx, cy = chip(j)
        blk = outs[a].at[2 * cx + cy, rows(a, core)]
        return pltpu.make_async_remote_copy(blk, blk, sem_a.at[a * 3 + j], sem_b.at[a * 3 + j], device_id=to, device_id_type=MESH)

    pairs = [(a, j) for a in range(n) for j in range(3)]

    def start():
        for a in range(n):
            own_cp(a).start()
        for a, j in pairs:
            ici_cp(a, j).start()

    def forward():
        for a, j in pairs:
            landed_cp(a, j, send_sems, recv_sems, c, (*chip(j), c)).wait_recv()
            if split[a]:
                landed_cp(a, j, fwd_send_sems, fwd_recv_sems, c, sib).start()

    def finish():
        for a, j in pairs:
            if split[a]:
                landed_cp(a, j, fwd_send_sems, fwd_recv_sems, 1 - c, sib).wait_recv()
        for a in range(n):
            own_cp(a).wait_recv()
        for a, j in pairs:
            ici_cp(a, j).wait_send()
            if split[a]:
                landed_cp(a, j, fwd_send_sems, fwd_recv_sems, c, sib).wait_send()
        for a in range(n):
            own_cp(a).wait_send()

    return start, forward, finish


def _gather_sems(n):
    return [pltpu.SemaphoreType.DMA((3 * n,))] * 4 + [pltpu.SemaphoreType.DMA((n,))] * 2


def _gather_chips(arrs, split):
    n = len(arrs)

    def body(*refs):
        start, forward, finish = _gather_ops(refs[:n], refs[n:2 * n], refs[2 * n:], split)
        start()
        forward()
        finish()

    return pl.pallas_call(
        body, name="gather_chips", out_shape=[_sds((N_CHIPS,) + a.shape, a.dtype) for a in arrs],
        in_specs=[_ANY] * n, out_specs=[_ANY] * n, scratch_shapes=_gather_sems(n),
        compiler_params=pltpu.CompilerParams(has_side_effects=True))(*arrs)


def _matmul_nn_gathering(name, a, b, out_dtype, tiles, shards):
    m, k = a.shape
    n = b.shape[1]
    tm, tn = tiles
    nc = _pick(tn, (512, 256, 128))
    ns = len(shards)
    gi, gj = m // tm, n // tn
    steps = gi * gj

    def body(a_ref, b_ref, *rest):
        ins, o_ref, outs, sems = rest[:ns], rest[ns], rest[ns + 1:2 * ns + 1], rest[2 * ns + 1:]
        step = pl.program_id(0) * gj + pl.program_id(1)
        start, forward, finish = _gather_ops(ins, outs, sems, [True] * ns)
        pl.when(step == 0)(start)
        for c0 in range(0, tn, nc):
            o_ref[:, c0:c0 + nc] = _bdot(a_ref[...], b_ref[:, c0:c0 + nc], NN).astype(o_ref.dtype)
        pl.when(step == steps - 2)(forward)
        pl.when(step == steps - 1)(finish)

    assert steps >= 2
    return pl.pallas_call(
        body, name=name, grid=(gi, gj),
        in_specs=[pl.BlockSpec((tm, k), lambda i, j: (i, 0)), pl.BlockSpec((k, tn), lambda i, j: (0, j))] + [_ANY] * ns,
        out_specs=[pl.BlockSpec((tm, tn), lambda i, j: (i, j))] + [_ANY] * ns,
        out_shape=[_sds((m, n), out_dtype)] + [_sds((N_CHIPS,) + s.shape, s.dtype) for s in shards],
        scratch_shapes=_gather_sems(ns),
        compiler_params=pltpu.CompilerParams(dimension_semantics=("arbitrary", "arbitrary"), vmem_limit_bytes=VMEM_LIMIT,
                                             has_side_effects=True))(a, b, *shards)


_PEERS = tuple((fx, fy, fc) for fx in (0, 1) for fy in (0, 1) for fc in (0, 1))[1:]


def _allreduce_small(x):
    r = x.shape[0]

    def body(x_ref, o_ref, buf, send_sems, recv_sems):
        cx, cy, cc = _coords()
        me = 4 * cx + 2 * cy + cc
        sends = []
        for j, (fx, fy, fc) in enumerate(_PEERS):
            cp = pltpu.make_async_remote_copy(x_ref, buf.at[me], send_sems.at[j], recv_sems.at[j],
                                              device_id=(_flip(cx, fx), _flip(cy, fy), _flip(cc, fc)), device_id_type=MESH)
            cp.start()
            sends.append(cp)
        buf[pl.ds(me, 1)] = x_ref[...][None]
        for j, (fx, fy, fc) in enumerate(_PEERS):
            src = 4 * _flip(cx, fx) + 2 * _flip(cy, fy) + _flip(cc, fc)
            pltpu.make_async_remote_copy(x_ref, buf.at[src], send_sems.at[j], recv_sems.at[j],
                                         device_id=(_flip(cx, fx), _flip(cy, fy), _flip(cc, fc)), device_id_type=MESH).wait_recv()
        for cp in sends:
            cp.wait_send()
        acc = buf[0]
        for d in range(1, N_DEV):
            acc = acc + buf[d]
        o_ref[...] = acc

    vm = pl.BlockSpec(memory_space=pltpu.VMEM)
    return pl.pallas_call(
        body, name="allreduce_small", out_shape=_sds((r, LANE)), in_specs=[vm], out_specs=vm,
        scratch_shapes=[pltpu.VMEM((N_DEV, r, LANE), F32), pltpu.SemaphoreType.DMA((7,)), pltpu.SemaphoreType.DMA((7,))],
        compiler_params=pltpu.CompilerParams(has_side_effects=True, vmem_limit_bytes=VMEM_LIMIT))(x)


def _pair_sums(tag, arrs):
    received = _pair_send_other_half("pair_reduce_send_" + tag, arrs)
    core = lax.axis_index("c").astype(jnp.int32).reshape(1)
    return [_pair_add("pair_add_%s_%d" % (tag, i), a, b, core) for i, (a, b) in enumerate(zip(arrs, received))]


def _pair_send_other_half(name, arrs):
    n = len(arrs)

    def body(*refs):
        ins, outs = refs[:n], refs[n:2 * n]
        send_sems, recv_sems = refs[2 * n:]
        x, y, c = _coords()
        sends = []
        for a in range(n):
            half = ins[a].shape[1] // 2
            cp = pltpu.make_async_remote_copy(ins[a].at[:, pl.ds((1 - c) * half, half), :], outs[a], send_sems.at[a], recv_sems.at[a],
                                              device_id=(x, y, 1 - c), device_id_type=MESH)
            cp.start()
            sends.append(cp)
        for cp in sends:
            cp.wait_recv()
        for cp in sends:
            cp.wait_send()

    return pl.pallas_call(
        body, name=name, out_shape=[_sds((a.shape[0], a.shape[1] // 2, a.shape[2]), a.dtype) for a in arrs],
        in_specs=[_ANY] * n, out_specs=[_ANY] * n, scratch_shapes=[pltpu.SemaphoreType.DMA((n,))] * 2,
        compiler_params=pltpu.CompilerParams(has_side_effects=True))(*arrs)


def _pair_fill(arrs):
    n = len(arrs)

    def body(*refs):
        bufs = refs[n:2 * n]
        send_sems, recv_sems = refs[2 * n:]
        x, y, c = _coords()
        sends = []
        for a in range(n):
            cp = pltpu.make_async_remote_copy(bufs[a].at[c], bufs[a].at[c], send_sems.at[a], recv_sems.at[a],
                                              device_id=(x, y, 1 - c), device_id_type=MESH)
            cp.start()
            sends.append(cp)
        for a in range(n):
            theirs = bufs[a].at[1 - c]
            pltpu.make_async_remote_copy(theirs, theirs, send_sems.at[a], recv_sems.at[a],
                                         device_id=(x, y, 1 - c), device_id_type=MESH).wait_recv()
        for cp in sends:
            cp.wait_send()

    return pl.pallas_call(
        body, name="pair_gather", out_shape=[_sds(a.shape, a.dtype) for a in arrs], in_specs=[_ANY] * n, out_specs=[_ANY] * n,
        scratch_shapes=[pltpu.SemaphoreType.DMA((n,))] * 2, input_output_aliases={a: a for a in range(n)},
        compiler_params=pltpu.CompilerParams(has_side_effects=True))(*arrs)


def _scatter_ops(ins, outs, sems):
    send_sems, recv_sems = sems
    x, y, c = _coords()

    def cp(a, j):
        fx, fy = _OTHER_CHIPS[j]
        to = 2 * _flip(x, fx) + _flip(y, fy)
        return pltpu.make_async_remote_copy(ins[a].at[to], outs[a].at[j], send_sems.at[a * 3 + j], recv_sems.at[a * 3 + j],
                                            device_id=(_flip(x, fx), _flip(y, fy), c), device_id_type=MESH)

    pairs = [(a, j) for a in range(len(ins)) for j in range(3)]

    def start():
        for a, j in pairs:
            cp(a, j).start()

    def finish():
        for a, j in pairs:
            cp(a, j).wait_recv()
        for a, j in pairs:
            cp(a, j).wait_send()

    return start, finish


def _scatter_sems(n):
    return [pltpu.SemaphoreType.DMA((3 * n,))] * 2


def _scatter_chips(arrs):
    n = len(arrs)

    def body(*refs):
        start, finish = _scatter_ops(refs[:n], refs[n:2 * n], refs[2 * n:])
        start()
        finish()

    return pl.pallas_call(
        body, name="scatter_chips", out_shape=[_sds((3,) + a.shape[1:], a.dtype) for a in arrs],
        in_specs=[_ANY] * n, out_specs=[_ANY] * n, scratch_shapes=_scatter_sems(n),
        compiler_params=pltpu.CompilerParams(has_side_effects=True))(*arrs)


def _pair_add(name, full, recv, core):
    _, r, c = full.shape
    half = r // 2
    tr = _pick(half, (256, 176, 128, 64, 8))
    nb = half // tr

    def body(c_ref, a_ref, b_ref, o_ref, ob_ref):
        s = a_ref[...] + b_ref[...]
        o_ref[...] = s
        ob_ref[...] = s.astype(BF16)

    blk = pl.BlockSpec((1, tr, c), lambda k, i, cref: (k, i, 0))
    grid_spec = pltpu.PrefetchScalarGridSpec(
        num_scalar_prefetch=1, grid=(N_CHIPS, nb),
        in_specs=[pl.BlockSpec((1, tr, c), lambda k, i, cref: (k, cref[0] * nb + i, 0)), blk], out_specs=[blk, blk])
    return pl.pallas_call(
        body, name=name, out_shape=[_sds((N_CHIPS, half, c)), _sds((N_CHIPS, half, c), BF16)], grid_spec=grid_spec,
        compiler_params=pltpu.CompilerParams(dimension_semantics=("parallel", "parallel"), vmem_limit_bytes=VMEM_LIMIT))(
            core, full, recv)


def _chip_sum(name, landed, own, where):
    _, r, c = landed.shape
    tr = _pick(r, (256, 176, 128, 64, 16))

    def body(w_ref, l_ref, o_ref, s_ref):
        s_ref[0] = ((o_ref[0] + l_ref[0].astype(F32)) + l_ref[1].astype(F32)) + l_ref[2].astype(F32)

    grid_spec = pltpu.PrefetchScalarGridSpec(
        num_scalar_prefetch=1, grid=(r // tr,),
        in_specs=[pl.BlockSpec((3, tr, c), lambda i, wref: (0, i, 0)),
                  pl.BlockSpec((1, tr, c), lambda i, wref: (wref[0], i, 0))],
        out_specs=pl.BlockSpec((1, tr, c), lambda i, wref: (wref[1], i, 0)))
    return pl.pallas_call(
        body, name=name, out_shape=_sds((2, r, c)), grid_spec=grid_spec,
        compiler_params=pltpu.CompilerParams(dimension_semantics=("parallel",), vmem_limit_bytes=VMEM_LIMIT))(where, landed, own)


_WEIGHTS = ("pre_mix_norm", "w_in", "gdn_conv_w", "gdn_a_log", "gdn_dt_bias", "gdn_norm_w", "ssd_conv_w", "ssd_conv_b",
            "ssd_a_log", "ssd_dt_bias", "ssd_d", "ssd_norm_w", "w_out", "post_mix_norm", "pre_ffn_norm", "w_up",
            "ffn_conv_w", "ffn_conv_b", "w_down", "post_ffn_norm")
_BIG = ("w_in", "w_out", "w_up", "w_down")
_COL_SHARDED_SMALL = ("gdn_conv_w", "ssd_conv_w", "ffn_conv_w")
_SMALL = tuple(k for k in _WEIGHTS if k not in _BIG)


def _pack(arrs):
    flat = jnp.concatenate([a.reshape(-1) for a in arrs])
    rows = -(-flat.shape[0] // (8 * LANE)) * 8
    return jnp.pad(flat, (0, rows * LANE - flat.shape[0])).reshape(rows, LANE)


def _unpack(packed, shapes):
    flat = packed.reshape(-1)
    out, off = [], 0
    for shp in shapes:
        size = 1
        for d in shp:
            size *= d
        out.append(flat[off:off + size].reshape(shp))
        off += size
    return out


def _cols_to_chips(a):
    r, c4 = a.shape
    return jnp.transpose(a.reshape(r, N_CHIPS, c4 // N_CHIPS), (1, 0, 2))


def _chips_to_cols(a):
    k, r, c = a.shape
    return jnp.transpose(a, (1, 0, 2)).reshape(r, k * c)


def kernel(x, pre_mix_norm, w_in, gdn_conv_w, gdn_a_log, gdn_dt_bias, gdn_norm_w, ssd_conv_w, ssd_conv_b, ssd_a_log, ssd_dt_bias, ssd_d, ssd_norm_w, w_out, post_mix_norm, pre_ffn_norm, w_up, ffn_conv_w, ffn_conv_b, w_down, post_ffn_norm, loss_target, m_pre_mix_norm, m_w_in, m_gdn_conv_w, m_gdn_a_log, m_gdn_dt_bias, m_gdn_norm_w, m_ssd_conv_w, m_ssd_conv_b, m_ssd_a_log, m_ssd_dt_bias, m_ssd_d, m_ssd_norm_w, m_w_out, m_post_mix_norm, m_pre_ffn_norm, m_w_up, m_ffn_conv_w, m_ffn_conv_b, m_w_down, m_post_ffn_norm, v_pre_mix_norm, v_w_in, v_gdn_conv_w, v_gdn_a_log, v_gdn_dt_bias, v_gdn_norm_w, v_ssd_conv_w, v_ssd_conv_b, v_ssd_a_log, v_ssd_dt_bias, v_ssd_d, v_ssd_norm_w, v_w_out, v_post_mix_norm, v_pre_ffn_norm, v_w_up, v_ffn_conv_w, v_ffn_conv_b, v_w_down, v_post_ffn_norm):
    w = dict(zip(_WEIGHTS, (pre_mix_norm, w_in, gdn_conv_w, gdn_a_log, gdn_dt_bias, gdn_norm_w, ssd_conv_w, ssd_conv_b,
                            ssd_a_log, ssd_dt_bias, ssd_d, ssd_norm_w, w_out, post_mix_norm, pre_ffn_norm, w_up,
                            ffn_conv_w, ffn_conv_b, w_down, post_ffn_norm)))
    m = dict(zip(_WEIGHTS, (m_pre_mix_norm, m_w_in, m_gdn_conv_w, m_gdn_a_log, m_gdn_dt_bias, m_gdn_norm_w, m_ssd_conv_w,
                            m_ssd_conv_b, m_ssd_a_log, m_ssd_dt_bias, m_ssd_d, m_ssd_norm_w, m_w_out, m_post_mix_norm,
                            m_pre_ffn_norm, m_w_up, m_ffn_conv_w, m_ffn_conv_b, m_w_down, m_post_ffn_norm)))
    v = dict(zip(_WEIGHTS, (v_pre_mix_norm, v_w_in, v_gdn_conv_w, v_gdn_a_log, v_gdn_dt_bias, v_gdn_norm_w, v_ssd_conv_w,
                            v_ssd_conv_b, v_ssd_a_log, v_ssd_dt_bias, v_ssd_d, v_ssd_norm_w, v_w_out, v_post_mix_norm,
                            v_pre_ffn_norm, v_w_up, v_ffn_conv_w, v_ffn_conv_b, v_w_down, v_post_ffn_norm)))
    cx, cy, cc = _coords()
    chip = 2 * cx + cy

    p = {k: w[k] for k in _SMALL if k not in _COL_SHARDED_SMALL}
    for k in _BIG:
        p[k] = w[k][0].astype(BF16)
    for k in _COL_SHARDED_SMALL:
        p[k] = w[k][0]
    loss_acc, grad_x, grads, (pair_sum_list, landed_list) = _local_step(x, loss_target, p)
    loss = lax.psum(loss_acc[0, 0], ("x", "y", "c"))

    small_full_shapes = [grads[k].shape for k in _SMALL]
    summed = _unpack(_allreduce_small(_pack([grads[k] for k in _SMALL])), small_full_shapes)
    g_small = dict(zip(_SMALL, summed))
    for k in _COL_SHARDED_SMALL:
        width = w[k].shape[2]
        g_small[k] = lax.dynamic_slice_in_dim(g_small[k], chip * width, width, axis=1)

    pair_sums = dict(zip(_BIG, pair_sum_list))
    landed = dict(zip(_BIG, landed_list))
    where = jnp.stack([chip, cc]).astype(jnp.int32)
    mine = [_chip_sum("chip_sum_" + k, landed[k], pair_sums[k][0], where) for k in _BIG]
    both = _pair_fill(mine)
    g_big = {k: a.reshape(-1, a.shape[2]) for k, a in zip(_BIG, both)}

    out_g, out_d, out_m, out_v = {}, {}, {}, {}
    for k in _BIG:
        out_g[k] = g_big[k][None]
        d_, m_, v_ = _adamw("adamw_" + k, w[k][0], g_big[k], m[k][0], v[k][0])
        out_d[k], out_m[k], out_v[k] = d_[None], m_[None], v_[None]
    shapes = [w[k].shape for k in _SMALL]
    for k in _SMALL:
        out_g[k] = g_small[k].reshape(w[k].shape)
    packed = [_pack([d[k] for k in _SMALL]) for d in (w, out_g, m, v)]
    d_p, m_p, v_p = _adamw("adamw_small", *packed)
    for dst, src in ((out_d, d_p), (out_m, m_p), (out_v, v_p)):
        dst.update(zip(_SMALL, _unpack(src, shapes)))
    return (loss, grad_x, *[out_g[k] for k in _WEIGHTS], *[out_d[k] for k in _WEIGHTS],
            *[out_m[k] for k in _WEIGHTS], *[out_v[k] for k in _WEIGHTS])
```

```python
import functools

import jax
import jax.numpy as jnp
from jax import lax
from jax.experimental import pallas as pl
from jax.experimental.pallas import tpu as pltpu

F32 = jnp.float32
BF16 = jnp.bfloat16

D_MODEL = 1024
GDN_HEADS = 8
GDN_DK = 128
SSD_HEADS = 16
SSD_HEADDIM = 64
SSD_GROUPS = 2
SSD_STATE = 128
CONV_K = 4
CHUNK = 64
D_FF = 2816
FFN_CONV_K = 3
EPS = 1e-6
GDN_QK = GDN_HEADS * GDN_DK
GDN_V = GDN_QK
SSD_D = SSD_HEADS * SSD_HEADDIM
SSD_BC = SSD_GROUPS * SSD_STATE
SSD_CONV_CH = SSD_D + 2 * SSD_BC
BIG = 4 * 1024 + 1024 + SSD_CONV_CH
SMALL = 128
D_IN_PROJ = 6688
LANE = 128
PAIR = 2 * CHUNK
NEG = -1e30
VMEM_LIMIT = 56 * 1024 * 1024

ADAM_LR = 0.001
ADAM_B1 = 0.9
ADAM_B2 = 0.999
ADAM_EPS = 1e-08
ADAM_WD = 0.01
ADAM_STEP = 10

N_CHIPS = 4
N_DEV = 8
MESH = pl.DeviceIdType.MESH

NN = ((1,), (0,))
NT = ((1,), (1,))
TN = ((0,), (0,))


def _bdot(a, b, dims):
    return lax.dot_general(a.astype(BF16), b.astype(BF16), (dims, ((), ())), preferred_element_type=F32)


def _split3(a):
    hi = a.astype(BF16)
    r1 = a - hi.astype(F32)
    mid = r1.astype(BF16)
    return hi, mid, (r1 - mid.astype(F32)).astype(BF16)


@jax.custom_vjp
def _nn(a, b):
    return _bdot(a, b, NN)


@jax.custom_vjp
def _nt(a, b):
    return _bdot(a, b, NT)


@jax.custom_vjp
def _tn(a, b):
    return _bdot(a, b, TN)


_nn.defvjp(lambda a, b: (_nn(a, b), (a, b)), lambda r, g: (_nt(g, r[1]), _tn(r[0], g)))
_nt.defvjp(lambda a, b: (_nt(a, b), (a, b)), lambda r, g: (_nn(g, r[1]), _tn(g, r[0])))
_tn.defvjp(lambda a, b: (_tn(a, b), (a, b)), lambda r, g: (_nt(r[1], g), _nn(r[0], g)))


def _mask_dot(x, mask, dims, x_first):
    acc = None
    for piece in _split3(x):
        term = _bdot(piece, mask, dims) if x_first else _bdot(mask, piece, dims)
        acc = term if acc is None else acc + term
    return acc


@jax.custom_vjp
def _cst_left(cst, x):
    return _mask_dot(x, cst, NN, False)


_cst_left.defvjp(lambda cst, x: (_cst_left(cst, x), cst), lambda cst, g: (jnp.zeros_like(cst), _mask_dot(g, cst, TN, False)))


@jax.custom_vjp
def _cst_right(x, cst):
    return _mask_dot(x, cst, NN, True)


_cst_right.defvjp(lambda x, cst: (_cst_right(x, cst), cst), lambda cst, g: (_mask_dot(g, cst, NT, True), jnp.zeros_like(cst)))


def _lin_left(cst):
    return functools.partial(_cst_left, cst)


def _lin_right(cst):
    return lambda x: _cst_right(x, cst)


@jax.custom_vjp
def _tri_inv_m1(a):
    pm = [-x for x in a]
    ap = list(a)
    for _ in range(5):
        ap = [_bdot(x, x, NN) for x in ap]
        pm = [(p + x) + _bdot(p, x, NN) for p, x in zip(pm, ap)]
    return pm


def _tri_inv_m1_bwd(pm, g):
    t = [gi + _bdot(p, gi, TN) for p, gi in zip(pm, g)]
    return ([-(ti + _bdot(ti, p, NT)) for p, ti in zip(pm, t)],)


_tri_inv_m1.defvjp(lambda a: (lambda pm: (pm, pm))(_tri_inv_m1(a)), _tri_inv_m1_bwd)


@jax.custom_vjp
def _top(x):
    return x[: x.shape[0] // 2]


_top.defvjp(lambda x: (_top(x), None), lambda _, g: (jnp.concatenate([g, jnp.zeros_like(g)], axis=0),))


@jax.custom_vjp
def _bot(x):
    return x[x.shape[0] // 2:]


_bot.defvjp(lambda x: (_bot(x), None), lambda _, g: (jnp.concatenate([jnp.zeros_like(g), g], axis=0),))


@jax.custom_vjp
def _vstack(a, b):
    return jnp.concatenate([a, b], axis=0)


_vstack.defvjp(lambda a, b: (_vstack(a, b), None), lambda _, g: (g[: g.shape[0] // 2], g[g.shape[0] // 2:]))


def _shift_dn_raw(x, s):
    if s == 0:
        return x
    r = pltpu.roll(x, s, axis=0)
    ri = lax.broadcasted_iota(jnp.int32, x.shape, 0)
    return jnp.where(ri >= s, r, 0.0)


def _shift_up_raw(x, s):
    if s == 0:
        return x
    n = x.shape[0]
    r = pltpu.roll(x, n - s, axis=0)
    ri = lax.broadcasted_iota(jnp.int32, x.shape, 0)
    return jnp.where(ri < n - s, r, 0.0)


@functools.partial(jax.custom_vjp, nondiff_argnums=(1,))
def _shift_dn(x, s):
    return _shift_dn_raw(x, s)


_shift_dn.defvjp(lambda x, s: (_shift_dn_raw(x, s), None), lambda s, _, g: (_shift_up_raw(g, s),))


def _conv(x, wrows):
    k_w = len(wrows)
    acc = wrows[k_w - 1] * x
    for k in range(k_w - 1):
        acc = acc + wrows[k] * _shift_dn(x, k_w - 1 - k)
    return acc


def _silu(x):
    return x * jax.nn.sigmoid(x)


def _rms(x, w):
    return x * lax.rsqrt(jnp.mean(x * x, axis=-1, keepdims=True) + EPS) * w


def _l2n(x):
    return x * lax.rsqrt(jnp.sum(x * x, axis=-1, keepdims=True) + EPS)


def _iota2(shape):
    return lax.broadcasted_iota(jnp.int32, shape, 0), lax.broadcasted_iota(jnp.int32, shape, 1)


_GDN_NARGS = 15
_SSD_NARGS = 8


def _gdn_multi(*flat):
    pairs = [flat[i:i + _GDN_NARGS] for i in range(0, len(flat), _GDN_NARGS)]
    idx = range(len(pairs))
    ri, ci = _iota2((PAIR, PAIR))
    blk = ((ri >= CHUNK) & (ci >= CHUNK)) | ((ri < CHUNK) & (ci < CHUNK))
    causal = blk & (ri >= ci)
    strict = blk & (ri > ci)
    q = [_vstack(p[0], p[1]) for p in pairs]
    k = [_vstack(p[2], p[3]) for p in pairs]
    v = [_vstack(p[4], p[5]) for p in pairs]
    gc = [_vstack(p[6], p[7]) for p in pairs]
    beta = [_vstack(p[8], p[9]) for p in pairs]
    glast = [_vstack(jnp.broadcast_to(p[11], (CHUNK, LANE)), jnp.broadcast_to(p[12], (CHUNK, LANE))) for p in pairs]
    sa = [p[13] for p in pairs]
    sb = [p[14] for p in pairs]
    decay = [jnp.exp(jnp.where(causal, gc[i] - jnp.broadcast_to(pairs[i][10], (PAIR, PAIR)), NEG)) for i in idx]
    eg = [jnp.exp(x) for x in gc]
    kbeta = [k[i] * beta[i] for i in idx]
    pm = _tri_inv_m1([jnp.where(strict, _nt(kbeta[i], k[i]) * decay[i], 0.0) for i in idx])
    qk = [_nt(q[i], k[i]) * decay[i] for i in idx]
    rhs_v = [v[i] * beta[i] for i in idx]
    rhs_k = [kbeta[i] * eg[i] for i in idx]
    u = [rhs_v[i] + _nn(pm[i], rhs_v[i]) for i in idx]
    w = [rhs_k[i] + _nn(pm[i], rhs_k[i]) for i in idx]
    q_dec = [q[i] * eg[i] for i in idx]
    k_dec = [k[i] * jnp.exp(glast[i] - gc[i]) for i in idx]
    gl = [jnp.exp(x) for x in glast]
    w_s = [_vstack(_nn(_top(w[i]), sa[i]), _nn(_bot(w[i]), sb[i])) for i in idx]
    q_s = [_vstack(_nn(_top(q_dec[i]), sa[i]), _nn(_bot(q_dec[i]), sb[i])) for i in idx]
    v_new = [u[i] - w_s[i] for i in idx]
    o = [q_s[i] + _nn(qk[i], v_new[i]) for i in idx]
    sa2 = [sa[i] * _vstack(_top(gl[i]), _top(gl[i])) + _tn(_top(k_dec[i]), _top(v_new[i])) for i in idx]
    sb2 = [sb[i] * _vstack(_bot(gl[i]), _bot(gl[i])) + _tn(_bot(k_dec[i]), _bot(v_new[i])) for i in idx]
    out = []
    for i in idx:
        out += [_top(o[i]), _bot(o[i]), sa2[i], sb2[i]]
    return tuple(out)


def _ssd_multi(*flat):
    pairs = [flat[i:i + _SSD_NARGS] for i in range(0, len(flat), _SSD_NARGS)]
    idx = range(len(pairs))
    ri, ci = _iota2((CHUNK, PAIR))
    causal = ri >= jnp.where(ci >= CHUNK, ci - CHUNK, ci)
    xdt = [p[0] * p[1] for p in pairs]
    acs = [p[2] for p in pairs]
    alast = [jnp.broadcast_to(p[4], (CHUNK, PAIR)) for p in pairs]
    lmat = [jnp.exp(jnp.where(causal, acs[i] - jnp.broadcast_to(pairs[i][3], (CHUNK, PAIR)), NEG)) for i in idx]
    cb2 = [_nt(p[6], _vstack(p[5], p[5])) for p in pairs]
    xblk = [_vstack(jnp.where(ci < CHUNK, x, 0.0), jnp.where(ci >= CHUNK, x, 0.0)) for x in xdt]
    y_off = [_nn(pairs[i][6], pairs[i][7]) * jnp.exp(acs[i]) for i in idx]
    y = [_nn(cb2[i] * lmat[i], xblk[i]) + y_off[i] for i in idx]
    el = [jnp.exp(x) for x in alast]
    st2 = [pairs[i][7] * _vstack(el[i], el[i]) + _tn(pairs[i][5], xdt[i] * jnp.exp(alast[i] - acs[i])) for i in idx]
    out = []
    for i in idx:
        out += [y[i], st2[i]]
    return tuple(out)


def _pcall(name, body, grid, in_specs, out_specs, out_shape, scratch=(), sem=None, aliases=None):
    if sem is None:
        sem = ("arbitrary",) * len(grid)
    return pl.pallas_call(
        functools.partial(body),
        out_shape=out_shape,
        grid=grid,
        in_specs=in_specs,
        out_specs=out_specs,
        scratch_shapes=scratch,
        input_output_aliases=aliases or {},
        name=name,
        compiler_params=pltpu.CompilerParams(dimension_semantics=sem, vmem_limit_bytes=VMEM_LIMIT),
    )


def _sds(shape, dtype=F32):
    return jax.ShapeDtypeStruct(shape, dtype)


def _row_spec(tm, width, colblock=0):
    return pl.BlockSpec((tm, width), lambda i, _c=colblock: (i, _c))


def _full_spec(shape):
    nd = len(shape)
    return pl.BlockSpec(shape, lambda *_: (0,) * nd)


def _zero_at_first(refs, first):
    @pl.when(first)
    def _():
        for r in refs:
            r[...] = jnp.zeros(r.shape, r.dtype)


def _pick(n, prefs):
    for p in prefs:
        if n % p == 0:
            return p
    return n


def _matmul(name, a, b, mode, out_dtype, tiles=None, part=None):
    def want(i, dim):
        return [tiles[i]] if tiles is not None and dim % tiles[i] == 0 else []

    if mode == "tn":
        r, m = a.shape
        n = b.shape[1]
        tm = _pick(m, want(0, m) + [1024, 1408])
        tn = _pick(n, want(1, n) + [512, 256, 128])
        tk = _pick(r, want(2, r) + [1024, 512, 256, 128, 64])
        in_specs = [pl.BlockSpec((tk, tm), lambda i, j, k: (k, i)), pl.BlockSpec((tk, tn), lambda i, j, k: (k, j))]
        sem = ("parallel", "parallel", "arbitrary")
        if part is None:
            nc = _pick(tn, (512, 256, 128))

            def body(a_ref, b_ref, o_ref):
                _zero_at_first([o_ref], pl.program_id(2) == 0)
                for c0 in range(0, tn, nc):
                    o_ref[:, c0:c0 + nc] += _bdot(a_ref[...], b_ref[:, c0:c0 + nc], TN)

            return _pcall(name, body, (m // tm, n // tn, r // tk), in_specs, pl.BlockSpec((tm, tn), lambda i, j, k: (i, j)),
                          _sds((m, n), out_dtype), sem=sem)(a, b)
        n_total, col_off, into = part
        width = n_total // N_CHIPS
        per_step = tn // width
        off = col_off // tn
        assert tn % width == 0 and col_off % tn == 0

        def body(a_ref, b_ref, *rest):
            o_ref = rest[-1]
            _zero_at_first([o_ref], pl.program_id(2) == 0)
            for q in range(per_step):
                o_ref[q] += _bdot(a_ref[...], b_ref[:, q * width:(q + 1) * width], TN)

        args = (a, b) if into is None else (a, b, into)
        return _pcall(
            name, body, (m // tm, n // tn, r // tk), in_specs + ([] if into is None else [_ANY]),
            pl.BlockSpec((per_step, tm, width), lambda i, j, k: (j + off, i, 0)), _sds((N_CHIPS, m, width), out_dtype),
            sem=sem, aliases=None if into is None else {2: 0})(*args)
    m, k = a.shape
    n = b.shape[1] if mode == "nn" else b.shape[0]
    tm = _pick(m, want(0, m) + ([1024, 512, 256, 128, 64] if k <= 2816 else [512, 256, 128, 64]))
    tn = _pick(n, want(1, n) + [512, 256, 128])
    dims = NN if mode == "nn" else NT

    nc = _pick(tn, (512, 256, 128))

    def body(a_ref, b_ref, o_ref):
        for c0 in range(0, tn, nc):
            b_blk = b_ref[:, c0:c0 + nc] if mode == "nn" else b_ref[c0:c0 + nc, :]
            o_ref[:, c0:c0 + nc] = _bdot(a_ref[...], b_blk, dims).astype(o_ref.dtype)

    b_spec = pl.BlockSpec((k, tn), lambda i, j: (0, j)) if mode == "nn" else pl.BlockSpec((tn, k), lambda i, j: (j, 0))
    return _pcall(
        name, body, (m // tm, n // tn), [pl.BlockSpec((tm, k), lambda i, j: (i, 0)), b_spec],
        pl.BlockSpec((tm, tn), lambda i, j: (i, j)), _sds((m, n), out_dtype), sem=("parallel", "parallel"))(a, b)


def _matmul_nt_scattering(name, a, b, tm_pref, scatter):
    m, k = a.shape
    n = b.shape[0]
    tm = _pick(m, (tm_pref, 256, 128, 64))
    nc = _pick(n, (512, 256, 128))
    ns = len(scatter)
    steps = m // tm

    def body(a_ref, b_ref, *rest):
        sc_in, o_ref, sc_out, sems = rest[:ns], rest[ns], rest[ns + 1:2 * ns + 1], rest[2 * ns + 1:]
        sc_start, sc_finish = _scatter_ops(sc_in, sc_out, sems)
        pl.when(pl.program_id(0) == 0)(sc_start)
        for c0 in range(0, n, nc):
            o_ref[:, c0:c0 + nc] = _bdot(a_ref[...], b_ref[c0:c0 + nc, :], NT)
        pl.when(pl.program_id(0) == steps - 1)(sc_finish)

    outs = pl.pallas_call(
        body, name=name, grid=(steps,),
        in_specs=[pl.BlockSpec((tm, k), lambda i: (i, 0)), _full_spec(b.shape)] + [_ANY] * ns,
        out_specs=[pl.BlockSpec((tm, n), lambda i: (i, 0))] + [_ANY] * ns,
        out_shape=[_sds((m, n))] + [_sds((3,) + s.shape[1:], s.dtype) for s in scatter],
        scratch_shapes=_scatter_sems(ns),
        compiler_params=pltpu.CompilerParams(dimension_semantics=("arbitrary",), vmem_limit_bytes=VMEM_LIMIT,
                                             has_side_effects=True))(a, b, *scatter)
    return outs[0], outs[1:]


def _matmul_nt_split(name, a1, a2, b, tm_pref):
    m, kh = a1.shape
    n = b.shape[0]
    tm = _pick(m, (tm_pref, 512, 256, 128, 64))
    nc = _pick(n, (512, 256, 128))

    def body(a1_ref, a2_ref, b_ref, o_ref):
        for c0 in range(0, n, nc):
            o_ref[:, c0:c0 + nc] = (_bdot(a1_ref[...], b_ref[c0:c0 + nc, :kh], NT)
                                    + _bdot(a2_ref[...], b_ref[c0:c0 + nc, kh:], NT))

    aspec = pl.BlockSpec((tm, kh), lambda i: (i, 0))
    return _pcall(name, body, (m // tm,), [aspec, aspec, _full_spec(b.shape)], pl.BlockSpec((tm, n), lambda i: (i, 0)),
                  _sds((m, n)), sem=("parallel",))(a1, a2, b)


def _row_tile(t):
    return _pick(t, (512, 256, 128, 64))


def _rms_fwd_gathering(name, x, g, shards, split):
    t = x.shape[0]
    tm = next(c for c in (128, 64, 32, 16) if t % c == 0 and t // c >= 2)
    steps = t // tm
    ns = len(shards)

    def body(x_ref, g_ref, *rest):
        ins, h_ref, outs, sems = rest[:ns], rest[ns], rest[ns + 1:2 * ns + 1], rest[2 * ns + 1:]
        start, forward, finish = _gather_ops(ins, outs, sems, split)
        pl.when(pl.program_id(0) == 0)(start)
        h_ref[...] = _rms(x_ref[...], g_ref[...]).astype(BF16)
        pl.when(pl.program_id(0) == steps - 2)(forward)
        pl.when(pl.program_id(0) == steps - 1)(finish)

    assert steps >= 2
    outs = pl.pallas_call(
        body, name=name, grid=(steps,),
        in_specs=[_row_spec(tm, D_MODEL), _full_spec((1, D_MODEL))] + [_ANY] * ns,
        out_specs=[_row_spec(tm, D_MODEL)] + [_ANY] * ns,
        out_shape=[_sds((t, D_MODEL), BF16)] + [_sds((N_CHIPS,) + s.shape, s.dtype) for s in shards],
        scratch_shapes=_gather_sems(ns),
        compiler_params=pltpu.CompilerParams(dimension_semantics=("arbitrary",), vmem_limit_bytes=VMEM_LIMIT,
                                             has_side_effects=True))(x, g, *shards)
    return outs[0], outs[1:]


_G_LO, _G_HI = GDN_HEADS, 2 * GDN_HEADS
_S_LO, _S_HI = 2 * GDN_HEADS, 2 * GDN_HEADS + SSD_HEADS


def _gates_fn(small, bias, a_log):
    tm = small.shape[0]
    rr, cc = _iota2((tm, tm))
    in_chunk_tril = (((rr >> 6) == (cc >> 6)) & (rr >= cc)).astype(F32)
    lane = lax.broadcasted_iota(jnp.int32, small.shape, 1)
    sp = jax.nn.softplus(small + bias)
    act = jnp.where(lane < _G_LO, jax.nn.sigmoid(small), sp)
    cum = _lin_left(in_chunk_tril)(-jnp.exp(a_log) * sp)
    r, c = _iota2((SMALL, D_MODEL))
    to_ssd_lanes = _lin_right((r == _S_LO + (c >> 6)).astype(F32))
    return act, cum, to_ssd_lanes(act), to_ssd_lanes(cum)


def _expand_lanes(src_ref, dst_ref, lo, heads, width):
    rows = src_ref.shape[0]
    for h in range(heads):
        dst_ref[:, h * width:(h + 1) * width] = jnp.broadcast_to(src_ref[:, lo + h:lo + h + 1], (rows, width))


def _reduce_lanes(wide_ref, lo, heads, width):
    rows = wide_ref.shape[0]
    lane = lax.broadcasted_iota(jnp.int32, (rows, SMALL), 1)
    acc = jnp.zeros((rows, SMALL), F32)
    for h in range(heads):
        col = jnp.sum(wide_ref[:, h * width:(h + 1) * width], axis=-1, keepdims=True)
        acc = jnp.where(lane == lo + h, jnp.broadcast_to(col, (rows, SMALL)), acc)
    return acc


def _gates_fwd(small, bias, a_log):
    t = small.shape[0]
    tm = _row_tile(t)

    def body(s_ref, p0, p1, act_ref, cum_ref, dt_ref, acs_ref):
        act_ref[...], cum_ref[...], dt_ref[...], acs_ref[...] = _gates_fn(s_ref[...], p0[...], p1[...])

    pspec, nspec, wspec = _full_spec((1, SMALL)), _row_spec(tm, SMALL), _row_spec(tm, D_MODEL)
    return _pcall("gates_fwd", body, (t // tm,), [nspec, pspec, pspec], [nspec, nspec, wspec, wspec],
                  [_sds((t, SMALL))] * 2 + [_sds((t, D_MODEL))] * 2, sem=("parallel",))(small, bias, a_log)


def _gates_bwd(small, bias, a_log, d_act, d_cums, d_dt, d_acs):
    t = small.shape[0]
    tm = _row_tile(t)
    nc = len(d_cums)

    def body(*refs):
        s_ref, p0, p1, dact_ref = refs[:4]
        dcum_refs = refs[4:4 + nc]
        ddt_ref, dacs_ref, ds_ref, db_ref, da_ref = refs[4 + nc:]
        _zero_at_first([db_ref, da_ref], pl.program_id(0) == 0)
        _, vjp = jax.vjp(_gates_fn, s_ref[...], p0[...], p1[...])
        d_cum = dcum_refs[0][...]
        for c in dcum_refs[1:]:
            d_cum = d_cum + c[...]
        d_s, d_b, d_a = vjp((dact_ref[...], d_cum, ddt_ref[...], dacs_ref[...]))
        ds_ref[...] = d_s.astype(BF16)
        db_ref[...] += d_b
        da_ref[...] += d_a

    pspec, nspec, wspec = _full_spec((1, SMALL)), _row_spec(tm, SMALL), _row_spec(tm, D_MODEL)
    return _pcall("gates_bwd", body, (t // tm,), [nspec, pspec, pspec] + [nspec] * (1 + nc) + [wspec, wspec],
                  [nspec, pspec, pspec], [_sds((t, SMALL), BF16), _sds((1, SMALL)), _sds((1, SMALL))])(
                      small, bias, a_log, d_act, *d_cums, d_dt, d_acs)


def _gdn_out_fn(o, z, w):
    return _rms(o, w) * _silu(z)


def _gdn_out_fwd(o, proj, gn):
    t = o.shape[0]
    tm = _row_tile(t)

    def body(o_ref, z_ref, w_ref, y_ref):
        for h in range(GDN_HEADS):
            sl = slice(h * GDN_DK, (h + 1) * GDN_DK)
            y_ref[:, sl] = _gdn_out_fn(o_ref[:, sl], z_ref[:, sl].astype(F32), w_ref[...]).astype(BF16)

    return _pcall("gdn_out_fwd", body, (t // tm,), [_row_spec(tm, GDN_V), _row_spec(tm, GDN_V, 3), _full_spec((1, GDN_DK))],
                  _row_spec(tm, GDN_V), _sds((t, GDN_V + SSD_D), BF16), sem=("parallel",))(o, proj, gn)


def _gdn_out_bwd(o, proj, gn, d_ocat):
    t = o.shape[0]
    tm = _row_tile(t)

    def body(o_ref, z_ref, w_ref, dy_ref, do_ref, dz_ref, dw_ref):
        _zero_at_first([dw_ref], pl.program_id(0) == 0)
        for h in range(GDN_HEADS):
            sl = slice(h * GDN_DK, (h + 1) * GDN_DK)
            _, vjp = jax.vjp(_gdn_out_fn, o_ref[:, sl], z_ref[:, sl].astype(F32), w_ref[...])
            d_o, d_z, d_w = vjp(dy_ref[:, sl])
            do_ref[:, sl] = d_o
            dz_ref[:, sl] = d_z.astype(BF16)
            dw_ref[...] += d_w

    return _pcall("gdn_out_bwd", body, (t // tm,),
                  [_row_spec(tm, GDN_V), _row_spec(tm, GDN_V, 3), _full_spec((1, GDN_DK)), _row_spec(tm, GDN_V, 0)],
                  [_row_spec(tm, GDN_V), _row_spec(tm, GDN_V, 3), _full_spec((1, GDN_DK))],
                  [_sds((t, GDN_V)), _sds((t, BIG), BF16), _sds((1, GDN_DK))])(o, proj, gn, d_ocat)


def _ssd_out_fn(y, xs, z, d_skip, w):
    return _rms((y + d_skip * xs) * _silu(z), w)


_SSD_GW = SSD_D // SSD_GROUPS


def _ssd_out_fwd(y, xbc, proj, d_skip, nw, ocat):
    t = y.shape[0]
    tm = _row_tile(t)

    def body(y_ref, x_ref, z_ref, d_ref, w_ref, _, o_ref):
        for gi in range(SSD_GROUPS):
            sl = slice(gi * _SSD_GW, (gi + 1) * _SSD_GW)
            o_ref[:, sl] = _ssd_out_fn(y_ref[:, sl], x_ref[:, sl], z_ref[:, sl].astype(F32), d_ref[:, sl], w_ref[:, sl]).astype(BF16)

    pspec = _full_spec((1, SSD_D))
    return _pcall("ssd_out_fwd", body, (t // tm,),
                  [_row_spec(tm, SSD_D), _row_spec(tm, SSD_D, 0), _row_spec(tm, SSD_D, 4), pspec, pspec, _ANY],
                  _row_spec(tm, SSD_D, 1), _sds(ocat.shape, BF16), sem=("parallel",), aliases={5: 0})(
                      y, xbc, proj, d_skip, nw, ocat)


def _ssd_out_bwd(y, xbc, proj, d_skip, nw, d_ocat, d_proj):
    t = y.shape[0]
    tm = _row_tile(t)

    def body(y_ref, x_ref, z_ref, d_ref, w_ref, do_ref, _, dy_ref, dx_ref, dz_ref, dd_ref, dw_ref):
        _zero_at_first([dd_ref, dw_ref], pl.program_id(0) == 0)
        for gi in range(SSD_GROUPS):
            sl = slice(gi * _SSD_GW, (gi + 1) * _SSD_GW)
            _, vjp = jax.vjp(_ssd_out_fn, y_ref[:, sl], x_ref[:, sl], z_ref[:, sl].astype(F32), d_ref[:, sl], w_ref[:, sl])
            d_y, d_x, d_z, d_d, d_w = vjp(do_ref[:, sl])
            dy_ref[:, sl] = d_y
            dx_ref[:, sl] = d_x
            dz_ref[:, sl] = d_z.astype(BF16)
            dd_ref[:, sl] += d_d
            dw_ref[:, sl] += d_w

    pspec = _full_spec((1, SSD_D))
    row = _row_spec(tm, SSD_D)
    return _pcall("ssd_out_bwd", body, (t // tm,),
                  [row, _row_spec(tm, SSD_D, 0), _row_spec(tm, SSD_D, 4), pspec, pspec, _row_spec(tm, SSD_D, 1), _ANY],
                  [row, row, _row_spec(tm, SSD_D, 4), pspec, pspec],
                  [_sds((t, SSD_D)), _sds((t, SSD_D)), _sds((t, BIG), BF16), _sds((1, SSD_D)), _sds((1, SSD_D))],
                  aliases={6: 2})(y, xbc, proj, d_skip, nw, d_ocat, d_proj)


def _res1_fn(x, mix, g_pm, g_pf):
    x1 = x + _rms(mix, g_pm)
    return x1, _rms(x1, g_pf)


def _res1_fwd(x, mix, g_pm, g_pf):
    t = x.shape[0]
    tm = _row_tile(t)

    def body(x_ref, m_ref, a_ref, b_ref, x1_ref, h2_ref):
        x1, h2 = _res1_fn(x_ref[...], m_ref[...], a_ref[...], b_ref[...])
        x1_ref[...] = x1
        h2_ref[...] = h2.astype(BF16)

    row, pspec = _row_spec(tm, D_MODEL), _full_spec((1, D_MODEL))
    return _pcall("res1_fwd", body, (t // tm,), [row, row, pspec, pspec], [row, row],
                  [_sds((t, D_MODEL)), _sds((t, D_MODEL), BF16)], sem=("parallel",))(x, mix, g_pm, g_pf)


def _res1_bwd(x, mix, g_pm, g_pf, d_x1, d_h2, pair_send):
    t = x.shape[0]
    tm = _row_tile(t)
    ns = len(pair_send)
    steps = t // tm

    def body(x_ref, m_ref, a_ref, b_ref, c1_ref, c2_ref, *rest):
        ps_in, rest = rest[:ns], rest[ns:]
        dx_ref, dm_ref, da_ref, db_ref = rest[:4]
        ps_out, sems = rest[4:4 + ns], rest[4 + ns:]
        ps_start, ps_finish = _pair_send_ops(ps_in, ps_out, sems)
        pl.when(pl.program_id(0) == 0)(ps_start)
        _zero_at_first([da_ref, db_ref], pl.program_id(0) == 0)
        _, vjp = jax.vjp(_res1_fn, x_ref[...], m_ref[...], a_ref[...], b_ref[...])
        d_x, d_m, d_a, d_b = vjp((c1_ref[...], c2_ref[...]))
        dx_ref[...] = d_x
        dm_ref[...] = d_m.astype(BF16)
        da_ref[...] += d_a
        db_ref[...] += d_b
        pl.when(pl.program_id(0) == steps - 1)(ps_finish)

    row, pspec = _row_spec(tm, D_MODEL), _full_spec((1, D_MODEL))
    outs = pl.pallas_call(
        body, name="res1_bwd", grid=(steps,),
        in_specs=[row, row, pspec, pspec, row, row] + [_ANY] * ns, out_specs=[row, row, pspec, pspec] + [_ANY] * ns,
        out_shape=[_sds((t, D_MODEL)), _sds((t, D_MODEL), BF16), _sds((1, D_MODEL)), _sds((1, D_MODEL))]
        + _pair_send_shapes(pair_send),
        scratch_shapes=[pltpu.SemaphoreType.DMA((ns,))] * 2,
        compiler_params=pltpu.CompilerParams(dimension_semantics=("arbitrary",), vmem_limit_bytes=VMEM_LIMIT,
                                             has_side_effects=True))(x, mix, g_pm, g_pf, d_x1, d_h2, *pair_send)
    return outs[:4], outs[4:]


def _final_fn(x1, f, g_po, tgt):
    err = x1 + _rms(f, g_po) - tgt
    return 0.5 * jnp.sum(jnp.mean(err * err, axis=-1))


def _final(x1, f, g_po, tgt):
    t = x1.shape[0]
    tm = _row_tile(t)

    def body(x_ref, f_ref, g_ref, t_ref, loss_ref, dx_ref, df_ref, dg_ref):
        _zero_at_first([loss_ref, dg_ref], pl.program_id(0) == 0)
        loss, (d_x, d_f, d_g) = jax.value_and_grad(_final_fn, argnums=(0, 1, 2))(x_ref[...], f_ref[...], g_ref[...], t_ref[...])
        loss_ref[...] += jnp.broadcast_to(loss, loss_ref.shape)
        dx_ref[...] = d_x
        df_ref[...] = d_f.astype(BF16)
        dg_ref[...] += d_g

    row, pspec = _row_spec(tm, D_MODEL), _full_spec((1, D_MODEL))
    return _pcall("final", body, (t // tm,), [row, row, pspec, row], [_full_spec((8, LANE)), row, row, pspec],
                  [_sds((8, LANE)), _sds((t, D_MODEL)), _sds((t, D_MODEL), BF16), _sds((1, D_MODEL))])(x1, f, g_po, tgt)


def _rms1_bwd(x, g, d_h_a, d_h_b, d_x1):
    t = x.shape[0]
    tm = _row_tile(t)

    def body(x_ref, g_ref, dha_ref, dhb_ref, dx1_ref, dx_ref, dg_ref):
        _zero_at_first([dg_ref], pl.program_id(0) == 0)
        _, vjp = jax.vjp(_rms, x_ref[...], g_ref[...])
        d_x, d_g = vjp(dha_ref[...] + dhb_ref[...])
        dx_ref[...] = d_x + dx1_ref[...]
        dg_ref[...] += d_g

    row, pspec = _row_spec(tm, D_MODEL), _full_spec((1, D_MODEL))
    return _pcall("rms1_bwd", body, (t // tm,), [row, pspec, row, row, row], [row, pspec],
                  [_sds((t, D_MODEL)), _sds((1, D_MODEL))])(x, g, d_h_a, d_h_b, d_x1)


def _qkv_fn(mode):
    def fn(x, *wrows):
        y = _silu(_conv(x, wrows))
        if mode == "q":
            return _l2n(y) * (GDN_DK ** -0.5)
        if mode == "k":
            return _l2n(y)
        return y
    return fn


def _seq_spec(s, tc, off):
    return pl.BlockSpec((s, tc), lambda j, b, _o=off: (b, _o + j))


def _par_spec(rows, tc, off):
    return pl.BlockSpec((rows, tc), lambda j, b, _o=off: (0, _o + j))


def _gdn_conv_fwd(mode, proj, w, bsz, s):
    off = {"q": 0, "k": GDN_HEADS, "v": 2 * GDN_HEADS}[mode]
    fn = _qkv_fn(mode)

    def body(x_ref, w_ref, y_ref):
        y_ref[...] = fn(x_ref[...].astype(F32), *[w_ref[k:k + 1, :] for k in range(CONV_K)])

    return _pcall("gdn_conv_fwd_" + mode, body, (GDN_HEADS, bsz),
                  [_seq_spec(s, GDN_DK, off), _par_spec(CONV_K, GDN_DK, off)], _seq_spec(s, GDN_DK, 0),
                  _sds((bsz * s, GDN_QK)), sem=("parallel", "parallel"))(proj, w)


def _gdn_conv_bwd(mode, proj, w, d_y, d_proj, bsz, s):
    off = {"q": 0, "k": GDN_HEADS, "v": 2 * GDN_HEADS}[mode]
    fn = _qkv_fn(mode)

    def body(x_ref, w_ref, dy_ref, _, dx_ref, dw_ref):
        _zero_at_first([dw_ref], pl.program_id(1) == 0)
        _, vjp = jax.vjp(fn, x_ref[...].astype(F32), *[w_ref[k:k + 1, :] for k in range(CONV_K)])
        grads = vjp(dy_ref[...])
        dx_ref[...] = grads[0].astype(BF16)
        for k in range(CONV_K):
            dw_ref[k:k + 1, :] += grads[1 + k]

    return _pcall("gdn_conv_bwd_" + mode, body, (GDN_HEADS, bsz),
                  [_seq_spec(s, GDN_DK, off), _par_spec(CONV_K, GDN_DK, off), _seq_spec(s, GDN_DK, 0), _ANY],
                  [_seq_spec(s, GDN_DK, off), _par_spec(CONV_K, GDN_DK, 0)],
                  [_sds(d_proj.shape, BF16), _sds((CONV_K, GDN_QK))], sem=("parallel", "arbitrary"), aliases={3: 0})(
                      proj, w, d_y, d_proj)


def _ssd_conv_fn(x, bias, *wrows):
    return _silu(_conv(x, wrows) + bias)


_XBC_OFF = (5 * 1024) // LANE


def _ssd_conv_fwd(proj, w, bias, bsz, s):
    nt_ = SSD_CONV_CH // LANE

    def body(x_ref, w_ref, b_ref, y_ref):
        y_ref[...] = _ssd_conv_fn(x_ref[...].astype(F32), b_ref[...], *[w_ref[k:k + 1, :] for k in range(CONV_K)])

    return _pcall("ssd_conv_fwd", body, (nt_, bsz),
                  [_seq_spec(s, LANE, _XBC_OFF), _par_spec(CONV_K, LANE, 0), _par_spec(1, LANE, 0)], _seq_spec(s, LANE, 0),
                  _sds((bsz * s, SSD_CONV_CH)), sem=("parallel", "parallel"))(proj, w, bias)


def _ssd_conv_bwd(proj, w, bias, d_y, d_proj, bsz, s):
    nt_ = SSD_CONV_CH // LANE

    def body(x_ref, w_ref, b_ref, dy_ref, _, dx_ref, dw_ref, db_ref):
        _zero_at_first([dw_ref, db_ref], pl.program_id(1) == 0)
        _, vjp = jax.vjp(_ssd_conv_fn, x_ref[...].astype(F32), b_ref[...], *[w_ref[k:k + 1, :] for k in range(CONV_K)])
        grads = vjp(dy_ref[...])
        dx_ref[...] = grads[0].astype(BF16)
        db_ref[...] += grads[1]
        for k in range(CONV_K):
            dw_ref[k:k + 1, :] += grads[2 + k]

    return _pcall("ssd_conv_bwd", body, (nt_, bsz),
                  [_seq_spec(s, LANE, _XBC_OFF), _par_spec(CONV_K, LANE, 0), _par_spec(1, LANE, 0), _seq_spec(s, LANE, 0), _ANY],
                  [_seq_spec(s, LANE, _XBC_OFF), _par_spec(CONV_K, LANE, 0), _par_spec(1, LANE, 0)],
                  [_sds(d_proj.shape, BF16), _sds((CONV_K, SSD_CONV_CH)), _sds((1, SSD_CONV_CH))],
                  sem=("parallel", "arbitrary"), aliases={4: 0})(proj, w, bias, d_y, d_proj)


_FFN_TC = 256
_FFN_NT = D_FF // _FFN_TC


def _ffn_act_fn(xg, xu, bg, bu, *wrows):
    k_w = FFN_CONV_K
    gate = _conv(xg, wrows[:k_w]) + bg
    up = _conv(xu, wrows[k_w:]) + bu
    return _silu(gate) * up


def _ffn_act_fwd(u_pre, w, bias, bsz, s):
    def body(xg_ref, xu_ref, wg_ref, wu_ref, bg_ref, bu_ref, a_ref):
        rows = [wg_ref[k:k + 1, :] for k in range(FFN_CONV_K)] + [wu_ref[k:k + 1, :] for k in range(FFN_CONV_K)]
        a_ref[...] = _ffn_act_fn(xg_ref[...].astype(F32), xu_ref[...].astype(F32), bg_ref[...], bu_ref[...], *rows).astype(BF16)

    return _pcall("ffn_act_fwd", body, (_FFN_NT, bsz),
                  [_seq_spec(s, _FFN_TC, 0), _seq_spec(s, _FFN_TC, _FFN_NT),
                   _par_spec(FFN_CONV_K, _FFN_TC, 0), _par_spec(FFN_CONV_K, _FFN_TC, _FFN_NT),
                   _par_spec(1, _FFN_TC, 0), _par_spec(1, _FFN_TC, _FFN_NT)],
                  _seq_spec(s, _FFN_TC, 0), _sds((bsz * s, D_FF), BF16), sem=("parallel", "parallel"))(
                      u_pre, u_pre, w, w, bias, bias)


def _ffn_act_bwd(u_pre, w, bias, d_a, bsz, s):
    def body(xg_ref, xu_ref, wg_ref, wu_ref, bg_ref, bu_ref, da_ref, dg_ref, du_ref, dwg_ref, dwu_ref, dbg_ref, dbu_ref):
        _zero_at_first([dwg_ref, dwu_ref, dbg_ref, dbu_ref], pl.program_id(1) == 0)
        rows = [wg_ref[k:k + 1, :] for k in range(FFN_CONV_K)] + [wu_ref[k:k + 1, :] for k in range(FFN_CONV_K)]
        _, vjp = jax.vjp(_ffn_act_fn, xg_ref[...].astype(F32), xu_ref[...].astype(F32), bg_ref[...], bu_ref[...], *rows)
        grads = vjp(da_ref[...])
        dg_ref[...] = grads[0].astype(BF16)
        du_ref[...] = grads[1].astype(BF16)
        dbg_ref[...] += grads[2]
        dbu_ref[...] += grads[3]
        for k in range(FFN_CONV_K):
            dwg_ref[k:k + 1, :] += grads[4 + k]
            dwu_ref[k:k + 1, :] += grads[4 + FFN_CONV_K + k]

    seq0, par3, par1 = _seq_spec(s, _FFN_TC, 0), _par_spec(FFN_CONV_K, _FFN_TC, 0), _par_spec(1, _FFN_TC, 0)
    return _pcall("ffn_act_bwd", body, (_FFN_NT, bsz),
                  [seq0, _seq_spec(s, _FFN_TC, _FFN_NT), par3, _par_spec(FFN_CONV_K, _FFN_TC, _FFN_NT),
                   par1, _par_spec(1, _FFN_TC, _FFN_NT), seq0],
                  [seq0, seq0, par3, par3, par1, par1],
                  [_sds((bsz * s, D_FF), BF16), _sds((bsz * s, D_FF), BF16), _sds((FFN_CONV_K, D_FF)), _sds((FFN_CONV_K, D_FF)),
                   _sds((1, D_FF)), _sds((1, D_FF))], sem=("parallel", "arbitrary"))(u_pre, u_pre, w, w, bias, bias, d_a)


_GP = GDN_HEADS // 2
_SP = SSD_HEADS // 2


def _pair_lanes(p):
    return slice(2 * p * LANE, (2 * p + 1) * LANE), slice((2 * p + 1) * LANE, (2 * p + 2) * LANE)


_LAST = slice(CHUNK - 1, CHUNK)


def _gdn_args(p, q_ref, k_ref, v_ref, g_ref, b_ref, gr_ref):
    la, lb = _pair_lanes(p)
    return (q_ref[:, la], q_ref[:, lb], k_ref[:, la], k_ref[:, lb], v_ref[:, la], v_ref[:, lb], g_ref[:, la], g_ref[:, lb],
            b_ref[:, la], b_ref[:, lb], gr_ref[p], g_ref[_LAST, la], g_ref[_LAST, lb])


def _gdn_fwd(q, k, v, act, cum, gc_row, bsz, n):
    def body(q_ref, k_ref, v_ref, act_ref, cum_ref, gr_ref, o_ref, sin_ref, s_scr, g_ref, b_ref):
        _zero_at_first([s_scr], pl.program_id(1) == 0)
        _expand_lanes(act_ref, b_ref, 0, GDN_HEADS, GDN_DK)
        _expand_lanes(cum_ref, g_ref, _G_LO, GDN_HEADS, GDN_DK)
        flat = []
        for p in range(_GP):
            flat += [*_gdn_args(p, q_ref, k_ref, v_ref, g_ref, b_ref, gr_ref), s_scr[2 * p], s_scr[2 * p + 1]]
        sin_ref[...] = s_scr[...]
        outs = _gdn_multi(*flat)
        for p in range(_GP):
            la, lb = _pair_lanes(p)
            o_ref[:, la], o_ref[:, lb], s_scr[2 * p], s_scr[2 * p + 1] = outs[4 * p:4 * p + 4]

    tspec = pl.BlockSpec((CHUNK, GDN_V), lambda b, c: (b * n + c, 0))
    rspec = pl.BlockSpec((_GP, 1, LANE), lambda b, c: (b * n + c, 0, 0))
    sspec = pl.BlockSpec((GDN_HEADS, LANE, LANE), lambda b, c: (b * n + c, 0, 0))
    nspec = pl.BlockSpec((CHUNK, SMALL), lambda b, c: (b * n + c, 0))
    wide = pltpu.VMEM((CHUNK, GDN_V), F32)
    return _pcall("gdn_fwd", body, (bsz, n), [tspec] * 3 + [nspec, nspec, rspec], [tspec, sspec],
                  [_sds((bsz * n * CHUNK, GDN_V)), _sds((bsz * n * GDN_HEADS, LANE, LANE))],
                  scratch=[pltpu.VMEM((GDN_HEADS, LANE, LANE), F32), wide, wide], sem=("parallel", "arbitrary"))(
                      q, k, v, act, cum, gc_row)


def _gdn_bwd(q, k, v, act, cum, gc_row, s_in, d_o, bsz, n, scatter):
    ns = len(scatter)

    def body(q_ref, k_ref, v_ref, act_ref, cum_ref, gr_ref, sin_ref, do_ref, *rest):
        sc_in, rest = rest[:ns], rest[ns:]
        dq_ref, dk_ref, dv_ref, dact_ref, dcum_ref, dgr_ref = rest[:6]
        sc_out, rest = rest[6:6 + ns], rest[6 + ns:]
        ds_scr, g_ref, b_ref, dg_ref, db_ref, send_sems, recv_sems = rest
        step = pl.program_id(0) * n + pl.program_id(1)
        sc_start, sc_finish = _scatter_ops(sc_in, sc_out, (send_sems, recv_sems))
        pl.when(step == 0)(sc_start)
        _zero_at_first([ds_scr], pl.program_id(1) == 0)
        _expand_lanes(act_ref, b_ref, 0, GDN_HEADS, GDN_DK)
        _expand_lanes(cum_ref, g_ref, _G_LO, GDN_HEADS, GDN_DK)
        flat, cots = [], []
        for p in range(_GP):
            la, lb = _pair_lanes(p)
            flat += [*_gdn_args(p, q_ref, k_ref, v_ref, g_ref, b_ref, gr_ref), sin_ref[2 * p], sin_ref[2 * p + 1]]
            cots += [do_ref[:, la], do_ref[:, lb], ds_scr[2 * p], ds_scr[2 * p + 1]]
        _, vjp = jax.vjp(_gdn_multi, *flat)
        grads = vjp(tuple(cots))
        for p in range(_GP):
            la, lb = _pair_lanes(p)
            cts = grads[_GDN_NARGS * p:_GDN_NARGS * (p + 1)]
            for ref, i in ((dq_ref, 0), (dk_ref, 2), (dv_ref, 4), (dg_ref, 6), (db_ref, 8)):
                ref[:, la] = cts[i]
                ref[:, lb] = cts[i + 1]
            dgr_ref[p] = cts[10]
            dg_ref[_LAST, la] += cts[11]
            dg_ref[_LAST, lb] += cts[12]
            ds_scr[2 * p] = cts[13]
            ds_scr[2 * p + 1] = cts[14]
        dact_ref[...] = _reduce_lanes(db_ref, 0, GDN_HEADS, GDN_DK)
        dcum_ref[...] = _reduce_lanes(dg_ref, _G_LO, GDN_HEADS, GDN_DK)
        pl.when(step == bsz * n - 1)(sc_finish)

    tspec = pl.BlockSpec((CHUNK, GDN_V), lambda b, c: (b * n + (n - 1 - c), 0))
    nspec = pl.BlockSpec((CHUNK, SMALL), lambda b, c: (b * n + (n - 1 - c), 0))
    rspec = pl.BlockSpec((_GP, 1, LANE), lambda b, c: (b * n + (n - 1 - c), 0, 0))
    sspec = pl.BlockSpec((GDN_HEADS, LANE, LANE), lambda b, c: (b * n + (n - 1 - c), 0, 0))
    tok_shape, nar_shape = _sds((bsz * n * CHUNK, GDN_V)), _sds((bsz * n * CHUNK, SMALL))
    wide = pltpu.VMEM((CHUNK, GDN_V), F32)
    outs = pl.pallas_call(
        body, name="gdn_bwd", grid=(bsz, n),
        in_specs=[tspec] * 3 + [nspec, nspec, rspec, sspec, tspec] + [_ANY] * ns,
        out_specs=[tspec] * 3 + [nspec, nspec, rspec] + [_ANY] * ns,
        out_shape=[tok_shape] * 3 + [nar_shape, nar_shape, _sds((bsz * n * _GP, 1, LANE))]
        + [_sds((3,) + a.shape[1:], a.dtype) for a in scatter],
        scratch_shapes=[pltpu.VMEM((GDN_HEADS, LANE, LANE), F32), wide, wide, wide, wide] + _scatter_sems(ns),
        compiler_params=pltpu.CompilerParams(dimension_semantics=("arbitrary", "arbitrary"), vmem_limit_bytes=VMEM_LIMIT,
                                             has_side_effects=True))(q, k, v, act, cum, gc_row, s_in, d_o, *scatter)
    return outs[:6], outs[6:]


_B_OFF = SSD_D // LANE
_C_OFF = (SSD_D + SSD_BC) // LANE
_PPG = _SP // SSD_GROUPS


def _ssd_args(p, x_ref, dt_ref, a_ref, ar_ref):
    lp = slice(p * LANE, (p + 1) * LANE)
    gi = p // _PPG
    b_sl = slice((_B_OFF + gi) * LANE, (_B_OFF + gi + 1) * LANE)
    c_sl = slice((_C_OFF + gi) * LANE, (_C_OFF + gi + 1) * LANE)
    return (x_ref[:, lp], dt_ref[:, lp], a_ref[:, lp], ar_ref[p], a_ref[_LAST, lp], x_ref[:, b_sl], x_ref[:, c_sl])


def _ssd_fwd(xbc, dt, acs, acs_row, bsz, n):
    def body(x_ref, dt_ref, a_ref, ar_ref, y_ref, sin_ref, s_scr):
        _zero_at_first([s_scr], pl.program_id(1) == 0)
        flat = []
        for p in range(_SP):
            flat += [*_ssd_args(p, x_ref, dt_ref, a_ref, ar_ref), s_scr[p]]
        sin_ref[...] = s_scr[...]
        outs = _ssd_multi(*flat)
        for p in range(_SP):
            y_ref[:, p * LANE:(p + 1) * LANE], s_scr[p] = outs[2 * p:2 * p + 2]

    tspec = pl.BlockSpec((CHUNK, SSD_D), lambda b, c: (b * n + c, 0))
    return _pcall("ssd_fwd", body, (bsz, n),
                  [pl.BlockSpec((CHUNK, SSD_CONV_CH), lambda b, c: (b * n + c, 0)), tspec, tspec,
                   pl.BlockSpec((_SP, 1, LANE), lambda b, c: (b * n + c, 0, 0))],
                  [tspec, pl.BlockSpec((_SP, LANE, LANE), lambda b, c: (b * n + c, 0, 0))],
                  [_sds((bsz * n * CHUNK, SSD_D)), _sds((bsz * n * _SP, LANE, LANE))],
                  scratch=[pltpu.VMEM((_SP, LANE, LANE), F32)], sem=("parallel", "arbitrary"))(xbc, dt, acs, acs_row)


def _ssd_bwd(xbc, dt, acs, acs_row, s_in, d_y, d_x_skip, bsz, n):
    def body(x_ref, dt_ref, a_ref, ar_ref, sin_ref, dy_ref, dsk_ref, dx_ref, ddt_ref, da_ref, dar_ref, ds_scr):
        _zero_at_first([ds_scr], pl.program_id(1) == 0)
        d_b = [None] * SSD_GROUPS
        d_c = [None] * SSD_GROUPS
        flat, cots = [], []
        for p in range(_SP):
            flat += [*_ssd_args(p, x_ref, dt_ref, a_ref, ar_ref), sin_ref[p]]
            cots += [dy_ref[:, p * LANE:(p + 1) * LANE], ds_scr[p]]
        _, vjp = jax.vjp(_ssd_multi, *flat)
        grads = vjp(tuple(cots))
        for p in range(_SP):
            lp = slice(p * LANE, (p + 1) * LANE)
            gi = p // _PPG
            cts = grads[_SSD_NARGS * p:_SSD_NARGS * (p + 1)]
            dx_ref[:, lp] = cts[0] + dsk_ref[:, lp]
            ddt_ref[:, lp] = cts[1]
            da_ref[:, lp] = cts[2]
            dar_ref[p] = cts[3]
            da_ref[_LAST, lp] += cts[4]
            d_b[gi] = cts[5] if d_b[gi] is None else d_b[gi] + cts[5]
            d_c[gi] = cts[6] if d_c[gi] is None else d_c[gi] + cts[6]
            ds_scr[p] = cts[7]
        for gi in range(SSD_GROUPS):
            dx_ref[:, (_B_OFF + gi) * LANE:(_B_OFF + gi + 1) * LANE] = d_b[gi]
            dx_ref[:, (_C_OFF + gi) * LANE:(_C_OFF + gi + 1) * LANE] = d_c[gi]

    def rev(b, c):
        return b * n + (n - 1 - c)

    tspec = pl.BlockSpec((CHUNK, SSD_D), lambda b, c: (rev(b, c), 0))
    xspec = pl.BlockSpec((CHUNK, SSD_CONV_CH), lambda b, c: (rev(b, c), 0))
    rspec = pl.BlockSpec((_SP, 1, LANE), lambda b, c: (rev(b, c), 0, 0))
    tok_shape = _sds((bsz * n * CHUNK, SSD_D))
    return _pcall("ssd_bwd", body, (bsz, n),
                  [xspec, tspec, tspec, rspec, pl.BlockSpec((_SP, LANE, LANE), lambda b, c: (rev(b, c), 0, 0)), tspec, tspec],
                  [xspec, tspec, tspec, rspec],
                  [_sds((bsz * n * CHUNK, SSD_CONV_CH)), tok_shape, tok_shape, _sds((bsz * n * _SP, 1, LANE))],
                  scratch=[pltpu.VMEM((_SP, LANE, LANE), F32)], sem=("parallel", "arbitrary"))(
                      xbc, dt, acs, acs_row, s_in, d_y, d_x_skip)


def _rep(p, width):
    return jnp.repeat(p.reshape(-1), width).reshape(1, -1)


def _to_rows(narrow, lo, hi, bsz, n):
    heads = hi - lo
    a = narrow[:, lo:hi].reshape(bsz, n, CHUNK, heads)
    return jnp.transpose(a, (0, 1, 3, 2)).reshape(bsz * n * (heads // 2), 1, 2 * CHUNK)


def _from_rows(rows, heads, bsz, n):
    return jnp.transpose(rows.reshape(bsz, n, heads, CHUNK), (0, 1, 3, 2)).reshape(bsz * n * CHUNK, heads)


def _narrow_row(gdn_part, ssd_part):
    return jnp.pad(jnp.concatenate([gdn_part, ssd_part], axis=1), ((0, 0), (_G_LO, SMALL - _S_HI)))


def _local_step(x, tgt, p):
    bsz, s, _ = x.shape
    t = bsz * s
    n = s // CHUNK
    x2 = x.reshape(t, D_MODEL)
    tgt2 = tgt.reshape(t, D_MODEL)
    gate_bias = _narrow_row(p["gdn_dt_bias"], p["ssd_dt_bias"])
    gate_a_log = _narrow_row(p["gdn_a_log"], p["ssd_a_log"])
    d_skip = _rep(p["ssd_d"], SSD_HEADDIM)

    h, (g_in, g_gcw, g_scw, g_fcw) = _rms_fwd_gathering(
        "rms0_fwd", x2, p["pre_mix_norm"], [p["w_in"]] + [p[k] for k in _COL_SHARDED_SMALL], [True, False, False, False])
    w_in = _chips_to_cols(g_in)
    p = dict(p, gdn_conv_w=_chips_to_cols(g_gcw), ssd_conv_w=_chips_to_cols(g_scw), ffn_conv_w=_chips_to_cols(g_fcw))
    w_big = jnp.concatenate([w_in[:, :4096], w_in[:, 4112:6672]], axis=1)
    w_small = jnp.concatenate([w_in[:, 4096:4112], w_in[:, 6672:6688], jnp.zeros((D_MODEL, SMALL - 32), BF16)], axis=1)
    proj, g_out, g_up, g_down = _matmul_nn_gathering(
        "mm_in_big", h, w_big, BF16, (_pick(t, (1024, 512, 256, 128, 64)), BIG // 2), [p["w_out"], p["w_up"], p["w_down"]])
    w_out, w_up, w_down = g_out.reshape(-1, D_MODEL), _chips_to_cols(g_up), g_down.reshape(-1, D_MODEL)
    small = _matmul("mm_in_small", h, w_small, "nn", F32, (1024, 128))
    gact, cum, dt, acs = _gates_fwd(small, gate_bias, gate_a_log)
    gc_row = _to_rows(cum, _G_LO, _G_HI, bsz, n)
    acs_row = _to_rows(cum, _S_LO, _S_HI, bsz, n)
    q = _gdn_conv_fwd("q", proj, p["gdn_conv_w"], bsz, s)
    k = _gdn_conv_fwd("k", proj, p["gdn_conv_w"], bsz, s)
    v = _gdn_conv_fwd("v", proj, p["gdn_conv_w"], bsz, s)
    o, gdn_s = _gdn_fwd(q, k, v, gact, cum, gc_row, bsz, n)
    ocat = _gdn_out_fwd(o, proj, p["gdn_norm_w"])
    xbc = _ssd_conv_fwd(proj, p["ssd_conv_w"], p["ssd_conv_b"], bsz, s)
    y, ssd_s = _ssd_fwd(xbc, dt, acs, acs_row, bsz, n)
    ocat = _ssd_out_fwd(y, xbc, proj, d_skip, p["ssd_norm_w"], ocat)
    mix = _matmul("mm_out", ocat, w_out, "nn", F32, (1024, 1024))
    x1, h2 = _res1_fwd(x2, mix, p["post_mix_norm"], p["pre_ffn_norm"])
    u_pre = _matmul("mm_up", h2, w_up, "nn", BF16, (1024, 2816))
    act = _ffn_act_fwd(u_pre, p["ffn_conv_w"], p["ffn_conv_b"], bsz, s)
    f = _matmul("mm_down", act, w_down, "nn", F32, (1024, 1024))
    loss_acc, d_out, d_f, g_post_ffn = _final(x1, f, p["post_ffn_norm"], tgt2)

    grads = {"post_ffn_norm": g_post_ffn}
    d_act = _matmul("mm_down_dx", d_f, w_down, "nt", F32, (1024, 2816))
    dw_down = _matmul("mm_down_dw", act, d_f, "tn", F32, (2816, 1024, 1024))
    d_gate, d_up, dwg, dwu, dbg, dbu = _ffn_act_bwd(u_pre, p["ffn_conv_w"], p["ffn_conv_b"], d_act, bsz, s)
    grads["ffn_conv_w"] = jnp.concatenate([dwg, dwu], axis=1)
    grads["ffn_conv_b"] = jnp.concatenate([dbg, dbu], axis=1)
    d_h2 = _matmul_nt_split("mm_up_dx", d_gate, d_up, w_up, 512)
    dw_up = _matmul("mm_up_dw_gate", h2, d_gate, "tn", F32, (1024, 2816, 1024), part=(2 * D_FF, 0, None))
    dw_up = _matmul("mm_up_dw_up", h2, d_up, "tn", F32, (1024, 2816, 1024), part=(2 * D_FF, D_FF, dw_up))
    dw_down = dw_down.reshape(N_CHIPS, -1, D_MODEL)
    (d_x1, d_mix, grads["post_mix_norm"], grads["pre_ffn_norm"]), (r_up, r_down) = _res1_bwd(
        x2, mix, p["post_mix_norm"], p["pre_ffn_norm"], d_out, d_h2, [dw_up, dw_down])
    d_ocat = _matmul("mm_out_dx", d_mix, w_out, "nt", F32, (1024, 2048))
    dw_out = _matmul("mm_out_dw", ocat, d_mix, "tn", F32, (2048, 1024, 1024)).reshape(N_CHIPS, -1, D_MODEL)

    (r_out,) = _pair_send_other_half("pair_reduce_send_early", [dw_out])
    early = _pair_adds("early", [dw_out, dw_up, dw_down], [r_out, r_up, r_down])

    d_o, d_proj, grads["gdn_norm_w"] = _gdn_out_bwd(o, proj, p["gdn_norm_w"], d_ocat)
    (d_q, d_k, d_v, d_act_g, d_cum_g, d_gc_row), early_landed = _gdn_bwd(
        q, k, v, gact, cum, gc_row, gdn_s, d_o, bsz, n, [ps[1] for ps in early])
    d_proj, dwq = _gdn_conv_bwd("q", proj, p["gdn_conv_w"], d_q, d_proj, bsz, s)
    d_proj, dwk = _gdn_conv_bwd("k", proj, p["gdn_conv_w"], d_k, d_proj, bsz, s)
    d_proj, dwv = _gdn_conv_bwd("v", proj, p["gdn_conv_w"], d_v, d_proj, bsz, s)
    grads["gdn_conv_w"] = jnp.concatenate([dwq, dwk, dwv], axis=1)

    d_y, d_xs_skip, d_proj, d_dskip, grads["ssd_norm_w"] = _ssd_out_bwd(y, xbc, proj, d_skip, p["ssd_norm_w"], d_ocat, d_proj)
    d_xbc, d_dt, d_acs, d_acs_row = _ssd_bwd(xbc, dt, acs, acs_row, ssd_s, d_y, d_xs_skip, bsz, n)
    d_proj, grads["ssd_conv_w"], grads["ssd_conv_b"] = _ssd_conv_bwd(proj, p["ssd_conv_w"], p["ssd_conv_b"], d_xbc, d_proj, bsz, s)

    d_cum_rows = jnp.concatenate([jnp.zeros((t, _G_LO), F32), _from_rows(d_gc_row, GDN_HEADS, bsz, n),
                                  _from_rows(d_acs_row, SSD_HEADS, bsz, n), jnp.zeros((t, SMALL - _S_HI), F32)], axis=1)
    d_small, d_gate_bias, d_gate_a_log = _gates_bwd(small, gate_bias, gate_a_log, d_act_g, [d_cum_g, d_cum_rows], d_dt, d_acs)
    grads["gdn_dt_bias"], grads["ssd_dt_bias"] = d_gate_bias[:, _G_LO:_G_HI], d_gate_bias[:, _S_LO:_S_HI]
    grads["gdn_a_log"], grads["ssd_a_log"] = d_gate_a_log[:, _G_LO:_G_HI], d_gate_a_log[:, _S_LO:_S_HI]
    dw_big = _matmul("mm_in_big_dw", h, d_proj, "tn", F32, (1024, 3328, 1024))
    dw_small = _matmul("mm_in_small_dw", h, d_small, "tn", F32)
    dw_in = jnp.concatenate([dw_big[:, :4096], dw_small[:, :16], dw_big[:, 4096:], dw_small[:, 16:32]], axis=1)
    late = _pair_sums("late", [_cols_to_chips(dw_in)])
    d_h_big, late_landed = _matmul_nt_scattering("mm_in_big_dx", d_proj, w_big, 512, [ps[1] for ps in late])
    d_h_small = _matmul("mm_in_small_dx", d_small, w_small, "nt", F32, (1024, 1024))
    grad_x, grads["pre_mix_norm"] = _rms1_bwd(x2, p["pre_mix_norm"], d_h_big, d_h_small, d_x1)
    grads["ssd_d"] = _head_sums(d_dskip)[:1, :SSD_HEADS]
    return loss_acc, grad_x.reshape(bsz, s, D_MODEL), grads, (late + early, list(late_landed) + list(early_landed))


def _head_sums(wide):
    def body(x_ref, o_ref):
        r, c = _iota2((D_MODEL, SMALL))
        o_ref[...] = _mask_dot(jnp.broadcast_to(x_ref[...], (8, D_MODEL)), ((r >> 6) == c).astype(F32), NN, True)

    return _pcall("head_sums", body, (1,), [_full_spec((1, D_MODEL))], _full_spec((8, SMALL)), _sds((8, SMALL)))(wide)


def _adamw_fn(w, g, m, v):
    m = ADAM_B1 * m + (1.0 - ADAM_B1) * g
    v = ADAM_B2 * v + (1.0 - ADAM_B2) * (g * g)
    m_hat = m / (1.0 - ADAM_B1 ** ADAM_STEP)
    v_hat = v / (1.0 - ADAM_B2 ** ADAM_STEP)
    delta = -ADAM_LR * (m_hat / (jnp.sqrt(v_hat) + ADAM_EPS) + ADAM_WD * w)
    return delta, m, v


def _adamw(name, w, g, m, v):
    r, c = w.shape
    tr = _pick(r, (256, 176, 128, 64, 8))

    def body(w_ref, g_ref, m_ref, v_ref, d_ref, m2_ref, v2_ref):
        d, m2, v2 = _adamw_fn(w_ref[...], g_ref[...], m_ref[...], v_ref[...])
        d_ref[...] = d
        m2_ref[...] = m2
        v2_ref[...] = v2

    spec = pl.BlockSpec((tr, c), lambda i: (i, 0))
    return _pcall(name, body, (r // tr,), [spec] * 4, [spec] * 3, [_sds((r, c))] * 3, sem=("parallel",))(w, g, m, v)


_ANY = pl.BlockSpec(memory_space=pl.ANY)
_OTHER_CHIPS = ((1, 0), (0, 1), (1, 1))


def _coords():
    return lax.axis_index("x"), lax.axis_index("y"), lax.axis_index("c")


def _flip(v, f):
    return 1 - v if f else v


def _gather_ops(ins, outs, sems, split):
    send_sems, recv_sems, fwd_send_sems, fwd_recv_sems, own_send_sems, own_recv_sems = sems
    n = len(ins)
    x, y, c = _coords()
    me = 2 * x + y
    sib = (x, y, 1 - c)

    def rows(a, core):
        if not split[a]:
            return slice(None)
        half = ins[a].shape[0] // 2
        return pl.ds(core * half, half)

    def chip(j):
        fx, fy = _OTHER_CHIPS[j]
        return _flip(x, fx), _flip(y, fy)

    def own_cp(a):
        return pltpu.make_async_remote_copy(ins[a], outs[a].at[me], own_send_sems.at[a], own_recv_sems.at[a],
                                            device_id=sib, device_id_type=MESH)

    def ici_cp(a, j):
        return pltpu.make_async_remote_copy(ins[a].at[rows(a, c)], outs[a].at[me, rows(a, c)],
                                            send_sems.at[a * 3 + j], recv_sems.at[a * 3 + j],
                                            device_id=(*chip(j), c), device_id_type=MESH)

    def landed_cp(a, j, sem_a, sem_b, core, to):
        cx, cy = chip(j)
        blk = outs[a].at[2 * cx + cy, rows(a, core)]
        return pltpu.make_async_remote_copy(blk, blk, sem_a.at[a * 3 + j], sem_b.at[a * 3 + j], device_id=to, device_id_type=MESH)

    pairs = [(a, j) for a in range(n) for j in range(3)]

    def start():
        for a in range(n):
            own_cp(a).start()
        for a, j in pairs:
            ici_cp(a, j).start()

    def forward():
        for a, j in pairs:
            landed_cp(a, j, send_sems, recv_sems, c, (*chip(j), c)).wait_recv()
            if split[a]:
                landed_cp(a, j, fwd_send_sems, fwd_recv_sems, c, sib).start()

    def finish():
        for a, j in pairs:
            if split[a]:
                landed_cp(a, j, fwd_send_sems, fwd_recv_sems, 1 - c, sib).wait_recv()
        for a in range(n):
            own_cp(a).wait_recv()
        for a, j in pairs:
            ici_cp(a, j).wait_send()
            if split[a]:
                landed_cp(a, j, fwd_send_sems, fwd_recv_sems, c, sib).wait_send()
        for a in range(n):
            own_cp(a).wait_send()

    return start, forward, finish


def _gather_sems(n):
    return [pltpu.SemaphoreType.DMA((3 * n,))] * 4 + [pltpu.SemaphoreType.DMA((n,))] * 2


def _matmul_nn_gathering(name, a, b, out_dtype, tiles, shards):
    m, k = a.shape
    n = b.shape[1]
    tm, tn = tiles
    nc = _pick(tn, (512, 256, 128))
    ns = len(shards)
    gi, gj = m // tm, n // tn
    steps = gi * gj

    def body(a_ref, b_ref, *rest):
        ins, o_ref, outs, sems = rest[:ns], rest[ns], rest[ns + 1:2 * ns + 1], rest[2 * ns + 1:]
        step = pl.program_id(0) * gj + pl.program_id(1)
        start, forward, finish = _gather_ops(ins, outs, sems, [True] * ns)
        pl.when(step == 0)(start)
        for c0 in range(0, tn, nc):
            o_ref[:, c0:c0 + nc] = _bdot(a_ref[...], b_ref[:, c0:c0 + nc], NN).astype(o_ref.dtype)
        pl.when(step == steps - 2)(forward)
        pl.when(step == steps - 1)(finish)

    assert steps >= 2
    return pl.pallas_call(
        body, name=name, grid=(gi, gj),
        in_specs=[pl.BlockSpec((tm, k), lambda i, j: (i, 0)), pl.BlockSpec((k, tn), lambda i, j: (0, j))] + [_ANY] * ns,
        out_specs=[pl.BlockSpec((tm, tn), lambda i, j: (i, j))] + [_ANY] * ns,
        out_shape=[_sds((m, n), out_dtype)] + [_sds((N_CHIPS,) + s.shape, s.dtype) for s in shards],
        scratch_shapes=_gather_sems(ns),
        compiler_params=pltpu.CompilerParams(dimension_semantics=("arbitrary", "arbitrary"), vmem_limit_bytes=VMEM_LIMIT,
                                             has_side_effects=True))(a, b, *shards)


_PEERS = tuple((fx, fy, fc) for fx in (0, 1) for fy in (0, 1) for fc in (0, 1))[1:]


def _allreduce_small(x):
    r = x.shape[0]

    def body(x_ref, o_ref, buf, send_sems, recv_sems):
        cx, cy, cc = _coords()
        me = 4 * cx + 2 * cy + cc
        sends = []
        for j, (fx, fy, fc) in enumerate(_PEERS):
            cp = pltpu.make_async_remote_copy(x_ref, buf.at[me], send_sems.at[j], recv_sems.at[j],
                                              device_id=(_flip(cx, fx), _flip(cy, fy), _flip(cc, fc)), device_id_type=MESH)
            cp.start()
            sends.append(cp)
        buf[pl.ds(me, 1)] = x_ref[...][None]
        for j, (fx, fy, fc) in enumerate(_PEERS):
            src = 4 * _flip(cx, fx) + 2 * _flip(cy, fy) + _flip(cc, fc)
            pltpu.make_async_remote_copy(x_ref, buf.at[src], send_sems.at[j], recv_sems.at[j],
                                         device_id=(_flip(cx, fx), _flip(cy, fy), _flip(cc, fc)), device_id_type=MESH).wait_recv()
        for cp in sends:
            cp.wait_send()
        acc = buf[0]
        for d in range(1, N_DEV):
            acc = acc + buf[d]
        o_ref[...] = acc

    vm = pl.BlockSpec(memory_space=pltpu.VMEM)
    return pl.pallas_call(
        body, name="allreduce_small", out_shape=_sds((r, LANE)), in_specs=[vm], out_specs=vm,
        scratch_shapes=[pltpu.VMEM((N_DEV, r, LANE), F32), pltpu.SemaphoreType.DMA((7,)), pltpu.SemaphoreType.DMA((7,))],
        compiler_params=pltpu.CompilerParams(has_side_effects=True, vmem_limit_bytes=VMEM_LIMIT))(x)


def _pair_sums(tag, arrs):
    return _pair_adds(tag, arrs, _pair_send_other_half("pair_reduce_send_" + tag, arrs))


def _pair_adds(tag, arrs, received):
    core = lax.axis_index("c").astype(jnp.int32).reshape(1)
    return [_pair_add("pair_add_%s_%d" % (tag, i), a, b, core) for i, (a, b) in enumerate(zip(arrs, received))]


def _pair_send_ops(ins, outs, sems):
    send_sems, recv_sems = sems
    x, y, c = _coords()

    def cp(a):
        half = ins[a].shape[1] // 2
        return pltpu.make_async_remote_copy(ins[a].at[:, pl.ds((1 - c) * half, half), :], outs[a], send_sems.at[a], recv_sems.at[a],
                                            device_id=(x, y, 1 - c), device_id_type=MESH)

    def start():
        for a in range(len(ins)):
            cp(a).start()

    def finish():
        for a in range(len(ins)):
            cp(a).wait_recv()
        for a in range(len(ins)):
            cp(a).wait_send()

    return start, finish


def _pair_send_shapes(arrs):
    return [_sds((a.shape[0], a.shape[1] // 2, a.shape[2]), a.dtype) for a in arrs]


def _pair_send_other_half(name, arrs):
    n = len(arrs)

    def body(*refs):
        start, finish = _pair_send_ops(refs[:n], refs[n:2 * n], refs[2 * n:])
        start()
        finish()

    return pl.pallas_call(
        body, name=name, out_shape=_pair_send_shapes(arrs), in_specs=[_ANY] * n, out_specs=[_ANY] * n,
        scratch_shapes=[pltpu.SemaphoreType.DMA((n,))] * 2, compiler_params=pltpu.CompilerParams(has_side_effects=True))(*arrs)


def _pair_fill(arrs):
    n = len(arrs)

    def body(*refs):
        bufs = refs[n:2 * n]
        send_sems, recv_sems = refs[2 * n:]
        x, y, c = _coords()
        sends = []
        for a in range(n):
            cp = pltpu.make_async_remote_copy(bufs[a].at[c], bufs[a].at[c], send_sems.at[a], recv_sems.at[a],
                                              device_id=(x, y, 1 - c), device_id_type=MESH)
            cp.start()
            sends.append(cp)
        for a in range(n):
            theirs = bufs[a].at[1 - c]
            pltpu.make_async_remote_copy(theirs, theirs, send_sems.at[a], recv_sems.at[a],
                                         device_id=(x, y, 1 - c), device_id_type=MESH).wait_recv()
        for cp in sends:
            cp.wait_send()

    return pl.pallas_call(
        body, name="pair_gather", out_shape=[_sds(a.shape, a.dtype) for a in arrs], in_specs=[_ANY] * n, out_specs=[_ANY] * n,
        scratch_shapes=[pltpu.SemaphoreType.DMA((n,))] * 2, input_output_aliases={a: a for a in range(n)},
        compiler_params=pltpu.CompilerParams(has_side_effects=True))(*arrs)


def _scatter_ops(ins, outs, sems):
    send_sems, recv_sems = sems
    x, y, c = _coords()

    def cp(a, j):
        fx, fy = _OTHER_CHIPS[j]
        to = 2 * _flip(x, fx) + _flip(y, fy)
        return pltpu.make_async_remote_copy(ins[a].at[to], outs[a].at[j], send_sems.at[a * 3 + j], recv_sems.at[a * 3 + j],
                                            device_id=(_flip(x, fx), _flip(y, fy), c), device_id_type=MESH)

    pairs = [(a, j) for a in range(len(ins)) for j in range(3)]

    def start():
        for a, j in pairs:
            cp(a, j).start()

    def finish():
        for a, j in pairs:
            cp(a, j).wait_recv()
        for a, j in pairs:
            cp(a, j).wait_send()

    return start, finish


def _scatter_sems(n):
    return [pltpu.SemaphoreType.DMA((3 * n,))] * 2


def _pair_add(name, full, recv, core):
    _, r, c = full.shape
    half = r // 2
    tr = _pick(half, (256, 176, 128, 64, 8))
    nb = half // tr

    def body(c_ref, a_ref, b_ref, o_ref, ob_ref):
        s = a_ref[...] + b_ref[...]
        o_ref[...] = s
        ob_ref[...] = s.astype(BF16)

    blk = pl.BlockSpec((1, tr, c), lambda k, i, cref: (k, i, 0))
    grid_spec = pltpu.PrefetchScalarGridSpec(
        num_scalar_prefetch=1, grid=(N_CHIPS, nb),
        in_specs=[pl.BlockSpec((1, tr, c), lambda k, i, cref: (k, cref[0] * nb + i, 0)), blk], out_specs=[blk, blk])
    return pl.pallas_call(
        body, name=name, out_shape=[_sds((N_CHIPS, half, c)), _sds((N_CHIPS, half, c), BF16)], grid_spec=grid_spec,
        compiler_params=pltpu.CompilerParams(dimension_semantics=("parallel", "parallel"), vmem_limit_bytes=VMEM_LIMIT))(
            core, full, recv)


def _chip_sum(name, landed, own, where):
    _, r, c = landed.shape
    tr = _pick(r, (256, 176, 128, 64, 16))

    def body(w_ref, l_ref, o_ref, s_ref):
        s_ref[0] = ((o_ref[0] + l_ref[0].astype(F32)) + l_ref[1].astype(F32)) + l_ref[2].astype(F32)

    grid_spec = pltpu.PrefetchScalarGridSpec(
        num_scalar_prefetch=1, grid=(r // tr,),
        in_specs=[pl.BlockSpec((3, tr, c), lambda i, wref: (0, i, 0)),
                  pl.BlockSpec((1, tr, c), lambda i, wref: (wref[0], i, 0))],
        out_specs=pl.BlockSpec((1, tr, c), lambda i, wref: (wref[1], i, 0)))
    return pl.pallas_call(
        body, name=name, out_shape=_sds((2, r, c)), grid_spec=grid_spec,
        compiler_params=pltpu.CompilerParams(dimension_semantics=("parallel",), vmem_limit_bytes=VMEM_LIMIT))(where, landed, own)


_WEIGHTS = ("pre_mix_norm", "w_in", "gdn_conv_w", "gdn_a_log", "gdn_dt_bias", "gdn_norm_w", "ssd_conv_w", "ssd_conv_b",
            "ssd_a_log", "ssd_dt_bias", "ssd_d", "ssd_norm_w", "w_out", "post_mix_norm", "pre_ffn_norm", "w_up",
            "ffn_conv_w", "ffn_conv_b", "w_down", "post_ffn_norm")
_BIG = ("w_in", "w_out", "w_up", "w_down")
_COL_SHARDED_SMALL = ("gdn_conv_w", "ssd_conv_w", "ffn_conv_w")
_SMALL = tuple(k for k in _WEIGHTS if k not in _BIG)


def _pack(arrs):
    flat = jnp.concatenate([a.reshape(-1) for a in arrs])
    rows = -(-flat.shape[0] // (8 * LANE)) * 8
    return jnp.pad(flat, (0, rows * LANE - flat.shape[0])).reshape(rows, LANE)


def _unpack(packed, shapes):
    flat = packed.reshape(-1)
    out, off = [], 0
    for shp in shapes:
        size = 1
        for d in shp:
            size *= d
        out.append(flat[off:off + size].reshape(shp))
        off += size
    return out


def _cols_to_chips(a):
    r, c4 = a.shape
    return jnp.transpose(a.reshape(r, N_CHIPS, c4 // N_CHIPS), (1, 0, 2))


def _chips_to_cols(a):
    k, r, c = a.shape
    return jnp.transpose(a, (1, 0, 2)).reshape(r, k * c)


def kernel(x, pre_mix_norm, w_in, gdn_conv_w, gdn_a_log, gdn_dt_bias, gdn_norm_w, ssd_conv_w, ssd_conv_b, ssd_a_log, ssd_dt_bias, ssd_d, ssd_norm_w, w_out, post_mix_norm, pre_ffn_norm, w_up, ffn_conv_w, ffn_conv_b, w_down, post_ffn_norm, loss_target, m_pre_mix_norm, m_w_in, m_gdn_conv_w, m_gdn_a_log, m_gdn_dt_bias, m_gdn_norm_w, m_ssd_conv_w, m_ssd_conv_b, m_ssd_a_log, m_ssd_dt_bias, m_ssd_d, m_ssd_norm_w, m_w_out, m_post_mix_norm, m_pre_ffn_norm, m_w_up, m_ffn_conv_w, m_ffn_conv_b, m_w_down, m_post_ffn_norm, v_pre_mix_norm, v_w_in, v_gdn_conv_w, v_gdn_a_log, v_gdn_dt_bias, v_gdn_norm_w, v_ssd_conv_w, v_ssd_conv_b, v_ssd_a_log, v_ssd_dt_bias, v_ssd_d, v_ssd_norm_w, v_w_out, v_post_mix_norm, v_pre_ffn_norm, v_w_up, v_ffn_conv_w, v_ffn_conv_b, v_w_down, v_post_ffn_norm):
    w = dict(zip(_WEIGHTS, (pre_mix_norm, w_in, gdn_conv_w, gdn_a_log, gdn_dt_bias, gdn_norm_w, ssd_conv_w, ssd_conv_b,
                            ssd_a_log, ssd_dt_bias, ssd_d, ssd_norm_w, w_out, post_mix_norm, pre_ffn_norm, w_up,
                            ffn_conv_w, ffn_conv_b, w_down, post_ffn_norm)))
    m = dict(zip(_WEIGHTS, (m_pre_mix_norm, m_w_in, m_gdn_conv_w, m_gdn_a_log, m_gdn_dt_bias, m_gdn_norm_w, m_ssd_conv_w,
                            m_ssd_conv_b, m_ssd_a_log, m_ssd_dt_bias, m_ssd_d, m_ssd_norm_w, m_w_out, m_post_mix_norm,
                            m_pre_ffn_norm, m_w_up, m_ffn_conv_w, m_ffn_conv_b, m_w_down, m_post_ffn_norm)))
    v = dict(zip(_WEIGHTS, (v_pre_mix_norm, v_w_in, v_gdn_conv_w, v_gdn_a_log, v_gdn_dt_bias, v_gdn_norm_w, v_ssd_conv_w,
                            v_ssd_conv_b, v_ssd_a_log, v_ssd_dt_bias, v_ssd_d, v_ssd_norm_w, v_w_out, v_post_mix_norm,
                            v_pre_ffn_norm, v_w_up, v_ffn_conv_w, v_ffn_conv_b, v_w_down, v_post_ffn_norm)))
    cx, cy, cc = _coords()
    chip = 2 * cx + cy

    p = {k: w[k] for k in _SMALL if k not in _COL_SHARDED_SMALL}
    for k in _BIG:
        p[k] = w[k][0].astype(BF16)
    for k in _COL_SHARDED_SMALL:
        p[k] = w[k][0]
    loss_acc, grad_x, grads, (pair_sum_list, landed_list) = _local_step(x, loss_target, p)
    loss = lax.psum(loss_acc[0, 0], ("x", "y", "c"))

    small_full_shapes = [grads[k].shape for k in _SMALL]
    summed = _unpack(_allreduce_small(_pack([grads[k] for k in _SMALL])), small_full_shapes)
    g_small = dict(zip(_SMALL, summed))
    for k in _COL_SHARDED_SMALL:
        width = w[k].shape[2]
        g_small[k] = lax.dynamic_slice_in_dim(g_small[k], chip * width, width, axis=1)

    pair_sums = dict(zip(_BIG, pair_sum_list))
    landed = dict(zip(_BIG, landed_list))
    where = jnp.stack([chip, cc]).astype(jnp.int32)
    mine = [_chip_sum("chip_sum_" + k, landed[k], pair_sums[k][0], where) for k in _BIG]
    both = _pair_fill(mine)
    g_big = {k: a.reshape(-1, a.shape[2]) for k, a in zip(_BIG, both)}

    out_g, out_d, out_m, out_v = {}, {}, {}, {}
    for k in _BIG:
        out_g[k] = g_big[k][None]
        d_, m_, v_ = _adamw("adamw_" + k, w[k][0], g_big[k], m[k][0], v[k][0])
        out_d[k], out_m[k], out_v[k] = d_[None], m_[None], v_[None]
    shapes = [w[k].shape for k in _SMALL]
    for k in _SMALL:
        out_g[k] = g_small[k].reshape(w[k].shape)
    packed = [_pack([d[k] for k in _SMALL]) for d in (w, out_g, m, v)]
    d_p, m_p, v_p = _adamw("adamw_small", *packed)
    for dst, src in ((out_d, d_p), (out_m, m_p), (out_v, v_p)):
        dst.update(zip(_SMALL, _unpack(src, shapes)))
    return (loss, grad_x, *[out_g[k] for k in _WEIGHTS], *[out_d[k] for k in _WEIGHTS],
            *[out_m[k] for k in _WEIGHTS], *[out_v[k] for k in _WEIGHTS])
```

```python
import functools

import jax
import jax.numpy as jnp
from jax import lax
from jax.experimental import pallas as pl
from jax.experimental.pallas import tpu as pltpu

F32 = jnp.float32
BF16 = jnp.bfloat16

D_MODEL = 1024
GDN_HEADS = 8
GDN_DK = 128
SSD_HEADS = 16
SSD_HEADDIM = 64
SSD_GROUPS = 2
SSD_STATE = 128
CONV_K = 4
CHUNK = 64
D_FF = 2816
FFN_CONV_K = 3
EPS = 1e-6
GDN_QK = GDN_HEADS * GDN_DK
GDN_V = GDN_QK
SSD_D = SSD_HEADS * SSD_HEADDIM
SSD_BC = SSD_GROUPS * SSD_STATE
SSD_CONV_CH = SSD_D + 2 * SSD_BC
BIG = 4 * 1024 + 1024 + SSD_CONV_CH
SMALL = 128
D_IN_PROJ = 6688
_SHARD_W = D_IN_PROJ // 4
_NARROW_A = 4096
_NARROW_B = 6672
LANE = 128
PAIR = 2 * CHUNK
NEG = -1e30
VMEM_LIMIT = 56 * 1024 * 1024

ADAM_LR = 0.001
ADAM_B1 = 0.9
ADAM_B2 = 0.999
ADAM_EPS = 1e-08
ADAM_WD = 0.01
ADAM_STEP = 10

N_CHIPS = 4
N_DEV = 8
MESH = pl.DeviceIdType.MESH

NN = ((1,), (0,))
NT = ((1,), (1,))
TN = ((0,), (0,))


def _bdot(a, b, dims):
    return lax.dot_general(a.astype(BF16), b.astype(BF16), (dims, ((), ())), preferred_element_type=F32)


def _split3(a):
    hi = a.astype(BF16)
    r1 = a - hi.astype(F32)
    mid = r1.astype(BF16)
    return hi, mid, (r1 - mid.astype(F32)).astype(BF16)


@jax.custom_vjp
def _nn(a, b):
    return _bdot(a, b, NN)


@jax.custom_vjp
def _nt(a, b):
    return _bdot(a, b, NT)


@jax.custom_vjp
def _tn(a, b):
    return _bdot(a, b, TN)


_nn.defvjp(lambda a, b: (_nn(a, b), (a, b)), lambda r, g: (_nt(g, r[1]), _tn(r[0], g)))
_nt.defvjp(lambda a, b: (_nt(a, b), (a, b)), lambda r, g: (_nn(g, r[1]), _tn(g, r[0])))
_tn.defvjp(lambda a, b: (_tn(a, b), (a, b)), lambda r, g: (_nt(r[1], g), _nn(r[0], g)))


def _mask_dot(x, mask, dims, x_first):
    acc = None
    for piece in _split3(x):
        term = _bdot(piece, mask, dims) if x_first else _bdot(mask, piece, dims)
        acc = term if acc is None else acc + term
    return acc


@jax.custom_vjp
def _cst_left(cst, x):
    return _mask_dot(x, cst, NN, False)


_cst_left.defvjp(lambda cst, x: (_cst_left(cst, x), cst), lambda cst, g: (jnp.zeros_like(cst), _mask_dot(g, cst, TN, False)))


@jax.custom_vjp
def _cst_right(x, cst):
    return _mask_dot(x, cst, NN, True)


_cst_right.defvjp(lambda x, cst: (_cst_right(x, cst), cst), lambda cst, g: (_mask_dot(g, cst, NT, True), jnp.zeros_like(cst)))


def _lin_left(cst):
    return functools.partial(_cst_left, cst)


def _lin_right(cst):
    return lambda x: _cst_right(x, cst)


@jax.custom_vjp
def _tri_inv_m1(a):
    pm = [-x for x in a]
    ap = list(a)
    for _ in range(5):
        ap = [_bdot(x, x, NN) for x in ap]
        pm = [(p + x) + _bdot(p, x, NN) for p, x in zip(pm, ap)]
    return pm


def _tri_inv_m1_bwd(pm, g):
    t = [gi + _bdot(p, gi, TN) for p, gi in zip(pm, g)]
    return ([-(ti + _bdot(ti, p, NT)) for p, ti in zip(pm, t)],)


_tri_inv_m1.defvjp(lambda a: (lambda pm: (pm, pm))(_tri_inv_m1(a)), _tri_inv_m1_bwd)


@jax.custom_vjp
def _top(x):
    return x[: x.shape[0] // 2]


_top.defvjp(lambda x: (_top(x), None), lambda _, g: (jnp.concatenate([g, jnp.zeros_like(g)], axis=0),))


@jax.custom_vjp
def _bot(x):
    return x[x.shape[0] // 2:]


_bot.defvjp(lambda x: (_bot(x), None), lambda _, g: (jnp.concatenate([jnp.zeros_like(g), g], axis=0),))


@jax.custom_vjp
def _vstack(a, b):
    return jnp.concatenate([a, b], axis=0)


_vstack.defvjp(lambda a, b: (_vstack(a, b), None), lambda _, g: (g[: g.shape[0] // 2], g[g.shape[0] // 2:]))


def _shift_dn_raw(x, s):
    if s == 0:
        return x
    r = pltpu.roll(x, s, axis=0)
    ri = lax.broadcasted_iota(jnp.int32, x.shape, 0)
    return jnp.where(ri >= s, r, 0.0)


def _shift_up_raw(x, s):
    if s == 0:
        return x
    n = x.shape[0]
    r = pltpu.roll(x, n - s, axis=0)
    ri = lax.broadcasted_iota(jnp.int32, x.shape, 0)
    return jnp.where(ri < n - s, r, 0.0)


@functools.partial(jax.custom_vjp, nondiff_argnums=(1,))
def _shift_dn(x, s):
    return _shift_dn_raw(x, s)


_shift_dn.defvjp(lambda x, s: (_shift_dn_raw(x, s), None), lambda s, _, g: (_shift_up_raw(g, s),))


def _conv(x, wrows):
    k_w = len(wrows)
    acc = wrows[k_w - 1] * x
    for k in range(k_w - 1):
        acc = acc + wrows[k] * _shift_dn(x, k_w - 1 - k)
    return acc


def _silu(x):
    return x * jax.nn.sigmoid(x)


def _rms(x, w):
    return x * lax.rsqrt(jnp.mean(x * x, axis=-1, keepdims=True) + EPS) * w


def _l2n(x):
    return x * lax.rsqrt(jnp.sum(x * x, axis=-1, keepdims=True) + EPS)


def _iota2(shape):
    return lax.broadcasted_iota(jnp.int32, shape, 0), lax.broadcasted_iota(jnp.int32, shape, 1)


_GDN_NARGS = 15
_SSD_NARGS = 8


def _gdn_multi(*flat):
    pairs = [flat[i:i + _GDN_NARGS] for i in range(0, len(flat), _GDN_NARGS)]
    idx = range(len(pairs))
    ri, ci = _iota2((PAIR, PAIR))
    blk = ((ri >= CHUNK) & (ci >= CHUNK)) | ((ri < CHUNK) & (ci < CHUNK))
    causal = blk & (ri >= ci)
    strict = blk & (ri > ci)
    q = [_vstack(p[0], p[1]) for p in pairs]
    k = [_vstack(p[2], p[3]) for p in pairs]
    v = [_vstack(p[4], p[5]) for p in pairs]
    gc = [_vstack(p[6], p[7]) for p in pairs]
    beta = [_vstack(p[8], p[9]) for p in pairs]
    glast = [_vstack(jnp.broadcast_to(p[11], (CHUNK, LANE)), jnp.broadcast_to(p[12], (CHUNK, LANE))) for p in pairs]
    sa = [p[13] for p in pairs]
    sb = [p[14] for p in pairs]
    decay = [jnp.exp(jnp.where(causal, gc[i] - jnp.broadcast_to(pairs[i][10], (PAIR, PAIR)), NEG)) for i in idx]
    eg = [jnp.exp(x) for x in gc]
    kbeta = [k[i] * beta[i] for i in idx]
    pm = _tri_inv_m1([jnp.where(strict, _nt(kbeta[i], k[i]) * decay[i], 0.0) for i in idx])
    qk = [_nt(q[i], k[i]) * decay[i] for i in idx]
    rhs_v = [v[i] * beta[i] for i in idx]
    rhs_k = [kbeta[i] * eg[i] for i in idx]
    u = [rhs_v[i] + _nn(pm[i], rhs_v[i]) for i in idx]
    w = [rhs_k[i] + _nn(pm[i], rhs_k[i]) for i in idx]
    q_dec = [q[i] * eg[i] for i in idx]
    k_dec = [k[i] * jnp.exp(glast[i] - gc[i]) for i in idx]
    gl = [jnp.exp(x) for x in glast]
    w_s = [_vstack(_nn(_top(w[i]), sa[i]), _nn(_bot(w[i]), sb[i])) for i in idx]
    q_s = [_vstack(_nn(_top(q_dec[i]), sa[i]), _nn(_bot(q_dec[i]), sb[i])) for i in idx]
    v_new = [u[i] - w_s[i] for i in idx]
    o = [q_s[i] + _nn(qk[i], v_new[i]) for i in idx]
    sa2 = [sa[i] * _vstack(_top(gl[i]), _top(gl[i])) + _tn(_top(k_dec[i]), _top(v_new[i])) for i in idx]
    sb2 = [sb[i] * _vstack(_bot(gl[i]), _bot(gl[i])) + _tn(_bot(k_dec[i]), _bot(v_new[i])) for i in idx]
    out = []
    for i in idx:
        out += [_top(o[i]), _bot(o[i]), sa2[i], sb2[i]]
    return tuple(out)


def _ssd_multi(*flat):
    pairs = [flat[i:i + _SSD_NARGS] for i in range(0, len(flat), _SSD_NARGS)]
    idx = range(len(pairs))
    ri, ci = _iota2((CHUNK, PAIR))
    causal = ri >= jnp.where(ci >= CHUNK, ci - CHUNK, ci)
    xdt = [p[0] * p[1] for p in pairs]
    acs = [p[2] for p in pairs]
    alast = [jnp.broadcast_to(p[4], (CHUNK, PAIR)) for p in pairs]
    lmat = [jnp.exp(jnp.where(causal, acs[i] - jnp.broadcast_to(pairs[i][3], (CHUNK, PAIR)), NEG)) for i in idx]
    cb2 = [_nt(p[6], _vstack(p[5], p[5])) for p in pairs]
    xblk = [_vstack(jnp.where(ci < CHUNK, x, 0.0), jnp.where(ci >= CHUNK, x, 0.0)) for x in xdt]
    y_off = [_nn(pairs[i][6], pairs[i][7]) * jnp.exp(acs[i]) for i in idx]
    y = [_nn(cb2[i] * lmat[i], xblk[i]) + y_off[i] for i in idx]
    el = [jnp.exp(x) for x in alast]
    st2 = [pairs[i][7] * _vstack(el[i], el[i]) + _tn(pairs[i][5], xdt[i] * jnp.exp(alast[i] - acs[i])) for i in idx]
    out = []
    for i in idx:
        out += [y[i], st2[i]]
    return tuple(out)


def _pcall(name, body, grid, in_specs, out_specs, out_shape, scratch=(), sem=None, aliases=None):
    if sem is None:
        sem = ("arbitrary",) * len(grid)
    return pl.pallas_call(
        functools.partial(body),
        out_shape=out_shape,
        grid=grid,
        in_specs=in_specs,
        out_specs=out_specs,
        scratch_shapes=scratch,
        input_output_aliases=aliases or {},
        name=name,
        compiler_params=pltpu.CompilerParams(dimension_semantics=sem, vmem_limit_bytes=VMEM_LIMIT),
    )


def _sds(shape, dtype=F32):
    return jax.ShapeDtypeStruct(shape, dtype)


def _row_spec(tm, width, colblock=0):
    return pl.BlockSpec((tm, width), lambda i, _c=colblock: (i, _c))


def _full_spec(shape):
    nd = len(shape)
    return pl.BlockSpec(shape, lambda *_: (0,) * nd)


def _zero_at_first(refs, first):
    @pl.when(first)
    def _():
        for r in refs:
            r[...] = jnp.zeros(r.shape, r.dtype)


def _pick(n, prefs):
    for p in prefs:
        if n % p == 0:
            return p
    return n


def _matmul(name, a, b, mode, out_dtype, tiles=None, part=None):
    def want(i, dim):
        return [tiles[i]] if tiles is not None and dim % tiles[i] == 0 else []

    if mode == "tn":
        r, m = a.shape
        n = b.shape[1]
        tm = _pick(m, want(0, m) + [1024, 1408])
        tn = _pick(n, want(1, n) + [512, 256, 128])
        tk = _pick(r, want(2, r) + [1024, 512, 256, 128, 64])
        in_specs = [pl.BlockSpec((tk, tm), lambda i, j, k: (k, i)), pl.BlockSpec((tk, tn), lambda i, j, k: (k, j))]
        sem = ("parallel", "parallel", "arbitrary")
        if part is None:
            nc = _pick(tn, (512, 256, 128))

            def body(a_ref, b_ref, o_ref):
                _zero_at_first([o_ref], pl.program_id(2) == 0)
                for c0 in range(0, tn, nc):
                    o_ref[:, c0:c0 + nc] += _bdot(a_ref[...], b_ref[:, c0:c0 + nc], TN)

            return _pcall(name, body, (m // tm, n // tn, r // tk), in_specs, pl.BlockSpec((tm, tn), lambda i, j, k: (i, j)),
                          _sds((m, n), out_dtype), sem=sem)(a, b)
        n_total, col_off, into = part
        width = n_total // N_CHIPS
        per_step = tn // width
        off = col_off // tn
        assert tn % width == 0 and col_off % tn == 0

        def body(a_ref, b_ref, *rest):
            o_ref = rest[-1]
            _zero_at_first([o_ref], pl.program_id(2) == 0)
            for q in range(per_step):
                o_ref[q] += _bdot(a_ref[...], b_ref[:, q * width:(q + 1) * width], TN)

        args = (a, b) if into is None else (a, b, into)
        return _pcall(
            name, body, (m // tm, n // tn, r // tk), in_specs + ([] if into is None else [_ANY]),
            pl.BlockSpec((per_step, tm, width), lambda i, j, k: (j + off, i, 0)), _sds((N_CHIPS, m, width), out_dtype),
            sem=sem, aliases=None if into is None else {2: 0})(*args)
    m, k = a.shape
    n = b.shape[1] if mode == "nn" else b.shape[0]
    tm = _pick(m, want(0, m) + ([1024, 512, 256, 128, 64] if k <= 2816 else [512, 256, 128, 64]))
    tn = _pick(n, want(1, n) + [512, 256, 128])
    dims = NN if mode == "nn" else NT

    nc = _pick(tn, (512, 256, 128))

    def body(a_ref, b_ref, o_ref):
        for c0 in range(0, tn, nc):
            b_blk = b_ref[:, c0:c0 + nc] if mode == "nn" else b_ref[c0:c0 + nc, :]
            o_ref[:, c0:c0 + nc] = _bdot(a_ref[...], b_blk, dims).astype(o_ref.dtype)

    b_spec = pl.BlockSpec((k, tn), lambda i, j: (0, j)) if mode == "nn" else pl.BlockSpec((tn, k), lambda i, j: (j, 0))
    return _pcall(
        name, body, (m // tm, n // tn), [pl.BlockSpec((tm, k), lambda i, j: (i, 0)), b_spec],
        pl.BlockSpec((tm, tn), lambda i, j: (i, j)), _sds((m, n), out_dtype), sem=("parallel", "parallel"))(a, b)


def _matmul_nt_scattering(name, a, b, tm_pref, scatter):
    m, k = a.shape
    n = b.shape[0]
    tm = _pick(m, (tm_pref, 256, 128, 64))
    nc = _pick(n, (512, 256, 128))
    ns = len(scatter)
    steps = m // tm

    def body(a_ref, b_ref, *rest):
        sc_in, o_ref, sc_out, sems = rest[:ns], rest[ns], rest[ns + 1:2 * ns + 1], rest[2 * ns + 1:]
        sc_start, sc_finish = _scatter_ops(sc_in, sc_out, sems)
        pl.when(pl.program_id(0) == 0)(sc_start)
        for c0 in range(0, n, nc):
            o_ref[:, c0:c0 + nc] = _bdot(a_ref[...], b_ref[c0:c0 + nc, :], NT)
        pl.when(pl.program_id(0) == steps - 1)(sc_finish)

    outs = pl.pallas_call(
        body, name=name, grid=(steps,),
        in_specs=[pl.BlockSpec((tm, k), lambda i: (i, 0)), _full_spec(b.shape)] + [_ANY] * ns,
        out_specs=[pl.BlockSpec((tm, n), lambda i: (i, 0))] + [_ANY] * ns,
        out_shape=[_sds((m, n))] + [_sds((3,) + s.shape[1:], s.dtype) for s in scatter],
        scratch_shapes=_scatter_sems(ns),
        compiler_params=pltpu.CompilerParams(dimension_semantics=("arbitrary",), vmem_limit_bytes=VMEM_LIMIT,
                                             has_side_effects=True))(a, b, *scatter)
    return outs[0], outs[1:]


def _matmul_nt_split(name, a1, a2, b, tm_pref):
    m, kh = a1.shape
    n = b.shape[0]
    tm = _pick(m, (tm_pref, 512, 256, 128, 64))
    nc = _pick(n, (512, 256, 128))

    def body(a1_ref, a2_ref, b_ref, o_ref):
        for c0 in range(0, n, nc):
            o_ref[:, c0:c0 + nc] = (_bdot(a1_ref[...], b_ref[c0:c0 + nc, :kh], NT)
                                    + _bdot(a2_ref[...], b_ref[c0:c0 + nc, kh:], NT))

    aspec = pl.BlockSpec((tm, kh), lambda i: (i, 0))
    return _pcall(name, body, (m // tm,), [aspec, aspec, _full_spec(b.shape)], pl.BlockSpec((tm, n), lambda i: (i, 0)),
                  _sds((m, n)), sem=("parallel",))(a1, a2, b)


def _row_tile(t):
    return _pick(t, (512, 256, 128, 64))


def _rms_fwd_gathering(name, x, g, shards, split):
    t = x.shape[0]
    tm = next(c for c in (128, 64, 32, 16) if t % c == 0 and t // c >= 2)
    steps = t // tm
    ns = len(shards)

    def body(x_ref, g_ref, *rest):
        ins, h_ref, outs, sems = rest[:ns], rest[ns], rest[ns + 1:2 * ns + 1], rest[2 * ns + 1:]
        start, forward, finish = _gather_ops(ins, outs, sems, split)
        pl.when(pl.program_id(0) == 0)(start)
        h_ref[...] = _rms(x_ref[...], g_ref[...]).astype(BF16)
        pl.when(pl.program_id(0) == steps - 2)(forward)
        pl.when(pl.program_id(0) == steps - 1)(finish)

    assert steps >= 2
    outs = pl.pallas_call(
        body, name=name, grid=(steps,),
        in_specs=[_row_spec(tm, D_MODEL), _full_spec((1, D_MODEL))] + [_ANY] * ns,
        out_specs=[_row_spec(tm, D_MODEL)] + [_ANY] * ns,
        out_shape=[_sds((t, D_MODEL), BF16)] + [_sds((N_CHIPS,) + s.shape, s.dtype) for s in shards],
        scratch_shapes=_gather_sems(ns),
        compiler_params=pltpu.CompilerParams(dimension_semantics=("arbitrary",), vmem_limit_bytes=VMEM_LIMIT,
                                             has_side_effects=True))(x, g, *shards)
    return outs[0], outs[1:]


_G_LO, _G_HI = GDN_HEADS, 2 * GDN_HEADS
_S_LO, _S_HI = 2 * GDN_HEADS, 2 * GDN_HEADS + SSD_HEADS


def _gates_fn(small, bias, a_log):
    tm = small.shape[0]
    rr, cc = _iota2((tm, tm))
    in_chunk_tril = (((rr >> 6) == (cc >> 6)) & (rr >= cc)).astype(F32)
    lane = lax.broadcasted_iota(jnp.int32, small.shape, 1)
    sp = jax.nn.softplus(small + bias)
    act = jnp.where(lane < _G_LO, jax.nn.sigmoid(small), sp)
    cum = _lin_left(in_chunk_tril)(-jnp.exp(a_log) * sp)
    r, c = _iota2((SMALL, D_MODEL))
    to_ssd_lanes = _lin_right((r == _S_LO + (c >> 6)).astype(F32))
    return act, cum, to_ssd_lanes(act), to_ssd_lanes(cum)


def _expand_lanes(src_ref, dst_ref, lo, heads, width):
    rows = src_ref.shape[0]
    for h in range(heads):
        dst_ref[:, h * width:(h + 1) * width] = jnp.broadcast_to(src_ref[:, lo + h:lo + h + 1], (rows, width))


def _reduce_lanes(wide_ref, lo, heads, width):
    rows = wide_ref.shape[0]
    lane = lax.broadcasted_iota(jnp.int32, (rows, SMALL), 1)
    acc = jnp.zeros((rows, SMALL), F32)
    for h in range(heads):
        col = jnp.sum(wide_ref[:, h * width:(h + 1) * width], axis=-1, keepdims=True)
        acc = jnp.where(lane == lo + h, jnp.broadcast_to(col, (rows, SMALL)), acc)
    return acc


def _gates_fwd(small, bias, a_log):
    t = small.shape[0]
    tm = _row_tile(t)

    def body(s_ref, p0, p1, act_ref, cum_ref, dt_ref, acs_ref):
        act_ref[...], cum_ref[...], dt_ref[...], acs_ref[...] = _gates_fn(s_ref[...], p0[...], p1[...])

    pspec, nspec, wspec = _full_spec((1, SMALL)), _row_spec(tm, SMALL), _row_spec(tm, D_MODEL)
    return _pcall("gates_fwd", body, (t // tm,), [nspec, pspec, pspec], [nspec, nspec, wspec, wspec],
                  [_sds((t, SMALL))] * 2 + [_sds((t, D_MODEL))] * 2, sem=("parallel",))(small, bias, a_log)


def _gates_bwd(small, bias, a_log, d_act, d_cums, d_dt, d_acs):
    t = small.shape[0]
    tm = _row_tile(t)
    nc = len(d_cums)

    def body(*refs):
        s_ref, p0, p1, dact_ref = refs[:4]
        dcum_refs = refs[4:4 + nc]
        ddt_ref, dacs_ref, ds_ref, db_ref, da_ref = refs[4 + nc:]
        _zero_at_first([db_ref, da_ref], pl.program_id(0) == 0)
        _, vjp = jax.vjp(_gates_fn, s_ref[...], p0[...], p1[...])
        d_cum = dcum_refs[0][...]
        for c in dcum_refs[1:]:
            d_cum = d_cum + c[...]
        d_s, d_b, d_a = vjp((dact_ref[...], d_cum, ddt_ref[...], dacs_ref[...]))
        ds_ref[...] = d_s.astype(BF16)
        db_ref[...] += d_b
        da_ref[...] += d_a

    pspec, nspec, wspec = _full_spec((1, SMALL)), _row_spec(tm, SMALL), _row_spec(tm, D_MODEL)
    return _pcall("gates_bwd", body, (t // tm,), [nspec, pspec, pspec] + [nspec] * (1 + nc) + [wspec, wspec],
                  [nspec, pspec, pspec], [_sds((t, SMALL), BF16), _sds((1, SMALL)), _sds((1, SMALL))])(
                      small, bias, a_log, d_act, *d_cums, d_dt, d_acs)


def _gdn_out_fn(o, z, w):
    return _rms(o, w) * _silu(z)


def _gdn_out_fwd(o, proj, gn):
    t = o.shape[0]
    tm = _row_tile(t)

    def body(o_ref, z_ref, w_ref, y_ref):
        for h in range(GDN_HEADS):
            sl = slice(h * GDN_DK, (h + 1) * GDN_DK)
            y_ref[:, sl] = _gdn_out_fn(o_ref[:, sl], z_ref[:, sl].astype(F32), w_ref[...]).astype(BF16)

    return _pcall("gdn_out_fwd", body, (t // tm,), [_row_spec(tm, GDN_V), _row_spec(tm, GDN_V, 3), _full_spec((1, GDN_DK))],
                  _row_spec(tm, GDN_V), _sds((t, GDN_V + SSD_D), BF16), sem=("parallel",))(o, proj, gn)


def _gdn_out_bwd(o, proj, gn, d_ocat):
    t = o.shape[0]
    tm = _row_tile(t)

    def body(o_ref, z_ref, w_ref, dy_ref, do_ref, dz_ref, dw_ref):
        _zero_at_first([dw_ref], pl.program_id(0) == 0)
        for h in range(GDN_HEADS):
            sl = slice(h * GDN_DK, (h + 1) * GDN_DK)
            _, vjp = jax.vjp(_gdn_out_fn, o_ref[:, sl], z_ref[:, sl].astype(F32), w_ref[...])
            d_o, d_z, d_w = vjp(dy_ref[:, sl])
            do_ref[:, sl] = d_o
            dz_ref[:, sl] = d_z.astype(BF16)
            dw_ref[...] += d_w

    return _pcall("gdn_out_bwd", body, (t // tm,),
                  [_row_spec(tm, GDN_V), _row_spec(tm, GDN_V, 3), _full_spec((1, GDN_DK)), _row_spec(tm, GDN_V, 0)],
                  [_row_spec(tm, GDN_V), _row_spec(tm, GDN_V, 3), _full_spec((1, GDN_DK))],
                  [_sds((t, GDN_V)), _sds((t, BIG), BF16), _sds((1, GDN_DK))])(o, proj, gn, d_ocat)


def _ssd_out_fn(y, xs, z, d_skip, w):
    return _rms((y + d_skip * xs) * _silu(z), w)


_SSD_GW = SSD_D // SSD_GROUPS


def _ssd_out_fwd(y, xbc, proj, d_skip, nw, ocat):
    t = y.shape[0]
    tm = _row_tile(t)

    def body(y_ref, x_ref, z_ref, d_ref, w_ref, _, o_ref):
        for gi in range(SSD_GROUPS):
            sl = slice(gi * _SSD_GW, (gi + 1) * _SSD_GW)
            o_ref[:, sl] = _ssd_out_fn(y_ref[:, sl], x_ref[:, sl], z_ref[:, sl].astype(F32), d_ref[:, sl], w_ref[:, sl]).astype(BF16)

    pspec = _full_spec((1, SSD_D))
    return _pcall("ssd_out_fwd", body, (t // tm,),
                  [_row_spec(tm, SSD_D), _row_spec(tm, SSD_D, 0), _row_spec(tm, SSD_D, 4), pspec, pspec, _ANY],
                  _row_spec(tm, SSD_D, 1), _sds(ocat.shape, BF16), sem=("parallel",), aliases={5: 0})(
                      y, xbc, proj, d_skip, nw, ocat)


def _ssd_out_bwd(y, xbc, proj, d_skip, nw, d_ocat, d_proj):
    t = y.shape[0]
    tm = _row_tile(t)

    def body(y_ref, x_ref, z_ref, d_ref, w_ref, do_ref, _, dy_ref, dx_ref, dz_ref, dd_ref, dw_ref):
        _zero_at_first([dd_ref, dw_ref], pl.program_id(0) == 0)
        for gi in range(SSD_GROUPS):
            sl = slice(gi * _SSD_GW, (gi + 1) * _SSD_GW)
            _, vjp = jax.vjp(_ssd_out_fn, y_ref[:, sl], x_ref[:, sl], z_ref[:, sl].astype(F32), d_ref[:, sl], w_ref[:, sl])
            d_y, d_x, d_z, d_d, d_w = vjp(do_ref[:, sl])
            dy_ref[:, sl] = d_y
            dx_ref[:, sl] = d_x
            dz_ref[:, sl] = d_z.astype(BF16)
            dd_ref[:, sl] += d_d
            dw_ref[:, sl] += d_w

    pspec = _full_spec((1, SSD_D))
    row = _row_spec(tm, SSD_D)
    return _pcall("ssd_out_bwd", body, (t // tm,),
                  [row, _row_spec(tm, SSD_D, 0), _row_spec(tm, SSD_D, 4), pspec, pspec, _row_spec(tm, SSD_D, 1), _ANY],
                  [row, row, _row_spec(tm, SSD_D, 4), pspec, pspec],
                  [_sds((t, SSD_D)), _sds((t, SSD_D)), _sds((t, BIG), BF16), _sds((1, SSD_D)), _sds((1, SSD_D))],
                  aliases={6: 2})(y, xbc, proj, d_skip, nw, d_ocat, d_proj)


def _res1_fn(x, mix, g_pm, g_pf):
    x1 = x + _rms(mix, g_pm)
    return x1, _rms(x1, g_pf)


def _res1_fwd(x, mix, g_pm, g_pf):
    t = x.shape[0]
    tm = _row_tile(t)

    def body(x_ref, m_ref, a_ref, b_ref, x1_ref, h2_ref):
        x1, h2 = _res1_fn(x_ref[...], m_ref[...], a_ref[...], b_ref[...])
        x1_ref[...] = x1
        h2_ref[...] = h2.astype(BF16)

    row, pspec = _row_spec(tm, D_MODEL), _full_spec((1, D_MODEL))
    return _pcall("res1_fwd", body, (t // tm,), [row, row, pspec, pspec], [row, row],
                  [_sds((t, D_MODEL)), _sds((t, D_MODEL), BF16)], sem=("parallel",))(x, mix, g_pm, g_pf)


def _res1_bwd(x, mix, g_pm, g_pf, d_x1, d_h2, pair_send):
    t = x.shape[0]
    tm = _row_tile(t)
    ns = len(pair_send)
    steps = t // tm

    def body(x_ref, m_ref, a_ref, b_ref, c1_ref, c2_ref, *rest):
        ps_in, rest = rest[:ns], rest[ns:]
        dx_ref, dm_ref, da_ref, db_ref = rest[:4]
        ps_out, sems = rest[4:4 + ns], rest[4 + ns:]
        ps_start, ps_finish = _pair_send_ops(ps_in, ps_out, sems)
        pl.when(pl.program_id(0) == 0)(ps_start)
        _zero_at_first([da_ref, db_ref], pl.program_id(0) == 0)
        _, vjp = jax.vjp(_res1_fn, x_ref[...], m_ref[...], a_ref[...], b_ref[...])
        d_x, d_m, d_a, d_b = vjp((c1_ref[...], c2_ref[...]))
        dx_ref[...] = d_x
        dm_ref[...] = d_m.astype(BF16)
        da_ref[...] += d_a
        db_ref[...] += d_b
        pl.when(pl.program_id(0) == steps - 1)(ps_finish)

    row, pspec = _row_spec(tm, D_MODEL), _full_spec((1, D_MODEL))
    outs = pl.pallas_call(
        body, name="res1_bwd", grid=(steps,),
        in_specs=[row, row, pspec, pspec, row, row] + [_ANY] * ns, out_specs=[row, row, pspec, pspec] + [_ANY] * ns,
        out_shape=[_sds((t, D_MODEL)), _sds((t, D_MODEL), BF16), _sds((1, D_MODEL)), _sds((1, D_MODEL))]
        + _pair_send_shapes(pair_send),
        scratch_shapes=[pltpu.SemaphoreType.DMA((ns,))] * 2,
        compiler_params=pltpu.CompilerParams(dimension_semantics=("arbitrary",), vmem_limit_bytes=VMEM_LIMIT,
                                             has_side_effects=True))(x, mix, g_pm, g_pf, d_x1, d_h2, *pair_send)
    return outs[:4], outs[4:]


def _final_fn(x1, f, g_po, tgt):
    err = x1 + _rms(f, g_po) - tgt
    return 0.5 * jnp.sum(jnp.mean(err * err, axis=-1))


def _final(x1, f, g_po, tgt):
    t = x1.shape[0]
    tm = _row_tile(t)

    def body(x_ref, f_ref, g_ref, t_ref, loss_ref, dx_ref, df_ref, dg_ref):
        _zero_at_first([loss_ref, dg_ref], pl.program_id(0) == 0)
        loss, (d_x, d_f, d_g) = jax.value_and_grad(_final_fn, argnums=(0, 1, 2))(x_ref[...], f_ref[...], g_ref[...], t_ref[...])
        loss_ref[...] += jnp.broadcast_to(loss, loss_ref.shape)
        dx_ref[...] = d_x
        df_ref[...] = d_f.astype(BF16)
        dg_ref[...] += d_g

    row, pspec = _row_spec(tm, D_MODEL), _full_spec((1, D_MODEL))
    return _pcall("final", body, (t // tm,), [row, row, pspec, row], [_full_spec((8, LANE)), row, row, pspec],
                  [_sds((8, LANE)), _sds((t, D_MODEL)), _sds((t, D_MODEL), BF16), _sds((1, D_MODEL))])(x1, f, g_po, tgt)


def _rms1_bwd(x, g, d_h_a, d_h_b, d_x1):
    t = x.shape[0]
    tm = _row_tile(t)

    def body(x_ref, g_ref, dha_ref, dhb_ref, dx1_ref, dx_ref, dg_ref):
        _zero_at_first([dg_ref], pl.program_id(0) == 0)
        _, vjp = jax.vjp(_rms, x_ref[...], g_ref[...])
        d_x, d_g = vjp(dha_ref[...] + dhb_ref[...])
        dx_ref[...] = d_x + dx1_ref[...]
        dg_ref[...] += d_g

    row, pspec = _row_spec(tm, D_MODEL), _full_spec((1, D_MODEL))
    return _pcall("rms1_bwd", body, (t // tm,), [row, pspec, row, row, row], [row, pspec],
                  [_sds((t, D_MODEL)), _sds((1, D_MODEL))])(x, g, d_h_a, d_h_b, d_x1)


def _qkv_fn(mode):
    def fn(x, *wrows):
        y = _silu(_conv(x, wrows))
        if mode == "q":
            return _l2n(y) * (GDN_DK ** -0.5)
        if mode == "k":
            return _l2n(y)
        return y
    return fn


def _seq_spec(s, tc, off):
    return pl.BlockSpec((s, tc), lambda j, b, _o=off: (b, _o + j))


def _par_spec(rows, tc, off):
    return pl.BlockSpec((rows, tc), lambda j, b, _o=off: (0, _o + j))


def _gdn_conv_fwd(mode, proj, w, bsz, s):
    off = {"q": 0, "k": GDN_HEADS, "v": 2 * GDN_HEADS}[mode]
    fn = _qkv_fn(mode)

    def body(x_ref, w_ref, y_ref):
        y_ref[...] = fn(x_ref[...].astype(F32), *[w_ref[k:k + 1, :] for k in range(CONV_K)])

    return _pcall("gdn_conv_fwd_" + mode, body, (GDN_HEADS, bsz),
                  [_seq_spec(s, GDN_DK, off), _par_spec(CONV_K, GDN_DK, off)], _seq_spec(s, GDN_DK, 0),
                  _sds((bsz * s, GDN_QK)), sem=("parallel", "parallel"))(proj, w)


def _gdn_conv_bwd(mode, proj, w, d_y, d_proj, bsz, s):
    off = {"q": 0, "k": GDN_HEADS, "v": 2 * GDN_HEADS}[mode]
    fn = _qkv_fn(mode)

    def body(x_ref, w_ref, dy_ref, _, dx_ref, dw_ref):
        _zero_at_first([dw_ref], pl.program_id(1) == 0)
        _, vjp = jax.vjp(fn, x_ref[...].astype(F32), *[w_ref[k:k + 1, :] for k in range(CONV_K)])
        grads = vjp(dy_ref[...])
        dx_ref[...] = grads[0].astype(BF16)
        for k in range(CONV_K):
            dw_ref[k:k + 1, :] += grads[1 + k]

    return _pcall("gdn_conv_bwd_" + mode, body, (GDN_HEADS, bsz),
                  [_seq_spec(s, GDN_DK, off), _par_spec(CONV_K, GDN_DK, off), _seq_spec(s, GDN_DK, 0), _ANY],
                  [_seq_spec(s, GDN_DK, off), _par_spec(CONV_K, GDN_DK, 0)],
                  [_sds(d_proj.shape, BF16), _sds((CONV_K, GDN_QK))], sem=("parallel", "arbitrary"), aliases={3: 0})(
                      proj, w, d_y, d_proj)


def _ssd_conv_fn(x, bias, *wrows):
    return _silu(_conv(x, wrows) + bias)


_XBC_OFF = (5 * 1024) // LANE


def _ssd_conv_fwd(proj, w, bias, bsz, s):
    nt_ = SSD_CONV_CH // LANE

    def body(x_ref, w_ref, b_ref, y_ref):
        y_ref[...] = _ssd_conv_fn(x_ref[...].astype(F32), b_ref[...], *[w_ref[k:k + 1, :] for k in range(CONV_K)])

    return _pcall("ssd_conv_fwd", body, (nt_, bsz),
                  [_seq_spec(s, LANE, _XBC_OFF), _par_spec(CONV_K, LANE, 0), _par_spec(1, LANE, 0)], _seq_spec(s, LANE, 0),
                  _sds((bsz * s, SSD_CONV_CH)), sem=("parallel", "parallel"))(proj, w, bias)


def _ssd_conv_bwd(proj, w, bias, d_y, d_proj, bsz, s):
    nt_ = SSD_CONV_CH // LANE

    def body(x_ref, w_ref, b_ref, dy_ref, _, dx_ref, dw_ref, db_ref):
        _zero_at_first([dw_ref, db_ref], pl.program_id(1) == 0)
        _, vjp = jax.vjp(_ssd_conv_fn, x_ref[...].astype(F32), b_ref[...], *[w_ref[k:k + 1, :] for k in range(CONV_K)])
        grads = vjp(dy_ref[...])
        dx_ref[...] = grads[0].astype(BF16)
        db_ref[...] += grads[1]
        for k in range(CONV_K):
            dw_ref[k:k + 1, :] += grads[2 + k]

    return _pcall("ssd_conv_bwd", body, (nt_, bsz),
                  [_seq_spec(s, LANE, _XBC_OFF), _par_spec(CONV_K, LANE, 0), _par_spec(1, LANE, 0), _seq_spec(s, LANE, 0), _ANY],
                  [_seq_spec(s, LANE, _XBC_OFF), _par_spec(CONV_K, LANE, 0), _par_spec(1, LANE, 0)],
                  [_sds(d_proj.shape, BF16), _sds((CONV_K, SSD_CONV_CH)), _sds((1, SSD_CONV_CH))],
                  sem=("parallel", "arbitrary"), aliases={4: 0})(proj, w, bias, d_y, d_proj)


_FFN_TC = 256
_FFN_NT = D_FF // _FFN_TC


def _ffn_act_fn(xg, xu, bg, bu, *wrows):
    k_w = FFN_CONV_K
    gate = _conv(xg, wrows[:k_w]) + bg
    up = _conv(xu, wrows[k_w:]) + bu
    return _silu(gate) * up


def _ffn_act_fwd(u_pre, w, bias, bsz, s):
    def body(xg_ref, xu_ref, wg_ref, wu_ref, bg_ref, bu_ref, a_ref):
        rows = [wg_ref[k:k + 1, :] for k in range(FFN_CONV_K)] + [wu_ref[k:k + 1, :] for k in range(FFN_CONV_K)]
        a_ref[...] = _ffn_act_fn(xg_ref[...].astype(F32), xu_ref[...].astype(F32), bg_ref[...], bu_ref[...], *rows).astype(BF16)

    return _pcall("ffn_act_fwd", body, (_FFN_NT, bsz),
                  [_seq_spec(s, _FFN_TC, 0), _seq_spec(s, _FFN_TC, _FFN_NT),
                   _par_spec(FFN_CONV_K, _FFN_TC, 0), _par_spec(FFN_CONV_K, _FFN_TC, _FFN_NT),
                   _par_spec(1, _FFN_TC, 0), _par_spec(1, _FFN_TC, _FFN_NT)],
                  _seq_spec(s, _FFN_TC, 0), _sds((bsz * s, D_FF), BF16), sem=("parallel", "parallel"))(
                      u_pre, u_pre, w, w, bias, bias)


def _ffn_act_bwd(u_pre, w, bias, d_a, bsz, s):
    def body(xg_ref, xu_ref, wg_ref, wu_ref, bg_ref, bu_ref, da_ref, dg_ref, du_ref, dwg_ref, dwu_ref, dbg_ref, dbu_ref):
        _zero_at_first([dwg_ref, dwu_ref, dbg_ref, dbu_ref], pl.program_id(1) == 0)
        rows = [wg_ref[k:k + 1, :] for k in range(FFN_CONV_K)] + [wu_ref[k:k + 1, :] for k in range(FFN_CONV_K)]
        _, vjp = jax.vjp(_ffn_act_fn, xg_ref[...].astype(F32), xu_ref[...].astype(F32), bg_ref[...], bu_ref[...], *rows)
        grads = vjp(da_ref[...])
        dg_ref[...] = grads[0].astype(BF16)
        du_ref[...] = grads[1].astype(BF16)
        dbg_ref[...] += grads[2]
        dbu_ref[...] += grads[3]
        for k in range(FFN_CONV_K):
            dwg_ref[k:k + 1, :] += grads[4 + k]
            dwu_ref[k:k + 1, :] += grads[4 + FFN_CONV_K + k]

    seq0, par3, par1 = _seq_spec(s, _FFN_TC, 0), _par_spec(FFN_CONV_K, _FFN_TC, 0), _par_spec(1, _FFN_TC, 0)
    return _pcall("ffn_act_bwd", body, (_FFN_NT, bsz),
                  [seq0, _seq_spec(s, _FFN_TC, _FFN_NT), par3, _par_spec(FFN_CONV_K, _FFN_TC, _FFN_NT),
                   par1, _par_spec(1, _FFN_TC, _FFN_NT), seq0],
                  [seq0, seq0, par3, par3, par1, par1],
                  [_sds((bsz * s, D_FF), BF16), _sds((bsz * s, D_FF), BF16), _sds((FFN_CONV_K, D_FF)), _sds((FFN_CONV_K, D_FF)),
                   _sds((1, D_FF)), _sds((1, D_FF))], sem=("parallel", "arbitrary"))(u_pre, u_pre, w, w, bias, bias, d_a)


_GP = GDN_HEADS // 2
_SP = SSD_HEADS // 2


def _pair_lanes(p):
    return slice(2 * p * LANE, (2 * p + 1) * LANE), slice((2 * p + 1) * LANE, (2 * p + 2) * LANE)


_LAST = slice(CHUNK - 1, CHUNK)


def _gdn_args(p, q_ref, k_ref, v_ref, g_ref, b_ref, gr_ref):
    la, lb = _pair_lanes(p)
    return (q_ref[:, la], q_ref[:, lb], k_ref[:, la], k_ref[:, lb], v_ref[:, la], v_ref[:, lb], g_ref[:, la], g_ref[:, lb],
            b_ref[:, la], b_ref[:, lb], gr_ref[p], g_ref[_LAST, la], g_ref[_LAST, lb])


def _gdn_fwd(q, k, v, act, cum, gc_row, bsz, n):
    def body(q_ref, k_ref, v_ref, act_ref, cum_ref, gr_ref, o_ref, sin_ref, s_scr, g_ref, b_ref):
        _zero_at_first([s_scr], pl.program_id(1) == 0)
        _expand_lanes(act_ref, b_ref, 0, GDN_HEADS, GDN_DK)
        _expand_lanes(cum_ref, g_ref, _G_LO, GDN_HEADS, GDN_DK)
        flat = []
        for p in range(_GP):
            flat += [*_gdn_args(p, q_ref, k_ref, v_ref, g_ref, b_ref, gr_ref), s_scr[2 * p], s_scr[2 * p + 1]]
        sin_ref[...] = s_scr[...]
        outs = _gdn_multi(*flat)
        for p in range(_GP):
            la, lb = _pair_lanes(p)
            o_ref[:, la], o_ref[:, lb], s_scr[2 * p], s_scr[2 * p + 1] = outs[4 * p:4 * p + 4]

    tspec = pl.BlockSpec((CHUNK, GDN_V), lambda b, c: (b * n + c, 0))
    rspec = pl.BlockSpec((_GP, 1, LANE), lambda b, c: (b * n + c, 0, 0))
    sspec = pl.BlockSpec((GDN_HEADS, LANE, LANE), lambda b, c: (b * n + c, 0, 0))
    nspec = pl.BlockSpec((CHUNK, SMALL), lambda b, c: (b * n + c, 0))
    wide = pltpu.VMEM((CHUNK, GDN_V), F32)
    return _pcall("gdn_fwd", body, (bsz, n), [tspec] * 3 + [nspec, nspec, rspec], [tspec, sspec],
                  [_sds((bsz * n * CHUNK, GDN_V)), _sds((bsz * n * GDN_HEADS, LANE, LANE))],
                  scratch=[pltpu.VMEM((GDN_HEADS, LANE, LANE), F32), wide, wide], sem=("parallel", "arbitrary"))(
                      q, k, v, act, cum, gc_row)


def _gdn_bwd(q, k, v, act, cum, gc_row, s_in, d_o, bsz, n, scatter):
    ns = len(scatter)

    def body(q_ref, k_ref, v_ref, act_ref, cum_ref, gr_ref, sin_ref, do_ref, *rest):
        sc_in, rest = rest[:ns], rest[ns:]
        dq_ref, dk_ref, dv_ref, dact_ref, dcum_ref, dgr_ref = rest[:6]
        sc_out, rest = rest[6:6 + ns], rest[6 + ns:]
        ds_scr, g_ref, b_ref, dg_ref, db_ref, send_sems, recv_sems = rest
        step = pl.program_id(0) * n + pl.program_id(1)
        sc_start, sc_finish = _scatter_ops(sc_in, sc_out, (send_sems, recv_sems))
        pl.when(step == 0)(sc_start)
        _zero_at_first([ds_scr], pl.program_id(1) == 0)
        _expand_lanes(act_ref, b_ref, 0, GDN_HEADS, GDN_DK)
        _expand_lanes(cum_ref, g_ref, _G_LO, GDN_HEADS, GDN_DK)
        flat, cots = [], []
        for p in range(_GP):
            la, lb = _pair_lanes(p)
            flat += [*_gdn_args(p, q_ref, k_ref, v_ref, g_ref, b_ref, gr_ref), sin_ref[2 * p], sin_ref[2 * p + 1]]
            cots += [do_ref[:, la], do_ref[:, lb], ds_scr[2 * p], ds_scr[2 * p + 1]]
        _, vjp = jax.vjp(_gdn_multi, *flat)
        grads = vjp(tuple(cots))
        for p in range(_GP):
            la, lb = _pair_lanes(p)
            cts = grads[_GDN_NARGS * p:_GDN_NARGS * (p + 1)]
            for ref, i in ((dq_ref, 0), (dk_ref, 2), (dv_ref, 4), (dg_ref, 6), (db_ref, 8)):
                ref[:, la] = cts[i]
                ref[:, lb] = cts[i + 1]
            dgr_ref[p] = cts[10]
            dg_ref[_LAST, la] += cts[11]
            dg_ref[_LAST, lb] += cts[12]
            ds_scr[2 * p] = cts[13]
            ds_scr[2 * p + 1] = cts[14]
        dact_ref[...] = _reduce_lanes(db_ref, 0, GDN_HEADS, GDN_DK)
        dcum_ref[...] = _reduce_lanes(dg_ref, _G_LO, GDN_HEADS, GDN_DK)
        pl.when(step == bsz * n - 1)(sc_finish)

    tspec = pl.BlockSpec((CHUNK, GDN_V), lambda b, c: (b * n + (n - 1 - c), 0))
    nspec = pl.BlockSpec((CHUNK, SMALL), lambda b, c: (b * n + (n - 1 - c), 0))
    rspec = pl.BlockSpec((_GP, 1, LANE), lambda b, c: (b * n + (n - 1 - c), 0, 0))
    sspec = pl.BlockSpec((GDN_HEADS, LANE, LANE), lambda b, c: (b * n + (n - 1 - c), 0, 0))
    tok_shape, nar_shape = _sds((bsz * n * CHUNK, GDN_V)), _sds((bsz * n * CHUNK, SMALL))
    wide = pltpu.VMEM((CHUNK, GDN_V), F32)
    outs = pl.pallas_call(
        body, name="gdn_bwd", grid=(bsz, n),
        in_specs=[tspec] * 3 + [nspec, nspec, rspec, sspec, tspec] + [_ANY] * ns,
        out_specs=[tspec] * 3 + [nspec, nspec, rspec] + [_ANY] * ns,
        out_shape=[tok_shape] * 3 + [nar_shape, nar_shape, _sds((bsz * n * _GP, 1, LANE))]
        + [_sds((3,) + a.shape[1:], a.dtype) for a in scatter],
        scratch_shapes=[pltpu.VMEM((GDN_HEADS, LANE, LANE), F32), wide, wide, wide, wide] + _scatter_sems(ns),
        compiler_params=pltpu.CompilerParams(dimension_semantics=("arbitrary", "arbitrary"), vmem_limit_bytes=VMEM_LIMIT,
                                             has_side_effects=True))(q, k, v, act, cum, gc_row, s_in, d_o, *scatter)
    return outs[:6], outs[6:]


_B_OFF = SSD_D // LANE
_C_OFF = (SSD_D + SSD_BC) // LANE
_PPG = _SP // SSD_GROUPS


def _ssd_args(p, x_ref, dt_ref, a_ref, ar_ref):
    lp = slice(p * LANE, (p + 1) * LANE)
    gi = p // _PPG
    b_sl = slice((_B_OFF + gi) * LANE, (_B_OFF + gi + 1) * LANE)
    c_sl = slice((_C_OFF + gi) * LANE, (_C_OFF + gi + 1) * LANE)
    return (x_ref[:, lp], dt_ref[:, lp], a_ref[:, lp], ar_ref[p], a_ref[_LAST, lp], x_ref[:, b_sl], x_ref[:, c_sl])


def _ssd_fwd(xbc, dt, acs, acs_row, bsz, n):
    def body(x_ref, dt_ref, a_ref, ar_ref, y_ref, sin_ref, s_scr):
        _zero_at_first([s_scr], pl.program_id(1) == 0)
        flat = []
        for p in range(_SP):
            flat += [*_ssd_args(p, x_ref, dt_ref, a_ref, ar_ref), s_scr[p]]
        sin_ref[...] = s_scr[...]
        outs = _ssd_multi(*flat)
        for p in range(_SP):
            y_ref[:, p * LANE:(p + 1) * LANE], s_scr[p] = outs[2 * p:2 * p + 2]

    tspec = pl.BlockSpec((CHUNK, SSD_D), lambda b, c: (b * n + c, 0))
    return _pcall("ssd_fwd", body, (bsz, n),
                  [pl.BlockSpec((CHUNK, SSD_CONV_CH), lambda b, c: (b * n + c, 0)), tspec, tspec,
                   pl.BlockSpec((_SP, 1, LANE), lambda b, c: (b * n + c, 0, 0))],
                  [tspec, pl.BlockSpec((_SP, LANE, LANE), lambda b, c: (b * n + c, 0, 0))],
                  [_sds((bsz * n * CHUNK, SSD_D)), _sds((bsz * n * _SP, LANE, LANE))],
                  scratch=[pltpu.VMEM((_SP, LANE, LANE), F32)], sem=("parallel", "arbitrary"))(xbc, dt, acs, acs_row)


def _ssd_bwd(xbc, dt, acs, acs_row, s_in, d_y, d_x_skip, bsz, n):
    def body(x_ref, dt_ref, a_ref, ar_ref, sin_ref, dy_ref, dsk_ref, dx_ref, ddt_ref, da_ref, dar_ref, ds_scr):
        _zero_at_first([ds_scr], pl.program_id(1) == 0)
        d_b = [None] * SSD_GROUPS
        d_c = [None] * SSD_GROUPS
        flat, cots = [], []
        for p in range(_SP):
            flat += [*_ssd_args(p, x_ref, dt_ref, a_ref, ar_ref), sin_ref[p]]
            cots += [dy_ref[:, p * LANE:(p + 1) * LANE], ds_scr[p]]
        _, vjp = jax.vjp(_ssd_multi, *flat)
        grads = vjp(tuple(cots))
        for p in range(_SP):
            lp = slice(p * LANE, (p + 1) * LANE)
            gi = p // _PPG
            cts = grads[_SSD_NARGS * p:_SSD_NARGS * (p + 1)]
            dx_ref[:, lp] = cts[0] + dsk_ref[:, lp]
            ddt_ref[:, lp] = cts[1]
            da_ref[:, lp] = cts[2]
            dar_ref[p] = cts[3]
            da_ref[_LAST, lp] += cts[4]
            d_b[gi] = cts[5] if d_b[gi] is None else d_b[gi] + cts[5]
            d_c[gi] = cts[6] if d_c[gi] is None else d_c[gi] + cts[6]
            ds_scr[p] = cts[7]
        for gi in range(SSD_GROUPS):
            dx_ref[:, (_B_OFF + gi) * LANE:(_B_OFF + gi + 1) * LANE] = d_b[gi]
            dx_ref[:, (_C_OFF + gi) * LANE:(_C_OFF + gi + 1) * LANE] = d_c[gi]

    def rev(b, c):
        return b * n + (n - 1 - c)

    tspec = pl.BlockSpec((CHUNK, SSD_D), lambda b, c: (rev(b, c), 0))
    xspec = pl.BlockSpec((CHUNK, SSD_CONV_CH), lambda b, c: (rev(b, c), 0))
    rspec = pl.BlockSpec((_SP, 1, LANE), lambda b, c: (rev(b, c), 0, 0))
    tok_shape = _sds((bsz * n * CHUNK, SSD_D))
    return _pcall("ssd_bwd", body, (bsz, n),
                  [xspec, tspec, tspec, rspec, pl.BlockSpec((_SP, LANE, LANE), lambda b, c: (rev(b, c), 0, 0)), tspec, tspec],
                  [xspec, tspec, tspec, rspec],
                  [_sds((bsz * n * CHUNK, SSD_CONV_CH)), tok_shape, tok_shape, _sds((bsz * n * _SP, 1, LANE))],
                  scratch=[pltpu.VMEM((_SP, LANE, LANE), F32)], sem=("parallel", "arbitrary"))(
                      xbc, dt, acs, acs_row, s_in, d_y, d_x_skip)


def _rep(p, width):
    return jnp.repeat(p.reshape(-1), width).reshape(1, -1)


def _to_rows(narrow, lo, hi, bsz, n):
    heads = hi - lo
    a = narrow[:, lo:hi].reshape(bsz, n, CHUNK, heads)
    return jnp.transpose(a, (0, 1, 3, 2)).reshape(bsz * n * (heads // 2), 1, 2 * CHUNK)


def _from_rows(rows, heads, bsz, n):
    return jnp.transpose(rows.reshape(bsz, n, heads, CHUNK), (0, 1, 3, 2)).reshape(bsz * n * CHUNK, heads)


def _narrow_row(gdn_part, ssd_part):
    return jnp.pad(jnp.concatenate([gdn_part, ssd_part], axis=1), ((0, 0), (_G_LO, SMALL - _S_HI)))


def _local_step(x, tgt, p):
    bsz, s, _ = x.shape
    t = bsz * s
    n = s // CHUNK
    x2 = x.reshape(t, D_MODEL)
    tgt2 = tgt.reshape(t, D_MODEL)
    gate_bias = _narrow_row(p["gdn_dt_bias"], p["ssd_dt_bias"])
    gate_a_log = _narrow_row(p["gdn_a_log"], p["ssd_a_log"])
    d_skip = _rep(p["ssd_d"], SSD_HEADDIM)

    h, (g_in, g_gcw, g_scw, g_fcw) = _rms_fwd_gathering(
        "rms0_fwd", x2, p["pre_mix_norm"], [p["w_in"]] + [p[k] for k in _COL_SHARDED_SMALL], [True, False, False, False])
    p = dict(p, gdn_conv_w=_chips_to_cols(g_gcw), ssd_conv_w=_chips_to_cols(g_scw), ffn_conv_w=_chips_to_cols(g_fcw))
    a0, a1, b0 = _NARROW_A - 2 * _SHARD_W, _NARROW_A + 16 - 2 * _SHARD_W, _NARROW_B - 3 * _SHARD_W
    w_big = jnp.concatenate([g_in[0], g_in[1], g_in[2][:, :a0], g_in[2][:, a1:], g_in[3][:, :b0]], axis=1)
    w_small = jnp.concatenate([g_in[2][:, a0:a1], g_in[3][:, b0:], jnp.zeros((D_MODEL, SMALL - 32), BF16)], axis=1)
    proj, g_out, g_up, g_down = _matmul_nn_gathering(
        "mm_in_big", h, w_big, BF16, (_pick(t, (1024, 512, 256, 128, 64)), BIG // 2), [p["w_out"], p["w_up"], p["w_down"]])
    w_out, w_up, w_down = g_out.reshape(-1, D_MODEL), _chips_to_cols(g_up), g_down.reshape(-1, D_MODEL)
    small = _matmul("mm_in_small", h, w_small, "nn", F32, (1024, 128))
    gact, cum, dt, acs = _gates_fwd(small, gate_bias, gate_a_log)
    gc_row = _to_rows(cum, _G_LO, _G_HI, bsz, n)
    acs_row = _to_rows(cum, _S_LO, _S_HI, bsz, n)
    q = _gdn_conv_fwd("q", proj, p["gdn_conv_w"], bsz, s)
    k = _gdn_conv_fwd("k", proj, p["gdn_conv_w"], bsz, s)
    v = _gdn_conv_fwd("v", proj, p["gdn_conv_w"], bsz, s)
    o, gdn_s = _gdn_fwd(q, k, v, gact, cum, gc_row, bsz, n)
    ocat = _gdn_out_fwd(o, proj, p["gdn_norm_w"])
    xbc = _ssd_conv_fwd(proj, p["ssd_conv_w"], p["ssd_conv_b"], bsz, s)
    y, ssd_s = _ssd_fwd(xbc, dt, acs, acs_row, bsz, n)
    ocat = _ssd_out_fwd(y, xbc, proj, d_skip, p["ssd_norm_w"], ocat)
    mix = _matmul("mm_out", ocat, w_out, "nn", F32, (1024, 1024))
    x1, h2 = _res1_fwd(x2, mix, p["post_mix_norm"], p["pre_ffn_norm"])
    u_pre = _matmul("mm_up", h2, w_up, "nn", BF16, (1024, 2816))
    act = _ffn_act_fwd(u_pre, p["ffn_conv_w"], p["ffn_conv_b"], bsz, s)
    f = _matmul("mm_down", act, w_down, "nn", F32, (1024, 1024))
    loss_acc, d_out, d_f, g_post_ffn = _final(x1, f, p["post_ffn_norm"], tgt2)

    grads = {"post_ffn_norm": g_post_ffn}
    d_act = _matmul("mm_down_dx", d_f, w_down, "nt", F32, (1024, 2816))
    dw_down = _matmul("mm_down_dw", act, d_f, "tn", F32, (2816, 1024, 1024))
    d_gate, d_up, dwg, dwu, dbg, dbu = _ffn_act_bwd(u_pre, p["ffn_conv_w"], p["ffn_conv_b"], d_act, bsz, s)
    grads["ffn_conv_w"] = jnp.concatenate([dwg, dwu], axis=1)
    grads["ffn_conv_b"] = jnp.concatenate([dbg, dbu], axis=1)
    d_h2 = _matmul_nt_split("mm_up_dx", d_gate, d_up, w_up, 512)
    dw_up = _matmul("mm_up_dw_gate", h2, d_gate, "tn", F32, (1024, 2816, 1024), part=(2 * D_FF, 0, None))
    dw_up = _matmul("mm_up_dw_up", h2, d_up, "tn", F32, (1024, 2816, 1024), part=(2 * D_FF, D_FF, dw_up))
    dw_down = dw_down.reshape(N_CHIPS, -1, D_MODEL)
    (d_x1, d_mix, grads["post_mix_norm"], grads["pre_ffn_norm"]), (r_up, r_down) = _res1_bwd(
        x2, mix, p["post_mix_norm"], p["pre_ffn_norm"], d_out, d_h2, [dw_up, dw_down])
    d_ocat = _matmul("mm_out_dx", d_mix, w_out, "nt", F32, (1024, 2048))
    dw_out = _matmul("mm_out_dw", ocat, d_mix, "tn", F32, (2048, 1024, 1024)).reshape(N_CHIPS, -1, D_MODEL)

    (r_out,) = _pair_send_other_half("pair_reduce_send_early", [dw_out])
    early = _pair_adds("early", [dw_out, dw_up, dw_down], [r_out, r_up, r_down])

    d_o, d_proj, grads["gdn_norm_w"] = _gdn_out_bwd(o, proj, p["gdn_norm_w"], d_ocat)
    (d_q, d_k, d_v, d_act_g, d_cum_g, d_gc_row), early_landed = _gdn_bwd(
        q, k, v, gact, cum, gc_row, gdn_s, d_o, bsz, n, [ps[1] for ps in early])
    d_proj, dwq = _gdn_conv_bwd("q", proj, p["gdn_conv_w"], d_q, d_proj, bsz, s)
    d_proj, dwk = _gdn_conv_bwd("k", proj, p["gdn_conv_w"], d_k, d_proj, bsz, s)
    d_proj, dwv = _gdn_conv_bwd("v", proj, p["gdn_conv_w"], d_v, d_proj, bsz, s)
    grads["gdn_conv_w"] = jnp.concatenate([dwq, dwk, dwv], axis=1)

    d_y, d_xs_skip, d_proj, d_dskip, grads["ssd_norm_w"] = _ssd_out_bwd(y, xbc, proj, d_skip, p["ssd_norm_w"], d_ocat, d_proj)
    d_xbc, d_dt, d_acs, d_acs_row = _ssd_bwd(xbc, dt, acs, acs_row, ssd_s, d_y, d_xs_skip, bsz, n)
    d_proj, grads["ssd_conv_w"], grads["ssd_conv_b"] = _ssd_conv_bwd(proj, p["ssd_conv_w"], p["ssd_conv_b"], d_xbc, d_proj, bsz, s)

    d_cum_rows = jnp.concatenate([jnp.zeros((t, _G_LO), F32), _from_rows(d_gc_row, GDN_HEADS, bsz, n),
                                  _from_rows(d_acs_row, SSD_HEADS, bsz, n), jnp.zeros((t, SMALL - _S_HI), F32)], axis=1)
    d_small, d_gate_bias, d_gate_a_log = _gates_bwd(small, gate_bias, gate_a_log, d_act_g, [d_cum_g, d_cum_rows], d_dt, d_acs)
    grads["gdn_dt_bias"], grads["ssd_dt_bias"] = d_gate_bias[:, _G_LO:_G_HI], d_gate_bias[:, _S_LO:_S_HI]
    grads["gdn_a_log"], grads["ssd_a_log"] = d_gate_a_log[:, _G_LO:_G_HI], d_gate_a_log[:, _S_LO:_S_HI]
    dw_big = _matmul("mm_in_big_dw", h, d_proj, "tn", F32, (1024, 3328, 1024))
    dw_small = _matmul("mm_in_small_dw", h, d_small, "tn", F32)
    dw_in = jnp.stack([
        dw_big[:, :_SHARD_W], dw_big[:, _SHARD_W:2 * _SHARD_W],
        jnp.concatenate([dw_big[:, 2 * _SHARD_W:_NARROW_A], dw_small[:, :16], dw_big[:, _NARROW_A:3 * _SHARD_W - 16]], axis=1),
        jnp.concatenate([dw_big[:, 3 * _SHARD_W - 16:], dw_small[:, 16:32]], axis=1)])
    late = _pair_sums("late", [dw_in])
    d_h_big, late_landed = _matmul_nt_scattering("mm_in_big_dx", d_proj, w_big, 512, [ps[1] for ps in late])
    d_h_small = _matmul("mm_in_small_dx", d_small, w_small, "nt", F32, (1024, 1024))
    grad_x, grads["pre_mix_norm"] = _rms1_bwd(x2, p["pre_mix_norm"], d_h_big, d_h_small, d_x1)
    grads["ssd_d"] = _head_sums(d_dskip)[:1, :SSD_HEADS]
    return loss_acc, grad_x.reshape(bsz, s, D_MODEL), grads, (late + early, list(late_landed) + list(early_landed))


def _head_sums(wide):
    def body(x_ref, o_ref):
        r, c = _iota2((D_MODEL, SMALL))
        o_ref[...] = _mask_dot(jnp.broadcast_to(x_ref[...], (8, D_MODEL)), ((r >> 6) == c).astype(F32), NN, True)

    return _pcall("head_sums", body, (1,), [_full_spec((1, D_MODEL))], _full_spec((8, SMALL)), _sds((8, SMALL)))(wide)


def _adamw_fn(w, g, m, v):
    m = ADAM_B1 * m + (1.0 - ADAM_B1) * g
    v = ADAM_B2 * v + (1.0 - ADAM_B2) * (g * g)
    m_hat = m / (1.0 - ADAM_B1 ** ADAM_STEP)
    v_hat = v / (1.0 - ADAM_B2 ** ADAM_STEP)
    delta = -ADAM_LR * (m_hat / (jnp.sqrt(v_hat) + ADAM_EPS) + ADAM_WD * w)
    return delta, m, v


def _adamw(name, w, g, m, v):
    r, c = w.shape
    tr = _pick(r, (256, 176, 128, 64, 8))

    def body(w_ref, g_ref, m_ref, v_ref, d_ref, m2_ref, v2_ref):
        d, m2, v2 = _adamw_fn(w_ref[...], g_ref[...], m_ref[...], v_ref[...])
        d_ref[...] = d
        m2_ref[...] = m2
        v2_ref[...] = v2

    spec = pl.BlockSpec((tr, c), lambda i: (i, 0))
    return _pcall(name, body, (r // tr,), [spec] * 4, [spec] * 3, [_sds((r, c))] * 3, sem=("parallel",))(w, g, m, v)


def _adamw_many(groups):
    n = len(groups[0])
    shapes = [a.shape for a in groups[0]]

    def body(*refs):
        ins, outs = refs[:4 * n], refs[4 * n:]
        for i in range(n):
            d, m2, v2 = _adamw_fn(ins[i][...], ins[n + i][...], ins[2 * n + i][...], ins[3 * n + i][...])
            outs[i][...] = d
            outs[n + i][...] = m2
            outs[2 * n + i][...] = v2

    specs = [_full_spec(s) for s in shapes]
    flat = _pcall("adamw_small", body, (1,), specs * 4, specs * 3, [_sds(s) for s in shapes] * 3)(
        *[a for g in groups for a in g])
    return [flat[:n], flat[n:2 * n], flat[2 * n:]]


_ANY = pl.BlockSpec(memory_space=pl.ANY)
_OTHER_CHIPS = ((1, 0), (0, 1), (1, 1))


def _coords():
    return lax.axis_index("x"), lax.axis_index("y"), lax.axis_index("c")


def _flip(v, f):
    return 1 - v if f else v


def _gather_ops(ins, outs, sems, split):
    send_sems, recv_sems, fwd_send_sems, fwd_recv_sems, own_send_sems, own_recv_sems = sems
    n = len(ins)
    x, y, c = _coords()
    me = 2 * x + y
    sib = (x, y, 1 - c)

    def rows(a, core):
        if not split[a]:
            return slice(None)
        half = ins[a].shape[0] // 2
        return pl.ds(core * half, half)

    def chip(j):
        fx, fy = _OTHER_CHIPS[j]
        return _flip(x, fx), _flip(y, fy)

    def own_cp(a):
        return pltpu.make_async_remote_copy(ins[a], outs[a].at[me], own_send_sems.at[a], own_recv_sems.at[a],
                                            device_id=sib, device_id_type=MESH)

    def ici_cp(a, j):
        return pltpu.make_async_remote_copy(ins[a].at[rows(a, c)], outs[a].at[me, rows(a, c)],
                                            send_sems.at[a * 3 + j], recv_sems.at[a * 3 + j],
                                            device_id=(*chip(j), c), device_id_type=MESH)

    def landed_cp(a, j, sem_a, sem_b, core, to):
        cx, cy = chip(j)
        blk = outs[a].at[2 * cx + cy, rows(a, core)]
        return pltpu.make_async_remote_copy(blk, blk, sem_a.at[a * 3 + j], sem_b.at[a * 3 + j], device_id=to, device_id_type=MESH)

    pairs = [(a, j) for a in range(n) for j in range(3)]

    def start():
        for a in range(n):
            own_cp(a).start()
        for a, j in pairs:
            ici_cp(a, j).start()

    def forward():
        for a, j in pairs:
            landed_cp(a, j, send_sems, recv_sems, c, (*chip(j), c)).wait_recv()
            if split[a]:
                landed_cp(a, j, fwd_send_sems, fwd_recv_sems, c, sib).start()

    def finish():
        for a, j in pairs:
            if split[a]:
                landed_cp(a, j, fwd_send_sems, fwd_recv_sems, 1 - c, sib).wait_recv()
        for a in range(n):
            own_cp(a).wait_recv()
        for a, j in pairs:
            ici_cp(a, j).wait_send()
            if split[a]:
                landed_cp(a, j, fwd_send_sems, fwd_recv_sems, c, sib).wait_send()
        for a in range(n):
            own_cp(a).wait_send()

    return start, forward, finish


def _gather_sems(n):
    return [pltpu.SemaphoreType.DMA((3 * n,))] * 4 + [pltpu.SemaphoreType.DMA((n,))] * 2


def _matmul_nn_gathering(name, a, b, out_dtype, tiles, shards):
    m, k = a.shape
    n = b.shape[1]
    tm, tn = tiles
    nc = _pick(tn, (512, 256, 128))
    ns = len(shards)
    gi, gj = m // tm, n // tn
    steps = gi * gj

    def body(a_ref, b_ref, *rest):
        ins, o_ref, outs, sems = rest[:ns], rest[ns], rest[ns + 1:2 * ns + 1], rest[2 * ns + 1:]
        step = pl.program_id(0) * gj + pl.program_id(1)
        start, forward, finish = _gather_ops(ins, outs, sems, [True] * ns)
        pl.when(step == 0)(start)
        for c0 in range(0, tn, nc):
            o_ref[:, c0:c0 + nc] = _bdot(a_ref[...], b_ref[:, c0:c0 + nc], NN).astype(o_ref.dtype)
        pl.when(step == steps - 2)(forward)
        pl.when(step == steps - 1)(finish)

    assert steps >= 2
    return pl.pallas_call(
        body, name=name, grid=(gi, gj),
        in_specs=[pl.BlockSpec((tm, k), lambda i, j: (i, 0)), pl.BlockSpec((k, tn), lambda i, j: (0, j))] + [_ANY] * ns,
        out_specs=[pl.BlockSpec((tm, tn), lambda i, j: (i, j))] + [_ANY] * ns,
        out_shape=[_sds((m, n), out_dtype)] + [_sds((N_CHIPS,) + s.shape, s.dtype) for s in shards],
        scratch_shapes=_gather_sems(ns),
        compiler_params=pltpu.CompilerParams(dimension_semantics=("arbitrary", "arbitrary"), vmem_limit_bytes=VMEM_LIMIT,
                                             has_side_effects=True))(a, b, *shards)


_PEERS = tuple((fx, fy, fc) for fx in (0, 1) for fy in (0, 1) for fc in (0, 1))[1:]


def _allreduce_small(x):
    r = x.shape[0]

    def body(x_ref, o_ref, buf, send_sems, recv_sems):
        cx, cy, cc = _coords()
        me = 4 * cx + 2 * cy + cc
        sends = []
        for j, (fx, fy, fc) in enumerate(_PEERS):
            cp = pltpu.make_async_remote_copy(x_ref, buf.at[me], send_sems.at[j], recv_sems.at[j],
                                              device_id=(_flip(cx, fx), _flip(cy, fy), _flip(cc, fc)), device_id_type=MESH)
            cp.start()
            sends.append(cp)
        buf[pl.ds(me, 1)] = x_ref[...][None]
        for j, (fx, fy, fc) in enumerate(_PEERS):
            src = 4 * _flip(cx, fx) + 2 * _flip(cy, fy) + _flip(cc, fc)
            pltpu.make_async_remote_copy(x_ref, buf.at[src], send_sems.at[j], recv_sems.at[j],
                                         device_id=(_flip(cx, fx), _flip(cy, fy), _flip(cc, fc)), device_id_type=MESH).wait_recv()
        for cp in sends:
            cp.wait_send()
        acc = buf[0]
        for d in range(1, N_DEV):
            acc = acc + buf[d]
        o_ref[...] = acc

    vm = pl.BlockSpec(memory_space=pltpu.VMEM)
    return pl.pallas_call(
        body, name="allreduce_small", out_shape=_sds((r, LANE)), in_specs=[vm], out_specs=vm,
        scratch_shapes=[pltpu.VMEM((N_DEV, r, LANE), F32), pltpu.SemaphoreType.DMA((7,)), pltpu.SemaphoreType.DMA((7,))],
        compiler_params=pltpu.CompilerParams(has_side_effects=True, vmem_limit_bytes=VMEM_LIMIT))(x)


def _pair_sums(tag, arrs):
    return _pair_adds(tag, arrs, _pair_send_other_half("pair_reduce_send_" + tag, arrs))


def _pair_adds(tag, arrs, received):
    core = lax.axis_index("c").astype(jnp.int32).reshape(1)
    return [_pair_add("pair_add_%s_%d" % (tag, i), a, b, core) for i, (a, b) in enumerate(zip(arrs, received))]


def _pair_send_ops(ins, outs, sems):
    send_sems, recv_sems = sems
    x, y, c = _coords()

    def cp(a):
        half = ins[a].shape[1] // 2
        return pltpu.make_async_remote_copy(ins[a].at[:, pl.ds((1 - c) * half, half), :], outs[a], send_sems.at[a], recv_sems.at[a],
                                            device_id=(x, y, 1 - c), device_id_type=MESH)

    def start():
        for a in range(len(ins)):
            cp(a).start()

    def finish():
        for a in range(len(ins)):
            cp(a).wait_recv()
        for a in range(len(ins)):
            cp(a).wait_send()

    return start, finish


def _pair_send_shapes(arrs):
    return [_sds((a.shape[0], a.shape[1] // 2, a.shape[2]), a.dtype) for a in arrs]


def _pair_send_other_half(name, arrs):
    n = len(arrs)

    def body(*refs):
        start, finish = _pair_send_ops(refs[:n], refs[n:2 * n], refs[2 * n:])
        start()
        finish()

    return pl.pallas_call(
        body, name=name, out_shape=_pair_send_shapes(arrs), in_specs=[_ANY] * n, out_specs=[_ANY] * n,
        scratch_shapes=[pltpu.SemaphoreType.DMA((n,))] * 2, compiler_params=pltpu.CompilerParams(has_side_effects=True))(*arrs)


def _pair_fill(arrs):
    n = len(arrs)

    def body(*refs):
        bufs = refs[n:2 * n]
        send_sems, recv_sems = refs[2 * n:]
        x, y, c = _coords()
        sends = []
        for a in range(n):
            cp = pltpu.make_async_remote_copy(bufs[a].at[c], bufs[a].at[c], send_sems.at[a], recv_sems.at[a],
                                              device_id=(x, y, 1 - c), device_id_type=MESH)
            cp.start()
            sends.append(cp)
        for a in range(n):
            theirs = bufs[a].at[1 - c]
            pltpu.make_async_remote_copy(theirs, theirs, send_sems.at[a], recv_sems.at[a],
                                         device_id=(x, y, 1 - c), device_id_type=MESH).wait_recv()
        for cp in sends:
            cp.wait_send()

    return pl.pallas_call(
        body, name="pair_gather", out_shape=[_sds(a.shape, a.dtype) for a in arrs], in_specs=[_ANY] * n, out_specs=[_ANY] * n,
        scratch_shapes=[pltpu.SemaphoreType.DMA((n,))] * 2, input_output_aliases={a: a for a in range(n)},
        compiler_params=pltpu.CompilerParams(has_side_effects=True))(*arrs)


def _scatter_ops(ins, outs, sems):
    send_sems, recv_sems = sems
    x, y, c = _coords()

    def cp(a, j):
        fx, fy = _OTHER_CHIPS[j]
        to = 2 * _flip(x, fx) + _flip(y, fy)
        return pltpu.make_async_remote_copy(ins[a].at[to], outs[a].at[j], send_sems.at[a * 3 + j], recv_sems.at[a * 3 + j],
                                            device_id=(_flip(x, fx), _flip(y, fy), c), device_id_type=MESH)

    pairs = [(a, j) for a in range(len(ins)) for j in range(3)]

    def start():
        for a, j in pairs:
            cp(a, j).start()

    def finish():
        for a, j in pairs:
            cp(a, j).wait_recv()
        for a, j in pairs:
            cp(a, j).wait_send()

    return start, finish


def _scatter_sems(n):
    return [pltpu.SemaphoreType.DMA((3 * n,))] * 2


def _pair_add(name, full, recv, core):
    _, r, c = full.shape
    half = r // 2
    tr = _pick(half, (256, 176, 128, 64, 8))
    nb = half // tr

    def body(c_ref, a_ref, b_ref, o_ref, ob_ref):
        s = a_ref[...] + b_ref[...]
        o_ref[...] = s
        ob_ref[...] = s.astype(BF16)

    blk = pl.BlockSpec((1, tr, c), lambda k, i, cref: (k, i, 0))
    grid_spec = pltpu.PrefetchScalarGridSpec(
        num_scalar_prefetch=1, grid=(N_CHIPS, nb),
        in_specs=[pl.BlockSpec((1, tr, c), lambda k, i, cref: (k, cref[0] * nb + i, 0)), blk], out_specs=[blk, blk])
    return pl.pallas_call(
        body, name=name, out_shape=[_sds((N_CHIPS, half, c)), _sds((N_CHIPS, half, c), BF16)], grid_spec=grid_spec,
        compiler_params=pltpu.CompilerParams(dimension_semantics=("parallel", "parallel"), vmem_limit_bytes=VMEM_LIMIT))(
            core, full, recv)


def _chip_sum(name, landed, own, where):
    _, r, c = landed.shape
    tr = _pick(r, (256, 176, 128, 64, 16))

    def body(w_ref, l_ref, o_ref, s_ref):
        s_ref[0] = ((o_ref[0] + l_ref[0].astype(F32)) + l_ref[1].astype(F32)) + l_ref[2].astype(F32)

    grid_spec = pltpu.PrefetchScalarGridSpec(
        num_scalar_prefetch=1, grid=(r // tr,),
        in_specs=[pl.BlockSpec((3, tr, c), lambda i, wref: (0, i, 0)),
                  pl.BlockSpec((1, tr, c), lambda i, wref: (wref[0], i, 0))],
        out_specs=pl.BlockSpec((1, tr, c), lambda i, wref: (wref[1], i, 0)))
    return pl.pallas_call(
        body, name=name, out_shape=_sds((2, r, c)), grid_spec=grid_spec,
        compiler_params=pltpu.CompilerParams(dimension_semantics=("parallel",), vmem_limit_bytes=VMEM_LIMIT))(where, landed, own)


_WEIGHTS = ("pre_mix_norm", "w_in", "gdn_conv_w", "gdn_a_log", "gdn_dt_bias", "gdn_norm_w", "ssd_conv_w", "ssd_conv_b",
            "ssd_a_log", "ssd_dt_bias", "ssd_d", "ssd_norm_w", "w_out", "post_mix_norm", "pre_ffn_norm", "w_up",
            "ffn_conv_w", "ffn_conv_b", "w_down", "post_ffn_norm")
_BIG = ("w_in", "w_out", "w_up", "w_down")
_COL_SHARDED_SMALL = ("gdn_conv_w", "ssd_conv_w", "ffn_conv_w")
_SMALL = tuple(k for k in _WEIGHTS if k not in _BIG)


def _pack(arrs):
    flat = jnp.concatenate([a.reshape(-1) for a in arrs])
    rows = -(-flat.shape[0] // (8 * LANE)) * 8
    return jnp.pad(flat, (0, rows * LANE - flat.shape[0])).reshape(rows, LANE)


def _unpack(packed, shapes):
    flat = packed.reshape(-1)
    out, off = [], 0
    for shp in shapes:
        size = 1
        for d in shp:
            size *= d
        out.append(flat[off:off + size].reshape(shp))
        off += size
    return out


def _chips_to_cols(a):
    k, r, c = a.shape
    return jnp.transpose(a, (1, 0, 2)).reshape(r, k * c)


def kernel(x, pre_mix_norm, w_in, gdn_conv_w, gdn_a_log, gdn_dt_bias, gdn_norm_w, ssd_conv_w, ssd_conv_b, ssd_a_log, ssd_dt_bias, ssd_d, ssd_norm_w, w_out, post_mix_norm, pre_ffn_norm, w_up, ffn_conv_w, ffn_conv_b, w_down, post_ffn_norm, loss_target, m_pre_mix_norm, m_w_in, m_gdn_conv_w, m_gdn_a_log, m_gdn_dt_bias, m_gdn_norm_w, m_ssd_conv_w, m_ssd_conv_b, m_ssd_a_log, m_ssd_dt_bias, m_ssd_d, m_ssd_norm_w, m_w_out, m_post_mix_norm, m_pre_ffn_norm, m_w_up, m_ffn_conv_w, m_ffn_conv_b, m_w_down, m_post_ffn_norm, v_pre_mix_norm, v_w_in, v_gdn_conv_w, v_gdn_a_log, v_gdn_dt_bias, v_gdn_norm_w, v_ssd_conv_w, v_ssd_conv_b, v_ssd_a_log, v_ssd_dt_bias, v_ssd_d, v_ssd_norm_w, v_w_out, v_post_mix_norm, v_pre_ffn_norm, v_w_up, v_ffn_conv_w, v_ffn_conv_b, v_w_down, v_post_ffn_norm):
    w = dict(zip(_WEIGHTS, (pre_mix_norm, w_in, gdn_conv_w, gdn_a_log, gdn_dt_bias, gdn_norm_w, ssd_conv_w, ssd_conv_b,
                            ssd_a_log, ssd_dt_bias, ssd_d, ssd_norm_w, w_out, post_mix_norm, pre_ffn_norm, w_up,
                            ffn_conv_w, ffn_conv_b, w_down, post_ffn_norm)))
    m = dict(zip(_WEIGHTS, (m_pre_mix_norm, m_w_in, m_gdn_conv_w, m_gdn_a_log, m_gdn_dt_bias, m_gdn_norm_w, m_ssd_conv_w,
                            m_ssd_conv_b, m_ssd_a_log, m_ssd_dt_bias, m_ssd_d, m_ssd_norm_w, m_w_out, m_post_mix_norm,
                            m_pre_ffn_norm, m_w_up, m_ffn_conv_w, m_ffn_conv_b, m_w_down, m_post_ffn_norm)))
    v = dict(zip(_WEIGHTS, (v_pre_mix_norm, v_w_in, v_gdn_conv_w, v_gdn_a_log, v_gdn_dt_bias, v_gdn_norm_w, v_ssd_conv_w,
                            v_ssd_conv_b, v_ssd_a_log, v_ssd_dt_bias, v_ssd_d, v_ssd_norm_w, v_w_out, v_post_mix_norm,
                            v_pre_ffn_norm, v_w_up, v_ffn_conv_w, v_ffn_conv_b, v_w_down, v_post_ffn_norm)))
    cx, cy, cc = _coords()
    chip = 2 * cx + cy

    p = {k: w[k] for k in _SMALL if k not in _COL_SHARDED_SMALL}
    for k in _BIG:
        p[k] = w[k][0].astype(BF16)
    for k in _COL_SHARDED_SMALL:
        p[k] = w[k][0]
    loss_acc, grad_x, grads, (pair_sum_list, landed_list) = _local_step(x, loss_target, p)
    loss = lax.psum(loss_acc[0, 0], ("x", "y", "c"))

    small_full_shapes = [grads[k].shape for k in _SMALL]
    summed = _unpack(_allreduce_small(_pack([grads[k] for k in _SMALL])), small_full_shapes)
    g_small = dict(zip(_SMALL, summed))
    for k in _COL_SHARDED_SMALL:
        width = w[k].shape[2]
        g_small[k] = lax.dynamic_slice_in_dim(g_small[k], chip * width, width, axis=1)

    pair_sums = dict(zip(_BIG, pair_sum_list))
    landed = dict(zip(_BIG, landed_list))
    where = jnp.stack([chip, cc]).astype(jnp.int32)
    mine = [_chip_sum("chip_sum_" + k, landed[k], pair_sums[k][0], where) for k in _BIG]
    both = _pair_fill(mine)
    g_big = {k: a.reshape(-1, a.shape[2]) for k, a in zip(_BIG, both)}

    out_g, out_d, out_m, out_v = {}, {}, {}, {}
    for k in _BIG:
        out_g[k] = g_big[k][None]
        d_, m_, v_ = _adamw("adamw_" + k, w[k][0], g_big[k], m[k][0], v[k][0])
        out_d[k], out_m[k], out_v[k] = d_[None], m_[None], v_[None]
    for k in _SMALL:
        out_g[k] = g_small[k].reshape(w[k].shape)
    two_d = {k: (w[k].shape[-2], w[k].shape[-1]) for k in _SMALL}
    small_out = _adamw_many([[d[k].reshape(two_d[k]) for k in _SMALL] for d in (w, out_g, m, v)])
    for dst, outs in zip((out_d, out_m, out_v), small_out):
        dst.update({k: o.reshape(w[k].shape) for k, o in zip(_SMALL, outs)})
    return (loss, grad_x, *[out_g[k] for k in _WEIGHTS], *[out_d[k] for k in _WEIGHTS],
            *[out_m[k] for k in _WEIGHTS], *[out_v[k] for k in _WEIGHTS])
```

```python
import functools

import jax
import jax.numpy as jnp
from jax import lax
from jax.experimental import pallas as pl
from jax.experimental.pallas import tpu as pltpu

F32 = jnp.float32
BF16 = jnp.bfloat16

D_MODEL = 1024
GDN_HEADS = 8
GDN_DK = 128
SSD_HEADS = 16
SSD_HEADDIM = 64
SSD_GROUPS = 2
SSD_STATE = 128
CONV_K = 4
CHUNK = 64
D_FF = 2816
FFN_CONV_K = 3
EPS = 1e-6
GDN_QK = GDN_HEADS * GDN_DK
GDN_V = GDN_QK
SSD_D = SSD_HEADS * SSD_HEADDIM
SSD_BC = SSD_GROUPS * SSD_STATE
SSD_CONV_CH = SSD_D + 2 * SSD_BC
BIG = 4 * 1024 + 1024 + SSD_CONV_CH
SMALL = 128
D_IN_PROJ = 6688
_SHARD_W = D_IN_PROJ // 4
_NARROW_A = 4096
_NARROW_B = 6672
LANE = 128
PAIR = 2 * CHUNK
NEG = -1e30
VMEM_LIMIT = 56 * 1024 * 1024

ADAM_LR = 0.001
ADAM_B1 = 0.9
ADAM_B2 = 0.999
ADAM_EPS = 1e-08
ADAM_WD = 0.01
ADAM_STEP = 10

N_CHIPS = 4
N_DEV = 8
MESH = pl.DeviceIdType.MESH

NN = ((1,), (0,))
NT = ((1,), (1,))
TN = ((0,), (0,))


def _bdot(a, b, dims):
    return lax.dot_general(a.astype(BF16), b.astype(BF16), (dims, ((), ())), preferred_element_type=F32)


def _split3(a):
    hi = a.astype(BF16)
    r1 = a - hi.astype(F32)
    mid = r1.astype(BF16)
    return hi, mid, (r1 - mid.astype(F32)).astype(BF16)


@jax.custom_vjp
def _nn(a, b):
    return _bdot(a, b, NN)


@jax.custom_vjp
def _nt(a, b):
    return _bdot(a, b, NT)


@jax.custom_vjp
def _tn(a, b):
    return _bdot(a, b, TN)


_nn.defvjp(lambda a, b: (_nn(a, b), (a, b)), lambda r, g: (_nt(g, r[1]), _tn(r[0], g)))
_nt.defvjp(lambda a, b: (_nt(a, b), (a, b)), lambda r, g: (_nn(g, r[1]), _tn(g, r[0])))
_tn.defvjp(lambda a, b: (_tn(a, b), (a, b)), lambda r, g: (_nt(r[1], g), _nn(r[0], g)))


def _mask_dot(x, mask, dims, x_first):
    acc = None
    for piece in _split3(x):
        term = _bdot(piece, mask, dims) if x_first else _bdot(mask, piece, dims)
        acc = term if acc is None else acc + term
    return acc


@jax.custom_vjp
def _cst_left(cst, x):
    return _mask_dot(x, cst, NN, False)


_cst_left.defvjp(lambda cst, x: (_cst_left(cst, x), cst), lambda cst, g: (jnp.zeros_like(cst), _mask_dot(g, cst, TN, False)))


@jax.custom_vjp
def _cst_right(x, cst):
    return _mask_dot(x, cst, NN, True)


_cst_right.defvjp(lambda x, cst: (_cst_right(x, cst), cst), lambda cst, g: (_mask_dot(g, cst, NT, True), jnp.zeros_like(cst)))


def _lin_left(cst):
    return functools.partial(_cst_left, cst)


def _lin_right(cst):
    return lambda x: _cst_right(x, cst)


@jax.custom_vjp
def _tri_inv_m1(a):
    pm = [-x for x in a]
    ap = list(a)
    for _ in range(5):
        ap = [_bdot(x, x, NN) for x in ap]
        pm = [(p + x) + _bdot(p, x, NN) for p, x in zip(pm, ap)]
    return pm


def _tri_inv_m1_bwd(pm, g):
    t = [gi + _bdot(p, gi, TN) for p, gi in zip(pm, g)]
    return ([-(ti + _bdot(ti, p, NT)) for p, ti in zip(pm, t)],)


_tri_inv_m1.defvjp(lambda a: (lambda pm: (pm, pm))(_tri_inv_m1(a)), _tri_inv_m1_bwd)


@jax.custom_vjp
def _top(x):
    return x[: x.shape[0] // 2]


_top.defvjp(lambda x: (_top(x), None), lambda _, g: (jnp.concatenate([g, jnp.zeros_like(g)], axis=0),))


@jax.custom_vjp
def _bot(x):
    return x[x.shape[0] // 2:]


_bot.defvjp(lambda x: (_bot(x), None), lambda _, g: (jnp.concatenate([jnp.zeros_like(g), g], axis=0),))


@jax.custom_vjp
def _vstack(a, b):
    return jnp.concatenate([a, b], axis=0)


_vstack.defvjp(lambda a, b: (_vstack(a, b), None), lambda _, g: (g[: g.shape[0] // 2], g[g.shape[0] // 2:]))


def _shift_dn_raw(x, s):
    if s == 0:
        return x
    r = pltpu.roll(x, s, axis=0)
    ri = lax.broadcasted_iota(jnp.int32, x.shape, 0)
    return jnp.where(ri >= s, r, 0.0)


def _shift_up_raw(x, s):
    if s == 0:
        return x
    n = x.shape[0]
    r = pltpu.roll(x, n - s, axis=0)
    ri = lax.broadcasted_iota(jnp.int32, x.shape, 0)
    return jnp.where(ri < n - s, r, 0.0)


@functools.partial(jax.custom_vjp, nondiff_argnums=(1,))
def _shift_dn(x, s):
    return _shift_dn_raw(x, s)


_shift_dn.defvjp(lambda x, s: (_shift_dn_raw(x, s), None), lambda s, _, g: (_shift_up_raw(g, s),))


def _conv(x, wrows):
    k_w = len(wrows)
    acc = wrows[k_w - 1] * x
    for k in range(k_w - 1):
        acc = acc + wrows[k] * _shift_dn(x, k_w - 1 - k)
    return acc


def _silu(x):
    return x * jax.nn.sigmoid(x)


def _rms(x, w):
    return x * lax.rsqrt(jnp.mean(x * x, axis=-1, keepdims=True) + EPS) * w


def _l2n(x):
    return x * lax.rsqrt(jnp.sum(x * x, axis=-1, keepdims=True) + EPS)


def _iota2(shape):
    return lax.broadcasted_iota(jnp.int32, shape, 0), lax.broadcasted_iota(jnp.int32, shape, 1)


_GDN_NARGS = 15
_SSD_NARGS = 8


def _gdn_multi(*flat):
    pairs = [flat[i:i + _GDN_NARGS] for i in range(0, len(flat), _GDN_NARGS)]
    idx = range(len(pairs))
    ri, ci = _iota2((PAIR, PAIR))
    blk = ((ri >= CHUNK) & (ci >= CHUNK)) | ((ri < CHUNK) & (ci < CHUNK))
    causal = blk & (ri >= ci)
    strict = blk & (ri > ci)
    q = [_vstack(p[0], p[1]) for p in pairs]
    k = [_vstack(p[2], p[3]) for p in pairs]
    v = [_vstack(p[4], p[5]) for p in pairs]
    gc = [_vstack(p[6], p[7]) for p in pairs]
    beta = [_vstack(p[8], p[9]) for p in pairs]
    glast = [_vstack(jnp.broadcast_to(p[11], (CHUNK, LANE)), jnp.broadcast_to(p[12], (CHUNK, LANE))) for p in pairs]
    sa = [p[13] for p in pairs]
    sb = [p[14] for p in pairs]
    decay = [jnp.exp(jnp.where(causal, gc[i] - jnp.broadcast_to(pairs[i][10], (PAIR, PAIR)), NEG)) for i in idx]
    eg = [jnp.exp(x) for x in gc]
    kbeta = [k[i] * beta[i] for i in idx]
    pm = _tri_inv_m1([jnp.where(strict, _nt(kbeta[i], k[i]) * decay[i], 0.0) for i in idx])
    qk = [_nt(q[i], k[i]) * decay[i] for i in idx]
    rhs_v = [v[i] * beta[i] for i in idx]
    rhs_k = [kbeta[i] * eg[i] for i in idx]
    u = [rhs_v[i] + _nn(pm[i], rhs_v[i]) for i in idx]
    w = [rhs_k[i] + _nn(pm[i], rhs_k[i]) for i in idx]
    q_dec = [q[i] * eg[i] for i in idx]
    k_dec = [k[i] * jnp.exp(glast[i] - gc[i]) for i in idx]
    gl = [jnp.exp(x) for x in glast]
    w_s = [_vstack(_nn(_top(w[i]), sa[i]), _nn(_bot(w[i]), sb[i])) for i in idx]
    q_s = [_vstack(_nn(_top(q_dec[i]), sa[i]), _nn(_bot(q_dec[i]), sb[i])) for i in idx]
    v_new = [u[i] - w_s[i] for i in idx]
    o = [q_s[i] + _nn(qk[i], v_new[i]) for i in idx]
    sa2 = [sa[i] * _vstack(_top(gl[i]), _top(gl[i])) + _tn(_top(k_dec[i]), _top(v_new[i])) for i in idx]
    sb2 = [sb[i] * _vstack(_bot(gl[i]), _bot(gl[i])) + _tn(_bot(k_dec[i]), _bot(v_new[i])) for i in idx]
    out = []
    for i in idx:
        out += [_top(o[i]), _bot(o[i]), sa2[i], sb2[i]]
    return tuple(out)


def _ssd_multi(*flat):
    pairs = [flat[i:i + _SSD_NARGS] for i in range(0, len(flat), _SSD_NARGS)]
    idx = range(len(pairs))
    ri, ci = _iota2((CHUNK, PAIR))
    causal = ri >= jnp.where(ci >= CHUNK, ci - CHUNK, ci)
    xdt = [p[0] * p[1] for p in pairs]
    acs = [p[2] for p in pairs]
    alast = [jnp.broadcast_to(p[4], (CHUNK, PAIR)) for p in pairs]
    lmat = [jnp.exp(jnp.where(causal, acs[i] - jnp.broadcast_to(pairs[i][3], (CHUNK, PAIR)), NEG)) for i in idx]
    cb2 = [_nt(p[6], _vstack(p[5], p[5])) for p in pairs]
    xblk = [_vstack(jnp.where(ci < CHUNK, x, 0.0), jnp.where(ci >= CHUNK, x, 0.0)) for x in xdt]
    y_off = [_nn(pairs[i][6], pairs[i][7]) * jnp.exp(acs[i]) for i in idx]
    y = [_nn(cb2[i] * lmat[i], xblk[i]) + y_off[i] for i in idx]
    el = [jnp.exp(x) for x in alast]
    st2 = [pairs[i][7] * _vstack(el[i], el[i]) + _tn(pairs[i][5], xdt[i] * jnp.exp(alast[i] - acs[i])) for i in idx]
    out = []
    for i in idx:
        out += [y[i], st2[i]]
    return tuple(out)


def _pcall(name, body, grid, in_specs, out_specs, out_shape, scratch=(), sem=None, aliases=None):
    if sem is None:
        sem = ("arbitrary",) * len(grid)
    return pl.pallas_call(
        functools.partial(body),
        out_shape=out_shape,
        grid=grid,
        in_specs=in_specs,
        out_specs=out_specs,
        scratch_shapes=scratch,
        input_output_aliases=aliases or {},
        name=name,
        compiler_params=pltpu.CompilerParams(dimension_semantics=sem, vmem_limit_bytes=VMEM_LIMIT),
    )


def _sds(shape, dtype=F32):
    return jax.ShapeDtypeStruct(shape, dtype)


def _row_spec(tm, width, colblock=0):
    return pl.BlockSpec((tm, width), lambda i, _c=colblock: (i, _c))


def _full_spec(shape):
    nd = len(shape)
    return pl.BlockSpec(shape, lambda *_: (0,) * nd)


def _zero_at_first(refs, first):
    @pl.when(first)
    def _():
        for r in refs:
            r[...] = jnp.zeros(r.shape, r.dtype)


def _pick(n, prefs):
    for p in prefs:
        if n % p == 0:
            return p
    return n


def _matmul(name, a, b, mode, out_dtype, tiles=None, part=None):
    def want(i, dim):
        return [tiles[i]] if tiles is not None and dim % tiles[i] == 0 else []

    if mode == "tn":
        r, m = a.shape
        n = b.shape[1]
        tm = _pick(m, want(0, m) + [1024, 1408])
        tn = _pick(n, want(1, n) + [512, 256, 128])
        tk = _pick(r, want(2, r) + [1024, 512, 256, 128, 64])
        in_specs = [pl.BlockSpec((tk, tm), lambda i, j, k: (k, i)), pl.BlockSpec((tk, tn), lambda i, j, k: (k, j))]
        sem = ("parallel", "parallel", "arbitrary")
        if part is None:
            nc = _pick(tn, (512, 256, 128))

            def body(a_ref, b_ref, o_ref):
                _zero_at_first([o_ref], pl.program_id(2) == 0)
                for c0 in range(0, tn, nc):
                    o_ref[:, c0:c0 + nc] += _bdot(a_ref[...], b_ref[:, c0:c0 + nc], TN)

            return _pcall(name, body, (m // tm, n // tn, r // tk), in_specs, pl.BlockSpec((tm, tn), lambda i, j, k: (i, j)),
                          _sds((m, n), out_dtype), sem=sem)(a, b)
        n_total, col_off, into = part
        width = n_total // N_CHIPS
        per_step = tn // width
        off = col_off // tn
        assert tn % width == 0 and col_off % tn == 0

        def body(a_ref, b_ref, *rest):
            o_ref = rest[-1]
            _zero_at_first([o_ref], pl.program_id(2) == 0)
            for q in range(per_step):
                o_ref[q] += _bdot(a_ref[...], b_ref[:, q * width:(q + 1) * width], TN)

        args = (a, b) if into is None else (a, b, into)
        return _pcall(
            name, body, (m // tm, n // tn, r // tk), in_specs + ([] if into is None else [_ANY]),
            pl.BlockSpec((per_step, tm, width), lambda i, j, k: (j + off, i, 0)), _sds((N_CHIPS, m, width), out_dtype),
            sem=sem, aliases=None if into is None else {2: 0})(*args)
    m, k = a.shape
    n = b.shape[1] if mode == "nn" else b.shape[0]
    tm = _pick(m, want(0, m) + ([1024, 512, 256, 128, 64] if k <= 2816 else [512, 256, 128, 64]))
    tn = _pick(n, want(1, n) + [512, 256, 128])
    dims = NN if mode == "nn" else NT

    nc = _pick(tn, (512, 256, 128))

    def body(a_ref, b_ref, o_ref):
        for c0 in range(0, tn, nc):
            b_blk = b_ref[:, c0:c0 + nc] if mode == "nn" else b_ref[c0:c0 + nc, :]
            o_ref[:, c0:c0 + nc] = _bdot(a_ref[...], b_blk, dims).astype(o_ref.dtype)

    b_spec = pl.BlockSpec((k, tn), lambda i, j: (0, j)) if mode == "nn" else pl.BlockSpec((tn, k), lambda i, j: (j, 0))
    return _pcall(
        name, body, (m // tm, n // tn), [pl.BlockSpec((tm, k), lambda i, j: (i, 0)), b_spec],
        pl.BlockSpec((tm, tn), lambda i, j: (i, j)), _sds((m, n), out_dtype), sem=("parallel", "parallel"))(a, b)


def _matmul_nt_scattering(name, a, b, tm_pref, scatter):
    m, k = a.shape
    n = b.shape[0]
    tm = _pick(m, (tm_pref, 256, 128, 64))
    nc = _pick(n, (512, 256, 128))
    ns = len(scatter)
    steps = m // tm

    def body(a_ref, b_ref, *rest):
        sc_in, o_ref, sc_out, sems = rest[:ns], rest[ns], rest[ns + 1:2 * ns + 1], rest[2 * ns + 1:]
        sc_start, sc_finish = _scatter_ops(sc_in, sc_out, sems)
        pl.when(pl.program_id(0) == 0)(sc_start)
        for c0 in range(0, n, nc):
            o_ref[:, c0:c0 + nc] = _bdot(a_ref[...], b_ref[c0:c0 + nc, :], NT)
        pl.when(pl.program_id(0) == steps - 1)(sc_finish)

    outs = pl.pallas_call(
        body, name=name, grid=(steps,),
        in_specs=[pl.BlockSpec((tm, k), lambda i: (i, 0)), _full_spec(b.shape)] + [_ANY] * ns,
        out_specs=[pl.BlockSpec((tm, n), lambda i: (i, 0))] + [_ANY] * ns,
        out_shape=[_sds((m, n))] + [_sds((3,) + s.shape[1:], s.dtype) for s in scatter],
        scratch_shapes=_scatter_sems(ns),
        compiler_params=pltpu.CompilerParams(dimension_semantics=("arbitrary",), vmem_limit_bytes=VMEM_LIMIT,
                                             has_side_effects=True))(a, b, *scatter)
    return outs[0], outs[1:]


def _matmul_nt_split(name, a1, a2, b, tm_pref):
    m, kh = a1.shape
    n = b.shape[0]
    tm = _pick(m, (tm_pref, 512, 256, 128, 64))
    nc = _pick(n, (512, 256, 128))

    def body(a1_ref, a2_ref, b_ref, o_ref):
        for c0 in range(0, n, nc):
            o_ref[:, c0:c0 + nc] = (_bdot(a1_ref[...], b_ref[c0:c0 + nc, :kh], NT)
                                    + _bdot(a2_ref[...], b_ref[c0:c0 + nc, kh:], NT))

    aspec = pl.BlockSpec((tm, kh), lambda i: (i, 0))
    return _pcall(name, body, (m // tm,), [aspec, aspec, _full_spec(b.shape)], pl.BlockSpec((tm, n), lambda i: (i, 0)),
                  _sds((m, n)), sem=("parallel",))(a1, a2, b)


def _row_tile(t):
    return _pick(t, (512, 256, 128, 64))


def _rms_fwd_gathering(name, x, g, shards, split):
    t = x.shape[0]
    tm = next(c for c in (128, 64, 32, 16) if t % c == 0 and t // c >= 2)
    steps = t // tm
    ns = len(shards)

    def body(x_ref, g_ref, *rest):
        ins, h_ref, outs, sems = rest[:ns], rest[ns], rest[ns + 1:2 * ns + 1], rest[2 * ns + 1:]
        start, forward, finish = _gather_ops(ins, outs, sems, split)
        pl.when(pl.program_id(0) == 0)(start)
        h_ref[...] = _rms(x_ref[...], g_ref[...]).astype(BF16)
        pl.when(pl.program_id(0) == steps - 2)(forward)
        pl.when(pl.program_id(0) == steps - 1)(finish)

    assert steps >= 2
    outs = pl.pallas_call(
        body, name=name, grid=(steps,),
        in_specs=[_row_spec(tm, D_MODEL), _full_spec((1, D_MODEL))] + [_ANY] * ns,
        out_specs=[_row_spec(tm, D_MODEL)] + [_ANY] * ns,
        out_shape=[_sds((t, D_MODEL), BF16)] + [_sds((N_CHIPS,) + s.shape, s.dtype) for s in shards],
        scratch_shapes=_gather_sems(ns),
        compiler_params=pltpu.CompilerParams(dimension_semantics=("arbitrary",), vmem_limit_bytes=VMEM_LIMIT,
                                             has_side_effects=True))(x, g, *shards)
    return outs[0], outs[1:]


_G_LO, _G_HI = GDN_HEADS, 2 * GDN_HEADS
_S_LO, _S_HI = 2 * GDN_HEADS, 2 * GDN_HEADS + SSD_HEADS


def _gates_fn(small, bias, a_log):
    tm = small.shape[0]
    rr, cc = _iota2((tm, tm))
    in_chunk_tril = (((rr >> 6) == (cc >> 6)) & (rr >= cc)).astype(F32)
    lane = lax.broadcasted_iota(jnp.int32, small.shape, 1)
    sp = jax.nn.softplus(small + bias)
    act = jnp.where(lane < _G_LO, jax.nn.sigmoid(small), sp)
    cum = _lin_left(in_chunk_tril)(-jnp.exp(a_log) * sp)
    r, c = _iota2((SMALL, D_MODEL))
    to_ssd_lanes = _lin_right((r == _S_LO + (c >> 6)).astype(F32))
    return act, cum, to_ssd_lanes(act), to_ssd_lanes(cum)


def _expand_lanes(src_ref, dst_ref, lo, heads, width):
    rows = src_ref.shape[0]
    for h in range(heads):
        dst_ref[:, h * width:(h + 1) * width] = jnp.broadcast_to(src_ref[:, lo + h:lo + h + 1], (rows, width))


def _reduce_lanes(wide_ref, lo, heads, width):
    rows = wide_ref.shape[0]
    lane = lax.broadcasted_iota(jnp.int32, (rows, SMALL), 1)
    acc = jnp.zeros((rows, SMALL), F32)
    for h in range(heads):
        col = jnp.sum(wide_ref[:, h * width:(h + 1) * width], axis=-1, keepdims=True)
        acc = jnp.where(lane == lo + h, jnp.broadcast_to(col, (rows, SMALL)), acc)
    return acc


def _gates_fwd(small, bias, a_log):
    t = small.shape[0]
    tm = _row_tile(t)

    def body(s_ref, p0, p1, act_ref, cum_ref, dt_ref, acs_ref):
        act_ref[...], cum_ref[...], dt_ref[...], acs_ref[...] = _gates_fn(s_ref[...], p0[...], p1[...])

    pspec, nspec, wspec = _full_spec((1, SMALL)), _row_spec(tm, SMALL), _row_spec(tm, D_MODEL)
    return _pcall("gates_fwd", body, (t // tm,), [nspec, pspec, pspec], [nspec, nspec, wspec, wspec],
                  [_sds((t, SMALL))] * 2 + [_sds((t, D_MODEL))] * 2, sem=("parallel",))(small, bias, a_log)


def _gates_bwd(small, bias, a_log, d_act, d_cums, d_dt, d_acs):
    t = small.shape[0]
    tm = _row_tile(t)
    nc = len(d_cums)

    def body(*refs):
        s_ref, p0, p1, dact_ref = refs[:4]
        dcum_refs = refs[4:4 + nc]
        ddt_ref, dacs_ref, ds_ref, db_ref, da_ref = refs[4 + nc:]
        _zero_at_first([db_ref, da_ref], pl.program_id(0) == 0)
        _, vjp = jax.vjp(_gates_fn, s_ref[...], p0[...], p1[...])
        d_cum = dcum_refs[0][...]
        for c in dcum_refs[1:]:
            d_cum = d_cum + c[...]
        d_s, d_b, d_a = vjp((dact_ref[...], d_cum, ddt_ref[...], dacs_ref[...]))
        ds_ref[...] = d_s.astype(BF16)
        db_ref[...] += d_b
        da_ref[...] += d_a

    pspec, nspec, wspec = _full_spec((1, SMALL)), _row_spec(tm, SMALL), _row_spec(tm, D_MODEL)
    return _pcall("gates_bwd", body, (t // tm,), [nspec, pspec, pspec] + [nspec] * (1 + nc) + [wspec, wspec],
                  [nspec, pspec, pspec], [_sds((t, SMALL), BF16), _sds((1, SMALL)), _sds((1, SMALL))])(
                      small, bias, a_log, d_act, *d_cums, d_dt, d_acs)


def _gdn_out_fn(o, z, w):
    return _rms(o, w) * _silu(z)


def _gdn_out_fwd(o, proj, gn):
    t = o.shape[0]
    tm = _row_tile(t)

    def body(o_ref, z_ref, w_ref, y_ref):
        for h in range(GDN_HEADS):
            sl = slice(h * GDN_DK, (h + 1) * GDN_DK)
            y_ref[:, sl] = _gdn_out_fn(o_ref[:, sl], z_ref[:, sl].astype(F32), w_ref[...]).astype(BF16)

    return _pcall("gdn_out_fwd", body, (t // tm,), [_row_spec(tm, GDN_V), _row_spec(tm, GDN_V, 3), _full_spec((1, GDN_DK))],
                  _row_spec(tm, GDN_V), _sds((t, GDN_V + SSD_D), BF16), sem=("parallel",))(o, proj, gn)


def _gdn_out_bwd(o, proj, gn, d_ocat):
    t = o.shape[0]
    tm = _row_tile(t)

    def body(o_ref, z_ref, w_ref, dy_ref, do_ref, dz_ref, dw_ref):
        _zero_at_first([dw_ref], pl.program_id(0) == 0)
        for h in range(GDN_HEADS):
            sl = slice(h * GDN_DK, (h + 1) * GDN_DK)
            _, vjp = jax.vjp(_gdn_out_fn, o_ref[:, sl], z_ref[:, sl].astype(F32), w_ref[...])
            d_o, d_z, d_w = vjp(dy_ref[:, sl])
            do_ref[:, sl] = d_o
            dz_ref[:, sl] = d_z.astype(BF16)
            dw_ref[...] += d_w

    return _pcall("gdn_out_bwd", body, (t // tm,),
                  [_row_spec(tm, GDN_V), _row_spec(tm, GDN_V, 3), _full_spec((1, GDN_DK)), _row_spec(tm, GDN_V, 0)],
                  [_row_spec(tm, GDN_V), _row_spec(tm, GDN_V, 3), _full_spec((1, GDN_DK))],
                  [_sds((t, GDN_V)), _sds((t, BIG), BF16), _sds((1, GDN_DK))])(o, proj, gn, d_ocat)


def _ssd_out_fn(y, xs, z, d_skip, w):
    return _rms((y + d_skip * xs) * _silu(z), w)


_SSD_GW = SSD_D // SSD_GROUPS


def _ssd_out_fwd(y, xbc, proj, d_skip, nw, ocat):
    t = y.shape[0]
    tm = _row_tile(t)

    def body(y_ref, x_ref, z_ref, d_ref, w_ref, _, o_ref):
        for gi in range(SSD_GROUPS):
            sl = slice(gi * _SSD_GW, (gi + 1) * _SSD_GW)
            o_ref[:, sl] = _ssd_out_fn(y_ref[:, sl], x_ref[:, sl], z_ref[:, sl].astype(F32), d_ref[:, sl], w_ref[:, sl]).astype(BF16)

    pspec = _full_spec((1, SSD_D))
    return _pcall("ssd_out_fwd", body, (t // tm,),
                  [_row_spec(tm, SSD_D), _row_spec(tm, SSD_D, 0), _row_spec(tm, SSD_D, 4), pspec, pspec, _ANY],
                  _row_spec(tm, SSD_D, 1), _sds(ocat.shape, BF16), sem=("parallel",), aliases={5: 0})(
                      y, xbc, proj, d_skip, nw, ocat)


def _ssd_out_bwd(y, xbc, proj, d_skip, nw, d_ocat, d_proj):
    t = y.shape[0]
    tm = _row_tile(t)

    def body(y_ref, x_ref, z_ref, d_ref, w_ref, do_ref, _, dy_ref, dx_ref, dz_ref, dd_ref, dw_ref):
        _zero_at_first([dd_ref, dw_ref], pl.program_id(0) == 0)
        for gi in range(SSD_GROUPS):
            sl = slice(gi * _SSD_GW, (gi + 1) * _SSD_GW)
            _, vjp = jax.vjp(_ssd_out_fn, y_ref[:, sl], x_ref[:, sl], z_ref[:, sl].astype(F32), d_ref[:, sl], w_ref[:, sl])
            d_y, d_x, d_z, d_d, d_w = vjp(do_ref[:, sl])
            dy_ref[:, sl] = d_y
            dx_ref[:, sl] = d_x
            dz_ref[:, sl] = d_z.astype(BF16)
            dd_ref[:, sl] += d_d
            dw_ref[:, sl] += d_w

    pspec = _full_spec((1, SSD_D))
    row = _row_spec(tm, SSD_D)
    return _pcall("ssd_out_bwd", body, (t // tm,),
                  [row, _row_spec(tm, SSD_D, 0), _row_spec(tm, SSD_D, 4), pspec, pspec, _row_spec(tm, SSD_D, 1), _ANY],
                  [row, row, _row_spec(tm, SSD_D, 4), pspec, pspec],
                  [_sds((t, SSD_D)), _sds((t, SSD_D)), _sds((t, BIG), BF16), _sds((1, SSD_D)), _sds((1, SSD_D))],
                  aliases={6: 2})(y, xbc, proj, d_skip, nw, d_ocat, d_proj)


def _res1_fn(x, mix, g_pm, g_pf):
    x1 = x + _rms(mix, g_pm)
    return x1, _rms(x1, g_pf)


def _res1_fwd(x, mix, g_pm, g_pf):
    t = x.shape[0]
    tm = _row_tile(t)

    def body(x_ref, m_ref, a_ref, b_ref, x1_ref, h2_ref):
        x1, h2 = _res1_fn(x_ref[...], m_ref[...], a_ref[...], b_ref[...])
        x1_ref[...] = x1
        h2_ref[...] = h2.astype(BF16)

    row, pspec = _row_spec(tm, D_MODEL), _full_spec((1, D_MODEL))
    return _pcall("res1_fwd", body, (t // tm,), [row, row, pspec, pspec], [row, row],
                  [_sds((t, D_MODEL)), _sds((t, D_MODEL), BF16)], sem=("parallel",))(x, mix, g_pm, g_pf)


def _res1_bwd(x, mix, g_pm, g_pf, d_x1, d_h2, pair_send):
    t = x.shape[0]
    tm = _row_tile(t)
    ns = len(pair_send)
    steps = t // tm

    def body(x_ref, m_ref, a_ref, b_ref, c1_ref, c2_ref, *rest):
        ps_in, rest = rest[:ns], rest[ns:]
        dx_ref, dm_ref, da_ref, db_ref = rest[:4]
        ps_out, sems = rest[4:4 + ns], rest[4 + ns:]
        ps_start, ps_finish = _pair_send_ops(ps_in, ps_out, sems)
        pl.when(pl.program_id(0) == 0)(ps_start)
        _zero_at_first([da_ref, db_ref], pl.program_id(0) == 0)
        _, vjp = jax.vjp(_res1_fn, x_ref[...], m_ref[...], a_ref[...], b_ref[...])
        d_x, d_m, d_a, d_b = vjp((c1_ref[...], c2_ref[...]))
        dx_ref[...] = d_x
        dm_ref[...] = d_m.astype(BF16)
        da_ref[...] += d_a
        db_ref[...] += d_b
        pl.when(pl.program_id(0) == steps - 1)(ps_finish)

    row, pspec = _row_spec(tm, D_MODEL), _full_spec((1, D_MODEL))
    outs = pl.pallas_call(
        body, name="res1_bwd", grid=(steps,),
        in_specs=[row, row, pspec, pspec, row, row] + [_ANY] * ns, out_specs=[row, row, pspec, pspec] + [_ANY] * ns,
        out_shape=[_sds((t, D_MODEL)), _sds((t, D_MODEL), BF16), _sds((1, D_MODEL)), _sds((1, D_MODEL))]
        + _pair_send_shapes(pair_send),
        scratch_shapes=[pltpu.SemaphoreType.DMA((ns,))] * 2,
        compiler_params=pltpu.CompilerParams(dimension_semantics=("arbitrary",), vmem_limit_bytes=VMEM_LIMIT,
                                             has_side_effects=True))(x, mix, g_pm, g_pf, d_x1, d_h2, *pair_send)
    return outs[:4], outs[4:]


def _final_fn(x1, f, g_po, tgt):
    err = x1 + _rms(f, g_po) - tgt
    return 0.5 * jnp.sum(jnp.mean(err * err, axis=-1))


def _final(x1, f, g_po, tgt):
    t = x1.shape[0]
    tm = _row_tile(t)

    def body(x_ref, f_ref, g_ref, t_ref, loss_ref, dx_ref, df_ref, dg_ref):
        _zero_at_first([loss_ref, dg_ref], pl.program_id(0) == 0)
        loss, (d_x, d_f, d_g) = jax.value_and_grad(_final_fn, argnums=(0, 1, 2))(x_ref[...], f_ref[...], g_ref[...], t_ref[...])
        loss_ref[...] += jnp.broadcast_to(loss, loss_ref.shape)
        dx_ref[...] = d_x
        df_ref[...] = d_f.astype(BF16)
        dg_ref[...] += d_g

    row, pspec = _row_spec(tm, D_MODEL), _full_spec((1, D_MODEL))
    return _pcall("final", body, (t // tm,), [row, row, pspec, row], [_full_spec((8, LANE)), row, row, pspec],
                  [_sds((8, LANE)), _sds((t, D_MODEL)), _sds((t, D_MODEL), BF16), _sds((1, D_MODEL))])(x1, f, g_po, tgt)


def _rms1_bwd(x, g, d_h_a, d_h_b, d_x1):
    t = x.shape[0]
    tm = _row_tile(t)

    def body(x_ref, g_ref, dha_ref, dhb_ref, dx1_ref, dx_ref, dg_ref):
        _zero_at_first([dg_ref], pl.program_id(0) == 0)
        _, vjp = jax.vjp(_rms, x_ref[...], g_ref[...])
        d_x, d_g = vjp(dha_ref[...] + dhb_ref[...])
        dx_ref[...] = d_x + dx1_ref[...]
        dg_ref[...] += d_g

    row, pspec = _row_spec(tm, D_MODEL), _full_spec((1, D_MODEL))
    return _pcall("rms1_bwd", body, (t // tm,), [row, pspec, row, row, row], [row, pspec],
                  [_sds((t, D_MODEL)), _sds((1, D_MODEL))])(x, g, d_h_a, d_h_b, d_x1)


def _qkv_fn(mode):
    def fn(x, *wrows):
        y = _silu(_conv(x, wrows))
        if mode == "q":
            return _l2n(y) * (GDN_DK ** -0.5)
        if mode == "k":
            return _l2n(y)
        return y
    return fn


def _seq_spec(s, tc, off):
    return pl.BlockSpec((s, tc), lambda j, b, _o=off: (b, _o + j))


def _par_spec(rows, tc, off):
    return pl.BlockSpec((rows, tc), lambda j, b, _o=off: (0, _o + j))


def _gdn_conv_fwd(mode, proj, w, bsz, s):
    off = {"q": 0, "k": GDN_HEADS, "v": 2 * GDN_HEADS}[mode]
    fn = _qkv_fn(mode)

    def body(x_ref, w_ref, y_ref):
        y_ref[...] = fn(x_ref[...].astype(F32), *[w_ref[k:k + 1, :] for k in range(CONV_K)])

    return _pcall("gdn_conv_fwd_" + mode, body, (GDN_HEADS, bsz),
                  [_seq_spec(s, GDN_DK, off), _par_spec(CONV_K, GDN_DK, off)], _seq_spec(s, GDN_DK, 0),
                  _sds((bsz * s, GDN_QK)), sem=("parallel", "parallel"))(proj, w)


def _gdn_conv_bwd(mode, proj, w, d_y, d_proj, bsz, s):
    off = {"q": 0, "k": GDN_HEADS, "v": 2 * GDN_HEADS}[mode]
    fn = _qkv_fn(mode)

    def body(x_ref, w_ref, dy_ref, _, dx_ref, dw_ref):
        _zero_at_first([dw_ref], pl.program_id(1) == 0)
        _, vjp = jax.vjp(fn, x_ref[...].astype(F32), *[w_ref[k:k + 1, :] for k in range(CONV_K)])
        grads = vjp(dy_ref[...])
        dx_ref[...] = grads[0].astype(BF16)
        for k in range(CONV_K):
            dw_ref[k:k + 1, :] += grads[1 + k]

    return _pcall("gdn_conv_bwd_" + mode, body, (GDN_HEADS, bsz),
                  [_seq_spec(s, GDN_DK, off), _par_spec(CONV_K, GDN_DK, off), _seq_spec(s, GDN_DK, 0), _ANY],
                  [_seq_spec(s, GDN_DK, off), _par_spec(CONV_K, GDN_DK, 0)],
                  [_sds(d_proj.shape, BF16), _sds((CONV_K, GDN_QK))], sem=("parallel", "arbitrary"), aliases={3: 0})(
                      proj, w, d_y, d_proj)


def _ssd_conv_fn(x, bias, *wrows):
    return _silu(_conv(x, wrows) + bias)


_XBC_OFF = (5 * 1024) // LANE


def _ssd_conv_fwd(proj, w, bias, bsz, s):
    nt_ = SSD_CONV_CH // LANE

    def body(x_ref, w_ref, b_ref, y_ref):
        y_ref[...] = _ssd_conv_fn(x_ref[...].astype(F32), b_ref[...], *[w_ref[k:k + 1, :] for k in range(CONV_K)])

    return _pcall("ssd_conv_fwd", body, (nt_, bsz),
                  [_seq_spec(s, LANE, _XBC_OFF), _par_spec(CONV_K, LANE, 0), _par_spec(1, LANE, 0)], _seq_spec(s, LANE, 0),
                  _sds((bsz * s, SSD_CONV_CH)), sem=("parallel", "parallel"))(proj, w, bias)


def _ssd_conv_bwd(proj, w, bias, d_y, d_proj, bsz, s):
    nt_ = SSD_CONV_CH // LANE

    def body(x_ref, w_ref, b_ref, dy_ref, _, dx_ref, dw_ref, db_ref):
        _zero_at_first([dw_ref, db_ref], pl.program_id(1) == 0)
        _, vjp = jax.vjp(_ssd_conv_fn, x_ref[...].astype(F32), b_ref[...], *[w_ref[k:k + 1, :] for k in range(CONV_K)])
        grads = vjp(dy_ref[...])
        dx_ref[...] = grads[0].astype(BF16)
        db_ref[...] += grads[1]
        for k in range(CONV_K):
            dw_ref[k:k + 1, :] += grads[2 + k]

    return _pcall("ssd_conv_bwd", body, (nt_, bsz),
                  [_seq_spec(s, LANE, _XBC_OFF), _par_spec(CONV_K, LANE, 0), _par_spec(1, LANE, 0), _seq_spec(s, LANE, 0), _ANY],
                  [_seq_spec(s, LANE, _XBC_OFF), _par_spec(CONV_K, LANE, 0), _par_spec(1, LANE, 0)],
                  [_sds(d_proj.shape, BF16), _sds((CONV_K, SSD_CONV_CH)), _sds((1, SSD_CONV_CH))],
                  sem=("parallel", "arbitrary"), aliases={4: 0})(proj, w, bias, d_y, d_proj)


_FFN_TC = 256
_FFN_NT = D_FF // _FFN_TC


def _ffn_act_fn(xg, xu, bg, bu, *wrows):
    k_w = FFN_CONV_K
    gate = _conv(xg, wrows[:k_w]) + bg
    up = _conv(xu, wrows[k_w:]) + bu
    return _silu(gate) * up


def _ffn_act_fwd(u_pre, w, bias, bsz, s):
    def body(xg_ref, xu_ref, wg_ref, wu_ref, bg_ref, bu_ref, a_ref):
        rows = [wg_ref[k:k + 1, :] for k in range(FFN_CONV_K)] + [wu_ref[k:k + 1, :] for k in range(FFN_CONV_K)]
        a_ref[...] = _ffn_act_fn(xg_ref[...].astype(F32), xu_ref[...].astype(F32), bg_ref[...], bu_ref[...], *rows).astype(BF16)

    return _pcall("ffn_act_fwd", body, (_FFN_NT, bsz),
                  [_seq_spec(s, _FFN_TC, 0), _seq_spec(s, _FFN_TC, _FFN_NT),
                   _par_spec(FFN_CONV_K, _FFN_TC, 0), _par_spec(FFN_CONV_K, _FFN_TC, _FFN_NT),
                   _par_spec(1, _FFN_TC, 0), _par_spec(1, _FFN_TC, _FFN_NT)],
                  _seq_spec(s, _FFN_TC, 0), _sds((bsz * s, D_FF), BF16), sem=("parallel", "parallel"))(
                      u_pre, u_pre, w, w, bias, bias)


def _ffn_act_bwd(u_pre, w, bias, d_a, bsz, s):
    def body(xg_ref, xu_ref, wg_ref, wu_ref, bg_ref, bu_ref, da_ref, dg_ref, du_ref, dwg_ref, dwu_ref, dbg_ref, dbu_ref):
        _zero_at_first([dwg_ref, dwu_ref, dbg_ref, dbu_ref], pl.program_id(1) == 0)
        rows = [wg_ref[k:k + 1, :] for k in range(FFN_CONV_K)] + [wu_ref[k:k + 1, :] for k in range(FFN_CONV_K)]
        _, vjp = jax.vjp(_ffn_act_fn, xg_ref[...].astype(F32), xu_ref[...].astype(F32), bg_ref[...], bu_ref[...], *rows)
        grads = vjp(da_ref[...])
        dg_ref[...] = grads[0].astype(BF16)
        du_ref[...] = grads[1].astype(BF16)
        dbg_ref[...] += grads[2]
        dbu_ref[...] += grads[3]
        for k in range(FFN_CONV_K):
            dwg_ref[k:k + 1, :] += grads[4 + k]
            dwu_ref[k:k + 1, :] += grads[4 + FFN_CONV_K + k]

    seq0, par3, par1 = _seq_spec(s, _FFN_TC, 0), _par_spec(FFN_CONV_K, _FFN_TC, 0), _par_spec(1, _FFN_TC, 0)
    return _pcall("ffn_act_bwd", body, (_FFN_NT, bsz),
                  [seq0, _seq_spec(s, _FFN_TC, _FFN_NT), par3, _par_spec(FFN_CONV_K, _FFN_TC, _FFN_NT),
                   par1, _par_spec(1, _FFN_TC, _FFN_NT), seq0],
                  [seq0, seq0, par3, par3, par1, par1],
                  [_sds((bsz * s, D_FF), BF16), _sds((bsz * s, D_FF), BF16), _sds((FFN_CONV_K, D_FF)), _sds((FFN_CONV_K, D_FF)),
                   _sds((1, D_FF)), _sds((1, D_FF))], sem=("parallel", "arbitrary"))(u_pre, u_pre, w, w, bias, bias, d_a)


_GP = GDN_HEADS // 2
_SP = SSD_HEADS // 2


def _pair_lanes(p):
    return slice(2 * p * LANE, (2 * p + 1) * LANE), slice((2 * p + 1) * LANE, (2 * p + 2) * LANE)


_LAST = slice(CHUNK - 1, CHUNK)


def _gdn_args(p, q_ref, k_ref, v_ref, g_ref, b_ref, gr_ref):
    la, lb = _pair_lanes(p)
    return (q_ref[:, la], q_ref[:, lb], k_ref[:, la], k_ref[:, lb], v_ref[:, la], v_ref[:, lb], g_ref[:, la], g_ref[:, lb],
            b_ref[:, la], b_ref[:, lb], gr_ref[p], g_ref[_LAST, la], g_ref[_LAST, lb])


def _gdn_fwd(q, k, v, act, cum, gc_row, bsz, n):
    def body(q_ref, k_ref, v_ref, act_ref, cum_ref, gr_ref, o_ref, sin_ref, s_scr, g_ref, b_ref):
        _zero_at_first([s_scr], pl.program_id(1) == 0)
        _expand_lanes(act_ref, b_ref, 0, GDN_HEADS, GDN_DK)
        _expand_lanes(cum_ref, g_ref, _G_LO, GDN_HEADS, GDN_DK)
        flat = []
        for p in range(_GP):
            flat += [*_gdn_args(p, q_ref, k_ref, v_ref, g_ref, b_ref, gr_ref), s_scr[2 * p], s_scr[2 * p + 1]]
        sin_ref[...] = s_scr[...]
        outs = _gdn_multi(*flat)
        for p in range(_GP):
            la, lb = _pair_lanes(p)
            o_ref[:, la], o_ref[:, lb], s_scr[2 * p], s_scr[2 * p + 1] = outs[4 * p:4 * p + 4]

    tspec = pl.BlockSpec((CHUNK, GDN_V), lambda b, c: (b * n + c, 0))
    rspec = pl.BlockSpec((_GP, 1, LANE), lambda b, c: (b * n + c, 0, 0))
    sspec = pl.BlockSpec((GDN_HEADS, LANE, LANE), lambda b, c: (b * n + c, 0, 0))
    nspec = pl.BlockSpec((CHUNK, SMALL), lambda b, c: (b * n + c, 0))
    wide = pltpu.VMEM((CHUNK, GDN_V), F32)
    return _pcall("gdn_fwd", body, (bsz, n), [tspec] * 3 + [nspec, nspec, rspec], [tspec, sspec],
                  [_sds((bsz * n * CHUNK, GDN_V)), _sds((bsz * n * GDN_HEADS, LANE, LANE))],
                  scratch=[pltpu.VMEM((GDN_HEADS, LANE, LANE), F32), wide, wide], sem=("parallel", "arbitrary"))(
                      q, k, v, act, cum, gc_row)


def _gdn_bwd(q, k, v, act, cum, gc_row, s_in, d_o, bsz, n, scatter):
    ns = len(scatter)

    def body(q_ref, k_ref, v_ref, act_ref, cum_ref, gr_ref, sin_ref, do_ref, *rest):
        sc_in, rest = rest[:ns], rest[ns:]
        dq_ref, dk_ref, dv_ref, dact_ref, dcum_ref, dgr_ref = rest[:6]
        sc_out, rest = rest[6:6 + ns], rest[6 + ns:]
        ds_scr, g_ref, b_ref, dg_ref, db_ref, send_sems, recv_sems = rest
        step = pl.program_id(0) * n + pl.program_id(1)
        sc_start, sc_finish = _scatter_ops(sc_in, sc_out, (send_sems, recv_sems))
        pl.when(step == 0)(sc_start)
        _zero_at_first([ds_scr], pl.program_id(1) == 0)
        _expand_lanes(act_ref, b_ref, 0, GDN_HEADS, GDN_DK)
        _expand_lanes(cum_ref, g_ref, _G_LO, GDN_HEADS, GDN_DK)
        flat, cots = [], []
        for p in range(_GP):
            la, lb = _pair_lanes(p)
            flat += [*_gdn_args(p, q_ref, k_ref, v_ref, g_ref, b_ref, gr_ref), sin_ref[2 * p], sin_ref[2 * p + 1]]
            cots += [do_ref[:, la], do_ref[:, lb], ds_scr[2 * p], ds_scr[2 * p + 1]]
        _, vjp = jax.vjp(_gdn_multi, *flat)
        grads = vjp(tuple(cots))
        for p in range(_GP):
            la, lb = _pair_lanes(p)
            cts = grads[_GDN_NARGS * p:_GDN_NARGS * (p + 1)]
            for ref, i in ((dq_ref, 0), (dk_ref, 2), (dv_ref, 4), (dg_ref, 6), (db_ref, 8)):
                ref[:, la] = cts[i]
                ref[:, lb] = cts[i + 1]
            dgr_ref[p] = cts[10]
            dg_ref[_LAST, la] += cts[11]
            dg_ref[_LAST, lb] += cts[12]
            ds_scr[2 * p] = cts[13]
            ds_scr[2 * p + 1] = cts[14]
        dact_ref[...] = _reduce_lanes(db_ref, 0, GDN_HEADS, GDN_DK)
        dcum_ref[...] = _reduce_lanes(dg_ref, _G_LO, GDN_HEADS, GDN_DK)
        pl.when(step == bsz * n - 1)(sc_finish)

    tspec = pl.BlockSpec((CHUNK, GDN_V), lambda b, c: (b * n + (n - 1 - c), 0))
    nspec = pl.BlockSpec((CHUNK, SMALL), lambda b, c: (b * n + (n - 1 - c), 0))
    rspec = pl.BlockSpec((_GP, 1, LANE), lambda b, c: (b * n + (n - 1 - c), 0, 0))
    sspec = pl.BlockSpec((GDN_HEADS, LANE, LANE), lambda b, c: (b * n + (n - 1 - c), 0, 0))
    tok_shape, nar_shape = _sds((bsz * n * CHUNK, GDN_V)), _sds((bsz * n * CHUNK, SMALL))
    wide = pltpu.VMEM((CHUNK, GDN_V), F32)
    outs = pl.pallas_call(
        body, name="gdn_bwd", grid=(bsz, n),
        in_specs=[tspec] * 3 + [nspec, nspec, rspec, sspec, tspec] + [_ANY] * ns,
        out_specs=[tspec] * 3 + [nspec, nspec, rspec] + [_ANY] * ns,
        out_shape=[tok_shape] * 3 + [nar_shape, nar_shape, _sds((bsz * n * _GP, 1, LANE))]
        + [_sds((3,) + a.shape[1:], a.dtype) for a in scatter],
        scratch_shapes=[pltpu.VMEM((GDN_HEADS, LANE, LANE), F32), wide, wide, wide, wide] + _scatter_sems(ns),
        compiler_params=pltpu.CompilerParams(dimension_semantics=("arbitrary", "arbitrary"), vmem_limit_bytes=VMEM_LIMIT,
                                             has_side_effects=True))(q, k, v, act, cum, gc_row, s_in, d_o, *scatter)
    return outs[:6], outs[6:]


_B_OFF = SSD_D // LANE
_C_OFF = (SSD_D + SSD_BC) // LANE
_PPG = _SP // SSD_GROUPS


def _ssd_args(p, x_ref, dt_ref, a_ref, ar_ref):
    lp = slice(p * LANE, (p + 1) * LANE)
    gi = p // _PPG
    b_sl = slice((_B_OFF + gi) * LANE, (_B_OFF + gi + 1) * LANE)
    c_sl = slice((_C_OFF + gi) * LANE, (_C_OFF + gi + 1) * LANE)
    return (x_ref[:, lp], dt_ref[:, lp], a_ref[:, lp], ar_ref[p], a_ref[_LAST, lp], x_ref[:, b_sl], x_ref[:, c_sl])


def _ssd_fwd(xbc, dt, acs, acs_row, bsz, n):
    def body(x_ref, dt_ref, a_ref, ar_ref, y_ref, sin_ref, s_scr):
        _zero_at_first([s_scr], pl.program_id(1) == 0)
        flat = []
        for p in range(_SP):
            flat += [*_ssd_args(p, x_ref, dt_ref, a_ref, ar_ref), s_scr[p]]
        sin_ref[...] = s_scr[...]
        outs = _ssd_multi(*flat)
        for p in range(_SP):
            y_ref[:, p * LANE:(p + 1) * LANE], s_scr[p] = outs[2 * p:2 * p + 2]

    tspec = pl.BlockSpec((CHUNK, SSD_D), lambda b, c: (b * n + c, 0))
    return _pcall("ssd_fwd", body, (bsz, n),
                  [pl.BlockSpec((CHUNK, SSD_CONV_CH), lambda b, c: (b * n + c, 0)), tspec, tspec,
                   pl.BlockSpec((_SP, 1, LANE), lambda b, c: (b * n + c, 0, 0))],
                  [tspec, pl.BlockSpec((_SP, LANE, LANE), lambda b, c: (b * n + c, 0, 0))],
                  [_sds((bsz * n * CHUNK, SSD_D)), _sds((bsz * n * _SP, LANE, LANE))],
                  scratch=[pltpu.VMEM((_SP, LANE, LANE), F32)], sem=("parallel", "arbitrary"))(xbc, dt, acs, acs_row)


def _ssd_bwd(xbc, dt, acs, acs_row, s_in, d_y, d_x_skip, bsz, n):
    def body(x_ref, dt_ref, a_ref, ar_ref, sin_ref, dy_ref, dsk_ref, dx_ref, ddt_ref, da_ref, dar_ref, ds_scr):
        _zero_at_first([ds_scr], pl.program_id(1) == 0)
        d_b = [None] * SSD_GROUPS
        d_c = [None] * SSD_GROUPS
        flat, cots = [], []
        for p in range(_SP):
            flat += [*_ssd_args(p, x_ref, dt_ref, a_ref, ar_ref), sin_ref[p]]
            cots += [dy_ref[:, p * LANE:(p + 1) * LANE], ds_scr[p]]
        _, vjp = jax.vjp(_ssd_multi, *flat)
        grads = vjp(tuple(cots))
        for p in range(_SP):
            lp = slice(p * LANE, (p + 1) * LANE)
            gi = p // _PPG
            cts = grads[_SSD_NARGS * p:_SSD_NARGS * (p + 1)]
            dx_ref[:, lp] = cts[0] + dsk_ref[:, lp]
            ddt_ref[:, lp] = cts[1]
            da_ref[:, lp] = cts[2]
            dar_ref[p] = cts[3]
            da_ref[_LAST, lp] += cts[4]
            d_b[gi] = cts[5] if d_b[gi] is None else d_b[gi] + cts[5]
            d_c[gi] = cts[6] if d_c[gi] is None else d_c[gi] + cts[6]
            ds_scr[p] = cts[7]
        for gi in range(SSD_GROUPS):
            dx_ref[:, (_B_OFF + gi) * LANE:(_B_OFF + gi + 1) * LANE] = d_b[gi]
            dx_ref[:, (_C_OFF + gi) * LANE:(_C_OFF + gi + 1) * LANE] = d_c[gi]

    def rev(b, c):
        return b * n + (n - 1 - c)

    tspec = pl.BlockSpec((CHUNK, SSD_D), lambda b, c: (rev(b, c), 0))
    xspec = pl.BlockSpec((CHUNK, SSD_CONV_CH), lambda b, c: (rev(b, c), 0))
    rspec = pl.BlockSpec((_SP, 1, LANE), lambda b, c: (rev(b, c), 0, 0))
    tok_shape = _sds((bsz * n * CHUNK, SSD_D))
    return _pcall("ssd_bwd", body, (bsz, n),
                  [xspec, tspec, tspec, rspec, pl.BlockSpec((_SP, LANE, LANE), lambda b, c: (rev(b, c), 0, 0)), tspec, tspec],
                  [xspec, tspec, tspec, rspec],
                  [_sds((bsz * n * CHUNK, SSD_CONV_CH)), tok_shape, tok_shape, _sds((bsz * n * _SP, 1, LANE))],
                  scratch=[pltpu.VMEM((_SP, LANE, LANE), F32)], sem=("parallel", "arbitrary"))(
                      xbc, dt, acs, acs_row, s_in, d_y, d_x_skip)


def _rep(p, width):
    return jnp.repeat(p.reshape(-1), width).reshape(1, -1)


def _to_rows(narrow, lo, hi, bsz, n):
    heads = hi - lo
    a = narrow[:, lo:hi].reshape(bsz, n, CHUNK, heads)
    return jnp.transpose(a, (0, 1, 3, 2)).reshape(bsz * n * (heads // 2), 1, 2 * CHUNK)


def _from_rows(rows, heads, bsz, n):
    return jnp.transpose(rows.reshape(bsz, n, heads, CHUNK), (0, 1, 3, 2)).reshape(bsz * n * CHUNK, heads)


def _narrow_row(gdn_part, ssd_part):
    return jnp.pad(jnp.concatenate([gdn_part, ssd_part], axis=1), ((0, 0), (_G_LO, SMALL - _S_HI)))


def _local_step(x, tgt, p):
    bsz, s, _ = x.shape
    t = bsz * s
    n = s // CHUNK
    x2 = x.reshape(t, D_MODEL)
    tgt2 = tgt.reshape(t, D_MODEL)
    gate_bias = _narrow_row(p["gdn_dt_bias"], p["ssd_dt_bias"])
    gate_a_log = _narrow_row(p["gdn_a_log"], p["ssd_a_log"])
    d_skip = _rep(p["ssd_d"], SSD_HEADDIM)

    h, (g_in, g_gcw, g_scw, g_fcw) = _rms_fwd_gathering(
        "rms0_fwd", x2, p["pre_mix_norm"], [p["w_in"]] + [p[k] for k in _COL_SHARDED_SMALL], [True, False, False, False])
    p = dict(p, gdn_conv_w=_chips_to_cols(g_gcw), ssd_conv_w=_chips_to_cols(g_scw), ffn_conv_w=_chips_to_cols(g_fcw))
    a0, a1, b0 = _NARROW_A - 2 * _SHARD_W, _NARROW_A + 16 - 2 * _SHARD_W, _NARROW_B - 3 * _SHARD_W
    w_big = jnp.concatenate([g_in[0], g_in[1], g_in[2][:, :a0], g_in[2][:, a1:], g_in[3][:, :b0]], axis=1)
    w_small = jnp.concatenate([g_in[2][:, a0:a1], g_in[3][:, b0:], jnp.zeros((D_MODEL, SMALL - 32), BF16)], axis=1)
    proj, g_out, g_up, g_down = _matmul_nn_gathering(
        "mm_in_big", h, w_big, BF16, (_pick(t, (1024, 512, 256, 128, 64)), BIG // 2), [p["w_out"], p["w_up"], p["w_down"]])
    w_out, w_up, w_down = g_out.reshape(-1, D_MODEL), _chips_to_cols(g_up), g_down.reshape(-1, D_MODEL)
    small = _matmul("mm_in_small", h, w_small, "nn", F32, (1024, 128))
    gact, cum, dt, acs = _gates_fwd(small, gate_bias, gate_a_log)
    gc_row = _to_rows(cum, _G_LO, _G_HI, bsz, n)
    acs_row = _to_rows(cum, _S_LO, _S_HI, bsz, n)
    q = _gdn_conv_fwd("q", proj, p["gdn_conv_w"], bsz, s)
    k = _gdn_conv_fwd("k", proj, p["gdn_conv_w"], bsz, s)
    v = _gdn_conv_fwd("v", proj, p["gdn_conv_w"], bsz, s)
    o, gdn_s = _gdn_fwd(q, k, v, gact, cum, gc_row, bsz, n)
    ocat = _gdn_out_fwd(o, proj, p["gdn_norm_w"])
    xbc = _ssd_conv_fwd(proj, p["ssd_conv_w"], p["ssd_conv_b"], bsz, s)
    y, ssd_s = _ssd_fwd(xbc, dt, acs, acs_row, bsz, n)
    ocat = _ssd_out_fwd(y, xbc, proj, d_skip, p["ssd_norm_w"], ocat)
    mix = _matmul("mm_out", ocat, w_out, "nn", F32, (1024, 1024))
    x1, h2 = _res1_fwd(x2, mix, p["post_mix_norm"], p["pre_ffn_norm"])
    u_pre = _matmul("mm_up", h2, w_up, "nn", BF16, (1024, 2816))
    act = _ffn_act_fwd(u_pre, p["ffn_conv_w"], p["ffn_conv_b"], bsz, s)
    f = _matmul("mm_down", act, w_down, "nn", F32, (1024, 1024))
    loss_acc, d_out, d_f, g_post_ffn = _final(x1, f, p["post_ffn_norm"], tgt2)

    grads = {"post_ffn_norm": g_post_ffn}
    d_act = _matmul("mm_down_dx", d_f, w_down, "nt", F32, (1024, 2816))
    dw_down = _matmul("mm_down_dw", act, d_f, "tn", F32, (2816, 1024, 1024))
    d_gate, d_up, dwg, dwu, dbg, dbu = _ffn_act_bwd(u_pre, p["ffn_conv_w"], p["ffn_conv_b"], d_act, bsz, s)
    grads["ffn_conv_w"] = jnp.concatenate([dwg, dwu], axis=1)
    grads["ffn_conv_b"] = jnp.concatenate([dbg, dbu], axis=1)
    d_h2 = _matmul_nt_split("mm_up_dx", d_gate, d_up, w_up, 512)
    dw_up = _matmul("mm_up_dw_gate", h2, d_gate, "tn", F32, (1024, 2816, 1024), part=(2 * D_FF, 0, None))
    dw_up = _matmul("mm_up_dw_up", h2, d_up, "tn", F32, (1024, 2816, 1024), part=(2 * D_FF, D_FF, dw_up))
    dw_down = dw_down.reshape(N_CHIPS, -1, D_MODEL)
    (d_x1, d_mix, grads["post_mix_norm"], grads["pre_ffn_norm"]), (r_up, r_down) = _res1_bwd(
        x2, mix, p["post_mix_norm"], p["pre_ffn_norm"], d_out, d_h2, [dw_up, dw_down])
    d_ocat = _matmul("mm_out_dx", d_mix, w_out, "nt", F32, (1024, 2048))
    dw_out = _matmul("mm_out_dw", ocat, d_mix, "tn", F32, (2048, 1024, 1024)).reshape(N_CHIPS, -1, D_MODEL)

    (r_out,) = _pair_send_other_half("pair_reduce_send_early", [dw_out])
    early = _pair_adds("early", [dw_out, dw_up, dw_down], [r_out, r_up, r_down])

    d_o, d_proj, grads["gdn_norm_w"] = _gdn_out_bwd(o, proj, p["gdn_norm_w"], d_ocat)
    (d_q, d_k, d_v, d_act_g, d_cum_g, d_gc_row), early_landed = _gdn_bwd(
        q, k, v, gact, cum, gc_row, gdn_s, d_o, bsz, n, [ps[1] for ps in early])
    d_proj, dwq = _gdn_conv_bwd("q", proj, p["gdn_conv_w"], d_q, d_proj, bsz, s)
    d_proj, dwk = _gdn_conv_bwd("k", proj, p["gdn_conv_w"], d_k, d_proj, bsz, s)
    d_proj, dwv = _gdn_conv_bwd("v", proj, p["gdn_conv_w"], d_v, d_proj, bsz, s)
    grads["gdn_conv_w"] = jnp.concatenate([dwq, dwk, dwv], axis=1)

    d_y, d_xs_skip, d_proj, d_dskip, grads["ssd_norm_w"] = _ssd_out_bwd(y, xbc, proj, d_skip, p["ssd_norm_w"], d_ocat, d_proj)
    d_xbc, d_dt, d_acs, d_acs_row = _ssd_bwd(xbc, dt, acs, acs_row, ssd_s, d_y, d_xs_skip, bsz, n)
    d_proj, grads["ssd_conv_w"], grads["ssd_conv_b"] = _ssd_conv_bwd(proj, p["ssd_conv_w"], p["ssd_conv_b"], d_xbc, d_proj, bsz, s)

    d_cum_rows = jnp.concatenate([jnp.zeros((t, _G_LO), F32), _from_rows(d_gc_row, GDN_HEADS, bsz, n),
                                  _from_rows(d_acs_row, SSD_HEADS, bsz, n), jnp.zeros((t, SMALL - _S_HI), F32)], axis=1)
    d_small, d_gate_bias, d_gate_a_log = _gates_bwd(small, gate_bias, gate_a_log, d_act_g, [d_cum_g, d_cum_rows], d_dt, d_acs)
    grads["gdn_dt_bias"], grads["ssd_dt_bias"] = d_gate_bias[:, _G_LO:_G_HI], d_gate_bias[:, _S_LO:_S_HI]
    grads["gdn_a_log"], grads["ssd_a_log"] = d_gate_a_log[:, _G_LO:_G_HI], d_gate_a_log[:, _S_LO:_S_HI]
    dw_big = _matmul("mm_in_big_dw", h, d_proj, "tn", F32, (1024, 3328, 1024))
    dw_small = _matmul("mm_in_small_dw", h, d_small, "tn", F32)
    dw_in = jnp.stack([
        dw_big[:, :_SHARD_W], dw_big[:, _SHARD_W:2 * _SHARD_W],
        jnp.concatenate([dw_big[:, 2 * _SHARD_W:_NARROW_A], dw_small[:, :16], dw_big[:, _NARROW_A:3 * _SHARD_W - 16]], axis=1),
        jnp.concatenate([dw_big[:, 3 * _SHARD_W - 16:], dw_small[:, 16:32]], axis=1)])
    late = _pair_sums("late", [dw_in])
    d_h_big, late_landed = _matmul_nt_scattering("mm_in_big_dx", d_proj, w_big, 512, [ps[1] for ps in late])
    d_h_small = _matmul("mm_in_small_dx", d_small, w_small, "nt", F32, (1024, 1024))
    grad_x, grads["pre_mix_norm"] = _rms1_bwd(x2, p["pre_mix_norm"], d_h_big, d_h_small, d_x1)
    grads["ssd_d"] = _head_sums(d_dskip)[:1, :SSD_HEADS]
    return loss_acc, grad_x.reshape(bsz, s, D_MODEL), grads, (late + early, list(late_landed) + list(early_landed))


def _head_sums(wide):
    def body(x_ref, o_ref):
        r, c = _iota2((D_MODEL, SMALL))
        o_ref[...] = _mask_dot(jnp.broadcast_to(x_ref[...], (8, D_MODEL)), ((r >> 6) == c).astype(F32), NN, True)

    return _pcall("head_sums", body, (1,), [_full_spec((1, D_MODEL))], _full_spec((8, SMALL)), _sds((8, SMALL)))(wide)


def _adamw_fn(w, g, m, v):
    m = ADAM_B1 * m + (1.0 - ADAM_B1) * g
    v = ADAM_B2 * v + (1.0 - ADAM_B2) * (g * g)
    m_hat = m / (1.0 - ADAM_B1 ** ADAM_STEP)
    v_hat = v / (1.0 - ADAM_B2 ** ADAM_STEP)
    delta = -ADAM_LR * (m_hat / (jnp.sqrt(v_hat) + ADAM_EPS) + ADAM_WD * w)
    return delta, m, v


def _adamw(name, w, g, m, v):
    r, c = w.shape
    tr = _pick(r, (256, 176, 128, 64, 8))

    def body(w_ref, g_ref, m_ref, v_ref, d_ref, m2_ref, v2_ref):
        d, m2, v2 = _adamw_fn(w_ref[...], g_ref[...], m_ref[...], v_ref[...])
        d_ref[...] = d
        m2_ref[...] = m2
        v2_ref[...] = v2

    spec = pl.BlockSpec((tr, c), lambda i: (i, 0))
    return _pcall(name, body, (r // tr,), [spec] * 4, [spec] * 3, [_sds((r, c))] * 3, sem=("parallel",))(w, g, m, v)


def _adamw_many(groups):
    n = len(groups[0])
    shapes = [a.shape for a in groups[0]]

    def body(*refs):
        ins, outs = refs[:4 * n], refs[4 * n:]
        for i in range(n):
            d, m2, v2 = _adamw_fn(ins[i][...], ins[n + i][...], ins[2 * n + i][...], ins[3 * n + i][...])
            outs[i][...] = d
            outs[n + i][...] = m2
            outs[2 * n + i][...] = v2

    specs = [_full_spec(s) for s in shapes]
    flat = _pcall("adamw_small", body, (1,), specs * 4, specs * 3, [_sds(s) for s in shapes] * 3)(
        *[a for g in groups for a in g])
    return [flat[:n], flat[n:2 * n], flat[2 * n:]]


_ANY = pl.BlockSpec(memory_space=pl.ANY)
_OTHER_CHIPS = ((1, 0), (0, 1), (1, 1))


def _coords():
    return lax.axis_index("x"), lax.axis_index("y"), lax.axis_index("c")


def _flip(v, f):
    return 1 - v if f else v


def _gather_ops(ins, outs, sems, split):
    send_sems, recv_sems, fwd_send_sems, fwd_recv_sems, own_send_sems, own_recv_sems = sems
    n = len(ins)
    x, y, c = _coords()
    me = 2 * x + y
    sib = (x, y, 1 - c)

    def rows(a, core):
        if not split[a]:
            return slice(None)
        half = ins[a].shape[0] // 2
        return pl.ds(core * half, half)

    def chip(j):
        fx, fy = _OTHER_CHIPS[j]
        return _flip(x, fx), _flip(y, fy)

    def own_cp(a):
        return pltpu.make_async_remote_copy(ins[a], outs[a].at[me], own_send_sems.at[a], own_recv_sems.at[a],
                                            device_id=sib, device_id_type=MESH)

    def ici_cp(a, j):
        return pltpu.make_async_remote_copy(ins[a].at[rows(a, c)], outs[a].at[me, rows(a, c)],
                                            send_sems.at[a * 3 + j], recv_sems.at[a * 3 + j],
                                            device_id=(*chip(j), c), device_id_type=MESH)

    def landed_cp(a, j, sem_a, sem_b, core, to):
        cx, cy = chip(j)
        blk = outs[a].at[2 * cx + cy, rows(a, core)]
        return pltpu.make_async_remote_copy(blk, blk, sem_a.at[a * 3 + j], sem_b.at[a * 3 + j], device_id=to, device_id_type=MESH)

    pairs = [(a, j) for a in range(n) for j in range(3)]

    def start():
        for a in range(n):
            own_cp(a).start()
        for a, j in pairs:
            ici_cp(a, j).start()

    def forward():
        for a, j in pairs:
            landed_cp(a, j, send_sems, recv_sems, c, (*chip(j), c)).wait_recv()
            if split[a]:
                landed_cp(a, j, fwd_send_sems, fwd_recv_sems, c, sib).start()

    def finish():
        for a, j in pairs:
            if split[a]:
                landed_cp(a, j, fwd_send_sems, fwd_recv_sems, 1 - c, sib).wait_recv()
        for a in range(n):
            own_cp(a).wait_recv()
        for a, j in pairs:
            ici_cp(a, j).wait_send()
            if split[a]:
                landed_cp(a, j, fwd_send_sems, fwd_recv_sems, c, sib).wait_send()
        for a in range(n):
            own_cp(a).wait_send()

    return start, forward, finish


def _gather_sems(n):
    return [pltpu.SemaphoreType.DMA((3 * n,))] * 4 + [pltpu.SemaphoreType.DMA((n,))] * 2


def _matmul_nn_gathering(name, a, b, out_dtype, tiles, shards):
    m, k = a.shape
    n = b.shape[1]
    tm, tn = tiles
    nc = _pick(tn, (512, 256, 128))
    ns = len(shards)
    gi, gj = m // tm, n // tn
    steps = gi * gj

    def body(a_ref, b_ref, *rest):
        ins, o_ref, outs, sems = rest[:ns], rest[ns], rest[ns + 1:2 * ns + 1], rest[2 * ns + 1:]
        step = pl.program_id(0) * gj + pl.program_id(1)
        start, forward, finish = _gather_ops(ins, outs, sems, [True] * ns)
        pl.when(step == 0)(start)
        for c0 in range(0, tn, nc):
            o_ref[:, c0:c0 + nc] = _bdot(a_ref[...], b_ref[:, c0:c0 + nc], NN).astype(o_ref.dtype)
        pl.when(step == steps - 2)(forward)
        pl.when(step == steps - 1)(finish)

    assert steps >= 2
    return pl.pallas_call(
        body, name=name, grid=(gi, gj),
        in_specs=[pl.BlockSpec((tm, k), lambda i, j: (i, 0)), pl.BlockSpec((k, tn), lambda i, j: (0, j))] + [_ANY] * ns,
        out_specs=[pl.BlockSpec((tm, tn), lambda i, j: (i, j))] + [_ANY] * ns,
        out_shape=[_sds((m, n), out_dtype)] + [_sds((N_CHIPS,) + s.shape, s.dtype) for s in shards],
        scratch_shapes=_gather_sems(ns),
        compiler_params=pltpu.CompilerParams(dimension_semantics=("arbitrary", "arbitrary"), vmem_limit_bytes=VMEM_LIMIT,
                                             has_side_effects=True))(a, b, *shards)


_PEERS = tuple((fx, fy, fc) for fx in (0, 1) for fy in (0, 1) for fc in (0, 1))[1:]


def _allreduce_small(x):
    r = x.shape[0]

    def body(x_ref, o_ref, buf, send_sems, recv_sems):
        cx, cy, cc = _coords()
        me = 4 * cx + 2 * cy + cc
        sends = []
        for j, (fx, fy, fc) in enumerate(_PEERS):
            cp = pltpu.make_async_remote_copy(x_ref, buf.at[me], send_sems.at[j], recv_sems.at[j],
                                              device_id=(_flip(cx, fx), _flip(cy, fy), _flip(cc, fc)), device_id_type=MESH)
            cp.start()
            sends.append(cp)
        buf[pl.ds(me, 1)] = x_ref[...][None]
        for j, (fx, fy, fc) in enumerate(_PEERS):
            src = 4 * _flip(cx, fx) + 2 * _flip(cy, fy) + _flip(cc, fc)
            pltpu.make_async_remote_copy(x_ref, buf.at[src], send_sems.at[j], recv_sems.at[j],
                                         device_id=(_flip(cx, fx), _flip(cy, fy), _flip(cc, fc)), device_id_type=MESH).wait_recv()
        for cp in sends:
            cp.wait_send()
        acc = buf[0]
        for d in range(1, N_DEV):
            acc = acc + buf[d]
        o_ref[...] = acc

    vm = pl.BlockSpec(memory_space=pltpu.VMEM)
    return pl.pallas_call(
        body, name="allreduce_small", out_shape=_sds((r, LANE)), in_specs=[vm], out_specs=vm,
        scratch_shapes=[pltpu.VMEM((N_DEV, r, LANE), F32), pltpu.SemaphoreType.DMA((7,)), pltpu.SemaphoreType.DMA((7,))],
        compiler_params=pltpu.CompilerParams(has_side_effects=True, vmem_limit_bytes=VMEM_LIMIT))(x)


def _pair_sums(tag, arrs):
    return _pair_adds(tag, arrs, _pair_send_other_half("pair_reduce_send_" + tag, arrs))


def _pair_adds(tag, arrs, received):
    core = lax.axis_index("c").astype(jnp.int32).reshape(1)
    return [_pair_add("pair_add_%s_%d" % (tag, i), a, b, core) for i, (a, b) in enumerate(zip(arrs, received))]


def _pair_send_ops(ins, outs, sems):
    send_sems, recv_sems = sems
    x, y, c = _coords()

    def cp(a):
        half = ins[a].shape[1] // 2
        return pltpu.make_async_remote_copy(ins[a].at[:, pl.ds((1 - c) * half, half), :], outs[a], send_sems.at[a], recv_sems.at[a],
                                            device_id=(x, y, 1 - c), device_id_type=MESH)

    def start():
        for a in range(len(ins)):
            cp(a).start()

    def finish():
        for a in range(len(ins)):
            cp(a).wait_recv()
        for a in range(len(ins)):
            cp(a).wait_send()

    return start, finish


def _pair_send_shapes(arrs):
    return [_sds((a.shape[0], a.shape[1] // 2, a.shape[2]), a.dtype) for a in arrs]


def _pair_send_other_half(name, arrs):
    n = len(arrs)

    def body(*refs):
        start, finish = _pair_send_ops(refs[:n], refs[n:2 * n], refs[2 * n:])
        start()
        finish()

    return pl.pallas_call(
        body, name=name, out_shape=_pair_send_shapes(arrs), in_specs=[_ANY] * n, out_specs=[_ANY] * n,
        scratch_shapes=[pltpu.SemaphoreType.DMA((n,))] * 2, compiler_params=pltpu.CompilerParams(has_side_effects=True))(*arrs)


def _pair_fill(arrs):
    n = len(arrs)

    def body(*refs):
        bufs = refs[n:2 * n]
        send_sems, recv_sems = refs[2 * n:]
        x, y, c = _coords()
        sends = []
        for a in range(n):
            cp = pltpu.make_async_remote_copy(bufs[a].at[c], bufs[a].at[c], send_sems.at[a], recv_sems.at[a],
                                              device_id=(x, y, 1 - c), device_id_type=MESH)
            cp.start()
            sends.append(cp)
        for a in range(n):
            theirs = bufs[a].at[1 - c]
            pltpu.make_async_remote_copy(theirs, theirs, send_sems.at[a], recv_sems.at[a],
                                         device_id=(x, y, 1 - c), device_id_type=MESH).wait_recv()
        for cp in sends:
            cp.wait_send()

    return pl.pallas_call(
        body, name="pair_gather", out_shape=[_sds(a.shape, a.dtype) for a in arrs], in_specs=[_ANY] * n, out_specs=[_ANY] * n,
        scratch_shapes=[pltpu.SemaphoreType.DMA((n,))] * 2, input_output_aliases={a: a for a in range(n)},
        compiler_params=pltpu.CompilerParams(has_side_effects=True))(*arrs)


def _scatter_ops(ins, outs, sems):
    send_sems, recv_sems = sems
    x, y, c = _coords()

    def cp(a, j):
        fx, fy = _OTHER_CHIPS[j]
        to = 2 * _flip(x, fx) + _flip(y, fy)
        return pltpu.make_async_remote_copy(ins[a].at[to], outs[a].at[j], send_sems.at[a * 3 + j], recv_sems.at[a * 3 + j],
                                            device_id=(_flip(x, fx), _flip(y, fy), c), device_id_type=MESH)

    pairs = [(a, j) for a in range(len(ins)) for j in range(3)]

    def start():
        for a, j in pairs:
            cp(a, j).start()

    def finish():
        for a, j in pairs:
            cp(a, j).wait_recv()
        for a, j in pairs:
            cp(a, j).wait_send()

    return start, finish


def _scatter_sems(n):
    return [pltpu.SemaphoreType.DMA((3 * n,))] * 2


def _pair_add(name, full, recv, core):
    _, r, c = full.shape
    half = r // 2
    tr = _pick(half, (256, 176, 128, 64, 8))
    nb = half // tr

    def body(c_ref, a_ref, b_ref, o_ref, ob_ref):
        s = a_ref[...] + b_ref[...]
        o_ref[...] = s
        ob_ref[...] = s.astype(BF16)

    blk = pl.BlockSpec((1, tr, c), lambda k, i, cref: (k, i, 0))
    grid_spec = pltpu.PrefetchScalarGridSpec(
        num_scalar_prefetch=1, grid=(N_CHIPS, nb),
        in_specs=[pl.BlockSpec((1, tr, c), lambda k, i, cref: (k, cref[0] * nb + i, 0)), blk], out_specs=[blk, blk])
    return pl.pallas_call(
        body, name=name, out_shape=[_sds((N_CHIPS, half, c)), _sds((N_CHIPS, half, c), BF16)], grid_spec=grid_spec,
        compiler_params=pltpu.CompilerParams(dimension_semantics=("parallel", "parallel"), vmem_limit_bytes=VMEM_LIMIT))(
            core, full, recv)


def _chip_sum(name, landed, own, where):
    _, r, c = landed.shape
    tr = _pick(r, (256, 176, 128, 64, 16))

    def body(w_ref, l_ref, o_ref, s_ref):
        s_ref[0] = ((o_ref[0] + l_ref[0].astype(F32)) + l_ref[1].astype(F32)) + l_ref[2].astype(F32)

    grid_spec = pltpu.PrefetchScalarGridSpec(
        num_scalar_prefetch=1, grid=(r // tr,),
        in_specs=[pl.BlockSpec((3, tr, c), lambda i, wref: (0, i, 0)),
                  pl.BlockSpec((1, tr, c), lambda i, wref: (wref[0], i, 0))],
        out_specs=pl.BlockSpec((1, tr, c), lambda i, wref: (wref[1], i, 0)))
    return pl.pallas_call(
        body, name=name, out_shape=_sds((2, r, c)), grid_spec=grid_spec,
        compiler_params=pltpu.CompilerParams(dimension_semantics=("parallel",), vmem_limit_bytes=VMEM_LIMIT))(where, landed, own)


_WEIGHTS = ("pre_mix_norm", "w_in", "gdn_conv_w", "gdn_a_log", "gdn_dt_bias", "gdn_norm_w", "ssd_conv_w", "ssd_conv_b",
            "ssd_a_log", "ssd_dt_bias", "ssd_d", "ssd_norm_w", "w_out", "post_mix_norm", "pre_ffn_norm", "w_up",
            "ffn_conv_w", "ffn_conv_b", "w_down", "post_ffn_norm")
_BIG = ("w_in", "w_out", "w_up", "w_down")
_COL_SHARDED_SMALL = ("gdn_conv_w", "ssd_conv_w", "ffn_conv_w")
_SMALL = tuple(k for k in _WEIGHTS if k not in _BIG)


def _pack(arrs):
    flat = jnp.concatenate([a.reshape(-1) for a in arrs])
    rows = -(-flat.shape[0] // (8 * LANE)) * 8
    return jnp.pad(flat, (0, rows * LANE - flat.shape[0])).reshape(rows, LANE)


def _unpack(packed, shapes):
    flat = packed.reshape(-1)
    out, off = [], 0
    for shp in shapes:
        size = 1
        for d in shp:
            size *= d
        out.append(flat[off:off + size].reshape(shp))
        off += size
    return out


def _chips_to_cols(a):
    k, r, c = a.shape
    return jnp.transpose(a, (1, 0, 2)).reshape(r, k * c)


def kernel(x, pre_mix_norm, w_in, gdn_conv_w, gdn_a_log, gdn_dt_bias, gdn_norm_w, ssd_conv_w, ssd_conv_b, ssd_a_log, ssd_dt_bias, ssd_d, ssd_norm_w, w_out, post_mix_norm, pre_ffn_norm, w_up, ffn_conv_w, ffn_conv_b, w_down, post_ffn_norm, loss_target, m_pre_mix_norm, m_w_in, m_gdn_conv_w, m_gdn_a_log, m_gdn_dt_bias, m_gdn_norm_w, m_ssd_conv_w, m_ssd_conv_b, m_ssd_a_log, m_ssd_dt_bias, m_ssd_d, m_ssd_norm_w, m_w_out, m_post_mix_norm, m_pre_ffn_norm, m_w_up, m_ffn_conv_w, m_ffn_conv_b, m_w_down, m_post_ffn_norm, v_pre_mix_norm, v_w_in, v_gdn_conv_w, v_gdn_a_log, v_gdn_dt_bias, v_gdn_norm_w, v_ssd_conv_w, v_ssd_conv_b, v_ssd_a_log, v_ssd_dt_bias, v_ssd_d, v_ssd_norm_w, v_w_out, v_post_mix_norm, v_pre_ffn_norm, v_w_up, v_ffn_conv_w, v_ffn_conv_b, v_w_down, v_post_ffn_norm):
    w = dict(zip(_WEIGHTS, (pre_mix_norm, w_in, gdn_conv_w, gdn_a_log, gdn_dt_bias, gdn_norm_w, ssd_conv_w, ssd_conv_b,
                            ssd_a_log, ssd_dt_bias, ssd_d, ssd_norm_w, w_out, post_mix_norm, pre_ffn_norm, w_up,
                            ffn_conv_w, ffn_conv_b, w_down, post_ffn_norm)))
    m = dict(zip(_WEIGHTS, (m_pre_mix_norm, m_w_in, m_gdn_conv_w, m_gdn_a_log, m_gdn_dt_bias, m_gdn_norm_w, m_ssd_conv_w,
                            m_ssd_conv_b, m_ssd_a_log, m_ssd_dt_bias, m_ssd_d, m_ssd_norm_w, m_w_out, m_post_mix_norm,
                            m_pre_ffn_norm, m_w_up, m_ffn_conv_w, m_ffn_conv_b, m_w_down, m_post_ffn_norm)))
    v = dict(zip(_WEIGHTS, (v_pre_mix_norm, v_w_in, v_gdn_conv_w, v_gdn_a_log, v_gdn_dt_bias, v_gdn_norm_w, v_ssd_conv_w,
                            v_ssd_conv_b, v_ssd_a_log, v_ssd_dt_bias, v_ssd_d, v_ssd_norm_w, v_w_out, v_post_mix_norm,
                            v_pre_ffn_norm, v_w_up, v_ffn_conv_w, v_ffn_conv_b, v_w_down, v_post_ffn_norm)))
    cx, cy, cc = _coords()
    chip = 2 * cx + cy

    p = {k: w[k] for k in _SMALL if k not in _COL_SHARDED_SMALL}
    for k in _BIG:
        p[k] = w[k][0].astype(BF16)
    for k in _COL_SHARDED_SMALL:
        p[k] = w[k][0]
    loss_acc, grad_x, grads, (pair_sum_list, landed_list) = _local_step(x, loss_target, p)

    small_full_shapes = [grads[k].shape for k in _SMALL]
    summed = _unpack(_allreduce_small(_pack([grads[k] for k in _SMALL] + [loss_acc[:1, :1]])), small_full_shapes + [(1, 1)])
    loss = summed[-1].reshape(())
    g_small = dict(zip(_SMALL, summed[:-1]))
    for k in _COL_SHARDED_SMALL:
        width = w[k].shape[2]
        g_small[k] = lax.dynamic_slice_in_dim(g_small[k], chip * width, width, axis=1)

    pair_sums = dict(zip(_BIG, pair_sum_list))
    landed = dict(zip(_BIG, landed_list))
    where = jnp.stack([chip, cc]).astype(jnp.int32)
    mine = [_chip_sum("chip_sum_" + k, landed[k], pair_sums[k][0], where) for k in _BIG]
    both = _pair_fill(mine)
    g_big = {k: a.reshape(-1, a.shape[2]) for k, a in zip(_BIG, both)}

    out_g, out_d, out_m, out_v = {}, {}, {}, {}
    for k in _BIG:
        out_g[k] = g_big[k][None]
        d_, m_, v_ = _adamw("adamw_" + k, w[k][0], g_big[k], m[k][0], v[k][0])
        out_d[k], out_m[k], out_v[k] = d_[None], m_[None], v_[None]
    for k in _SMALL:
        out_g[k] = g_small[k].reshape(w[k].shape)
    two_d = {k: (w[k].shape[-2], w[k].shape[-1]) for k in _SMALL}
    small_out = _adamw_many([[d[k].reshape(two_d[k]) for k in _SMALL] for d in (w, out_g, m, v)])
    for dst, outs in zip((out_d, out_m, out_v), small_out):
        dst.update({k: o.reshape(w[k].shape) for k, o in zip(_SMALL, outs)})
    return (loss, grad_x, *[out_g[k] for k in _WEIGHTS], *[out_d[k] for k in _WEIGHTS],
            *[out_m[k] for k in _WEIGHTS], *[out_v[k] for k in _WEIGHTS])
```

```python
import functools

import jax
import jax.numpy as jnp
from jax import lax
from jax.experimental import pallas as pl
from jax.experimental.pallas import tpu as pltpu

F32 = jnp.float32
BF16 = jnp.bfloat16

D_MODEL = 1024
GDN_HEADS = 8
GDN_DK = 128
SSD_HEADS = 16
SSD_HEADDIM = 64
SSD_GROUPS = 2
SSD_STATE = 128
CONV_K = 4
CHUNK = 64
D_FF = 2816
FFN_CONV_K = 3
EPS = 1e-6
GDN_QK = GDN_HEADS * GDN_DK
GDN_V = GDN_QK
SSD_D = SSD_HEADS * SSD_HEADDIM
SSD_BC = SSD_GROUPS * SSD_STATE
SSD_CONV_CH = SSD_D + 2 * SSD_BC
BIG = 4 * 1024 + 1024 + SSD_CONV_CH
SMALL = 128
D_IN_PROJ = 6688
_SHARD_W = D_IN_PROJ // 4
_NARROW_A = 4096
_NARROW_B = 6672
LANE = 128
PAIR = 2 * CHUNK
NEG = -1e30
VMEM_LIMIT = 56 * 1024 * 1024

ADAM_LR = 0.001
ADAM_B1 = 0.9
ADAM_B2 = 0.999
ADAM_EPS = 1e-08
ADAM_WD = 0.01
ADAM_STEP = 10

N_CHIPS = 4
N_DEV = 8
MESH = pl.DeviceIdType.MESH

NN = ((1,), (0,))
NT = ((1,), (1,))
TN = ((0,), (0,))


def _bdot(a, b, dims):
    return lax.dot_general(a.astype(BF16), b.astype(BF16), (dims, ((), ())), preferred_element_type=F32)


def _split3(a):
    hi = a.astype(BF16)
    r1 = a - hi.astype(F32)
    mid = r1.astype(BF16)
    return hi, mid, (r1 - mid.astype(F32)).astype(BF16)


@jax.custom_vjp
def _nn(a, b):
    return _bdot(a, b, NN)


@jax.custom_vjp
def _nt(a, b):
    return _bdot(a, b, NT)


@jax.custom_vjp
def _tn(a, b):
    return _bdot(a, b, TN)


_nn.defvjp(lambda a, b: (_nn(a, b), (a, b)), lambda r, g: (_nt(g, r[1]), _tn(r[0], g)))
_nt.defvjp(lambda a, b: (_nt(a, b), (a, b)), lambda r, g: (_nn(g, r[1]), _tn(g, r[0])))
_tn.defvjp(lambda a, b: (_tn(a, b), (a, b)), lambda r, g: (_nt(r[1], g), _nn(r[0], g)))


def _mask_dot(x, mask, dims, x_first):
    acc = None
    for piece in _split3(x):
        term = _bdot(piece, mask, dims) if x_first else _bdot(mask, piece, dims)
        acc = term if acc is None else acc + term
    return acc


@jax.custom_vjp
def _cst_left(cst, x):
    return _mask_dot(x, cst, NN, False)


_cst_left.defvjp(lambda cst, x: (_cst_left(cst, x), cst), lambda cst, g: (jnp.zeros_like(cst), _mask_dot(g, cst, TN, False)))


@jax.custom_vjp
def _cst_right(x, cst):
    return _mask_dot(x, cst, NN, True)


_cst_right.defvjp(lambda x, cst: (_cst_right(x, cst), cst), lambda cst, g: (_mask_dot(g, cst, NT, True), jnp.zeros_like(cst)))


def _lin_left(cst):
    return functools.partial(_cst_left, cst)


def _lin_right(cst):
    return lambda x: _cst_right(x, cst)


@jax.custom_vjp
def _tri_inv_m1(a):
    pm = [-x for x in a]
    ap = list(a)
    for _ in range(5):
        ap = [_bdot(x, x, NN) for x in ap]
        pm = [(p + x) + _bdot(p, x, NN) for p, x in zip(pm, ap)]
    return pm


def _tri_inv_m1_bwd(pm, g):
    t = [gi + _bdot(p, gi, TN) for p, gi in zip(pm, g)]
    return ([-(ti + _bdot(ti, p, NT)) for p, ti in zip(pm, t)],)


_tri_inv_m1.defvjp(lambda a: (lambda pm: (pm, pm))(_tri_inv_m1(a)), _tri_inv_m1_bwd)


@jax.custom_vjp
def _top(x):
    return x[: x.shape[0] // 2]


_top.defvjp(lambda x: (_top(x), None), lambda _, g: (jnp.concatenate([g, jnp.zeros_like(g)], axis=0),))


@jax.custom_vjp
def _bot(x):
    return x[x.shape[0] // 2:]


_bot.defvjp(lambda x: (_bot(x), None), lambda _, g: (jnp.concatenate([jnp.zeros_like(g), g], axis=0),))


@jax.custom_vjp
def _vstack(a, b):
    return jnp.concatenate([a, b], axis=0)


_vstack.defvjp(lambda a, b: (_vstack(a, b), None), lambda _, g: (g[: g.shape[0] // 2], g[g.shape[0] // 2:]))


def _shift_dn_raw(x, s):
    if s == 0:
        return x
    r = pltpu.roll(x, s, axis=0)
    ri = lax.broadcasted_iota(jnp.int32, x.shape, 0)
    return jnp.where(ri >= s, r, 0.0)


def _shift_up_raw(x, s):
    if s == 0:
        return x
    n = x.shape[0]
    r = pltpu.roll(x, n - s, axis=0)
    ri = lax.broadcasted_iota(jnp.int32, x.shape, 0)
    return jnp.where(ri < n - s, r, 0.0)


@functools.partial(jax.custom_vjp, nondiff_argnums=(1,))
def _shift_dn(x, s):
    return _shift_dn_raw(x, s)


_shift_dn.defvjp(lambda x, s: (_shift_dn_raw(x, s), None), lambda s, _, g: (_shift_up_raw(g, s),))


def _conv(x, wrows):
    k_w = len(wrows)
    acc = wrows[k_w - 1] * x
    for k in range(k_w - 1):
        acc = acc + wrows[k] * _shift_dn(x, k_w - 1 - k)
    return acc


def _silu(x):
    return x * jax.nn.sigmoid(x)


def _rms(x, w):
    return x * lax.rsqrt(jnp.mean(x * x, axis=-1, keepdims=True) + EPS) * w


def _l2n(x):
    return x * lax.rsqrt(jnp.sum(x * x, axis=-1, keepdims=True) + EPS)


def _iota2(shape):
    return lax.broadcasted_iota(jnp.int32, shape, 0), lax.broadcasted_iota(jnp.int32, shape, 1)


_GDN_NARGS = 15
_SSD_NARGS = 8


def _gdn_multi(*flat):
    pairs = [flat[i:i + _GDN_NARGS] for i in range(0, len(flat), _GDN_NARGS)]
    idx = range(len(pairs))
    ri, ci = _iota2((PAIR, PAIR))
    blk = ((ri >= CHUNK) & (ci >= CHUNK)) | ((ri < CHUNK) & (ci < CHUNK))
    causal = blk & (ri >= ci)
    strict = blk & (ri > ci)
    q = [_vstack(p[0], p[1]) for p in pairs]
    k = [_vstack(p[2], p[3]) for p in pairs]
    v = [_vstack(p[4], p[5]) for p in pairs]
    gc = [_vstack(p[6], p[7]) for p in pairs]
    beta = [_vstack(p[8], p[9]) for p in pairs]
    glast = [_vstack(jnp.broadcast_to(p[11], (CHUNK, LANE)), jnp.broadcast_to(p[12], (CHUNK, LANE))) for p in pairs]
    sa = [p[13] for p in pairs]
    sb = [p[14] for p in pairs]
    decay = [jnp.exp(jnp.where(causal, gc[i] - jnp.broadcast_to(pairs[i][10], (PAIR, PAIR)), NEG)) for i in idx]
    eg = [jnp.exp(x) for x in gc]
    kbeta = [k[i] * beta[i] for i in idx]
    pm = _tri_inv_m1([jnp.where(strict, _nt(kbeta[i], k[i]) * decay[i], 0.0) for i in idx])
    qk = [_nt(q[i], k[i]) * decay[i] for i in idx]
    rhs_v = [v[i] * beta[i] for i in idx]
    rhs_k = [kbeta[i] * eg[i] for i in idx]
    u = [rhs_v[i] + _nn(pm[i], rhs_v[i]) for i in idx]
    w = [rhs_k[i] + _nn(pm[i], rhs_k[i]) for i in idx]
    q_dec = [q[i] * eg[i] for i in idx]
    k_dec = [k[i] * jnp.exp(glast[i] - gc[i]) for i in idx]
    gl = [jnp.exp(x) for x in glast]
    w_s = [_vstack(_nn(_top(w[i]), sa[i]), _nn(_bot(w[i]), sb[i])) for i in idx]
    q_s = [_vstack(_nn(_top(q_dec[i]), sa[i]), _nn(_bot(q_dec[i]), sb[i])) for i in idx]
    v_new = [u[i] - w_s[i] for i in idx]
    o = [q_s[i] + _nn(qk[i], v_new[i]) for i in idx]
    sa2 = [sa[i] * _vstack(_top(gl[i]), _top(gl[i])) + _tn(_top(k_dec[i]), _top(v_new[i])) for i in idx]
    sb2 = [sb[i] * _vstack(_bot(gl[i]), _bot(gl[i])) + _tn(_bot(k_dec[i]), _bot(v_new[i])) for i in idx]
    out = []
    for i in idx:
        out += [_top(o[i]), _bot(o[i]), sa2[i], sb2[i]]
    return tuple(out)


def _ssd_multi(*flat):
    pairs = [flat[i:i + _SSD_NARGS] for i in range(0, len(flat), _SSD_NARGS)]
    idx = range(len(pairs))
    ri, ci = _iota2((CHUNK, PAIR))
    causal = ri >= jnp.where(ci >= CHUNK, ci - CHUNK, ci)
    xdt = [p[0] * p[1] for p in pairs]
    acs = [p[2] for p in pairs]
    alast = [jnp.broadcast_to(p[4], (CHUNK, PAIR)) for p in pairs]
    lmat = [jnp.exp(jnp.where(causal, acs[i] - jnp.broadcast_to(pairs[i][3], (CHUNK, PAIR)), NEG)) for i in idx]
    cb2 = [_nt(p[6], _vstack(p[5], p[5])) for p in pairs]
    xblk = [_vstack(jnp.where(ci < CHUNK, x, 0.0), jnp.where(ci >= CHUNK, x, 0.0)) for x in xdt]
    y_off = [_nn(pairs[i][6], pairs[i][7]) * jnp.exp(acs[i]) for i in idx]
    y = [_nn(cb2[i] * lmat[i], xblk[i]) + y_off[i] for i in idx]
    el = [jnp.exp(x) for x in alast]
    st2 = [pairs[i][7] * _vstack(el[i], el[i]) + _tn(pairs[i][5], xdt[i] * jnp.exp(alast[i] - acs[i])) for i in idx]
    out = []
    for i in idx:
        out += [y[i], st2[i]]
    return tuple(out)


def _pcall(name, body, grid, in_specs, out_specs, out_shape, scratch=(), sem=None, aliases=None):
    if sem is None:
        sem = ("arbitrary",) * len(grid)
    return pl.pallas_call(
        functools.partial(body),
        out_shape=out_shape,
        grid=grid,
        in_specs=in_specs,
        out_specs=out_specs,
        scratch_shapes=scratch,
        input_output_aliases=aliases or {},
        name=name,
        compiler_params=pltpu.CompilerParams(dimension_semantics=sem, vmem_limit_bytes=VMEM_LIMIT),
    )


def _sds(shape, dtype=F32):
    return jax.ShapeDtypeStruct(shape, dtype)


def _row_spec(tm, width, colblock=0):
    return pl.BlockSpec((tm, width), lambda i, _c=colblock: (i, _c))


def _full_spec(shape):
    nd = len(shape)
    return pl.BlockSpec(shape, lambda *_: (0,) * nd)


def _zero_at_first(refs, first):
    @pl.when(first)
    def _():
        for r in refs:
            r[...] = jnp.zeros(r.shape, r.dtype)


def _pick(n, prefs):
    for p in prefs:
        if n % p == 0:
            return p
    return n


def _matmul(name, a, b, mode, out_dtype, tiles=None, part=None):
    def want(i, dim):
        return [tiles[i]] if tiles is not None and dim % tiles[i] == 0 else []

    if mode == "tn":
        r, m = a.shape
        n = b.shape[1]
        tm = _pick(m, want(0, m) + [1024, 1408])
        tn = _pick(n, want(1, n) + [512, 256, 128])
        tk = _pick(r, want(2, r) + [1024, 512, 256, 128, 64])
        in_specs = [pl.BlockSpec((tk, tm), lambda i, j, k: (k, i)), pl.BlockSpec((tk, tn), lambda i, j, k: (k, j))]
        sem = ("parallel", "parallel", "arbitrary")
        if part is None:
            nc = _pick(tn, (512, 256, 128))

            def body(a_ref, b_ref, o_ref):
                _zero_at_first([o_ref], pl.program_id(2) == 0)
                for c0 in range(0, tn, nc):
                    o_ref[:, c0:c0 + nc] += _bdot(a_ref[...], b_ref[:, c0:c0 + nc], TN)

            return _pcall(name, body, (m // tm, n // tn, r // tk), in_specs, pl.BlockSpec((tm, tn), lambda i, j, k: (i, j)),
                          _sds((m, n), out_dtype), sem=sem)(a, b)
        n_total, col_off, into = part
        width = n_total // N_CHIPS
        per_step = tn // width
        off = col_off // tn
        assert tn % width == 0 and col_off % tn == 0

        def body(a_ref, b_ref, *rest):
            o_ref = rest[-1]
            _zero_at_first([o_ref], pl.program_id(2) == 0)
            for q in range(per_step):
                o_ref[q] += _bdot(a_ref[...], b_ref[:, q * width:(q + 1) * width], TN)

        args = (a, b) if into is None else (a, b, into)
        return _pcall(
            name, body, (m // tm, n // tn, r // tk), in_specs + ([] if into is None else [_ANY]),
            pl.BlockSpec((per_step, tm, width), lambda i, j, k: (j + off, i, 0)), _sds((N_CHIPS, m, width), out_dtype),
            sem=sem, aliases=None if into is None else {2: 0})(*args)
    m, k = a.shape
    n = b.shape[1] if mode == "nn" else b.shape[0]
    tm = _pick(m, want(0, m) + ([1024, 512, 256, 128, 64] if k <= 2816 else [512, 256, 128, 64]))
    tn = _pick(n, want(1, n) + [512, 256, 128])
    dims = NN if mode == "nn" else NT

    nc = _pick(tn, (512, 256, 128))

    def body(a_ref, b_ref, o_ref):
        for c0 in range(0, tn, nc):
            b_blk = b_ref[:, c0:c0 + nc] if mode == "nn" else b_ref[c0:c0 + nc, :]
            o_ref[:, c0:c0 + nc] = _bdot(a_ref[...], b_blk, dims).astype(o_ref.dtype)

    b_spec = pl.BlockSpec((k, tn), lambda i, j: (0, j)) if mode == "nn" else pl.BlockSpec((tn, k), lambda i, j: (j, 0))
    return _pcall(
        name, body, (m // tm, n // tn), [pl.BlockSpec((tm, k), lambda i, j: (i, 0)), b_spec],
        pl.BlockSpec((tm, tn), lambda i, j: (i, j)), _sds((m, n), out_dtype), sem=("parallel", "parallel"))(a, b)


def _matmul_nt_scattering(name, a, b, tm_pref, scatter):
    m, k = a.shape
    n = b.shape[0]
    tm = _pick(m, (tm_pref, 256, 128, 64))
    nc = _pick(n, (512, 256, 128))
    ns = len(scatter)
    steps = m // tm

    def body(a_ref, b_ref, *rest):
        sc_in, o_ref, sc_out, sems = rest[:ns], rest[ns], rest[ns + 1:2 * ns + 1], rest[2 * ns + 1:]
        sc_start, sc_finish = _scatter_ops(sc_in, sc_out, sems)
        pl.when(pl.program_id(0) == 0)(sc_start)
        for c0 in range(0, n, nc):
            o_ref[:, c0:c0 + nc] = _bdot(a_ref[...], b_ref[c0:c0 + nc, :], NT)
        pl.when(pl.program_id(0) == steps - 1)(sc_finish)

    outs = pl.pallas_call(
        body, name=name, grid=(steps,),
        in_specs=[pl.BlockSpec((tm, k), lambda i: (i, 0)), _full_spec(b.shape)] + [_ANY] * ns,
        out_specs=[pl.BlockSpec((tm, n), lambda i: (i, 0))] + [_ANY] * ns,
        out_shape=[_sds((m, n))] + [_sds((3,) + s.shape[1:], s.dtype) for s in scatter],
        scratch_shapes=_scatter_sems(ns),
        compiler_params=pltpu.CompilerParams(dimension_semantics=("arbitrary",), vmem_limit_bytes=VMEM_LIMIT,
                                             has_side_effects=True))(a, b, *scatter)
    return outs[0], outs[1:]


def _matmul_nt_split(name, a1, a2, b, tm_pref):
    m, kh = a1.shape
    n = b.shape[0]
    tm = _pick(m, (tm_pref, 512, 256, 128, 64))
    nc = _pick(n, (512, 256, 128))

    def body(a1_ref, a2_ref, b_ref, o_ref):
        for c0 in range(0, n, nc):
            o_ref[:, c0:c0 + nc] = (_bdot(a1_ref[...], b_ref[c0:c0 + nc, :kh], NT)
                                    + _bdot(a2_ref[...], b_ref[c0:c0 + nc, kh:], NT))

    aspec = pl.BlockSpec((tm, kh), lambda i: (i, 0))
    return _pcall(name, body, (m // tm,), [aspec, aspec, _full_spec(b.shape)], pl.BlockSpec((tm, n), lambda i: (i, 0)),
                  _sds((m, n)), sem=("parallel",))(a1, a2, b)


def _row_tile(t):
    return _pick(t, (512, 256, 128, 64))


def _rms_fwd_gathering(name, x, g, shards, split):
    t = x.shape[0]
    tm = next(c for c in (128, 64, 32, 16) if t % c == 0 and t // c >= 2)
    steps = t // tm
    ns = len(shards)

    def body(x_ref, g_ref, *rest):
        ins, h_ref, outs, sems = rest[:ns], rest[ns], rest[ns + 1:2 * ns + 1], rest[2 * ns + 1:]
        start, forward, finish = _gather_ops(ins, outs, sems, split)
        pl.when(pl.program_id(0) == 0)(start)
        h_ref[...] = _rms(x_ref[...], g_ref[...]).astype(BF16)
        pl.when(pl.program_id(0) == steps - 2)(forward)
        pl.when(pl.program_id(0) == steps - 1)(finish)

    assert steps >= 2
    outs = pl.pallas_call(
        body, name=name, grid=(steps,),
        in_specs=[_row_spec(tm, D_MODEL), _full_spec((1, D_MODEL))] + [_ANY] * ns,
        out_specs=[_row_spec(tm, D_MODEL)] + [_ANY] * ns,
        out_shape=[_sds((t, D_MODEL), BF16)] + [_sds((N_CHIPS,) + s.shape, s.dtype) for s in shards],
        scratch_shapes=_gather_sems(ns),
        compiler_params=pltpu.CompilerParams(dimension_semantics=("arbitrary",), vmem_limit_bytes=VMEM_LIMIT,
                                             has_side_effects=True))(x, g, *shards)
    return outs[0], outs[1:]


_G_LO, _G_HI = GDN_HEADS, 2 * GDN_HEADS
_S_LO, _S_HI = 2 * GDN_HEADS, 2 * GDN_HEADS + SSD_HEADS


def _gates_fn(small, bias, a_log):
    tm = small.shape[0]
    rr, cc = _iota2((tm, tm))
    in_chunk_tril = (((rr >> 6) == (cc >> 6)) & (rr >= cc)).astype(F32)
    lane = lax.broadcasted_iota(jnp.int32, small.shape, 1)
    sp = jax.nn.softplus(small + bias)
    act = jnp.where(lane < _G_LO, jax.nn.sigmoid(small), sp)
    cum = _lin_left(in_chunk_tril)(-jnp.exp(a_log) * sp)
    r, c = _iota2((SMALL, D_MODEL))
    to_ssd_lanes = _lin_right((r == _S_LO + (c >> 6)).astype(F32))
    return act, cum, to_ssd_lanes(act), to_ssd_lanes(cum)


def _expand_lanes(src_ref, dst_ref, lo, heads, width):
    rows = src_ref.shape[0]
    for h in range(heads):
        dst_ref[:, h * width:(h + 1) * width] = jnp.broadcast_to(src_ref[:, lo + h:lo + h + 1], (rows, width))


def _reduce_lanes(wide_ref, lo, heads, width):
    rows = wide_ref.shape[0]
    lane = lax.broadcasted_iota(jnp.int32, (rows, SMALL), 1)
    acc = jnp.zeros((rows, SMALL), F32)
    for h in range(heads):
        col = jnp.sum(wide_ref[:, h * width:(h + 1) * width], axis=-1, keepdims=True)
        acc = jnp.where(lane == lo + h, jnp.broadcast_to(col, (rows, SMALL)), acc)
    return acc


def _gates_fwd(small, bias, a_log):
    t = small.shape[0]
    tm = _row_tile(t)

    def body(s_ref, p0, p1, act_ref, cum_ref, dt_ref, acs_ref):
        act_ref[...], cum_ref[...], dt_ref[...], acs_ref[...] = _gates_fn(s_ref[...], p0[...], p1[...])

    pspec, nspec, wspec = _full_spec((1, SMALL)), _row_spec(tm, SMALL), _row_spec(tm, D_MODEL)
    return _pcall("gates_fwd", body, (t // tm,), [nspec, pspec, pspec], [nspec, nspec, wspec, wspec],
                  [_sds((t, SMALL))] * 2 + [_sds((t, D_MODEL))] * 2, sem=("parallel",))(small, bias, a_log)


def _gates_bwd(small, bias, a_log, d_act, d_cums, d_dt, d_acs):
    t = small.shape[0]
    tm = _row_tile(t)
    nc = len(d_cums)

    def body(*refs):
        s_ref, p0, p1, dact_ref = refs[:4]
        dcum_refs = refs[4:4 + nc]
        ddt_ref, dacs_ref, ds_ref, db_ref, da_ref = refs[4 + nc:]
        _zero_at_first([db_ref, da_ref], pl.program_id(0) == 0)
        _, vjp = jax.vjp(_gates_fn, s_ref[...], p0[...], p1[...])
        d_cum = dcum_refs[0][...]
        for c in dcum_refs[1:]:
            d_cum = d_cum + c[...]
        d_s, d_b, d_a = vjp((dact_ref[...], d_cum, ddt_ref[...], dacs_ref[...]))
        ds_ref[...] = d_s.astype(BF16)
        db_ref[...] += d_b
        da_ref[...] += d_a

    pspec, nspec, wspec = _full_spec((1, SMALL)), _row_spec(tm, SMALL), _row_spec(tm, D_MODEL)
    return _pcall("gates_bwd", body, (t // tm,), [nspec, pspec, pspec] + [nspec] * (1 + nc) + [wspec, wspec],
                  [nspec, pspec, pspec], [_sds((t, SMALL), BF16), _sds((1, SMALL)), _sds((1, SMALL))])(
                      small, bias, a_log, d_act, *d_cums, d_dt, d_acs)


def _gdn_out_fn(o, z, w):
    return _rms(o, w) * _silu(z)


def _gdn_out_fwd(o, proj, gn):
    t = o.shape[0]
    tm = _row_tile(t)

    def body(o_ref, z_ref, w_ref, y_ref):
        for h in range(GDN_HEADS):
            sl = slice(h * GDN_DK, (h + 1) * GDN_DK)
            y_ref[:, sl] = _gdn_out_fn(o_ref[:, sl], z_ref[:, sl].astype(F32), w_ref[...]).astype(BF16)

    return _pcall("gdn_out_fwd", body, (t // tm,), [_row_spec(tm, GDN_V), _row_spec(tm, GDN_V, 3), _full_spec((1, GDN_DK))],
                  _row_spec(tm, GDN_V), _sds((t, GDN_V + SSD_D), BF16), sem=("parallel",))(o, proj, gn)


def _gdn_out_bwd(o, proj, gn, d_ocat):
    t = o.shape[0]
    tm = _row_tile(t)

    def body(o_ref, z_ref, w_ref, dy_ref, do_ref, dz_ref, dw_ref):
        _zero_at_first([dw_ref], pl.program_id(0) == 0)
        for h in range(GDN_HEADS):
            sl = slice(h * GDN_DK, (h + 1) * GDN_DK)
            _, vjp = jax.vjp(_gdn_out_fn, o_ref[:, sl], z_ref[:, sl].astype(F32), w_ref[...])
            d_o, d_z, d_w = vjp(dy_ref[:, sl])
            do_ref[:, sl] = d_o
            dz_ref[:, sl] = d_z.astype(BF16)
            dw_ref[...] += d_w

    return _pcall("gdn_out_bwd", body, (t // tm,),
                  [_row_spec(tm, GDN_V), _row_spec(tm, GDN_V, 3), _full_spec((1, GDN_DK)), _row_spec(tm, GDN_V, 0)],
                  [_row_spec(tm, GDN_V), _row_spec(tm, GDN_V, 3), _full_spec((1, GDN_DK))],
                  [_sds((t, GDN_V)), _sds((t, BIG), BF16), _sds((1, GDN_DK))])(o, proj, gn, d_ocat)


def _ssd_out_fn(y, xs, z, d_skip, w):
    return _rms((y + d_skip * xs) * _silu(z), w)


_SSD_GW = SSD_D // SSD_GROUPS


def _ssd_out_fwd(y, xbc, proj, d_skip, nw, ocat):
    t = y.shape[0]
    tm = _row_tile(t)

    def body(y_ref, x_ref, z_ref, d_ref, w_ref, _, o_ref):
        for gi in range(SSD_GROUPS):
            sl = slice(gi * _SSD_GW, (gi + 1) * _SSD_GW)
            o_ref[:, sl] = _ssd_out_fn(y_ref[:, sl], x_ref[:, sl], z_ref[:, sl].astype(F32), d_ref[:, sl], w_ref[:, sl]).astype(BF16)

    pspec = _full_spec((1, SSD_D))
    return _pcall("ssd_out_fwd", body, (t // tm,),
                  [_row_spec(tm, SSD_D), _row_spec(tm, SSD_D, 0), _row_spec(tm, SSD_D, 4), pspec, pspec, _ANY],
                  _row_spec(tm, SSD_D, 1), _sds(ocat.shape, BF16), sem=("parallel",), aliases={5: 0})(
                      y, xbc, proj, d_skip, nw, ocat)


def _ssd_out_bwd(y, xbc, proj, d_skip, nw, d_ocat, d_proj):
    t = y.shape[0]
    tm = _row_tile(t)

    def body(y_ref, x_ref, z_ref, d_ref, w_ref, do_ref, _, dy_ref, dx_ref, dz_ref, dd_ref, dw_ref):
        _zero_at_first([dd_ref, dw_ref], pl.program_id(0) == 0)
        for gi in range(SSD_GROUPS):
            sl = slice(gi * _SSD_GW, (gi + 1) * _SSD_GW)
            _, vjp = jax.vjp(_ssd_out_fn, y_ref[:, sl], x_ref[:, sl], z_ref[:, sl].astype(F32), d_ref[:, sl], w_ref[:, sl])
            d_y, d_x, d_z, d_d, d_w = vjp(do_ref[:, sl])
            dy_ref[:, sl] = d_y
            dx_ref[:, sl] = d_x
            dz_ref[:, sl] = d_z.astype(BF16)
            dd_ref[:, sl] += d_d
            dw_ref[:, sl] += d_w

    pspec = _full_spec((1, SSD_D))
    row = _row_spec(tm, SSD_D)
    return _pcall("ssd_out_bwd", body, (t // tm,),
                  [row, _row_spec(tm, SSD_D, 0), _row_spec(tm, SSD_D, 4), pspec, pspec, _row_spec(tm, SSD_D, 1), _ANY],
                  [row, row, _row_spec(tm, SSD_D, 4), pspec, pspec],
                  [_sds((t, SSD_D)), _sds((t, SSD_D)), _sds((t, BIG), BF16), _sds((1, SSD_D)), _sds((1, SSD_D))],
                  aliases={6: 2})(y, xbc, proj, d_skip, nw, d_ocat, d_proj)


def _res1_fn(x, mix, g_pm, g_pf):
    x1 = x + _rms(mix, g_pm)
    return x1, _rms(x1, g_pf)


def _res1_fwd(x, mix, g_pm, g_pf):
    t = x.shape[0]
    tm = _row_tile(t)

    def body(x_ref, m_ref, a_ref, b_ref, x1_ref, h2_ref):
        x1, h2 = _res1_fn(x_ref[...], m_ref[...], a_ref[...], b_ref[...])
        x1_ref[...] = x1
        h2_ref[...] = h2.astype(BF16)

    row, pspec = _row_spec(tm, D_MODEL), _full_spec((1, D_MODEL))
    return _pcall("res1_fwd", body, (t // tm,), [row, row, pspec, pspec], [row, row],
                  [_sds((t, D_MODEL)), _sds((t, D_MODEL), BF16)], sem=("parallel",))(x, mix, g_pm, g_pf)


def _res1_bwd(x, mix, g_pm, g_pf, d_x1, d_h2, pair_send):
    t = x.shape[0]
    tm = _row_tile(t)
    ns = len(pair_send)
    steps = t // tm

    def body(x_ref, m_ref, a_ref, b_ref, c1_ref, c2_ref, *rest):
        ps_in, rest = rest[:ns], rest[ns:]
        dx_ref, dm_ref, da_ref, db_ref = rest[:4]
        ps_out, sems = rest[4:4 + ns], rest[4 + ns:]
        ps_start, ps_finish = _pair_send_ops(ps_in, ps_out, sems)
        pl.when(pl.program_id(0) == 0)(ps_start)
        _zero_at_first([da_ref, db_ref], pl.program_id(0) == 0)
        _, vjp = jax.vjp(_res1_fn, x_ref[...], m_ref[...], a_ref[...], b_ref[...])
        d_x, d_m, d_a, d_b = vjp((c1_ref[...], c2_ref[...]))
        dx_ref[...] = d_x
        dm_ref[...] = d_m.astype(BF16)
        da_ref[...] += d_a
        db_ref[...] += d_b
        pl.when(pl.program_id(0) == steps - 1)(ps_finish)

    row, pspec = _row_spec(tm, D_MODEL), _full_spec((1, D_MODEL))
    outs = pl.pallas_call(
        body, name="res1_bwd", grid=(steps,),
        in_specs=[row, row, pspec, pspec, row, row] + [_ANY] * ns, out_specs=[row, row, pspec, pspec] + [_ANY] * ns,
        out_shape=[_sds((t, D_MODEL)), _sds((t, D_MODEL), BF16), _sds((1, D_MODEL)), _sds((1, D_MODEL))]
        + _pair_send_shapes(pair_send),
        scratch_shapes=[pltpu.SemaphoreType.DMA((ns,))] * 2,
        compiler_params=pltpu.CompilerParams(dimension_semantics=("arbitrary",), vmem_limit_bytes=VMEM_LIMIT,
                                             has_side_effects=True))(x, mix, g_pm, g_pf, d_x1, d_h2, *pair_send)
    return outs[:4], outs[4:]


def _final_fn(x1, f, g_po, tgt):
    err = x1 + _rms(f, g_po) - tgt
    return 0.5 * jnp.sum(jnp.mean(err * err, axis=-1))


def _final(x1, f, g_po, tgt):
    t = x1.shape[0]
    tm = _row_tile(t)

    def body(x_ref, f_ref, g_ref, t_ref, loss_ref, dx_ref, df_ref, dg_ref):
        _zero_at_first([loss_ref, dg_ref], pl.program_id(0) == 0)
        loss, (d_x, d_f, d_g) = jax.value_and_grad(_final_fn, argnums=(0, 1, 2))(x_ref[...], f_ref[...], g_ref[...], t_ref[...])
        loss_ref[...] += jnp.broadcast_to(loss, loss_ref.shape)
        dx_ref[...] = d_x
        df_ref[...] = d_f.astype(BF16)
        dg_ref[...] += d_g

    row, pspec = _row_spec(tm, D_MODEL), _full_spec((1, D_MODEL))
    return _pcall("final", body, (t // tm,), [row, row, pspec, row], [_full_spec((8, LANE)), row, row, pspec],
                  [_sds((8, LANE)), _sds((t, D_MODEL)), _sds((t, D_MODEL), BF16), _sds((1, D_MODEL))])(x1, f, g_po, tgt)


def _rms1_bwd(x, g, d_h_a, d_h_b, d_x1):
    t = x.shape[0]
    tm = _row_tile(t)

    def body(x_ref, g_ref, dha_ref, dhb_ref, dx1_ref, dx_ref, dg_ref):
        _zero_at_first([dg_ref], pl.program_id(0) == 0)
        _, vjp = jax.vjp(_rms, x_ref[...], g_ref[...])
        d_x, d_g = vjp(dha_ref[...] + dhb_ref[...])
        dx_ref[...] = d_x + dx1_ref[...]
        dg_ref[...] += d_g

    row, pspec = _row_spec(tm, D_MODEL), _full_spec((1, D_MODEL))
    return _pcall("rms1_bwd", body, (t // tm,), [row, pspec, row, row, row], [row, pspec],
                  [_sds((t, D_MODEL)), _sds((1, D_MODEL))])(x, g, d_h_a, d_h_b, d_x1)


def _qkv_fn(mode):
    def fn(x, *wrows):
        y = _silu(_conv(x, wrows))
        if mode == "q":
            return _l2n(y) * (GDN_DK ** -0.5)
        if mode == "k":
            return _l2n(y)
        return y
    return fn


def _seq_spec(s, tc, off):
    return pl.BlockSpec((s, tc), lambda j, b, _o=off: (b, _o + j))


def _par_spec(rows, tc, off):
    return pl.BlockSpec((rows, tc), lambda j, b, _o=off: (0, _o + j))


def _gdn_conv_fwd(mode, proj, w, bsz, s):
    off = {"q": 0, "k": GDN_HEADS, "v": 2 * GDN_HEADS}[mode]
    fn = _qkv_fn(mode)

    def body(x_ref, w_ref, y_ref):
        y_ref[...] = fn(x_ref[...].astype(F32), *[w_ref[k:k + 1, :] for k in range(CONV_K)])

    return _pcall("gdn_conv_fwd_" + mode, body, (GDN_HEADS, bsz),
                  [_seq_spec(s, GDN_DK, off), _par_spec(CONV_K, GDN_DK, off)], _seq_spec(s, GDN_DK, 0),
                  _sds((bsz * s, GDN_QK)), sem=("parallel", "parallel"))(proj, w)


def _gdn_conv_bwd(mode, proj, w, d_y, d_proj, bsz, s):
    off = {"q": 0, "k": GDN_HEADS, "v": 2 * GDN_HEADS}[mode]
    fn = _qkv_fn(mode)

    def body(x_ref, w_ref, dy_ref, _, dx_ref, dw_ref):
        _zero_at_first([dw_ref], pl.program_id(1) == 0)
        _, vjp = jax.vjp(fn, x_ref[...].astype(F32), *[w_ref[k:k + 1, :] for k in range(CONV_K)])
        grads = vjp(dy_ref[...])
        dx_ref[...] = grads[0].astype(BF16)
        for k in range(CONV_K):
            dw_ref[k:k + 1, :] += grads[1 + k]

    return _pcall("gdn_conv_bwd_" + mode, body, (GDN_HEADS, bsz),
                  [_seq_spec(s, GDN_DK, off), _par_spec(CONV_K, GDN_DK, off), _seq_spec(s, GDN_DK, 0), _ANY],
                  [_seq_spec(s, GDN_DK, off), _par_spec(CONV_K, GDN_DK, 0)],
                  [_sds(d_proj.shape, BF16), _sds((CONV_K, GDN_QK))], sem=("parallel", "arbitrary"), aliases={3: 0})(
                      proj, w, d_y, d_proj)


def _ssd_conv_fn(x, bias, *wrows):
    return _silu(_conv(x, wrows) + bias)


_XBC_OFF = (5 * 1024) // LANE


def _ssd_conv_fwd(proj, w, bias, bsz, s):
    nt_ = SSD_CONV_CH // LANE

    def body(x_ref, w_ref, b_ref, y_ref):
        y_ref[...] = _ssd_conv_fn(x_ref[...].astype(F32), b_ref[...], *[w_ref[k:k + 1, :] for k in range(CONV_K)])

    return _pcall("ssd_conv_fwd", body, (nt_, bsz),
                  [_seq_spec(s, LANE, _XBC_OFF), _par_spec(CONV_K, LANE, 0), _par_spec(1, LANE, 0)], _seq_spec(s, LANE, 0),
                  _sds((bsz * s, SSD_CONV_CH)), sem=("parallel", "parallel"))(proj, w, bias)


def _ssd_conv_bwd(proj, w, bias, d_y, d_proj, bsz, s):
    nt_ = SSD_CONV_CH // LANE

    def body(x_ref, w_ref, b_ref, dy_ref, _, dx_ref, dw_ref, db_ref):
        _zero_at_first([dw_ref, db_ref], pl.program_id(1) == 0)
        _, vjp = jax.vjp(_ssd_conv_fn, x_ref[...].astype(F32), b_ref[...], *[w_ref[k:k + 1, :] for k in range(CONV_K)])
        grads = vjp(dy_ref[...])
        dx_ref[...] = grads[0].astype(BF16)
        db_ref[...] += grads[1]
        for k in range(CONV_K):
            dw_ref[k:k + 1, :] += grads[2 + k]

    return _pcall("ssd_conv_bwd", body, (nt_, bsz),
                  [_seq_spec(s, LANE, _XBC_OFF), _par_spec(CONV_K, LANE, 0), _par_spec(1, LANE, 0), _seq_spec(s, LANE, 0), _ANY],
                  [_seq_spec(s, LANE, _XBC_OFF), _par_spec(CONV_K, LANE, 0), _par_spec(1, LANE, 0)],
                  [_sds(d_proj.shape, BF16), _sds((CONV_K, SSD_CONV_CH)), _sds((1, SSD_CONV_CH))],
                  sem=("parallel", "arbitrary"), aliases={4: 0})(proj, w, bias, d_y, d_proj)


_FFN_TC = 256
_FFN_NT = D_FF // _FFN_TC


def _ffn_act_fn(xg, xu, bg, bu, *wrows):
    k_w = FFN_CONV_K
    gate = _conv(xg, wrows[:k_w]) + bg
    up = _conv(xu, wrows[k_w:]) + bu
    return _silu(gate) * up


def _ffn_act_fwd(u_pre, w, bias, bsz, s):
    def body(xg_ref, xu_ref, wg_ref, wu_ref, bg_ref, bu_ref, a_ref):
        rows = [wg_ref[k:k + 1, :] for k in range(FFN_CONV_K)] + [wu_ref[k:k + 1, :] for k in range(FFN_CONV_K)]
        a_ref[...] = _ffn_act_fn(xg_ref[...].astype(F32), xu_ref[...].astype(F32), bg_ref[...], bu_ref[...], *rows).astype(BF16)

    return _pcall("ffn_act_fwd", body, (_FFN_NT, bsz),
                  [_seq_spec(s, _FFN_TC, 0), _seq_spec(s, _FFN_TC, _FFN_NT),
                   _par_spec(FFN_CONV_K, _FFN_TC, 0), _par_spec(FFN_CONV_K, _FFN_TC, _FFN_NT),
                   _par_spec(1, _FFN_TC, 0), _par_spec(1, _FFN_TC, _FFN_NT)],
                  _seq_spec(s, _FFN_TC, 0), _sds((bsz * s, D_FF), BF16), sem=("parallel", "parallel"))(
                      u_pre, u_pre, w, w, bias, bias)


def _ffn_act_bwd(u_pre, w, bias, d_a, bsz, s):
    def body(xg_ref, xu_ref, wg_ref, wu_ref, bg_ref, bu_ref, da_ref, dg_ref, du_ref, dwg_ref, dwu_ref, dbg_ref, dbu_ref):
        _zero_at_first([dwg_ref, dwu_ref, dbg_ref, dbu_ref], pl.program_id(1) == 0)
        rows = [wg_ref[k:k + 1, :] for k in range(FFN_CONV_K)] + [wu_ref[k:k + 1, :] for k in range(FFN_CONV_K)]
        _, vjp = jax.vjp(_ffn_act_fn, xg_ref[...].astype(F32), xu_ref[...].astype(F32), bg_ref[...], bu_ref[...], *rows)
        grads = vjp(da_ref[...])
        dg_ref[...] = grads[0].astype(BF16)
        du_ref[...] = grads[1].astype(BF16)
        dbg_ref[...] += grads[2]
        dbu_ref[...] += grads[3]
        for k in range(FFN_CONV_K):
            dwg_ref[k:k + 1, :] += grads[4 + k]
            dwu_ref[k:k + 1, :] += grads[4 + FFN_CONV_K + k]

    seq0, par3, par1 = _seq_spec(s, _FFN_TC, 0), _par_spec(FFN_CONV_K, _FFN_TC, 0), _par_spec(1, _FFN_TC, 0)
    return _pcall("ffn_act_bwd", body, (_FFN_NT, bsz),
                  [seq0, _seq_spec(s, _FFN_TC, _FFN_NT), par3, _par_spec(FFN_CONV_K, _FFN_TC, _FFN_NT),
                   par1, _par_spec(1, _FFN_TC, _FFN_NT), seq0],
                  [seq0, seq0, par3, par3, par1, par1],
                  [_sds((bsz * s, D_FF), BF16), _sds((bsz * s, D_FF), BF16), _sds((FFN_CONV_K, D_FF)), _sds((FFN_CONV_K, D_FF)),
                   _sds((1, D_FF)), _sds((1, D_FF))], sem=("parallel", "arbitrary"))(u_pre, u_pre, w, w, bias, bias, d_a)


_GP = GDN_HEADS // 2
_SP = SSD_HEADS // 2


def _pair_lanes(p):
    return slice(2 * p * LANE, (2 * p + 1) * LANE), slice((2 * p + 1) * LANE, (2 * p + 2) * LANE)


_LAST = slice(CHUNK - 1, CHUNK)


def _gdn_args(p, q_ref, k_ref, v_ref, g_ref, b_ref, gr_ref):
    la, lb = _pair_lanes(p)
    return (q_ref[:, la], q_ref[:, lb], k_ref[:, la], k_ref[:, lb], v_ref[:, la], v_ref[:, lb], g_ref[:, la], g_ref[:, lb],
            b_ref[:, la], b_ref[:, lb], gr_ref[p], g_ref[_LAST, la], g_ref[_LAST, lb])


def _gdn_fwd(q, k, v, act, cum, gc_row, bsz, n):
    def body(q_ref, k_ref, v_ref, act_ref, cum_ref, gr_ref, o_ref, sin_ref, s_scr, g_ref, b_ref):
        _zero_at_first([s_scr], pl.program_id(1) == 0)
        _expand_lanes(act_ref, b_ref, 0, GDN_HEADS, GDN_DK)
        _expand_lanes(cum_ref, g_ref, _G_LO, GDN_HEADS, GDN_DK)
        flat = []
        for p in range(_GP):
            flat += [*_gdn_args(p, q_ref, k_ref, v_ref, g_ref, b_ref, gr_ref), s_scr[2 * p], s_scr[2 * p + 1]]
        sin_ref[...] = s_scr[...]
        outs = _gdn_multi(*flat)
        for p in range(_GP):
            la, lb = _pair_lanes(p)
            o_ref[:, la], o_ref[:, lb], s_scr[2 * p], s_scr[2 * p + 1] = outs[4 * p:4 * p + 4]

    tspec = pl.BlockSpec((CHUNK, GDN_V), lambda b, c: (b * n + c, 0))
    rspec = pl.BlockSpec((_GP, 1, LANE), lambda b, c: (b * n + c, 0, 0))
    sspec = pl.BlockSpec((GDN_HEADS, LANE, LANE), lambda b, c: (b * n + c, 0, 0))
    nspec = pl.BlockSpec((CHUNK, SMALL), lambda b, c: (b * n + c, 0))
    wide = pltpu.VMEM((CHUNK, GDN_V), F32)
    return _pcall("gdn_fwd", body, (bsz, n), [tspec] * 3 + [nspec, nspec, rspec], [tspec, sspec],
                  [_sds((bsz * n * CHUNK, GDN_V)), _sds((bsz * n * GDN_HEADS, LANE, LANE))],
                  scratch=[pltpu.VMEM((GDN_HEADS, LANE, LANE), F32), wide, wide], sem=("parallel", "arbitrary"))(
                      q, k, v, act, cum, gc_row)


def _gdn_bwd(q, k, v, act, cum, gc_row, s_in, d_o, bsz, n, scatter):
    ns = len(scatter)

    def body(q_ref, k_ref, v_ref, act_ref, cum_ref, gr_ref, sin_ref, do_ref, *rest):
        sc_in, rest = rest[:ns], rest[ns:]
        dq_ref, dk_ref, dv_ref, dact_ref, dcum_ref, dgr_ref = rest[:6]
        sc_out, rest = rest[6:6 + ns], rest[6 + ns:]
        ds_scr, g_ref, b_ref, dg_ref, db_ref, send_sems, recv_sems = rest
        step = pl.program_id(0) * n + pl.program_id(1)
        sc_start, sc_finish = _scatter_ops(sc_in, sc_out, (send_sems, recv_sems))
        pl.when(step == 0)(sc_start)
        _zero_at_first([ds_scr], pl.program_id(1) == 0)
        _expand_lanes(act_ref, b_ref, 0, GDN_HEADS, GDN_DK)
        _expand_lanes(cum_ref, g_ref, _G_LO, GDN_HEADS, GDN_DK)
        flat, cots = [], []
        for p in range(_GP):
            la, lb = _pair_lanes(p)
            flat += [*_gdn_args(p, q_ref, k_ref, v_ref, g_ref, b_ref, gr_ref), sin_ref[2 * p], sin_ref[2 * p + 1]]
            cots += [do_ref[:, la], do_ref[:, lb], ds_scr[2 * p], ds_scr[2 * p + 1]]
        _, vjp = jax.vjp(_gdn_multi, *flat)
        grads = vjp(tuple(cots))
        for p in range(_GP):
            la, lb = _pair_lanes(p)
            cts = grads[_GDN_NARGS * p:_GDN_NARGS * (p + 1)]
            for ref, i in ((dq_ref, 0), (dk_ref, 2), (dv_ref, 4), (dg_ref, 6), (db_ref, 8)):
                ref[:, la] = cts[i]
                ref[:, lb] = cts[i + 1]
            dgr_ref[p] = cts[10]
            dg_ref[_LAST, la] += cts[11]
            dg_ref[_LAST, lb] += cts[12]
            ds_scr[2 * p] = cts[13]
            ds_scr[2 * p + 1] = cts[14]
        dact_ref[...] = _reduce_lanes(db_ref, 0, GDN_HEADS, GDN_DK)
        dcum_ref[...] = _reduce_lanes(dg_ref, _G_LO, GDN_HEADS, GDN_DK)
        pl.when(step == bsz * n - 1)(sc_finish)

    tspec = pl.BlockSpec((CHUNK, GDN_V), lambda b, c: (b * n + (n - 1 - c), 0))
    nspec = pl.BlockSpec((CHUNK, SMALL), lambda b, c: (b * n + (n - 1 - c), 0))
    rspec = pl.BlockSpec((_GP, 1, LANE), lambda b, c: (b * n + (n - 1 - c), 0, 0))
    sspec = pl.BlockSpec((GDN_HEADS, LANE, LANE), lambda b, c: (b * n + (n - 1 - c), 0, 0))
    tok_shape, nar_shape = _sds((bsz * n * CHUNK, GDN_V)), _sds((bsz * n * CHUNK, SMALL))
    wide = pltpu.VMEM((CHUNK, GDN_V), F32)
    outs = pl.pallas_call(
        body, name="gdn_bwd", grid=(bsz, n),
        in_specs=[tspec] * 3 + [nspec, nspec, rspec, sspec, tspec] + [_ANY] * ns,
        out_specs=[tspec] * 3 + [nspec, nspec, rspec] + [_ANY] * ns,
        out_shape=[tok_shape] * 3 + [nar_shape, nar_shape, _sds((bsz * n * _GP, 1, LANE))]
        + [_sds((3,) + a.shape[1:], a.dtype) for a in scatter],
        scratch_shapes=[pltpu.VMEM((GDN_HEADS, LANE, LANE), F32), wide, wide, wide, wide] + _scatter_sems(ns),
        compiler_params=pltpu.CompilerParams(dimension_semantics=("arbitrary", "arbitrary"), vmem_limit_bytes=VMEM_LIMIT,
                                             has_side_effects=True))(q, k, v, act, cum, gc_row, s_in, d_o, *scatter)
    return outs[:6], outs[6:]


_B_OFF = SSD_D // LANE
_C_OFF = (SSD_D + SSD_BC) // LANE
_PPG = _SP // SSD_GROUPS


def _ssd_args(p, x_ref, dt_ref, a_ref, ar_ref):
    lp = slice(p * LANE, (p + 1) * LANE)
    gi = p // _PPG
    b_sl = slice((_B_OFF + gi) * LANE, (_B_OFF + gi + 1) * LANE)
    c_sl = slice((_C_OFF + gi) * LANE, (_C_OFF + gi + 1) * LANE)
    return (x_ref[:, lp], dt_ref[:, lp], a_ref[:, lp], ar_ref[p], a_ref[_LAST, lp], x_ref[:, b_sl], x_ref[:, c_sl])


def _ssd_fwd(xbc, dt, acs, acs_row, bsz, n):
    cps = 2 if n % 2 == 0 else 1
    m = n // cps

    def body(x_ref, dt_ref, a_ref, ar_ref, y_ref, sin_ref, s_scr):
        _zero_at_first([s_scr], pl.program_id(1) == 0)
        for j in range(cps):
            rows = pl.ds(j * CHUNK, CHUNK)
            xj, dj, aj, yj = x_ref.at[rows], dt_ref.at[rows], a_ref.at[rows], y_ref.at[rows]
            arj = ar_ref.at[pl.ds(j * _SP, _SP)]
            flat = []
            for p in range(_SP):
                flat += [*_ssd_args(p, xj, dj, aj, arj), s_scr[p]]
            sin_ref[j * _SP:(j + 1) * _SP] = s_scr[...]
            outs = _ssd_multi(*flat)
            for p in range(_SP):
                yj[:, p * LANE:(p + 1) * LANE], s_scr[p] = outs[2 * p:2 * p + 2]

    tspec = pl.BlockSpec((cps * CHUNK, SSD_D), lambda b, c: (b * m + c, 0))
    return _pcall("ssd_fwd", body, (bsz, m),
                  [pl.BlockSpec((cps * CHUNK, SSD_CONV_CH), lambda b, c: (b * m + c, 0)), tspec, tspec,
                   pl.BlockSpec((cps * _SP, 1, LANE), lambda b, c: (b * m + c, 0, 0))],
                  [tspec, pl.BlockSpec((cps * _SP, LANE, LANE), lambda b, c: (b * m + c, 0, 0))],
                  [_sds((bsz * n * CHUNK, SSD_D)), _sds((bsz * n * _SP, LANE, LANE))],
                  scratch=[pltpu.VMEM((_SP, LANE, LANE), F32)], sem=("parallel", "arbitrary"))(xbc, dt, acs, acs_row)


def _ssd_bwd(xbc, dt, acs, acs_row, s_in, d_y, d_x_skip, bsz, n):
    def body(x_ref, dt_ref, a_ref, ar_ref, sin_ref, dy_ref, dsk_ref, dx_ref, ddt_ref, da_ref, dar_ref, ds_scr):
        _zero_at_first([ds_scr], pl.program_id(1) == 0)
        d_b = [None] * SSD_GROUPS
        d_c = [None] * SSD_GROUPS
        flat, cots = [], []
        for p in range(_SP):
            flat += [*_ssd_args(p, x_ref, dt_ref, a_ref, ar_ref), sin_ref[p]]
            cots += [dy_ref[:, p * LANE:(p + 1) * LANE], ds_scr[p]]
        _, vjp = jax.vjp(_ssd_multi, *flat)
        grads = vjp(tuple(cots))
        for p in range(_SP):
            lp = slice(p * LANE, (p + 1) * LANE)
            gi = p // _PPG
            cts = grads[_SSD_NARGS * p:_SSD_NARGS * (p + 1)]
            dx_ref[:, lp] = cts[0] + dsk_ref[:, lp]
            ddt_ref[:, lp] = cts[1]
            da_ref[:, lp] = cts[2]
            dar_ref[p] = cts[3]
            da_ref[_LAST, lp] += cts[4]
            d_b[gi] = cts[5] if d_b[gi] is None else d_b[gi] + cts[5]
            d_c[gi] = cts[6] if d_c[gi] is None else d_c[gi] + cts[6]
            ds_scr[p] = cts[7]
        for gi in range(SSD_GROUPS):
            dx_ref[:, (_B_OFF + gi) * LANE:(_B_OFF + gi + 1) * LANE] = d_b[gi]
            dx_ref[:, (_C_OFF + gi) * LANE:(_C_OFF + gi + 1) * LANE] = d_c[gi]

    def rev(b, c):
        return b * n + (n - 1 - c)

    tspec = pl.BlockSpec((CHUNK, SSD_D), lambda b, c: (rev(b, c), 0))
    xspec = pl.BlockSpec((CHUNK, SSD_CONV_CH), lambda b, c: (rev(b, c), 0))
    rspec = pl.BlockSpec((_SP, 1, LANE), lambda b, c: (rev(b, c), 0, 0))
    tok_shape = _sds((bsz * n * CHUNK, SSD_D))
    return _pcall("ssd_bwd", body, (bsz, n),
                  [xspec, tspec, tspec, rspec, pl.BlockSpec((_SP, LANE, LANE), lambda b, c: (rev(b, c), 0, 0)), tspec, tspec],
                  [xspec, tspec, tspec, rspec],
                  [_sds((bsz * n * CHUNK, SSD_CONV_CH)), tok_shape, tok_shape, _sds((bsz * n * _SP, 1, LANE))],
                  scratch=[pltpu.VMEM((_SP, LANE, LANE), F32)], sem=("parallel", "arbitrary"))(
                      xbc, dt, acs, acs_row, s_in, d_y, d_x_skip)


def _rep(p, width):
    return jnp.repeat(p.reshape(-1), width).reshape(1, -1)


def _to_rows(narrow, lo, hi, bsz, n):
    heads = hi - lo
    a = narrow[:, lo:hi].reshape(bsz, n, CHUNK, heads)
    return jnp.transpose(a, (0, 1, 3, 2)).reshape(bsz * n * (heads // 2), 1, 2 * CHUNK)


def _from_rows(rows, heads, bsz, n):
    return jnp.transpose(rows.reshape(bsz, n, heads, CHUNK), (0, 1, 3, 2)).reshape(bsz * n * CHUNK, heads)


def _narrow_row(gdn_part, ssd_part):
    return jnp.pad(jnp.concatenate([gdn_part, ssd_part], axis=1), ((0, 0), (_G_LO, SMALL - _S_HI)))


def _local_step(x, tgt, p):
    bsz, s, _ = x.shape
    t = bsz * s
    n = s // CHUNK
    x2 = x.reshape(t, D_MODEL)
    tgt2 = tgt.reshape(t, D_MODEL)
    gate_bias = _narrow_row(p["gdn_dt_bias"], p["ssd_dt_bias"])
    gate_a_log = _narrow_row(p["gdn_a_log"], p["ssd_a_log"])
    d_skip = _rep(p["ssd_d"], SSD_HEADDIM)

    h, (g_in, g_gcw, g_scw, g_fcw) = _rms_fwd_gathering(
        "rms0_fwd", x2, p["pre_mix_norm"], [p["w_in"]] + [p[k] for k in _COL_SHARDED_SMALL], [True, False, False, False])
    p = dict(p, gdn_conv_w=_chips_to_cols(g_gcw), ssd_conv_w=_chips_to_cols(g_scw), ffn_conv_w=_chips_to_cols(g_fcw))
    a0, a1, b0 = _NARROW_A - 2 * _SHARD_W, _NARROW_A + 16 - 2 * _SHARD_W, _NARROW_B - 3 * _SHARD_W
    w_big = jnp.concatenate([g_in[0], g_in[1], g_in[2][:, :a0], g_in[2][:, a1:], g_in[3][:, :b0]], axis=1)
    w_small = jnp.concatenate([g_in[2][:, a0:a1], g_in[3][:, b0:], jnp.zeros((D_MODEL, SMALL - 32), BF16)], axis=1)
    proj, g_out, g_up, g_down = _matmul_nn_gathering(
        "mm_in_big", h, w_big, BF16, (_pick(t, (1024, 512, 256, 128, 64)), BIG // 2), [p["w_out"], p["w_up"], p["w_down"]])
    w_out, w_up, w_down = g_out.reshape(-1, D_MODEL), _chips_to_cols(g_up), g_down.reshape(-1, D_MODEL)
    small = _matmul("mm_in_small", h, w_small, "nn", F32, (1024, 128))
    gact, cum, dt, acs = _gates_fwd(small, gate_bias, gate_a_log)
    gc_row = _to_rows(cum, _G_LO, _G_HI, bsz, n)
    acs_row = _to_rows(cum, _S_LO, _S_HI, bsz, n)
    q = _gdn_conv_fwd("q", proj, p["gdn_conv_w"], bsz, s)
    k = _gdn_conv_fwd("k", proj, p["gdn_conv_w"], bsz, s)
    v = _gdn_conv_fwd("v", proj, p["gdn_conv_w"], bsz, s)
    o, gdn_s = _gdn_fwd(q, k, v, gact, cum, gc_row, bsz, n)
    ocat = _gdn_out_fwd(o, proj, p["gdn_norm_w"])
    xbc = _ssd_conv_fwd(proj, p["ssd_conv_w"], p["ssd_conv_b"], bsz, s)
    y, ssd_s = _ssd_fwd(xbc, dt, acs, acs_row, bsz, n)
    ocat = _ssd_out_fwd(y, xbc, proj, d_skip, p["ssd_norm_w"], ocat)
    mix = _matmul("mm_out", ocat, w_out, "nn", F32, (1024, 1024))
    x1, h2 = _res1_fwd(x2, mix, p["post_mix_norm"], p["pre_ffn_norm"])
    u_pre = _matmul("mm_up", h2, w_up, "nn", BF16, (1024, 2816))
    act = _ffn_act_fwd(u_pre, p["ffn_conv_w"], p["ffn_conv_b"], bsz, s)
    f = _matmul("mm_down", act, w_down, "nn", F32, (1024, 1024))
    loss_acc, d_out, d_f, g_post_ffn = _final(x1, f, p["post_ffn_norm"], tgt2)

    grads = {"post_ffn_norm": g_post_ffn}
    d_act = _matmul("mm_down_dx", d_f, w_down, "nt", F32, (1024, 2816))
    dw_down = _matmul("mm_down_dw", act, d_f, "tn", F32, (2816, 1024, 1024))
    d_gate, d_up, dwg, dwu, dbg, dbu = _ffn_act_bwd(u_pre, p["ffn_conv_w"], p["ffn_conv_b"], d_act, bsz, s)
    grads["ffn_conv_w"] = jnp.concatenate([dwg, dwu], axis=1)
    grads["ffn_conv_b"] = jnp.concatenate([dbg, dbu], axis=1)
    d_h2 = _matmul_nt_split("mm_up_dx", d_gate, d_up, w_up, 512)
    dw_up = _matmul("mm_up_dw_gate", h2, d_gate, "tn", F32, (1024, 2816, 1024), part=(2 * D_FF, 0, None))
    dw_up = _matmul("mm_up_dw_up", h2, d_up, "tn", F32, (1024, 2816, 1024), part=(2 * D_FF, D_FF, dw_up))
    dw_down = dw_down.reshape(N_CHIPS, -1, D_MODEL)
    (d_x1, d_mix, grads["post_mix_norm"], grads["pre_ffn_norm"]), (r_up, r_down) = _res1_bwd(
        x2, mix, p["post_mix_norm"], p["pre_ffn_norm"], d_out, d_h2, [dw_up, dw_down])
    d_ocat = _matmul("mm_out_dx", d_mix, w_out, "nt", F32, (1024, 2048))
    dw_out = _matmul("mm_out_dw", ocat, d_mix, "tn", F32, (2048, 1024, 1024)).reshape(N_CHIPS, -1, D_MODEL)

    (r_out,) = _pair_send_other_half("pair_reduce_send_early", [dw_out])
    early = _pair_adds("early", [dw_out, dw_up, dw_down], [r_out, r_up, r_down])

    d_o, d_proj, grads["gdn_norm_w"] = _gdn_out_bwd(o, proj, p["gdn_norm_w"], d_ocat)
    (d_q, d_k, d_v, d_act_g, d_cum_g, d_gc_row), early_landed = _gdn_bwd(
        q, k, v, gact, cum, gc_row, gdn_s, d_o, bsz, n, [ps[1] for ps in early])
    d_proj, dwq = _gdn_conv_bwd("q", proj, p["gdn_conv_w"], d_q, d_proj, bsz, s)
    d_proj, dwk = _gdn_conv_bwd("k", proj, p["gdn_conv_w"], d_k, d_proj, bsz, s)
    d_proj, dwv = _gdn_conv_bwd("v", proj, p["gdn_conv_w"], d_v, d_proj, bsz, s)
    grads["gdn_conv_w"] = jnp.concatenate([dwq, dwk, dwv], axis=1)

    d_y, d_xs_skip, d_proj, d_dskip, grads["ssd_norm_w"] = _ssd_out_bwd(y, xbc, proj, d_skip, p["ssd_norm_w"], d_ocat, d_proj)
    d_xbc, d_dt, d_acs, d_acs_row = _ssd_bwd(xbc, dt, acs, acs_row, ssd_s, d_y, d_xs_skip, bsz, n)
    d_proj, grads["ssd_conv_w"], grads["ssd_conv_b"] = _ssd_conv_bwd(proj, p["ssd_conv_w"], p["ssd_conv_b"], d_xbc, d_proj, bsz, s)

    d_cum_rows = jnp.concatenate([jnp.zeros((t, _G_LO), F32), _from_rows(d_gc_row, GDN_HEADS, bsz, n),
                                  _from_rows(d_acs_row, SSD_HEADS, bsz, n), jnp.zeros((t, SMALL - _S_HI), F32)], axis=1)
    d_small, d_gate_bias, d_gate_a_log = _gates_bwd(small, gate_bias, gate_a_log, d_act_g, [d_cum_g, d_cum_rows], d_dt, d_acs)
    grads["gdn_dt_bias"], grads["ssd_dt_bias"] = d_gate_bias[:, _G_LO:_G_HI], d_gate_bias[:, _S_LO:_S_HI]
    grads["gdn_a_log"], grads["ssd_a_log"] = d_gate_a_log[:, _G_LO:_G_HI], d_gate_a_log[:, _S_LO:_S_HI]
    dw_big = _matmul("mm_in_big_dw", h, d_proj, "tn", F32, (1024, 3328, 1024))
    dw_small = _matmul("mm_in_small_dw", h, d_small, "tn", F32)
    dw_in = jnp.stack([
        dw_big[:, :_SHARD_W], dw_big[:, _SHARD_W:2 * _SHARD_W],
        jnp.concatenate([dw_big[:, 2 * _SHARD_W:_NARROW_A], dw_small[:, :16], dw_big[:, _NARROW_A:3 * _SHARD_W - 16]], axis=1),
        jnp.concatenate([dw_big[:, 3 * _SHARD_W - 16:], dw_small[:, 16:32]], axis=1)])
    late = _pair_sums("late", [dw_in])
    d_h_big, late_landed = _matmul_nt_scattering("mm_in_big_dx", d_proj, w_big, 512, [ps[1] for ps in late])
    d_h_small = _matmul("mm_in_small_dx", d_small, w_small, "nt", F32, (1024, 1024))
    grad_x, grads["pre_mix_norm"] = _rms1_bwd(x2, p["pre_mix_norm"], d_h_big, d_h_small, d_x1)
    grads["ssd_d"] = _head_sums(d_dskip)[:1, :SSD_HEADS]
    return loss_acc, grad_x.reshape(bsz, s, D_MODEL), grads, (late + early, list(late_landed) + list(early_landed))


def _head_sums(wide):
    def body(x_ref, o_ref):
        r, c = _iota2((D_MODEL, SMALL))
        o_ref[...] = _mask_dot(jnp.broadcast_to(x_ref[...], (8, D_MODEL)), ((r >> 6) == c).astype(F32), NN, True)

    return _pcall("head_sums", body, (1,), [_full_spec((1, D_MODEL))], _full_spec((8, SMALL)), _sds((8, SMALL)))(wide)


def _adamw_fn(w, g, m, v):
    m = ADAM_B1 * m + (1.0 - ADAM_B1) * g
    v = ADAM_B2 * v + (1.0 - ADAM_B2) * (g * g)
    m_hat = m / (1.0 - ADAM_B1 ** ADAM_STEP)
    v_hat = v / (1.0 - ADAM_B2 ** ADAM_STEP)
    delta = -ADAM_LR * (m_hat / (jnp.sqrt(v_hat) + ADAM_EPS) + ADAM_WD * w)
    return delta, m, v


def _adamw(name, w, g, m, v):
    r, c = w.shape
    tr = _pick(r, (256, 176, 128, 64, 8))

    def body(w_ref, g_ref, m_ref, v_ref, d_ref, m2_ref, v2_ref):
        d, m2, v2 = _adamw_fn(w_ref[...], g_ref[...], m_ref[...], v_ref[...])
        d_ref[...] = d
        m2_ref[...] = m2
        v2_ref[...] = v2

    spec = pl.BlockSpec((tr, c), lambda i: (i, 0))
    return _pcall(name, body, (r // tr,), [spec] * 4, [spec] * 3, [_sds((r, c))] * 3, sem=("parallel",))(w, g, m, v)


def _adamw_many(groups):
    n = len(groups[0])
    shapes = [a.shape for a in groups[0]]

    def body(*refs):
        ins, outs = refs[:4 * n], refs[4 * n:]
        for i in range(n):
            d, m2, v2 = _adamw_fn(ins[i][...], ins[n + i][...], ins[2 * n + i][...], ins[3 * n + i][...])
            outs[i][...] = d
            outs[n + i][...] = m2
            outs[2 * n + i][...] = v2

    specs = [_full_spec(s) for s in shapes]
    flat = _pcall("adamw_small", body, (1,), specs * 4, specs * 3, [_sds(s) for s in shapes] * 3)(
        *[a for g in groups for a in g])
    return [flat[:n], flat[n:2 * n], flat[2 * n:]]


_ANY = pl.BlockSpec(memory_space=pl.ANY)
_OTHER_CHIPS = ((1, 0), (0, 1), (1, 1))


def _coords():
    return lax.axis_index("x"), lax.axis_index("y"), lax.axis_index("c")


def _flip(v, f):
    return 1 - v if f else v


def _gather_ops(ins, outs, sems, split):
    send_sems, recv_sems, fwd_send_sems, fwd_recv_sems, own_send_sems, own_recv_sems = sems
    n = len(ins)
    x, y, c = _coords()
    me = 2 * x + y
    sib = (x, y, 1 - c)

    def rows(a, core):
        if not split[a]:
            return slice(None)
        half = ins[a].shape[0] // 2
        return pl.ds(core * half, half)

    def chip(j):
        fx, fy = _OTHER_CHIPS[j]
        return _flip(x, fx), _flip(y, fy)

    def own_cp(a):
        return pltpu.make_async_remote_copy(ins[a], outs[a].at[me], own_send_sems.at[a], own_recv_sems.at[a],
                                            device_id=sib, device_id_type=MESH)

    def ici_cp(a, j):
        return pltpu.make_async_remote_copy(ins[a].at[rows(a, c)], outs[a].at[me, rows(a, c)],
                                            send_sems.at[a * 3 + j], recv_sems.at[a * 3 + j],
                                            device_id=(*chip(j), c), device_id_type=MESH)

    def landed_cp(a, j, sem_a, sem_b, core, to):
        cx, cy = chip(j)
        blk = outs[a].at[2 * cx + cy, rows(a, core)]
        return pltpu.make_async_remote_copy(blk, blk, sem_a.at[a * 3 + j], sem_b.at[a * 3 + j], device_id=to, device_id_type=MESH)

    pairs = [(a, j) for a in range(n) for j in range(3)]

    def start():
        for a in range(n):
            own_cp(a).start()
        for a, j in pairs:
            ici_cp(a, j).start()

    def forward():
        for a, j in pairs:
            landed_cp(a, j, send_sems, recv_sems, c, (*chip(j), c)).wait_recv()
            if split[a]:
                landed_cp(a, j, fwd_send_sems, fwd_recv_sems, c, sib).start()

    def finish():
        for a, j in pairs:
            if split[a]:
                landed_cp(a, j, fwd_send_sems, fwd_recv_sems, 1 - c, sib).wait_recv()
        for a in range(n):
            own_cp(a).wait_recv()
        for a, j in pairs:
            ici_cp(a, j).wait_send()
            if split[a]:
                landed_cp(a, j, fwd_send_sems, fwd_recv_sems, c, sib).wait_send()
        for a in range(n):
            own_cp(a).wait_send()

    return start, forward, finish


def _gather_sems(n):
    return [pltpu.SemaphoreType.DMA((3 * n,))] * 4 + [pltpu.SemaphoreType.DMA((n,))] * 2


def _matmul_nn_gathering(name, a, b, out_dtype, tiles, shards):
    m, k = a.shape
    n = b.shape[1]
    tm, tn = tiles
    nc = _pick(tn, (512, 256, 128))
    ns = len(shards)
    gi, gj = m // tm, n // tn
    steps = gi * gj

    def body(a_ref, b_ref, *rest):
        ins, o_ref, outs, sems = rest[:ns], rest[ns], rest[ns + 1:2 * ns + 1], rest[2 * ns + 1:]
        step = pl.program_id(0) * gj + pl.program_id(1)
        start, forward, finish = _gather_ops(ins, outs, sems, [True] * ns)
        pl.when(step == 0)(start)
        for c0 in range(0, tn, nc):
            o_ref[:, c0:c0 + nc] = _bdot(a_ref[...], b_ref[:, c0:c0 + nc], NN).astype(o_ref.dtype)
        pl.when(step == steps - 2)(forward)
        pl.when(step == steps - 1)(finish)

    assert steps >= 2
    return pl.pallas_call(
        body, name=name, grid=(gi, gj),
        in_specs=[pl.BlockSpec((tm, k), lambda i, j: (i, 0)), pl.BlockSpec((k, tn), lambda i, j: (0, j))] + [_ANY] * ns,
        out_specs=[pl.BlockSpec((tm, tn), lambda i, j: (i, j))] + [_ANY] * ns,
        out_shape=[_sds((m, n), out_dtype)] + [_sds((N_CHIPS,) + s.shape, s.dtype) for s in shards],
        scratch_shapes=_gather_sems(ns),
        compiler_params=pltpu.CompilerParams(dimension_semantics=("arbitrary", "arbitrary"), vmem_limit_bytes=VMEM_LIMIT,
                                             has_side_effects=True))(a, b, *shards)


_PEERS = tuple((fx, fy, fc) for fx in (0, 1) for fy in (0, 1) for fc in (0, 1))[1:]


def _allreduce_small(x):
    r = x.shape[0]

    def body(x_ref, o_ref, buf, send_sems, recv_sems):
        cx, cy, cc = _coords()
        me = 4 * cx + 2 * cy + cc
        sends = []
        for j, (fx, fy, fc) in enumerate(_PEERS):
            cp = pltpu.make_async_remote_copy(x_ref, buf.at[me], send_sems.at[j], recv_sems.at[j],
                                              device_id=(_flip(cx, fx), _flip(cy, fy), _flip(cc, fc)), device_id_type=MESH)
            cp.start()
            sends.append(cp)
        buf[pl.ds(me, 1)] = x_ref[...][None]
        for j, (fx, fy, fc) in enumerate(_PEERS):
            src = 4 * _flip(cx, fx) + 2 * _flip(cy, fy) + _flip(cc, fc)
            pltpu.make_async_remote_copy(x_ref, buf.at[src], send_sems.at[j], recv_sems.at[j],
                                         device_id=(_flip(cx, fx), _flip(cy, fy), _flip(cc, fc)), device_id_type=MESH).wait_recv()
        for cp in sends:
            cp.wait_send()
        acc = buf[0]
        for d in range(1, N_DEV):
            acc = acc + buf[d]
        o_ref[...] = acc

    vm = pl.BlockSpec(memory_space=pltpu.VMEM)
    return pl.pallas_call(
        body, name="allreduce_small", out_shape=_sds((r, LANE)), in_specs=[vm], out_specs=vm,
        scratch_shapes=[pltpu.VMEM((N_DEV, r, LANE), F32), pltpu.SemaphoreType.DMA((7,)), pltpu.SemaphoreType.DMA((7,))],
        compiler_params=pltpu.CompilerParams(has_side_effects=True, vmem_limit_bytes=VMEM_LIMIT))(x)


def _pair_sums(tag, arrs):
    return _pair_adds(tag, arrs, _pair_send_other_half("pair_reduce_send_" + tag, arrs))


def _pair_adds(tag, arrs, received):
    core = lax.axis_index("c").astype(jnp.int32).reshape(1)
    return [_pair_add("pair_add_%s_%d" % (tag, i), a, b, core) for i, (a, b) in enumerate(zip(arrs, received))]


def _pair_send_ops(ins, outs, sems):
    send_sems, recv_sems = sems
    x, y, c = _coords()

    def cp(a):
        half = ins[a].shape[1] // 2
        return pltpu.make_async_remote_copy(ins[a].at[:, pl.ds((1 - c) * half, half), :], outs[a], send_sems.at[a], recv_sems.at[a],
                                            device_id=(x, y, 1 - c), device_id_type=MESH)

    def start():
        for a in range(len(ins)):
            cp(a).start()

    def finish():
        for a in range(len(ins)):
            cp(a).wait_recv()
        for a in range(len(ins)):
            cp(a).wait_send()

    return start, finish


def _pair_send_shapes(arrs):
    return [_sds((a.shape[0], a.shape[1] // 2, a.shape[2]), a.dtype) for a in arrs]


def _pair_send_other_half(name, arrs):
    n = len(arrs)

    def body(*refs):
        start, finish = _pair_send_ops(refs[:n], refs[n:2 * n], refs[2 * n:])
        start()
        finish()

    return pl.pallas_call(
        body, name=name, out_shape=_pair_send_shapes(arrs), in_specs=[_ANY] * n, out_specs=[_ANY] * n,
        scratch_shapes=[pltpu.SemaphoreType.DMA((n,))] * 2, compiler_params=pltpu.CompilerParams(has_side_effects=True))(*arrs)


def _pair_fill(arrs):
    n = len(arrs)

    def body(*refs):
        bufs = refs[n:2 * n]
        send_sems, recv_sems = refs[2 * n:]
        x, y, c = _coords()
        sends = []
        for a in range(n):
            cp = pltpu.make_async_remote_copy(bufs[a].at[c], bufs[a].at[c], send_sems.at[a], recv_sems.at[a],
                                              device_id=(x, y, 1 - c), device_id_type=MESH)
            cp.start()
            sends.append(cp)
        for a in range(n):
            theirs = bufs[a].at[1 - c]
            pltpu.make_async_remote_copy(theirs, theirs, send_sems.at[a], recv_sems.at[a],
                                         device_id=(x, y, 1 - c), device_id_type=MESH).wait_recv()
        for cp in sends:
            cp.wait_send()

    return pl.pallas_call(
        body, name="pair_gather", out_shape=[_sds(a.shape, a.dtype) for a in arrs], in_specs=[_ANY] * n, out_specs=[_ANY] * n,
        scratch_shapes=[pltpu.SemaphoreType.DMA((n,))] * 2, input_output_aliases={a: a for a in range(n)},
        compiler_params=pltpu.CompilerParams(has_side_effects=True))(*arrs)


def _scatter_ops(ins, outs, sems):
    send_sems, recv_sems = sems
    x, y, c = _coords()

    def cp(a, j):
        fx, fy = _OTHER_CHIPS[j]
        to = 2 * _flip(x, fx) + _flip(y, fy)
        return pltpu.make_async_remote_copy(ins[a].at[to], outs[a].at[j], send_sems.at[a * 3 + j], recv_sems.at[a * 3 + j],
                                            device_id=(_flip(x, fx), _flip(y, fy), c), device_id_type=MESH)

    pairs = [(a, j) for a in range(len(ins)) for j in range(3)]

    def start():
        for a, j in pairs:
            cp(a, j).start()

    def finish():
        for a, j in pairs:
            cp(a, j).wait_recv()
        for a, j in pairs:
            cp(a, j).wait_send()

    return start, finish


def _scatter_sems(n):
    return [pltpu.SemaphoreType.DMA((3 * n,))] * 2


def _pair_add(name, full, recv, core):
    _, r, c = full.shape
    half = r // 2
    tr = _pick(half, (256, 176, 128, 64, 8))
    nb = half // tr

    def body(c_ref, a_ref, b_ref, o_ref, ob_ref):
        s = a_ref[...] + b_ref[...]
        o_ref[...] = s
        ob_ref[...] = s.astype(BF16)

    blk = pl.BlockSpec((1, tr, c), lambda k, i, cref: (k, i, 0))
    grid_spec = pltpu.PrefetchScalarGridSpec(
        num_scalar_prefetch=1, grid=(N_CHIPS, nb),
        in_specs=[pl.BlockSpec((1, tr, c), lambda k, i, cref: (k, cref[0] * nb + i, 0)), blk], out_specs=[blk, blk])
    return pl.pallas_call(
        body, name=name, out_shape=[_sds((N_CHIPS, half, c)), _sds((N_CHIPS, half, c), BF16)], grid_spec=grid_spec,
        compiler_params=pltpu.CompilerParams(dimension_semantics=("parallel", "parallel"), vmem_limit_bytes=VMEM_LIMIT))(
            core, full, recv)


def _chip_sum(name, landed, own, where):
    _, r, c = landed.shape
    tr = _pick(r, (256, 176, 128, 64, 16))

    def body(w_ref, l_ref, o_ref, s_ref):
        s_ref[0] = ((o_ref[0] + l_ref[0].astype(F32)) + l_ref[1].astype(F32)) + l_ref[2].astype(F32)

    grid_spec = pltpu.PrefetchScalarGridSpec(
        num_scalar_prefetch=1, grid=(r // tr,),
        in_specs=[pl.BlockSpec((3, tr, c), lambda i, wref: (0, i, 0)),
                  pl.BlockSpec((1, tr, c), lambda i, wref: (wref[0], i, 0))],
        out_specs=pl.BlockSpec((1, tr, c), lambda i, wref: (wref[1], i, 0)))
    return pl.pallas_call(
        body, name=name, out_shape=_sds((2, r, c)), grid_spec=grid_spec,
        compiler_params=pltpu.CompilerParams(dimension_semantics=("parallel",), vmem_limit_bytes=VMEM_LIMIT))(where, landed, own)


_WEIGHTS = ("pre_mix_norm", "w_in", "gdn_conv_w", "gdn_a_log", "gdn_dt_bias", "gdn_norm_w", "ssd_conv_w", "ssd_conv_b",
            "ssd_a_log", "ssd_dt_bias", "ssd_d", "ssd_norm_w", "w_out", "post_mix_norm", "pre_ffn_norm", "w_up",
            "ffn_conv_w", "ffn_conv_b", "w_down", "post_ffn_norm")
_BIG = ("w_in", "w_out", "w_up", "w_down")
_COL_SHARDED_SMALL = ("gdn_conv_w", "ssd_conv_w", "ffn_conv_w")
_SMALL = tuple(k for k in _WEIGHTS if k not in _BIG)


def _pack(arrs):
    flat = jnp.concatenate([a.reshape(-1) for a in arrs])
    rows = -(-flat.shape[0] // (8 * LANE)) * 8
    return jnp.pad(flat, (0, rows * LANE - flat.shape[0])).reshape(rows, LANE)


def _unpack(packed, shapes):
    flat = packed.reshape(-1)
    out, off = [], 0
    for shp in shapes:
        size = 1
        for d in shp:
            size *= d
        out.append(flat[off:off + size].reshape(shp))
        off += size
    return out


def _chips_to_cols(a):
    k, r, c = a.shape
    return jnp.transpose(a, (1, 0, 2)).reshape(r, k * c)


def kernel(x, pre_mix_norm, w_in, gdn_conv_w, gdn_a_log, gdn_dt_bias, gdn_norm_w, ssd_conv_w, ssd_conv_b, ssd_a_log, ssd_dt_bias, ssd_d, ssd_norm_w, w_out, post_mix_norm, pre_ffn_norm, w_up, ffn_conv_w, ffn_conv_b, w_down, post_ffn_norm, loss_target, m_pre_mix_norm, m_w_in, m_gdn_conv_w, m_gdn_a_log, m_gdn_dt_bias, m_gdn_norm_w, m_ssd_conv_w, m_ssd_conv_b, m_ssd_a_log, m_ssd_dt_bias, m_ssd_d, m_ssd_norm_w, m_w_out, m_post_mix_norm, m_pre_ffn_norm, m_w_up, m_ffn_conv_w, m_ffn_conv_b, m_w_down, m_post_ffn_norm, v_pre_mix_norm, v_w_in, v_gdn_conv_w, v_gdn_a_log, v_gdn_dt_bias, v_gdn_norm_w, v_ssd_conv_w, v_ssd_conv_b, v_ssd_a_log, v_ssd_dt_bias, v_ssd_d, v_ssd_norm_w, v_w_out, v_post_mix_norm, v_pre_ffn_norm, v_w_up, v_ffn_conv_w, v_ffn_conv_b, v_w_down, v_post_ffn_norm):
    w = dict(zip(_WEIGHTS, (pre_mix_norm, w_in, gdn_conv_w, gdn_a_log, gdn_dt_bias, gdn_norm_w, ssd_conv_w, ssd_conv_b,
                            ssd_a_log, ssd_dt_bias, ssd_d, ssd_norm_w, w_out, post_mix_norm, pre_ffn_norm, w_up,
                            ffn_conv_w, ffn_conv_b, w_down, post_ffn_norm)))
    m = dict(zip(_WEIGHTS, (m_pre_mix_norm, m_w_in, m_gdn_conv_w, m_gdn_a_log, m_gdn_dt_bias, m_gdn_norm_w, m_ssd_conv_w,
                            m_ssd_conv_b, m_ssd_a_log, m_ssd_dt_bias, m_ssd_d, m_ssd_norm_w, m_w_out, m_post_mix_norm,
                            m_pre_ffn_norm, m_w_up, m_ffn_conv_w, m_ffn_conv_b, m_w_down, m_post_ffn_norm)))
    v = dict(zip(_WEIGHTS, (v_pre_mix_norm, v_w_in, v_gdn_conv_w, v_gdn_a_log, v_gdn_dt_bias, v_gdn_norm_w, v_ssd_conv_w,
                            v_ssd_conv_b, v_ssd_a_log, v_ssd_dt_bias, v_ssd_d, v_ssd_norm_w, v_w_out, v_post_mix_norm,
                            v_pre_ffn_norm, v_w_up, v_ffn_conv_w, v_ffn_conv_b, v_w_down, v_post_ffn_norm)))
    cx, cy, cc = _coords()
    chip = 2 * cx + cy

    p = {k: w[k] for k in _SMALL if k not in _COL_SHARDED_SMALL}
    for k in _BIG:
        p[k] = w[k][0].astype(BF16)
    for k in _COL_SHARDED_SMALL:
        p[k] = w[k][0]
    loss_acc, grad_x, grads, (pair_sum_list, landed_list) = _local_step(x, loss_target, p)

    small_full_shapes = [grads[k].shape for k in _SMALL]
    summed = _unpack(_allreduce_small(_pack([grads[k] for k in _SMALL] + [loss_acc[:1, :1]])), small_full_shapes + [(1, 1)])
    loss = summed[-1].reshape(())
    g_small = dict(zip(_SMALL, summed[:-1]))
    for k in _COL_SHARDED_SMALL:
        width = w[k].shape[2]
        g_small[k] = lax.dynamic_slice_in_dim(g_small[k], chip * width, width, axis=1)

    pair_sums = dict(zip(_BIG, pair_sum_list))
    landed = dict(zip(_BIG, landed_list))
    where = jnp.stack([chip, cc]).astype(jnp.int32)
    mine = [_chip_sum("chip_sum_" + k, landed[k], pair_sums[k][0], where) for k in _BIG]
    both = _pair_fill(mine)
    g_big = {k: a.reshape(-1, a.shape[2]) for k, a in zip(_BIG, both)}

    out_g, out_d, out_m, out_v = {}, {}, {}, {}
    for k in _BIG:
        out_g[k] = g_big[k][None]
        d_, m_, v_ = _adamw("adamw_" + k, w[k][0], g_big[k], m[k][0], v[k][0])
        out_d[k], out_m[k], out_v[k] = d_[None], m_[None], v_[None]
    for k in _SMALL:
        out_g[k] = g_small[k].reshape(w[k].shape)
    two_d = {k: (w[k].shape[-2], w[k].shape[-1]) for k in _SMALL}
    small_out = _adamw_many([[d[k].reshape(two_d[k]) for k in _SMALL] for d in (w, out_g, m, v)])
    for dst, outs in zip((out_d, out_m, out_v), small_out):
        dst.update({k: o.reshape(w[k].shape) for k, o in zip(_SMALL, outs)})
    return (loss, grad_x, *[out_g[k] for k in _WEIGHTS], *[out_d[k] for k in _WEIGHTS],
            *[out_m[k] for k in _WEIGHTS], *[out_v[k] for k in _WEIGHTS])
```
